```python
import math
import jax, jax.numpy as jnp
from jax import lax
import numpy as np


D_MODEL = 1024
BATCH = 8
SEQ = 16384
DEPTH = 2

N_META = 16
CHUNK = 128
PAD = CHUNK - N_META
EPS = 1e-6

CONV_A_WIDTH = D_MODEL
CONV_A_K = 3
SSD_HEAD_DIM = 64
SSD_HEADS = D_MODEL // SSD_HEAD_DIM
SSD_INNER = SSD_HEADS * SSD_HEAD_DIM
SSD_GROUPS = 4
SSD_STATE = 128
SSD_CONV_K = 4
SSD_CONV_DIM = SSD_INNER + 2 * SSD_GROUPS * SSD_STATE
RET_HEADS = 4
RET_QK_DIM = 256
RET_V_DIM = D_MODEL // RET_HEADS
RET_WIDTH = RET_HEADS * RET_V_DIM
ROPE_BASE = 10000.0
SB_HEADS = 8
SB_HEAD_DIM = D_MODEL // SB_HEADS
SB_WIDTH = SB_HEADS * SB_HEAD_DIM
N_BRANCH = 4
BRANCH_WIDTH = D_MODEL
D_FF = ((8 * D_MODEL // 3 + 255) // 256) * 256

IN_SIZES = (
    CONV_A_WIDTH, CONV_A_WIDTH, CONV_A_WIDTH,
    SSD_INNER, SSD_CONV_DIM, SSD_HEADS,
    RET_HEADS * RET_QK_DIM, RET_HEADS * RET_QK_DIM,
    RET_WIDTH, RET_WIDTH,
    SB_WIDTH, SB_WIDTH, SB_WIDTH,
    N_BRANCH * D_MODEL,
)
IN_WIDTH = sum(IN_SIZES)
IN_SPLITS = tuple(np.cumsum(IN_SIZES)[:-1].tolist())

kernel_name = 'hybrid_gated_conv_ssd_retention_stickbreaking'


def rmsnorm(x, w):
    xf = x.astype(jnp.float32)
    y = xf * lax.rsqrt(jnp.mean(xf * xf, axis=-1, keepdims=True) + EPS)
    return (y * w.astype(jnp.float32)).astype(x.dtype)


def causal_dwconv(u, w):
    k_taps = w.shape[0]
    length = u.shape[1]
    up = jnp.pad(u, ((0, 0), (k_taps - 1, 0), (0, 0)))
    out = up[:, 0:length] * w[0]
    for i in range(1, k_taps):
        out = out + up[:, i:i + length] * w[i]
    return out


def short_conv_mixer(b_gate, c_gate, xa, conv_w, valid):
    vm = valid[None, :, None].astype(xa.dtype)
    u = c_gate * xa * vm
    return (b_gate * causal_dwconv(u, conv_w)).astype(xa.dtype)


def ssd_mixer(z, xbc, dt_raw, conv_w, conv_b, dt_bias, a_log, d_skip, norm_w, valid):
    f32 = jnp.float32
    b, L, _ = z.shape
    nc = L // CHUNK
    hpg = SSD_HEADS // SSD_GROUPS
    vm = valid[None, :, None].astype(xbc.dtype)
    xbc = jax.nn.silu(causal_dwconv(xbc * vm, conv_w) + conv_b)
    xs, bm, cm = jnp.split(xbc, (SSD_INNER, SSD_INNER + SSD_GROUPS * SSD_STATE), axis=-1)
    xs = (xs * vm).astype(f32).reshape(b, nc, CHUNK, SSD_GROUPS, hpg, SSD_HEAD_DIM)
    bc = bm.astype(f32).reshape(b, nc, CHUNK, SSD_GROUPS, SSD_STATE)
    cc = cm.astype(f32).reshape(b, nc, CHUNK, SSD_GROUPS, SSD_STATE)
    dt = jax.nn.softplus(dt_raw.astype(f32) + dt_bias.astype(f32))
    a = (-jnp.exp(a_log.astype(f32)) * dt).reshape(b, nc, CHUNK, SSD_GROUPS, hpg)
    xdt = xs * dt.reshape(b, nc, CHUNK, SSD_GROUPS, hpg)[..., None]
    acs = jnp.moveaxis(jnp.cumsum(a, axis=2), 2, -1)
    causal = jnp.tril(jnp.ones((CHUNK, CHUNK), dtype=bool))
    seg = jnp.exp(jnp.where(causal, acs[..., :, None] - acs[..., None, :], -jnp.inf))
    cb = jnp.einsum('bclgn,bcsgn->bcgls', cc, bc)
    y_diag = jnp.einsum('bcgjls,bcsgjp->bclgjp', cb[:, :, :, None] * seg, xdt)
    decay_states = jnp.exp(acs[..., -1:] - acs)
    states = jnp.einsum('bclgn,bcgjl,bclgjp->bcgjpn', bc, decay_states, xdt)
    chunk_decay = jnp.exp(acs[..., -1])

    def step(hstate, inp):
        st, dec = inp
        return hstate * dec[..., None, None] + st, hstate

    h0 = jnp.zeros((b, SSD_GROUPS, hpg, SSD_HEAD_DIM, SSD_STATE), f32)
    _, prev = lax.scan(step, h0, (jnp.moveaxis(states, 1, 0), jnp.moveaxis(chunk_decay, 1, 0)))
    prev = jnp.moveaxis(prev, 0, 1)
    y_off = jnp.einsum('bclgn,bcgjpn,bcgjl->bclgjp', cc, prev, jnp.exp(acs))
    y = y_diag + y_off + xs * d_skip.astype(f32).reshape(SSD_GROUPS, hpg)[..., None]
    y = y.reshape(b, L, SSD_INNER) * jax.nn.silu(z.astype(f32))
    yg = y.reshape(b, L, SSD_GROUPS, SSD_INNER // SSD_GROUPS)
    yg = yg * lax.rsqrt(jnp.mean(yg * yg, axis=-1, keepdims=True) + EPS)
    return (yg.reshape(b, L, SSD_INNER) * norm_w.astype(f32)).astype(z.dtype)


def rotate(x, pos):
    half = x.shape[-1] // 2
    inv = ROPE_BASE ** (-jnp.arange(half, dtype=jnp.float32) / half)
    ang = pos.astype(jnp.float32)[:, None] * inv[None, :]
    cos = jnp.cos(ang)[None, :, None, :]
    sin = jnp.sin(ang)[None, :, None, :]
    x1, x2 = x[..., :half], x[..., half:]
    return jnp.concatenate([x1 * cos - x2 * sin, x1 * sin + x2 * cos], axis=-1)


def retention_mixer(q, k, v, g, valid):
    f32 = jnp.float32
    b, L, _ = q.shape
    nc = L // CHUNK
    pos = jnp.arange(L)
    qr = rotate(q.astype(f32).reshape(b, L, RET_HEADS, RET_QK_DIM), pos)
    kr = rotate(k.astype(f32).reshape(b, L, RET_HEADS, RET_QK_DIM), pos) * (RET_QK_DIM ** -0.5)
    vr = v.astype(f32).reshape(b, L, RET_HEADS, RET_V_DIM) * valid.astype(f32)[None, :, None, None]
    log_gamma = jnp.log(1.0 - jnp.power(2.0, -5.0 - jnp.arange(RET_HEADS, dtype=f32)))
    idx = jnp.arange(CHUNK, dtype=f32)
    rel = idx[:, None] - idx[None, :]
    dmask = jnp.where(rel >= 0, jnp.exp(log_gamma[:, None, None] * jnp.maximum(rel, 0.0)), 0.0)
    qc = qr.reshape(b, nc, CHUNK, RET_HEADS, RET_QK_DIM)
    kc = kr.reshape(b, nc, CHUNK, RET_HEADS, RET_QK_DIM)
    vc = vr.reshape(b, nc, CHUNK, RET_HEADS, RET_V_DIM)
    scores = jnp.einsum('bclhd,bcshd->bchls', qc, kc) * dmask
    y_in = jnp.einsum('bchls,bcshe->bclhe', scores, vc)
    k_decay = jnp.exp(log_gamma[:, None] * (CHUNK - 1 - idx)[None, :])
    kv = jnp.einsum('bcshd,hs,bcshe->bchde', kc, k_decay, vc)
    chunk_decay = jnp.exp(log_gamma * CHUNK)

    def step(r, kv_c):
        return r * chunk_decay[:, None, None] + kv_c, r

    r0 = jnp.zeros((b, RET_HEADS, RET_QK_DIM, RET_V_DIM), f32)
    _, prev = lax.scan(step, r0, jnp.moveaxis(kv, 1, 0))
    prev = jnp.moveaxis(prev, 0, 1)
    q_decay = jnp.exp(log_gamma[None, :] * (idx + 1.0)[:, None])
    y_cr = jnp.einsum('bclhd,bchde->bclhe', qc, prev) * q_decay[:, :, None]
    y = (y_in + y_cr).reshape(b, L, RET_HEADS, RET_V_DIM)
    mu = jnp.mean(y, axis=-1, keepdims=True)
    var = jnp.mean(jnp.square(y - mu), axis=-1, keepdims=True)
    y = ((y - mu) * lax.rsqrt(var + EPS)).reshape(b, L, RET_WIDTH)
    return (y * jax.nn.silu(g.astype(f32))).astype(q.dtype)


def stick_breaking_mixer(q, k, v, valid):
    f32 = jnp.float32
    b, L, _ = q.shape
    nb = L // CHUNK
    qh = q.reshape(b, L, SB_HEADS, SB_HEAD_DIM)
    kh = k.reshape(b, L, SB_HEADS, SB_HEAD_DIM)
    vh = v.reshape(b, L, SB_HEADS, SB_HEAD_DIM)
    qb = jnp.moveaxis(qh.reshape(b, nb, CHUNK, SB_HEADS, SB_HEAD_DIM), 1, 0)
    key_pos = jnp.arange(L)
    scale = SB_HEAD_DIM ** -0.5

    def block(args):
        q_blk, i = args
        t = i * CHUNK + jnp.arange(CHUNK)
        z = jnp.einsum('bthd,bshd->bhts', q_blk, kh).astype(f32) * scale
        m = (key_pos[None, :] < t[:, None]) & valid[None, :]
        l_neg = jnp.where(m, jax.nn.log_sigmoid(-z), 0.0)
        log_w = jax.nn.log_sigmoid(z) + lax.cumsum(l_neg, axis=3, reverse=True) - l_neg
        w = jnp.where(m, jnp.exp(log_w), 0.0)
        return jnp.einsum('bhts,bshe->bthe', w.astype(vh.dtype), vh)

    out = lax.map(block, (qb, jnp.arange(nb)))
    return jnp.moveaxis(out, 0, 1).reshape(b, L, SB_WIDTH).astype(q.dtype)


def hybrid_layer(h_res, valid, w_in, conv_a, ssd_conv_w, ssd_conv_b, ssd_dt_bias, ssd_a_log,
                 ssd_d, ssd_norm, w_branch, w_out, w_ffn_in, w_ffn_out,
                 n_mix_pre, n_mix_post, n_ffn_pre, n_ffn_post):
    b, L, _ = h_res.shape
    h = rmsnorm(h_res, n_mix_pre)
    proj = jnp.einsum('bld,de->ble', h, w_in)
    (a_b, a_c, a_x, s_z, s_xbc, s_dt, r_q, r_k, r_v, r_g,
     sb_q, sb_k, sb_v, gate_logits) = jnp.split(proj, IN_SPLITS, axis=-1)
    y_a = short_conv_mixer(a_b, a_c, a_x, conv_a, valid)
    y_b = ssd_mixer(s_z, s_xbc, s_dt, ssd_conv_w, ssd_conv_b, ssd_dt_bias, ssd_a_log, ssd_d, ssd_norm, valid)
    y_c = retention_mixer(r_q, r_k, r_v, r_g, valid)
    y_d = stick_breaking_mixer(sb_q, sb_k, sb_v, valid)
    branches = jnp.stack([y_a, y_b, y_c, y_d], axis=2).astype(h.dtype)
    up = jnp.einsum('blnw,nwd->blnd', branches, w_branch)
    gates = jax.nn.sigmoid(gate_logits.reshape(b, L, N_BRANCH, D_MODEL))
    merged = jnp.sum(gates * up, axis=2)
    mix = jnp.einsum('bld,de->ble', merged, w_out)
    h_res = h_res + rmsnorm(mix, n_mix_post)
    f = jnp.einsum('bld,df->blf', rmsnorm(h_res, n_ffn_pre), w_ffn_in)
    f_gate, f_up = jnp.split(f, 2, axis=-1)
    f = jnp.einsum('blf,fd->bld', jax.nn.silu(f_gate) * f_up, w_ffn_out)
    return h_res + rmsnorm(f, n_ffn_post)


def _fwd_setup_inputs(seed: int = 0) -> dict:
    key = jax.random.key(seed)
    ks = jax.random.split(key, 18)
    f32 = jnp.float32

    def nrm(k, shape, scale):
        return jax.random.normal(k, shape, f32) * scale

    dt0 = jnp.exp(jax.random.uniform(ks[6], (DEPTH, SSD_HEADS), f32, math.log(1e-3), math.log(1e-1)))
    return {
        'x': nrm(ks[0], (BATCH, SEQ, D_MODEL), 1.0),
        'meta': nrm(ks[1], (N_META, D_MODEL), 1.0),
        'w_in': nrm(ks[2], (DEPTH, D_MODEL, IN_WIDTH), D_MODEL ** -0.5),
        'conv_a': nrm(ks[3], (DEPTH, CONV_A_K, CONV_A_WIDTH), CONV_A_K ** -0.5),
        'ssd_conv_w': nrm(ks[4], (DEPTH, SSD_CONV_K, SSD_CONV_DIM), SSD_CONV_K ** -0.5),
        'ssd_conv_b': nrm(ks[5], (DEPTH, SSD_CONV_DIM), 0.02),
        'ssd_dt_bias': dt0 + jnp.log(-jnp.expm1(-dt0)),
        'ssd_a_log': jnp.log(jax.random.uniform(ks[7], (DEPTH, SSD_HEADS), f32, 1.0, 16.0)),
        'ssd_d': 1.0 + nrm(ks[8], (DEPTH, SSD_HEADS), 0.1),
        'ssd_norm': 1.0 + nrm(ks[9], (DEPTH, SSD_INNER), 0.02),
        'w_branch': nrm(ks[10], (DEPTH, N_BRANCH, BRANCH_WIDTH, D_MODEL), BRANCH_WIDTH ** -0.5),
        'w_out': nrm(ks[11], (DEPTH, D_MODEL, D_MODEL), D_MODEL ** -0.5),
        'w_ffn_in': nrm(ks[12], (DEPTH, D_MODEL, 2 * D_FF), D_MODEL ** -0.5),
        'w_ffn_out': nrm(ks[13], (DEPTH, D_FF, D_MODEL), D_FF ** -0.5),
        'norm_mix_pre': 1.0 + nrm(ks[14], (DEPTH, D_MODEL), 0.02),
        'norm_mix_post': 1.0 + nrm(ks[15], (DEPTH, D_MODEL), 0.02),
        'norm_ffn_pre': 1.0 + nrm(ks[16], (DEPTH, D_MODEL), 0.02),
        'norm_ffn_post': 1.0 + nrm(ks[17], (DEPTH, D_MODEL), 0.02),
    }


def _fwd_reference(x, meta, w_in, conv_a, ssd_conv_w, ssd_conv_b, ssd_dt_bias, ssd_a_log, ssd_d,
              ssd_norm, w_branch, w_out, w_ffn_in, w_ffn_out,
              norm_mix_pre, norm_mix_post, norm_ffn_pre, norm_ffn_post):
    b = x.shape[0]
    dtype = x.dtype
    h = jnp.concatenate([
        jnp.zeros((b, PAD, D_MODEL), dtype),
        jnp.broadcast_to(meta.astype(dtype)[None], (b, N_META, D_MODEL)),
        x,
    ], axis=1)
    valid = jnp.arange(h.shape[1]) >= PAD
    for l in range(DEPTH):
        h = hybrid_layer(h, valid, w_in[l], conv_a[l], ssd_conv_w[l], ssd_conv_b[l],
                         ssd_dt_bias[l], ssd_a_log[l], ssd_d[l], ssd_norm[l], w_branch[l],
                         w_out[l], w_ffn_in[l], w_ffn_out[l], norm_mix_pre[l],
                         norm_mix_post[l], norm_ffn_pre[l], norm_ffn_post[l])
    return h[:, CHUNK:]


import jax as _jax
import jax.numpy as _jnp

TWIN_FORMAT = 'train_step'
FWD_PARAMS = ['x', 'meta', 'w_in', 'conv_a', 'ssd_conv_w', 'ssd_conv_b', 'ssd_dt_bias', 'ssd_a_log', 'ssd_d', 'ssd_norm', 'w_branch', 'w_out', 'w_ffn_in', 'w_ffn_out', 'norm_mix_pre', 'norm_mix_post', 'norm_ffn_pre', 'norm_ffn_post']
TWIN_WEIGHTS = ['meta', 'w_in', 'conv_a', 'ssd_conv_w', 'ssd_conv_b', 'ssd_dt_bias', 'ssd_a_log', 'ssd_d', 'ssd_norm', 'w_branch', 'w_out', 'w_ffn_in', 'w_ffn_out', 'norm_mix_pre', 'norm_mix_post', 'norm_ffn_pre', 'norm_ffn_post']
TWIN_DIFF_INPUT = 'x'
TWIN_INPUTS = ['x', 'meta', 'w_in', 'conv_a', 'ssd_conv_w', 'ssd_conv_b', 'ssd_dt_bias', 'ssd_a_log', 'ssd_d', 'ssd_norm', 'w_branch', 'w_out', 'w_ffn_in', 'w_ffn_out', 'norm_mix_pre', 'norm_mix_post', 'norm_ffn_pre', 'norm_ffn_post', 'loss_target', 'm_meta', 'm_w_in', 'm_conv_a', 'm_ssd_conv_w', 'm_ssd_conv_b', 'm_ssd_dt_bias', 'm_ssd_a_log', 'm_ssd_d', 'm_ssd_norm', 'm_w_branch', 'm_w_out', 'm_w_ffn_in', 'm_w_ffn_out', 'm_norm_mix_pre', 'm_norm_mix_post', 'm_norm_ffn_pre', 'm_norm_ffn_post', 'v_meta', 'v_w_in', 'v_conv_a', 'v_ssd_conv_w', 'v_ssd_conv_b', 'v_ssd_dt_bias', 'v_ssd_a_log', 'v_ssd_d', 'v_ssd_norm', 'v_w_branch', 'v_w_out', 'v_w_ffn_in', 'v_w_ffn_out', 'v_norm_mix_pre', 'v_norm_mix_post', 'v_norm_ffn_pre', 'v_norm_ffn_post']
TWIN_OUTPUTS = ['loss', 'grad_x', 'grad_meta', 'grad_w_in', 'grad_conv_a', 'grad_ssd_conv_w', 'grad_ssd_conv_b', 'grad_ssd_dt_bias', 'grad_ssd_a_log', 'grad_ssd_d', 'grad_ssd_norm', 'grad_w_branch', 'grad_w_out', 'grad_w_ffn_in', 'grad_w_ffn_out', 'grad_norm_mix_pre', 'grad_norm_mix_post', 'grad_norm_ffn_pre', 'grad_norm_ffn_post', 'delta_meta', 'delta_w_in', 'delta_conv_a', 'delta_ssd_conv_w', 'delta_ssd_conv_b', 'delta_ssd_dt_bias', 'delta_ssd_a_log', 'delta_ssd_d', 'delta_ssd_norm', 'delta_w_branch', 'delta_w_out', 'delta_w_ffn_in', 'delta_w_ffn_out', 'delta_norm_mix_pre', 'delta_norm_mix_post', 'delta_norm_ffn_pre', 'delta_norm_ffn_post', 'new_m_meta', 'new_m_w_in', 'new_m_conv_a', 'new_m_ssd_conv_w', 'new_m_ssd_conv_b', 'new_m_ssd_dt_bias', 'new_m_ssd_a_log', 'new_m_ssd_d', 'new_m_ssd_norm', 'new_m_w_branch', 'new_m_w_out', 'new_m_w_ffn_in', 'new_m_w_ffn_out', 'new_m_norm_mix_pre', 'new_m_norm_mix_post', 'new_m_norm_ffn_pre', 'new_m_norm_ffn_post', 'new_v_meta', 'new_v_w_in', 'new_v_conv_a', 'new_v_ssd_conv_w', 'new_v_ssd_conv_b', 'new_v_ssd_dt_bias', 'new_v_ssd_a_log', 'new_v_ssd_d', 'new_v_ssd_norm', 'new_v_w_branch', 'new_v_w_out', 'new_v_w_ffn_in', 'new_v_w_ffn_out', 'new_v_norm_mix_pre', 'new_v_norm_mix_post', 'new_v_norm_ffn_pre', 'new_v_norm_ffn_post']
TWIN_LEAF_KINDS = {'loss': 'loss', 'grad_x': 'grad_x', 'grad_meta': 'grad_w', 'grad_w_in': 'grad_w', 'grad_conv_a': 'grad_w', 'grad_ssd_conv_w': 'grad_w', 'grad_ssd_conv_b': 'grad_w', 'grad_ssd_dt_bias': 'grad_w', 'grad_ssd_a_log': 'grad_w', 'grad_ssd_d': 'grad_w', 'grad_ssd_norm': 'grad_w', 'grad_w_branch': 'grad_w', 'grad_w_out': 'grad_w', 'grad_w_ffn_in': 'grad_w', 'grad_w_ffn_out': 'grad_w', 'grad_norm_mix_pre': 'grad_w', 'grad_norm_mix_post': 'grad_w', 'grad_norm_ffn_pre': 'grad_w', 'grad_norm_ffn_post': 'grad_w', 'delta_meta': 'delta_w', 'delta_w_in': 'delta_w', 'delta_conv_a': 'delta_w', 'delta_ssd_conv_w': 'delta_w', 'delta_ssd_conv_b': 'delta_w', 'delta_ssd_dt_bias': 'delta_w', 'delta_ssd_a_log': 'delta_w', 'delta_ssd_d': 'delta_w', 'delta_ssd_norm': 'delta_w', 'delta_w_branch': 'delta_w', 'delta_w_out': 'delta_w', 'delta_w_ffn_in': 'delta_w', 'delta_w_ffn_out': 'delta_w', 'delta_norm_mix_pre': 'delta_w', 'delta_norm_mix_post': 'delta_w', 'delta_norm_ffn_pre': 'delta_w', 'delta_norm_ffn_post': 'delta_w', 'new_m_meta': 'new_m', 'new_m_w_in': 'new_m', 'new_m_conv_a': 'new_m', 'new_m_ssd_conv_w': 'new_m', 'new_m_ssd_conv_b': 'new_m', 'new_m_ssd_dt_bias': 'new_m', 'new_m_ssd_a_log': 'new_m', 'new_m_ssd_d': 'new_m', 'new_m_ssd_norm': 'new_m', 'new_m_w_branch': 'new_m', 'new_m_w_out': 'new_m', 'new_m_w_ffn_in': 'new_m', 'new_m_w_ffn_out': 'new_m', 'new_m_norm_mix_pre': 'new_m', 'new_m_norm_mix_post': 'new_m', 'new_m_norm_ffn_pre': 'new_m', 'new_m_norm_ffn_post': 'new_m', 'new_v_meta': 'new_v', 'new_v_w_in': 'new_v', 'new_v_conv_a': 'new_v', 'new_v_ssd_conv_w': 'new_v', 'new_v_ssd_conv_b': 'new_v', 'new_v_ssd_dt_bias': 'new_v', 'new_v_ssd_a_log': 'new_v', 'new_v_ssd_d': 'new_v', 'new_v_ssd_norm': 'new_v', 'new_v_w_branch': 'new_v', 'new_v_w_out': 'new_v', 'new_v_w_ffn_in': 'new_v', 'new_v_w_ffn_out': 'new_v', 'new_v_norm_mix_pre': 'new_v', 'new_v_norm_mix_post': 'new_v', 'new_v_norm_ffn_pre': 'new_v', 'new_v_norm_ffn_post': 'new_v'}


def _forward(args):
    return _fwd_reference(*[args[k] for k in FWD_PARAMS])


def _output_shape():
    def fwd():
        inp = _fwd_setup_inputs(0)
        return _fwd_reference(*[inp[k] for k in FWD_PARAMS])
    out = _jax.eval_shape(fwd)
    return out.shape, out.dtype

N_MICROBATCH = 1
ADAM_LR = 0.001
ADAM_B1 = 0.9
ADAM_B2 = 0.999
ADAM_EPS = 1e-08
ADAM_WD = 0.01
ADAM_STEP = 10
PER_EXAMPLE_BATCH_AXIS = {'x': 0, 'loss_target': 0}
SHARED_INPUTS = []
_WEIGHT_DTYPES = {'meta': _jnp.float32, 'w_in': _jnp.float32, 'conv_a': _jnp.float32, 'ssd_conv_w': _jnp.float32, 'ssd_conv_b': _jnp.float32, 'ssd_dt_bias': _jnp.float32, 'ssd_a_log': _jnp.float32, 'ssd_d': _jnp.float32, 'ssd_norm': _jnp.float32, 'w_branch': _jnp.float32, 'w_out': _jnp.float32, 'w_ffn_in': _jnp.float32, 'w_ffn_out': _jnp.float32, 'norm_mix_pre': _jnp.float32, 'norm_mix_post': _jnp.float32, 'norm_ffn_pre': _jnp.float32, 'norm_ffn_post': _jnp.float32}
MOMENT_SCALE = {'meta': 1.105241e-01, 'w_in': 6.478410e-01, 'conv_a': 9.395291e-01, 'ssd_conv_w': 9.507107e-01, 'ssd_conv_b': 2.729273e+00, 'ssd_dt_bias': 1.546654e+00, 'ssd_a_log': 3.389406e+00, 'ssd_d': 9.531027e+00, 'ssd_norm': 1.821066e+00, 'w_branch': 1.004586e+00, 'w_out': 2.147654e+00, 'w_ffn_in': 9.033692e-01, 'w_ffn_out': 1.811258e+00, 'norm_mix_pre': 2.648064e+00, 'norm_mix_post': 1.274483e+02, 'norm_ffn_pre': 1.970085e+00, 'norm_ffn_post': 1.274498e+02}


def _to_microbatches(a, axis):
    t = _jnp.moveaxis(a, axis, 0)
    t = t.reshape((N_MICROBATCH, t.shape[0] // N_MICROBATCH) + t.shape[1:])
    return _jnp.moveaxis(t, 1, axis + 1)


def setup_inputs(seed: int = 0) -> dict:
    inp = _fwd_setup_inputs(seed)
    key = _jax.random.fold_in(_jax.random.key(seed), 7919)
    shape, _ = _output_shape()
    out = dict(inp)
    out["loss_target"] = _jax.random.normal(_jax.random.fold_in(key, 0), shape, _jnp.float32)
    for i, name in enumerate(TWIN_WEIGHTS):
        w = inp[name].astype(_jnp.float32)
        if MOMENT_SCALE is None:
            s = _jnp.sqrt(_jnp.mean(_jnp.square(w)) + 1e-30)
        else:
            s = MOMENT_SCALE[name]
        km, kv = _jax.random.split(_jax.random.fold_in(key, i + 1))
        out[name] = w
        out["m_" + name] = s * _jax.random.normal(km, w.shape, _jnp.float32)
        out["v_" + name] = (s * s) * _jax.random.uniform(kv, w.shape, _jnp.float32, 0.5, 1.5)
    if N_MICROBATCH > 1:
        for name, axis in PER_EXAMPLE_BATCH_AXIS.items():
            out[name] = _to_microbatches(out[name], axis)
    return {'x': out['x'], 'meta': out['meta'], 'w_in': out['w_in'], 'conv_a': out['conv_a'], 'ssd_conv_w': out['ssd_conv_w'], 'ssd_conv_b': out['ssd_conv_b'], 'ssd_dt_bias': out['ssd_dt_bias'], 'ssd_a_log': out['ssd_a_log'], 'ssd_d': out['ssd_d'], 'ssd_norm': out['ssd_norm'], 'w_branch': out['w_branch'], 'w_out': out['w_out'], 'w_ffn_in': out['w_ffn_in'], 'w_ffn_out': out['w_ffn_out'], 'norm_mix_pre': out['norm_mix_pre'], 'norm_mix_post': out['norm_mix_post'], 'norm_ffn_pre': out['norm_ffn_pre'], 'norm_ffn_post': out['norm_ffn_post'], 'loss_target': out['loss_target'], 'm_meta': out['m_meta'], 'm_w_in': out['m_w_in'], 'm_conv_a': out['m_conv_a'], 'm_ssd_conv_w': out['m_ssd_conv_w'], 'm_ssd_conv_b': out['m_ssd_conv_b'], 'm_ssd_dt_bias': out['m_ssd_dt_bias'], 'm_ssd_a_log': out['m_ssd_a_log'], 'm_ssd_d': out['m_ssd_d'], 'm_ssd_norm': out['m_ssd_norm'], 'm_w_branch': out['m_w_branch'], 'm_w_out': out['m_w_out'], 'm_w_ffn_in': out['m_w_ffn_in'], 'm_w_ffn_out': out['m_w_ffn_out'], 'm_norm_mix_pre': out['m_norm_mix_pre'], 'm_norm_mix_post': out['m_norm_mix_post'], 'm_norm_ffn_pre': out['m_norm_ffn_pre'], 'm_norm_ffn_post': out['m_norm_ffn_post'], 'v_meta': out['v_meta'], 'v_w_in': out['v_w_in'], 'v_conv_a': out['v_conv_a'], 'v_ssd_conv_w': out['v_ssd_conv_w'], 'v_ssd_conv_b': out['v_ssd_conv_b'], 'v_ssd_dt_bias': out['v_ssd_dt_bias'], 'v_ssd_a_log': out['v_ssd_a_log'], 'v_ssd_d': out['v_ssd_d'], 'v_ssd_norm': out['v_ssd_norm'], 'v_w_branch': out['v_w_branch'], 'v_w_out': out['v_w_out'], 'v_w_ffn_in': out['v_w_ffn_in'], 'v_w_ffn_out': out['v_w_ffn_out'], 'v_norm_mix_pre': out['v_norm_mix_pre'], 'v_norm_mix_post': out['v_norm_mix_post'], 'v_norm_ffn_pre': out['v_norm_ffn_pre'], 'v_norm_ffn_post': out['v_norm_ffn_post']}


def _loss(weights, diff, rest, loss_target):
    with _jax.named_scope("forward"):
        args = {**rest, TWIN_DIFF_INPUT: diff, **{k: w.astype(_WEIGHT_DTYPES[k]) for k, w in weights.items()}}
        y = _forward(args)
    with _jax.named_scope("loss_head"):
        err = _jnp.square(y.astype(_jnp.float32) - loss_target)
        return 0.5 * _jnp.sum(_jnp.mean(err, axis=-1)) if err.ndim else 0.5 * err


def _adamw(w, g, m, v):
    m = ADAM_B1 * m + (1.0 - ADAM_B1) * g
    v = ADAM_B2 * v + (1.0 - ADAM_B2) * _jnp.square(g)
    m_hat = m / (1.0 - ADAM_B1 ** ADAM_STEP)
    v_hat = v / (1.0 - ADAM_B2 ** ADAM_STEP)
    delta = -ADAM_LR * (m_hat / (_jnp.sqrt(v_hat) + ADAM_EPS) + ADAM_WD * w)
    return delta, m, v


def reference(x, meta, w_in, conv_a, ssd_conv_w, ssd_conv_b, ssd_dt_bias, ssd_a_log, ssd_d, ssd_norm, w_branch, w_out, w_ffn_in, w_ffn_out, norm_mix_pre, norm_mix_post, norm_ffn_pre, norm_ffn_post, loss_target, m_meta, m_w_in, m_conv_a, m_ssd_conv_w, m_ssd_conv_b, m_ssd_dt_bias, m_ssd_a_log, m_ssd_d, m_ssd_norm, m_w_branch, m_w_out, m_w_ffn_in, m_w_ffn_out, m_norm_mix_pre, m_norm_mix_post, m_norm_ffn_pre, m_norm_ffn_post, v_meta, v_w_in, v_conv_a, v_ssd_conv_w, v_ssd_conv_b, v_ssd_dt_bias, v_ssd_a_log, v_ssd_d, v_ssd_norm, v_w_branch, v_w_out, v_w_ffn_in, v_w_ffn_out, v_norm_mix_pre, v_norm_mix_post, v_norm_ffn_pre, v_norm_ffn_post):
    given = dict(x=x, meta=meta, w_in=w_in, conv_a=conv_a, ssd_conv_w=ssd_conv_w, ssd_conv_b=ssd_conv_b, ssd_dt_bias=ssd_dt_bias, ssd_a_log=ssd_a_log, ssd_d=ssd_d, ssd_norm=ssd_norm, w_branch=w_branch, w_out=w_out, w_ffn_in=w_ffn_in, w_ffn_out=w_ffn_out, norm_mix_pre=norm_mix_pre, norm_mix_post=norm_mix_post, norm_ffn_pre=norm_ffn_pre, norm_ffn_post=norm_ffn_post, loss_target=loss_target, m_meta=m_meta, m_w_in=m_w_in, m_conv_a=m_conv_a, m_ssd_conv_w=m_ssd_conv_w, m_ssd_conv_b=m_ssd_conv_b, m_ssd_dt_bias=m_ssd_dt_bias, m_ssd_a_log=m_ssd_a_log, m_ssd_d=m_ssd_d, m_ssd_norm=m_ssd_norm, m_w_branch=m_w_branch, m_w_out=m_w_out, m_w_ffn_in=m_w_ffn_in, m_w_ffn_out=m_w_ffn_out, m_norm_mix_pre=m_norm_mix_pre, m_norm_mix_post=m_norm_mix_post, m_norm_ffn_pre=m_norm_ffn_pre, m_norm_ffn_post=m_norm_ffn_post, v_meta=v_meta, v_w_in=v_w_in, v_conv_a=v_conv_a, v_ssd_conv_w=v_ssd_conv_w, v_ssd_conv_b=v_ssd_conv_b, v_ssd_dt_bias=v_ssd_dt_bias, v_ssd_a_log=v_ssd_a_log, v_ssd_d=v_ssd_d, v_ssd_norm=v_ssd_norm, v_w_branch=v_w_branch, v_w_out=v_w_out, v_w_ffn_in=v_w_ffn_in, v_w_ffn_out=v_w_ffn_out, v_norm_mix_pre=v_norm_mix_pre, v_norm_mix_post=v_norm_mix_post, v_norm_ffn_pre=v_norm_ffn_pre, v_norm_ffn_post=v_norm_ffn_post)
    weights = {n: given[n] for n in TWIN_WEIGHTS}
    shared = {n: given[n] for n in SHARED_INPUTS}
    per_example = {n: given[n] for n in ['x']}
    grad_fn = _jax.value_and_grad(_loss, argnums=(0, 1))

    def one_microbatch(ex, loss_target):
        ex = dict(ex)
        diff = ex.pop(TWIN_DIFF_INPUT)
        return grad_fn(weights, diff, {**shared, **ex}, loss_target)

    if N_MICROBATCH == 1:
        loss, (grad_w, grad_x) = one_microbatch(per_example, given["loss_target"])
    else:
        def body(carry, xs):
            loss_sum, grad_sum = carry
            l_k, (gw_k, gx_k) = one_microbatch(xs[0], xs[1])
            with _jax.named_scope("update"):
                return (loss_sum + l_k, _jax.tree.map(_jnp.add, grad_sum, gw_k)), gx_k

        init = (_jnp.zeros((), _jnp.float32), _jax.tree.map(_jnp.zeros_like, weights))
        (loss, grad_w), grad_x = _jax.lax.scan(body, init, (per_example, given["loss_target"]))
    with _jax.named_scope("update"):
        delta_w, new_m, new_v = {}, {}, {}
        for n in TWIN_WEIGHTS:
            delta_w[n], new_m[n], new_v[n] = _adamw(weights[n], grad_w[n], given["m_" + n], given["v_" + n])
    return (loss, grad_x, *[grad_w[n] for n in TWIN_WEIGHTS], *[delta_w[n] for n in TWIN_WEIGHTS],
            *[new_m[n] for n in TWIN_WEIGHTS], *[new_v[n] for n in TWIN_WEIGHTS])
```

```python
import functools
import math

import numpy as np
import jax
import jax.numpy as jnp
from jax import lax
from jax.experimental import pallas as pl
from jax.experimental.pallas import tpu as pltpu

F32 = jnp.float32
BF16 = jnp.bfloat16

D_MODEL = 1024
DEPTH = 2
N_META = 16
CHUNK = 128
PAD = CHUNK - N_META
EPS = 1e-6

CONV_A_K = 3
SSD_HEAD_DIM = 64
SSD_HEADS = 16
SSD_INNER = 1024
SSD_GROUPS = 4
SSD_STATE = 128
SSD_CONV_K = 4
SSD_CONV_DIM = SSD_INNER + 2 * SSD_GROUPS * SSD_STATE
RET_HEADS = 4
RET_QK_DIM = 256
RET_V_DIM = 256
RET_WIDTH = 1024
ROPE_BASE = 10000.0
SB_HEADS = 8
SB_HEAD_DIM = 128
N_BRANCH = 4
D_FF = 2816
DT_PAD = 128

ADAM_LR = 0.001
ADAM_B1 = 0.9
ADAM_B2 = 0.999
ADAM_EPS = 1e-08
ADAM_WD = 0.01
ADAM_STEP = 10

N_CHIPS = 4
N_DEV = 8
LANES = 128
VMEM_LIMIT = 56 * 1024 * 1024
MESH = pl.DeviceIdType.MESH

WEIGHTS = ['meta', 'w_in', 'conv_a', 'ssd_conv_w', 'ssd_conv_b', 'ssd_dt_bias', 'ssd_a_log', 'ssd_d',
           'ssd_norm', 'w_branch', 'w_out', 'w_ffn_in', 'w_ffn_out', 'norm_mix_pre', 'norm_mix_post',
           'norm_ffn_pre', 'norm_ffn_post']
SHARD_AXIS = {'meta': 1, 'w_in': 2, 'conv_a': 2, 'ssd_conv_w': 2, 'w_branch': 2, 'w_out': 1,
              'w_ffn_in': 2, 'w_ffn_out': 1}
MATMUL_WEIGHTS = ['w_in', 'w_branch', 'w_out', 'w_ffn_in', 'w_ffn_out']
SMALL_SHARDED = ['meta', 'conv_a', 'ssd_conv_w']
PACK_ORDER = MATMUL_WEIGHTS + SMALL_SHARDED + [n for n in WEIGHTS if n not in SHARD_AXIS]
ADAM_TILE_ROWS = 1024


def _params(**kw):
    return pltpu.CompilerParams(vmem_limit_bytes=VMEM_LIMIT, **kw)


def _tile(n, prefs):
    for p in prefs:
        if n % p == 0:
            return p
    return n


def mm_nn(a, b, out_dtype=F32, name="mm_nn"):
    m, k = a.shape
    n = b.shape[1]
    tm = _tile(m, (384, 256, 128))
    tn = _tile(n, (1024, 512, 256, 128))

    def body(a_ref, b_ref, o_ref):
        o_ref[...] = jnp.dot(a_ref[...].astype(BF16), b_ref[...].astype(BF16),
                             preferred_element_type=F32).astype(o_ref.dtype)

    return pl.pallas_call(
        body, name=name, grid=(m // tm, n // tn),
        in_specs=[pl.BlockSpec((tm, k), lambda i, j: (i, 0)), pl.BlockSpec((k, tn), lambda i, j: (0, j))],
        out_specs=pl.BlockSpec((tm, tn), lambda i, j: (i, j)),
        out_shape=jax.ShapeDtypeStruct((m, n), out_dtype),
        compiler_params=_params(dimension_semantics=("arbitrary", "arbitrary")),
    )(a, b)


def mm_nt(g, w, acc=None, name="mm_nt"):
    m, n = g.shape
    k = w.shape[0]
    tm = _tile(m, (384, 256, 128))
    tn = _tile(n, (1024, 512, 256, 128))
    has_acc = acc is not None

    def body(*refs):
        if has_acc:
            g_ref, w_ref, acc_ref, o_ref = refs
        else:
            g_ref, w_ref, o_ref = refs
        j = pl.program_id(1)

        @pl.when(j == 0)
        def _():
            o_ref[...] = acc_ref[...] if has_acc else jnp.zeros_like(o_ref)

        o_ref[...] += lax.dot_general(g_ref[...].astype(BF16), w_ref[...].astype(BF16),
                                      (((1,), (1,)), ((), ())), preferred_element_type=F32)

    in_specs = [pl.BlockSpec((tm, tn), lambda i, j: (i, j)), pl.BlockSpec((k, tn), lambda i, j: (0, j))]
    args = [g, w]
    if has_acc:
        in_specs.append(pl.BlockSpec((tm, k), lambda i, j: (i, 0)))
        args.append(acc)
    return pl.pallas_call(
        body, name=name, grid=(m // tm, n // tn),
        in_specs=in_specs,
        out_specs=pl.BlockSpec((tm, k), lambda i, j: (i, 0)),
        out_shape=jax.ShapeDtypeStruct((m, k), F32),
        compiler_params=_params(dimension_semantics=("arbitrary", "arbitrary")),
    )(*args)


def mm_tn(x, g, name="mm_tn"):
    m, k = x.shape
    n = g.shape[1]
    tm = _tile(m, (384, 256, 128))
    tk = _tile(k, (1024, 1408, 512, 256, 128))
    tn = _tile(n, (1024, 512, 256, 128))

    def body(x_ref, g_ref, o_ref):
        s = pl.program_id(2)

        @pl.when(s == 0)
        def _():
            o_ref[...] = jnp.zeros_like(o_ref)

        o_ref[...] += lax.dot_general(x_ref[...].astype(BF16), g_ref[...].astype(BF16),
                                      (((0,), (0,)), ((), ())), preferred_element_type=F32)

    return pl.pallas_call(
        body, name=name, grid=(k // tk, n // tn, m // tm),
        in_specs=[pl.BlockSpec((tm, tk), lambda a, b, s: (s, a)), pl.BlockSpec((tm, tn), lambda a, b, s: (s, b))],
        out_specs=pl.BlockSpec((tk, tn), lambda a, b, s: (a, b)),
        out_shape=jax.ShapeDtypeStruct((k, n), F32),
        compiler_params=_params(dimension_semantics=("arbitrary", "arbitrary", "arbitrary")),
    )(x, g)


def rms_fwd(x, w, res=None, name="rms_fwd"):
    m, d = x.shape
    tm = _tile(m, (384, 256, 128))
    has_res = res is not None

    def body(*refs):
        if has_res:
            x_ref, w_ref, r_ref, o_ref = refs
        else:
            x_ref, w_ref, o_ref = refs
        xv = x_ref[...]
        y = xv * lax.rsqrt(jnp.mean(xv * xv, axis=-1, keepdims=True) + EPS) * w_ref[...]
        o_ref[...] = y + r_ref[...] if has_res else y

    row = pl.BlockSpec((tm, d), lambda i: (i, 0))
    in_specs = [row, pl.BlockSpec((1, d), lambda i: (0, 0))]
    args = [x, w]
    if has_res:
        in_specs.append(row)
        args.append(res)
    return pl.pallas_call(
        body, name=name, grid=(m // tm,), in_specs=in_specs, out_specs=row,
        out_shape=jax.ShapeDtypeStruct((m, d), F32),
        compiler_params=_params(dimension_semantics=("arbitrary",)),
    )(*args)


def rms_bwd(x, w, dy, add=None, name="rms_bwd"):
    m, d = x.shape
    tm = _tile(m, (384, 256, 128))
    has_add = add is not None

    def body(*refs):
        if has_add:
            x_ref, w_ref, dy_ref, add_ref, dx_ref, dw_ref = refs
        else:
            x_ref, w_ref, dy_ref, dx_ref, dw_ref = refs
        i = pl.program_id(0)
        xv = x_ref[...]
        dyv = dy_ref[...]
        r = lax.rsqrt(jnp.mean(xv * xv, axis=-1, keepdims=True) + EPS)
        xh = xv * r
        dxh = dyv * w_ref[...]
        dx = r * (dxh - xh * jnp.mean(dxh * xh, axis=-1, keepdims=True))
        dx_ref[...] = dx + add_ref[...] if has_add else dx

        @pl.when(i == 0)
        def _():
            dw_ref[...] = jnp.zeros_like(dw_ref)

        dw_ref[...] += jnp.sum(dyv * xh, axis=0, keepdims=True)

    row = pl.BlockSpec((tm, d), lambda i: (i, 0))
    vec = pl.BlockSpec((1, d), lambda i: (0, 0))
    in_specs = [row, vec, row]
    args = [x, w, dy]
    if has_add:
        in_specs.append(row)
        args.append(add)
    return pl.pallas_call(
        body, name=name, grid=(m // tm,), in_specs=in_specs, out_specs=[row, vec],
        out_shape=[jax.ShapeDtypeStruct((m, d), F32), jax.ShapeDtypeStruct((1, d), F32)],
        compiler_params=_params(dimension_semantics=("arbitrary",)),
    )(*args)


def loss_head(h, target):
    l, d = h.shape
    nblk = l // CHUNK

    def body(h_ref, t_ref, loss_ref, dh_ref, acc_ref):
        i = pl.program_id(0)

        @pl.when(i == 0)
        def _():
            acc_ref[...] = jnp.zeros_like(acc_ref)
            dh_ref[...] = jnp.zeros_like(dh_ref)

        @pl.when(i > 0)
        def _():
            e = h_ref[...] - t_ref[...]
            dh_ref[...] = e / d
            acc_ref[...] += jnp.sum(e * e, axis=0, keepdims=True)

        @pl.when(i == nblk - 1)
        def _():
            loss_ref[...] = jnp.zeros_like(loss_ref) + 0.5 * jnp.sum(acc_ref[...]) / d

    return pl.pallas_call(
        body, name="loss_head", grid=(nblk,),
        in_specs=[pl.BlockSpec((CHUNK, d), lambda i: (i, 0)),
                  pl.BlockSpec((CHUNK, d), lambda i: (jnp.maximum(i - 1, 0), 0))],
        out_specs=[pl.BlockSpec((1, LANES), lambda i: (0, 0)), pl.BlockSpec((CHUNK, d), lambda i: (i, 0))],
        out_shape=[jax.ShapeDtypeStruct((1, LANES), F32), jax.ShapeDtypeStruct((l, d), F32)],
        scratch_shapes=[pltpu.VMEM((1, d), F32)],
        compiler_params=_params(dimension_semantics=("arbitrary",)),
    )(h, target)


SB_TK = 128


def _sb_scores(q, k, scale, mask):
    z = lax.dot_general(q, k, (((1,), (1,)), ((), ())), preferred_element_type=F32) * scale
    sp = jnp.maximum(z, 0.0) + jnp.log(1.0 + jnp.exp(-jnp.abs(z)))
    lneg = -sp if mask is None else jnp.where(mask, -sp, 0.0)
    return z, sp, lneg


def _later_sum(lneg, tri):
    hi = lneg.astype(BF16)
    lo = (lneg - hi.astype(F32)).astype(BF16)
    return jnp.dot(hi, tri, preferred_element_type=F32) + jnp.dot(lo, tri, preferred_element_type=F32)


def _sb_tiles(l):
    tq = _tile(l, (384, 256, 128))
    return tq, tq // SB_TK


def _sb_mask(i, j, tq):
    qpos = i * tq + lax.broadcasted_iota(jnp.int32, (tq, SB_TK), 0)
    kpos = j * SB_TK + lax.broadcasted_iota(jnp.int32, (tq, SB_TK), 1)
    return (kpos < qpos) & (kpos >= PAD)


def _sb_walk(i, nsub, step, carry):
    jd = i * nsub
    for s in reversed(range(nsub)):
        carry = step(jd + s, carry, True)
    carry = lax.fori_loop(0, jnp.maximum(jd - 1, 0), lambda t, c: step(jd - 1 - t, c, False), carry)
    return lax.fori_loop(0, jnp.minimum(jd, 1), lambda t, c: step(0, c, True), carry)


def _sb_walk_up(i, nsub, step, carry):
    jd = i * nsub
    carry = lax.fori_loop(0, jnp.minimum(jd, 1), lambda t, c: step(0, c, True), carry)
    carry = lax.fori_loop(1, jnp.maximum(jd, 1), lambda j, c: step(j, c, False), carry)
    for s in range(nsub):
        carry = step(jd + s, carry, True)
    return carry


def sb_fwd(qkv):
    l = qkv.shape[0]
    tq, nsub = _sb_tiles(l)
    scale = SB_HEAD_DIM ** -0.5

    def body(q_ref, k_ref, v_ref, o_ref, tot_ref):
        i = pl.program_id(1)
        q = q_ref[...]
        r_i = lax.broadcasted_iota(jnp.int32, (SB_TK, SB_TK), 0)
        c_i = lax.broadcasted_iota(jnp.int32, (SB_TK, SB_TK), 1)
        tri = jnp.where(r_i > c_i, 1.0, 0.0).astype(BF16)

        def step(j, carry, masked):
            run, acc = carry
            rows = pl.ds(pl.multiple_of(j * SB_TK, SB_TK), SB_TK)
            k = k_ref[rows, :]
            v = v_ref[rows, :]
            mask = _sb_mask(i, j, tq) if masked else None
            z, sp, lneg = _sb_scores(q, k, scale, mask)
            w = jnp.exp(z - sp + _later_sum(lneg, tri) + run)
            if masked:
                w = jnp.where(mask, w, 0.0)
            acc = acc + jnp.dot(w.astype(BF16), v, preferred_element_type=F32)
            return run + jnp.sum(lneg, axis=1, keepdims=True), acc

        carry = (jnp.zeros((tq, 1), F32), jnp.zeros((tq, SB_HEAD_DIM), F32))
        run, acc = _sb_walk(i, nsub, step, carry)
        o_ref[...] = acc
        tot_ref[...] = run

    return pl.pallas_call(
        body, name="sb_fwd", grid=(SB_HEADS, l // tq),
        in_specs=[pl.BlockSpec((tq, SB_HEAD_DIM), lambda h, i: (i, h)),
                  pl.BlockSpec((l, SB_HEAD_DIM), lambda h, i: (0, SB_HEADS + h)),
                  pl.BlockSpec((l, SB_HEAD_DIM), lambda h, i: (0, 2 * SB_HEADS + h))],
        out_specs=[pl.BlockSpec((tq, SB_HEAD_DIM), lambda h, i: (i, h)),
                   pl.BlockSpec((None, tq, 1), lambda h, i: (h, i, 0))],
        out_shape=[jax.ShapeDtypeStruct((l, D_MODEL), F32), jax.ShapeDtypeStruct((SB_HEADS, l, 1), F32)],
        compiler_params=_params(dimension_semantics=("arbitrary", "arbitrary")),
    )(qkv, qkv, qkv)


def sb_bwd(qkv, row_total, dout):
    l = qkv.shape[0]
    tq, nsub = _sb_tiles(l)
    nq = l // tq
    scale = SB_HEAD_DIM ** -0.5

    def body(q_ref, k_ref, v_ref, tot_ref, do_ref, dq_ref, dk_hbm, dv_hbm, dk_acc, dv_acc):
        h = pl.program_id(0)
        i = pl.program_id(1)

        @pl.when(i == 0)
        def _():
            dk_acc[...] = jnp.zeros_like(dk_acc)
            dv_acc[...] = jnp.zeros_like(dv_acc)

        q = q_ref[...]
        dob = do_ref[...].astype(BF16)
        r_i = lax.broadcasted_iota(jnp.int32, (SB_TK, SB_TK), 0)
        c_i = lax.broadcasted_iota(jnp.int32, (SB_TK, SB_TK), 1)
        tri = jnp.where(r_i > c_i, 1.0, 0.0).astype(BF16)
        tri_before = jnp.where(r_i < c_i, 1.0, 0.0).astype(BF16)

        def step(j, carry, masked):
            later, g_before, dq = carry
            rows = pl.ds(pl.multiple_of(j * SB_TK, SB_TK), SB_TK)
            k = k_ref[rows, :]
            v = v_ref[rows, :]
            mask = _sb_mask(i, j, tq) if masked else None
            z, sp, lneg = _sb_scores(q, k, scale, mask)
            later = later - jnp.sum(lneg, axis=1, keepdims=True)
            w = jnp.exp(z - sp + _later_sum(lneg, tri) + later)
            if masked:
                w = jnp.where(mask, w, 0.0)
            g = lax.dot_general(dob, v, (((1,), (1,)), ((), ())), preferred_element_type=F32) * w
            before = g_before + jnp.dot(g.astype(BF16), tri_before, preferred_element_type=F32)
            dz = g - (g + before) * jnp.exp(z - sp)
            if masked:
                dz = jnp.where(mask, dz, 0.0)
            dzb = (dz * scale).astype(BF16)
            dq = dq + jnp.dot(dzb, k, preferred_element_type=F32)
            dk_acc[rows, :] += lax.dot_general(dzb, q, (((0,), (0,)), ((), ())), preferred_element_type=F32)
            dv_acc[rows, :] += lax.dot_general(w.astype(BF16), dob, (((0,), (0,)), ((), ())),
                                               preferred_element_type=F32)
            return later, g_before + jnp.sum(g, axis=1, keepdims=True), dq

        carry = (tot_ref[...], jnp.zeros((tq, 1), F32), jnp.zeros((tq, SB_HEAD_DIM), F32))
        _, _, dq = _sb_walk_up(i, nsub, step, carry)
        dq_ref[...] = dq.astype(dq_ref.dtype)

        @pl.when(i == nq - 1)
        def _():
            cols = pl.ds(pl.multiple_of(h * SB_HEAD_DIM, SB_HEAD_DIM), SB_HEAD_DIM)
            pltpu.sync_copy(dk_acc, dk_hbm.at[:, cols])
            pltpu.sync_copy(dv_acc, dv_hbm.at[:, cols])

    blk = lambda h, i: (i, h)
    return pl.pallas_call(
        body, name="sb_bwd", grid=(SB_HEADS, nq),
        in_specs=[pl.BlockSpec((tq, SB_HEAD_DIM), blk),
                  pl.BlockSpec((l, SB_HEAD_DIM), lambda h, i: (0, SB_HEADS + h)),
                  pl.BlockSpec((l, SB_HEAD_DIM), lambda h, i: (0, 2 * SB_HEADS + h)),
                  pl.BlockSpec((None, tq, 1), lambda h, i: (h, i, 0)), pl.BlockSpec((tq, SB_HEAD_DIM), blk)],
        out_specs=[pl.BlockSpec((tq, SB_HEAD_DIM), blk), pl.BlockSpec(memory_space=pl.ANY),
                   pl.BlockSpec(memory_space=pl.ANY)],
        out_shape=[jax.ShapeDtypeStruct((l, D_MODEL), BF16), jax.ShapeDtypeStruct((l, D_MODEL), F32),
                   jax.ShapeDtypeStruct((l, D_MODEL), F32)],
        scratch_shapes=[pltpu.VMEM((l, SB_HEAD_DIM), F32), pltpu.VMEM((l, SB_HEAD_DIM), F32)],
        compiler_params=_params(dimension_semantics=("arbitrary", "arbitrary")),
    )(qkv, qkv, qkv, row_total, dout)


def gather_weight_shards(pack_bf16, pack_f32):
    rb = pack_bf16.shape[0]
    rf = pack_f32.shape[0]

    def body(b_ref, f_ref, ob_ref, of_ref, send_sems, recv_sems, local_sems):
        x, y, c = lax.axis_index("x"), lax.axis_index("y"), lax.axis_index("c")
        me = 2 * x + y
        chips = [(1 - x, y), (x, 1 - y), (1 - x, 1 - y)]
        own = [pltpu.make_async_copy(b_ref, ob_ref.at[me], local_sems.at[0]),
               pltpu.make_async_copy(f_ref, of_ref.at[me], local_sems.at[1])]
        for cp in own:
            cp.start()
        sends = []
        for k, (px, py) in enumerate(chips):
            for t, (src, dst) in enumerate(((b_ref, ob_ref), (f_ref, of_ref))):
                sends.append(pltpu.make_async_remote_copy(
                    src_ref=src, dst_ref=dst.at[me], send_sem=send_sems.at[2 * k + t],
                    recv_sem=recv_sems.at[2 * k + t], device_id=(px, py, c), device_id_type=MESH))
        for cp in sends:
            cp.start()
        for k, (px, py) in enumerate(chips):
            for t, (src, dst) in enumerate(((b_ref, ob_ref), (f_ref, of_ref))):
                pltpu.make_async_remote_copy(
                    src_ref=src, dst_ref=dst.at[2 * px + py], send_sem=send_sems.at[2 * k + t],
                    recv_sem=recv_sems.at[2 * k + t], device_id=(px, py, c), device_id_type=MESH).wait_recv()
        for cp in sends:
            cp.wait_send()
        for cp in own:
            cp.wait()

    hbm = pl.BlockSpec(memory_space=pl.ANY)
    return pl.pallas_call(
        body, name="gather_weight_shards",
        in_specs=[hbm, hbm], out_specs=[hbm, hbm],
        out_shape=[jax.ShapeDtypeStruct((N_CHIPS, rb, LANES), BF16),
                   jax.ShapeDtypeStruct((N_CHIPS, rf, LANES), F32)],
        scratch_shapes=[pltpu.SemaphoreType.DMA((6,)), pltpu.SemaphoreType.DMA((6,)),
                        pltpu.SemaphoreType.DMA((2,))],
    )(pack_bf16, pack_f32)


def exchange_grad_quarters(gpack):
    rows = gpack.shape[1]

    def body(g_ref, slots_ref, send_sems, recv_sems, local_sem):
        x, y, c = lax.axis_index("x"), lax.axis_index("y"), lax.axis_index("c")
        me = 4 * x + 2 * y + c
        my_chip = 2 * x + y
        own = pltpu.make_async_copy(g_ref.at[my_chip], slots_ref.at[me], local_sem)
        own.start()
        peers = []
        for rel in range(1, N_DEV):
            fx, fy, fc = (rel >> 2) & 1, (rel >> 1) & 1, rel & 1
            px = x + fx - 2 * x * fx
            py = y + fy - 2 * y * fy
            pc = c + fc - 2 * c * fc
            peers.append((px, py, pc))
        sends = [pltpu.make_async_remote_copy(
            src_ref=g_ref.at[2 * px + py], dst_ref=slots_ref.at[me], send_sem=send_sems.at[k],
            recv_sem=recv_sems.at[k], device_id=(px, py, pc), device_id_type=MESH)
            for k, (px, py, pc) in enumerate(peers)]
        for cp in sends:
            cp.start()
        for k, (px, py, pc) in enumerate(peers):
            pltpu.make_async_remote_copy(
                src_ref=g_ref.at[my_chip], dst_ref=slots_ref.at[4 * px + 2 * py + pc], send_sem=send_sems.at[k],
                recv_sem=recv_sems.at[k], device_id=(px, py, pc), device_id_type=MESH).wait_recv()
        for cp in sends:
            cp.wait_send()
        own.wait()

    hbm = pl.BlockSpec(memory_space=pl.ANY)
    return pl.pallas_call(
        body, name="exchange_grad_quarters",
        in_specs=[hbm], out_specs=hbm,
        out_shape=jax.ShapeDtypeStruct((N_DEV, rows, LANES), F32),
        scratch_shapes=[pltpu.SemaphoreType.DMA((N_DEV - 1,)), pltpu.SemaphoreType.DMA((N_DEV - 1,)),
                        pltpu.SemaphoreType.DMA],
    )(gpack)


def reduce_adamw(slots, w, m, v):
    rows = w.shape[0]
    tr = ADAM_TILE_ROWS

    def body(s_ref, w_ref, m_ref, v_ref, g_out, d_out, m_out, v_out):
        g = s_ref[0]
        for d in range(1, N_DEV):
            g = g + s_ref[d]
        m_new = ADAM_B1 * m_ref[...] + (1.0 - ADAM_B1) * g
        v_new = ADAM_B2 * v_ref[...] + (1.0 - ADAM_B2) * jnp.square(g)
        m_hat = m_new / (1.0 - ADAM_B1 ** ADAM_STEP)
        v_hat = v_new / (1.0 - ADAM_B2 ** ADAM_STEP)
        g_out[...] = g
        d_out[...] = -ADAM_LR * (m_hat / (jnp.sqrt(v_hat) + ADAM_EPS) + ADAM_WD * w_ref[...])
        m_out[...] = m_new
        v_out[...] = v_new

    row = pl.BlockSpec((tr, LANES), lambda i: (i, 0))
    return pl.pallas_call(
        body, name="reduce_adamw", grid=(rows // tr,),
        in_specs=[pl.BlockSpec((N_DEV, tr, LANES), lambda i: (0, i, 0)), row, row, row],
        out_specs=[row, row, row, row],
        out_shape=[jax.ShapeDtypeStruct((rows, LANES), F32)] * 4,
        compiler_params=_params(dimension_semantics=("arbitrary",)),
    )(slots, w, m, v)


def _pack(pieces, dtype, row_multiple):
    flat = jnp.concatenate([p.astype(dtype).reshape(-1) for p in pieces])
    per = row_multiple * LANES
    padded = -(-flat.shape[0] // per) * per
    flat = jnp.pad(flat, (0, padded - flat.shape[0]))
    return flat.reshape(-1, LANES)


def _unpack(buf, shapes):
    flat = buf.reshape(-1)
    out, off = [], 0
    for s in shapes:
        n = int(np.prod(s))
        out.append(flat[off:off + n].reshape(s))
        off += n
    return out


def _causal_dwconv(u, w):
    k_taps = w.shape[0]
    length = u.shape[0]
    up = jnp.pad(u, ((k_taps - 1, 0), (0, 0)))
    out = up[0:length] * w[0]
    for i in range(1, k_taps):
        out = out + up[i:i + length] * w[i]
    return out


def _short_conv_mixer(p, conv_w, valid):
    b_gate, c_gate, xa = p[:, :D_MODEL], p[:, D_MODEL:2 * D_MODEL], p[:, 2 * D_MODEL:]
    u = c_gate * xa * valid[:, None].astype(F32)
    return b_gate * _causal_dwconv(u, conv_w)


def _ssd_mixer(p, dt_raw, conv_w, conv_b, dt_bias, a_log, d_skip, norm_w, valid):
    z, xbc = p[:, :SSD_INNER], p[:, SSD_INNER:]
    dt_raw = dt_raw[:, :SSD_HEADS]
    length = z.shape[0]
    nc = length // CHUNK
    hpg = SSD_HEADS // SSD_GROUPS
    vm = valid[:, None].astype(F32)
    xbc = jax.nn.silu(_causal_dwconv(xbc * vm, conv_w) + conv_b)
    xs = xbc[:, :SSD_INNER]
    bm = xbc[:, SSD_INNER:SSD_INNER + SSD_GROUPS * SSD_STATE]
    cm = xbc[:, SSD_INNER + SSD_GROUPS * SSD_STATE:]
    xs = (xs * vm).reshape(nc, CHUNK, SSD_GROUPS, hpg, SSD_HEAD_DIM)
    bc = bm.reshape(nc, CHUNK, SSD_GROUPS, SSD_STATE)
    cc = cm.reshape(nc, CHUNK, SSD_GROUPS, SSD_STATE)
    dt = jax.nn.softplus(dt_raw + dt_bias)
    a = (-jnp.exp(a_log) * dt).reshape(nc, CHUNK, SSD_GROUPS, hpg)
    xdt = xs * dt.reshape(nc, CHUNK, SSD_GROUPS, hpg)[..., None]
    acs = jnp.moveaxis(jnp.cumsum(a, axis=1), 1, -1)
    causal = jnp.tril(jnp.ones((CHUNK, CHUNK), dtype=bool))
    seg = jnp.exp(jnp.where(causal, acs[..., :, None] - acs[..., None, :], -jnp.inf))
    cb = jnp.einsum('clgn,csgn->cgls', cc, bc)
    y_diag = jnp.einsum('cgjls,csgjp->clgjp', cb[:, :, None] * seg, xdt)
    decay_states = jnp.exp(acs[..., -1:] - acs)
    states = jnp.einsum('clgn,cgjl,clgjp->cgjpn', bc, decay_states, xdt)
    chunk_decay = jnp.exp(acs[..., -1])

    def step(hstate, inp):
        st, dec = inp
        return hstate * dec[..., None, None] + st, hstate

    h0 = jnp.zeros((SSD_GROUPS, hpg, SSD_HEAD_DIM, SSD_STATE), F32)
    _, prev = lax.scan(step, h0, (states, chunk_decay))
    y_off = jnp.einsum('clgn,cgjpn,cgjl->clgjp', cc, prev, jnp.exp(acs))
    y = y_diag + y_off + xs * d_skip.reshape(SSD_GROUPS, hpg)[..., None]
    y = y.reshape(length, SSD_INNER) * jax.nn.silu(z)
    yg = y.reshape(length, SSD_GROUPS, SSD_INNER // SSD_GROUPS)
    yg = yg * lax.rsqrt(jnp.mean(yg * yg, axis=-1, keepdims=True) + EPS)
    return yg.reshape(length, SSD_INNER) * norm_w


def _rotate(x, pos):
    half = x.shape[-1] // 2
    inv = ROPE_BASE ** (-jnp.arange(half, dtype=F32) / half)
    ang = pos.astype(F32)[:, None] * inv[None, :]
    cos = jnp.cos(ang)[:, None, :]
    sin = jnp.sin(ang)[:, None, :]
    x1, x2 = x[..., :half], x[..., half:]
    return jnp.concatenate([x1 * cos - x2 * sin, x1 * sin + x2 * cos], axis=-1)


def _retention_mixer(p, valid):
    w = RET_HEADS * RET_QK_DIM
    q, k, v, g = p[:, :w], p[:, w:2 * w], p[:, 2 * w:2 * w + RET_WIDTH], p[:, 2 * w + RET_WIDTH:]
    length = q.shape[0]
    nc = length // CHUNK
    pos = jnp.arange(length)
    qr = _rotate(q.reshape(length, RET_HEADS, RET_QK_DIM), pos)
    kr = _rotate(k.reshape(length, RET_HEADS, RET_QK_DIM), pos) * (RET_QK_DIM ** -0.5)
    vr = v.reshape(length, RET_HEADS, RET_V_DIM) * valid.astype(F32)[:, None, None]
    log_gamma = jnp.log(1.0 - jnp.power(2.0, -5.0 - jnp.arange(RET_HEADS, dtype=F32)))
    idx = jnp.arange(CHUNK, dtype=F32)
    rel = idx[:, None] - idx[None, :]
    dmask = jnp.where(rel >= 0, jnp.exp(log_gamma[:, None, None] * jnp.maximum(rel, 0.0)), 0.0)
    qc = qr.reshape(nc, CHUNK, RET_HEADS, RET_QK_DIM)
    kc = kr.reshape(nc, CHUNK, RET_HEADS, RET_QK_DIM)
    vc = vr.reshape(nc, CHUNK, RET_HEADS, RET_V_DIM)
    scores = jnp.einsum('clhd,cshd->chls', qc, kc) * dmask
    y_in = jnp.einsum('chls,cshe->clhe', scores, vc)
    k_decay = jnp.exp(log_gamma[:, None] * (CHUNK - 1 - idx)[None, :])
    kv = jnp.einsum('cshd,hs,cshe->chde', kc, k_decay, vc)
    chunk_decay = jnp.exp(log_gamma * CHUNK)

    def step(r, kv_c):
        return r * chunk_decay[:, None, None] + kv_c, r

    r0 = jnp.zeros((RET_HEADS, RET_QK_DIM, RET_V_DIM), F32)
    _, prev = lax.scan(step, r0, kv)
    q_decay = jnp.exp(log_gamma[None, :] * (idx + 1.0)[:, None])
    y_cr = jnp.einsum('clhd,chde->clhe', qc, prev) * q_decay[:, :, None]
    y = (y_in + y_cr).reshape(length, RET_HEADS, RET_V_DIM)
    mu = jnp.mean(y, axis=-1, keepdims=True)
    var = jnp.mean(jnp.square(y - mu), axis=-1, keepdims=True)
    y = ((y - mu) * lax.rsqrt(var + EPS)).reshape(length, RET_WIDTH)
    return y * jax.nn.silu(g)


def _merge(gate_logits, up0, up1, up2, up3):
    gates = jax.nn.sigmoid(gate_logits)
    out = gates[:, :D_MODEL] * up0
    for n, up in enumerate((up1, up2, up3), start=1):
        out = out + gates[:, n * D_MODEL:(n + 1) * D_MODEL] * up
    return out


def _swiglu(f):
    return jax.nn.silu(f[:, :D_FF]) * f[:, D_FF:]


IN_A = (0, 3 * D_MODEL)
IN_S = (IN_A[1], IN_A[1] + SSD_INNER + SSD_CONV_DIM)
IN_DT = (IN_S[1], IN_S[1] + SSD_HEADS)
IN_R = (IN_DT[1], IN_DT[1] + 4 * D_MODEL)
IN_SB = (IN_R[1], IN_R[1] + 3 * D_MODEL)
IN_G = (IN_SB[1], IN_SB[1] + N_BRANCH * D_MODEL)
IN_WIDTH = IN_G[1]


def _layer_weights(full, small, l):
    w_in = full['w_in'][l]
    cut = lambda r: w_in[:, r[0]:r[1]]
    w_dt = jnp.pad(cut(IN_DT), ((0, 0), (0, DT_PAD - SSD_HEADS)))
    return dict(
        w_a=cut(IN_A), w_s=cut(IN_S), w_dt=w_dt, w_r=cut(IN_R), w_sb=cut(IN_SB), w_g=cut(IN_G),
        w_branch=[full['w_branch'][l, n] for n in range(N_BRANCH)],
        w_out=full['w_out'][l], w_ffn_in=full['w_ffn_in'][l], w_ffn_out=full['w_ffn_out'][l],
        conv_a=full['conv_a'][l], ssd_conv_w=full['ssd_conv_w'][l],
        ssd_conv_b=small['ssd_conv_b'][l], ssd_dt_bias=small['ssd_dt_bias'][l], ssd_a_log=small['ssd_a_log'][l],
        ssd_d=small['ssd_d'][l], ssd_norm=small['ssd_norm'][l],
        n_mix_pre=small['norm_mix_pre'][l][None], n_mix_post=small['norm_mix_post'][l][None],
        n_ffn_pre=small['norm_ffn_pre'][l][None], n_ffn_post=small['norm_ffn_post'][l][None],
    )


def _layer_fwd(h_res, lw, valid):
    s = {'h_res': h_res}
    hn = rms_fwd(h_res, lw['n_mix_pre'], name="rms_mix_pre")
    s['hn'] = hn
    p_a = mm_nn(hn, lw['w_a'], name="proj_conv")
    p_s = mm_nn(hn, lw['w_s'], name="proj_ssd")
    p_dt = mm_nn(hn, lw['w_dt'], name="proj_dt")
    p_r = mm_nn(hn, lw['w_r'], name="proj_ret")
    p_sb = mm_nn(hn, lw['w_sb'], out_dtype=BF16, name="proj_sb")
    p_g = mm_nn(hn, lw['w_g'], name="proj_gate")
    y_a, s['vjp_a'] = jax.vjp(lambda p, cw: _short_conv_mixer(p, cw, valid), p_a, lw['conv_a'])
    y_b, s['vjp_b'] = jax.vjp(
        lambda p, dt, cw, cb, db, al, ds, nw: _ssd_mixer(p, dt, cw, cb, db, al, ds, nw, valid),
        p_s, p_dt, lw['ssd_conv_w'], lw['ssd_conv_b'], lw['ssd_dt_bias'], lw['ssd_a_log'], lw['ssd_d'],
        lw['ssd_norm'])
    y_c, s['vjp_c'] = jax.vjp(lambda p: _retention_mixer(p, valid), p_r)
    y_d, s['sb_total'] = sb_fwd(p_sb)
    s['p_sb'] = p_sb
    ys = [y_a, y_b, y_c, y_d]
    s['ys'] = ys
    ups = [mm_nn(ys[n], lw['w_branch'][n], name="branch_up") for n in range(N_BRANCH)]
    merged, s['vjp_m'] = jax.vjp(_merge, p_g, *ups)
    s['merged'] = merged
    mix = mm_nn(merged, lw['w_out'], name="mix_out")
    s['mix'] = mix
    h2 = rms_fwd(mix, lw['n_mix_post'], res=h_res, name="rms_mix_post")
    s['h2'] = h2
    hf = rms_fwd(h2, lw['n_ffn_pre'], name="rms_ffn_pre")
    s['hf'] = hf
    f = mm_nn(hf, lw['w_ffn_in'], name="ffn_in")
    act, s['vjp_act'] = jax.vjp(_swiglu, f)
    s['act'] = act
    fo = mm_nn(act, lw['w_ffn_out'], name="ffn_out")
    s['fo'] = fo
    return rms_fwd(fo, lw['n_ffn_post'], res=h2, name="rms_ffn_post"), s


def _layer_bwd(dh3, lw, s):
    g = {}
    d_fo, g['norm_ffn_post'] = rms_bwd(s['fo'], lw['n_ffn_post'], dh3, name="rms_ffn_post_bwd")
    d_act = mm_nt(d_fo, lw['w_ffn_out'], name="ffn_out_dx")
    g['w_ffn_out'] = mm_tn(s['act'], d_fo, name="ffn_out_dw")
    (df,) = s['vjp_act'](d_act)
    d_hf = mm_nt(df, lw['w_ffn_in'], name="ffn_in_dx")
    g['w_ffn_in'] = mm_tn(s['hf'], df, name="ffn_in_dw")
    dh2, g['norm_ffn_pre'] = rms_bwd(s['h2'], lw['n_ffn_pre'], d_hf, add=dh3, name="rms_ffn_pre_bwd")
    d_mix, g['norm_mix_post'] = rms_bwd(s['mix'], lw['n_mix_post'], dh2, name="rms_mix_post_bwd")
    d_merged = mm_nt(d_mix, lw['w_out'], name="mix_out_dx")
    g['w_out'] = mm_tn(s['merged'], d_mix, name="mix_out_dw")
    dp_g, *dups = s['vjp_m'](d_merged)
    dys = [mm_nt(dups[n], lw['w_branch'][n], name="branch_dx") for n in range(N_BRANCH)]
    g['w_branch'] = jnp.stack([mm_tn(s['ys'][n], dups[n], name="branch_dw") for n in range(N_BRANCH)])
    dp_a, g['conv_a'] = s['vjp_a'](dys[0])
    (dp_s, dp_dt, g['ssd_conv_w'], g['ssd_conv_b'], g['ssd_dt_bias'], g['ssd_a_log'], g['ssd_d'],
     g['ssd_norm']) = s['vjp_b'](dys[1])
    (dp_r,) = s['vjp_c'](dys[2])
    dq, dk, dv = sb_bwd(s['p_sb'], s['sb_total'], dys[3])
    dp_sb = jnp.concatenate([dq, dk.astype(BF16), dv.astype(BF16)], axis=1)
    hn = s['hn']
    d_hn = None
    dws = []
    for dp, w, nm in ((dp_a, lw['w_a'], "conv"), (dp_s, lw['w_s'], "ssd"), (dp_dt, lw['w_dt'], "dt"),
                      (dp_r, lw['w_r'], "ret"), (dp_sb, lw['w_sb'], "sb"), (dp_g, lw['w_g'], "gate")):
        d_hn = mm_nt(dp, w, acc=d_hn, name="proj_dx")
        dws.append(mm_tn(hn, dp, name="proj_dw"))
    dws[2] = dws[2][:, :SSD_HEADS]
    g['w_in'] = jnp.concatenate(dws, axis=1)
    dh_res, g['norm_mix_pre'] = rms_bwd(s['h_res'], lw['n_mix_pre'], d_hn, add=dh2, name="rms_mix_pre_bwd")
    for k in ('norm_ffn_post', 'norm_ffn_pre', 'norm_mix_post', 'norm_mix_pre'):
        g[k] = g[k][0]
    return dh_res, g


def _quarter(a, axis, j):
    n = a.shape[axis] // N_CHIPS
    return lax.slice_in_dim(a, j * n, (j + 1) * n, axis=axis)


def kernel(x, meta, w_in, conv_a, ssd_conv_w, ssd_conv_b, ssd_dt_bias, ssd_a_log, ssd_d, ssd_norm, w_branch, w_out, w_ffn_in, w_ffn_out, norm_mix_pre, norm_mix_post, norm_ffn_pre, norm_ffn_post, loss_target, m_meta, m_w_in, m_conv_a, m_ssd_conv_w, m_ssd_conv_b, m_ssd_dt_bias, m_ssd_a_log, m_ssd_d, m_ssd_norm, m_w_branch, m_w_out, m_w_ffn_in, m_w_ffn_out, m_norm_mix_pre, m_norm_mix_post, m_norm_ffn_pre, m_norm_ffn_post, v_meta, v_w_in, v_conv_a, v_ssd_conv_w, v_ssd_conv_b, v_ssd_dt_bias, v_ssd_a_log, v_ssd_d, v_ssd_norm, v_w_branch, v_w_out, v_w_ffn_in, v_w_ffn_out, v_norm_mix_pre, v_norm_mix_post, v_norm_ffn_pre, v_norm_ffn_post):
    w_loc = dict(meta=meta, w_in=w_in, conv_a=conv_a, ssd_conv_w=ssd_conv_w, ssd_conv_b=ssd_conv_b,
                 ssd_dt_bias=ssd_dt_bias, ssd_a_log=ssd_a_log, ssd_d=ssd_d, ssd_norm=ssd_norm, w_branch=w_branch,
                 w_out=w_out, w_ffn_in=w_ffn_in, w_ffn_out=w_ffn_out, norm_mix_pre=norm_mix_pre,
                 norm_mix_post=norm_mix_post, norm_ffn_pre=norm_ffn_pre, norm_ffn_post=norm_ffn_post)
    m_loc = dict(meta=m_meta, w_in=m_w_in, conv_a=m_conv_a, ssd_conv_w=m_ssd_conv_w, ssd_conv_b=m_ssd_conv_b,
                 ssd_dt_bias=m_ssd_dt_bias, ssd_a_log=m_ssd_a_log, ssd_d=m_ssd_d, ssd_norm=m_ssd_norm,
                 w_branch=m_w_branch, w_out=m_w_out, w_ffn_in=m_w_ffn_in, w_ffn_out=m_w_ffn_out,
                 norm_mix_pre=m_norm_mix_pre, norm_mix_post=m_norm_mix_post, norm_ffn_pre=m_norm_ffn_pre,
                 norm_ffn_post=m_norm_ffn_post)
    v_loc = dict(meta=v_meta, w_in=v_w_in, conv_a=v_conv_a, ssd_conv_w=v_ssd_conv_w, ssd_conv_b=v_ssd_conv_b,
                 ssd_dt_bias=v_ssd_dt_bias, ssd_a_log=v_ssd_a_log, ssd_d=v_ssd_d, ssd_norm=v_ssd_norm,
                 w_branch=v_w_branch, w_out=v_w_out, w_ffn_in=v_w_ffn_in, w_ffn_out=v_w_ffn_out,
                 norm_mix_pre=v_norm_mix_pre, norm_mix_post=v_norm_mix_post, norm_ffn_pre=v_norm_ffn_pre,
                 norm_ffn_post=v_norm_ffn_post)

    gb, gf = gather_weight_shards(_pack([w_loc[n] for n in MATMUL_WEIGHTS], BF16, 16),
                                  _pack([w_loc[n] for n in SMALL_SHARDED], F32, 8))
    full = {}
    parts_b = [_unpack(gb[j], [w_loc[n].shape for n in MATMUL_WEIGHTS]) for j in range(N_CHIPS)]
    parts_f = [_unpack(gf[j], [w_loc[n].shape for n in SMALL_SHARDED]) for j in range(N_CHIPS)]
    for t, n in enumerate(MATMUL_WEIGHTS):
        full[n] = jnp.concatenate([parts_b[j][t] for j in range(N_CHIPS)], axis=SHARD_AXIS[n])
    for t, n in enumerate(SMALL_SHARDED):
        full[n] = jnp.concatenate([parts_f[j][t] for j in range(N_CHIPS)], axis=SHARD_AXIS[n])

    xs = x[0]
    seq = xs.shape[0]
    length = CHUNK + seq
    valid = jnp.arange(length) >= PAD
    h = jnp.concatenate([jnp.zeros((PAD, D_MODEL), F32), full['meta'], xs], axis=0)
    lws, saved = [], []
    for l in range(DEPTH):
        lw = _layer_weights(full, w_loc, l)
        h, s = _layer_fwd(h, lw, valid)
        lws.append(lw)
        saved.append(s)

    loss_row, dh = loss_head(h, loss_target[0])
    loss = lax.psum(loss_row[0, 0], ("x", "y", "c"))

    layer_grads = [None] * DEPTH
    for l in reversed(range(DEPTH)):
        dh, layer_grads[l] = _layer_bwd(dh, lws[l], saved[l])
    grad_x = dh[CHUNK:][None]
    grads = {n: jnp.stack([layer_grads[l][n] for l in range(DEPTH)]) for n in WEIGHTS if n != 'meta'}
    grads['meta'] = dh[PAD:CHUNK]

    def grad_pieces(j):
        return [_quarter(grads[n], SHARD_AXIS[n], j) if n in SHARD_AXIS else grads[n] for n in PACK_ORDER]

    gpack = jnp.stack([_pack(grad_pieces(j), F32, ADAM_TILE_ROWS) for j in range(N_CHIPS)])
    slots = exchange_grad_quarters(gpack)
    packed = reduce_adamw(slots, *[_pack([d[n] for n in PACK_ORDER], F32, ADAM_TILE_ROWS)
                                   for d in (w_loc, m_loc, v_loc)])
    shapes = [w_loc[n].shape for n in PACK_ORDER]
    outs = []
    for buf in packed:
        pieces = dict(zip(PACK_ORDER, _unpack(buf, shapes)))
        outs.extend(pieces[n] for n in WEIGHTS)
    return (loss, grad_x, *outs)
```

```python
import functools
import math

import numpy as np
import jax
import jax.numpy as jnp
from jax import lax
from jax.experimental import pallas as pl
from jax.experimental.pallas import tpu as pltpu

F32 = jnp.float32
BF16 = jnp.bfloat16

D_MODEL = 1024
DEPTH = 2
N_META = 16
CHUNK = 128
PAD = CHUNK - N_META
EPS = 1e-6

CONV_A_K = 3
SSD_HEAD_DIM = 64
SSD_HEADS = 16
SSD_INNER = 1024
SSD_GROUPS = 4
SSD_STATE = 128
SSD_CONV_K = 4
SSD_CONV_DIM = SSD_INNER + 2 * SSD_GROUPS * SSD_STATE
RET_HEADS = 4
RET_QK_DIM = 256
RET_V_DIM = 256
RET_WIDTH = 1024
ROPE_BASE = 10000.0
SB_HEADS = 8
SB_HEAD_DIM = 128
N_BRANCH = 4
D_FF = 2816
DT_PAD = 128

ADAM_LR = 0.001
ADAM_B1 = 0.9
ADAM_B2 = 0.999
ADAM_EPS = 1e-08
ADAM_WD = 0.01
ADAM_STEP = 10

N_CHIPS = 4
N_DEV = 8
LANES = 128
VMEM_LIMIT = 56 * 1024 * 1024
MESH = pl.DeviceIdType.MESH

WEIGHTS = ['meta', 'w_in', 'conv_a', 'ssd_conv_w', 'ssd_conv_b', 'ssd_dt_bias', 'ssd_a_log', 'ssd_d',
           'ssd_norm', 'w_branch', 'w_out', 'w_ffn_in', 'w_ffn_out', 'norm_mix_pre', 'norm_mix_post',
           'norm_ffn_pre', 'norm_ffn_post']
SHARD_AXIS = {'meta': 1, 'w_in': 2, 'conv_a': 2, 'ssd_conv_w': 2, 'w_branch': 2, 'w_out': 1,
              'w_ffn_in': 2, 'w_ffn_out': 1}
MATMUL_WEIGHTS = ['w_in', 'w_branch', 'w_out', 'w_ffn_in', 'w_ffn_out']
SMALL_SHARDED = ['meta', 'conv_a', 'ssd_conv_w']
PACK_ORDER = MATMUL_WEIGHTS + SMALL_SHARDED + [n for n in WEIGHTS if n not in SHARD_AXIS]
ADAM_TILE_ROWS = 1024


def _params(**kw):
    return pltpu.CompilerParams(vmem_limit_bytes=VMEM_LIMIT, **kw)


def _tile(n, prefs):
    for p in prefs:
        if n % p == 0:
            return p
    return n


def mm_nn(a, b, out_dtype=F32, name="mm_nn"):
    m, k = a.shape
    n = b.shape[1]
    tm = _tile(m, (384, 256, 128))
    tn = _tile(n, (1024, 512, 256, 128))

    def body(a_ref, b_ref, o_ref):
        o_ref[...] = jnp.dot(a_ref[...].astype(BF16), b_ref[...].astype(BF16),
                             preferred_element_type=F32).astype(o_ref.dtype)

    return pl.pallas_call(
        body, name=name, grid=(m // tm, n // tn),
        in_specs=[pl.BlockSpec((tm, k), lambda i, j: (i, 0)), pl.BlockSpec((k, tn), lambda i, j: (0, j))],
        out_specs=pl.BlockSpec((tm, tn), lambda i, j: (i, j)),
        out_shape=jax.ShapeDtypeStruct((m, n), out_dtype),
        compiler_params=_params(dimension_semantics=("arbitrary", "arbitrary")),
    )(a, b)


def mm_nt(g, w, acc=None, name="mm_nt"):
    m, n = g.shape
    k = w.shape[0]
    tm = _tile(m, (384, 256, 128))
    tn = _tile(n, (1024, 512, 256, 128))
    has_acc = acc is not None

    def body(*refs):
        if has_acc:
            g_ref, w_ref, acc_ref, o_ref = refs
        else:
            g_ref, w_ref, o_ref = refs
        j = pl.program_id(1)

        @pl.when(j == 0)
        def _():
            o_ref[...] = acc_ref[...] if has_acc else jnp.zeros_like(o_ref)

        o_ref[...] += lax.dot_general(g_ref[...].astype(BF16), w_ref[...].astype(BF16),
                                      (((1,), (1,)), ((), ())), preferred_element_type=F32)

    in_specs = [pl.BlockSpec((tm, tn), lambda i, j: (i, j)), pl.BlockSpec((k, tn), lambda i, j: (0, j))]
    args = [g, w]
    if has_acc:
        in_specs.append(pl.BlockSpec((tm, k), lambda i, j: (i, 0)))
        args.append(acc)
    return pl.pallas_call(
        body, name=name, grid=(m // tm, n // tn),
        in_specs=in_specs,
        out_specs=pl.BlockSpec((tm, k), lambda i, j: (i, 0)),
        out_shape=jax.ShapeDtypeStruct((m, k), F32),
        compiler_params=_params(dimension_semantics=("arbitrary", "arbitrary")),
    )(*args)


def mm_tn(x, g, name="mm_tn"):
    m, k = x.shape
    n = g.shape[1]
    tm = _tile(m, (384, 256, 128))
    tk = _tile(k, (1024, 1408, 512, 256, 128))
    tn = _tile(n, (1024, 512, 256, 128))

    def body(x_ref, g_ref, o_ref):
        s = pl.program_id(2)

        @pl.when(s == 0)
        def _():
            o_ref[...] = jnp.zeros_like(o_ref)

        o_ref[...] += lax.dot_general(x_ref[...].astype(BF16), g_ref[...].astype(BF16),
                                      (((0,), (0,)), ((), ())), preferred_element_type=F32)

    return pl.pallas_call(
        body, name=name, grid=(k // tk, n // tn, m // tm),
        in_specs=[pl.BlockSpec((tm, tk), lambda a, b, s: (s, a)), pl.BlockSpec((tm, tn), lambda a, b, s: (s, b))],
        out_specs=pl.BlockSpec((tk, tn), lambda a, b, s: (a, b)),
        out_shape=jax.ShapeDtypeStruct((k, n), F32),
        compiler_params=_params(dimension_semantics=("arbitrary", "arbitrary", "arbitrary")),
    )(x, g)


def rms_fwd(x, w, res=None, name="rms_fwd"):
    m, d = x.shape
    tm = _tile(m, (384, 256, 128))
    has_res = res is not None

    def body(*refs):
        if has_res:
            x_ref, w_ref, r_ref, o_ref = refs
        else:
            x_ref, w_ref, o_ref = refs
        xv = x_ref[...]
        y = xv * lax.rsqrt(jnp.mean(xv * xv, axis=-1, keepdims=True) + EPS) * w_ref[...]
        o_ref[...] = y + r_ref[...] if has_res else y

    row = pl.BlockSpec((tm, d), lambda i: (i, 0))
    in_specs = [row, pl.BlockSpec((1, d), lambda i: (0, 0))]
    args = [x, w]
    if has_res:
        in_specs.append(row)
        args.append(res)
    return pl.pallas_call(
        body, name=name, grid=(m // tm,), in_specs=in_specs, out_specs=row,
        out_shape=jax.ShapeDtypeStruct((m, d), F32),
        compiler_params=_params(dimension_semantics=("arbitrary",)),
    )(*args)


def rms_bwd(x, w, dy, add=None, name="rms_bwd"):
    m, d = x.shape
    tm = _tile(m, (384, 256, 128))
    has_add = add is not None

    def body(*refs):
        if has_add:
            x_ref, w_ref, dy_ref, add_ref, dx_ref, dw_ref = refs
        else:
            x_ref, w_ref, dy_ref, dx_ref, dw_ref = refs
        i = pl.program_id(0)
        xv = x_ref[...]
        dyv = dy_ref[...]
        r = lax.rsqrt(jnp.mean(xv * xv, axis=-1, keepdims=True) + EPS)
        xh = xv * r
        dxh = dyv * w_ref[...]
        dx = r * (dxh - xh * jnp.mean(dxh * xh, axis=-1, keepdims=True))
        dx_ref[...] = dx + add_ref[...] if has_add else dx

        @pl.when(i == 0)
        def _():
            dw_ref[...] = jnp.zeros_like(dw_ref)

        dw_ref[...] += jnp.sum(dyv * xh, axis=0, keepdims=True)

    row = pl.BlockSpec((tm, d), lambda i: (i, 0))
    vec = pl.BlockSpec((1, d), lambda i: (0, 0))
    in_specs = [row, vec, row]
    args = [x, w, dy]
    if has_add:
        in_specs.append(row)
        args.append(add)
    return pl.pallas_call(
        body, name=name, grid=(m // tm,), in_specs=in_specs, out_specs=[row, vec],
        out_shape=[jax.ShapeDtypeStruct((m, d), F32), jax.ShapeDtypeStruct((1, d), F32)],
        compiler_params=_params(dimension_semantics=("arbitrary",)),
    )(*args)


def loss_head(h, target):
    l, d = h.shape
    nblk = l // CHUNK

    def body(h_ref, t_ref, loss_ref, dh_ref, acc_ref):
        i = pl.program_id(0)

        @pl.when(i == 0)
        def _():
            acc_ref[...] = jnp.zeros_like(acc_ref)
            dh_ref[...] = jnp.zeros_like(dh_ref)

        @pl.when(i > 0)
        def _():
            e = h_ref[...] - t_ref[...]
            dh_ref[...] = e / d
            acc_ref[...] += jnp.sum(e * e, axis=0, keepdims=True)

        @pl.when(i == nblk - 1)
        def _():
            loss_ref[...] = jnp.zeros_like(loss_ref) + 0.5 * jnp.sum(acc_ref[...]) / d

    return pl.pallas_call(
        body, name="loss_head", grid=(nblk,),
        in_specs=[pl.BlockSpec((CHUNK, d), lambda i: (i, 0)),
                  pl.BlockSpec((CHUNK, d), lambda i: (jnp.maximum(i - 1, 0), 0))],
        out_specs=[pl.BlockSpec((1, LANES), lambda i: (0, 0)), pl.BlockSpec((CHUNK, d), lambda i: (i, 0))],
        out_shape=[jax.ShapeDtypeStruct((1, LANES), F32), jax.ShapeDtypeStruct((l, d), F32)],
        scratch_shapes=[pltpu.VMEM((1, d), F32)],
        compiler_params=_params(dimension_semantics=("arbitrary",)),
    )(h, target)


SB_BLK = 128


def _sb_tile(l):
    return _tile(l, (384, 256, 128))


def _sb_tri(strict_later):
    r = lax.broadcasted_iota(jnp.int32, (2 * SB_BLK, 2 * SB_BLK), 0) & (SB_BLK - 1)
    c = lax.broadcasted_iota(jnp.int32, (2 * SB_BLK, 2 * SB_BLK), 1)
    keep = (r > c) if strict_later else (r < c)
    return jnp.where(keep | (c >= SB_BLK), 1.0, 0.0).astype(BF16)


def _sb_mask(i, j, t):
    qpos = i * t + lax.broadcasted_iota(jnp.int32, (t, t), 0)
    kpos = j * t + lax.broadcasted_iota(jnp.int32, (t, t), 1)
    return (kpos < qpos) & (kpos >= PAD)


def _sb_scores(q, k, scale, mask):
    z = lax.dot_general(q, k, (((1,), (1,)), ((), ())), preferred_element_type=F32) * scale
    sp = jnp.maximum(z, 0.0) + jnp.log(1.0 + jnp.exp(-jnp.abs(z)))
    lneg = -sp if mask is None else jnp.where(mask, -sp, 0.0)
    return z - sp, lneg


def _sb_block_sums(x, tri, two_parts):
    hi = x.astype(BF16)
    if two_parts:
        lo = (x - hi.astype(F32)).astype(BF16)
        s = jnp.dot(jnp.concatenate([hi, lo], axis=1), tri, preferred_element_type=F32)
    else:
        s = jnp.dot(hi, tri[:SB_BLK], preferred_element_type=F32)
    return s[:, :SB_BLK], s[:, SB_BLK:]


def _sb_walk_down(i, step, carry):
    carry = step(i, carry, True)
    carry = lax.fori_loop(0, jnp.maximum(i - 1, 0), lambda t, c: step(i - 1 - t, c, False), carry)
    return lax.fori_loop(0, jnp.minimum(i, 1), lambda t, c: step(0, c, True), carry)


def _sb_walk_up(i, step, carry):
    carry = lax.fori_loop(0, jnp.minimum(i, 1), lambda t, c: step(0, c, True), carry)
    carry = lax.fori_loop(1, jnp.maximum(i, 1), lambda j, c: step(j, c, False), carry)
    return step(i, carry, True)


def sb_fwd(qkv):
    l = qkv.shape[0]
    t = _sb_tile(l)
    nb = t // SB_BLK
    scale = SB_HEAD_DIM ** -0.5

    def body(q_ref, k_ref, v_ref, o_ref, tot_ref):
        i = pl.program_id(1)
        q = q_ref[...]
        tri = _sb_tri(True)

        def step(j, carry, masked):
            later, acc = carry
            rows = pl.ds(pl.multiple_of(j * t, t), t)
            mask = _sb_mask(i, j, t) if masked else None
            lpos, lneg = _sb_scores(q, k_ref[rows, :], scale, mask)
            ws = [None] * nb
            for b in reversed(range(nb)):
                cols = slice(b * SB_BLK, (b + 1) * SB_BLK)
                within, total = _sb_block_sums(lneg[:, cols], tri, True)
                ws[b] = jnp.exp(lpos[:, cols] + within + later)
                later = later + total
            w = jnp.concatenate(ws, axis=1)
            if masked:
                w = jnp.where(mask, w, 0.0)
            acc = acc + jnp.dot(w.astype(BF16), v_ref[rows, :], preferred_element_type=F32)
            return later, acc

        carry = (jnp.zeros((t, SB_BLK), F32), jnp.zeros((t, SB_HEAD_DIM), F32))
        later, acc = _sb_walk_down(i, step, carry)
        o_ref[...] = acc
        tot_ref[...] = later[:, :1]

    return pl.pallas_call(
        body, name="sb_fwd", grid=(SB_HEADS, l // t),
        in_specs=[pl.BlockSpec((t, SB_HEAD_DIM), lambda h, i: (i, h)),
                  pl.BlockSpec((l, SB_HEAD_DIM), lambda h, i: (0, SB_HEADS + h)),
                  pl.BlockSpec((l, SB_HEAD_DIM), lambda h, i: (0, 2 * SB_HEADS + h))],
        out_specs=[pl.BlockSpec((t, SB_HEAD_DIM), lambda h, i: (i, h)),
                   pl.BlockSpec((None, t, 1), lambda h, i: (h, i, 0))],
        out_shape=[jax.ShapeDtypeStruct((l, D_MODEL), F32), jax.ShapeDtypeStruct((SB_HEADS, l, 1), F32)],
        compiler_params=_params(dimension_semantics=("arbitrary", "arbitrary")),
    )(qkv, qkv, qkv)


def sb_bwd(qkv, row_total, dout):
    l = qkv.shape[0]
    t = _sb_tile(l)
    nb = t // SB_BLK
    nq = l // t
    scale = SB_HEAD_DIM ** -0.5

    def body(q_ref, k_ref, v_ref, tot_ref, do_ref, dq_ref, dk_hbm, dv_hbm, dk_acc, dv_acc):
        h = pl.program_id(0)
        i = pl.program_id(1)

        @pl.when(i == 0)
        def _():
            dk_acc[...] = jnp.zeros_like(dk_acc)
            dv_acc[...] = jnp.zeros_like(dv_acc)

        q = q_ref[...]
        dob = do_ref[...].astype(BF16)
        tri_later = _sb_tri(True)
        tri_before = _sb_tri(False)

        def step(j, carry, masked):
            later, g_before, dq = carry
            rows = pl.ds(pl.multiple_of(j * t, t), t)
            k = k_ref[rows, :]
            v = v_ref[rows, :]
            mask = _sb_mask(i, j, t) if masked else None
            lpos, lneg = _sb_scores(q, k, scale, mask)
            dw = lax.dot_general(dob, v, (((1,), (1,)), ((), ())), preferred_element_type=F32)
            ws, dzs = [None] * nb, [None] * nb
            for b in range(nb):
                cols = slice(b * SB_BLK, (b + 1) * SB_BLK)
                within, total = _sb_block_sums(lneg[:, cols], tri_later, True)
                later = later - total
                wb = jnp.exp(lpos[:, cols] + within + later)
                if masked:
                    wb = jnp.where(mask[:, cols], wb, 0.0)
                g = dw[:, cols] * wb
                g_within, g_total = _sb_block_sums(g, tri_before, False)
                dz = g - (g + g_before + g_within) * jnp.exp(lpos[:, cols])
                if masked:
                    dz = jnp.where(mask[:, cols], dz, 0.0)
                g_before = g_before + g_total
                ws[b] = wb.astype(BF16)
                dzs[b] = (dz * scale).astype(BF16)
            w = jnp.concatenate(ws, axis=1)
            dzb = jnp.concatenate(dzs, axis=1)
            dq = dq + jnp.dot(dzb, k, preferred_element_type=F32)
            dk_acc[rows, :] += lax.dot_general(dzb, q, (((0,), (0,)), ((), ())), preferred_element_type=F32)
            dv_acc[rows, :] += lax.dot_general(w, dob, (((0,), (0,)), ((), ())), preferred_element_type=F32)
            return later, g_before, dq

        carry = (jnp.broadcast_to(tot_ref[...], (t, SB_BLK)), jnp.zeros((t, SB_BLK), F32),
                 jnp.zeros((t, SB_HEAD_DIM), F32))
        _, _, dq = _sb_walk_up(i, step, carry)
        dq_ref[...] = dq.astype(dq_ref.dtype)

        @pl.when(i == nq - 1)
        def _():
            cols = pl.ds(pl.multiple_of(h * SB_HEAD_DIM, SB_HEAD_DIM), SB_HEAD_DIM)
            pltpu.sync_copy(dk_acc, dk_hbm.at[:, cols])
            pltpu.sync_copy(dv_acc, dv_hbm.at[:, cols])

    blk = lambda h, i: (i, h)
    return pl.pallas_call(
        body, name="sb_bwd", grid=(SB_HEADS, nq),
        in_specs=[pl.BlockSpec((t, SB_HEAD_DIM), blk),
                  pl.BlockSpec((l, SB_HEAD_DIM), lambda h, i: (0, SB_HEADS + h)),
                  pl.BlockSpec((l, SB_HEAD_DIM), lambda h, i: (0, 2 * SB_HEADS + h)),
                  pl.BlockSpec((None, t, 1), lambda h, i: (h, i, 0)), pl.BlockSpec((t, SB_HEAD_DIM), blk)],
        out_specs=[pl.BlockSpec((t, SB_HEAD_DIM), blk), pl.BlockSpec(memory_space=pl.ANY),
                   pl.BlockSpec(memory_space=pl.ANY)],
        out_shape=[jax.ShapeDtypeStruct((l, D_MODEL), BF16), jax.ShapeDtypeStruct((l, D_MODEL), F32),
                   jax.ShapeDtypeStruct((l, D_MODEL), F32)],
        scratch_shapes=[pltpu.VMEM((l, SB_HEAD_DIM), F32), pltpu.VMEM((l, SB_HEAD_DIM), F32)],
        compiler_params=_params(dimension_semantics=("arbitrary", "arbitrary")),
    )(qkv, qkv, qkv, row_total, dout)


def gather_weight_shards(pack_bf16, pack_f32):
    rb = pack_bf16.shape[0]
    rf = pack_f32.shape[0]

    def body(b_ref, f_ref, ob_ref, of_ref, send_sems, recv_sems, local_sems):
        x, y, c = lax.axis_index("x"), lax.axis_index("y"), lax.axis_index("c")
        me = 2 * x + y
        chips = [(1 - x, y), (x, 1 - y), (1 - x, 1 - y)]
        own = [pltpu.make_async_copy(b_ref, ob_ref.at[me], local_sems.at[0]),
               pltpu.make_async_copy(f_ref, of_ref.at[me], local_sems.at[1])]
        for cp in own:
            cp.start()
        sends = []
        for k, (px, py) in enumerate(chips):
            for t, (src, dst) in enumerate(((b_ref, ob_ref), (f_ref, of_ref))):
                sends.append(pltpu.make_async_remote_copy(
                    src_ref=src, dst_ref=dst.at[me], send_sem=send_sems.at[2 * k + t],
                    recv_sem=recv_sems.at[2 * k + t], device_id=(px, py, c), device_id_type=MESH))
        for cp in sends:
            cp.start()
        for k, (px, py) in enumerate(chips):
            for t, (src, dst) in enumerate(((b_ref, ob_ref), (f_ref, of_ref))):
                pltpu.make_async_remote_copy(
                    src_ref=src, dst_ref=dst.at[2 * px + py], send_sem=send_sems.at[2 * k + t],
                    recv_sem=recv_sems.at[2 * k + t], device_id=(px, py, c), device_id_type=MESH).wait_recv()
        for cp in sends:
            cp.wait_send()
        for cp in own:
            cp.wait()

    hbm = pl.BlockSpec(memory_space=pl.ANY)
    return pl.pallas_call(
        body, name="gather_weight_shards",
        in_specs=[hbm, hbm], out_specs=[hbm, hbm],
        out_shape=[jax.ShapeDtypeStruct((N_CHIPS, rb, LANES), BF16),
                   jax.ShapeDtypeStruct((N_CHIPS, rf, LANES), F32)],
        scratch_shapes=[pltpu.SemaphoreType.DMA((6,)), pltpu.SemaphoreType.DMA((6,)),
                        pltpu.SemaphoreType.DMA((2,))],
    )(pack_bf16, pack_f32)


def exchange_grad_quarters(gpack):
    rows = gpack.shape[1]

    def body(g_ref, slots_ref, send_sems, recv_sems, local_sem):
        x, y, c = lax.axis_index("x"), lax.axis_index("y"), lax.axis_index("c")
        me = 4 * x + 2 * y + c
        my_chip = 2 * x + y
        own = pltpu.make_async_copy(g_ref.at[my_chip], slots_ref.at[me], local_sem)
        own.start()
        peers = []
        for rel in range(1, N_DEV):
            fx, fy, fc = (rel >> 2) & 1, (rel >> 1) & 1, rel & 1
            px = x + fx - 2 * x * fx
            py = y + fy - 2 * y * fy
            pc = c + fc - 2 * c * fc
            peers.append((px, py, pc))
        sends = [pltpu.make_async_remote_copy(
            src_ref=g_ref.at[2 * px + py], dst_ref=slots_ref.at[me], send_sem=send_sems.at[k],
            recv_sem=recv_sems.at[k], device_id=(px, py, pc), device_id_type=MESH)
            for k, (px, py, pc) in enumerate(peers)]
        for cp in sends:
            cp.start()
        for k, (px, py, pc) in enumerate(peers):
            pltpu.make_async_remote_copy(
                src_ref=g_ref.at[my_chip], dst_ref=slots_ref.at[4 * px + 2 * py + pc], send_sem=send_sems.at[k],
                recv_sem=recv_sems.at[k], device_id=(px, py, pc), device_id_type=MESH).wait_recv()
        for cp in sends:
            cp.wait_send()
        own.wait()

    hbm = pl.BlockSpec(memory_space=pl.ANY)
    return pl.pallas_call(
        body, name="exchange_grad_quarters",
        in_specs=[hbm], out_specs=hbm,
        out_shape=jax.ShapeDtypeStruct((N_DEV, rows, LANES), F32),
        scratch_shapes=[pltpu.SemaphoreType.DMA((N_DEV - 1,)), pltpu.SemaphoreType.DMA((N_DEV - 1,)),
                        pltpu.SemaphoreType.DMA],
    )(gpack)


def reduce_adamw(slots, w, m, v):
    rows = w.shape[0]
    tr = ADAM_TILE_ROWS

    def body(s_ref, w_ref, m_ref, v_ref, g_out, d_out, m_out, v_out):
        g = s_ref[0]
        for d in range(1, N_DEV):
            g = g + s_ref[d]
        m_new = ADAM_B1 * m_ref[...] + (1.0 - ADAM_B1) * g
        v_new = ADAM_B2 * v_ref[...] + (1.0 - ADAM_B2) * jnp.square(g)
        m_hat = m_new / (1.0 - ADAM_B1 ** ADAM_STEP)
        v_hat = v_new / (1.0 - ADAM_B2 ** ADAM_STEP)
        g_out[...] = g
        d_out[...] = -ADAM_LR * (m_hat / (jnp.sqrt(v_hat) + ADAM_EPS) + ADAM_WD * w_ref[...])
        m_out[...] = m_new
        v_out[...] = v_new

    row = pl.BlockSpec((tr, LANES), lambda i: (i, 0))
    return pl.pallas_call(
        body, name="reduce_adamw", grid=(rows // tr,),
        in_specs=[pl.BlockSpec((N_DEV, tr, LANES), lambda i: (0, i, 0)), row, row, row],
        out_specs=[row, row, row, row],
        out_shape=[jax.ShapeDtypeStruct((rows, LANES), F32)] * 4,
        compiler_params=_params(dimension_semantics=("arbitrary",)),
    )(slots, w, m, v)


def _pack(pieces, dtype, row_multiple):
    flat = jnp.concatenate([p.astype(dtype).reshape(-1) for p in pieces])
    per = row_multiple * LANES
    padded = -(-flat.shape[0] // per) * per
    flat = jnp.pad(flat, (0, padded - flat.shape[0]))
    return flat.reshape(-1, LANES)


def _unpack(buf, shapes):
    flat = buf.reshape(-1)
    out, off = [], 0
    for s in shapes:
        n = int(np.prod(s))
        out.append(flat[off:off + n].reshape(s))
        off += n
    return out


RET_SCALE = RET_QK_DIM ** -0.5
RET_LOG_GAMMA = [math.log(1.0 - 2.0 ** (-5.0 - h)) for h in range(RET_HEADS)]
RET_HALF = RET_QK_DIM // 2


def _ret_tables(length):
    inv = ROPE_BASE ** (-jnp.arange(RET_HALF, dtype=F32) / RET_HALF)
    ang = jnp.arange(length).astype(F32)[:, None] * inv[None, :]
    log_gamma = jnp.log(1.0 - jnp.power(2.0, -5.0 - jnp.arange(RET_HEADS, dtype=F32)))
    idx = jnp.arange(CHUNK, dtype=F32)
    rel = idx[:, None] - idx[None, :]
    dmask = jnp.where(rel >= 0, jnp.exp(log_gamma[:, None, None] * jnp.maximum(rel, 0.0)), 0.0)
    k_decay = jnp.exp(log_gamma[:, None] * (CHUNK - 1 - idx)[None, :])[:, :, None]
    q_decay = jnp.exp(log_gamma[:, None] * (idx + 1.0)[None, :])[:, :, None]
    return jnp.cos(ang), jnp.sin(ang), dmask, k_decay, q_decay


def _rot(x, cs, sn):
    x1, x2 = x[:, :RET_HALF], x[:, RET_HALF:]
    return jnp.concatenate([x1 * cs - x2 * sn, x1 * sn + x2 * cs], axis=1)


def _unrot(d, cs, sn):
    d1, d2 = d[:, :RET_HALF], d[:, RET_HALF:]
    return jnp.concatenate([d1 * cs + d2 * sn, d2 * cs - d1 * sn], axis=1)


def _sigmoid(x):
    return 1.0 / (1.0 + jnp.exp(-x))


_NT = (((1,), (1,)), ((), ()))
_TN = (((0,), (0,)), ((), ()))


def _ret_specs(nc, rev):
    ch = (lambda c: nc - 1 - c) if rev else (lambda c: c)
    row = lambda w: pl.BlockSpec((CHUNK, w), lambda c: (ch(c), 0))
    const3 = lambda a, b: pl.BlockSpec((RET_HEADS, a, b), lambda c: (0, 0, 0))
    tables = [row(RET_HALF), row(RET_HALF), const3(CHUNK, CHUNK), const3(CHUNK, 1), const3(CHUNK, 1)]
    state = pl.BlockSpec((None, RET_HEADS, RET_QK_DIM, RET_V_DIM), lambda c: (ch(c), 0, 0, 0))
    return row, tables, state


def ret_fwd(p, tables):
    length = p.shape[0]
    nc = length // CHUNK
    row, table_specs, state_spec = _ret_specs(nc, False)

    def body(p_ref, cos_ref, sin_ref, dm_ref, kd_ref, qd_ref, y_ref, ypre_ref, st_ref, r_scr):
        c = pl.program_id(0)

        @pl.when(c == 0)
        def _():
            r_scr[...] = jnp.zeros_like(r_scr)

        cs, sn = cos_ref[...], sin_ref[...]
        valid = (c * CHUNK + lax.broadcasted_iota(jnp.int32, (CHUNK, 1), 0)) >= PAD
        for h in range(RET_HEADS):
            col = lambda part: slice(part * D_MODEL + h * RET_QK_DIM, part * D_MODEL + (h + 1) * RET_QK_DIM)
            qb = _rot(p_ref[:, col(0)], cs, sn).astype(BF16)
            kr = _rot(p_ref[:, col(1)], cs, sn) * RET_SCALE
            kb = kr.astype(BF16)
            vb = jnp.where(valid, p_ref[:, col(2)], 0.0).astype(BF16)
            s = lax.dot_general(qb, kb, _NT, preferred_element_type=F32) * dm_ref[h]
            r = r_scr[h]
            st_ref[h] = r
            y = (jnp.dot(s.astype(BF16), vb, preferred_element_type=F32)
                 + jnp.dot(qb, r.astype(BF16), preferred_element_type=F32) * qd_ref[h])
            kdb = (kr * kd_ref[h]).astype(BF16)
            r_scr[h] = r * math.exp(RET_LOG_GAMMA[h] * CHUNK) + lax.dot_general(kdb, vb, _TN,
                                                                                preferred_element_type=F32)
            out = slice(h * RET_V_DIM, (h + 1) * RET_V_DIM)
            ypre_ref[:, out] = y
            mu = jnp.mean(y, axis=-1, keepdims=True)
            yc = y - mu
            yn = yc * lax.rsqrt(jnp.mean(yc * yc, axis=-1, keepdims=True) + EPS)
            g = p_ref[:, col(3)]
            y_ref[:, out] = yn * (g * _sigmoid(g))

    return pl.pallas_call(
        body, name="ret_fwd", grid=(nc,),
        in_specs=[row(4 * D_MODEL)] + table_specs,
        out_specs=[row(D_MODEL), row(D_MODEL), state_spec],
        out_shape=[jax.ShapeDtypeStruct((length, D_MODEL), F32), jax.ShapeDtypeStruct((length, D_MODEL), F32),
                   jax.ShapeDtypeStruct((nc, RET_HEADS, RET_QK_DIM, RET_V_DIM), F32)],
        scratch_shapes=[pltpu.VMEM((RET_HEADS, RET_QK_DIM, RET_V_DIM), F32)],
        compiler_params=_params(dimension_semantics=("arbitrary",)),
    )(p, *tables)


def ret_bwd(p, tables, ypre, states, dyo):
    length = p.shape[0]
    nc = length // CHUNK
    row, table_specs, state_spec = _ret_specs(nc, True)

    def body(p_ref, cos_ref, sin_ref, dm_ref, kd_ref, qd_ref, ypre_ref, st_ref, dyo_ref, dp_ref, dr_scr):
        c = pl.program_id(0)

        @pl.when(c == 0)
        def _():
            dr_scr[...] = jnp.zeros_like(dr_scr)

        cs, sn = cos_ref[...], sin_ref[...]
        valid = ((nc - 1 - c) * CHUNK + lax.broadcasted_iota(jnp.int32, (CHUNK, 1), 0)) >= PAD
        for h in range(RET_HEADS):
            col = lambda part: slice(part * D_MODEL + h * RET_QK_DIM, part * D_MODEL + (h + 1) * RET_QK_DIM)
            out = slice(h * RET_V_DIM, (h + 1) * RET_V_DIM)
            qb = _rot(p_ref[:, col(0)], cs, sn).astype(BF16)
            kr = _rot(p_ref[:, col(1)], cs, sn) * RET_SCALE
            kb = kr.astype(BF16)
            vb = jnp.where(valid, p_ref[:, col(2)], 0.0).astype(BF16)
            g = p_ref[:, col(3)]
            y = ypre_ref[:, out]
            dyo_h = dyo_ref[:, out]
            mu = jnp.mean(y, axis=-1, keepdims=True)
            yc = y - mu
            rs = lax.rsqrt(jnp.mean(yc * yc, axis=-1, keepdims=True) + EPS)
            xh = yc * rs
            sg = _sigmoid(g)
            dp_ref[:, col(3)] = dyo_h * xh * (sg * (1.0 + g * (1.0 - sg)))
            dyn = dyo_h * (g * sg)
            dy = rs * (dyn - jnp.mean(dyn, axis=-1, keepdims=True)
                       - xh * jnp.mean(dyn * xh, axis=-1, keepdims=True))
            dyb = dy.astype(BF16)
            dm = dm_ref[h]
            sm = (lax.dot_general(qb, kb, _NT, preferred_element_type=F32) * dm).astype(BF16)
            dsb = (lax.dot_general(dyb, vb, _NT, preferred_element_type=F32) * dm).astype(BF16)
            rb = st_ref[h].astype(BF16)
            dyqb = (dy * qd_ref[h]).astype(BF16)
            dr = dr_scr[h]
            drb = dr.astype(BF16)
            kd = kd_ref[h]
            dq = (jnp.dot(dsb, kb, preferred_element_type=F32)
                  + lax.dot_general(dyqb, rb, _NT, preferred_element_type=F32))
            dk = (lax.dot_general(dsb, qb, _TN, preferred_element_type=F32)
                  + lax.dot_general(vb, drb, _NT, preferred_element_type=F32) * kd)
            dv = (lax.dot_general(sm, dyb, _TN, preferred_element_type=F32)
                  + jnp.dot((kr * kd).astype(BF16), drb, preferred_element_type=F32))
            dr_scr[h] = dr * math.exp(RET_LOG_GAMMA[h] * CHUNK) + lax.dot_general(qb, dyqb, _TN,
                                                                                 preferred_element_type=F32)
            dp_ref[:, col(0)] = _unrot(dq, cs, sn)
            dp_ref[:, col(1)] = _unrot(dk, cs, sn) * RET_SCALE
            dp_ref[:, col(2)] = jnp.where(valid, dv, 0.0)

    return pl.pallas_call(
        body, name="ret_bwd", grid=(nc,),
        in_specs=[row(4 * D_MODEL)] + table_specs + [row(D_MODEL), state_spec, row(D_MODEL)],
        out_specs=row(4 * D_MODEL),
        out_shape=jax.ShapeDtypeStruct((length, 4 * D_MODEL), F32),
        scratch_shapes=[pltpu.VMEM((RET_HEADS, RET_QK_DIM, RET_V_DIM), F32)],
        compiler_params=_params(dimension_semantics=("arbitrary",)),
    )(p, *tables, ypre, states, dyo)


def _shift_down(cur, prev, m):
    if m == 0:
        return cur
    rows = lax.broadcasted_iota(jnp.int32, cur.shape, 0)
    return jnp.where(rows < m, pltpu.roll(prev, m, 0), pltpu.roll(cur, m, 0))


def _shift_up(cur, nxt, m):
    if m == 0:
        return cur
    n = cur.shape[0]
    rows = lax.broadcasted_iota(jnp.int32, cur.shape, 0)
    return jnp.where(rows >= n - m, pltpu.roll(nxt, n - m, 0), pltpu.roll(cur, n - m, 0))


def conv_mixer_fwd(p, conv_w):
    length = p.shape[0]
    nc = length // CHUNK
    kt = conv_w.shape[0]

    def body(cur_ref, prev_ref, w_ref, y_ref):
        c = pl.program_id(0)
        rows = lax.broadcasted_iota(jnp.int32, (CHUNK, 1), 0)

        def u_of(ref, blk):
            ok = (blk * CHUNK + rows >= PAD) & (blk >= 0)
            return jnp.where(ok, ref[:, D_MODEL:2 * D_MODEL] * ref[:, 2 * D_MODEL:], 0.0)

        u_cur = u_of(cur_ref, c)
        u_prev = u_of(prev_ref, c - 1)
        acc = jnp.zeros((CHUNK, D_MODEL), F32)
        for i in range(kt):
            acc = acc + _shift_down(u_cur, u_prev, kt - 1 - i) * w_ref[i:i + 1, :]
        y_ref[...] = cur_ref[:, :D_MODEL] * acc

    return pl.pallas_call(
        body, name="conv_mixer_fwd", grid=(nc,),
        in_specs=[pl.BlockSpec((CHUNK, 3 * D_MODEL), lambda c: (c, 0)),
                  pl.BlockSpec((CHUNK, 3 * D_MODEL), lambda c: (jnp.maximum(c - 1, 0), 0)),
                  pl.BlockSpec((kt, D_MODEL), lambda c: (0, 0))],
        out_specs=pl.BlockSpec((CHUNK, D_MODEL), lambda c: (c, 0)),
        out_shape=jax.ShapeDtypeStruct((length, D_MODEL), F32),
        compiler_params=_params(dimension_semantics=("arbitrary",)),
    )(p, p, conv_w)


def conv_mixer_bwd(p, conv_w, dy):
    length = p.shape[0]
    nc = length // CHUNK
    kt = conv_w.shape[0]

    def body(cur_ref, prev_ref, w_ref, dy_ref, dyn_ref, pn_ref, dp_ref, dw_ref):
        c = pl.program_id(0)
        rows = lax.broadcasted_iota(jnp.int32, (CHUNK, 1), 0)

        def u_of(ref, blk):
            ok = (blk * CHUNK + rows >= PAD) & (blk >= 0)
            return jnp.where(ok, ref[:, D_MODEL:2 * D_MODEL] * ref[:, 2 * D_MODEL:], 0.0)

        u_cur = u_of(cur_ref, c)
        u_prev = u_of(prev_ref, c - 1)
        b_gate = cur_ref[:, :D_MODEL]
        dyv = dy_ref[...]
        dconv = dyv * b_gate
        dconv_next = jnp.where(c + 1 < nc, dyn_ref[...] * pn_ref[:, :D_MODEL], 0.0)

        @pl.when(c == 0)
        def _():
            dw_ref[...] = jnp.zeros_like(dw_ref)

        acc = jnp.zeros((CHUNK, D_MODEL), F32)
        du = jnp.zeros((CHUNK, D_MODEL), F32)
        for i in range(kt):
            shifted = _shift_down(u_cur, u_prev, kt - 1 - i)
            acc = acc + shifted * w_ref[i:i + 1, :]
            dw_ref[i:i + 1, :] += jnp.sum(dconv * shifted, axis=0, keepdims=True)
            du = du + _shift_up(dconv, dconv_next, kt - 1 - i) * w_ref[i:i + 1, :]
        du = jnp.where(c * CHUNK + rows >= PAD, du, 0.0)
        dp_ref[:, :D_MODEL] = dyv * acc
        dp_ref[:, D_MODEL:2 * D_MODEL] = du * cur_ref[:, 2 * D_MODEL:]
        dp_ref[:, 2 * D_MODEL:] = du * cur_ref[:, D_MODEL:2 * D_MODEL]

    nxt = lambda c: (jnp.minimum(c + 1, nc - 1), 0)
    return pl.pallas_call(
        body, name="conv_mixer_bwd", grid=(nc,),
        in_specs=[pl.BlockSpec((CHUNK, 3 * D_MODEL), lambda c: (c, 0)),
                  pl.BlockSpec((CHUNK, 3 * D_MODEL), lambda c: (jnp.maximum(c - 1, 0), 0)),
                  pl.BlockSpec((kt, D_MODEL), lambda c: (0, 0)),
                  pl.BlockSpec((CHUNK, D_MODEL), lambda c: (c, 0)),
                  pl.BlockSpec((CHUNK, D_MODEL), nxt),
                  pl.BlockSpec((CHUNK, 3 * D_MODEL), nxt)],
        out_specs=[pl.BlockSpec((CHUNK, 3 * D_MODEL), lambda c: (c, 0)),
                   pl.BlockSpec((kt, D_MODEL), lambda c: (0, 0))],
        out_shape=[jax.ShapeDtypeStruct((length, 3 * D_MODEL), F32), jax.ShapeDtypeStruct((kt, D_MODEL), F32)],
        compiler_params=_params(dimension_semantics=("arbitrary",)),
    )(p, p, conv_w, dy, dy, p)


def merge_fwd(gate_logits, ups):
    length = gate_logits.shape[0]
    tm = _tile(length, (384, 256, 128))

    def body(g_ref, u0, u1, u2, u3, o_ref):
        acc = jnp.zeros((tm, D_MODEL), F32)
        for n, u in enumerate((u0, u1, u2, u3)):
            acc = acc + _sigmoid(g_ref[:, n * D_MODEL:(n + 1) * D_MODEL]) * u[...]
        o_ref[...] = acc

    row = pl.BlockSpec((tm, D_MODEL), lambda i: (i, 0))
    return pl.pallas_call(
        body, name="merge_fwd", grid=(length // tm,),
        in_specs=[pl.BlockSpec((tm, N_BRANCH * D_MODEL), lambda i: (i, 0))] + [row] * N_BRANCH,
        out_specs=row, out_shape=jax.ShapeDtypeStruct((length, D_MODEL), F32),
        compiler_params=_params(dimension_semantics=("arbitrary",)),
    )(gate_logits, *ups)


def merge_bwd(gate_logits, ups, dmerged):
    length = gate_logits.shape[0]
    tm = _tile(length, (384, 256, 128))

    def body(g_ref, u0, u1, u2, u3, dm_ref, dg_ref, d0, d1, d2, d3):
        dm = dm_ref[...]
        for n, (u, du) in enumerate(((u0, d0), (u1, d1), (u2, d2), (u3, d3))):
            cols = slice(n * D_MODEL, (n + 1) * D_MODEL)
            s = _sigmoid(g_ref[:, cols])
            du[...] = dm * s
            dg_ref[:, cols] = dm * u[...] * (s * (1.0 - s))

    row = pl.BlockSpec((tm, D_MODEL), lambda i: (i, 0))
    wide = pl.BlockSpec((tm, N_BRANCH * D_MODEL), lambda i: (i, 0))
    outs = pl.pallas_call(
        body, name="merge_bwd", grid=(length // tm,),
        in_specs=[wide] + [row] * (N_BRANCH + 1),
        out_specs=[wide] + [row] * N_BRANCH,
        out_shape=[jax.ShapeDtypeStruct((length, N_BRANCH * D_MODEL), F32)]
        + [jax.ShapeDtypeStruct((length, D_MODEL), F32)] * N_BRANCH,
        compiler_params=_params(dimension_semantics=("arbitrary",)),
    )(gate_logits, *ups, dmerged)
    return outs[0], list(outs[1:])


def swiglu_fwd(f):
    length = f.shape[0]
    tm = _tile(length, (384, 256, 128))

    def body(f_ref, o_ref):
        a = f_ref[:, :D_FF]
        o_ref[...] = a * _sigmoid(a) * f_ref[:, D_FF:]

    return pl.pallas_call(
        body, name="swiglu_fwd", grid=(length // tm,),
        in_specs=[pl.BlockSpec((tm, 2 * D_FF), lambda i: (i, 0))],
        out_specs=pl.BlockSpec((tm, D_FF), lambda i: (i, 0)),
        out_shape=jax.ShapeDtypeStruct((length, D_FF), F32),
        compiler_params=_params(dimension_semantics=("arbitrary",)),
    )(f)


def swiglu_bwd(f, dact):
    length = f.shape[0]
    tm = _tile(length, (384, 256, 128))

    def body(f_ref, d_ref, df_ref):
        a = f_ref[:, :D_FF]
        up = f_ref[:, D_FF:]
        d = d_ref[...]
        s = _sigmoid(a)
        df_ref[:, :D_FF] = d * up * (s * (1.0 + a * (1.0 - s)))
        df_ref[:, D_FF:] = d * (a * s)

    return pl.pallas_call(
        body, name="swiglu_bwd", grid=(length // tm,),
        in_specs=[pl.BlockSpec((tm, 2 * D_FF), lambda i: (i, 0)), pl.BlockSpec((tm, D_FF), lambda i: (i, 0))],
        out_specs=pl.BlockSpec((tm, 2 * D_FF), lambda i: (i, 0)),
        out_shape=jax.ShapeDtypeStruct((length, 2 * D_FF), F32),
        compiler_params=_params(dimension_semantics=("arbitrary",)),
    )(f, dact)


def _causal_dwconv(u, w):
    k_taps = w.shape[0]
    length = u.shape[0]
    up = jnp.pad(u, ((k_taps - 1, 0), (0, 0)))
    out = up[0:length] * w[0]
    for i in range(1, k_taps):
        out = out + up[i:i + length] * w[i]
    return out


def _short_conv_mixer(p, conv_w, valid):
    b_gate, c_gate, xa = p[:, :D_MODEL], p[:, D_MODEL:2 * D_MODEL], p[:, 2 * D_MODEL:]
    u = c_gate * xa * valid[:, None].astype(F32)
    return b_gate * _causal_dwconv(u, conv_w)


def _ssd_mixer(p, dt_raw, conv_w, conv_b, dt_bias, a_log, d_skip, norm_w, valid):
    z, xbc = p[:, :SSD_INNER], p[:, SSD_INNER:]
    dt_raw = dt_raw[:, :SSD_HEADS]
    length = z.shape[0]
    nc = length // CHUNK
    hpg = SSD_HEADS // SSD_GROUPS
    vm = valid[:, None].astype(F32)
    xbc = jax.nn.silu(_causal_dwconv(xbc * vm, conv_w) + conv_b)
    xs = xbc[:, :SSD_INNER]
    bm = xbc[:, SSD_INNER:SSD_INNER + SSD_GROUPS * SSD_STATE]
    cm = xbc[:, SSD_INNER + SSD_GROUPS * SSD_STATE:]
    xs = (xs * vm).reshape(nc, CHUNK, SSD_GROUPS, hpg, SSD_HEAD_DIM)
    bc = bm.reshape(nc, CHUNK, SSD_GROUPS, SSD_STATE)
    cc = cm.reshape(nc, CHUNK, SSD_GROUPS, SSD_STATE)
    dt = jax.nn.softplus(dt_raw + dt_bias)
    a = (-jnp.exp(a_log) * dt).reshape(nc, CHUNK, SSD_GROUPS, hpg)
    xdt = xs * dt.reshape(nc, CHUNK, SSD_GROUPS, hpg)[..., None]
    acs = jnp.moveaxis(jnp.cumsum(a, axis=1), 1, -1)
    causal = jnp.tril(jnp.ones((CHUNK, CHUNK), dtype=bool))
    seg = jnp.exp(jnp.where(causal, acs[..., :, None] - acs[..., None, :], -jnp.inf))
    cb = jnp.einsum('clgn,csgn->cgls', cc, bc)
    y_diag = jnp.einsum('cgjls,csgjp->clgjp', cb[:, :, None] * seg, xdt)
    decay_states = jnp.exp(acs[..., -1:] - acs)
    states = jnp.einsum('clgn,cgjl,clgjp->cgjpn', bc, decay_states, xdt)
    chunk_decay = jnp.exp(acs[..., -1])

    def step(hstate, inp):
        st, dec = inp
        return hstate * dec[..., None, None] + st, hstate

    h0 = jnp.zeros((SSD_GROUPS, hpg, SSD_HEAD_DIM, SSD_STATE), F32)
    _, prev = lax.scan(step, h0, (states, chunk_decay))
    y_off = jnp.einsum('clgn,cgjpn,cgjl->clgjp', cc, prev, jnp.exp(acs))
    y = y_diag + y_off + xs * d_skip.reshape(SSD_GROUPS, hpg)[..., None]
    y = y.reshape(length, SSD_INNER) * jax.nn.silu(z)
    yg = y.reshape(length, SSD_GROUPS, SSD_INNER // SSD_GROUPS)
    yg = yg * lax.rsqrt(jnp.mean(yg * yg, axis=-1, keepdims=True) + EPS)
    return yg.reshape(length, SSD_INNER) * norm_w


def _rotate(x, pos):
    half = x.shape[-1] // 2
    inv = ROPE_BASE ** (-jnp.arange(half, dtype=F32) / half)
    ang = pos.astype(F32)[:, None] * inv[None, :]
    cos = jnp.cos(ang)[:, None, :]
    sin = jnp.sin(ang)[:, None, :]
    x1, x2 = x[..., :half], x[..., half:]
    return jnp.concatenate([x1 * cos - x2 * sin, x1 * sin + x2 * cos], axis=-1)


def _retention_mixer(p, valid):
    w = RET_HEADS * RET_QK_DIM
    q, k, v, g = p[:, :w], p[:, w:2 * w], p[:, 2 * w:2 * w + RET_WIDTH], p[:, 2 * w + RET_WIDTH:]
    length = q.shape[0]
    nc = length // CHUNK
    pos = jnp.arange(length)
    qr = _rotate(q.reshape(length, RET_HEADS, RET_QK_DIM), pos)
    kr = _rotate(k.reshape(length, RET_HEADS, RET_QK_DIM), pos) * (RET_QK_DIM ** -0.5)
    vr = v.reshape(length, RET_HEADS, RET_V_DIM) * valid.astype(F32)[:, None, None]
    log_gamma = jnp.log(1.0 - jnp.power(2.0, -5.0 - jnp.arange(RET_HEADS, dtype=F32)))
    idx = jnp.arange(CHUNK, dtype=F32)
    rel = idx[:, None] - idx[None, :]
    dmask = jnp.where(rel >= 0, jnp.exp(log_gamma[:, None, None] * jnp.maximum(rel, 0.0)), 0.0)
    qc = qr.reshape(nc, CHUNK, RET_HEADS, RET_QK_DIM)
    kc = kr.reshape(nc, CHUNK, RET_HEADS, RET_QK_DIM)
    vc = vr.reshape(nc, CHUNK, RET_HEADS, RET_V_DIM)
    scores = jnp.einsum('clhd,cshd->chls', qc, kc) * dmask
    y_in = jnp.einsum('chls,cshe->clhe', scores, vc)
    k_decay = jnp.exp(log_gamma[:, None] * (CHUNK - 1 - idx)[None, :])
    kv = jnp.einsum('cshd,hs,cshe->chde', kc, k_decay, vc)
    chunk_decay = jnp.exp(log_gamma * CHUNK)

    def step(r, kv_c):
        return r * chunk_decay[:, None, None] + kv_c, r

    r0 = jnp.zeros((RET_HEADS, RET_QK_DIM, RET_V_DIM), F32)
    _, prev = lax.scan(step, r0, kv)
    q_decay = jnp.exp(log_gamma[None, :] * (idx + 1.0)[:, None])
    y_cr = jnp.einsum('clhd,chde->clhe', qc, prev) * q_decay[:, :, None]
    y = (y_in + y_cr).reshape(length, RET_HEADS, RET_V_DIM)
    mu = jnp.mean(y, axis=-1, keepdims=True)
    var = jnp.mean(jnp.square(y - mu), axis=-1, keepdims=True)
    y = ((y - mu) * lax.rsqrt(var + EPS)).reshape(length, RET_WIDTH)
    return y * jax.nn.silu(g)


def _merge(gate_logits, up0, up1, up2, up3):
    gates = jax.nn.sigmoid(gate_logits)
    out = gates[:, :D_MODEL] * up0
    for n, up in enumerate((up1, up2, up3), start=1):
        out = out + gates[:, n * D_MODEL:(n + 1) * D_MODEL] * up
    return out


def _swiglu(f):
    return jax.nn.silu(f[:, :D_FF]) * f[:, D_FF:]


IN_A = (0, 3 * D_MODEL)
IN_S = (IN_A[1], IN_A[1] + SSD_INNER + SSD_CONV_DIM)
IN_DT = (IN_S[1], IN_S[1] + SSD_HEADS)
IN_R = (IN_DT[1], IN_DT[1] + 4 * D_MODEL)
IN_SB = (IN_R[1], IN_R[1] + 3 * D_MODEL)
IN_G = (IN_SB[1], IN_SB[1] + N_BRANCH * D_MODEL)
IN_WIDTH = IN_G[1]


def _layer_weights(full, small, l):
    w_in = full['w_in'][l]
    cut = lambda r: w_in[:, r[0]:r[1]]
    w_dt = jnp.pad(cut(IN_DT), ((0, 0), (0, DT_PAD - SSD_HEADS)))
    return dict(
        w_a=cut(IN_A), w_s=cut(IN_S), w_dt=w_dt, w_r=cut(IN_R), w_sb=cut(IN_SB), w_g=cut(IN_G),
        w_branch=[full['w_branch'][l, n] for n in range(N_BRANCH)],
        w_out=full['w_out'][l], w_ffn_in=full['w_ffn_in'][l], w_ffn_out=full['w_ffn_out'][l],
        conv_a=full['conv_a'][l], ssd_conv_w=full['ssd_conv_w'][l],
        ssd_conv_b=small['ssd_conv_b'][l], ssd_dt_bias=small['ssd_dt_bias'][l], ssd_a_log=small['ssd_a_log'][l],
        ssd_d=small['ssd_d'][l], ssd_norm=small['ssd_norm'][l],
        n_mix_pre=small['norm_mix_pre'][l][None], n_mix_post=small['norm_mix_post'][l][None],
        n_ffn_pre=small['norm_ffn_pre'][l][None], n_ffn_post=small['norm_ffn_post'][l][None],
    )


def _layer_fwd(h_res, lw, valid, ret_tables):
    s = {'h_res': h_res, 'ret_tables': ret_tables}
    hn = rms_fwd(h_res, lw['n_mix_pre'], name="rms_mix_pre")
    s['hn'] = hn
    p_a = mm_nn(hn, lw['w_a'], name="proj_conv")
    p_s = mm_nn(hn, lw['w_s'], name="proj_ssd")
    p_dt = mm_nn(hn, lw['w_dt'], name="proj_dt")
    p_r = mm_nn(hn, lw['w_r'], name="proj_ret")
    p_sb = mm_nn(hn, lw['w_sb'], out_dtype=BF16, name="proj_sb")
    p_g = mm_nn(hn, lw['w_g'], name="proj_gate")
    y_a = conv_mixer_fwd(p_a, lw['conv_a'])
    s['p_a'] = p_a
    y_b, s['vjp_b'] = jax.vjp(
        lambda p, dt, cw, cb, db, al, ds, nw: _ssd_mixer(p, dt, cw, cb, db, al, ds, nw, valid),
        p_s, p_dt, lw['ssd_conv_w'], lw['ssd_conv_b'], lw['ssd_dt_bias'], lw['ssd_a_log'], lw['ssd_d'],
        lw['ssd_norm'])
    y_c, s['ret_ypre'], s['ret_states'] = ret_fwd(p_r, ret_tables)
    s['p_r'] = p_r
    y_d, s['sb_total'] = sb_fwd(p_sb)
    s['p_sb'] = p_sb
    ys = [y_a, y_b, y_c, y_d]
    s['ys'] = ys
    ups = [mm_nn(ys[n], lw['w_branch'][n], name="branch_up") for n in range(N_BRANCH)]
    merged = merge_fwd(p_g, ups)
    s['p_g'], s['ups'] = p_g, ups
    s['merged'] = merged
    mix = mm_nn(merged, lw['w_out'], name="mix_out")
    s['mix'] = mix
    h2 = rms_fwd(mix, lw['n_mix_post'], res=h_res, name="rms_mix_post")
    s['h2'] = h2
    hf = rms_fwd(h2, lw['n_ffn_pre'], name="rms_ffn_pre")
    s['hf'] = hf
    f = mm_nn(hf, lw['w_ffn_in'], name="ffn_in")
    act = swiglu_fwd(f)
    s['f'], s['act'] = f, act
    fo = mm_nn(act, lw['w_ffn_out'], name="ffn_out")
    s['fo'] = fo
    return rms_fwd(fo, lw['n_ffn_post'], res=h2, name="rms_ffn_post"), s


def _layer_bwd(dh3, lw, s):
    g = {}
    d_fo, g['norm_ffn_post'] = rms_bwd(s['fo'], lw['n_ffn_post'], dh3, name="rms_ffn_post_bwd")
    d_act = mm_nt(d_fo, lw['w_ffn_out'], name="ffn_out_dx")
    g['w_ffn_out'] = mm_tn(s['act'], d_fo, name="ffn_out_dw")
    df = swiglu_bwd(s['f'], d_act)
    d_hf = mm_nt(df, lw['w_ffn_in'], name="ffn_in_dx")
    g['w_ffn_in'] = mm_tn(s['hf'], df, name="ffn_in_dw")
    dh2, g['norm_ffn_pre'] = rms_bwd(s['h2'], lw['n_ffn_pre'], d_hf, add=dh3, name="rms_ffn_pre_bwd")
    d_mix, g['norm_mix_post'] = rms_bwd(s['mix'], lw['n_mix_post'], dh2, name="rms_mix_post_bwd")
    d_merged = mm_nt(d_mix, lw['w_out'], name="mix_out_dx")
    g['w_out'] = mm_tn(s['merged'], d_mix, name="mix_out_dw")
    dp_g, dups = merge_bwd(s['p_g'], s['ups'], d_merged)
    dys = [mm_nt(dups[n], lw['w_branch'][n], name="branch_dx") for n in range(N_BRANCH)]
    g['w_branch'] = jnp.stack([mm_tn(s['ys'][n], dups[n], name="branch_dw") for n in range(N_BRANCH)])
    dp_a, g['conv_a'] = conv_mixer_bwd(s['p_a'], lw['conv_a'], dys[0])
    (dp_s, dp_dt, g['ssd_conv_w'], g['ssd_conv_b'], g['ssd_dt_bias'], g['ssd_a_log'], g['ssd_d'],
     g['ssd_norm']) = s['vjp_b'](dys[1])
    dp_r = ret_bwd(s['p_r'], s['ret_tables'], s['ret_ypre'], s['ret_states'], dys[2])
    dq, dk, dv = sb_bwd(s['p_sb'], s['sb_total'], dys[3])
    dp_sb = jnp.concatenate([dq, dk.astype(BF16), dv.astype(BF16)], axis=1)
    hn = s['hn']
    d_hn = None
    dws = []
    for dp, w, nm in ((dp_a, lw['w_a'], "conv"), (dp_s, lw['w_s'], "ssd"), (dp_dt, lw['w_dt'], "dt"),
                      (dp_r, lw['w_r'], "ret"), (dp_sb, lw['w_sb'], "sb"), (dp_g, lw['w_g'], "gate")):
        d_hn = mm_nt(dp, w, acc=d_hn, name="proj_dx")
        dws.append(mm_tn(hn, dp, name="proj_dw"))
    dws[2] = dws[2][:, :SSD_HEADS]
    g['w_in'] = jnp.concatenate(dws, axis=1)
    dh_res, g['norm_mix_pre'] = rms_bwd(s['h_res'], lw['n_mix_pre'], d_hn, add=dh2, name="rms_mix_pre_bwd")
    for k in ('norm_ffn_post', 'norm_ffn_pre', 'norm_mix_post', 'norm_mix_pre'):
        g[k] = g[k][0]
    return dh_res, g


def _quarter(a, axis, j):
    n = a.shape[axis] // N_CHIPS
    return lax.slice_in_dim(a, j * n, (j + 1) * n, axis=axis)


def kernel(x, meta, w_in, conv_a, ssd_conv_w, ssd_conv_b, ssd_dt_bias, ssd_a_log, ssd_d, ssd_norm, w_branch, w_out, w_ffn_in, w_ffn_out, norm_mix_pre, norm_mix_post, norm_ffn_pre, norm_ffn_post, loss_target, m_meta, m_w_in, m_conv_a, m_ssd_conv_w, m_ssd_conv_b, m_ssd_dt_bias, m_ssd_a_log, m_ssd_d, m_ssd_norm, m_w_branch, m_w_out, m_w_ffn_in, m_w_ffn_out, m_norm_mix_pre, m_norm_mix_post, m_norm_ffn_pre, m_norm_ffn_post, v_meta, v_w_in, v_conv_a, v_ssd_conv_w, v_ssd_conv_b, v_ssd_dt_bias, v_ssd_a_log, v_ssd_d, v_ssd_norm, v_w_branch, v_w_out, v_w_ffn_in, v_w_ffn_out, v_norm_mix_pre, v_norm_mix_post, v_norm_ffn_pre, v_norm_ffn_post):
    w_loc = dict(meta=meta, w_in=w_in, conv_a=conv_a, ssd_conv_w=ssd_conv_w, ssd_conv_b=ssd_conv_b,
                 ssd_dt_bias=ssd_dt_bias, ssd_a_log=ssd_a_log, ssd_d=ssd_d, ssd_norm=ssd_norm, w_branch=w_branch,
                 w_out=w_out, w_ffn_in=w_ffn_in, w_ffn_out=w_ffn_out, norm_mix_pre=norm_mix_pre,
                 norm_mix_post=norm_mix_post, norm_ffn_pre=norm_ffn_pre, norm_ffn_post=norm_ffn_post)
    m_loc = dict(meta=m_meta, w_in=m_w_in, conv_a=m_conv_a, ssd_conv_w=m_ssd_conv_w, ssd_conv_b=m_ssd_conv_b,
                 ssd_dt_bias=m_ssd_dt_bias, ssd_a_log=m_ssd_a_log, ssd_d=m_ssd_d, ssd_norm=m_ssd_norm,
                 w_branch=m_w_branch, w_out=m_w_out, w_ffn_in=m_w_ffn_in, w_ffn_out=m_w_ffn_out,
                 norm_mix_pre=m_norm_mix_pre, norm_mix_post=m_norm_mix_post, norm_ffn_pre=m_norm_ffn_pre,
                 norm_ffn_post=m_norm_ffn_post)
    v_loc = dict(meta=v_meta, w_in=v_w_in, conv_a=v_conv_a, ssd_conv_w=v_ssd_conv_w, ssd_conv_b=v_ssd_conv_b,
                 ssd_dt_bias=v_ssd_dt_bias, ssd_a_log=v_ssd_a_log, ssd_d=v_ssd_d, ssd_norm=v_ssd_norm,
                 w_branch=v_w_branch, w_out=v_w_out, w_ffn_in=v_w_ffn_in, w_ffn_out=v_w_ffn_out,
                 norm_mix_pre=v_norm_mix_pre, norm_mix_post=v_norm_mix_post, norm_ffn_pre=v_norm_ffn_pre,
                 norm_ffn_post=v_norm_ffn_post)

    gb, gf = gather_weight_shards(_pack([w_loc[n] for n in MATMUL_WEIGHTS], BF16, 16),
                                  _pack([w_loc[n] for n in SMALL_SHARDED], F32, 8))
    full = {}
    parts_b = [_unpack(gb[j], [w_loc[n].shape for n in MATMUL_WEIGHTS]) for j in range(N_CHIPS)]
    parts_f = [_unpack(gf[j], [w_loc[n].shape for n in SMALL_SHARDED]) for j in range(N_CHIPS)]
    for t, n in enumerate(MATMUL_WEIGHTS):
        full[n] = jnp.concatenate([parts_b[j][t] for j in range(N_CHIPS)], axis=SHARD_AXIS[n])
    for t, n in enumerate(SMALL_SHARDED):
        full[n] = jnp.concatenate([parts_f[j][t] for j in range(N_CHIPS)], axis=SHARD_AXIS[n])

    xs = x[0]
    seq = xs.shape[0]
    length = CHUNK + seq
    valid = jnp.arange(length) >= PAD
    h = jnp.concatenate([jnp.zeros((PAD, D_MODEL), F32), full['meta'], xs], axis=0)
    lws, saved = [], []
    ret_tables = _ret_tables(length)
    for l in range(DEPTH):
        lw = _layer_weights(full, w_loc, l)
        h, s = _layer_fwd(h, lw, valid, ret_tables)
        lws.append(lw)
        saved.append(s)

    loss_row, dh = loss_head(h, loss_target[0])
    loss = lax.psum(loss_row[0, 0], ("x", "y", "c"))

    layer_grads = [None] * DEPTH
    for l in reversed(range(DEPTH)):
        dh, layer_grads[l] = _layer_bwd(dh, lws[l], saved[l])
    grad_x = dh[CHUNK:][None]
    grads = {n: jnp.stack([layer_grads[l][n] for l in range(DEPTH)]) for n in WEIGHTS if n != 'meta'}
    grads['meta'] = dh[PAD:CHUNK]

    def grad_pieces(j):
        return [_quarter(grads[n], SHARD_AXIS[n], j) if n in SHARD_AXIS else grads[n] for n in PACK_ORDER]

    gpack = jnp.stack([_pack(grad_pieces(j), F32, ADAM_TILE_ROWS) for j in range(N_CHIPS)])
    slots = exchange_grad_quarters(gpack)
    packed = reduce_adamw(slots, *[_pack([d[n] for n in PACK_ORDER], F32, ADAM_TILE_ROWS)
                                   for d in (w_loc, m_loc, v_loc)])
    shapes = [w_loc[n].shape for n in PACK_ORDER]
    outs = []
    for buf in packed:
        pieces = dict(zip(PACK_ORDER, _unpack(buf, shapes)))
        outs.extend(pieces[n] for n in WEIGHTS)
    return (loss, grad_x, *outs)
```

```python
import functools
import math

import numpy as np
import jax
import jax.numpy as jnp
from jax import lax
from jax.experimental import pallas as pl
from jax.experimental.pallas import tpu as pltpu

F32 = jnp.float32
BF16 = jnp.bfloat16

D_MODEL = 1024
DEPTH = 2
N_META = 16
CHUNK = 128
PAD = CHUNK - N_META
EPS = 1e-6

CONV_A_K = 3
SSD_HEAD_DIM = 64
SSD_HEADS = 16
SSD_INNER = 1024
SSD_GROUPS = 4
SSD_STATE = 128
SSD_CONV_K = 4
SSD_CONV_DIM = SSD_INNER + 2 * SSD_GROUPS * SSD_STATE
RET_HEADS = 4
RET_QK_DIM = 256
RET_V_DIM = 256
RET_WIDTH = 1024
ROPE_BASE = 10000.0
SB_HEADS = 8
SB_HEAD_DIM = 128
N_BRANCH = 4
D_FF = 2816
DT_PAD = 128

ADAM_LR = 0.001
ADAM_B1 = 0.9
ADAM_B2 = 0.999
ADAM_EPS = 1e-08
ADAM_WD = 0.01
ADAM_STEP = 10

N_CHIPS = 4
N_DEV = 8
LANES = 128
VMEM_LIMIT = 56 * 1024 * 1024
MESH = pl.DeviceIdType.MESH

WEIGHTS = ['meta', 'w_in', 'conv_a', 'ssd_conv_w', 'ssd_conv_b', 'ssd_dt_bias', 'ssd_a_log', 'ssd_d',
           'ssd_norm', 'w_branch', 'w_out', 'w_ffn_in', 'w_ffn_out', 'norm_mix_pre', 'norm_mix_post',
           'norm_ffn_pre', 'norm_ffn_post']
SHARD_AXIS = {'meta': 1, 'w_in': 2, 'conv_a': 2, 'ssd_conv_w': 2, 'w_branch': 2, 'w_out': 1,
              'w_ffn_in': 2, 'w_ffn_out': 1}
MATMUL_WEIGHTS = ['w_in', 'w_branch', 'w_out', 'w_ffn_in', 'w_ffn_out']
SMALL_SHARDED = ['meta', 'conv_a', 'ssd_conv_w']
PACK_ORDER = MATMUL_WEIGHTS + SMALL_SHARDED + [n for n in WEIGHTS if n not in SHARD_AXIS]
ADAM_TILE_ROWS = 1024


def _params(**kw):
    return pltpu.CompilerParams(vmem_limit_bytes=VMEM_LIMIT, **kw)


def _tile(n, prefs):
    for p in prefs:
        if n % p == 0:
            return p
    return n


def mm_nn(a, b, out_dtype=F32, name="mm_nn"):
    m, k = a.shape
    n = b.shape[1]
    tm = _tile(m, (384, 256, 128))
    tn = _tile(n, (1024, 512, 256, 128))

    def body(a_ref, b_ref, o_ref):
        o_ref[...] = jnp.dot(a_ref[...].astype(BF16), b_ref[...].astype(BF16),
                             preferred_element_type=F32).astype(o_ref.dtype)

    return pl.pallas_call(
        body, name=name, grid=(m // tm, n // tn),
        in_specs=[pl.BlockSpec((tm, k), lambda i, j: (i, 0)), pl.BlockSpec((k, tn), lambda i, j: (0, j))],
        out_specs=pl.BlockSpec((tm, tn), lambda i, j: (i, j)),
        out_shape=jax.ShapeDtypeStruct((m, n), out_dtype),
        compiler_params=_params(dimension_semantics=("arbitrary", "arbitrary")),
    )(a, b)


def mm_nt(g, w, acc=None, name="mm_nt"):
    m, n = g.shape
    k = w.shape[0]
    tm = _tile(m, (384, 256, 128))
    tn = _tile(n, (1024, 512, 256, 128))
    has_acc = acc is not None

    def body(*refs):
        if has_acc:
            g_ref, w_ref, acc_ref, o_ref = refs
        else:
            g_ref, w_ref, o_ref = refs
        j = pl.program_id(1)

        @pl.when(j == 0)
        def _():
            o_ref[...] = acc_ref[...] if has_acc else jnp.zeros_like(o_ref)

        o_ref[...] += lax.dot_general(g_ref[...].astype(BF16), w_ref[...].astype(BF16),
                                      (((1,), (1,)), ((), ())), preferred_element_type=F32)

    in_specs = [pl.BlockSpec((tm, tn), lambda i, j: (i, j)), pl.BlockSpec((k, tn), lambda i, j: (0, j))]
    args = [g, w]
    if has_acc:
        in_specs.append(pl.BlockSpec((tm, k), lambda i, j: (i, 0)))
        args.append(acc)
    return pl.pallas_call(
        body, name=name, grid=(m // tm, n // tn),
        in_specs=in_specs,
        out_specs=pl.BlockSpec((tm, k), lambda i, j: (i, 0)),
        out_shape=jax.ShapeDtypeStruct((m, k), F32),
        compiler_params=_params(dimension_semantics=("arbitrary", "arbitrary")),
    )(*args)


def mm_tn(x, g, name="mm_tn"):
    m, k = x.shape
    n = g.shape[1]
    tm = _tile(m, (384, 256, 128))
    tk = _tile(k, (1024, 1408, 512, 256, 128))
    tn = _tile(n, (1024, 512, 256, 128))

    def body(x_ref, g_ref, o_ref):
        s = pl.program_id(2)

        @pl.when(s == 0)
        def _():
            o_ref[...] = jnp.zeros_like(o_ref)

        o_ref[...] += lax.dot_general(x_ref[...].astype(BF16), g_ref[...].astype(BF16),
                                      (((0,), (0,)), ((), ())), preferred_element_type=F32)

    return pl.pallas_call(
        body, name=name, grid=(k // tk, n // tn, m // tm),
        in_specs=[pl.BlockSpec((tm, tk), lambda a, b, s: (s, a)), pl.BlockSpec((tm, tn), lambda a, b, s: (s, b))],
        out_specs=pl.BlockSpec((tk, tn), lambda a, b, s: (a, b)),
        out_shape=jax.ShapeDtypeStruct((k, n), F32),
        compiler_params=_params(dimension_semantics=("arbitrary", "arbitrary", "arbitrary")),
    )(x, g)


def rms_fwd(x, w, res=None, name="rms_fwd"):
    m, d = x.shape
    tm = _tile(m, (384, 256, 128))
    has_res = res is not None

    def body(*refs):
        if has_res:
            x_ref, w_ref, r_ref, o_ref = refs
        else:
            x_ref, w_ref, o_ref = refs
        xv = x_ref[...]
        y = xv * lax.rsqrt(jnp.mean(xv * xv, axis=-1, keepdims=True) + EPS) * w_ref[...]
        o_ref[...] = y + r_ref[...] if has_res else y

    row = pl.BlockSpec((tm, d), lambda i: (i, 0))
    in_specs = [row, pl.BlockSpec((1, d), lambda i: (0, 0))]
    args = [x, w]
    if has_res:
        in_specs.append(row)
        args.append(res)
    return pl.pallas_call(
        body, name=name, grid=(m // tm,), in_specs=in_specs, out_specs=row,
        out_shape=jax.ShapeDtypeStruct((m, d), F32),
        compiler_params=_params(dimension_semantics=("arbitrary",)),
    )(*args)


def rms_bwd(x, w, dy, add=None, name="rms_bwd"):
    m, d = x.shape
    tm = _tile(m, (384, 256, 128))
    has_add = add is not None

    def body(*refs):
        if has_add:
            x_ref, w_ref, dy_ref, add_ref, dx_ref, dw_ref = refs
        else:
            x_ref, w_ref, dy_ref, dx_ref, dw_ref = refs
        i = pl.program_id(0)
        xv = x_ref[...]
        dyv = dy_ref[...]
        r = lax.rsqrt(jnp.mean(xv * xv, axis=-1, keepdims=True) + EPS)
        xh = xv * r
        dxh = dyv * w_ref[...]
        dx = r * (dxh - xh * jnp.mean(dxh * xh, axis=-1, keepdims=True))
        dx_ref[...] = dx + add_ref[...] if has_add else dx

        @pl.when(i == 0)
        def _():
            dw_ref[...] = jnp.zeros_like(dw_ref)

        dw_ref[...] += jnp.sum(dyv * xh, axis=0, keepdims=True)

    row = pl.BlockSpec((tm, d), lambda i: (i, 0))
    vec = pl.BlockSpec((1, d), lambda i: (0, 0))
    in_specs = [row, vec, row]
    args = [x, w, dy]
    if has_add:
        in_specs.append(row)
        args.append(add)
    return pl.pallas_call(
        body, name=name, grid=(m // tm,), in_specs=in_specs, out_specs=[row, vec],
        out_shape=[jax.ShapeDtypeStruct((m, d), F32), jax.ShapeDtypeStruct((1, d), F32)],
        compiler_params=_params(dimension_semantics=("arbitrary",)),
    )(*args)


def loss_head(h, target):
    l, d = h.shape
    nblk = l // CHUNK

    def body(h_ref, t_ref, loss_ref, dh_ref, acc_ref):
        i = pl.program_id(0)

        @pl.when(i == 0)
        def _():
            acc_ref[...] = jnp.zeros_like(acc_ref)
            dh_ref[...] = jnp.zeros_like(dh_ref)

        @pl.when(i > 0)
        def _():
            e = h_ref[...] - t_ref[...]
            dh_ref[...] = e / d
            acc_ref[...] += jnp.sum(e * e, axis=0, keepdims=True)

        @pl.when(i == nblk - 1)
        def _():
            loss_ref[...] = jnp.zeros_like(loss_ref) + 0.5 * jnp.sum(acc_ref[...]) / d

    return pl.pallas_call(
        body, name="loss_head", grid=(nblk,),
        in_specs=[pl.BlockSpec((CHUNK, d), lambda i: (i, 0)),
                  pl.BlockSpec((CHUNK, d), lambda i: (jnp.maximum(i - 1, 0), 0))],
        out_specs=[pl.BlockSpec((1, LANES), lambda i: (0, 0)), pl.BlockSpec((CHUNK, d), lambda i: (i, 0))],
        out_shape=[jax.ShapeDtypeStruct((1, LANES), F32), jax.ShapeDtypeStruct((l, d), F32)],
        scratch_shapes=[pltpu.VMEM((1, d), F32)],
        compiler_params=_params(dimension_semantics=("arbitrary",)),
    )(h, target)


SB_BLK = 128


def _sb_tile(l):
    return _tile(l, (384, 256, 128))


def _sb_tri(strict_later):
    r = lax.broadcasted_iota(jnp.int32, (2 * SB_BLK, 2 * SB_BLK), 0) & (SB_BLK - 1)
    c = lax.broadcasted_iota(jnp.int32, (2 * SB_BLK, 2 * SB_BLK), 1)
    keep = (r > c) if strict_later else (r < c)
    return jnp.where(keep | (c >= SB_BLK), 1.0, 0.0).astype(BF16)


def _sb_mask(i, j, t):
    qpos = i * t + lax.broadcasted_iota(jnp.int32, (t, t), 0)
    kpos = j * t + lax.broadcasted_iota(jnp.int32, (t, t), 1)
    return (kpos < qpos) & (kpos >= PAD)


def _sb_scores(q, k, scale, mask):
    z = lax.dot_general(q, k, (((1,), (1,)), ((), ())), preferred_element_type=F32) * scale
    sp = jnp.maximum(z, 0.0) + jnp.log(1.0 + jnp.exp(-jnp.abs(z)))
    lneg = -sp if mask is None else jnp.where(mask, -sp, 0.0)
    return z - sp, lneg


def _sb_block_sums(x, tri, two_parts):
    hi = x.astype(BF16)
    if two_parts:
        lo = (x - hi.astype(F32)).astype(BF16)
        s = jnp.dot(jnp.concatenate([hi, lo], axis=1), tri, preferred_element_type=F32)
    else:
        s = jnp.dot(hi, tri[:SB_BLK], preferred_element_type=F32)
    return s[:, :SB_BLK], s[:, SB_BLK:]


def _sb_walk_down(i, step, carry):
    carry = step(i, carry, True)
    carry = lax.fori_loop(0, jnp.maximum(i - 1, 0), lambda t, c: step(i - 1 - t, c, False), carry)
    return lax.fori_loop(0, jnp.minimum(i, 1), lambda t, c: step(0, c, True), carry)


def _sb_walk_up(i, step, carry):
    carry = lax.fori_loop(0, jnp.minimum(i, 1), lambda t, c: step(0, c, True), carry)
    carry = lax.fori_loop(1, jnp.maximum(i, 1), lambda j, c: step(j, c, False), carry)
    return step(i, carry, True)


def sb_fwd(qkv):
    l = qkv.shape[0]
    t = _sb_tile(l)
    nb = t // SB_BLK
    scale = SB_HEAD_DIM ** -0.5

    def body(q_ref, k_ref, v_ref, o_ref, tot_ref):
        i = pl.program_id(1)
        q = q_ref[...]
        tri = _sb_tri(True)

        def step(j, carry, masked):
            later, acc = carry
            rows = pl.ds(pl.multiple_of(j * t, t), t)
            mask = _sb_mask(i, j, t) if masked else None
            lpos, lneg = _sb_scores(q, k_ref[rows, :], scale, mask)
            ws = [None] * nb
            for b in reversed(range(nb)):
                cols = slice(b * SB_BLK, (b + 1) * SB_BLK)
                within, total = _sb_block_sums(lneg[:, cols], tri, True)
                ws[b] = jnp.exp(lpos[:, cols] + within + later)
                later = later + total
            w = jnp.concatenate(ws, axis=1)
            if masked:
                w = jnp.where(mask, w, 0.0)
            acc = acc + jnp.dot(w.astype(BF16), v_ref[rows, :], preferred_element_type=F32)
            return later, acc

        carry = (jnp.zeros((t, SB_BLK), F32), jnp.zeros((t, SB_HEAD_DIM), F32))
        later, acc = _sb_walk_down(i, step, carry)
        o_ref[...] = acc
        tot_ref[...] = later[:, :1]

    return pl.pallas_call(
        body, name="sb_fwd", grid=(SB_HEADS, l // t),
        in_specs=[pl.BlockSpec((t, SB_HEAD_DIM), lambda h, i: (i, h)),
                  pl.BlockSpec((l, SB_HEAD_DIM), lambda h, i: (0, SB_HEADS + h)),
                  pl.BlockSpec((l, SB_HEAD_DIM), lambda h, i: (0, 2 * SB_HEADS + h))],
        out_specs=[pl.BlockSpec((t, SB_HEAD_DIM), lambda h, i: (i, h)),
                   pl.BlockSpec((None, t, 1), lambda h, i: (h, i, 0))],
        out_shape=[jax.ShapeDtypeStruct((l, D_MODEL), F32), jax.ShapeDtypeStruct((SB_HEADS, l, 1), F32)],
        compiler_params=_params(dimension_semantics=("arbitrary", "arbitrary")),
    )(qkv, qkv, qkv)


def sb_bwd(qkv, row_total, dout):
    l = qkv.shape[0]
    t = _sb_tile(l)
    nb = t // SB_BLK
    nq = l // t
    scale = SB_HEAD_DIM ** -0.5

    def body(q_ref, k_ref, v_ref, tot_ref, do_ref, dq_ref, dk_hbm, dv_hbm, dk_acc, dv_acc):
        h = pl.program_id(0)
        i = pl.program_id(1)

        @pl.when(i == 0)
        def _():
            dk_acc[...] = jnp.zeros_like(dk_acc)
            dv_acc[...] = jnp.zeros_like(dv_acc)

        q = q_ref[...]
        dob = do_ref[...].astype(BF16)
        tri_later = _sb_tri(True)
        tri_before = _sb_tri(False)

        def step(j, carry, masked):
            later, g_before, dq = carry
            rows = pl.ds(pl.multiple_of(j * t, t), t)
            k = k_ref[rows, :]
            v = v_ref[rows, :]
            mask = _sb_mask(i, j, t) if masked else None
            lpos, lneg = _sb_scores(q, k, scale, mask)
            dw = lax.dot_general(dob, v, (((1,), (1,)), ((), ())), preferred_element_type=F32)
            ws, dzs = [None] * nb, [None] * nb
            for b in range(nb):
                cols = slice(b * SB_BLK, (b + 1) * SB_BLK)
                within, total = _sb_block_sums(lneg[:, cols], tri_later, True)
                later = later - total
                wb = jnp.exp(lpos[:, cols] + within + later)
                if masked:
                    wb = jnp.where(mask[:, cols], wb, 0.0)
                g = dw[:, cols] * wb
                g_within, g_total = _sb_block_sums(g, tri_before, False)
                dz = g - (g + g_before + g_within) * jnp.exp(lpos[:, cols])
                if masked:
                    dz = jnp.where(mask[:, cols], dz, 0.0)
                g_before = g_before + g_total
                ws[b] = wb.astype(BF16)
                dzs[b] = (dz * scale).astype(BF16)
            w = jnp.concatenate(ws, axis=1)
            dzb = jnp.concatenate(dzs, axis=1)
            dq = dq + jnp.dot(dzb, k, preferred_element_type=F32)
            dk_acc[rows, :] += lax.dot_general(dzb, q, (((0,), (0,)), ((), ())), preferred_element_type=F32)
            dv_acc[rows, :] += lax.dot_general(w, dob, (((0,), (0,)), ((), ())), preferred_element_type=F32)
            return later, g_before, dq

        carry = (jnp.broadcast_to(tot_ref[...], (t, SB_BLK)), jnp.zeros((t, SB_BLK), F32),
                 jnp.zeros((t, SB_HEAD_DIM), F32))
        _, _, dq = _sb_walk_up(i, step, carry)
        dq_ref[...] = dq.astype(dq_ref.dtype)

        @pl.when(i == nq - 1)
        def _():
            cols = pl.ds(pl.multiple_of(h * SB_HEAD_DIM, SB_HEAD_DIM), SB_HEAD_DIM)
            pltpu.sync_copy(dk_acc, dk_hbm.at[:, cols])
            pltpu.sync_copy(dv_acc, dv_hbm.at[:, cols])

    blk = lambda h, i: (i, h)
    return pl.pallas_call(
        body, name="sb_bwd", grid=(SB_HEADS, nq),
        in_specs=[pl.BlockSpec((t, SB_HEAD_DIM), blk),
                  pl.BlockSpec((l, SB_HEAD_DIM), lambda h, i: (0, SB_HEADS + h)),
                  pl.BlockSpec((l, SB_HEAD_DIM), lambda h, i: (0, 2 * SB_HEADS + h)),
                  pl.BlockSpec((None, t, 1), lambda h, i: (h, i, 0)), pl.BlockSpec((t, SB_HEAD_DIM), blk)],
        out_specs=[pl.BlockSpec((t, SB_HEAD_DIM), blk), pl.BlockSpec(memory_space=pl.ANY),
                   pl.BlockSpec(memory_space=pl.ANY)],
        out_shape=[jax.ShapeDtypeStruct((l, D_MODEL), BF16), jax.ShapeDtypeStruct((l, D_MODEL), F32),
                   jax.ShapeDtypeStruct((l, D_MODEL), F32)],
        scratch_shapes=[pltpu.VMEM((l, SB_HEAD_DIM), F32), pltpu.VMEM((l, SB_HEAD_DIM), F32)],
        compiler_params=_params(dimension_semantics=("arbitrary", "arbitrary")),
    )(qkv, qkv, qkv, row_total, dout)


def gather_weight_shards(pack_bf16, pack_f32):
    rb = pack_bf16.shape[0]
    rf = pack_f32.shape[0]

    def body(b_ref, f_ref, ob_ref, of_ref, send_sems, recv_sems, local_sems):
        x, y, c = lax.axis_index("x"), lax.axis_index("y"), lax.axis_index("c")
        me = 2 * x + y
        chips = [(1 - x, y), (x, 1 - y), (1 - x, 1 - y)]
        own = [pltpu.make_async_copy(b_ref, ob_ref.at[me], local_sems.at[0]),
               pltpu.make_async_copy(f_ref, of_ref.at[me], local_sems.at[1])]
        for cp in own:
            cp.start()
        sends = []
        for k, (px, py) in enumerate(chips):
            for t, (src, dst) in enumerate(((b_ref, ob_ref), (f_ref, of_ref))):
                sends.append(pltpu.make_async_remote_copy(
                    src_ref=src, dst_ref=dst.at[me], send_sem=send_sems.at[2 * k + t],
                    recv_sem=recv_sems.at[2 * k + t], device_id=(px, py, c), device_id_type=MESH))
        for cp in sends:
            cp.start()
        for k, (px, py) in enumerate(chips):
            for t, (src, dst) in enumerate(((b_ref, ob_ref), (f_ref, of_ref))):
                pltpu.make_async_remote_copy(
                    src_ref=src, dst_ref=dst.at[2 * px + py], send_sem=send_sems.at[2 * k + t],
                    recv_sem=recv_sems.at[2 * k + t], device_id=(px, py, c), device_id_type=MESH).wait_recv()
        for cp in sends:
            cp.wait_send()
        for cp in own:
            cp.wait()

    hbm = pl.BlockSpec(memory_space=pl.ANY)
    return pl.pallas_call(
        body, name="gather_weight_shards",
        in_specs=[hbm, hbm], out_specs=[hbm, hbm],
        out_shape=[jax.ShapeDtypeStruct((N_CHIPS, rb, LANES), BF16),
                   jax.ShapeDtypeStruct((N_CHIPS, rf, LANES), F32)],
        scratch_shapes=[pltpu.SemaphoreType.DMA((6,)), pltpu.SemaphoreType.DMA((6,)),
                        pltpu.SemaphoreType.DMA((2,))],
    )(pack_bf16, pack_f32)


def exchange_grad_quarters(gpack):
    rows = gpack.shape[1]

    def body(g_ref, slots_ref, send_sems, recv_sems, local_sem):
        x, y, c = lax.axis_index("x"), lax.axis_index("y"), lax.axis_index("c")
        me = 4 * x + 2 * y + c
        my_chip = 2 * x + y
        own = pltpu.make_async_copy(g_ref.at[my_chip], slots_ref.at[me], local_sem)
        own.start()
        peers = []
        for rel in range(1, N_DEV):
            fx, fy, fc = (rel >> 2) & 1, (rel >> 1) & 1, rel & 1
            px = x + fx - 2 * x * fx
            py = y + fy - 2 * y * fy
            pc = c + fc - 2 * c * fc
            peers.append((px, py, pc))
        sends = [pltpu.make_async_remote_copy(
            src_ref=g_ref.at[2 * px + py], dst_ref=slots_ref.at[me], send_sem=send_sems.at[k],
            recv_sem=recv_sems.at[k], device_id=(px, py, pc), device_id_type=MESH)
            for k, (px, py, pc) in enumerate(peers)]
        for cp in sends:
            cp.start()
        for k, (px, py, pc) in enumerate(peers):
            pltpu.make_async_remote_copy(
                src_ref=g_ref.at[my_chip], dst_ref=slots_ref.at[4 * px + 2 * py + pc], send_sem=send_sems.at[k],
                recv_sem=recv_sems.at[k], device_id=(px, py, pc), device_id_type=MESH).wait_recv()
        for cp in sends:
            cp.wait_send()
        own.wait()

    hbm = pl.BlockSpec(memory_space=pl.ANY)
    return pl.pallas_call(
        body, name="exchange_grad_quarters",
        in_specs=[hbm], out_specs=hbm,
        out_shape=jax.ShapeDtypeStruct((N_DEV, rows, LANES), F32),
        scratch_shapes=[pltpu.SemaphoreType.DMA((N_DEV - 1,)), pltpu.SemaphoreType.DMA((N_DEV - 1,)),
                        pltpu.SemaphoreType.DMA],
    )(gpack)


def reduce_adamw(slots, w, m, v):
    rows = w.shape[0]
    tr = ADAM_TILE_ROWS

    def body(s_ref, w_ref, m_ref, v_ref, g_out, d_out, m_out, v_out):
        g = s_ref[0]
        for d in range(1, N_DEV):
            g = g + s_ref[d]
        m_new = ADAM_B1 * m_ref[...] + (1.0 - ADAM_B1) * g
        v_new = ADAM_B2 * v_ref[...] + (1.0 - ADAM_B2) * jnp.square(g)
        m_hat = m_new / (1.0 - ADAM_B1 ** ADAM_STEP)
        v_hat = v_new / (1.0 - ADAM_B2 ** ADAM_STEP)
        g_out[...] = g
        d_out[...] = -ADAM_LR * (m_hat / (jnp.sqrt(v_hat) + ADAM_EPS) + ADAM_WD * w_ref[...])
        m_out[...] = m_new
        v_out[...] = v_new

    row = pl.BlockSpec((tr, LANES), lambda i: (i, 0))
    return pl.pallas_call(
        body, name="reduce_adamw", grid=(rows // tr,),
        in_specs=[pl.BlockSpec((N_DEV, tr, LANES), lambda i: (0, i, 0)), row, row, row],
        out_specs=[row, row, row, row],
        out_shape=[jax.ShapeDtypeStruct((rows, LANES), F32)] * 4,
        compiler_params=_params(dimension_semantics=("arbitrary",)),
    )(slots, w, m, v)


def _pack(pieces, dtype, row_multiple):
    flat = jnp.concatenate([p.astype(dtype).reshape(-1) for p in pieces])
    per = row_multiple * LANES
    padded = -(-flat.shape[0] // per) * per
    flat = jnp.pad(flat, (0, padded - flat.shape[0]))
    return flat.reshape(-1, LANES)


def _unpack(buf, shapes):
    flat = buf.reshape(-1)
    out, off = [], 0
    for s in shapes:
        n = int(np.prod(s))
        out.append(flat[off:off + n].reshape(s))
        off += n
    return out


RET_SCALE = RET_QK_DIM ** -0.5
RET_LOG_GAMMA = [math.log(1.0 - 2.0 ** (-5.0 - h)) for h in range(RET_HEADS)]
RET_HALF = RET_QK_DIM // 2


def _ret_tables(length):
    inv = ROPE_BASE ** (-jnp.arange(RET_HALF, dtype=F32) / RET_HALF)
    ang = jnp.arange(length).astype(F32)[:, None] * inv[None, :]
    log_gamma = jnp.log(1.0 - jnp.power(2.0, -5.0 - jnp.arange(RET_HEADS, dtype=F32)))
    idx = jnp.arange(CHUNK, dtype=F32)
    rel = idx[:, None] - idx[None, :]
    dmask = jnp.where(rel >= 0, jnp.exp(log_gamma[:, None, None] * jnp.maximum(rel, 0.0)), 0.0)
    k_decay = jnp.exp(log_gamma[:, None] * (CHUNK - 1 - idx)[None, :])[:, :, None]
    q_decay = jnp.exp(log_gamma[:, None] * (idx + 1.0)[None, :])[:, :, None]
    return jnp.cos(ang), jnp.sin(ang), dmask, k_decay, q_decay


def _rot(x, cs, sn):
    x1, x2 = x[:, :RET_HALF], x[:, RET_HALF:]
    return jnp.concatenate([x1 * cs - x2 * sn, x1 * sn + x2 * cs], axis=1)


def _unrot(d, cs, sn):
    d1, d2 = d[:, :RET_HALF], d[:, RET_HALF:]
    return jnp.concatenate([d1 * cs + d2 * sn, d2 * cs - d1 * sn], axis=1)


def _sigmoid(x):
    return 1.0 / (1.0 + jnp.exp(-x))


_NT = (((1,), (1,)), ((), ()))
_TN = (((0,), (0,)), ((), ()))


def _ret_specs(nc, rev):
    ch = (lambda c: nc - 1 - c) if rev else (lambda c: c)
    row = lambda w: pl.BlockSpec((CHUNK, w), lambda c: (ch(c), 0))
    const3 = lambda a, b: pl.BlockSpec((RET_HEADS, a, b), lambda c: (0, 0, 0))
    tables = [row(RET_HALF), row(RET_HALF), const3(CHUNK, CHUNK), const3(CHUNK, 1), const3(CHUNK, 1)]
    state = pl.BlockSpec((None, RET_HEADS, RET_QK_DIM, RET_V_DIM), lambda c: (ch(c), 0, 0, 0))
    return row, tables, state


def ret_fwd(p, tables):
    length = p.shape[0]
    nc = length // CHUNK
    row, table_specs, state_spec = _ret_specs(nc, False)

    def body(p_ref, cos_ref, sin_ref, dm_ref, kd_ref, qd_ref, y_ref, ypre_ref, st_ref, r_scr):
        c = pl.program_id(0)

        @pl.when(c == 0)
        def _():
            r_scr[...] = jnp.zeros_like(r_scr)

        cs, sn = cos_ref[...], sin_ref[...]
        valid = (c * CHUNK + lax.broadcasted_iota(jnp.int32, (CHUNK, 1), 0)) >= PAD
        for h in range(RET_HEADS):
            col = lambda part: slice(part * D_MODEL + h * RET_QK_DIM, part * D_MODEL + (h + 1) * RET_QK_DIM)
            qb = _rot(p_ref[:, col(0)], cs, sn).astype(BF16)
            kr = _rot(p_ref[:, col(1)], cs, sn) * RET_SCALE
            kb = kr.astype(BF16)
            vb = jnp.where(valid, p_ref[:, col(2)], 0.0).astype(BF16)
            s = lax.dot_general(qb, kb, _NT, preferred_element_type=F32) * dm_ref[h]
            r = r_scr[h]
            st_ref[h] = r
            y = (jnp.dot(s.astype(BF16), vb, preferred_element_type=F32)
                 + jnp.dot(qb, r.astype(BF16), preferred_element_type=F32) * qd_ref[h])
            kdb = (kr * kd_ref[h]).astype(BF16)
            r_scr[h] = r * math.exp(RET_LOG_GAMMA[h] * CHUNK) + lax.dot_general(kdb, vb, _TN,
                                                                                preferred_element_type=F32)
            out = slice(h * RET_V_DIM, (h + 1) * RET_V_DIM)
            ypre_ref[:, out] = y
            mu = jnp.mean(y, axis=-1, keepdims=True)
            yc = y - mu
            yn = yc * lax.rsqrt(jnp.mean(yc * yc, axis=-1, keepdims=True) + EPS)
            g = p_ref[:, col(3)]
            y_ref[:, out] = yn * (g * _sigmoid(g))

    return pl.pallas_call(
        body, name="ret_fwd", grid=(nc,),
        in_specs=[row(4 * D_MODEL)] + table_specs,
        out_specs=[row(D_MODEL), row(D_MODEL), state_spec],
        out_shape=[jax.ShapeDtypeStruct((length, D_MODEL), F32), jax.ShapeDtypeStruct((length, D_MODEL), F32),
                   jax.ShapeDtypeStruct((nc, RET_HEADS, RET_QK_DIM, RET_V_DIM), F32)],
        scratch_shapes=[pltpu.VMEM((RET_HEADS, RET_QK_DIM, RET_V_DIM), F32)],
        compiler_params=_params(dimension_semantics=("arbitrary",)),
    )(p, *tables)


def ret_bwd(p, tables, ypre, states, dyo):
    length = p.shape[0]
    nc = length // CHUNK
    row, table_specs, state_spec = _ret_specs(nc, True)

    def body(p_ref, cos_ref, sin_ref, dm_ref, kd_ref, qd_ref, ypre_ref, st_ref, dyo_ref, dp_ref, dr_scr):
        c = pl.program_id(0)

        @pl.when(c == 0)
        def _():
            dr_scr[...] = jnp.zeros_like(dr_scr)

        cs, sn = cos_ref[...], sin_ref[...]
        valid = ((nc - 1 - c) * CHUNK + lax.broadcasted_iota(jnp.int32, (CHUNK, 1), 0)) >= PAD
        for h in range(RET_HEADS):
            col = lambda part: slice(part * D_MODEL + h * RET_QK_DIM, part * D_MODEL + (h + 1) * RET_QK_DIM)
            out = slice(h * RET_V_DIM, (h + 1) * RET_V_DIM)
            qb = _rot(p_ref[:, col(0)], cs, sn).astype(BF16)
            kr = _rot(p_ref[:, col(1)], cs, sn) * RET_SCALE
            kb = kr.astype(BF16)
            vb = jnp.where(valid, p_ref[:, col(2)], 0.0).astype(BF16)
            g = p_ref[:, col(3)]
            y = ypre_ref[:, out]
            dyo_h = dyo_ref[:, out]
            mu = jnp.mean(y, axis=-1, keepdims=True)
            yc = y - mu
            rs = lax.rsqrt(jnp.mean(yc * yc, axis=-1, keepdims=True) + EPS)
            xh = yc * rs
            sg = _sigmoid(g)
            dp_ref[:, col(3)] = dyo_h * xh * (sg * (1.0 + g * (1.0 - sg)))
            dyn = dyo_h * (g * sg)
            dy = rs * (dyn - jnp.mean(dyn, axis=-1, keepdims=True)
                       - xh * jnp.mean(dyn * xh, axis=-1, keepdims=True))
            dyb = dy.astype(BF16)
            dm = dm_ref[h]
            sm = (lax.dot_general(qb, kb, _NT, preferred_element_type=F32) * dm).astype(BF16)
            dsb = (lax.dot_general(dyb, vb, _NT, preferred_element_type=F32) * dm).astype(BF16)
            rb = st_ref[h].astype(BF16)
            dyqb = (dy * qd_ref[h]).astype(BF16)
            dr = dr_scr[h]
            drb = dr.astype(BF16)
            kd = kd_ref[h]
            dq = (jnp.dot(dsb, kb, preferred_element_type=F32)
                  + lax.dot_general(dyqb, rb, _NT, preferred_element_type=F32))
            dk = (lax.dot_general(dsb, qb, _TN, preferred_element_type=F32)
                  + lax.dot_general(vb, drb, _NT, preferred_element_type=F32) * kd)
            dv = (lax.dot_general(sm, dyb, _TN, preferred_element_type=F32)
                  + jnp.dot((kr * kd).astype(BF16), drb, preferred_element_type=F32))
            dr_scr[h] = dr * math.exp(RET_LOG_GAMMA[h] * CHUNK) + lax.dot_general(qb, dyqb, _TN,
                                                                                 preferred_element_type=F32)
            dp_ref[:, col(0)] = _unrot(dq, cs, sn)
            dp_ref[:, col(1)] = _unrot(dk, cs, sn) * RET_SCALE
            dp_ref[:, col(2)] = jnp.where(valid, dv, 0.0)

    return pl.pallas_call(
        body, name="ret_bwd", grid=(nc,),
        in_specs=[row(4 * D_MODEL)] + table_specs + [row(D_MODEL), state_spec, row(D_MODEL)],
        out_specs=row(4 * D_MODEL),
        out_shape=jax.ShapeDtypeStruct((length, 4 * D_MODEL), F32),
        scratch_shapes=[pltpu.VMEM((RET_HEADS, RET_QK_DIM, RET_V_DIM), F32)],
        compiler_params=_params(dimension_semantics=("arbitrary",)),
    )(p, *tables, ypre, states, dyo)


def _shift_down(cur, prev, m):
    if m == 0:
        return cur
    rows = lax.broadcasted_iota(jnp.int32, cur.shape, 0)
    return jnp.where(rows < m, pltpu.roll(prev, m, 0), pltpu.roll(cur, m, 0))


def _shift_up(cur, nxt, m):
    if m == 0:
        return cur
    n = cur.shape[0]
    rows = lax.broadcasted_iota(jnp.int32, cur.shape, 0)
    return jnp.where(rows >= n - m, pltpu.roll(nxt, n - m, 0), pltpu.roll(cur, n - m, 0))


def conv_mixer_fwd(p, conv_w):
    length = p.shape[0]
    nc = length // CHUNK
    kt = conv_w.shape[0]

    def body(cur_ref, prev_ref, w_ref, y_ref):
        c = pl.program_id(0)
        rows = lax.broadcasted_iota(jnp.int32, (CHUNK, 1), 0)

        def u_of(ref, blk):
            ok = (blk * CHUNK + rows >= PAD) & (blk >= 0)
            return jnp.where(ok, ref[:, D_MODEL:2 * D_MODEL] * ref[:, 2 * D_MODEL:], 0.0)

        u_cur = u_of(cur_ref, c)
        u_prev = u_of(prev_ref, c - 1)
        acc = jnp.zeros((CHUNK, D_MODEL), F32)
        for i in range(kt):
            acc = acc + _shift_down(u_cur, u_prev, kt - 1 - i) * w_ref[i:i + 1, :]
        y_ref[...] = cur_ref[:, :D_MODEL] * acc

    return pl.pallas_call(
        body, name="conv_mixer_fwd", grid=(nc,),
        in_specs=[pl.BlockSpec((CHUNK, 3 * D_MODEL), lambda c: (c, 0)),
                  pl.BlockSpec((CHUNK, 3 * D_MODEL), lambda c: (jnp.maximum(c - 1, 0), 0)),
                  pl.BlockSpec((kt, D_MODEL), lambda c: (0, 0))],
        out_specs=pl.BlockSpec((CHUNK, D_MODEL), lambda c: (c, 0)),
        out_shape=jax.ShapeDtypeStruct((length, D_MODEL), F32),
        compiler_params=_params(dimension_semantics=("arbitrary",)),
    )(p, p, conv_w)


def conv_mixer_bwd(p, conv_w, dy):
    length = p.shape[0]
    nc = length // CHUNK
    kt = conv_w.shape[0]

    def body(cur_ref, prev_ref, w_ref, dy_ref, dyn_ref, pn_ref, dp_ref, dw_ref):
        c = pl.program_id(0)
        rows = lax.broadcasted_iota(jnp.int32, (CHUNK, 1), 0)

        def u_of(ref, blk):
            ok = (blk * CHUNK + rows >= PAD) & (blk >= 0)
            return jnp.where(ok, ref[:, D_MODEL:2 * D_MODEL] * ref[:, 2 * D_MODEL:], 0.0)

        u_cur = u_of(cur_ref, c)
        u_prev = u_of(prev_ref, c - 1)
        b_gate = cur_ref[:, :D_MODEL]
        dyv = dy_ref[...]
        dconv = dyv * b_gate
        dconv_next = jnp.where(c + 1 < nc, dyn_ref[...] * pn_ref[:, :D_MODEL], 0.0)

        @pl.when(c == 0)
        def _():
            dw_ref[...] = jnp.zeros_like(dw_ref)

        acc = jnp.zeros((CHUNK, D_MODEL), F32)
        du = jnp.zeros((CHUNK, D_MODEL), F32)
        for i in range(kt):
            shifted = _shift_down(u_cur, u_prev, kt - 1 - i)
            acc = acc + shifted * w_ref[i:i + 1, :]
            dw_ref[i:i + 1, :] += jnp.sum(dconv * shifted, axis=0, keepdims=True)
            du = du + _shift_up(dconv, dconv_next, kt - 1 - i) * w_ref[i:i + 1, :]
        du = jnp.where(c * CHUNK + rows >= PAD, du, 0.0)
        dp_ref[:, :D_MODEL] = dyv * acc
        dp_ref[:, D_MODEL:2 * D_MODEL] = du * cur_ref[:, 2 * D_MODEL:]
        dp_ref[:, 2 * D_MODEL:] = du * cur_ref[:, D_MODEL:2 * D_MODEL]

    nxt = lambda c: (jnp.minimum(c + 1, nc - 1), 0)
    return pl.pallas_call(
        body, name="conv_mixer_bwd", grid=(nc,),
        in_specs=[pl.BlockSpec((CHUNK, 3 * D_MODEL), lambda c: (c, 0)),
                  pl.BlockSpec((CHUNK, 3 * D_MODEL), lambda c: (jnp.maximum(c - 1, 0), 0)),
                  pl.BlockSpec((kt, D_MODEL), lambda c: (0, 0)),
                  pl.BlockSpec((CHUNK, D_MODEL), lambda c: (c, 0)),
                  pl.BlockSpec((CHUNK, D_MODEL), nxt),
                  pl.BlockSpec((CHUNK, 3 * D_MODEL), nxt)],
        out_specs=[pl.BlockSpec((CHUNK, 3 * D_MODEL), lambda c: (c, 0)),
                   pl.BlockSpec((kt, D_MODEL), lambda c: (0, 0))],
        out_shape=[jax.ShapeDtypeStruct((length, 3 * D_MODEL), F32), jax.ShapeDtypeStruct((kt, D_MODEL), F32)],
        compiler_params=_params(dimension_semantics=("arbitrary",)),
    )(p, p, conv_w, dy, dy, p)


def merge_fwd(gate_logits, ups):
    length = gate_logits.shape[0]
    tm = _tile(length, (384, 256, 128))

    def body(g_ref, u0, u1, u2, u3, o_ref):
        acc = jnp.zeros((tm, D_MODEL), F32)
        for n, u in enumerate((u0, u1, u2, u3)):
            acc = acc + _sigmoid(g_ref[:, n * D_MODEL:(n + 1) * D_MODEL]) * u[...]
        o_ref[...] = acc

    row = pl.BlockSpec((tm, D_MODEL), lambda i: (i, 0))
    return pl.pallas_call(
        body, name="merge_fwd", grid=(length // tm,),
        in_specs=[pl.BlockSpec((tm, N_BRANCH * D_MODEL), lambda i: (i, 0))] + [row] * N_BRANCH,
        out_specs=row, out_shape=jax.ShapeDtypeStruct((length, D_MODEL), F32),
        compiler_params=_params(dimension_semantics=("arbitrary",)),
    )(gate_logits, *ups)


def merge_bwd(gate_logits, ups, dmerged):
    length = gate_logits.shape[0]
    tm = _tile(length, (384, 256, 128))

    def body(g_ref, u0, u1, u2, u3, dm_ref, dg_ref, d0, d1, d2, d3):
        dm = dm_ref[...]
        for n, (u, du) in enumerate(((u0, d0), (u1, d1), (u2, d2), (u3, d3))):
            cols = slice(n * D_MODEL, (n + 1) * D_MODEL)
            s = _sigmoid(g_ref[:, cols])
            du[...] = dm * s
            dg_ref[:, cols] = dm * u[...] * (s * (1.0 - s))

    row = pl.BlockSpec((tm, D_MODEL), lambda i: (i, 0))
    wide = pl.BlockSpec((tm, N_BRANCH * D_MODEL), lambda i: (i, 0))
    outs = pl.pallas_call(
        body, name="merge_bwd", grid=(length // tm,),
        in_specs=[wide] + [row] * (N_BRANCH + 1),
        out_specs=[wide] + [row] * N_BRANCH,
        out_shape=[jax.ShapeDtypeStruct((length, N_BRANCH * D_MODEL), F32)]
        + [jax.ShapeDtypeStruct((length, D_MODEL), F32)] * N_BRANCH,
        compiler_params=_params(dimension_semantics=("arbitrary",)),
    )(gate_logits, *ups, dmerged)
    return outs[0], list(outs[1:])


def swiglu_fwd(f):
    length = f.shape[0]
    tm = _tile(length, (384, 256, 128))

    def body(f_ref, o_ref):
        a = f_ref[:, :D_FF]
        o_ref[...] = a * _sigmoid(a) * f_ref[:, D_FF:]

    return pl.pallas_call(
        body, name="swiglu_fwd", grid=(length // tm,),
        in_specs=[pl.BlockSpec((tm, 2 * D_FF), lambda i: (i, 0))],
        out_specs=pl.BlockSpec((tm, D_FF), lambda i: (i, 0)),
        out_shape=jax.ShapeDtypeStruct((length, D_FF), F32),
        compiler_params=_params(dimension_semantics=("arbitrary",)),
    )(f)


def swiglu_bwd(f, dact):
    length = f.shape[0]
    tm = _tile(length, (384, 256, 128))

    def body(f_ref, d_ref, df_ref):
        a = f_ref[:, :D_FF]
        up = f_ref[:, D_FF:]
        d = d_ref[...]
        s = _sigmoid(a)
        df_ref[:, :D_FF] = d * up * (s * (1.0 + a * (1.0 - s)))
        df_ref[:, D_FF:] = d * (a * s)

    return pl.pallas_call(
        body, name="swiglu_bwd", grid=(length // tm,),
        in_specs=[pl.BlockSpec((tm, 2 * D_FF), lambda i: (i, 0)), pl.BlockSpec((tm, D_FF), lambda i: (i, 0))],
        out_specs=pl.BlockSpec((tm, 2 * D_FF), lambda i: (i, 0)),
        out_shape=jax.ShapeDtypeStruct((length, 2 * D_FF), F32),
        compiler_params=_params(dimension_semantics=("arbitrary",)),
    )(f, dact)


SSD_PAIRS = SSD_HEADS // 2
SSD_XBC = SSD_CONV_DIM
SSD_GW = SSD_INNER // SSD_GROUPS


def _split3(x):
    h1 = x.astype(BF16)
    r1 = x - h1.astype(F32)
    h2 = r1.astype(BF16)
    h3 = (r1 - h2.astype(F32)).astype(BF16)
    return h1, h2, h3


def _tri_apply(tri, x, dims):
    out = None
    for part in _split3(x):
        t = lax.dot_general(tri, part, dims, preferred_element_type=F32)
        out = t if out is None else out + t
    return out


def _softplus(x):
    return jnp.maximum(x, 0.0) + jnp.log(1.0 + jnp.exp(-jnp.abs(x)))


def _lane_pair(x, pair):
    lanes = lax.broadcasted_iota(jnp.int32, (x.shape[0], LANES), 1)
    return jnp.where(lanes < SSD_HEAD_DIM, x[:, 2 * pair:2 * pair + 1], x[:, 2 * pair + 1:2 * pair + 2])


def _half_sums(t):
    lanes = lax.broadcasted_iota(jnp.int32, t.shape, 1)
    lo = jnp.sum(jnp.where(lanes < SSD_HEAD_DIM, t, 0.0), axis=1, keepdims=True)
    return lo, jnp.sum(t, axis=1, keepdims=True) - lo


def _put_cols(cols):
    rows = cols[0].shape[0]
    lanes = lax.broadcasted_iota(jnp.int32, (rows, LANES), 1)
    out = jnp.zeros((rows, LANES), F32)
    for h, col in enumerate(cols):
        out = out + jnp.where(lanes == h, col, 0.0)
    return out


def ssd_pre_fwd(p, dt_raw, conv_w, conv_b, dt_bias, a_log):
    length = p.shape[0]
    nc = length // CHUNK
    kt = conv_w.shape[0]

    def body(cur_ref, prev_ref, raw_ref, w_ref, b_ref, bias_ref, alog_ref, act_ref, dt_ref, a_ref):
        c = pl.program_id(0)
        rows = lax.broadcasted_iota(jnp.int32, (CHUNK, 1), 0)
        vm = c * CHUNK + rows >= PAD
        u_cur = jnp.where(vm, cur_ref[:, SSD_INNER:], 0.0)
        u_prev = jnp.where(((c - 1) * CHUNK + rows >= PAD) & (c >= 1), prev_ref[:, SSD_INNER:], 0.0)
        pre = jnp.zeros((CHUNK, SSD_XBC), F32) + b_ref[...]
        for i in range(kt):
            pre = pre + _shift_down(u_cur, u_prev, kt - 1 - i) * w_ref[i:i + 1, :]
        act = pre * _sigmoid(pre)
        act_ref[:, :SSD_INNER] = jnp.where(vm, act[:, :SSD_INNER], 0.0)
        act_ref[:, SSD_INNER:] = act[:, SSD_INNER:]
        dt = _softplus(raw_ref[...] + bias_ref[...])
        dt_ref[...] = dt
        a_ref[...] = -jnp.exp(alog_ref[...]) * dt

    row = lambda w: pl.BlockSpec((CHUNK, w), lambda c: (c, 0))
    vec = lambda w: pl.BlockSpec((1, w), lambda c: (0, 0))
    return pl.pallas_call(
        body, name="ssd_pre_fwd", grid=(nc,),
        in_specs=[row(3 * D_MODEL), pl.BlockSpec((CHUNK, 3 * D_MODEL), lambda c: (jnp.maximum(c - 1, 0), 0)),
                  row(LANES), pl.BlockSpec((kt, SSD_XBC), lambda c: (0, 0)), vec(SSD_XBC), vec(LANES), vec(LANES)],
        out_specs=[row(SSD_XBC), row(LANES), row(LANES)],
        out_shape=[jax.ShapeDtypeStruct((length, SSD_XBC), F32), jax.ShapeDtypeStruct((length, LANES), F32),
                   jax.ShapeDtypeStruct((length, LANES), F32)],
        compiler_params=_params(dimension_semantics=("arbitrary",)),
    )(p, p, dt_raw, conv_w, conv_b, dt_bias, a_log)


def ssd_pre_bwd(p, dt_raw, conv_w, conv_b, dt_bias, a_log, dact, ddt, da, dz):
    length = p.shape[0]
    nc = length // CHUNK
    kt = conv_w.shape[0]

    def body(cur_ref, prev_ref, raw_ref, w_ref, b_ref, bias_ref, alog_ref, dact_ref, ddt_ref, da_ref, dz_ref,
             dp_ref, draw_ref, dw_ref, db_ref, dbias_ref, dalog_ref, dpre_next):
        step = pl.program_id(0)
        c = nc - 1 - step
        rows = lax.broadcasted_iota(jnp.int32, (CHUNK, 1), 0)
        vm = c * CHUNK + rows >= PAD

        @pl.when(step == 0)
        def _():
            dpre_next[...] = jnp.zeros_like(dpre_next)
            dw_ref[...] = jnp.zeros_like(dw_ref)
            db_ref[...] = jnp.zeros_like(db_ref)
            dbias_ref[...] = jnp.zeros_like(dbias_ref)
            dalog_ref[...] = jnp.zeros_like(dalog_ref)

        u_cur = jnp.where(vm, cur_ref[:, SSD_INNER:], 0.0)
        u_prev = jnp.where(((c - 1) * CHUNK + rows >= PAD) & (c >= 1), prev_ref[:, SSD_INNER:], 0.0)
        shifted = [_shift_down(u_cur, u_prev, kt - 1 - i) for i in range(kt)]
        pre = jnp.zeros((CHUNK, SSD_XBC), F32) + b_ref[...]
        for i in range(kt):
            pre = pre + shifted[i] * w_ref[i:i + 1, :]
        sg = _sigmoid(pre)
        lanes = lax.broadcasted_iota(jnp.int32, (CHUNK, SSD_XBC), 1)
        dact_v = jnp.where(vm | (lanes >= SSD_INNER), dact_ref[...], 0.0)
        dpre = dact_v * (sg * (1.0 + pre * (1.0 - sg)))
        db_ref[...] += jnp.sum(dpre, axis=0, keepdims=True)
        nxt = dpre_next[...]
        du = jnp.zeros((CHUNK, SSD_XBC), F32)
        for i in range(kt):
            dw_ref[i:i + 1, :] += jnp.sum(dpre * shifted[i], axis=0, keepdims=True)
            du = du + _shift_up(dpre, nxt, kt - 1 - i) * w_ref[i:i + 1, :]
        dpre_next[...] = dpre
        dp_ref[:, :SSD_INNER] = dz_ref[...]
        dp_ref[:, SSD_INNER:] = jnp.where(vm, du, 0.0)
        x = raw_ref[...] + bias_ref[...]
        neg_exp = -jnp.exp(alog_ref[...])
        dav = da_ref[...]
        draw = (ddt_ref[...] + dav * neg_exp) * _sigmoid(x)
        draw_ref[...] = draw
        dbias_ref[...] += jnp.sum(draw, axis=0, keepdims=True)
        dalog_ref[...] += jnp.sum(dav * (neg_exp * _softplus(x)), axis=0, keepdims=True)

    rev = lambda c: (nc - 1 - c, 0)
    row = lambda w: pl.BlockSpec((CHUNK, w), rev)
    vec = lambda w: pl.BlockSpec((1, w), lambda c: (0, 0))
    taps = pl.BlockSpec((kt, SSD_XBC), lambda c: (0, 0))
    return pl.pallas_call(
        body, name="ssd_pre_bwd", grid=(nc,),
        in_specs=[row(3 * D_MODEL),
                  pl.BlockSpec((CHUNK, 3 * D_MODEL), lambda c: (jnp.maximum(nc - 2 - c, 0), 0)),
                  row(LANES), taps, vec(SSD_XBC), vec(LANES), vec(LANES),
                  row(SSD_XBC), row(LANES), row(LANES), row(SSD_INNER)],
        out_specs=[row(3 * D_MODEL), row(LANES), taps, vec(SSD_XBC), vec(LANES), vec(LANES)],
        out_shape=[jax.ShapeDtypeStruct((length, 3 * D_MODEL), F32), jax.ShapeDtypeStruct((length, LANES), F32),
                   jax.ShapeDtypeStruct((kt, SSD_XBC), F32), jax.ShapeDtypeStruct((1, SSD_XBC), F32),
                   jax.ShapeDtypeStruct((1, LANES), F32), jax.ShapeDtypeStruct((1, LANES), F32)],
        scratch_shapes=[pltpu.VMEM((CHUNK, SSD_XBC), F32)],
        compiler_params=_params(dimension_semantics=("arbitrary",)),
    )(p, p, dt_raw, conv_w, conv_b, dt_bias, a_log, dact, ddt, da, dz)


def _tri_apply_lhs_t(x, tri):
    out = None
    for part in _split3(x):
        t = lax.dot_general(part, tri, (((0,), (1,)), ((), ())), preferred_element_type=F32)
        out = t if out is None else out + t
    return out


def ssd_core_fwd(act, dt, a, d_skip):
    length = act.shape[0]
    nc = length // CHUNK

    def body(act_ref, dt_ref, a_ref, dskip_ref, y_ref, st_ref, h_scr):
        c = pl.program_id(0)

        @pl.when(c == 0)
        def _():
            h_scr[...] = jnp.zeros_like(h_scr)

        r = lax.broadcasted_iota(jnp.int32, (CHUNK, CHUNK), 0)
        s = lax.broadcasted_iota(jnp.int32, (CHUNK, CHUNK), 1)
        causal = r >= s
        incl = jnp.where(causal, 1.0, 0.0).astype(BF16)
        a_v = a_ref[...]
        acs = _tri_apply(incl, a_v, (((1,), (0,)), ((), ())))
        acs_t = _tri_apply_lhs_t(a_v, incl)
        dt_v = dt_ref[...]
        lanes = lax.broadcasted_iota(jnp.int32, (CHUNK, LANES), 1)
        low = lanes < SSD_HEAD_DIM
        for g in range(SSD_GROUPS):
            bg = act_ref[:, SSD_INNER + g * SSD_STATE:SSD_INNER + (g + 1) * SSD_STATE].astype(BF16)
            cg = act_ref[:, SSD_INNER + (SSD_GROUPS + g) * SSD_STATE:
                         SSD_INNER + (SSD_GROUPS + g + 1) * SSD_STATE].astype(BF16)
            cb = lax.dot_general(cg, bg, _NT, preferred_element_type=F32)
            for pair in (2 * g, 2 * g + 1):
                cols = slice(pair * LANES, (pair + 1) * LANES)
                xs = act_ref[:, cols]
                x = xs * _lane_pair(dt_v, pair)
                ydiag = jnp.zeros((CHUNK, LANES), F32)
                for k, keep in ((0, low), (1, ~low)):
                    h = 2 * pair + k
                    seg = jnp.where(causal, jnp.exp(acs[:, h:h + 1] - acs_t[h:h + 1, :]), 0.0)
                    ydiag = ydiag + jnp.dot((cb * seg).astype(BF16), jnp.where(keep, x, 0.0).astype(BF16),
                                            preferred_element_type=F32)
                acs_p = _lane_pair(acs, pair)
                last = acs_p[CHUNK - 1:CHUNK, :]
                xds = (x * jnp.exp(last - acs_p)).astype(BF16)
                hprev = h_scr[pair]
                st_ref[pair] = hprev
                yoff = lax.dot_general(cg, hprev.astype(BF16), _NT, preferred_element_type=F32) * jnp.exp(acs_p)
                prow = lax.broadcasted_iota(jnp.int32, (LANES, 1), 0)
                cd = jnp.where(prow < SSD_HEAD_DIM, jnp.exp(acs_t[2 * pair:2 * pair + 1, CHUNK - 1:CHUNK]),
                               jnp.exp(acs_t[2 * pair + 1:2 * pair + 2, CHUNK - 1:CHUNK]))
                h_scr[pair] = hprev * cd + lax.dot_general(xds, bg, _TN, preferred_element_type=F32)
                y_ref[:, cols] = ydiag + yoff + xs * dskip_ref[:, cols]

    row = lambda w: pl.BlockSpec((CHUNK, w), lambda c: (c, 0))
    return pl.pallas_call(
        body, name="ssd_core_fwd", grid=(nc,),
        in_specs=[row(SSD_XBC), row(LANES), row(LANES), pl.BlockSpec((1, SSD_INNER), lambda c: (0, 0))],
        out_specs=[row(SSD_INNER), pl.BlockSpec((None, SSD_PAIRS, LANES, SSD_STATE), lambda c: (c, 0, 0, 0))],
        out_shape=[jax.ShapeDtypeStruct((length, SSD_INNER), F32),
                   jax.ShapeDtypeStruct((nc, SSD_PAIRS, LANES, SSD_STATE), F32)],
        scratch_shapes=[pltpu.VMEM((SSD_PAIRS, LANES, SSD_STATE), F32)],
        compiler_params=_params(dimension_semantics=("arbitrary",)),
    )(act, dt, a, d_skip)


def ssd_core_bwd(act, dt, a, d_skip, states, dy):
    length = act.shape[0]
    nc = length // CHUNK

    def body(act_ref, dt_ref, a_ref, dskip_ref, st_ref, dy_ref, dact_ref, ddt_ref, da_ref, dds_ref, dh_scr):
        step = pl.program_id(0)

        @pl.when(step == 0)
        def _():
            dh_scr[...] = jnp.zeros_like(dh_scr)
            dds_ref[...] = jnp.zeros_like(dds_ref)

        r = lax.broadcasted_iota(jnp.int32, (CHUNK, CHUNK), 0)
        s = lax.broadcasted_iota(jnp.int32, (CHUNK, CHUNK), 1)
        causal = r >= s
        incl = jnp.where(causal, 1.0, 0.0).astype(BF16)
        a_v = a_ref[...]
        acs = _tri_apply(incl, a_v, (((1,), (0,)), ((), ())))
        acs_t = _tri_apply_lhs_t(a_v, incl)
        dt_v = dt_ref[...]
        lanes = lax.broadcasted_iota(jnp.int32, (CHUNK, LANES), 1)
        low = lanes < SSD_HEAD_DIM
        prow = lax.broadcasted_iota(jnp.int32, (LANES, 1), 0)
        is_last = lax.broadcasted_iota(jnp.int32, (CHUNK, 1), 0) == CHUNK - 1
        dacs_cols = [None] * SSD_HEADS
        dacs_rows = [None] * SSD_HEADS
        ddt_cols = [None] * SSD_HEADS
        for g in range(SSD_GROUPS):
            b_cols = slice(SSD_INNER + g * SSD_STATE, SSD_INNER + (g + 1) * SSD_STATE)
            c_cols = slice(SSD_INNER + (SSD_GROUPS + g) * SSD_STATE, SSD_INNER + (SSD_GROUPS + g + 1) * SSD_STATE)
            bg = act_ref[:, b_cols].astype(BF16)
            cg = act_ref[:, c_cols].astype(BF16)
            cb = lax.dot_general(cg, bg, _NT, preferred_element_type=F32)
            dcb = jnp.zeros((CHUNK, CHUNK), F32)
            dbg = jnp.zeros((CHUNK, SSD_STATE), F32)
            dcg = jnp.zeros((CHUNK, SSD_STATE), F32)
            for pair in (2 * g, 2 * g + 1):
                cols = slice(pair * LANES, (pair + 1) * LANES)
                xs = act_ref[:, cols]
                dtp = _lane_pair(dt_v, pair)
                x = xs * dtp
                xb = x.astype(BF16)
                dyv = dy_ref[:, cols]
                dyb = dyv.astype(BF16)
                dds_ref[:, cols] += jnp.sum(dyv * xs, axis=0, keepdims=True)
                acs_p = _lane_pair(acs, pair)
                last = acs_p[CHUNK - 1:CHUNK, :]
                ds = jnp.exp(last - acs_p)
                ea = jnp.exp(acs_p)
                hprev = st_ref[pair]
                hb = hprev.astype(BF16)
                dh = dh_scr[pair]
                dhb = dh.astype(BF16)
                dx = jnp.zeros((CHUNK, LANES), F32)
                for k, keep in ((0, low), (1, ~low)):
                    h = 2 * pair + k
                    seg = jnp.where(causal, jnp.exp(acs[:, h:h + 1] - acs_t[h:h + 1, :]), 0.0)
                    lmat = cb * seg
                    dl = lax.dot_general(jnp.where(keep, dyv, 0.0).astype(BF16), xb, _NT,
                                         preferred_element_type=F32)
                    dcb = dcb + dl * seg
                    t = dl * lmat
                    dacs_cols[h] = jnp.sum(t, axis=1, keepdims=True)
                    dacs_rows[h] = jnp.sum(t, axis=0, keepdims=True)
                    dx = dx + jnp.where(keep, lax.dot_general(lmat.astype(BF16), dyb, _TN,
                                                              preferred_element_type=F32), 0.0)
                yoff = lax.dot_general(cg, hb, _NT, preferred_element_type=F32) * ea
                dm = (dyv * ea).astype(BF16)
                dcg = dcg + jnp.dot(dm, hb, preferred_element_type=F32)
                dxds = lax.dot_general(bg, dhb, _NT, preferred_element_type=F32)
                xds = x * ds
                dbg = dbg + jnp.dot(xds.astype(BF16), dhb, preferred_element_type=F32)
                dx = dx + dxds * ds
                t_ds = dxds * xds
                e_a = jnp.exp(acs_t[2 * pair:2 * pair + 1, CHUNK - 1:CHUNK])
                e_b = jnp.exp(acs_t[2 * pair + 1:2 * pair + 2, CHUNK - 1:CHUNK])
                cd = jnp.where(prow < SSD_HEAD_DIM, e_a, e_b)
                hd = dh * hprev
                dcd_a = jnp.sum(jnp.where(prow < SSD_HEAD_DIM, hd, 0.0), keepdims=True)
                dcd_b = jnp.sum(hd, keepdims=True) - dcd_a
                dh_scr[pair] = dh * cd + lax.dot_general(dm, cg, _TN, preferred_element_type=F32)
                col_lo, col_hi = _half_sums(dyv * yoff - t_ds)
                tot_lo, tot_hi = _half_sums(jnp.sum(t_ds, axis=0, keepdims=True))
                dacs_cols[2 * pair] += col_lo + jnp.where(is_last, tot_lo + dcd_a.reshape(1, 1) * e_a, 0.0)
                dacs_cols[2 * pair + 1] += col_hi + jnp.where(is_last, tot_hi + dcd_b.reshape(1, 1) * e_b, 0.0)
                dact_ref[:, cols] = dyv * dskip_ref[:, cols] + dx * dtp
                ddt_cols[2 * pair], ddt_cols[2 * pair + 1] = _half_sums(dx * xs)
            dcbb = dcb.astype(BF16)
            dact_ref[:, b_cols] = dbg + lax.dot_general(dcbb, cg, _TN, preferred_element_type=F32)
            dact_ref[:, c_cols] = dcg + jnp.dot(dcbb, bg, preferred_element_type=F32)
        ddt_ref[...] = _put_cols(ddt_cols)
        sub = lax.broadcasted_iota(jnp.int32, (LANES, CHUNK), 0)
        rows_mat = jnp.zeros((LANES, CHUNK), F32)
        for h in range(SSD_HEADS):
            rows_mat = rows_mat + jnp.where(sub == h, dacs_rows[h], 0.0)
        dacs = _put_cols(dacs_cols) - rows_mat.T
        da_ref[...] = _tri_apply(incl, dacs, (((0,), (0,)), ((), ())))

    rev = lambda c: (nc - 1 - c, 0)
    row = lambda w: pl.BlockSpec((CHUNK, w), rev)
    lane_vec = pl.BlockSpec((1, SSD_INNER), lambda c: (0, 0))
    return pl.pallas_call(
        body, name="ssd_core_bwd", grid=(nc,),
        in_specs=[row(SSD_XBC), row(LANES), row(LANES), lane_vec,
                  pl.BlockSpec((None, SSD_PAIRS, LANES, SSD_STATE), lambda c: (nc - 1 - c, 0, 0, 0)),
                  row(SSD_INNER)],
        out_specs=[row(SSD_XBC), row(LANES), row(LANES), lane_vec],
        out_shape=[jax.ShapeDtypeStruct((length, SSD_XBC), F32), jax.ShapeDtypeStruct((length, LANES), F32),
                   jax.ShapeDtypeStruct((length, LANES), F32), jax.ShapeDtypeStruct((1, SSD_INNER), F32)],
        scratch_shapes=[pltpu.VMEM((SSD_PAIRS, LANES, SSD_STATE), F32)],
        compiler_params=_params(dimension_semantics=("arbitrary",)),
    )(act, dt, a, d_skip, states, dy)


def ssd_post_fwd(y, p, norm_w):
    length = y.shape[0]
    tm = _tile(length, (384, 256, 128))

    def body(y_ref, p_ref, w_ref, o_ref):
        for g in range(SSD_GROUPS):
            cols = slice(g * SSD_GW, (g + 1) * SSD_GW)
            z = p_ref[:, cols]
            v = y_ref[:, cols] * (z * _sigmoid(z))
            o_ref[:, cols] = v * lax.rsqrt(jnp.mean(v * v, axis=-1, keepdims=True) + EPS) * w_ref[:, cols]

    return pl.pallas_call(
        body, name="ssd_post_fwd", grid=(length // tm,),
        in_specs=[pl.BlockSpec((tm, SSD_INNER), lambda i: (i, 0)), pl.BlockSpec((tm, SSD_INNER), lambda i: (i, 0)),
                  pl.BlockSpec((1, SSD_INNER), lambda i: (0, 0))],
        out_specs=pl.BlockSpec((tm, SSD_INNER), lambda i: (i, 0)),
        out_shape=jax.ShapeDtypeStruct((length, SSD_INNER), F32),
        compiler_params=_params(dimension_semantics=("arbitrary",)),
    )(y, p, norm_w)


def ssd_post_bwd(y, p, norm_w, dout):
    length = y.shape[0]
    tm = _tile(length, (384, 256, 128))

    def body(y_ref, p_ref, w_ref, do_ref, dy_ref, dz_ref, dw_ref):
        @pl.when(pl.program_id(0) == 0)
        def _():
            dw_ref[...] = jnp.zeros_like(dw_ref)

        for g in range(SSD_GROUPS):
            cols = slice(g * SSD_GW, (g + 1) * SSD_GW)
            z = p_ref[:, cols]
            yv = y_ref[:, cols]
            sg = _sigmoid(z)
            v = yv * (z * sg)
            rs = lax.rsqrt(jnp.mean(v * v, axis=-1, keepdims=True) + EPS)
            vh = v * rs
            do = do_ref[:, cols]
            dw_ref[:, cols] += jnp.sum(do * vh, axis=0, keepdims=True)
            dvh = do * w_ref[:, cols]
            dv = rs * (dvh - vh * jnp.mean(dvh * vh, axis=-1, keepdims=True))
            dy_ref[:, cols] = dv * (z * sg)
            dz_ref[:, cols] = dv * yv * (sg * (1.0 + z * (1.0 - sg)))

    blk = pl.BlockSpec((tm, SSD_INNER), lambda i: (i, 0))
    vec = pl.BlockSpec((1, SSD_INNER), lambda i: (0, 0))
    return pl.pallas_call(
        body, name="ssd_post_bwd", grid=(length // tm,),
        in_specs=[blk, blk, vec, blk], out_specs=[blk, blk, vec],
        out_shape=[jax.ShapeDtypeStruct((length, SSD_INNER), F32), jax.ShapeDtypeStruct((length, SSD_INNER), F32),
                   jax.ShapeDtypeStruct((1, SSD_INNER), F32)],
        compiler_params=_params(dimension_semantics=("arbitrary",)),
    )(y, p, norm_w, dout)


def _ssd_rows(lw):
    pad = lambda v: jnp.pad(v, (0, LANES - SSD_HEADS))[None]
    return dict(conv_w=lw['ssd_conv_w'], conv_b=lw['ssd_conv_b'][None], dt_bias=pad(lw['ssd_dt_bias']),
                a_log=pad(lw['ssd_a_log']), d_skip=jnp.repeat(lw['ssd_d'], SSD_HEAD_DIM)[None],
                norm_w=lw['ssd_norm'][None])


def ssd_fwd(p, dt_raw, rows):
    act, dt, a = ssd_pre_fwd(p, dt_raw, rows['conv_w'], rows['conv_b'], rows['dt_bias'], rows['a_log'])
    y, states = ssd_core_fwd(act, dt, a, rows['d_skip'])
    return ssd_post_fwd(y, p, rows['norm_w']), (act, dt, a, y, states)


def ssd_bwd(p, dt_raw, rows, saved, dout):
    act, dt, a, y, states = saved
    dy, dz, dnorm = ssd_post_bwd(y, p, rows['norm_w'], dout)
    dact, ddt, da, dskip_lanes = ssd_core_bwd(act, dt, a, rows['d_skip'], states, dy)
    dp, draw, dconv_w, dconv_b, dbias, dalog = ssd_pre_bwd(
        p, dt_raw, rows['conv_w'], rows['conv_b'], rows['dt_bias'], rows['a_log'], dact, ddt, da, dz)
    grads = dict(ssd_conv_w=dconv_w, ssd_conv_b=dconv_b[0], ssd_dt_bias=dbias[0, :SSD_HEADS],
                 ssd_a_log=dalog[0, :SSD_HEADS], ssd_norm=dnorm[0],
                 ssd_d=jnp.sum(dskip_lanes.reshape(SSD_HEADS, SSD_HEAD_DIM), axis=1))
    return dp, draw, grads


IN_A = (0, 3 * D_MODEL)
IN_S = (IN_A[1], IN_A[1] + SSD_INNER + SSD_CONV_DIM)
IN_DT = (IN_S[1], IN_S[1] + SSD_HEADS)
IN_R = (IN_DT[1], IN_DT[1] + 4 * D_MODEL)
IN_SB = (IN_R[1], IN_R[1] + 3 * D_MODEL)
IN_G = (IN_SB[1], IN_SB[1] + N_BRANCH * D_MODEL)
IN_WIDTH = IN_G[1]


def _layer_weights(full, small, l):
    w_in = full['w_in'][l]
    cut = lambda r: w_in[:, r[0]:r[1]]
    w_dt = jnp.pad(cut(IN_DT), ((0, 0), (0, DT_PAD - SSD_HEADS)))
    return dict(
        w_a=cut(IN_A), w_s=cut(IN_S), w_dt=w_dt, w_r=cut(IN_R), w_sb=cut(IN_SB), w_g=cut(IN_G),
        w_branch=[full['w_branch'][l, n] for n in range(N_BRANCH)],
        w_out=full['w_out'][l], w_ffn_in=full['w_ffn_in'][l], w_ffn_out=full['w_ffn_out'][l],
        conv_a=full['conv_a'][l], ssd_conv_w=full['ssd_conv_w'][l],
        ssd_conv_b=small['ssd_conv_b'][l], ssd_dt_bias=small['ssd_dt_bias'][l], ssd_a_log=small['ssd_a_log'][l],
        ssd_d=small['ssd_d'][l], ssd_norm=small['ssd_norm'][l],
        n_mix_pre=small['norm_mix_pre'][l][None], n_mix_post=small['norm_mix_post'][l][None],
        n_ffn_pre=small['norm_ffn_pre'][l][None], n_ffn_post=small['norm_ffn_post'][l][None],
    )


def _layer_fwd(h_res, lw, ret_tables):
    s = {'h_res': h_res, 'ret_tables': ret_tables}
    hn = rms_fwd(h_res, lw['n_mix_pre'], name="rms_mix_pre")
    s['hn'] = hn
    p_a = mm_nn(hn, lw['w_a'], name="proj_conv")
    p_s = mm_nn(hn, lw['w_s'], name="proj_ssd")
    p_dt = mm_nn(hn, lw['w_dt'], name="proj_dt")
    p_r = mm_nn(hn, lw['w_r'], name="proj_ret")
    p_sb = mm_nn(hn, lw['w_sb'], out_dtype=BF16, name="proj_sb")
    p_g = mm_nn(hn, lw['w_g'], name="proj_gate")
    y_a = conv_mixer_fwd(p_a, lw['conv_a'])
    s['p_a'] = p_a
    s['ssd_rows'] = _ssd_rows(lw)
    y_b, s['ssd_saved'] = ssd_fwd(p_s, p_dt, s['ssd_rows'])
    s['p_s'], s['p_dt'] = p_s, p_dt
    y_c, s['ret_ypre'], s['ret_states'] = ret_fwd(p_r, ret_tables)
    s['p_r'] = p_r
    y_d, s['sb_total'] = sb_fwd(p_sb)
    s['p_sb'] = p_sb
    ys = [y_a, y_b, y_c, y_d]
    s['ys'] = ys
    ups = [mm_nn(ys[n], lw['w_branch'][n], name="branch_up") for n in range(N_BRANCH)]
    merged = merge_fwd(p_g, ups)
    s['p_g'], s['ups'] = p_g, ups
    s['merged'] = merged
    mix = mm_nn(merged, lw['w_out'], name="mix_out")
    s['mix'] = mix
    h2 = rms_fwd(mix, lw['n_mix_post'], res=h_res, name="rms_mix_post")
    s['h2'] = h2
    hf = rms_fwd(h2, lw['n_ffn_pre'], name="rms_ffn_pre")
    s['hf'] = hf
    f = mm_nn(hf, lw['w_ffn_in'], name="ffn_in")
    act = swiglu_fwd(f)
    s['f'], s['act'] = f, act
    fo = mm_nn(act, lw['w_ffn_out'], name="ffn_out")
    s['fo'] = fo
    return rms_fwd(fo, lw['n_ffn_post'], res=h2, name="rms_ffn_post"), s


def _layer_bwd(dh3, lw, s):
    g = {}
    d_fo, g['norm_ffn_post'] = rms_bwd(s['fo'], lw['n_ffn_post'], dh3, name="rms_ffn_post_bwd")
    d_act = mm_nt(d_fo, lw['w_ffn_out'], name="ffn_out_dx")
    g['w_ffn_out'] = mm_tn(s['act'], d_fo, name="ffn_out_dw")
    df = swiglu_bwd(s['f'], d_act)
    d_hf = mm_nt(df, lw['w_ffn_in'], name="ffn_in_dx")
    g['w_ffn_in'] = mm_tn(s['hf'], df, name="ffn_in_dw")
    dh2, g['norm_ffn_pre'] = rms_bwd(s['h2'], lw['n_ffn_pre'], d_hf, add=dh3, name="rms_ffn_pre_bwd")
    d_mix, g['norm_mix_post'] = rms_bwd(s['mix'], lw['n_mix_post'], dh2, name="rms_mix_post_bwd")
    d_merged = mm_nt(d_mix, lw['w_out'], name="mix_out_dx")
    g['w_out'] = mm_tn(s['merged'], d_mix, name="mix_out_dw")
    dp_g, dups = merge_bwd(s['p_g'], s['ups'], d_merged)
    dys = [mm_nt(dups[n], lw['w_branch'][n], name="branch_dx") for n in range(N_BRANCH)]
    g['w_branch'] = jnp.stack([mm_tn(s['ys'][n], dups[n], name="branch_dw") for n in range(N_BRANCH)])
    dp_a, g['conv_a'] = conv_mixer_bwd(s['p_a'], lw['conv_a'], dys[0])
    dp_s, dp_dt, ssd_grads = ssd_bwd(s['p_s'], s['p_dt'], s['ssd_rows'], s['ssd_saved'], dys[1])
    g.update(ssd_grads)
    dp_r = ret_bwd(s['p_r'], s['ret_tables'], s['ret_ypre'], s['ret_states'], dys[2])
    dq, dk, dv = sb_bwd(s['p_sb'], s['sb_total'], dys[3])
    dp_sb = jnp.concatenate([dq, dk.astype(BF16), dv.astype(BF16)], axis=1)
    hn = s['hn']
    d_hn = None
    dws = []
    for dp, w, nm in ((dp_a, lw['w_a'], "conv"), (dp_s, lw['w_s'], "ssd"), (dp_dt, lw['w_dt'], "dt"),
                      (dp_r, lw['w_r'], "ret"), (dp_sb, lw['w_sb'], "sb"), (dp_g, lw['w_g'], "gate")):
        d_hn = mm_nt(dp, w, acc=d_hn, name="proj_dx")
        dws.append(mm_tn(hn, dp, name="proj_dw"))
    dws[2] = dws[2][:, :SSD_HEADS]
    g['w_in'] = jnp.concatenate(dws, axis=1)
    dh_res, g['norm_mix_pre'] = rms_bwd(s['h_res'], lw['n_mix_pre'], d_hn, add=dh2, name="rms_mix_pre_bwd")
    for k in ('norm_ffn_post', 'norm_ffn_pre', 'norm_mix_post', 'norm_mix_pre'):
        g[k] = g[k][0]
    return dh_res, g


def _quarter(a, axis, j):
    n = a.shape[axis] // N_CHIPS
    return lax.slice_in_dim(a, j * n, (j + 1) * n, axis=axis)


def kernel(x, meta, w_in, conv_a, ssd_conv_w, ssd_conv_b, ssd_dt_bias, ssd_a_log, ssd_d, ssd_norm, w_branch, w_out, w_ffn_in, w_ffn_out, norm_mix_pre, norm_mix_post, norm_ffn_pre, norm_ffn_post, loss_target, m_meta, m_w_in, m_conv_a, m_ssd_conv_w, m_ssd_conv_b, m_ssd_dt_bias, m_ssd_a_log, m_ssd_d, m_ssd_norm, m_w_branch, m_w_out, m_w_ffn_in, m_w_ffn_out, m_norm_mix_pre, m_norm_mix_post, m_norm_ffn_pre, m_norm_ffn_post, v_meta, v_w_in, v_conv_a, v_ssd_conv_w, v_ssd_conv_b, v_ssd_dt_bias, v_ssd_a_log, v_ssd_d, v_ssd_norm, v_w_branch, v_w_out, v_w_ffn_in, v_w_ffn_out, v_norm_mix_pre, v_norm_mix_post, v_norm_ffn_pre, v_norm_ffn_post):
    w_loc = dict(meta=meta, w_in=w_in, conv_a=conv_a, ssd_conv_w=ssd_conv_w, ssd_conv_b=ssd_conv_b,
                 ssd_dt_bias=ssd_dt_bias, ssd_a_log=ssd_a_log, ssd_d=ssd_d, ssd_norm=ssd_norm, w_branch=w_branch,
                 w_out=w_out, w_ffn_in=w_ffn_in, w_ffn_out=w_ffn_out, norm_mix_pre=norm_mix_pre,
                 norm_mix_post=norm_mix_post, norm_ffn_pre=norm_ffn_pre, norm_ffn_post=norm_ffn_post)
    m_loc = dict(meta=m_meta, w_in=m_w_in, conv_a=m_conv_a, ssd_conv_w=m_ssd_conv_w, ssd_conv_b=m_ssd_conv_b,
                 ssd_dt_bias=m_ssd_dt_bias, ssd_a_log=m_ssd_a_log, ssd_d=m_ssd_d, ssd_norm=m_ssd_norm,
                 w_branch=m_w_branch, w_out=m_w_out, w_ffn_in=m_w_ffn_in, w_ffn_out=m_w_ffn_out,
                 norm_mix_pre=m_norm_mix_pre, norm_mix_post=m_norm_mix_post, norm_ffn_pre=m_norm_ffn_pre,
                 norm_ffn_post=m_norm_ffn_post)
    v_loc = dict(meta=v_meta, w_in=v_w_in, conv_a=v_conv_a, ssd_conv_w=v_ssd_conv_w, ssd_conv_b=v_ssd_conv_b,
                 ssd_dt_bias=v_ssd_dt_bias, ssd_a_log=v_ssd_a_log, ssd_d=v_ssd_d, ssd_norm=v_ssd_norm,
                 w_branch=v_w_branch, w_out=v_w_out, w_ffn_in=v_w_ffn_in, w_ffn_out=v_w_ffn_out,
                 norm_mix_pre=v_norm_mix_pre, norm_mix_post=v_norm_mix_post, norm_ffn_pre=v_norm_ffn_pre,
                 norm_ffn_post=v_norm_ffn_post)

    gb, gf = gather_weight_shards(_pack([w_loc[n] for n in MATMUL_WEIGHTS], BF16, 16),
                                  _pack([w_loc[n] for n in SMALL_SHARDED], F32, 8))
    full = {}
    parts_b = [_unpack(gb[j], [w_loc[n].shape for n in MATMUL_WEIGHTS]) for j in range(N_CHIPS)]
    parts_f = [_unpack(gf[j], [w_loc[n].shape for n in SMALL_SHARDED]) for j in range(N_CHIPS)]
    for t, n in enumerate(MATMUL_WEIGHTS):
        full[n] = jnp.concatenate([parts_b[j][t] for j in range(N_CHIPS)], axis=SHARD_AXIS[n])
    for t, n in enumerate(SMALL_SHARDED):
        full[n] = jnp.concatenate([parts_f[j][t] for j in range(N_CHIPS)], axis=SHARD_AXIS[n])

    xs = x[0]
    seq = xs.shape[0]
    length = CHUNK + seq
    h = jnp.concatenate([jnp.zeros((PAD, D_MODEL), F32), full['meta'], xs], axis=0)
    lws, saved = [], []
    ret_tables = _ret_tables(length)
    for l in range(DEPTH):
        lw = _layer_weights(full, w_loc, l)
        h, s = _layer_fwd(h, lw, ret_tables)
        lws.append(lw)
        saved.append(s)

    loss_row, dh = loss_head(h, loss_target[0])
    loss = lax.psum(loss_row[0, 0], ("x", "y", "c"))

    layer_grads = [None] * DEPTH
    for l in reversed(range(DEPTH)):
        dh, layer_grads[l] = _layer_bwd(dh, lws[l], saved[l])
    grad_x = dh[CHUNK:][None]
    grads = {n: jnp.stack([layer_grads[l][n] for l in range(DEPTH)]) for n in WEIGHTS if n != 'meta'}
    grads['meta'] = dh[PAD:CHUNK]

    def grad_pieces(j):
        return [_quarter(grads[n], SHARD_AXIS[n], j) if n in SHARD_AXIS else grads[n] for n in PACK_ORDER]

    gpack = jnp.stack([_pack(grad_pieces(j), F32, ADAM_TILE_ROWS) for j in range(N_CHIPS)])
    slots = exchange_grad_quarters(gpack)
    packed = reduce_adamw(slots, *[_pack([d[n] for n in PACK_ORDER], F32, ADAM_TILE_ROWS)
                                   for d in (w_loc, m_loc, v_loc)])
    shapes = [w_loc[n].shape for n in PACK_ORDER]
    outs = []
    for buf in packed:
        pieces = dict(zip(PACK_ORDER, _unpack(buf, shapes)))
        outs.extend(pieces[n] for n in WEIGHTS)
    return (loss, grad_x, *outs)
```

```python
import functools
import math

import numpy as np
import jax
import jax.numpy as jnp
from jax import lax
from jax.experimental import pallas as pl
from jax.experimental.pallas import tpu as pltpu

F32 = jnp.float32
BF16 = jnp.bfloat16

D_MODEL = 1024
DEPTH = 2
N_META = 16
CHUNK = 128
PAD = CHUNK - N_META
EPS = 1e-6

CONV_A_K = 3
SSD_HEAD_DIM = 64
SSD_HEADS = 16
SSD_INNER = 1024
SSD_GROUPS = 4
SSD_STATE = 128
SSD_CONV_K = 4
SSD_CONV_DIM = SSD_INNER + 2 * SSD_GROUPS * SSD_STATE
RET_HEADS = 4
RET_QK_DIM = 256
RET_V_DIM = 256
RET_WIDTH = 1024
ROPE_BASE = 10000.0
SB_HEADS = 8
SB_HEAD_DIM = 128
N_BRANCH = 4
D_FF = 2816
DT_PAD = 128

ADAM_LR = 0.001
ADAM_B1 = 0.9
ADAM_B2 = 0.999
ADAM_EPS = 1e-08
ADAM_WD = 0.01
ADAM_STEP = 10

N_CHIPS = 4
N_DEV = 8
LANES = 128
VMEM_LIMIT = 56 * 1024 * 1024
MESH = pl.DeviceIdType.MESH

WEIGHTS = ['meta', 'w_in', 'conv_a', 'ssd_conv_w', 'ssd_conv_b', 'ssd_dt_bias', 'ssd_a_log', 'ssd_d',
           'ssd_norm', 'w_branch', 'w_out', 'w_ffn_in', 'w_ffn_out', 'norm_mix_pre', 'norm_mix_post',
           'norm_ffn_pre', 'norm_ffn_post']
SHARD_AXIS = {'meta': 1, 'w_in': 2, 'conv_a': 2, 'ssd_conv_w': 2, 'w_branch': 2, 'w_out': 1,
              'w_ffn_in': 2, 'w_ffn_out': 1}
MATMUL_WEIGHTS = ['w_in', 'w_branch', 'w_out', 'w_ffn_in', 'w_ffn_out']
SMALL_SHARDED = ['meta', 'conv_a', 'ssd_conv_w']
SMALL_ORDER = SMALL_SHARDED + [n for n in WEIGHTS if n not in SHARD_AXIS]


def _params(**kw):
    return pltpu.CompilerParams(vmem_limit_bytes=VMEM_LIMIT, **kw)


def _tile(n, prefs):
    for p in prefs:
        if n % p == 0:
            return p
    return n


def mm_nn(a, b, out_dtype=F32, name="mm_nn"):
    m, k = a.shape
    n = b.shape[1]
    tm = _tile(m, (384, 256, 128))
    tn = _tile(n, (1024, 512, 256, 128))

    def body(a_ref, b_ref, o_ref):
        o_ref[...] = jnp.dot(a_ref[...].astype(BF16), b_ref[...].astype(BF16),
                             preferred_element_type=F32).astype(o_ref.dtype)

    return pl.pallas_call(
        body, name=name, grid=(m // tm, n // tn),
        in_specs=[pl.BlockSpec((tm, k), lambda i, j: (i, 0)), pl.BlockSpec((k, tn), lambda i, j: (0, j))],
        out_specs=pl.BlockSpec((tm, tn), lambda i, j: (i, j)),
        out_shape=jax.ShapeDtypeStruct((m, n), out_dtype),
        compiler_params=_params(dimension_semantics=("arbitrary", "arbitrary")),
    )(a, b)


def mm_nt(g, w, acc=None, name="mm_nt"):
    m, n = g.shape
    k = w.shape[0]
    tm = _tile(m, (384, 256, 128))
    tn = _tile(n, (1024, 512, 256, 128))
    has_acc = acc is not None

    def body(*refs):
        if has_acc:
            g_ref, w_ref, acc_ref, o_ref = refs
        else:
            g_ref, w_ref, o_ref = refs
        j = pl.program_id(1)

        @pl.when(j == 0)
        def _():
            o_ref[...] = acc_ref[...] if has_acc else jnp.zeros_like(o_ref)

        o_ref[...] += lax.dot_general(g_ref[...].astype(BF16), w_ref[...].astype(BF16),
                                      (((1,), (1,)), ((), ())), preferred_element_type=F32)

    in_specs = [pl.BlockSpec((tm, tn), lambda i, j: (i, j)), pl.BlockSpec((k, tn), lambda i, j: (0, j))]
    args = [g, w]
    if has_acc:
        in_specs.append(pl.BlockSpec((tm, k), lambda i, j: (i, 0)))
        args.append(acc)
    return pl.pallas_call(
        body, name=name, grid=(m // tm, n // tn),
        in_specs=in_specs,
        out_specs=pl.BlockSpec((tm, k), lambda i, j: (i, 0)),
        out_shape=jax.ShapeDtypeStruct((m, k), F32),
        compiler_params=_params(dimension_semantics=("arbitrary", "arbitrary")),
    )(*args)


def mm_tn(x, g, name="mm_tn"):
    m, k = x.shape
    n = g.shape[1]
    tm = _tile(m, (384, 256, 128))
    tk = _tile(k, (1024, 1408, 512, 256, 128))
    tn = _tile(n, (1024, 512, 256, 128))

    def body(x_ref, g_ref, o_ref):
        s = pl.program_id(2)

        @pl.when(s == 0)
        def _():
            o_ref[...] = jnp.zeros_like(o_ref)

        o_ref[...] += lax.dot_general(x_ref[...].astype(BF16), g_ref[...].astype(BF16),
                                      (((0,), (0,)), ((), ())), preferred_element_type=F32)

    return pl.pallas_call(
        body, name=name, grid=(k // tk, n // tn, m // tm),
        in_specs=[pl.BlockSpec((tm, tk), lambda a, b, s: (s, a)), pl.BlockSpec((tm, tn), lambda a, b, s: (s, b))],
        out_specs=pl.BlockSpec((tk, tn), lambda a, b, s: (a, b)),
        out_shape=jax.ShapeDtypeStruct((k, n), F32),
        compiler_params=_params(dimension_semantics=("arbitrary", "arbitrary", "arbitrary")),
    )(x, g)


def rms_fwd(x, w, res=None, name="rms_fwd"):
    m, d = x.shape
    tm = _tile(m, (384, 256, 128))
    has_res = res is not None

    def body(*refs):
        if has_res:
            x_ref, w_ref, r_ref, o_ref = refs
        else:
            x_ref, w_ref, o_ref = refs
        xv = x_ref[...]
        y = xv * lax.rsqrt(jnp.mean(xv * xv, axis=-1, keepdims=True) + EPS) * w_ref[...]
        o_ref[...] = y + r_ref[...] if has_res else y

    row = pl.BlockSpec((tm, d), lambda i: (i, 0))
    in_specs = [row, pl.BlockSpec((1, d), lambda i: (0, 0))]
    args = [x, w]
    if has_res:
        in_specs.append(row)
        args.append(res)
    return pl.pallas_call(
        body, name=name, grid=(m // tm,), in_specs=in_specs, out_specs=row,
        out_shape=jax.ShapeDtypeStruct((m, d), F32),
        compiler_params=_params(dimension_semantics=("arbitrary",)),
    )(*args)


def rms_bwd(x, w, dy, add=None, name="rms_bwd"):
    m, d = x.shape
    tm = _tile(m, (384, 256, 128))
    has_add = add is not None

    def body(*refs):
        if has_add:
            x_ref, w_ref, dy_ref, add_ref, dx_ref, dw_ref = refs
        else:
            x_ref, w_ref, dy_ref, dx_ref, dw_ref = refs
        i = pl.program_id(0)
        xv = x_ref[...]
        dyv = dy_ref[...]
        r = lax.rsqrt(jnp.mean(xv * xv, axis=-1, keepdims=True) + EPS)
        xh = xv * r
        dxh = dyv * w_ref[...]
        dx = r * (dxh - xh * jnp.mean(dxh * xh, axis=-1, keepdims=True))
        dx_ref[...] = dx + add_ref[...] if has_add else dx

        @pl.when(i == 0)
        def _():
            dw_ref[...] = jnp.zeros_like(dw_ref)

        dw_ref[...] += jnp.sum(dyv * xh, axis=0, keepdims=True)

    row = pl.BlockSpec((tm, d), lambda i: (i, 0))
    vec = pl.BlockSpec((1, d), lambda i: (0, 0))
    in_specs = [row, vec, row]
    args = [x, w, dy]
    if has_add:
        in_specs.append(row)
        args.append(add)
    return pl.pallas_call(
        body, name=name, grid=(m // tm,), in_specs=in_specs, out_specs=[row, vec],
        out_shape=[jax.ShapeDtypeStruct((m, d), F32), jax.ShapeDtypeStruct((1, d), F32)],
        compiler_params=_params(dimension_semantics=("arbitrary",)),
    )(*args)


def loss_head(h, target):
    l, d = h.shape
    nblk = l // CHUNK

    def body(h_ref, t_ref, loss_ref, dh_ref, acc_ref):
        i = pl.program_id(0)

        @pl.when(i == 0)
        def _():
            acc_ref[...] = jnp.zeros_like(acc_ref)
            dh_ref[...] = jnp.zeros_like(dh_ref)

        @pl.when(i > 0)
        def _():
            e = h_ref[...] - t_ref[...]
            dh_ref[...] = e / d
            acc_ref[...] += jnp.sum(e * e, axis=0, keepdims=True)

        @pl.when(i == nblk - 1)
        def _():
            loss_ref[...] = jnp.zeros_like(loss_ref) + 0.5 * jnp.sum(acc_ref[...]) / d

    return pl.pallas_call(
        body, name="loss_head", grid=(nblk,),
        in_specs=[pl.BlockSpec((CHUNK, d), lambda i: (i, 0)),
                  pl.BlockSpec((CHUNK, d), lambda i: (jnp.maximum(i - 1, 0), 0))],
        out_specs=[pl.BlockSpec((1, LANES), lambda i: (0, 0)), pl.BlockSpec((CHUNK, d), lambda i: (i, 0))],
        out_shape=[jax.ShapeDtypeStruct((1, LANES), F32), jax.ShapeDtypeStruct((l, d), F32)],
        scratch_shapes=[pltpu.VMEM((1, d), F32)],
        compiler_params=_params(dimension_semantics=("arbitrary",)),
    )(h, target)


SB_BLK = 128


def _sb_tile(l):
    return _tile(l, (384, 256, 128))


def _sb_tri(strict_later):
    r = lax.broadcasted_iota(jnp.int32, (2 * SB_BLK, 2 * SB_BLK), 0) & (SB_BLK - 1)
    c = lax.broadcasted_iota(jnp.int32, (2 * SB_BLK, 2 * SB_BLK), 1)
    keep = (r > c) if strict_later else (r < c)
    return jnp.where(keep | (c >= SB_BLK), 1.0, 0.0).astype(BF16)


def _sb_mask(i, j, t):
    qpos = i * t + lax.broadcasted_iota(jnp.int32, (t, t), 0)
    kpos = j * t + lax.broadcasted_iota(jnp.int32, (t, t), 1)
    return (kpos < qpos) & (kpos >= PAD)


def _sb_scores(q, k, scale, mask):
    z = lax.dot_general(q, k, (((1,), (1,)), ((), ())), preferred_element_type=F32) * scale
    sp = jnp.maximum(z, 0.0) + jnp.log(1.0 + jnp.exp(-jnp.abs(z)))
    lneg = -sp if mask is None else jnp.where(mask, -sp, 0.0)
    return z - sp, lneg


def _sb_block_sums(x, tri, two_parts):
    hi = x.astype(BF16)
    if two_parts:
        lo = (x - hi.astype(F32)).astype(BF16)
        s = jnp.dot(jnp.concatenate([hi, lo], axis=1), tri, preferred_element_type=F32)
    else:
        s = jnp.dot(hi, tri[:SB_BLK], preferred_element_type=F32)
    return s[:, :SB_BLK], s[:, SB_BLK:]


def _sb_walk_down(i, step, carry):
    carry = step(i, carry, True)
    carry = lax.fori_loop(0, jnp.maximum(i - 1, 0), lambda t, c: step(i - 1 - t, c, False), carry)
    return lax.fori_loop(0, jnp.minimum(i, 1), lambda t, c: step(0, c, True), carry)


def _sb_walk_up(i, step, carry):
    carry = lax.fori_loop(0, jnp.minimum(i, 1), lambda t, c: step(0, c, True), carry)
    carry = lax.fori_loop(1, jnp.maximum(i, 1), lambda j, c: step(j, c, False), carry)
    return step(i, carry, True)


def sb_fwd(qkv):
    l = qkv.shape[0]
    t = _sb_tile(l)
    nb = t // SB_BLK
    scale = SB_HEAD_DIM ** -0.5

    def body(q_ref, k_ref, v_ref, o_ref, tot_ref):
        i = pl.program_id(1)
        q = q_ref[...]
        tri = _sb_tri(True)

        def step(j, carry, masked):
            later, acc = carry
            rows = pl.ds(pl.multiple_of(j * t, t), t)
            mask = _sb_mask(i, j, t) if masked else None
            lpos, lneg = _sb_scores(q, k_ref[rows, :], scale, mask)
            ws = [None] * nb
            for b in reversed(range(nb)):
                cols = slice(b * SB_BLK, (b + 1) * SB_BLK)
                within, total = _sb_block_sums(lneg[:, cols], tri, True)
                ws[b] = jnp.exp(lpos[:, cols] + within + later)
                later = later + total
            w = jnp.concatenate(ws, axis=1)
            if masked:
                w = jnp.where(mask, w, 0.0)
            acc = acc + jnp.dot(w.astype(BF16), v_ref[rows, :], preferred_element_type=F32)
            return later, acc

        carry = (jnp.zeros((t, SB_BLK), F32), jnp.zeros((t, SB_HEAD_DIM), F32))
        later, acc = _sb_walk_down(i, step, carry)
        o_ref[...] = acc
        tot_ref[...] = later[:, :1]

    return pl.pallas_call(
        body, name="sb_fwd", grid=(SB_HEADS, l // t),
        in_specs=[pl.BlockSpec((t, SB_HEAD_DIM), lambda h, i: (i, h)),
                  pl.BlockSpec((l, SB_HEAD_DIM), lambda h, i: (0, SB_HEADS + h)),
                  pl.BlockSpec((l, SB_HEAD_DIM), lambda h, i: (0, 2 * SB_HEADS + h))],
        out_specs=[pl.BlockSpec((t, SB_HEAD_DIM), lambda h, i: (i, h)),
                   pl.BlockSpec((None, t, 1), lambda h, i: (h, i, 0))],
        out_shape=[jax.ShapeDtypeStruct((l, D_MODEL), F32), jax.ShapeDtypeStruct((SB_HEADS, l, 1), F32)],
        compiler_params=_params(dimension_semantics=("arbitrary", "arbitrary")),
    )(qkv, qkv, qkv)


def sb_bwd(qkv, row_total, dout):
    l = qkv.shape[0]
    t = _sb_tile(l)
    nb = t // SB_BLK
    nq = l // t
    scale = SB_HEAD_DIM ** -0.5

    def body(q_ref, k_ref, v_ref, tot_ref, do_ref, dq_ref, dk_hbm, dv_hbm, dk_acc, dv_acc):
        h = pl.program_id(0)
        i = pl.program_id(1)

        @pl.when(i == 0)
        def _():
            dk_acc[...] = jnp.zeros_like(dk_acc)
            dv_acc[...] = jnp.zeros_like(dv_acc)

        q = q_ref[...]
        dob = do_ref[...].astype(BF16)
        tri_later = _sb_tri(True)
        tri_before = _sb_tri(False)

        def step(j, carry, masked):
            later, g_before, dq = carry
            rows = pl.ds(pl.multiple_of(j * t, t), t)
            k = k_ref[rows, :]
            v = v_ref[rows, :]
            mask = _sb_mask(i, j, t) if masked else None
            lpos, lneg = _sb_scores(q, k, scale, mask)
            dw = lax.dot_general(dob, v, (((1,), (1,)), ((), ())), preferred_element_type=F32)
            ws, dzs = [None] * nb, [None] * nb
            for b in range(nb):
                cols = slice(b * SB_BLK, (b + 1) * SB_BLK)
                within, total = _sb_block_sums(lneg[:, cols], tri_later, True)
                later = later - total
                wb = jnp.exp(lpos[:, cols] + within + later)
                if masked:
                    wb = jnp.where(mask[:, cols], wb, 0.0)
                g = dw[:, cols] * wb
                g_within, g_total = _sb_block_sums(g, tri_before, False)
                dz = g - (g + g_before + g_within) * jnp.exp(lpos[:, cols])
                if masked:
                    dz = jnp.where(mask[:, cols], dz, 0.0)
                g_before = g_before + g_total
                ws[b] = wb.astype(BF16)
                dzs[b] = (dz * scale).astype(BF16)
            w = jnp.concatenate(ws, axis=1)
            dzb = jnp.concatenate(dzs, axis=1)
            dq = dq + jnp.dot(dzb, k, preferred_element_type=F32)
            dk_acc[rows, :] += lax.dot_general(dzb, q, (((0,), (0,)), ((), ())), preferred_element_type=F32)
            dv_acc[rows, :] += lax.dot_general(w, dob, (((0,), (0,)), ((), ())), preferred_element_type=F32)
            return later, g_before, dq

        carry = (jnp.broadcast_to(tot_ref[...], (t, SB_BLK)), jnp.zeros((t, SB_BLK), F32),
                 jnp.zeros((t, SB_HEAD_DIM), F32))
        _, _, dq = _sb_walk_up(i, step, carry)
        dq_ref[...] = dq.astype(dq_ref.dtype)

        @pl.when(i == nq - 1)
        def _():
            cols = pl.ds(pl.multiple_of(h * SB_HEAD_DIM, SB_HEAD_DIM), SB_HEAD_DIM)
            pltpu.sync_copy(dk_acc, dk_hbm.at[:, cols])
            pltpu.sync_copy(dv_acc, dv_hbm.at[:, cols])

    blk = lambda h, i: (i, h)
    return pl.pallas_call(
        body, name="sb_bwd", grid=(SB_HEADS, nq),
        in_specs=[pl.BlockSpec((t, SB_HEAD_DIM), blk),
                  pl.BlockSpec((l, SB_HEAD_DIM), lambda h, i: (0, SB_HEADS + h)),
                  pl.BlockSpec((l, SB_HEAD_DIM), lambda h, i: (0, 2 * SB_HEADS + h)),
                  pl.BlockSpec((None, t, 1), lambda h, i: (h, i, 0)), pl.BlockSpec((t, SB_HEAD_DIM), blk)],
        out_specs=[pl.BlockSpec((t, SB_HEAD_DIM), blk), pl.BlockSpec(memory_space=pl.ANY),
                   pl.BlockSpec(memory_space=pl.ANY)],
        out_shape=[jax.ShapeDtypeStruct((l, D_MODEL), BF16), jax.ShapeDtypeStruct((l, D_MODEL), F32),
                   jax.ShapeDtypeStruct((l, D_MODEL), F32)],
        scratch_shapes=[pltpu.VMEM((l, SB_HEAD_DIM), F32), pltpu.VMEM((l, SB_HEAD_DIM), F32)],
        compiler_params=_params(dimension_semantics=("arbitrary", "arbitrary")),
    )(qkv, qkv, qkv, row_total, dout)


_HBM = pl.BlockSpec(memory_space=pl.ANY)


def _other_chips(x, y):
    return [(1 - x, y), (x, 1 - y), (1 - x, 1 - y)]


def _comm_call(body, name, ins, out_shapes, n_remote, n_local):
    return pl.pallas_call(
        body, name=name, in_specs=[_HBM] * len(ins), out_specs=[_HBM] * len(out_shapes), out_shape=out_shapes,
        scratch_shapes=[pltpu.SemaphoreType.DMA((n_remote,)), pltpu.SemaphoreType.DMA((n_remote,)),
                        pltpu.SemaphoreType.DMA((max(n_local, 1),))],
    )(*ins)


def gather_shards(shards):
    n = len(shards)

    def body(*refs):
        ins, outs = refs[:n], refs[n:2 * n]
        send_sems, recv_sems, local_sems = refs[2 * n:]
        x, y, c = lax.axis_index("x"), lax.axis_index("y"), lax.axis_index("c")
        me = 2 * x + y
        own = [pltpu.make_async_copy(ins[t], outs[t].at[me], local_sems.at[t]) for t in range(n)]
        for cp in own:
            cp.start()

        def copy(t, k, px, py, slot):
            return pltpu.make_async_remote_copy(
                src_ref=ins[t], dst_ref=outs[t].at[slot], send_sem=send_sems.at[3 * t + k],
                recv_sem=recv_sems.at[3 * t + k], device_id=(px, py, c), device_id_type=MESH)

        chips = _other_chips(x, y)
        sends = [copy(t, k, px, py, me) for t in range(n) for k, (px, py) in enumerate(chips)]
        for cp in sends:
            cp.start()
        for t in range(n):
            for k, (px, py) in enumerate(chips):
                copy(t, k, px, py, 2 * px + py).wait_recv()
        for cp in sends:
            cp.wait_send()
        for cp in own:
            cp.wait()

    out_shapes = [jax.ShapeDtypeStruct((N_CHIPS,) + s.shape, s.dtype) for s in shards]
    return _comm_call(body, "gather_shards", shards, out_shapes, 3 * n, n)


def sibling_swap_halves(gs):
    n = len(gs)

    def body(*refs):
        ins, outs = refs[:n], refs[n:2 * n]
        send_sems, recv_sems, _ = refs[2 * n:]
        x, y, c = lax.axis_index("x"), lax.axis_index("y"), lax.axis_index("c")
        copies = []
        for t in range(n):
            rh = ins[t].shape[1] // 2
            src = ins[t].at[:, pl.ds(pl.multiple_of((1 - c) * rh, 8), rh), :]
            copies.append(pltpu.make_async_remote_copy(
                src_ref=src, dst_ref=outs[t], send_sem=send_sems.at[t], recv_sem=recv_sems.at[t],
                device_id=(x, y, 1 - c), device_id_type=MESH))
        for cp in copies:
            cp.start()
        for cp in copies:
            cp.wait_recv()
        for cp in copies:
            cp.wait_send()

    out_shapes = [jax.ShapeDtypeStruct((g.shape[0], g.shape[1] // 2, g.shape[2]), g.dtype) for g in gs]
    return _comm_call(body, "sibling_swap_halves", gs, out_shapes, n, 0)


def chip_exchange(ps):
    n = len(ps)

    def body(*refs):
        ins, outs = refs[:n], refs[n:2 * n]
        send_sems, recv_sems, local_sems = refs[2 * n:]
        x, y, c = lax.axis_index("x"), lax.axis_index("y"), lax.axis_index("c")
        me = 2 * x + y
        own = [pltpu.make_async_copy(ins[t].at[me], outs[t].at[me], local_sems.at[t]) for t in range(n)]
        for cp in own:
            cp.start()

        def copy(t, k, px, py, src_slot, dst_slot):
            return pltpu.make_async_remote_copy(
                src_ref=ins[t].at[src_slot], dst_ref=outs[t].at[dst_slot], send_sem=send_sems.at[3 * t + k],
                recv_sem=recv_sems.at[3 * t + k], device_id=(px, py, c), device_id_type=MESH)

        chips = _other_chips(x, y)
        sends = [copy(t, k, px, py, 2 * px + py, me) for t in range(n) for k, (px, py) in enumerate(chips)]
        for cp in sends:
            cp.start()
        for t in range(n):
            for k, (px, py) in enumerate(chips):
                copy(t, k, px, py, me, 2 * px + py).wait_recv()
        for cp in sends:
            cp.wait_send()
        for cp in own:
            cp.wait()

    out_shapes = [jax.ShapeDtypeStruct(p.shape, p.dtype) for p in ps]
    return _comm_call(body, "chip_exchange", ps, out_shapes, 3 * n, n)


def sibling_share(ss):
    n = len(ss)

    def body(*refs):
        ins, outs = refs[:n], refs[n:2 * n]
        send_sems, recv_sems, local_sems = refs[2 * n:]
        x, y, c = lax.axis_index("x"), lax.axis_index("y"), lax.axis_index("c")
        own = [pltpu.make_async_copy(ins[t], outs[t].at[c], local_sems.at[t]) for t in range(n)]
        for cp in own:
            cp.start()
        sends = [pltpu.make_async_remote_copy(
            src_ref=ins[t], dst_ref=outs[t].at[c], send_sem=send_sems.at[t], recv_sem=recv_sems.at[t],
            device_id=(x, y, 1 - c), device_id_type=MESH) for t in range(n)]
        for cp in sends:
            cp.start()
        for t in range(n):
            pltpu.make_async_remote_copy(
                src_ref=ins[t], dst_ref=outs[t].at[1 - c], send_sem=send_sems.at[t], recv_sem=recv_sems.at[t],
                device_id=(x, y, 1 - c), device_id_type=MESH).wait_recv()
        for cp in sends:
            cp.wait_send()
        for cp in own:
            cp.wait()

    out_shapes = [jax.ShapeDtypeStruct((2,) + s.shape, s.dtype) for s in ss]
    return _comm_call(body, "sibling_share", ss, out_shapes, n, n)


EW_BLOCK_BYTES = 2 * 1024 * 1024


def _ew_rows(rows, cols, copies=1):
    padded = -(-cols // LANES) * LANES
    for tr in (1024, 512, 256, 128, 64, 32, 16, 8):
        if rows % tr == 0 and copies * tr * padded * 4 <= EW_BLOCK_BYTES:
            return tr
    return rows


def add_pairs(a, b):
    rows, cols = a.shape
    tr = _ew_rows(rows, cols)

    def body(a_ref, b_ref, o_ref):
        o_ref[...] = a_ref[...] + b_ref[...]

    blk = pl.BlockSpec((tr, cols), lambda i: (i, 0))
    return pl.pallas_call(
        body, name="add_pairs", grid=(rows // tr,), in_specs=[blk, blk], out_specs=blk,
        out_shape=jax.ShapeDtypeStruct((rows, cols), F32),
        compiler_params=_params(dimension_semantics=("arbitrary",)),
    )(a, b)


def sum_chips(slots):
    _, rows, cols = slots.shape
    tr = _ew_rows(rows, cols, N_CHIPS)

    def body(s_ref, o_ref):
        acc = s_ref[0]
        for j in range(1, N_CHIPS):
            acc = acc + s_ref[j]
        o_ref[...] = acc

    return pl.pallas_call(
        body, name="sum_chips", grid=(rows // tr,),
        in_specs=[pl.BlockSpec((N_CHIPS, tr, cols), lambda i: (0, i, 0))],
        out_specs=pl.BlockSpec((tr, cols), lambda i: (i, 0)),
        out_shape=jax.ShapeDtypeStruct((rows, cols), F32),
        compiler_params=_params(dimension_semantics=("arbitrary",)),
    )(slots)


def adamw(g, w, m, v):
    rows, cols = g.shape
    tr = _ew_rows(rows, cols)

    def body(g_ref, w_ref, m_ref, v_ref, d_out, m_out, v_out):
        gv = g_ref[...]
        m_new = ADAM_B1 * m_ref[...] + (1.0 - ADAM_B1) * gv
        v_new = ADAM_B2 * v_ref[...] + (1.0 - ADAM_B2) * jnp.square(gv)
        m_hat = m_new / (1.0 - ADAM_B1 ** ADAM_STEP)
        v_hat = v_new / (1.0 - ADAM_B2 ** ADAM_STEP)
        d_out[...] = -ADAM_LR * (m_hat / (jnp.sqrt(v_hat) + ADAM_EPS) + ADAM_WD * w_ref[...])
        m_out[...] = m_new
        v_out[...] = v_new

    blk = pl.BlockSpec((tr, cols), lambda i: (i, 0))
    return pl.pallas_call(
        body, name="adamw", grid=(rows // tr,), in_specs=[blk] * 4, out_specs=[blk] * 3,
        out_shape=[jax.ShapeDtypeStruct((rows, cols), F32)] * 3,
        compiler_params=_params(dimension_semantics=("arbitrary",)),
    )(g, w, m, v)


def reduce_gradients(quarters):
    theirs = sibling_swap_halves(quarters)
    c = lax.axis_index("c")
    chip_partials = []
    for q, t in zip(quarters, theirs):
        four, rh, cols = t.shape
        mine = lax.dynamic_slice_in_dim(q, c * rh, rh, axis=1)
        chip_partials.append(add_pairs(mine.reshape(four * rh, cols), t.reshape(four * rh, cols))
                             .reshape(four, rh, cols))
    slots = chip_exchange(chip_partials)
    halves = sibling_share([sum_chips(s) for s in slots])
    return [h.reshape(2 * h.shape[1], h.shape[2]) for h in halves]


def _pack(pieces, dtype, row_multiple):
    flat = jnp.concatenate([p.astype(dtype).reshape(-1) for p in pieces])
    per = row_multiple * LANES
    padded = -(-flat.shape[0] // per) * per
    flat = jnp.pad(flat, (0, padded - flat.shape[0]))
    return flat.reshape(-1, LANES)


def _unpack(buf, shapes):
    flat = buf.reshape(-1)
    out, off = [], 0
    for s in shapes:
        n = int(np.prod(s))
        out.append(flat[off:off + n].reshape(s))
        off += n
    return out


RET_SCALE = RET_QK_DIM ** -0.5
RET_LOG_GAMMA = [math.log(1.0 - 2.0 ** (-5.0 - h)) for h in range(RET_HEADS)]
RET_HALF = RET_QK_DIM // 2


def _ret_tables(length):
    inv = ROPE_BASE ** (-jnp.arange(RET_HALF, dtype=F32) / RET_HALF)
    ang = jnp.arange(length).astype(F32)[:, None] * inv[None, :]
    log_gamma = jnp.log(1.0 - jnp.power(2.0, -5.0 - jnp.arange(RET_HEADS, dtype=F32)))
    idx = jnp.arange(CHUNK, dtype=F32)
    rel = idx[:, None] - idx[None, :]
    dmask = jnp.where(rel >= 0, jnp.exp(log_gamma[:, None, None] * jnp.maximum(rel, 0.0)), 0.0)
    k_decay = jnp.exp(log_gamma[:, None] * (CHUNK - 1 - idx)[None, :])[:, :, None]
    q_decay = jnp.exp(log_gamma[:, None] * (idx + 1.0)[None, :])[:, :, None]
    return jnp.cos(ang), jnp.sin(ang), dmask, k_decay, q_decay


def _rot(x, cs, sn):
    x1, x2 = x[:, :RET_HALF], x[:, RET_HALF:]
    return jnp.concatenate([x1 * cs - x2 * sn, x1 * sn + x2 * cs], axis=1)


def _unrot(d, cs, sn):
    d1, d2 = d[:, :RET_HALF], d[:, RET_HALF:]
    return jnp.concatenate([d1 * cs + d2 * sn, d2 * cs - d1 * sn], axis=1)


def _sigmoid(x):
    return 1.0 / (1.0 + jnp.exp(-x))


_NT = (((1,), (1,)), ((), ()))
_TN = (((0,), (0,)), ((), ()))


def _ret_specs(nc, rev):
    ch = (lambda c: nc - 1 - c) if rev else (lambda c: c)
    row = lambda w: pl.BlockSpec((CHUNK, w), lambda c: (ch(c), 0))
    const3 = lambda a, b: pl.BlockSpec((RET_HEADS, a, b), lambda c: (0, 0, 0))
    tables = [row(RET_HALF), row(RET_HALF), const3(CHUNK, CHUNK), const3(CHUNK, 1), const3(CHUNK, 1)]
    state = pl.BlockSpec((None, RET_HEADS, RET_QK_DIM, RET_V_DIM), lambda c: (ch(c), 0, 0, 0))
    return row, tables, state


def ret_fwd(p, tables):
    length = p.shape[0]
    nc = length // CHUNK
    row, table_specs, state_spec = _ret_specs(nc, False)

    def body(p_ref, cos_ref, sin_ref, dm_ref, kd_ref, qd_ref, y_ref, ypre_ref, st_ref, r_scr):
        c = pl.program_id(0)

        @pl.when(c == 0)
        def _():
            r_scr[...] = jnp.zeros_like(r_scr)

        cs, sn = cos_ref[...], sin_ref[...]
        valid = (c * CHUNK + lax.broadcasted_iota(jnp.int32, (CHUNK, 1), 0)) >= PAD
        for h in range(RET_HEADS):
            col = lambda part: slice(part * D_MODEL + h * RET_QK_DIM, part * D_MODEL + (h + 1) * RET_QK_DIM)
            qb = _rot(p_ref[:, col(0)], cs, sn).astype(BF16)
            kr = _rot(p_ref[:, col(1)], cs, sn) * RET_SCALE
            kb = kr.astype(BF16)
            vb = jnp.where(valid, p_ref[:, col(2)], 0.0).astype(BF16)
            s = lax.dot_general(qb, kb, _NT, preferred_element_type=F32) * dm_ref[h]
            r = r_scr[h]
            st_ref[h] = r
            y = (jnp.dot(s.astype(BF16), vb, preferred_element_type=F32)
                 + jnp.dot(qb, r.astype(BF16), preferred_element_type=F32) * qd_ref[h])
            kdb = (kr * kd_ref[h]).astype(BF16)
            r_scr[h] = r * math.exp(RET_LOG_GAMMA[h] * CHUNK) + lax.dot_general(kdb, vb, _TN,
                                                                                preferred_element_type=F32)
            out = slice(h * RET_V_DIM, (h + 1) * RET_V_DIM)
            ypre_ref[:, out] = y
            mu = jnp.mean(y, axis=-1, keepdims=True)
            yc = y - mu
            yn = yc * lax.rsqrt(jnp.mean(yc * yc, axis=-1, keepdims=True) + EPS)
            g = p_ref[:, col(3)]
            y_ref[:, out] = yn * (g * _sigmoid(g))

    return pl.pallas_call(
        body, name="ret_fwd", grid=(nc,),
        in_specs=[row(4 * D_MODEL)] + table_specs,
        out_specs=[row(D_MODEL), row(D_MODEL), state_spec],
        out_shape=[jax.ShapeDtypeStruct((length, D_MODEL), F32), jax.ShapeDtypeStruct((length, D_MODEL), F32),
                   jax.ShapeDtypeStruct((nc, RET_HEADS, RET_QK_DIM, RET_V_DIM), F32)],
        scratch_shapes=[pltpu.VMEM((RET_HEADS, RET_QK_DIM, RET_V_DIM), F32)],
        compiler_params=_params(dimension_semantics=("arbitrary",)),
    )(p, *tables)


def ret_bwd(p, tables, ypre, states, dyo):
    length = p.shape[0]
    nc = length // CHUNK
    row, table_specs, state_spec = _ret_specs(nc, True)

    def body(p_ref, cos_ref, sin_ref, dm_ref, kd_ref, qd_ref, ypre_ref, st_ref, dyo_ref, dp_ref, dr_scr):
        c = pl.program_id(0)

        @pl.when(c == 0)
        def _():
            dr_scr[...] = jnp.zeros_like(dr_scr)

        cs, sn = cos_ref[...], sin_ref[...]
        valid = ((nc - 1 - c) * CHUNK + lax.broadcasted_iota(jnp.int32, (CHUNK, 1), 0)) >= PAD
        for h in range(RET_HEADS):
            col = lambda part: slice(part * D_MODEL + h * RET_QK_DIM, part * D_MODEL + (h + 1) * RET_QK_DIM)
            out = slice(h * RET_V_DIM, (h + 1) * RET_V_DIM)
            qb = _rot(p_ref[:, col(0)], cs, sn).astype(BF16)
            kr = _rot(p_ref[:, col(1)], cs, sn) * RET_SCALE
            kb = kr.astype(BF16)
            vb = jnp.where(valid, p_ref[:, col(2)], 0.0).astype(BF16)
            g = p_ref[:, col(3)]
            y = ypre_ref[:, out]
            dyo_h = dyo_ref[:, out]
            mu = jnp.mean(y, axis=-1, keepdims=True)
            yc = y - mu
            rs = lax.rsqrt(jnp.mean(yc * yc, axis=-1, keepdims=True) + EPS)
            xh = yc * rs
            sg = _sigmoid(g)
            dp_ref[:, col(3)] = dyo_h * xh * (sg * (1.0 + g * (1.0 - sg)))
            dyn = dyo_h * (g * sg)
            dy = rs * (dyn - jnp.mean(dyn, axis=-1, keepdims=True)
                       - xh * jnp.mean(dyn * xh, axis=-1, keepdims=True))
            dyb = dy.astype(BF16)
            dm = dm_ref[h]
            sm = (lax.dot_general(qb, kb, _NT, preferred_element_type=F32) * dm).astype(BF16)
            dsb = (lax.dot_general(dyb, vb, _NT, preferred_element_type=F32) * dm).astype(BF16)
            rb = st_ref[h].astype(BF16)
            dyqb = (dy * qd_ref[h]).astype(BF16)
            dr = dr_scr[h]
            drb = dr.astype(BF16)
            kd = kd_ref[h]
            dq = (jnp.dot(dsb, kb, preferred_element_type=F32)
                  + lax.dot_general(dyqb, rb, _NT, preferred_element_type=F32))
            dk = (lax.dot_general(dsb, qb, _TN, preferred_element_type=F32)
                  + lax.dot_general(vb, drb, _NT, preferred_element_type=F32) * kd)
            dv = (lax.dot_general(sm, dyb, _TN, preferred_element_type=F32)
                  + jnp.dot((kr * kd).astype(BF16), drb, preferred_element_type=F32))
            dr_scr[h] = dr * math.exp(RET_LOG_GAMMA[h] * CHUNK) + lax.dot_general(qb, dyqb, _TN,
                                                                                 preferred_element_type=F32)
            dp_ref[:, col(0)] = _unrot(dq, cs, sn)
            dp_ref[:, col(1)] = _unrot(dk, cs, sn) * RET_SCALE
            dp_ref[:, col(2)] = jnp.where(valid, dv, 0.0)

    return pl.pallas_call(
        body, name="ret_bwd", grid=(nc,),
        in_specs=[row(4 * D_MODEL)] + table_specs + [row(D_MODEL), state_spec, row(D_MODEL)],
        out_specs=row(4 * D_MODEL),
        out_shape=jax.ShapeDtypeStruct((length, 4 * D_MODEL), F32),
        scratch_shapes=[pltpu.VMEM((RET_HEADS, RET_QK_DIM, RET_V_DIM), F32)],
        compiler_params=_params(dimension_semantics=("arbitrary",)),
    )(p, *tables, ypre, states, dyo)


def _shift_down(cur, prev, m):
    if m == 0:
        return cur
    rows = lax.broadcasted_iota(jnp.int32, cur.shape, 0)
    return jnp.where(rows < m, pltpu.roll(prev, m, 0), pltpu.roll(cur, m, 0))


def _shift_up(cur, nxt, m):
    if m == 0:
        return cur
    n = cur.shape[0]
    rows = lax.broadcasted_iota(jnp.int32, cur.shape, 0)
    return jnp.where(rows >= n - m, pltpu.roll(nxt, n - m, 0), pltpu.roll(cur, n - m, 0))


def conv_mixer_fwd(p, conv_w):
    length = p.shape[0]
    nc = length // CHUNK
    kt = conv_w.shape[0]

    def body(cur_ref, prev_ref, w_ref, y_ref):
        c = pl.program_id(0)
        rows = lax.broadcasted_iota(jnp.int32, (CHUNK, 1), 0)

        def u_of(ref, blk):
            ok = (blk * CHUNK + rows >= PAD) & (blk >= 0)
            return jnp.where(ok, ref[:, D_MODEL:2 * D_MODEL] * ref[:, 2 * D_MODEL:], 0.0)

        u_cur = u_of(cur_ref, c)
        u_prev = u_of(prev_ref, c - 1)
        acc = jnp.zeros((CHUNK, D_MODEL), F32)
        for i in range(kt):
            acc = acc + _shift_down(u_cur, u_prev, kt - 1 - i) * w_ref[i:i + 1, :]
        y_ref[...] = cur_ref[:, :D_MODEL] * acc

    return pl.pallas_call(
        body, name="conv_mixer_fwd", grid=(nc,),
        in_specs=[pl.BlockSpec((CHUNK, 3 * D_MODEL), lambda c: (c, 0)),
                  pl.BlockSpec((CHUNK, 3 * D_MODEL), lambda c: (jnp.maximum(c - 1, 0), 0)),
                  pl.BlockSpec((kt, D_MODEL), lambda c: (0, 0))],
        out_specs=pl.BlockSpec((CHUNK, D_MODEL), lambda c: (c, 0)),
        out_shape=jax.ShapeDtypeStruct((length, D_MODEL), F32),
        compiler_params=_params(dimension_semantics=("arbitrary",)),
    )(p, p, conv_w)


def conv_mixer_bwd(p, conv_w, dy):
    length = p.shape[0]
    nc = length // CHUNK
    kt = conv_w.shape[0]

    def body(cur_ref, prev_ref, w_ref, dy_ref, dyn_ref, pn_ref, dp_ref, dw_ref):
        c = pl.program_id(0)
        rows = lax.broadcasted_iota(jnp.int32, (CHUNK, 1), 0)

        def u_of(ref, blk):
            ok = (blk * CHUNK + rows >= PAD) & (blk >= 0)
            return jnp.where(ok, ref[:, D_MODEL:2 * D_MODEL] * ref[:, 2 * D_MODEL:], 0.0)

        u_cur = u_of(cur_ref, c)
        u_prev = u_of(prev_ref, c - 1)
        b_gate = cur_ref[:, :D_MODEL]
        dyv = dy_ref[...]
        dconv = dyv * b_gate
        dconv_next = jnp.where(c + 1 < nc, dyn_ref[...] * pn_ref[:, :D_MODEL], 0.0)

        @pl.when(c == 0)
        def _():
            dw_ref[...] = jnp.zeros_like(dw_ref)

        acc = jnp.zeros((CHUNK, D_MODEL), F32)
        du = jnp.zeros((CHUNK, D_MODEL), F32)
        for i in range(kt):
            shifted = _shift_down(u_cur, u_prev, kt - 1 - i)
            acc = acc + shifted * w_ref[i:i + 1, :]
            dw_ref[i:i + 1, :] += jnp.sum(dconv * shifted, axis=0, keepdims=True)
            du = du + _shift_up(dconv, dconv_next, kt - 1 - i) * w_ref[i:i + 1, :]
        du = jnp.where(c * CHUNK + rows >= PAD, du, 0.0)
        dp_ref[:, :D_MODEL] = dyv * acc
        dp_ref[:, D_MODEL:2 * D_MODEL] = du * cur_ref[:, 2 * D_MODEL:]
        dp_ref[:, 2 * D_MODEL:] = du * cur_ref[:, D_MODEL:2 * D_MODEL]

    nxt = lambda c: (jnp.minimum(c + 1, nc - 1), 0)
    return pl.pallas_call(
        body, name="conv_mixer_bwd", grid=(nc,),
        in_specs=[pl.BlockSpec((CHUNK, 3 * D_MODEL), lambda c: (c, 0)),
                  pl.BlockSpec((CHUNK, 3 * D_MODEL), lambda c: (jnp.maximum(c - 1, 0), 0)),
                  pl.BlockSpec((kt, D_MODEL), lambda c: (0, 0)),
                  pl.BlockSpec((CHUNK, D_MODEL), lambda c: (c, 0)),
                  pl.BlockSpec((CHUNK, D_MODEL), nxt),
                  pl.BlockSpec((CHUNK, 3 * D_MODEL), nxt)],
        out_specs=[pl.BlockSpec((CHUNK, 3 * D_MODEL), lambda c: (c, 0)),
                   pl.BlockSpec((kt, D_MODEL), lambda c: (0, 0))],
        out_shape=[jax.ShapeDtypeStruct((length, 3 * D_MODEL), F32), jax.ShapeDtypeStruct((kt, D_MODEL), F32)],
        compiler_params=_params(dimension_semantics=("arbitrary",)),
    )(p, p, conv_w, dy, dy, p)


def merge_fwd(gate_logits, ups):
    length = gate_logits.shape[0]
    tm = _tile(length, (384, 256, 128))

    def body(g_ref, u0, u1, u2, u3, o_ref):
        acc = jnp.zeros((tm, D_MODEL), F32)
        for n, u in enumerate((u0, u1, u2, u3)):
            acc = acc + _sigmoid(g_ref[:, n * D_MODEL:(n + 1) * D_MODEL]) * u[...]
        o_ref[...] = acc

    row = pl.BlockSpec((tm, D_MODEL), lambda i: (i, 0))
    return pl.pallas_call(
        body, name="merge_fwd", grid=(length // tm,),
        in_specs=[pl.BlockSpec((tm, N_BRANCH * D_MODEL), lambda i: (i, 0))] + [row] * N_BRANCH,
        out_specs=row, out_shape=jax.ShapeDtypeStruct((length, D_MODEL), F32),
        compiler_params=_params(dimension_semantics=("arbitrary",)),
    )(gate_logits, *ups)


def merge_bwd(gate_logits, ups, dmerged):
    length = gate_logits.shape[0]
    tm = _tile(length, (384, 256, 128))

    def body(g_ref, u0, u1, u2, u3, dm_ref, dg_ref, d0, d1, d2, d3):
        dm = dm_ref[...]
        for n, (u, du) in enumerate(((u0, d0), (u1, d1), (u2, d2), (u3, d3))):
            cols = slice(n * D_MODEL, (n + 1) * D_MODEL)
            s = _sigmoid(g_ref[:, cols])
            du[...] = dm * s
            dg_ref[:, cols] = dm * u[...] * (s * (1.0 - s))

    row = pl.BlockSpec((tm, D_MODEL), lambda i: (i, 0))
    wide = pl.BlockSpec((tm, N_BRANCH * D_MODEL), lambda i: (i, 0))
    outs = pl.pallas_call(
        body, name="merge_bwd", grid=(length // tm,),
        in_specs=[wide] + [row] * (N_BRANCH + 1),
        out_specs=[wide] + [row] * N_BRANCH,
        out_shape=[jax.ShapeDtypeStruct((length, N_BRANCH * D_MODEL), F32)]
        + [jax.ShapeDtypeStruct((length, D_MODEL), F32)] * N_BRANCH,
        compiler_params=_params(dimension_semantics=("arbitrary",)),
    )(gate_logits, *ups, dmerged)
    return outs[0], list(outs[1:])


def swiglu_fwd(f):
    length = f.shape[0]
    tm = _tile(length, (384, 256, 128))

    def body(f_ref, o_ref):
        a = f_ref[:, :D_FF]
        o_ref[...] = a * _sigmoid(a) * f_ref[:, D_FF:]

    return pl.pallas_call(
        body, name="swiglu_fwd", grid=(length // tm,),
        in_specs=[pl.BlockSpec((tm, 2 * D_FF), lambda i: (i, 0))],
        out_specs=pl.BlockSpec((tm, D_FF), lambda i: (i, 0)),
        out_shape=jax.ShapeDtypeStruct((length, D_FF), F32),
        compiler_params=_params(dimension_semantics=("arbitrary",)),
    )(f)


def swiglu_bwd(f, dact):
    length = f.shape[0]
    tm = _tile(length, (384, 256, 128))

    def body(f_ref, d_ref, df_ref):
        a = f_ref[:, :D_FF]
        up = f_ref[:, D_FF:]
        d = d_ref[...]
        s = _sigmoid(a)
        df_ref[:, :D_FF] = d * up * (s * (1.0 + a * (1.0 - s)))
        df_ref[:, D_FF:] = d * (a * s)

    return pl.pallas_call(
        body, name="swiglu_bwd", grid=(length // tm,),
        in_specs=[pl.BlockSpec((tm, 2 * D_FF), lambda i: (i, 0)), pl.BlockSpec((tm, D_FF), lambda i: (i, 0))],
        out_specs=pl.BlockSpec((tm, 2 * D_FF), lambda i: (i, 0)),
        out_shape=jax.ShapeDtypeStruct((length, 2 * D_FF), F32),
        compiler_params=_params(dimension_semantics=("arbitrary",)),
    )(f, dact)


SSD_PAIRS = SSD_HEADS // 2
SSD_XBC = SSD_CONV_DIM
SSD_GW = SSD_INNER // SSD_GROUPS


def _split3(x):
    h1 = x.astype(BF16)
    r1 = x - h1.astype(F32)
    h2 = r1.astype(BF16)
    h3 = (r1 - h2.astype(F32)).astype(BF16)
    return h1, h2, h3


def _tri_apply(tri, x, dims):
    out = None
    for part in _split3(x):
        t = lax.dot_general(tri, part, dims, preferred_element_type=F32)
        out = t if out is None else out + t
    return out


def _softplus(x):
    return jnp.maximum(x, 0.0) + jnp.log(1.0 + jnp.exp(-jnp.abs(x)))


def _lane_pair(x, pair):
    lanes = lax.broadcasted_iota(jnp.int32, (x.shape[0], LANES), 1)
    return jnp.where(lanes < SSD_HEAD_DIM, x[:, 2 * pair:2 * pair + 1], x[:, 2 * pair + 1:2 * pair + 2])


def _half_sums(t):
    lanes = lax.broadcasted_iota(jnp.int32, t.shape, 1)
    lo = jnp.sum(jnp.where(lanes < SSD_HEAD_DIM, t, 0.0), axis=1, keepdims=True)
    return lo, jnp.sum(t, axis=1, keepdims=True) - lo


def _put_cols(cols):
    rows = cols[0].shape[0]
    lanes = lax.broadcasted_iota(jnp.int32, (rows, LANES), 1)
    out = jnp.zeros((rows, LANES), F32)
    for h, col in enumerate(cols):
        out = out + jnp.where(lanes == h, col, 0.0)
    return out


def ssd_pre_fwd(p, dt_raw, conv_w, conv_b, dt_bias, a_log):
    length = p.shape[0]
    nc = length // CHUNK
    kt = conv_w.shape[0]

    def body(cur_ref, prev_ref, raw_ref, w_ref, b_ref, bias_ref, alog_ref, act_ref, dt_ref, a_ref):
        c = pl.program_id(0)
        rows = lax.broadcasted_iota(jnp.int32, (CHUNK, 1), 0)
        vm = c * CHUNK + rows >= PAD
        u_cur = jnp.where(vm, cur_ref[:, SSD_INNER:], 0.0)
        u_prev = jnp.where(((c - 1) * CHUNK + rows >= PAD) & (c >= 1), prev_ref[:, SSD_INNER:], 0.0)
        pre = jnp.zeros((CHUNK, SSD_XBC), F32) + b_ref[...]
        for i in range(kt):
            pre = pre + _shift_down(u_cur, u_prev, kt - 1 - i) * w_ref[i:i + 1, :]
        act = pre * _sigmoid(pre)
        act_ref[:, :SSD_INNER] = jnp.where(vm, act[:, :SSD_INNER], 0.0)
        act_ref[:, SSD_INNER:] = act[:, SSD_INNER:]
        dt = _softplus(raw_ref[...] + bias_ref[...])
        dt_ref[...] = dt
        a_ref[...] = -jnp.exp(alog_ref[...]) * dt

    row = lambda w: pl.BlockSpec((CHUNK, w), lambda c: (c, 0))
    vec = lambda w: pl.BlockSpec((1, w), lambda c: (0, 0))
    return pl.pallas_call(
        body, name="ssd_pre_fwd", grid=(nc,),
        in_specs=[row(3 * D_MODEL), pl.BlockSpec((CHUNK, 3 * D_MODEL), lambda c: (jnp.maximum(c - 1, 0), 0)),
                  row(LANES), pl.BlockSpec((kt, SSD_XBC), lambda c: (0, 0)), vec(SSD_XBC), vec(LANES), vec(LANES)],
        out_specs=[row(SSD_XBC), row(LANES), row(LANES)],
        out_shape=[jax.ShapeDtypeStruct((length, SSD_XBC), F32), jax.ShapeDtypeStruct((length, LANES), F32),
                   jax.ShapeDtypeStruct((length, LANES), F32)],
        compiler_params=_params(dimension_semantics=("arbitrary",)),
    )(p, p, dt_raw, conv_w, conv_b, dt_bias, a_log)


def ssd_pre_bwd(p, dt_raw, conv_w, conv_b, dt_bias, a_log, dact, ddt, da, dz):
    length = p.shape[0]
    nc = length // CHUNK
    kt = conv_w.shape[0]

    def body(cur_ref, prev_ref, raw_ref, w_ref, b_ref, bias_ref, alog_ref, dact_ref, ddt_ref, da_ref, dz_ref,
             dp_ref, draw_ref, dw_ref, db_ref, dbias_ref, dalog_ref, dpre_next):
        step = pl.program_id(0)
        c = nc - 1 - step
        rows = lax.broadcasted_iota(jnp.int32, (CHUNK, 1), 0)
        vm = c * CHUNK + rows >= PAD

        @pl.when(step == 0)
        def _():
            dpre_next[...] = jnp.zeros_like(dpre_next)
            dw_ref[...] = jnp.zeros_like(dw_ref)
            db_ref[...] = jnp.zeros_like(db_ref)
            dbias_ref[...] = jnp.zeros_like(dbias_ref)
            dalog_ref[...] = jnp.zeros_like(dalog_ref)

        u_cur = jnp.where(vm, cur_ref[:, SSD_INNER:], 0.0)
        u_prev = jnp.where(((c - 1) * CHUNK + rows >= PAD) & (c >= 1), prev_ref[:, SSD_INNER:], 0.0)
        shifted = [_shift_down(u_cur, u_prev, kt - 1 - i) for i in range(kt)]
        pre = jnp.zeros((CHUNK, SSD_XBC), F32) + b_ref[...]
        for i in range(kt):
            pre = pre + shifted[i] * w_ref[i:i + 1, :]
        sg = _sigmoid(pre)
        lanes = lax.broadcasted_iota(jnp.int32, (CHUNK, SSD_XBC), 1)
        dact_v = jnp.where(vm | (lanes >= SSD_INNER), dact_ref[...], 0.0)
        dpre = dact_v * (sg * (1.0 + pre * (1.0 - sg)))
        db_ref[...] += jnp.sum(dpre, axis=0, keepdims=True)
        nxt = dpre_next[...]
        du = jnp.zeros((CHUNK, SSD_XBC), F32)
        for i in range(kt):
            dw_ref[i:i + 1, :] += jnp.sum(dpre * shifted[i], axis=0, keepdims=True)
            du = du + _shift_up(dpre, nxt, kt - 1 - i) * w_ref[i:i + 1, :]
        dpre_next[...] = dpre
        dp_ref[:, :SSD_INNER] = dz_ref[...]
        dp_ref[:, SSD_INNER:] = jnp.where(vm, du, 0.0)
        x = raw_ref[...] + bias_ref[...]
        neg_exp = -jnp.exp(alog_ref[...])
        dav = da_ref[...]
        draw = (ddt_ref[...] + dav * neg_exp) * _sigmoid(x)
        draw_ref[...] = draw
        dbias_ref[...] += jnp.sum(draw, axis=0, keepdims=True)
        dalog_ref[...] += jnp.sum(dav * (neg_exp * _softplus(x)), axis=0, keepdims=True)

    rev = lambda c: (nc - 1 - c, 0)
    row = lambda w: pl.BlockSpec((CHUNK, w), rev)
    vec = lambda w: pl.BlockSpec((1, w), lambda c: (0, 0))
    taps = pl.BlockSpec((kt, SSD_XBC), lambda c: (0, 0))
    return pl.pallas_call(
        body, name="ssd_pre_bwd", grid=(nc,),
        in_specs=[row(3 * D_MODEL),
                  pl.BlockSpec((CHUNK, 3 * D_MODEL), lambda c: (jnp.maximum(nc - 2 - c, 0), 0)),
                  row(LANES), taps, vec(SSD_XBC), vec(LANES), vec(LANES),
                  row(SSD_XBC), row(LANES), row(LANES), row(SSD_INNER)],
        out_specs=[row(3 * D_MODEL), row(LANES), taps, vec(SSD_XBC), vec(LANES), vec(LANES)],
        out_shape=[jax.ShapeDtypeStruct((length, 3 * D_MODEL), F32), jax.ShapeDtypeStruct((length, LANES), F32),
                   jax.ShapeDtypeStruct((kt, SSD_XBC), F32), jax.ShapeDtypeStruct((1, SSD_XBC), F32),
                   jax.ShapeDtypeStruct((1, LANES), F32), jax.ShapeDtypeStruct((1, LANES), F32)],
        scratch_shapes=[pltpu.VMEM((CHUNK, SSD_XBC), F32)],
        compiler_params=_params(dimension_semantics=("arbitrary",)),
    )(p, p, dt_raw, conv_w, conv_b, dt_bias, a_log, dact, ddt, da, dz)


def _tri_apply_lhs_t(x, tri):
    out = None
    for part in _split3(x):
        t = lax.dot_general(part, tri, (((0,), (1,)), ((), ())), preferred_element_type=F32)
        out = t if out is None else out + t
    return out


def ssd_core_fwd(act, dt, a, d_skip):
    length = act.shape[0]
    nc = length // CHUNK

    def body(act_ref, dt_ref, a_ref, dskip_ref, y_ref, st_ref, h_scr):
        c = pl.program_id(0)

        @pl.when(c == 0)
        def _():
            h_scr[...] = jnp.zeros_like(h_scr)

        r = lax.broadcasted_iota(jnp.int32, (CHUNK, CHUNK), 0)
        s = lax.broadcasted_iota(jnp.int32, (CHUNK, CHUNK), 1)
        causal = r >= s
        incl = jnp.where(causal, 1.0, 0.0).astype(BF16)
        a_v = a_ref[...]
        acs = _tri_apply(incl, a_v, (((1,), (0,)), ((), ())))
        acs_t = _tri_apply_lhs_t(a_v, incl)
        dt_v = dt_ref[...]
        lanes = lax.broadcasted_iota(jnp.int32, (CHUNK, LANES), 1)
        low = lanes < SSD_HEAD_DIM
        for g in range(SSD_GROUPS):
            bg = act_ref[:, SSD_INNER + g * SSD_STATE:SSD_INNER + (g + 1) * SSD_STATE].astype(BF16)
            cg = act_ref[:, SSD_INNER + (SSD_GROUPS + g) * SSD_STATE:
                         SSD_INNER + (SSD_GROUPS + g + 1) * SSD_STATE].astype(BF16)
            cb = lax.dot_general(cg, bg, _NT, preferred_element_type=F32)
            for pair in (2 * g, 2 * g + 1):
                cols = slice(pair * LANES, (pair + 1) * LANES)
                xs = act_ref[:, cols]
                x = xs * _lane_pair(dt_v, pair)
                ydiag = jnp.zeros((CHUNK, LANES), F32)
                for k, keep in ((0, low), (1, ~low)):
                    h = 2 * pair + k
                    seg = jnp.where(causal, jnp.exp(acs[:, h:h + 1] - acs_t[h:h + 1, :]), 0.0)
                    ydiag = ydiag + jnp.dot((cb * seg).astype(BF16), jnp.where(keep, x, 0.0).astype(BF16),
                                            preferred_element_type=F32)
                acs_p = _lane_pair(acs, pair)
                last = acs_p[CHUNK - 1:CHUNK, :]
                xds = (x * jnp.exp(last - acs_p)).astype(BF16)
                hprev = h_scr[pair]
                st_ref[pair] = hprev
                yoff = lax.dot_general(cg, hprev.astype(BF16), _NT, preferred_element_type=F32) * jnp.exp(acs_p)
                prow = lax.broadcasted_iota(jnp.int32, (LANES, 1), 0)
                cd = jnp.where(prow < SSD_HEAD_DIM, jnp.exp(acs_t[2 * pair:2 * pair + 1, CHUNK - 1:CHUNK]),
                               jnp.exp(acs_t[2 * pair + 1:2 * pair + 2, CHUNK - 1:CHUNK]))
                h_scr[pair] = hprev * cd + lax.dot_general(xds, bg, _TN, preferred_element_type=F32)
                y_ref[:, cols] = ydiag + yoff + xs * dskip_ref[:, cols]

    row = lambda w: pl.BlockSpec((CHUNK, w), lambda c: (c, 0))
    return pl.pallas_call(
        body, name="ssd_core_fwd", grid=(nc,),
        in_specs=[row(SSD_XBC), row(LANES), row(LANES), pl.BlockSpec((1, SSD_INNER), lambda c: (0, 0))],
        out_specs=[row(SSD_INNER), pl.BlockSpec((None, SSD_PAIRS, LANES, SSD_STATE), lambda c: (c, 0, 0, 0))],
        out_shape=[jax.ShapeDtypeStruct((length, SSD_INNER), F32),
                   jax.ShapeDtypeStruct((nc, SSD_PAIRS, LANES, SSD_STATE), F32)],
        scratch_shapes=[pltpu.VMEM((SSD_PAIRS, LANES, SSD_STATE), F32)],
        compiler_params=_params(dimension_semantics=("arbitrary",)),
    )(act, dt, a, d_skip)


def ssd_core_bwd(act, dt, a, d_skip, states, dy):
    length = act.shape[0]
    nc = length // CHUNK

    def body(act_ref, dt_ref, a_ref, dskip_ref, st_ref, dy_ref, dact_ref, ddt_ref, da_ref, dds_ref, dh_scr):
        step = pl.program_id(0)

        @pl.when(step == 0)
        def _():
            dh_scr[...] = jnp.zeros_like(dh_scr)
            dds_ref[...] = jnp.zeros_like(dds_ref)

        r = lax.broadcasted_iota(jnp.int32, (CHUNK, CHUNK), 0)
        s = lax.broadcasted_iota(jnp.int32, (CHUNK, CHUNK), 1)
        causal = r >= s
        incl = jnp.where(causal, 1.0, 0.0).astype(BF16)
        a_v = a_ref[...]
        acs = _tri_apply(incl, a_v, (((1,), (0,)), ((), ())))
        acs_t = _tri_apply_lhs_t(a_v, incl)
        dt_v = dt_ref[...]
        lanes = lax.broadcasted_iota(jnp.int32, (CHUNK, LANES), 1)
        low = lanes < SSD_HEAD_DIM
        prow = lax.broadcasted_iota(jnp.int32, (LANES, 1), 0)
        is_last = lax.broadcasted_iota(jnp.int32, (CHUNK, 1), 0) == CHUNK - 1
        dacs_cols = [None] * SSD_HEADS
        dacs_rows = [None] * SSD_HEADS
        ddt_cols = [None] * SSD_HEADS
        for g in range(SSD_GROUPS):
            b_cols = slice(SSD_INNER + g * SSD_STATE, SSD_INNER + (g + 1) * SSD_STATE)
            c_cols = slice(SSD_INNER + (SSD_GROUPS + g) * SSD_STATE, SSD_INNER + (SSD_GROUPS + g + 1) * SSD_STATE)
            bg = act_ref[:, b_cols].astype(BF16)
            cg = act_ref[:, c_cols].astype(BF16)
            cb = lax.dot_general(cg, bg, _NT, preferred_element_type=F32)
            dcb = jnp.zeros((CHUNK, CHUNK), F32)
            dbg = jnp.zeros((CHUNK, SSD_STATE), F32)
            dcg = jnp.zeros((CHUNK, SSD_STATE), F32)
            for pair in (2 * g, 2 * g + 1):
                cols = slice(pair * LANES, (pair + 1) * LANES)
                xs = act_ref[:, cols]
                dtp = _lane_pair(dt_v, pair)
                x = xs * dtp
                xb = x.astype(BF16)
                dyv = dy_ref[:, cols]
                dyb = dyv.astype(BF16)
                dds_ref[:, cols] += jnp.sum(dyv * xs, axis=0, keepdims=True)
                acs_p = _lane_pair(acs, pair)
                last = acs_p[CHUNK - 1:CHUNK, :]
                ds = jnp.exp(last - acs_p)
                ea = jnp.exp(acs_p)
                hprev = st_ref[pair]
                hb = hprev.astype(BF16)
                dh = dh_scr[pair]
                dhb = dh.astype(BF16)
                dx = jnp.zeros((CHUNK, LANES), F32)
                for k, keep in ((0, low), (1, ~low)):
                    h = 2 * pair + k
                    seg = jnp.where(causal, jnp.exp(acs[:, h:h + 1] - acs_t[h:h + 1, :]), 0.0)
                    lmat = cb * seg
                    dl = lax.dot_general(jnp.where(keep, dyv, 0.0).astype(BF16), xb, _NT,
                                         preferred_element_type=F32)
                    dcb = dcb + dl * seg
                    t = dl * lmat
                    dacs_cols[h] = jnp.sum(t, axis=1, keepdims=True)
                    dacs_rows[h] = jnp.sum(t, axis=0, keepdims=True)
                    dx = dx + jnp.where(keep, lax.dot_general(lmat.astype(BF16), dyb, _TN,
                                                              preferred_element_type=F32), 0.0)
                yoff = lax.dot_general(cg, hb, _NT, preferred_element_type=F32) * ea
                dm = (dyv * ea).astype(BF16)
                dcg = dcg + jnp.dot(dm, hb, preferred_element_type=F32)
                dxds = lax.dot_general(bg, dhb, _NT, preferred_element_type=F32)
                xds = x * ds
                dbg = dbg + jnp.dot(xds.astype(BF16), dhb, preferred_element_type=F32)
                dx = dx + dxds * ds
                t_ds = dxds * xds
                e_a = jnp.exp(acs_t[2 * pair:2 * pair + 1, CHUNK - 1:CHUNK])
                e_b = jnp.exp(acs_t[2 * pair + 1:2 * pair + 2, CHUNK - 1:CHUNK])
                cd = jnp.where(prow < SSD_HEAD_DIM, e_a, e_b)
                hd = dh * hprev
                dcd_a = jnp.sum(jnp.where(prow < SSD_HEAD_DIM, hd, 0.0), keepdims=True)
                dcd_b = jnp.sum(hd, keepdims=True) - dcd_a
                dh_scr[pair] = dh * cd + lax.dot_general(dm, cg, _TN, preferred_element_type=F32)
                col_lo, col_hi = _half_sums(dyv * yoff - t_ds)
                tot_lo, tot_hi = _half_sums(jnp.sum(t_ds, axis=0, keepdims=True))
                dacs_cols[2 * pair] += col_lo + jnp.where(is_last, tot_lo + dcd_a.reshape(1, 1) * e_a, 0.0)
                dacs_cols[2 * pair + 1] += col_hi + jnp.where(is_last, tot_hi + dcd_b.reshape(1, 1) * e_b, 0.0)
                dact_ref[:, cols] = dyv * dskip_ref[:, cols] + dx * dtp
                ddt_cols[2 * pair], ddt_cols[2 * pair + 1] = _half_sums(dx * xs)
            dcbb = dcb.astype(BF16)
            dact_ref[:, b_cols] = dbg + lax.dot_general(dcbb, cg, _TN, preferred_element_type=F32)
            dact_ref[:, c_cols] = dcg + jnp.dot(dcbb, bg, preferred_element_type=F32)
        ddt_ref[...] = _put_cols(ddt_cols)
        sub = lax.broadcasted_iota(jnp.int32, (LANES, CHUNK), 0)
        rows_mat = jnp.zeros((LANES, CHUNK), F32)
        for h in range(SSD_HEADS):
            rows_mat = rows_mat + jnp.where(sub == h, dacs_rows[h], 0.0)
        dacs = _put_cols(dacs_cols) - rows_mat.T
        da_ref[...] = _tri_apply(incl, dacs, (((0,), (0,)), ((), ())))

    rev = lambda c: (nc - 1 - c, 0)
    row = lambda w: pl.BlockSpec((CHUNK, w), rev)
    lane_vec = pl.BlockSpec((1, SSD_INNER), lambda c: (0, 0))
    return pl.pallas_call(
        body, name="ssd_core_bwd", grid=(nc,),
        in_specs=[row(SSD_XBC), row(LANES), row(LANES), lane_vec,
                  pl.BlockSpec((None, SSD_PAIRS, LANES, SSD_STATE), lambda c: (nc - 1 - c, 0, 0, 0)),
                  row(SSD_INNER)],
        out_specs=[row(SSD_XBC), row(LANES), row(LANES), lane_vec],
        out_shape=[jax.ShapeDtypeStruct((length, SSD_XBC), F32), jax.ShapeDtypeStruct((length, LANES), F32),
                   jax.ShapeDtypeStruct((length, LANES), F32), jax.ShapeDtypeStruct((1, SSD_INNER), F32)],
        scratch_shapes=[pltpu.VMEM((SSD_PAIRS, LANES, SSD_STATE), F32)],
        compiler_params=_params(dimension_semantics=("arbitrary",)),
    )(act, dt, a, d_skip, states, dy)


def ssd_post_fwd(y, p, norm_w):
    length = y.shape[0]
    tm = _tile(length, (384, 256, 128))

    def body(y_ref, p_ref, w_ref, o_ref):
        for g in range(SSD_GROUPS):
            cols = slice(g * SSD_GW, (g + 1) * SSD_GW)
            z = p_ref[:, cols]
            v = y_ref[:, cols] * (z * _sigmoid(z))
            o_ref[:, cols] = v * lax.rsqrt(jnp.mean(v * v, axis=-1, keepdims=True) + EPS) * w_ref[:, cols]

    return pl.pallas_call(
        body, name="ssd_post_fwd", grid=(length // tm,),
        in_specs=[pl.BlockSpec((tm, SSD_INNER), lambda i: (i, 0)), pl.BlockSpec((tm, SSD_INNER), lambda i: (i, 0)),
                  pl.BlockSpec((1, SSD_INNER), lambda i: (0, 0))],
        out_specs=pl.BlockSpec((tm, SSD_INNER), lambda i: (i, 0)),
        out_shape=jax.ShapeDtypeStruct((length, SSD_INNER), F32),
        compiler_params=_params(dimension_semantics=("arbitrary",)),
    )(y, p, norm_w)


def ssd_post_bwd(y, p, norm_w, dout):
    length = y.shape[0]
    tm = _tile(length, (384, 256, 128))

    def body(y_ref, p_ref, w_ref, do_ref, dy_ref, dz_ref, dw_ref):
        @pl.when(pl.program_id(0) == 0)
        def _():
            dw_ref[...] = jnp.zeros_like(dw_ref)

        for g in range(SSD_GROUPS):
            cols = slice(g * SSD_GW, (g + 1) * SSD_GW)
            z = p_ref[:, cols]
            yv = y_ref[:, cols]
            sg = _sigmoid(z)
            v = yv * (z * sg)
            rs = lax.rsqrt(jnp.mean(v * v, axis=-1, keepdims=True) + EPS)
            vh = v * rs
            do = do_ref[:, cols]
            dw_ref[:, cols] += jnp.sum(do * vh, axis=0, keepdims=True)
            dvh = do * w_ref[:, cols]
            dv = rs * (dvh - vh * jnp.mean(dvh * vh, axis=-1, keepdims=True))
            dy_ref[:, cols] = dv * (z * sg)
            dz_ref[:, cols] = dv * yv * (sg * (1.0 + z * (1.0 - sg)))

    blk = pl.BlockSpec((tm, SSD_INNER), lambda i: (i, 0))
    vec = pl.BlockSpec((1, SSD_INNER), lambda i: (0, 0))
    return pl.pallas_call(
        body, name="ssd_post_bwd", grid=(length // tm,),
        in_specs=[blk, blk, vec, blk], out_specs=[blk, blk, vec],
        out_shape=[jax.ShapeDtypeStruct((length, SSD_INNER), F32), jax.ShapeDtypeStruct((length, SSD_INNER), F32),
                   jax.ShapeDtypeStruct((1, SSD_INNER), F32)],
        compiler_params=_params(dimension_semantics=("arbitrary",)),
    )(y, p, norm_w, dout)


def _ssd_rows(lw):
    pad = lambda v: jnp.pad(v, (0, LANES - SSD_HEADS))[None]
    return dict(conv_w=lw['ssd_conv_w'], conv_b=lw['ssd_conv_b'][None], dt_bias=pad(lw['ssd_dt_bias']),
                a_log=pad(lw['ssd_a_log']), d_skip=jnp.repeat(lw['ssd_d'], SSD_HEAD_DIM)[None],
                norm_w=lw['ssd_norm'][None])


def ssd_fwd(p, dt_raw, rows):
    act, dt, a = ssd_pre_fwd(p, dt_raw, rows['conv_w'], rows['conv_b'], rows['dt_bias'], rows['a_log'])
    y, states = ssd_core_fwd(act, dt, a, rows['d_skip'])
    return ssd_post_fwd(y, p, rows['norm_w']), (act, dt, a, y, states)


def ssd_bwd(p, dt_raw, rows, saved, dout):
    act, dt, a, y, states = saved
    dy, dz, dnorm = ssd_post_bwd(y, p, rows['norm_w'], dout)
    dact, ddt, da, dskip_lanes = ssd_core_bwd(act, dt, a, rows['d_skip'], states, dy)
    dp, draw, dconv_w, dconv_b, dbias, dalog = ssd_pre_bwd(
        p, dt_raw, rows['conv_w'], rows['conv_b'], rows['dt_bias'], rows['a_log'], dact, ddt, da, dz)
    grads = dict(ssd_conv_w=dconv_w, ssd_conv_b=dconv_b[0], ssd_dt_bias=dbias[0, :SSD_HEADS],
                 ssd_a_log=dalog[0, :SSD_HEADS], ssd_norm=dnorm[0],
                 ssd_d=jnp.sum(dskip_lanes.reshape(SSD_HEADS, SSD_HEAD_DIM), axis=1))
    return dp, draw, grads


IN_A = (0, 3 * D_MODEL)
IN_S = (IN_A[1], IN_A[1] + SSD_INNER + SSD_CONV_DIM)
IN_DT = (IN_S[1], IN_S[1] + SSD_HEADS)
IN_R = (IN_DT[1], IN_DT[1] + 4 * D_MODEL)
IN_SB = (IN_R[1], IN_R[1] + 3 * D_MODEL)
IN_G = (IN_SB[1], IN_SB[1] + N_BRANCH * D_MODEL)
IN_WIDTH = IN_G[1]


def _layer_weights(full, small, l):
    w_in = full['w_in'][l]
    cut = lambda r: w_in[:, r[0]:r[1]]
    w_dt = jnp.pad(cut(IN_DT), ((0, 0), (0, DT_PAD - SSD_HEADS)))
    return dict(
        w_a=cut(IN_A), w_s=cut(IN_S), w_dt=w_dt, w_r=cut(IN_R), w_sb=cut(IN_SB), w_g=cut(IN_G),
        w_branch=[full['w_branch'][l, n] for n in range(N_BRANCH)],
        w_out=full['w_out'][l], w_ffn_in=full['w_ffn_in'][l], w_ffn_out=full['w_ffn_out'][l],
        conv_a=full['conv_a'][l], ssd_conv_w=full['ssd_conv_w'][l],
        ssd_conv_b=small['ssd_conv_b'][l], ssd_dt_bias=small['ssd_dt_bias'][l], ssd_a_log=small['ssd_a_log'][l],
        ssd_d=small['ssd_d'][l], ssd_norm=small['ssd_norm'][l],
        n_mix_pre=small['norm_mix_pre'][l][None], n_mix_post=small['norm_mix_post'][l][None],
        n_ffn_pre=small['norm_ffn_pre'][l][None], n_ffn_post=small['norm_ffn_post'][l][None],
    )


def _layer_fwd(h_res, lw, ret_tables):
    s = {'h_res': h_res, 'ret_tables': ret_tables}
    hn = rms_fwd(h_res, lw['n_mix_pre'], name="rms_mix_pre")
    s['hn'] = hn
    p_a = mm_nn(hn, lw['w_a'], name="proj_conv")
    p_s = mm_nn(hn, lw['w_s'], name="proj_ssd")
    p_dt = mm_nn(hn, lw['w_dt'], name="proj_dt")
    p_r = mm_nn(hn, lw['w_r'], name="proj_ret")
    p_sb = mm_nn(hn, lw['w_sb'], out_dtype=BF16, name="proj_sb")
    p_g = mm_nn(hn, lw['w_g'], name="proj_gate")
    y_a = conv_mixer_fwd(p_a, lw['conv_a'])
    s['p_a'] = p_a
    s['ssd_rows'] = _ssd_rows(lw)
    y_b, s['ssd_saved'] = ssd_fwd(p_s, p_dt, s['ssd_rows'])
    s['p_s'], s['p_dt'] = p_s, p_dt
    y_c, s['ret_ypre'], s['ret_states'] = ret_fwd(p_r, ret_tables)
    s['p_r'] = p_r
    y_d, s['sb_total'] = sb_fwd(p_sb)
    s['p_sb'] = p_sb
    ys = [y_a, y_b, y_c, y_d]
    s['ys'] = ys
    ups = [mm_nn(ys[n], lw['w_branch'][n], name="branch_up") for n in range(N_BRANCH)]
    merged = merge_fwd(p_g, ups)
    s['p_g'], s['ups'] = p_g, ups
    s['merged'] = merged
    mix = mm_nn(merged, lw['w_out'], name="mix_out")
    s['mix'] = mix
    h2 = rms_fwd(mix, lw['n_mix_post'], res=h_res, name="rms_mix_post")
    s['h2'] = h2
    hf = rms_fwd(h2, lw['n_ffn_pre'], name="rms_ffn_pre")
    s['hf'] = hf
    f = mm_nn(hf, lw['w_ffn_in'], name="ffn_in")
    act = swiglu_fwd(f)
    s['f'], s['act'] = f, act
    fo = mm_nn(act, lw['w_ffn_out'], name="ffn_out")
    s['fo'] = fo
    return rms_fwd(fo, lw['n_ffn_post'], res=h2, name="rms_ffn_post"), s


def _layer_bwd(dh3, lw, s):
    g = {}
    d_fo, g['norm_ffn_post'] = rms_bwd(s['fo'], lw['n_ffn_post'], dh3, name="rms_ffn_post_bwd")
    d_act = mm_nt(d_fo, lw['w_ffn_out'], name="ffn_out_dx")
    g['w_ffn_out'] = mm_tn(s['act'], d_fo, name="ffn_out_dw")
    df = swiglu_bwd(s['f'], d_act)
    d_hf = mm_nt(df, lw['w_ffn_in'], name="ffn_in_dx")
    g['w_ffn_in'] = mm_tn(s['hf'], df, name="ffn_in_dw")
    dh2, g['norm_ffn_pre'] = rms_bwd(s['h2'], lw['n_ffn_pre'], d_hf, add=dh3, name="rms_ffn_pre_bwd")
    d_mix, g['norm_mix_post'] = rms_bwd(s['mix'], lw['n_mix_post'], dh2, name="rms_mix_post_bwd")
    d_merged = mm_nt(d_mix, lw['w_out'], name="mix_out_dx")
    g['w_out'] = mm_tn(s['merged'], d_mix, name="mix_out_dw")
    dp_g, dups = merge_bwd(s['p_g'], s['ups'], d_merged)
    dys = [mm_nt(dups[n], lw['w_branch'][n], name="branch_dx") for n in range(N_BRANCH)]
    g['w_branch'] = jnp.stack([mm_tn(s['ys'][n], dups[n], name="branch_dw") for n in range(N_BRANCH)])
    dp_a, g['conv_a'] = conv_mixer_bwd(s['p_a'], lw['conv_a'], dys[0])
    dp_s, dp_dt, ssd_grads = ssd_bwd(s['p_s'], s['p_dt'], s['ssd_rows'], s['ssd_saved'], dys[1])
    g.update(ssd_grads)
    dp_r = ret_bwd(s['p_r'], s['ret_tables'], s['ret_ypre'], s['ret_states'], dys[2])
    dq, dk, dv = sb_bwd(s['p_sb'], s['sb_total'], dys[3])
    dp_sb = jnp.concatenate([dq, dk.astype(BF16), dv.astype(BF16)], axis=1)
    hn = s['hn']
    d_hn = None
    dws = []
    for dp, w, nm in ((dp_a, lw['w_a'], "conv"), (dp_s, lw['w_s'], "ssd"), (dp_dt, lw['w_dt'], "dt"),
                      (dp_r, lw['w_r'], "ret"), (dp_sb, lw['w_sb'], "sb"), (dp_g, lw['w_g'], "gate")):
        d_hn = mm_nt(dp, w, acc=d_hn, name="proj_dx")
        dws.append(mm_tn(hn, dp, name="proj_dw"))
    dws[2] = dws[2][:, :SSD_HEADS]
    g['w_in'] = jnp.concatenate(dws, axis=1)
    dh_res, g['norm_mix_pre'] = rms_bwd(s['h_res'], lw['n_mix_pre'], d_hn, add=dh2, name="rms_mix_pre_bwd")
    for k in ('norm_ffn_post', 'norm_ffn_pre', 'norm_mix_post', 'norm_mix_pre'):
        g[k] = g[k][0]
    return dh_res, g


def _quarter(a, axis, j):
    n = a.shape[axis] // N_CHIPS
    return lax.slice_in_dim(a, j * n, (j + 1) * n, axis=axis)


def kernel(x, meta, w_in, conv_a, ssd_conv_w, ssd_conv_b, ssd_dt_bias, ssd_a_log, ssd_d, ssd_norm, w_branch, w_out, w_ffn_in, w_ffn_out, norm_mix_pre, norm_mix_post, norm_ffn_pre, norm_ffn_post, loss_target, m_meta, m_w_in, m_conv_a, m_ssd_conv_w, m_ssd_conv_b, m_ssd_dt_bias, m_ssd_a_log, m_ssd_d, m_ssd_norm, m_w_branch, m_w_out, m_w_ffn_in, m_w_ffn_out, m_norm_mix_pre, m_norm_mix_post, m_norm_ffn_pre, m_norm_ffn_post, v_meta, v_w_in, v_conv_a, v_ssd_conv_w, v_ssd_conv_b, v_ssd_dt_bias, v_ssd_a_log, v_ssd_d, v_ssd_norm, v_w_branch, v_w_out, v_w_ffn_in, v_w_ffn_out, v_norm_mix_pre, v_norm_mix_post, v_norm_ffn_pre, v_norm_ffn_post):
    w_loc = dict(meta=meta, w_in=w_in, conv_a=conv_a, ssd_conv_w=ssd_conv_w, ssd_conv_b=ssd_conv_b,
                 ssd_dt_bias=ssd_dt_bias, ssd_a_log=ssd_a_log, ssd_d=ssd_d, ssd_norm=ssd_norm, w_branch=w_branch,
                 w_out=w_out, w_ffn_in=w_ffn_in, w_ffn_out=w_ffn_out, norm_mix_pre=norm_mix_pre,
                 norm_mix_post=norm_mix_post, norm_ffn_pre=norm_ffn_pre, norm_ffn_post=norm_ffn_post)
    m_loc = dict(meta=m_meta, w_in=m_w_in, conv_a=m_conv_a, ssd_conv_w=m_ssd_conv_w, ssd_conv_b=m_ssd_conv_b,
                 ssd_dt_bias=m_ssd_dt_bias, ssd_a_log=m_ssd_a_log, ssd_d=m_ssd_d, ssd_norm=m_ssd_norm,
                 w_branch=m_w_branch, w_out=m_w_out, w_ffn_in=m_w_ffn_in, w_ffn_out=m_w_ffn_out,
                 norm_mix_pre=m_norm_mix_pre, norm_mix_post=m_norm_mix_post, norm_ffn_pre=m_norm_ffn_pre,
                 norm_ffn_post=m_norm_ffn_post)
    v_loc = dict(meta=v_meta, w_in=v_w_in, conv_a=v_conv_a, ssd_conv_w=v_ssd_conv_w, ssd_conv_b=v_ssd_conv_b,
                 ssd_dt_bias=v_ssd_dt_bias, ssd_a_log=v_ssd_a_log, ssd_d=v_ssd_d, ssd_norm=v_ssd_norm,
                 w_branch=v_w_branch, w_out=v_w_out, w_ffn_in=v_w_ffn_in, w_ffn_out=v_w_ffn_out,
                 norm_mix_pre=v_norm_mix_pre, norm_mix_post=v_norm_mix_post, norm_ffn_pre=v_norm_ffn_pre,
                 norm_ffn_post=v_norm_ffn_post)

    gathered = gather_shards([w_loc[n].astype(BF16) for n in MATMUL_WEIGHTS]
                             + [_pack([w_loc[n] for n in SMALL_SHARDED], F32, 8)])
    full = {}
    for t, n in enumerate(MATMUL_WEIGHTS):
        full[n] = jnp.concatenate([gathered[t][j] for j in range(N_CHIPS)], axis=SHARD_AXIS[n])
    parts_f = [_unpack(gathered[-1][j], [w_loc[n].shape for n in SMALL_SHARDED]) for j in range(N_CHIPS)]
    for t, n in enumerate(SMALL_SHARDED):
        full[n] = jnp.concatenate([parts_f[j][t] for j in range(N_CHIPS)], axis=SHARD_AXIS[n])

    xs = x[0]
    seq = xs.shape[0]
    length = CHUNK + seq
    h = jnp.concatenate([jnp.zeros((PAD, D_MODEL), F32), full['meta'], xs], axis=0)
    lws, saved = [], []
    ret_tables = _ret_tables(length)
    for l in range(DEPTH):
        lw = _layer_weights(full, w_loc, l)
        h, s = _layer_fwd(h, lw, ret_tables)
        lws.append(lw)
        saved.append(s)

    loss_row, dh = loss_head(h, loss_target[0])
    loss = lax.psum(loss_row[0, 0], ("x", "y", "c"))

    layer_grads = [None] * DEPTH
    for l in reversed(range(DEPTH)):
        dh, layer_grads[l] = _layer_bwd(dh, lws[l], saved[l])
    grad_x = dh[CHUNK:][None]
    grads = {n: jnp.stack([layer_grads[l][n] for l in range(DEPTH)]) for n in WEIGHTS if n != 'meta'}
    grads['meta'] = dh[PAD:CHUNK]

    def rows2d(a):
        return a.reshape(-1, a.shape[-1])

    def small_pieces(j):
        return [_quarter(grads[n], SHARD_AXIS[n], j) if n in SHARD_AXIS else grads[n] for n in SMALL_ORDER]

    quarters = [jnp.stack([rows2d(_quarter(grads[n], SHARD_AXIS[n], j)) for j in range(N_CHIPS)])
                for n in MATMUL_WEIGHTS]
    quarters.append(jnp.stack([_pack(small_pieces(j), F32, 16) for j in range(N_CHIPS)]))
    reduced = reduce_gradients(quarters)
    results = {}
    for t, n in enumerate(MATMUL_WEIGHTS):
        shape = w_loc[n].shape
        new = adamw(reduced[t], rows2d(w_loc[n]), rows2d(m_loc[n]), rows2d(v_loc[n]))
        results[n] = [a.reshape(shape) for a in (reduced[t], *new)]
    small_new = adamw(reduced[-1], *[_pack([d[n] for n in SMALL_ORDER], F32, 16) for d in (w_loc, m_loc, v_loc)])
    small_shapes = [w_loc[n].shape for n in SMALL_ORDER]
    for kind, buf in enumerate((reduced[-1], *small_new)):
        for n, piece in zip(SMALL_ORDER, _unpack(buf, small_shapes)):
            results.setdefault(n, [None] * 4)[kind] = piece
    outs = [results[n][kind] for kind in range(4) for n in WEIGHTS]
    return (loss, grad_x, *outs)
```

```python
import functools
import math

import numpy as np
import jax
import jax.numpy as jnp
from jax import lax
from jax.experimental import pallas as pl
from jax.experimental.pallas import tpu as pltpu

F32 = jnp.float32
BF16 = jnp.bfloat16

D_MODEL = 1024
DEPTH = 2
N_META = 16
CHUNK = 128
PAD = CHUNK - N_META
EPS = 1e-6

CONV_A_K = 3
SSD_HEAD_DIM = 64
SSD_HEADS = 16
SSD_INNER = 1024
SSD_GROUPS = 4
SSD_STATE = 128
SSD_CONV_K = 4
SSD_CONV_DIM = SSD_INNER + 2 * SSD_GROUPS * SSD_STATE
RET_HEADS = 4
RET_QK_DIM = 256
RET_V_DIM = 256
RET_WIDTH = 1024
ROPE_BASE = 10000.0
SB_HEADS = 8
SB_HEAD_DIM = 128
N_BRANCH = 4
D_FF = 2816
DT_PAD = 128

ADAM_LR = 0.001
ADAM_B1 = 0.9
ADAM_B2 = 0.999
ADAM_EPS = 1e-08
ADAM_WD = 0.01
ADAM_STEP = 10

N_CHIPS = 4
N_DEV = 8
LANES = 128
VMEM_LIMIT = 56 * 1024 * 1024
MESH = pl.DeviceIdType.MESH

WEIGHTS = ['meta', 'w_in', 'conv_a', 'ssd_conv_w', 'ssd_conv_b', 'ssd_dt_bias', 'ssd_a_log', 'ssd_d',
           'ssd_norm', 'w_branch', 'w_out', 'w_ffn_in', 'w_ffn_out', 'norm_mix_pre', 'norm_mix_post',
           'norm_ffn_pre', 'norm_ffn_post']
SHARD_AXIS = {'meta': 1, 'w_in': 2, 'conv_a': 2, 'ssd_conv_w': 2, 'w_branch': 2, 'w_out': 1,
              'w_ffn_in': 2, 'w_ffn_out': 1}
MATMUL_WEIGHTS = ['w_in', 'w_branch', 'w_out', 'w_ffn_in', 'w_ffn_out']
SMALL_SHARDED = ['meta', 'conv_a', 'ssd_conv_w']
SMALL_ORDER = SMALL_SHARDED + [n for n in WEIGHTS if n not in SHARD_AXIS]


def _params(**kw):
    return pltpu.CompilerParams(vmem_limit_bytes=VMEM_LIMIT, **kw)


def _tile(n, prefs):
    for p in prefs:
        if n % p == 0:
            return p
    return n


MM_VMEM_BUDGET = 40 * 1024 * 1024
MM_ROW_TILES = (2752, 1376, 688, 384, 256, 128)
MM_COL_TILES = (1024, 512, 256, 128)


def _mm_tiles(m, n, cost):
    for tm in MM_ROW_TILES:
        if m % tm:
            continue
        for tn in MM_COL_TILES:
            if n % tn == 0 and cost(tm, tn) <= MM_VMEM_BUDGET:
                return tm, tn
    return _tile(m, (128, 8)), _tile(n, (128,))


def _size(x):
    return jnp.dtype(x.dtype).itemsize


def mm_nn(a, b, out_dtype=F32, name="mm_nn"):
    m, k = a.shape
    n = b.shape[1]
    ob = jnp.dtype(out_dtype).itemsize
    tm, tn = _mm_tiles(m, n, lambda tm, tn: (2 * tm * k * _size(a) + tm * k * 2 + 2 * k * tn * _size(b)
                                              + 2 * tm * tn * ob + tm * tn * 4))

    def body(a_ref, b_ref, o_ref):
        o_ref[...] = jnp.dot(a_ref[...].astype(BF16), b_ref[...].astype(BF16),
                             preferred_element_type=F32).astype(o_ref.dtype)

    return pl.pallas_call(
        body, name=name, grid=(m // tm, n // tn),
        in_specs=[pl.BlockSpec((tm, k), lambda i, j: (i, 0)), pl.BlockSpec((k, tn), lambda i, j: (0, j))],
        out_specs=pl.BlockSpec((tm, tn), lambda i, j: (i, j)),
        out_shape=jax.ShapeDtypeStruct((m, n), out_dtype),
        compiler_params=_params(dimension_semantics=("arbitrary", "arbitrary")),
    )(a, b)


def mm_nt(g, w, acc=None, name="mm_nt"):
    m, n = g.shape
    k = w.shape[0]
    has_acc = acc is not None
    tm, tn = _mm_tiles(m, n, lambda tm, tn: ((2 + 2 * has_acc) * tm * k * 4 + tm * k * 4 + 2 * tm * tn * _size(g)
                                              + tm * tn * 2 + 2 * k * tn * _size(w)))

    def body(*refs):
        if has_acc:
            g_ref, w_ref, acc_ref, o_ref = refs
        else:
            g_ref, w_ref, o_ref = refs
        j = pl.program_id(1)

        @pl.when(j == 0)
        def _():
            o_ref[...] = acc_ref[...] if has_acc else jnp.zeros_like(o_ref)

        o_ref[...] += lax.dot_general(g_ref[...].astype(BF16), w_ref[...].astype(BF16),
                                      (((1,), (1,)), ((), ())), preferred_element_type=F32)

    in_specs = [pl.BlockSpec((tm, tn), lambda i, j: (i, j)), pl.BlockSpec((k, tn), lambda i, j: (0, j))]
    args = [g, w]
    if has_acc:
        in_specs.append(pl.BlockSpec((tm, k), lambda i, j: (i, 0)))
        args.append(acc)
    return pl.pallas_call(
        body, name=name, grid=(m // tm, n // tn),
        in_specs=in_specs,
        out_specs=pl.BlockSpec((tm, k), lambda i, j: (i, 0)),
        out_shape=jax.ShapeDtypeStruct((m, k), F32),
        compiler_params=_params(dimension_semantics=("arbitrary", "arbitrary")),
    )(*args)


def mm_tn(x, g, name="mm_tn"):
    m, k = x.shape
    n = g.shape[1]
    tk = _tile(k, (1024, 1408, 512, 256, 128))
    tm, tn = _mm_tiles(m, n, lambda tm, tn: (3 * tk * tn * 4 + 2 * tm * tk * _size(x) + tm * tk * 2
                                              + 2 * tm * tn * _size(g) + tm * tn * 2))

    def body(x_ref, g_ref, o_ref):
        s = pl.program_id(2)

        @pl.when(s == 0)
        def _():
            o_ref[...] = jnp.zeros_like(o_ref)

        o_ref[...] += lax.dot_general(x_ref[...].astype(BF16), g_ref[...].astype(BF16),
                                      (((0,), (0,)), ((), ())), preferred_element_type=F32)

    return pl.pallas_call(
        body, name=name, grid=(k // tk, n // tn, m // tm),
        in_specs=[pl.BlockSpec((tm, tk), lambda a, b, s: (s, a)), pl.BlockSpec((tm, tn), lambda a, b, s: (s, b))],
        out_specs=pl.BlockSpec((tk, tn), lambda a, b, s: (a, b)),
        out_shape=jax.ShapeDtypeStruct((k, n), F32),
        compiler_params=_params(dimension_semantics=("arbitrary", "arbitrary", "arbitrary")),
    )(x, g)


def rms_fwd(x, w, res=None, out_dtype=F32, name="rms_fwd"):
    m, d = x.shape
    tm = _tile(m, (384, 256, 128))
    has_res = res is not None

    def body(*refs):
        if has_res:
            x_ref, w_ref, r_ref, o_ref = refs
        else:
            x_ref, w_ref, o_ref = refs
        xv = x_ref[...]
        y = xv * lax.rsqrt(jnp.mean(xv * xv, axis=-1, keepdims=True) + EPS) * w_ref[...]
        o_ref[...] = (y + r_ref[...] if has_res else y).astype(o_ref.dtype)

    row = pl.BlockSpec((tm, d), lambda i: (i, 0))
    in_specs = [row, pl.BlockSpec((1, d), lambda i: (0, 0))]
    args = [x, w]
    if has_res:
        in_specs.append(row)
        args.append(res)
    return pl.pallas_call(
        body, name=name, grid=(m // tm,), in_specs=in_specs, out_specs=row,
        out_shape=jax.ShapeDtypeStruct((m, d), out_dtype),
        compiler_params=_params(dimension_semantics=("arbitrary",)),
    )(*args)


def rms_bwd(x, w, dy, add=None, dx_dtype=F32, name="rms_bwd"):
    m, d = x.shape
    tm = _tile(m, (384, 256, 128))
    has_add = add is not None

    def body(*refs):
        if has_add:
            x_ref, w_ref, dy_ref, add_ref, dx_ref, dw_ref = refs
        else:
            x_ref, w_ref, dy_ref, dx_ref, dw_ref = refs
        i = pl.program_id(0)
        xv = x_ref[...]
        dyv = dy_ref[...]
        r = lax.rsqrt(jnp.mean(xv * xv, axis=-1, keepdims=True) + EPS)
        xh = xv * r
        dxh = dyv * w_ref[...]
        dx = r * (dxh - xh * jnp.mean(dxh * xh, axis=-1, keepdims=True))
        dx_ref[...] = (dx + add_ref[...] if has_add else dx).astype(dx_ref.dtype)

        @pl.when(i == 0)
        def _():
            dw_ref[...] = jnp.zeros_like(dw_ref)

        dw_ref[...] += jnp.sum(dyv * xh, axis=0, keepdims=True)

    row = pl.BlockSpec((tm, d), lambda i: (i, 0))
    vec = pl.BlockSpec((1, d), lambda i: (0, 0))
    in_specs = [row, vec, row]
    args = [x, w, dy]
    if has_add:
        in_specs.append(row)
        args.append(add)
    return pl.pallas_call(
        body, name=name, grid=(m // tm,), in_specs=in_specs, out_specs=[row, vec],
        out_shape=[jax.ShapeDtypeStruct((m, d), dx_dtype), jax.ShapeDtypeStruct((1, d), F32)],
        compiler_params=_params(dimension_semantics=("arbitrary",)),
    )(*args)


def loss_head(h, target):
    l, d = h.shape
    nblk = l // CHUNK

    def body(h_ref, t_ref, loss_ref, dh_ref, acc_ref):
        i = pl.program_id(0)

        @pl.when(i == 0)
        def _():
            acc_ref[...] = jnp.zeros_like(acc_ref)
            dh_ref[...] = jnp.zeros_like(dh_ref)

        @pl.when(i > 0)
        def _():
            e = h_ref[...] - t_ref[...]
            dh_ref[...] = e / d
            acc_ref[...] += jnp.sum(e * e, axis=0, keepdims=True)

        @pl.when(i == nblk - 1)
        def _():
            loss_ref[...] = jnp.zeros_like(loss_ref) + 0.5 * jnp.sum(acc_ref[...]) / d

    return pl.pallas_call(
        body, name="loss_head", grid=(nblk,),
        in_specs=[pl.BlockSpec((CHUNK, d), lambda i: (i, 0)),
                  pl.BlockSpec((CHUNK, d), lambda i: (jnp.maximum(i - 1, 0), 0))],
        out_specs=[pl.BlockSpec((1, LANES), lambda i: (0, 0)), pl.BlockSpec((CHUNK, d), lambda i: (i, 0))],
        out_shape=[jax.ShapeDtypeStruct((1, LANES), F32), jax.ShapeDtypeStruct((l, d), F32)],
        scratch_shapes=[pltpu.VMEM((1, d), F32)],
        compiler_params=_params(dimension_semantics=("arbitrary",)),
    )(h, target)


SB_BLK = 128


def _sb_tile(l):
    return _tile(l, (384, 256, 128))


def _sb_tri(strict_later):
    r = lax.broadcasted_iota(jnp.int32, (2 * SB_BLK, 2 * SB_BLK), 0) & (SB_BLK - 1)
    c = lax.broadcasted_iota(jnp.int32, (2 * SB_BLK, 2 * SB_BLK), 1)
    keep = (r > c) if strict_later else (r < c)
    return jnp.where(keep | (c >= SB_BLK), 1.0, 0.0).astype(BF16)


def _sb_mask(i, j, t):
    qpos = i * t + lax.broadcasted_iota(jnp.int32, (t, t), 0)
    kpos = j * t + lax.broadcasted_iota(jnp.int32, (t, t), 1)
    return (kpos < qpos) & (kpos >= PAD)


def _sb_scores(q, k, scale, mask):
    z = lax.dot_general(q, k, (((1,), (1,)), ((), ())), preferred_element_type=F32) * scale
    sp = jnp.maximum(z, 0.0) + jnp.log(1.0 + jnp.exp(-jnp.abs(z)))
    lneg = -sp if mask is None else jnp.where(mask, -sp, 0.0)
    return z - sp, lneg


def _sb_block_sums(x, tri, two_parts):
    hi = x.astype(BF16)
    if two_parts:
        lo = (x - hi.astype(F32)).astype(BF16)
        s = jnp.dot(jnp.concatenate([hi, lo], axis=1), tri, preferred_element_type=F32)
    else:
        s = jnp.dot(hi, tri[:SB_BLK], preferred_element_type=F32)
    return s[:, :SB_BLK], s[:, SB_BLK:]


def _sb_walk_down(i, step, carry):
    carry = step(i, carry, True)
    n = jnp.maximum(i - 1, 0)
    carry = lax.fori_loop(0, n // 2, lambda t, c: step(i - 2 - 2 * t, step(i - 1 - 2 * t, c, False), False), carry)
    carry = lax.fori_loop(0, n % 2, lambda t, c: step(1, c, False), carry)
    return lax.fori_loop(0, jnp.minimum(i, 1), lambda t, c: step(0, c, True), carry)


def _sb_walk_up(i, step, carry):
    carry = lax.fori_loop(0, jnp.minimum(i, 1), lambda t, c: step(0, c, True), carry)
    n = jnp.maximum(i - 1, 0)
    carry = lax.fori_loop(0, n % 2, lambda t, c: step(1, c, False), carry)
    first = 1 + n % 2
    carry = lax.fori_loop(0, n // 2, lambda t, c: step(first + 2 * t + 1, step(first + 2 * t, c, False), False),
                          carry)
    return step(i, carry, True)


def sb_fwd(qkv):
    l = qkv.shape[0]
    t = _sb_tile(l)
    nb = t // SB_BLK
    scale = SB_HEAD_DIM ** -0.5

    def body(q_ref, k_ref, v_ref, o_ref, tot_ref):
        i = pl.program_id(1)
        q = q_ref[...]
        tri = _sb_tri(True)

        def step(j, carry, masked):
            later, acc = carry
            rows = pl.ds(pl.multiple_of(j * t, t), t)
            mask = _sb_mask(i, j, t) if masked else None
            lpos, lneg = _sb_scores(q, k_ref[rows, :], scale, mask)
            ws = [None] * nb
            for b in reversed(range(nb)):
                cols = slice(b * SB_BLK, (b + 1) * SB_BLK)
                within, total = _sb_block_sums(lneg[:, cols], tri, True)
                ws[b] = jnp.exp(lpos[:, cols] + within + later)
                later = later + total
            w = jnp.concatenate(ws, axis=1)
            if masked:
                w = jnp.where(mask, w, 0.0)
            acc = acc + jnp.dot(w.astype(BF16), v_ref[rows, :], preferred_element_type=F32)
            return later, acc

        carry = (jnp.zeros((t, SB_BLK), F32), jnp.zeros((t, SB_HEAD_DIM), F32))
        later, acc = _sb_walk_down(i, step, carry)
        o_ref[...] = acc.astype(o_ref.dtype)
        tot_ref[...] = later[:, :1]

    return pl.pallas_call(
        body, name="sb_fwd", grid=(SB_HEADS, l // t),
        in_specs=[pl.BlockSpec((t, SB_HEAD_DIM), lambda h, i: (i, h)),
                  pl.BlockSpec((l, SB_HEAD_DIM), lambda h, i: (0, SB_HEADS + h)),
                  pl.BlockSpec((l, SB_HEAD_DIM), lambda h, i: (0, 2 * SB_HEADS + h))],
        out_specs=[pl.BlockSpec((t, SB_HEAD_DIM), lambda h, i: (i, h)),
                   pl.BlockSpec((None, t, 1), lambda h, i: (h, i, 0))],
        out_shape=[jax.ShapeDtypeStruct((l, D_MODEL), BF16), jax.ShapeDtypeStruct((SB_HEADS, l, 1), F32)],
        compiler_params=_params(dimension_semantics=("arbitrary", "arbitrary")),
    )(qkv, qkv, qkv)


def sb_bwd(qkv, row_total, dout):
    l = qkv.shape[0]
    t = _sb_tile(l)
    nb = t // SB_BLK
    nq = l // t
    scale = SB_HEAD_DIM ** -0.5

    def body(q_ref, k_ref, v_ref, tot_ref, do_ref, dq_ref, dk_hbm, dv_hbm, dk_acc, dv_acc):
        h = pl.program_id(0)
        i = pl.program_id(1)

        @pl.when(i == 0)
        def _():
            dk_acc[...] = jnp.zeros_like(dk_acc)
            dv_acc[...] = jnp.zeros_like(dv_acc)

        q = q_ref[...]
        dob = do_ref[...].astype(BF16)
        tri_later = _sb_tri(True)
        tri_before = _sb_tri(False)

        def step(j, carry, masked):
            later, g_before, dq = carry
            rows = pl.ds(pl.multiple_of(j * t, t), t)
            k = k_ref[rows, :]
            v = v_ref[rows, :]
            mask = _sb_mask(i, j, t) if masked else None
            lpos, lneg = _sb_scores(q, k, scale, mask)
            dw = lax.dot_general(dob, v, (((1,), (1,)), ((), ())), preferred_element_type=F32)
            ws, dzs = [None] * nb, [None] * nb
            for b in range(nb):
                cols = slice(b * SB_BLK, (b + 1) * SB_BLK)
                within, total = _sb_block_sums(lneg[:, cols], tri_later, True)
                later = later - total
                wb = jnp.exp(lpos[:, cols] + within + later)
                if masked:
                    wb = jnp.where(mask[:, cols], wb, 0.0)
                g = dw[:, cols] * wb
                g_within, g_total = _sb_block_sums(g, tri_before, False)
                dz = g - (g + g_before + g_within) * jnp.exp(lpos[:, cols])
                if masked:
                    dz = jnp.where(mask[:, cols], dz, 0.0)
                g_before = g_before + g_total
                ws[b] = wb.astype(BF16)
                dzs[b] = (dz * scale).astype(BF16)
            w = jnp.concatenate(ws, axis=1)
            dzb = jnp.concatenate(dzs, axis=1)
            dq = dq + jnp.dot(dzb, k, preferred_element_type=F32)
            dk_acc[rows, :] += lax.dot_general(dzb, q, (((0,), (0,)), ((), ())), preferred_element_type=F32)
            dv_acc[rows, :] += lax.dot_general(w, dob, (((0,), (0,)), ((), ())), preferred_element_type=F32)
            return later, g_before, dq

        carry = (jnp.broadcast_to(tot_ref[...], (t, SB_BLK)), jnp.zeros((t, SB_BLK), F32),
                 jnp.zeros((t, SB_HEAD_DIM), F32))
        _, _, dq = _sb_walk_up(i, step, carry)
        dq_ref[...] = dq.astype(dq_ref.dtype)

        @pl.when(i == nq - 1)
        def _():
            cols = pl.ds(pl.multiple_of(h * SB_HEAD_DIM, SB_HEAD_DIM), SB_HEAD_DIM)
            pltpu.sync_copy(dk_acc, dk_hbm.at[:, cols])
            pltpu.sync_copy(dv_acc, dv_hbm.at[:, cols])

    blk = lambda h, i: (i, h)
    return pl.pallas_call(
        body, name="sb_bwd", grid=(SB_HEADS, nq),
        in_specs=[pl.BlockSpec((t, SB_HEAD_DIM), blk),
                  pl.BlockSpec((l, SB_HEAD_DIM), lambda h, i: (0, SB_HEADS + h)),
                  pl.BlockSpec((l, SB_HEAD_DIM), lambda h, i: (0, 2 * SB_HEADS + h)),
                  pl.BlockSpec((None, t, 1), lambda h, i: (h, i, 0)), pl.BlockSpec((t, SB_HEAD_DIM), blk)],
        out_specs=[pl.BlockSpec((t, SB_HEAD_DIM), blk), pl.BlockSpec(memory_space=pl.ANY),
                   pl.BlockSpec(memory_space=pl.ANY)],
        out_shape=[jax.ShapeDtypeStruct((l, D_MODEL), BF16), jax.ShapeDtypeStruct((l, D_MODEL), F32),
                   jax.ShapeDtypeStruct((l, D_MODEL), F32)],
        scratch_shapes=[pltpu.VMEM((l, SB_HEAD_DIM), F32), pltpu.VMEM((l, SB_HEAD_DIM), F32)],
        compiler_params=_params(dimension_semantics=("arbitrary", "arbitrary")),
    )(qkv, qkv, qkv, row_total, dout)


_HBM = pl.BlockSpec(memory_space=pl.ANY)


def _other_chips(x, y):
    return [(1 - x, y), (x, 1 - y), (1 - x, 1 - y)]


def _comm_call(body, name, ins, out_shapes, n_remote, n_local):
    return pl.pallas_call(
        body, name=name, in_specs=[_HBM] * len(ins), out_specs=[_HBM] * len(out_shapes), out_shape=out_shapes,
        scratch_shapes=[pltpu.SemaphoreType.DMA((n_remote,)), pltpu.SemaphoreType.DMA((n_remote,)),
                        pltpu.SemaphoreType.DMA((max(n_local, 1),))],
    )(*ins)


def gather_shards(shards):
    n = len(shards)

    def body(*refs):
        ins, outs = refs[:n], refs[n:2 * n]
        send_sems, recv_sems, local_sems = refs[2 * n:]
        x, y, c = lax.axis_index("x"), lax.axis_index("y"), lax.axis_index("c")
        me = 2 * x + y
        own = [pltpu.make_async_copy(ins[t], outs[t].at[me], local_sems.at[t]) for t in range(n)]
        for cp in own:
            cp.start()

        def copy(t, k, px, py, slot):
            return pltpu.make_async_remote_copy(
                src_ref=ins[t], dst_ref=outs[t].at[slot], send_sem=send_sems.at[3 * t + k],
                recv_sem=recv_sems.at[3 * t + k], device_id=(px, py, c), device_id_type=MESH)

        chips = _other_chips(x, y)
        sends = [copy(t, k, px, py, me) for t in range(n) for k, (px, py) in enumerate(chips)]
        for cp in sends:
            cp.start()
        for t in range(n):
            for k, (px, py) in enumerate(chips):
                copy(t, k, px, py, 2 * px + py).wait_recv()
        for cp in sends:
            cp.wait_send()
        for cp in own:
            cp.wait()

    out_shapes = [jax.ShapeDtypeStruct((N_CHIPS,) + s.shape, s.dtype) for s in shards]
    return _comm_call(body, "gather_shards", shards, out_shapes, 3 * n, n)


def sibling_swap_halves(gs):
    n = len(gs)

    def body(*refs):
        ins, outs = refs[:n], refs[n:2 * n]
        send_sems, recv_sems, _ = refs[2 * n:]
        x, y, c = lax.axis_index("x"), lax.axis_index("y"), lax.axis_index("c")
        copies = []
        for t in range(n):
            rh = ins[t].shape[1] // 2
            src = ins[t].at[:, pl.ds(pl.multiple_of((1 - c) * rh, 8), rh), :]
            copies.append(pltpu.make_async_remote_copy(
                src_ref=src, dst_ref=outs[t], send_sem=send_sems.at[t], recv_sem=recv_sems.at[t],
                device_id=(x, y, 1 - c), device_id_type=MESH))
        for cp in copies:
            cp.start()
        for cp in copies:
            cp.wait_recv()
        for cp in copies:
            cp.wait_send()

    out_shapes = [jax.ShapeDtypeStruct((g.shape[0], g.shape[1] // 2, g.shape[2]), g.dtype) for g in gs]
    return _comm_call(body, "sibling_swap_halves", gs, out_shapes, n, 0)


def chip_exchange(ps):
    n = len(ps)

    def body(*refs):
        ins, outs = refs[:n], refs[n:2 * n]
        send_sems, recv_sems, local_sems = refs[2 * n:]
        x, y, c = lax.axis_index("x"), lax.axis_index("y"), lax.axis_index("c")
        me = 2 * x + y
        own = [pltpu.make_async_copy(ins[t].at[me], outs[t].at[me], local_sems.at[t]) for t in range(n)]
        for cp in own:
            cp.start()

        def copy(t, k, px, py, src_slot, dst_slot):
            return pltpu.make_async_remote_copy(
                src_ref=ins[t].at[src_slot], dst_ref=outs[t].at[dst_slot], send_sem=send_sems.at[3 * t + k],
                recv_sem=recv_sems.at[3 * t + k], device_id=(px, py, c), device_id_type=MESH)

        chips = _other_chips(x, y)
        sends = [copy(t, k, px, py, 2 * px + py, me) for t in range(n) for k, (px, py) in enumerate(chips)]
        for cp in sends:
            cp.start()
        for t in range(n):
            for k, (px, py) in enumerate(chips):
                copy(t, k, px, py, me, 2 * px + py).wait_recv()
        for cp in sends:
            cp.wait_send()
        for cp in own:
            cp.wait()

    out_shapes = [jax.ShapeDtypeStruct(p.shape, p.dtype) for p in ps]
    return _comm_call(body, "chip_exchange", ps, out_shapes, 3 * n, n)


def sibling_share(ss):
    n = len(ss)

    def body(*refs):
        ins, outs = refs[:n], refs[n:2 * n]
        send_sems, recv_sems, local_sems = refs[2 * n:]
        x, y, c = lax.axis_index("x"), lax.axis_index("y"), lax.axis_index("c")
        own = [pltpu.make_async_copy(ins[t], outs[t].at[c], local_sems.at[t]) for t in range(n)]
        for cp in own:
            cp.start()
        sends = [pltpu.make_async_remote_copy(
            src_ref=ins[t], dst_ref=outs[t].at[c], send_sem=send_sems.at[t], recv_sem=recv_sems.at[t],
            device_id=(x, y, 1 - c), device_id_type=MESH) for t in range(n)]
        for cp in sends:
            cp.start()
        for t in range(n):
            pltpu.make_async_remote_copy(
                src_ref=ins[t], dst_ref=outs[t].at[1 - c], send_sem=send_sems.at[t], recv_sem=recv_sems.at[t],
                device_id=(x, y, 1 - c), device_id_type=MESH).wait_recv()
        for cp in sends:
            cp.wait_send()
        for cp in own:
            cp.wait()

    out_shapes = [jax.ShapeDtypeStruct((2,) + s.shape, s.dtype) for s in ss]
    return _comm_call(body, "sibling_share", ss, out_shapes, n, n)


EW_BLOCK_BYTES = 2 * 1024 * 1024


def _ew_rows(rows, cols, copies=1):
    padded = -(-cols // LANES) * LANES
    for tr in (1024, 512, 256, 128, 64, 32, 16, 8):
        if rows % tr == 0 and copies * tr * padded * 4 <= EW_BLOCK_BYTES:
            return tr
    return rows


def add_pairs(a, b):
    rows, cols = a.shape
    tr = _ew_rows(rows, cols)

    def body(a_ref, b_ref, o_ref):
        o_ref[...] = a_ref[...] + b_ref[...]

    blk = pl.BlockSpec((tr, cols), lambda i: (i, 0))
    return pl.pallas_call(
        body, name="add_pairs", grid=(rows // tr,), in_specs=[blk, blk], out_specs=blk,
        out_shape=jax.ShapeDtypeStruct((rows, cols), F32),
        compiler_params=_params(dimension_semantics=("arbitrary",)),
    )(a, b)


def sum_chips(slots):
    _, rows, cols = slots.shape
    tr = _ew_rows(rows, cols, N_CHIPS)

    def body(s_ref, o_ref):
        acc = s_ref[0]
        for j in range(1, N_CHIPS):
            acc = acc + s_ref[j]
        o_ref[...] = acc

    return pl.pallas_call(
        body, name="sum_chips", grid=(rows // tr,),
        in_specs=[pl.BlockSpec((N_CHIPS, tr, cols), lambda i: (0, i, 0))],
        out_specs=pl.BlockSpec((tr, cols), lambda i: (i, 0)),
        out_shape=jax.ShapeDtypeStruct((rows, cols), F32),
        compiler_params=_params(dimension_semantics=("arbitrary",)),
    )(slots)


def adamw(g, w, m, v):
    rows, cols = g.shape
    tr = _ew_rows(rows, cols)

    def body(g_ref, w_ref, m_ref, v_ref, d_out, m_out, v_out):
        gv = g_ref[...]
        m_new = ADAM_B1 * m_ref[...] + (1.0 - ADAM_B1) * gv
        v_new = ADAM_B2 * v_ref[...] + (1.0 - ADAM_B2) * jnp.square(gv)
        m_hat = m_new / (1.0 - ADAM_B1 ** ADAM_STEP)
        v_hat = v_new / (1.0 - ADAM_B2 ** ADAM_STEP)
        d_out[...] = -ADAM_LR * (m_hat / (jnp.sqrt(v_hat) + ADAM_EPS) + ADAM_WD * w_ref[...])
        m_out[...] = m_new
        v_out[...] = v_new

    blk = pl.BlockSpec((tr, cols), lambda i: (i, 0))
    return pl.pallas_call(
        body, name="adamw", grid=(rows // tr,), in_specs=[blk] * 4, out_specs=[blk] * 3,
        out_shape=[jax.ShapeDtypeStruct((rows, cols), F32)] * 3,
        compiler_params=_params(dimension_semantics=("arbitrary",)),
    )(g, w, m, v)


def reduce_gradients(quarters):
    theirs = sibling_swap_halves(quarters)
    c = lax.axis_index("c")
    chip_partials = []
    for q, t in zip(quarters, theirs):
        four, rh, cols = t.shape
        mine = lax.dynamic_slice_in_dim(q, c * rh, rh, axis=1)
        chip_partials.append(add_pairs(mine.reshape(four * rh, cols), t.reshape(four * rh, cols))
                             .reshape(four, rh, cols))
    slots = chip_exchange(chip_partials)
    halves = sibling_share([sum_chips(s) for s in slots])
    return [h.reshape(2 * h.shape[1], h.shape[2]) for h in halves]


def _pack(pieces, dtype, row_multiple):
    flat = jnp.concatenate([p.astype(dtype).reshape(-1) for p in pieces])
    per = row_multiple * LANES
    padded = -(-flat.shape[0] // per) * per
    flat = jnp.pad(flat, (0, padded - flat.shape[0]))
    return flat.reshape(-1, LANES)


def _unpack(buf, shapes):
    flat = buf.reshape(-1)
    out, off = [], 0
    for s in shapes:
        n = int(np.prod(s))
        out.append(flat[off:off + n].reshape(s))
        off += n
    return out


RET_SCALE = RET_QK_DIM ** -0.5
RET_LOG_GAMMA = [math.log(1.0 - 2.0 ** (-5.0 - h)) for h in range(RET_HEADS)]
RET_HALF = RET_QK_DIM // 2


def _ret_tables(length):
    inv = ROPE_BASE ** (-jnp.arange(RET_HALF, dtype=F32) / RET_HALF)
    ang = jnp.arange(length).astype(F32)[:, None] * inv[None, :]
    log_gamma = jnp.log(1.0 - jnp.power(2.0, -5.0 - jnp.arange(RET_HEADS, dtype=F32)))
    idx = jnp.arange(CHUNK, dtype=F32)
    rel = idx[:, None] - idx[None, :]
    dmask = jnp.where(rel >= 0, jnp.exp(log_gamma[:, None, None] * jnp.maximum(rel, 0.0)), 0.0)
    k_decay = jnp.exp(log_gamma[:, None] * (CHUNK - 1 - idx)[None, :])[:, :, None]
    q_decay = jnp.exp(log_gamma[:, None] * (idx + 1.0)[None, :])[:, :, None]
    return jnp.cos(ang), jnp.sin(ang), dmask, k_decay, q_decay


def _rot(x, cs, sn):
    x1, x2 = x[:, :RET_HALF], x[:, RET_HALF:]
    return jnp.concatenate([x1 * cs - x2 * sn, x1 * sn + x2 * cs], axis=1)


def _unrot(d, cs, sn):
    d1, d2 = d[:, :RET_HALF], d[:, RET_HALF:]
    return jnp.concatenate([d1 * cs + d2 * sn, d2 * cs - d1 * sn], axis=1)


def _sigmoid(x):
    return 1.0 / (1.0 + jnp.exp(-x))


_NT = (((1,), (1,)), ((), ()))
_TN = (((0,), (0,)), ((), ()))


def _ret_specs(nc, rev):
    ch = (lambda c: nc - 1 - c) if rev else (lambda c: c)
    row = lambda w: pl.BlockSpec((CHUNK, w), lambda c: (ch(c), 0))
    const3 = lambda a, b: pl.BlockSpec((RET_HEADS, a, b), lambda c: (0, 0, 0))
    tables = [row(RET_HALF), row(RET_HALF), const3(CHUNK, CHUNK), const3(CHUNK, 1), const3(CHUNK, 1)]
    state = pl.BlockSpec((None, RET_HEADS, RET_QK_DIM, RET_V_DIM), lambda c: (ch(c), 0, 0, 0))
    return row, tables, state


def ret_fwd(p, tables):
    length = p.shape[0]
    nc = length // CHUNK
    row, table_specs, state_spec = _ret_specs(nc, False)

    def body(p_ref, cos_ref, sin_ref, dm_ref, kd_ref, qd_ref, y_ref, ypre_ref, st_ref, r_scr):
        c = pl.program_id(0)

        @pl.when(c == 0)
        def _():
            r_scr[...] = jnp.zeros_like(r_scr)

        cs, sn = cos_ref[...], sin_ref[...]
        valid = (c * CHUNK + lax.broadcasted_iota(jnp.int32, (CHUNK, 1), 0)) >= PAD
        for h in range(RET_HEADS):
            col = lambda part: slice(part * D_MODEL + h * RET_QK_DIM, part * D_MODEL + (h + 1) * RET_QK_DIM)
            qb = _rot(p_ref[:, col(0)], cs, sn).astype(BF16)
            kr = _rot(p_ref[:, col(1)], cs, sn) * RET_SCALE
            kb = kr.astype(BF16)
            vb = jnp.where(valid, p_ref[:, col(2)], 0.0).astype(BF16)
            s = lax.dot_general(qb, kb, _NT, preferred_element_type=F32) * dm_ref[h]
            r = r_scr[h]
            st_ref[h] = r
            y = (jnp.dot(s.astype(BF16), vb, preferred_element_type=F32)
                 + jnp.dot(qb, r.astype(BF16), preferred_element_type=F32) * qd_ref[h])
            kdb = (kr * kd_ref[h]).astype(BF16)
            r_scr[h] = r * math.exp(RET_LOG_GAMMA[h] * CHUNK) + lax.dot_general(kdb, vb, _TN,
                                                                                preferred_element_type=F32)
            out = slice(h * RET_V_DIM, (h + 1) * RET_V_DIM)
            ypre_ref[:, out] = y
            mu = jnp.mean(y, axis=-1, keepdims=True)
            yc = y - mu
            yn = yc * lax.rsqrt(jnp.mean(yc * yc, axis=-1, keepdims=True) + EPS)
            g = p_ref[:, col(3)]
            y_ref[:, out] = (yn * (g * _sigmoid(g))).astype(y_ref.dtype)

    return pl.pallas_call(
        body, name="ret_fwd", grid=(nc,),
        in_specs=[row(4 * D_MODEL)] + table_specs,
        out_specs=[row(D_MODEL), row(D_MODEL), state_spec],
        out_shape=[jax.ShapeDtypeStruct((length, D_MODEL), BF16), jax.ShapeDtypeStruct((length, D_MODEL), F32),
                   jax.ShapeDtypeStruct((nc, RET_HEADS, RET_QK_DIM, RET_V_DIM), F32)],
        scratch_shapes=[pltpu.VMEM((RET_HEADS, RET_QK_DIM, RET_V_DIM), F32)],
        compiler_params=_params(dimension_semantics=("arbitrary",)),
    )(p, *tables)


def ret_bwd(p, tables, ypre, states, dyo):
    length = p.shape[0]
    nc = length // CHUNK
    row, table_specs, state_spec = _ret_specs(nc, True)

    def body(p_ref, cos_ref, sin_ref, dm_ref, kd_ref, qd_ref, ypre_ref, st_ref, dyo_ref, dp_ref, dr_scr):
        c = pl.program_id(0)

        @pl.when(c == 0)
        def _():
            dr_scr[...] = jnp.zeros_like(dr_scr)

        cs, sn = cos_ref[...], sin_ref[...]
        valid = ((nc - 1 - c) * CHUNK + lax.broadcasted_iota(jnp.int32, (CHUNK, 1), 0)) >= PAD
        for h in range(RET_HEADS):
            col = lambda part: slice(part * D_MODEL + h * RET_QK_DIM, part * D_MODEL + (h + 1) * RET_QK_DIM)
            out = slice(h * RET_V_DIM, (h + 1) * RET_V_DIM)
            qb = _rot(p_ref[:, col(0)], cs, sn).astype(BF16)
            kr = _rot(p_ref[:, col(1)], cs, sn) * RET_SCALE
            kb = kr.astype(BF16)
            vb = jnp.where(valid, p_ref[:, col(2)], 0.0).astype(BF16)
            g = p_ref[:, col(3)]
            y = ypre_ref[:, out]
            dyo_h = dyo_ref[:, out]
            mu = jnp.mean(y, axis=-1, keepdims=True)
            yc = y - mu
            rs = lax.rsqrt(jnp.mean(yc * yc, axis=-1, keepdims=True) + EPS)
            xh = yc * rs
            sg = _sigmoid(g)
            dp_ref[:, col(3)] = (dyo_h * xh * (sg * (1.0 + g * (1.0 - sg)))).astype(dp_ref.dtype)
            dyn = dyo_h * (g * sg)
            dy = rs * (dyn - jnp.mean(dyn, axis=-1, keepdims=True)
                       - xh * jnp.mean(dyn * xh, axis=-1, keepdims=True))
            dyb = dy.astype(BF16)
            dm = dm_ref[h]
            sm = (lax.dot_general(qb, kb, _NT, preferred_element_type=F32) * dm).astype(BF16)
            dsb = (lax.dot_general(dyb, vb, _NT, preferred_element_type=F32) * dm).astype(BF16)
            rb = st_ref[h].astype(BF16)
            dyqb = (dy * qd_ref[h]).astype(BF16)
            dr = dr_scr[h]
            drb = dr.astype(BF16)
            kd = kd_ref[h]
            dq = (jnp.dot(dsb, kb, preferred_element_type=F32)
                  + lax.dot_general(dyqb, rb, _NT, preferred_element_type=F32))
            dk = (lax.dot_general(dsb, qb, _TN, preferred_element_type=F32)
                  + lax.dot_general(vb, drb, _NT, preferred_element_type=F32) * kd)
            dv = (lax.dot_general(sm, dyb, _TN, preferred_element_type=F32)
                  + jnp.dot((kr * kd).astype(BF16), drb, preferred_element_type=F32))
            dr_scr[h] = dr * math.exp(RET_LOG_GAMMA[h] * CHUNK) + lax.dot_general(qb, dyqb, _TN,
                                                                                 preferred_element_type=F32)
            dp_ref[:, col(0)] = _unrot(dq, cs, sn).astype(dp_ref.dtype)
            dp_ref[:, col(1)] = (_unrot(dk, cs, sn) * RET_SCALE).astype(dp_ref.dtype)
            dp_ref[:, col(2)] = jnp.where(valid, dv, 0.0).astype(dp_ref.dtype)

    return pl.pallas_call(
        body, name="ret_bwd", grid=(nc,),
        in_specs=[row(4 * D_MODEL)] + table_specs + [row(D_MODEL), state_spec, row(D_MODEL)],
        out_specs=row(4 * D_MODEL),
        out_shape=jax.ShapeDtypeStruct((length, 4 * D_MODEL), BF16),
        scratch_shapes=[pltpu.VMEM((RET_HEADS, RET_QK_DIM, RET_V_DIM), F32)],
        compiler_params=_params(dimension_semantics=("arbitrary",)),
    )(p, *tables, ypre, states, dyo)


def _shift_down(cur, prev, m):
    if m == 0:
        return cur
    rows = lax.broadcasted_iota(jnp.int32, cur.shape, 0)
    return jnp.where(rows < m, pltpu.roll(prev, m, 0), pltpu.roll(cur, m, 0))


def _shift_up(cur, nxt, m):
    if m == 0:
        return cur
    n = cur.shape[0]
    rows = lax.broadcasted_iota(jnp.int32, cur.shape, 0)
    return jnp.where(rows >= n - m, pltpu.roll(nxt, n - m, 0), pltpu.roll(cur, n - m, 0))


def conv_mixer_fwd(p, conv_w):
    length = p.shape[0]
    nc = length // CHUNK
    kt = conv_w.shape[0]

    def body(cur_ref, prev_ref, w_ref, y_ref):
        c = pl.program_id(0)
        rows = lax.broadcasted_iota(jnp.int32, (CHUNK, 1), 0)

        def u_of(ref, blk):
            ok = (blk * CHUNK + rows >= PAD) & (blk >= 0)
            return jnp.where(ok, ref[:, D_MODEL:2 * D_MODEL] * ref[:, 2 * D_MODEL:], 0.0)

        u_cur = u_of(cur_ref, c)
        u_prev = u_of(prev_ref, c - 1)
        acc = jnp.zeros((CHUNK, D_MODEL), F32)
        for i in range(kt):
            acc = acc + _shift_down(u_cur, u_prev, kt - 1 - i) * w_ref[i:i + 1, :]
        y_ref[...] = (cur_ref[:, :D_MODEL] * acc).astype(y_ref.dtype)

    return pl.pallas_call(
        body, name="conv_mixer_fwd", grid=(nc,),
        in_specs=[pl.BlockSpec((CHUNK, 3 * D_MODEL), lambda c: (c, 0)),
                  pl.BlockSpec((CHUNK, 3 * D_MODEL), lambda c: (jnp.maximum(c - 1, 0), 0)),
                  pl.BlockSpec((kt, D_MODEL), lambda c: (0, 0))],
        out_specs=pl.BlockSpec((CHUNK, D_MODEL), lambda c: (c, 0)),
        out_shape=jax.ShapeDtypeStruct((length, D_MODEL), BF16),
        compiler_params=_params(dimension_semantics=("arbitrary",)),
    )(p, p, conv_w)


def conv_mixer_bwd(p, conv_w, dy):
    length = p.shape[0]
    nc = length // CHUNK
    kt = conv_w.shape[0]

    def body(cur_ref, prev_ref, w_ref, dy_ref, dyn_ref, pn_ref, dp_ref, dw_ref):
        c = pl.program_id(0)
        rows = lax.broadcasted_iota(jnp.int32, (CHUNK, 1), 0)

        def u_of(ref, blk):
            ok = (blk * CHUNK + rows >= PAD) & (blk >= 0)
            return jnp.where(ok, ref[:, D_MODEL:2 * D_MODEL] * ref[:, 2 * D_MODEL:], 0.0)

        u_cur = u_of(cur_ref, c)
        u_prev = u_of(prev_ref, c - 1)
        b_gate = cur_ref[:, :D_MODEL]
        dyv = dy_ref[...]
        dconv = dyv * b_gate
        dconv_next = jnp.where(c + 1 < nc, dyn_ref[...] * pn_ref[:, :D_MODEL], 0.0)

        @pl.when(c == 0)
        def _():
            dw_ref[...] = jnp.zeros_like(dw_ref)

        acc = jnp.zeros((CHUNK, D_MODEL), F32)
        du = jnp.zeros((CHUNK, D_MODEL), F32)
        for i in range(kt):
            shifted = _shift_down(u_cur, u_prev, kt - 1 - i)
            acc = acc + shifted * w_ref[i:i + 1, :]
            dw_ref[i:i + 1, :] += jnp.sum(dconv * shifted, axis=0, keepdims=True)
            du = du + _shift_up(dconv, dconv_next, kt - 1 - i) * w_ref[i:i + 1, :]
        du = jnp.where(c * CHUNK + rows >= PAD, du, 0.0)
        dp_ref[:, :D_MODEL] = (dyv * acc).astype(dp_ref.dtype)
        dp_ref[:, D_MODEL:2 * D_MODEL] = (du * cur_ref[:, 2 * D_MODEL:]).astype(dp_ref.dtype)
        dp_ref[:, 2 * D_MODEL:] = (du * cur_ref[:, D_MODEL:2 * D_MODEL]).astype(dp_ref.dtype)

    nxt = lambda c: (jnp.minimum(c + 1, nc - 1), 0)
    return pl.pallas_call(
        body, name="conv_mixer_bwd", grid=(nc,),
        in_specs=[pl.BlockSpec((CHUNK, 3 * D_MODEL), lambda c: (c, 0)),
                  pl.BlockSpec((CHUNK, 3 * D_MODEL), lambda c: (jnp.maximum(c - 1, 0), 0)),
                  pl.BlockSpec((kt, D_MODEL), lambda c: (0, 0)),
                  pl.BlockSpec((CHUNK, D_MODEL), lambda c: (c, 0)),
                  pl.BlockSpec((CHUNK, D_MODEL), nxt),
                  pl.BlockSpec((CHUNK, 3 * D_MODEL), nxt)],
        out_specs=[pl.BlockSpec((CHUNK, 3 * D_MODEL), lambda c: (c, 0)),
                   pl.BlockSpec((kt, D_MODEL), lambda c: (0, 0))],
        out_shape=[jax.ShapeDtypeStruct((length, 3 * D_MODEL), BF16), jax.ShapeDtypeStruct((kt, D_MODEL), F32)],
        compiler_params=_params(dimension_semantics=("arbitrary",)),
    )(p, p, conv_w, dy, dy, p)


def merge_fwd(gate_logits, ups):
    length = gate_logits.shape[0]
    tm = _tile(length, (384, 256, 128))

    def body(g_ref, u0, u1, u2, u3, o_ref):
        acc = jnp.zeros((tm, D_MODEL), F32)
        for n, u in enumerate((u0, u1, u2, u3)):
            acc = acc + _sigmoid(g_ref[:, n * D_MODEL:(n + 1) * D_MODEL]) * u[...]
        o_ref[...] = acc.astype(o_ref.dtype)

    row = pl.BlockSpec((tm, D_MODEL), lambda i: (i, 0))
    return pl.pallas_call(
        body, name="merge_fwd", grid=(length // tm,),
        in_specs=[pl.BlockSpec((tm, N_BRANCH * D_MODEL), lambda i: (i, 0))] + [row] * N_BRANCH,
        out_specs=row, out_shape=jax.ShapeDtypeStruct((length, D_MODEL), BF16),
        compiler_params=_params(dimension_semantics=("arbitrary",)),
    )(gate_logits, *ups)


def merge_bwd(gate_logits, ups, dmerged):
    length = gate_logits.shape[0]
    tm = _tile(length, (384, 256, 128))

    def body(g_ref, u0, u1, u2, u3, dm_ref, dg_ref, d0, d1, d2, d3):
        dm = dm_ref[...]
        for n, (u, du) in enumerate(((u0, d0), (u1, d1), (u2, d2), (u3, d3))):
            cols = slice(n * D_MODEL, (n + 1) * D_MODEL)
            s = _sigmoid(g_ref[:, cols])
            du[...] = (dm * s).astype(du.dtype)
            dg_ref[:, cols] = (dm * u[...] * (s * (1.0 - s))).astype(dg_ref.dtype)

    row = pl.BlockSpec((tm, D_MODEL), lambda i: (i, 0))
    wide = pl.BlockSpec((tm, N_BRANCH * D_MODEL), lambda i: (i, 0))
    outs = pl.pallas_call(
        body, name="merge_bwd", grid=(length // tm,),
        in_specs=[wide] + [row] * (N_BRANCH + 1),
        out_specs=[wide] + [row] * N_BRANCH,
        out_shape=[jax.ShapeDtypeStruct((length, N_BRANCH * D_MODEL), BF16)]
        + [jax.ShapeDtypeStruct((length, D_MODEL), BF16)] * N_BRANCH,
        compiler_params=_params(dimension_semantics=("arbitrary",)),
    )(gate_logits, *ups, dmerged)
    return outs[0], list(outs[1:])


def swiglu_fwd(f):
    length = f.shape[0]
    tm = _tile(length, (384, 256, 128))

    def body(f_ref, o_ref):
        a = f_ref[:, :D_FF]
        o_ref[...] = (a * _sigmoid(a) * f_ref[:, D_FF:]).astype(o_ref.dtype)

    return pl.pallas_call(
        body, name="swiglu_fwd", grid=(length // tm,),
        in_specs=[pl.BlockSpec((tm, 2 * D_FF), lambda i: (i, 0))],
        out_specs=pl.BlockSpec((tm, D_FF), lambda i: (i, 0)),
        out_shape=jax.ShapeDtypeStruct((length, D_FF), BF16),
        compiler_params=_params(dimension_semantics=("arbitrary",)),
    )(f)


def swiglu_bwd(f, dact):
    length = f.shape[0]
    tm = _tile(length, (384, 256, 128))

    def body(f_ref, d_ref, df_ref):
        a = f_ref[:, :D_FF]
        up = f_ref[:, D_FF:]
        d = d_ref[...]
        s = _sigmoid(a)
        df_ref[:, :D_FF] = (d * up * (s * (1.0 + a * (1.0 - s)))).astype(df_ref.dtype)
        df_ref[:, D_FF:] = (d * (a * s)).astype(df_ref.dtype)

    return pl.pallas_call(
        body, name="swiglu_bwd", grid=(length // tm,),
        in_specs=[pl.BlockSpec((tm, 2 * D_FF), lambda i: (i, 0)), pl.BlockSpec((tm, D_FF), lambda i: (i, 0))],
        out_specs=pl.BlockSpec((tm, 2 * D_FF), lambda i: (i, 0)),
        out_shape=jax.ShapeDtypeStruct((length, 2 * D_FF), BF16),
        compiler_params=_params(dimension_semantics=("arbitrary",)),
    )(f, dact)


SSD_PAIRS = SSD_HEADS // 2
SSD_XBC = SSD_CONV_DIM
SSD_GW = SSD_INNER // SSD_GROUPS


def _split3(x):
    h1 = x.astype(BF16)
    r1 = x - h1.astype(F32)
    h2 = r1.astype(BF16)
    h3 = (r1 - h2.astype(F32)).astype(BF16)
    return h1, h2, h3


def _tri_apply(tri, x, dims):
    out = None
    for part in _split3(x):
        t = lax.dot_general(tri, part, dims, preferred_element_type=F32)
        out = t if out is None else out + t
    return out


def _softplus(x):
    return jnp.maximum(x, 0.0) + jnp.log(1.0 + jnp.exp(-jnp.abs(x)))


def _lane_pair(x, pair):
    lanes = lax.broadcasted_iota(jnp.int32, (x.shape[0], LANES), 1)
    return jnp.where(lanes < SSD_HEAD_DIM, x[:, 2 * pair:2 * pair + 1], x[:, 2 * pair + 1:2 * pair + 2])


def _half_sums(t):
    lanes = lax.broadcasted_iota(jnp.int32, t.shape, 1)
    lo = jnp.sum(jnp.where(lanes < SSD_HEAD_DIM, t, 0.0), axis=1, keepdims=True)
    return lo, jnp.sum(t, axis=1, keepdims=True) - lo


def _put_cols(cols):
    rows = cols[0].shape[0]
    lanes = lax.broadcasted_iota(jnp.int32, (rows, LANES), 1)
    out = jnp.zeros((rows, LANES), F32)
    for h, col in enumerate(cols):
        out = out + jnp.where(lanes == h, col, 0.0)
    return out


def ssd_pre_fwd(p, dt_raw, conv_w, conv_b, dt_bias, a_log):
    length = p.shape[0]
    nc = length // CHUNK
    kt = conv_w.shape[0]

    def body(cur_ref, prev_ref, raw_ref, w_ref, b_ref, bias_ref, alog_ref, act_ref, dt_ref, a_ref):
        c = pl.program_id(0)
        rows = lax.broadcasted_iota(jnp.int32, (CHUNK, 1), 0)
        vm = c * CHUNK + rows >= PAD
        u_cur = jnp.where(vm, cur_ref[:, SSD_INNER:], 0.0)
        u_prev = jnp.where(((c - 1) * CHUNK + rows >= PAD) & (c >= 1), prev_ref[:, SSD_INNER:], 0.0)
        pre = jnp.zeros((CHUNK, SSD_XBC), F32) + b_ref[...]
        for i in range(kt):
            pre = pre + _shift_down(u_cur, u_prev, kt - 1 - i) * w_ref[i:i + 1, :]
        act = pre * _sigmoid(pre)
        act_ref[:, :SSD_INNER] = jnp.where(vm, act[:, :SSD_INNER], 0.0)
        act_ref[:, SSD_INNER:] = act[:, SSD_INNER:]
        dt = _softplus(raw_ref[...] + bias_ref[...])
        dt_ref[...] = dt
        a_ref[...] = -jnp.exp(alog_ref[...]) * dt

    row = lambda w: pl.BlockSpec((CHUNK, w), lambda c: (c, 0))
    vec = lambda w: pl.BlockSpec((1, w), lambda c: (0, 0))
    return pl.pallas_call(
        body, name="ssd_pre_fwd", grid=(nc,),
        in_specs=[row(3 * D_MODEL), pl.BlockSpec((CHUNK, 3 * D_MODEL), lambda c: (jnp.maximum(c - 1, 0), 0)),
                  row(LANES), pl.BlockSpec((kt, SSD_XBC), lambda c: (0, 0)), vec(SSD_XBC), vec(LANES), vec(LANES)],
        out_specs=[row(SSD_XBC), row(LANES), row(LANES)],
        out_shape=[jax.ShapeDtypeStruct((length, SSD_XBC), F32), jax.ShapeDtypeStruct((length, LANES), F32),
                   jax.ShapeDtypeStruct((length, LANES), F32)],
        compiler_params=_params(dimension_semantics=("arbitrary",)),
    )(p, p, dt_raw, conv_w, conv_b, dt_bias, a_log)


def ssd_pre_bwd(p, dt_raw, conv_w, conv_b, dt_bias, a_log, dact, ddt, da, dz):
    length = p.shape[0]
    nc = length // CHUNK
    kt = conv_w.shape[0]

    def body(cur_ref, prev_ref, raw_ref, w_ref, b_ref, bias_ref, alog_ref, dact_ref, ddt_ref, da_ref, dz_ref,
             dp_ref, draw_ref, dw_ref, db_ref, dbias_ref, dalog_ref, dpre_next):
        step = pl.program_id(0)
        c = nc - 1 - step
        rows = lax.broadcasted_iota(jnp.int32, (CHUNK, 1), 0)
        vm = c * CHUNK + rows >= PAD

        @pl.when(step == 0)
        def _():
            dpre_next[...] = jnp.zeros_like(dpre_next)
            dw_ref[...] = jnp.zeros_like(dw_ref)
            db_ref[...] = jnp.zeros_like(db_ref)
            dbias_ref[...] = jnp.zeros_like(dbias_ref)
            dalog_ref[...] = jnp.zeros_like(dalog_ref)

        u_cur = jnp.where(vm, cur_ref[:, SSD_INNER:], 0.0)
        u_prev = jnp.where(((c - 1) * CHUNK + rows >= PAD) & (c >= 1), prev_ref[:, SSD_INNER:], 0.0)
        shifted = [_shift_down(u_cur, u_prev, kt - 1 - i) for i in range(kt)]
        pre = jnp.zeros((CHUNK, SSD_XBC), F32) + b_ref[...]
        for i in range(kt):
            pre = pre + shifted[i] * w_ref[i:i + 1, :]
        sg = _sigmoid(pre)
        lanes = lax.broadcasted_iota(jnp.int32, (CHUNK, SSD_XBC), 1)
        dact_v = jnp.where(vm | (lanes >= SSD_INNER), dact_ref[...], 0.0)
        dpre = dact_v * (sg * (1.0 + pre * (1.0 - sg)))
        db_ref[...] += jnp.sum(dpre, axis=0, keepdims=True)
        nxt = dpre_next[...]
        du = jnp.zeros((CHUNK, SSD_XBC), F32)
        for i in range(kt):
            dw_ref[i:i + 1, :] += jnp.sum(dpre * shifted[i], axis=0, keepdims=True)
            du = du + _shift_up(dpre, nxt, kt - 1 - i) * w_ref[i:i + 1, :]
        dpre_next[...] = dpre
        dp_ref[:, :SSD_INNER] = dz_ref[...].astype(dp_ref.dtype)
        dp_ref[:, SSD_INNER:] = jnp.where(vm, du, 0.0).astype(dp_ref.dtype)
        x = raw_ref[...] + bias_ref[...]
        neg_exp = -jnp.exp(alog_ref[...])
        dav = da_ref[...]
        draw = (ddt_ref[...] + dav * neg_exp) * _sigmoid(x)
        draw_ref[...] = draw.astype(draw_ref.dtype)
        dbias_ref[...] += jnp.sum(draw, axis=0, keepdims=True)
        dalog_ref[...] += jnp.sum(dav * (neg_exp * _softplus(x)), axis=0, keepdims=True)

    rev = lambda c: (nc - 1 - c, 0)
    row = lambda w: pl.BlockSpec((CHUNK, w), rev)
    vec = lambda w: pl.BlockSpec((1, w), lambda c: (0, 0))
    taps = pl.BlockSpec((kt, SSD_XBC), lambda c: (0, 0))
    return pl.pallas_call(
        body, name="ssd_pre_bwd", grid=(nc,),
        in_specs=[row(3 * D_MODEL),
                  pl.BlockSpec((CHUNK, 3 * D_MODEL), lambda c: (jnp.maximum(nc - 2 - c, 0), 0)),
                  row(LANES), taps, vec(SSD_XBC), vec(LANES), vec(LANES),
                  row(SSD_XBC), row(LANES), row(LANES), row(SSD_INNER)],
        out_specs=[row(3 * D_MODEL), row(LANES), taps, vec(SSD_XBC), vec(LANES), vec(LANES)],
        out_shape=[jax.ShapeDtypeStruct((length, 3 * D_MODEL), BF16), jax.ShapeDtypeStruct((length, LANES), BF16),
                   jax.ShapeDtypeStruct((kt, SSD_XBC), F32), jax.ShapeDtypeStruct((1, SSD_XBC), F32),
                   jax.ShapeDtypeStruct((1, LANES), F32), jax.ShapeDtypeStruct((1, LANES), F32)],
        scratch_shapes=[pltpu.VMEM((CHUNK, SSD_XBC), F32)],
        compiler_params=_params(dimension_semantics=("arbitrary",)),
    )(p, p, dt_raw, conv_w, conv_b, dt_bias, a_log, dact, ddt, da, dz)


def _tri_apply_lhs_t(x, tri):
    out = None
    for part in _split3(x):
        t = lax.dot_general(part, tri, (((0,), (1,)), ((), ())), preferred_element_type=F32)
        out = t if out is None else out + t
    return out


def ssd_core_fwd(act, dt, a, d_skip):
    length = act.shape[0]
    nc = length // CHUNK

    def body(act_ref, dt_ref, a_ref, dskip_ref, y_ref, st_ref, h_scr):
        c = pl.program_id(0)

        @pl.when(c == 0)
        def _():
            h_scr[...] = jnp.zeros_like(h_scr)

        r = lax.broadcasted_iota(jnp.int32, (CHUNK, CHUNK), 0)
        s = lax.broadcasted_iota(jnp.int32, (CHUNK, CHUNK), 1)
        causal = r >= s
        incl = jnp.where(causal, 1.0, 0.0).astype(BF16)
        a_v = a_ref[...]
        acs = _tri_apply(incl, a_v, (((1,), (0,)), ((), ())))
        acs_t = _tri_apply_lhs_t(a_v, incl)
        dt_v = dt_ref[...]
        lanes = lax.broadcasted_iota(jnp.int32, (CHUNK, LANES), 1)
        low = lanes < SSD_HEAD_DIM
        for g in range(SSD_GROUPS):
            bg = act_ref[:, SSD_INNER + g * SSD_STATE:SSD_INNER + (g + 1) * SSD_STATE].astype(BF16)
            cg = act_ref[:, SSD_INNER + (SSD_GROUPS + g) * SSD_STATE:
                         SSD_INNER + (SSD_GROUPS + g + 1) * SSD_STATE].astype(BF16)
            cb = lax.dot_general(cg, bg, _NT, preferred_element_type=F32)
            for pair in (2 * g, 2 * g + 1):
                cols = slice(pair * LANES, (pair + 1) * LANES)
                xs = act_ref[:, cols]
                x = xs * _lane_pair(dt_v, pair)
                ydiag = jnp.zeros((CHUNK, LANES), F32)
                for k, keep in ((0, low), (1, ~low)):
                    h = 2 * pair + k
                    seg = jnp.where(causal, jnp.exp(acs[:, h:h + 1] - acs_t[h:h + 1, :]), 0.0)
                    ydiag = ydiag + jnp.dot((cb * seg).astype(BF16), jnp.where(keep, x, 0.0).astype(BF16),
                                            preferred_element_type=F32)
                acs_p = _lane_pair(acs, pair)
                last = acs_p[CHUNK - 1:CHUNK, :]
                xds = (x * jnp.exp(last - acs_p)).astype(BF16)
                hprev = h_scr[pair]
                st_ref[pair] = hprev
                yoff = lax.dot_general(cg, hprev.astype(BF16), _NT, preferred_element_type=F32) * jnp.exp(acs_p)
                prow = lax.broadcasted_iota(jnp.int32, (LANES, 1), 0)
                cd = jnp.where(prow < SSD_HEAD_DIM, jnp.exp(acs_t[2 * pair:2 * pair + 1, CHUNK - 1:CHUNK]),
                               jnp.exp(acs_t[2 * pair + 1:2 * pair + 2, CHUNK - 1:CHUNK]))
                h_scr[pair] = hprev * cd + lax.dot_general(xds, bg, _TN, preferred_element_type=F32)
                y_ref[:, cols] = ydiag + yoff + xs * dskip_ref[:, cols]

    row = lambda w: pl.BlockSpec((CHUNK, w), lambda c: (c, 0))
    return pl.pallas_call(
        body, name="ssd_core_fwd", grid=(nc,),
        in_specs=[row(SSD_XBC), row(LANES), row(LANES), pl.BlockSpec((1, SSD_INNER), lambda c: (0, 0))],
        out_specs=[row(SSD_INNER), pl.BlockSpec((None, SSD_PAIRS, LANES, SSD_STATE), lambda c: (c, 0, 0, 0))],
        out_shape=[jax.ShapeDtypeStruct((length, SSD_INNER), F32),
                   jax.ShapeDtypeStruct((nc, SSD_PAIRS, LANES, SSD_STATE), F32)],
        scratch_shapes=[pltpu.VMEM((SSD_PAIRS, LANES, SSD_STATE), F32)],
        compiler_params=_params(dimension_semantics=("arbitrary",)),
    )(act, dt, a, d_skip)


def ssd_core_bwd(act, dt, a, d_skip, states, dy):
    length = act.shape[0]
    nc = length // CHUNK

    def body(act_ref, dt_ref, a_ref, dskip_ref, st_ref, dy_ref, dact_ref, ddt_ref, da_ref, dds_ref, dh_scr):
        step = pl.program_id(0)

        @pl.when(step == 0)
        def _():
            dh_scr[...] = jnp.zeros_like(dh_scr)
            dds_ref[...] = jnp.zeros_like(dds_ref)

        r = lax.broadcasted_iota(jnp.int32, (CHUNK, CHUNK), 0)
        s = lax.broadcasted_iota(jnp.int32, (CHUNK, CHUNK), 1)
        causal = r >= s
        incl = jnp.where(causal, 1.0, 0.0).astype(BF16)
        a_v = a_ref[...]
        acs = _tri_apply(incl, a_v, (((1,), (0,)), ((), ())))
        acs_t = _tri_apply_lhs_t(a_v, incl)
        dt_v = dt_ref[...]
        lanes = lax.broadcasted_iota(jnp.int32, (CHUNK, LANES), 1)
        low = lanes < SSD_HEAD_DIM
        prow = lax.broadcasted_iota(jnp.int32, (LANES, 1), 0)
        is_last = lax.broadcasted_iota(jnp.int32, (CHUNK, 1), 0) == CHUNK - 1
        dacs_cols = [None] * SSD_HEADS
        dacs_rows = [None] * SSD_HEADS
        ddt_cols = [None] * SSD_HEADS
        for g in range(SSD_GROUPS):
            b_cols = slice(SSD_INNER + g * SSD_STATE, SSD_INNER + (g + 1) * SSD_STATE)
            c_cols = slice(SSD_INNER + (SSD_GROUPS + g) * SSD_STATE, SSD_INNER + (SSD_GROUPS + g + 1) * SSD_STATE)
            bg = act_ref[:, b_cols].astype(BF16)
            cg = act_ref[:, c_cols].astype(BF16)
            cb = lax.dot_general(cg, bg, _NT, preferred_element_type=F32)
            dcb = jnp.zeros((CHUNK, CHUNK), F32)
            dbg = jnp.zeros((CHUNK, SSD_STATE), F32)
            dcg = jnp.zeros((CHUNK, SSD_STATE), F32)
            for pair in (2 * g, 2 * g + 1):
                cols = slice(pair * LANES, (pair + 1) * LANES)
                xs = act_ref[:, cols]
                dtp = _lane_pair(dt_v, pair)
                x = xs * dtp
                xb = x.astype(BF16)
                dyv = dy_ref[:, cols]
                dyb = dyv.astype(BF16)
                dds_ref[:, cols] += jnp.sum(dyv * xs, axis=0, keepdims=True)
                acs_p = _lane_pair(acs, pair)
                last = acs_p[CHUNK - 1:CHUNK, :]
                ds = jnp.exp(last - acs_p)
                ea = jnp.exp(acs_p)
                hprev = st_ref[pair]
                hb = hprev.astype(BF16)
                dh = dh_scr[pair]
                dhb = dh.astype(BF16)
                dx = jnp.zeros((CHUNK, LANES), F32)
                for k, keep in ((0, low), (1, ~low)):
                    h = 2 * pair + k
                    seg = jnp.where(causal, jnp.exp(acs[:, h:h + 1] - acs_t[h:h + 1, :]), 0.0)
                    lmat = cb * seg
                    dl = lax.dot_general(jnp.where(keep, dyv, 0.0).astype(BF16), xb, _NT,
                                         preferred_element_type=F32)
                    dcb = dcb + dl * seg
                    t = dl * lmat
                    dacs_cols[h] = jnp.sum(t, axis=1, keepdims=True)
                    dacs_rows[h] = jnp.sum(t, axis=0, keepdims=True)
                    dx = dx + jnp.where(keep, lax.dot_general(lmat.astype(BF16), dyb, _TN,
                                                              preferred_element_type=F32), 0.0)
                yoff = lax.dot_general(cg, hb, _NT, preferred_element_type=F32) * ea
                dm = (dyv * ea).astype(BF16)
                dcg = dcg + jnp.dot(dm, hb, preferred_element_type=F32)
                dxds = lax.dot_general(bg, dhb, _NT, preferred_element_type=F32)
                xds = x * ds
                dbg = dbg + jnp.dot(xds.astype(BF16), dhb, preferred_element_type=F32)
                dx = dx + dxds * ds
                t_ds = dxds * xds
                e_a = jnp.exp(acs_t[2 * pair:2 * pair + 1, CHUNK - 1:CHUNK])
                e_b = jnp.exp(acs_t[2 * pair + 1:2 * pair + 2, CHUNK - 1:CHUNK])
                cd = jnp.where(prow < SSD_HEAD_DIM, e_a, e_b)
                hd = dh * hprev
                dcd_a = jnp.sum(jnp.where(prow < SSD_HEAD_DIM, hd, 0.0), keepdims=True)
                dcd_b = jnp.sum(hd, keepdims=True) - dcd_a
                dh_scr[pair] = dh * cd + lax.dot_general(dm, cg, _TN, preferred_element_type=F32)
                col_lo, col_hi = _half_sums(dyv * yoff - t_ds)
                tot_lo, tot_hi = _half_sums(jnp.sum(t_ds, axis=0, keepdims=True))
                dacs_cols[2 * pair] += col_lo + jnp.where(is_last, tot_lo + dcd_a.reshape(1, 1) * e_a, 0.0)
                dacs_cols[2 * pair + 1] += col_hi + jnp.where(is_last, tot_hi + dcd_b.reshape(1, 1) * e_b, 0.0)
                dact_ref[:, cols] = dyv * dskip_ref[:, cols] + dx * dtp
                ddt_cols[2 * pair], ddt_cols[2 * pair + 1] = _half_sums(dx * xs)
            dcbb = dcb.astype(BF16)
            dact_ref[:, b_cols] = dbg + lax.dot_general(dcbb, cg, _TN, preferred_element_type=F32)
            dact_ref[:, c_cols] = dcg + jnp.dot(dcbb, bg, preferred_element_type=F32)
        ddt_ref[...] = _put_cols(ddt_cols)
        sub = lax.broadcasted_iota(jnp.int32, (LANES, CHUNK), 0)
        rows_mat = jnp.zeros((LANES, CHUNK), F32)
        for h in range(SSD_HEADS):
            rows_mat = rows_mat + jnp.where(sub == h, dacs_rows[h], 0.0)
        dacs = _put_cols(dacs_cols) - rows_mat.T
        da_ref[...] = _tri_apply(incl, dacs, (((0,), (0,)), ((), ())))

    rev = lambda c: (nc - 1 - c, 0)
    row = lambda w: pl.BlockSpec((CHUNK, w), rev)
    lane_vec = pl.BlockSpec((1, SSD_INNER), lambda c: (0, 0))
    return pl.pallas_call(
        body, name="ssd_core_bwd", grid=(nc,),
        in_specs=[row(SSD_XBC), row(LANES), row(LANES), lane_vec,
                  pl.BlockSpec((None, SSD_PAIRS, LANES, SSD_STATE), lambda c: (nc - 1 - c, 0, 0, 0)),
                  row(SSD_INNER)],
        out_specs=[row(SSD_XBC), row(LANES), row(LANES), lane_vec],
        out_shape=[jax.ShapeDtypeStruct((length, SSD_XBC), F32), jax.ShapeDtypeStruct((length, LANES), F32),
                   jax.ShapeDtypeStruct((length, LANES), F32), jax.ShapeDtypeStruct((1, SSD_INNER), F32)],
        scratch_shapes=[pltpu.VMEM((SSD_PAIRS, LANES, SSD_STATE), F32)],
        compiler_params=_params(dimension_semantics=("arbitrary",)),
    )(act, dt, a, d_skip, states, dy)


def ssd_post_fwd(y, p, norm_w):
    length = y.shape[0]
    tm = _tile(length, (384, 256, 128))

    def body(y_ref, p_ref, w_ref, o_ref):
        for g in range(SSD_GROUPS):
            cols = slice(g * SSD_GW, (g + 1) * SSD_GW)
            z = p_ref[:, cols]
            v = y_ref[:, cols] * (z * _sigmoid(z))
            o_ref[:, cols] = (v * lax.rsqrt(jnp.mean(v * v, axis=-1, keepdims=True) + EPS)
                              * w_ref[:, cols]).astype(o_ref.dtype)

    return pl.pallas_call(
        body, name="ssd_post_fwd", grid=(length // tm,),
        in_specs=[pl.BlockSpec((tm, SSD_INNER), lambda i: (i, 0)), pl.BlockSpec((tm, SSD_INNER), lambda i: (i, 0)),
                  pl.BlockSpec((1, SSD_INNER), lambda i: (0, 0))],
        out_specs=pl.BlockSpec((tm, SSD_INNER), lambda i: (i, 0)),
        out_shape=jax.ShapeDtypeStruct((length, SSD_INNER), BF16),
        compiler_params=_params(dimension_semantics=("arbitrary",)),
    )(y, p, norm_w)


def ssd_post_bwd(y, p, norm_w, dout):
    length = y.shape[0]
    tm = _tile(length, (384, 256, 128))

    def body(y_ref, p_ref, w_ref, do_ref, dy_ref, dz_ref, dw_ref):
        @pl.when(pl.program_id(0) == 0)
        def _():
            dw_ref[...] = jnp.zeros_like(dw_ref)

        for g in range(SSD_GROUPS):
            cols = slice(g * SSD_GW, (g + 1) * SSD_GW)
            z = p_ref[:, cols]
            yv = y_ref[:, cols]
            sg = _sigmoid(z)
            v = yv * (z * sg)
            rs = lax.rsqrt(jnp.mean(v * v, axis=-1, keepdims=True) + EPS)
            vh = v * rs
            do = do_ref[:, cols]
            dw_ref[:, cols] += jnp.sum(do * vh, axis=0, keepdims=True)
            dvh = do * w_ref[:, cols]
            dv = rs * (dvh - vh * jnp.mean(dvh * vh, axis=-1, keepdims=True))
            dy_ref[:, cols] = dv * (z * sg)
            dz_ref[:, cols] = dv * yv * (sg * (1.0 + z * (1.0 - sg)))

    blk = pl.BlockSpec((tm, SSD_INNER), lambda i: (i, 0))
    vec = pl.BlockSpec((1, SSD_INNER), lambda i: (0, 0))
    return pl.pallas_call(
        body, name="ssd_post_bwd", grid=(length // tm,),
        in_specs=[blk, blk, vec, blk], out_specs=[blk, blk, vec],
        out_shape=[jax.ShapeDtypeStruct((length, SSD_INNER), F32), jax.ShapeDtypeStruct((length, SSD_INNER), F32),
                   jax.ShapeDtypeStruct((1, SSD_INNER), F32)],
        compiler_params=_params(dimension_semantics=("arbitrary",)),
    )(y, p, norm_w, dout)


def _ssd_rows(lw):
    pad = lambda v: jnp.pad(v, (0, LANES - SSD_HEADS))[None]
    return dict(conv_w=lw['ssd_conv_w'], conv_b=lw['ssd_conv_b'][None], dt_bias=pad(lw['ssd_dt_bias']),
                a_log=pad(lw['ssd_a_log']), d_skip=jnp.repeat(lw['ssd_d'], SSD_HEAD_DIM)[None],
                norm_w=lw['ssd_norm'][None])


def ssd_fwd(p, dt_raw, rows):
    act, dt, a = ssd_pre_fwd(p, dt_raw, rows['conv_w'], rows['conv_b'], rows['dt_bias'], rows['a_log'])
    y, states = ssd_core_fwd(act, dt, a, rows['d_skip'])
    return ssd_post_fwd(y, p, rows['norm_w']), (act, dt, a, y, states)


def ssd_bwd(p, dt_raw, rows, saved, dout):
    act, dt, a, y, states = saved
    dy, dz, dnorm = ssd_post_bwd(y, p, rows['norm_w'], dout)
    dact, ddt, da, dskip_lanes = ssd_core_bwd(act, dt, a, rows['d_skip'], states, dy)
    dp, draw, dconv_w, dconv_b, dbias, dalog = ssd_pre_bwd(
        p, dt_raw, rows['conv_w'], rows['conv_b'], rows['dt_bias'], rows['a_log'], dact, ddt, da, dz)
    grads = dict(ssd_conv_w=dconv_w, ssd_conv_b=dconv_b[0], ssd_dt_bias=dbias[0, :SSD_HEADS],
                 ssd_a_log=dalog[0, :SSD_HEADS], ssd_norm=dnorm[0],
                 ssd_d=jnp.sum(dskip_lanes.reshape(SSD_HEADS, SSD_HEAD_DIM), axis=1))
    return dp, draw, grads


IN_A = (0, 3 * D_MODEL)
IN_S = (IN_A[1], IN_A[1] + SSD_INNER + SSD_CONV_DIM)
IN_DT = (IN_S[1], IN_S[1] + SSD_HEADS)
IN_R = (IN_DT[1], IN_DT[1] + 4 * D_MODEL)
IN_SB = (IN_R[1], IN_R[1] + 3 * D_MODEL)
IN_G = (IN_SB[1], IN_SB[1] + N_BRANCH * D_MODEL)
IN_WIDTH = IN_G[1]


def _layer_weights(full, small, l):
    w_in = full['w_in'][l]
    cut = lambda r: w_in[:, r[0]:r[1]]
    w_dt = jnp.pad(cut(IN_DT), ((0, 0), (0, DT_PAD - SSD_HEADS)))
    return dict(
        w_a=cut(IN_A), w_s=cut(IN_S), w_dt=w_dt, w_r=cut(IN_R), w_sb=cut(IN_SB), w_g=cut(IN_G),
        w_branch=[full['w_branch'][l, n] for n in range(N_BRANCH)],
        w_out=full['w_out'][l], w_ffn_in=full['w_ffn_in'][l], w_ffn_out=full['w_ffn_out'][l],
        conv_a=full['conv_a'][l], ssd_conv_w=full['ssd_conv_w'][l],
        ssd_conv_b=small['ssd_conv_b'][l], ssd_dt_bias=small['ssd_dt_bias'][l], ssd_a_log=small['ssd_a_log'][l],
        ssd_d=small['ssd_d'][l], ssd_norm=small['ssd_norm'][l],
        n_mix_pre=small['norm_mix_pre'][l][None], n_mix_post=small['norm_mix_post'][l][None],
        n_ffn_pre=small['norm_ffn_pre'][l][None], n_ffn_post=small['norm_ffn_post'][l][None],
    )


def _layer_fwd(h_res, lw, ret_tables):
    s = {'h_res': h_res, 'ret_tables': ret_tables}
    hn = rms_fwd(h_res, lw['n_mix_pre'], out_dtype=BF16, name="rms_mix_pre")
    s['hn'] = hn
    p_a = mm_nn(hn, lw['w_a'], name="proj_conv")
    p_s = mm_nn(hn, lw['w_s'], name="proj_ssd")
    p_dt = mm_nn(hn, lw['w_dt'], name="proj_dt")
    p_r = mm_nn(hn, lw['w_r'], name="proj_ret")
    p_sb = mm_nn(hn, lw['w_sb'], out_dtype=BF16, name="proj_sb")
    p_g = mm_nn(hn, lw['w_g'], name="proj_gate")
    y_a = conv_mixer_fwd(p_a, lw['conv_a'])
    s['p_a'] = p_a
    s['ssd_rows'] = _ssd_rows(lw)
    y_b, s['ssd_saved'] = ssd_fwd(p_s, p_dt, s['ssd_rows'])
    s['p_s'], s['p_dt'] = p_s, p_dt
    y_c, s['ret_ypre'], s['ret_states'] = ret_fwd(p_r, ret_tables)
    s['p_r'] = p_r
    y_d, s['sb_total'] = sb_fwd(p_sb)
    s['p_sb'] = p_sb
    ys = [y_a, y_b, y_c, y_d]
    s['ys'] = ys
    ups = [mm_nn(ys[n], lw['w_branch'][n], name="branch_up") for n in range(N_BRANCH)]
    merged = merge_fwd(p_g, ups)
    s['p_g'], s['ups'] = p_g, ups
    s['merged'] = merged
    mix = mm_nn(merged, lw['w_out'], name="mix_out")
    s['mix'] = mix
    h2 = rms_fwd(mix, lw['n_mix_post'], res=h_res, name="rms_mix_post")
    s['h2'] = h2
    hf = rms_fwd(h2, lw['n_ffn_pre'], out_dtype=BF16, name="rms_ffn_pre")
    s['hf'] = hf
    f = mm_nn(hf, lw['w_ffn_in'], name="ffn_in")
    act = swiglu_fwd(f)
    s['f'], s['act'] = f, act
    fo = mm_nn(act, lw['w_ffn_out'], name="ffn_out")
    s['fo'] = fo
    return rms_fwd(fo, lw['n_ffn_post'], res=h2, name="rms_ffn_post"), s


def _layer_bwd(dh3, lw, s):
    g = {}
    d_fo, g['norm_ffn_post'] = rms_bwd(s['fo'], lw['n_ffn_post'], dh3, dx_dtype=BF16, name="rms_ffn_post_bwd")
    d_act = mm_nt(d_fo, lw['w_ffn_out'], name="ffn_out_dx")
    g['w_ffn_out'] = mm_tn(s['act'], d_fo, name="ffn_out_dw")
    df = swiglu_bwd(s['f'], d_act)
    d_hf = mm_nt(df, lw['w_ffn_in'], name="ffn_in_dx")
    g['w_ffn_in'] = mm_tn(s['hf'], df, name="ffn_in_dw")
    dh2, g['norm_ffn_pre'] = rms_bwd(s['h2'], lw['n_ffn_pre'], d_hf, add=dh3, name="rms_ffn_pre_bwd")
    d_mix, g['norm_mix_post'] = rms_bwd(s['mix'], lw['n_mix_post'], dh2, dx_dtype=BF16, name="rms_mix_post_bwd")
    d_merged = mm_nt(d_mix, lw['w_out'], name="mix_out_dx")
    g['w_out'] = mm_tn(s['merged'], d_mix, name="mix_out_dw")
    dp_g, dups = merge_bwd(s['p_g'], s['ups'], d_merged)
    dys = [mm_nt(dups[n], lw['w_branch'][n], name="branch_dx") for n in range(N_BRANCH)]
    g['w_branch'] = jnp.stack([mm_tn(s['ys'][n], dups[n], name="branch_dw") for n in range(N_BRANCH)])
    dp_a, g['conv_a'] = conv_mixer_bwd(s['p_a'], lw['conv_a'], dys[0])
    dp_s, dp_dt, ssd_grads = ssd_bwd(s['p_s'], s['p_dt'], s['ssd_rows'], s['ssd_saved'], dys[1])
    g.update(ssd_grads)
    dp_r = ret_bwd(s['p_r'], s['ret_tables'], s['ret_ypre'], s['ret_states'], dys[2])
    dq, dk, dv = sb_bwd(s['p_sb'], s['sb_total'], dys[3])
    dp_sb = jnp.concatenate([dq, dk.astype(BF16), dv.astype(BF16)], axis=1)
    hn = s['hn']
    d_hn = None
    dws = []
    for dp, w, nm in ((dp_a, lw['w_a'], "conv"), (dp_s, lw['w_s'], "ssd"), (dp_dt, lw['w_dt'], "dt"),
                      (dp_r, lw['w_r'], "ret"), (dp_sb, lw['w_sb'], "sb"), (dp_g, lw['w_g'], "gate")):
        d_hn = mm_nt(dp, w, acc=d_hn, name="proj_dx")
        dws.append(mm_tn(hn, dp, name="proj_dw"))
    dws[2] = dws[2][:, :SSD_HEADS]
    g['w_in'] = jnp.concatenate(dws, axis=1)
    dh_res, g['norm_mix_pre'] = rms_bwd(s['h_res'], lw['n_mix_pre'], d_hn, add=dh2, name="rms_mix_pre_bwd")
    for k in ('norm_ffn_post', 'norm_ffn_pre', 'norm_mix_post', 'norm_mix_pre'):
        g[k] = g[k][0]
    return dh_res, g


def _quarter(a, axis, j):
    n = a.shape[axis] // N_CHIPS
    return lax.slice_in_dim(a, j * n, (j + 1) * n, axis=axis)


def kernel(x, meta, w_in, conv_a, ssd_conv_w, ssd_conv_b, ssd_dt_bias, ssd_a_log, ssd_d, ssd_norm, w_branch, w_out, w_ffn_in, w_ffn_out, norm_mix_pre, norm_mix_post, norm_ffn_pre, norm_ffn_post, loss_target, m_meta, m_w_in, m_conv_a, m_ssd_conv_w, m_ssd_conv_b, m_ssd_dt_bias, m_ssd_a_log, m_ssd_d, m_ssd_norm, m_w_branch, m_w_out, m_w_ffn_in, m_w_ffn_out, m_norm_mix_pre, m_norm_mix_post, m_norm_ffn_pre, m_norm_ffn_post, v_meta, v_w_in, v_conv_a, v_ssd_conv_w, v_ssd_conv_b, v_ssd_dt_bias, v_ssd_a_log, v_ssd_d, v_ssd_norm, v_w_branch, v_w_out, v_w_ffn_in, v_w_ffn_out, v_norm_mix_pre, v_norm_mix_post, v_norm_ffn_pre, v_norm_ffn_post):
    w_loc = dict(meta=meta, w_in=w_in, conv_a=conv_a, ssd_conv_w=ssd_conv_w, ssd_conv_b=ssd_conv_b,
                 ssd_dt_bias=ssd_dt_bias, ssd_a_log=ssd_a_log, ssd_d=ssd_d, ssd_norm=ssd_norm, w_branch=w_branch,
                 w_out=w_out, w_ffn_in=w_ffn_in, w_ffn_out=w_ffn_out, norm_mix_pre=norm_mix_pre,
                 norm_mix_post=norm_mix_post, norm_ffn_pre=norm_ffn_pre, norm_ffn_post=norm_ffn_post)
    m_loc = dict(meta=m_meta, w_in=m_w_in, conv_a=m_conv_a, ssd_conv_w=m_ssd_conv_w, ssd_conv_b=m_ssd_conv_b,
                 ssd_dt_bias=m_ssd_dt_bias, ssd_a_log=m_ssd_a_log, ssd_d=m_ssd_d, ssd_norm=m_ssd_norm,
                 w_branch=m_w_branch, w_out=m_w_out, w_ffn_in=m_w_ffn_in, w_ffn_out=m_w_ffn_out,
                 norm_mix_pre=m_norm_mix_pre, norm_mix_post=m_norm_mix_post, norm_ffn_pre=m_norm_ffn_pre,
                 norm_ffn_post=m_norm_ffn_post)
    v_loc = dict(meta=v_meta, w_in=v_w_in, conv_a=v_conv_a, ssd_conv_w=v_ssd_conv_w, ssd_conv_b=v_ssd_conv_b,
                 ssd_dt_bias=v_ssd_dt_bias, ssd_a_log=v_ssd_a_log, ssd_d=v_ssd_d, ssd_norm=v_ssd_norm,
                 w_branch=v_w_branch, w_out=v_w_out, w_ffn_in=v_w_ffn_in, w_ffn_out=v_w_ffn_out,
                 norm_mix_pre=v_norm_mix_pre, norm_mix_post=v_norm_mix_post, norm_ffn_pre=v_norm_ffn_pre,
                 norm_ffn_post=v_norm_ffn_post)

    gathered = gather_shards([w_loc[n].astype(BF16) for n in MATMUL_WEIGHTS]
                             + [_pack([w_loc[n] for n in SMALL_SHARDED], F32, 8)])
    full = {}
    for t, n in enumerate(MATMUL_WEIGHTS):
        full[n] = jnp.concatenate([gathered[t][j] for j in range(N_CHIPS)], axis=SHARD_AXIS[n])
    parts_f = [_unpack(gathered[-1][j], [w_loc[n].shape for n in SMALL_SHARDED]) for j in range(N_CHIPS)]
    for t, n in enumerate(SMALL_SHARDED):
        full[n] = jnp.concatenate([parts_f[j][t] for j in range(N_CHIPS)], axis=SHARD_AXIS[n])

    xs = x[0]
    seq = xs.shape[0]
    length = CHUNK + seq
    h = jnp.concatenate([jnp.zeros((PAD, D_MODEL), F32), full['meta'], xs], axis=0)
    lws, saved = [], []
    ret_tables = _ret_tables(length)
    for l in range(DEPTH):
        lw = _layer_weights(full, w_loc, l)
        h, s = _layer_fwd(h, lw, ret_tables)
        lws.append(lw)
        saved.append(s)

    loss_row, dh = loss_head(h, loss_target[0])
    loss = lax.psum(loss_row[0, 0], ("x", "y", "c"))

    layer_grads = [None] * DEPTH
    for l in reversed(range(DEPTH)):
        dh, layer_grads[l] = _layer_bwd(dh, lws[l], saved[l])
    grad_x = dh[CHUNK:][None]
    grads = {n: jnp.stack([layer_grads[l][n] for l in range(DEPTH)]) for n in WEIGHTS if n != 'meta'}
    grads['meta'] = dh[PAD:CHUNK]

    def rows2d(a):
        return a.reshape(-1, a.shape[-1])

    def small_pieces(j):
        return [_quarter(grads[n], SHARD_AXIS[n], j) if n in SHARD_AXIS else grads[n] for n in SMALL_ORDER]

    quarters = [jnp.stack([rows2d(_quarter(grads[n], SHARD_AXIS[n], j)) for j in range(N_CHIPS)])
                for n in MATMUL_WEIGHTS]
    quarters.append(jnp.stack([_pack(small_pieces(j), F32, 16) for j in range(N_CHIPS)]))
    reduced = reduce_gradients(quarters)
    results = {}
    for t, n in enumerate(MATMUL_WEIGHTS):
        shape = w_loc[n].shape
        new = adamw(reduced[t], rows2d(w_loc[n]), rows2d(m_loc[n]), rows2d(v_loc[n]))
        results[n] = [a.reshape(shape) for a in (reduced[t], *new)]
    small_new = adamw(reduced[-1], *[_pack([d[n] for n in SMALL_ORDER], F32, 16) for d in (w_loc, m_loc, v_loc)])
    small_shapes = [w_loc[n].shape for n in SMALL_ORDER]
    for kind, buf in enumerate((reduced[-1], *small_new)):
        for n, piece in zip(SMALL_ORDER, _unpack(buf, small_shapes)):
            results.setdefault(n, [None] * 4)[kind] = piece
    outs = [results[n][kind] for kind in range(4) for n in WEIGHTS]
    return (loss, grad_x, *outs)
```

```python
import functools
import math

import numpy as np
import jax
import jax.numpy as jnp
from jax import lax
from jax.experimental import pallas as pl
from jax.experimental.pallas import tpu as pltpu

F32 = jnp.float32
BF16 = jnp.bfloat16

D_MODEL = 1024
DEPTH = 2
N_META = 16
CHUNK = 128
PAD = CHUNK - N_META
EPS = 1e-6

CONV_A_K = 3
SSD_HEAD_DIM = 64
SSD_HEADS = 16
SSD_INNER = 1024
SSD_GROUPS = 4
SSD_STATE = 128
SSD_CONV_K = 4
SSD_CONV_DIM = SSD_INNER + 2 * SSD_GROUPS * SSD_STATE
RET_HEADS = 4
RET_QK_DIM = 256
RET_V_DIM = 256
RET_WIDTH = 1024
ROPE_BASE = 10000.0
SB_HEADS = 8
SB_HEAD_DIM = 128
N_BRANCH = 4
D_FF = 2816
DT_PAD = 128

ADAM_LR = 0.001
ADAM_B1 = 0.9
ADAM_B2 = 0.999
ADAM_EPS = 1e-08
ADAM_WD = 0.01
ADAM_STEP = 10

N_CHIPS = 4
N_DEV = 8
LANES = 128
VMEM_LIMIT = 56 * 1024 * 1024
MESH = pl.DeviceIdType.MESH

WEIGHTS = ['meta', 'w_in', 'conv_a', 'ssd_conv_w', 'ssd_conv_b', 'ssd_dt_bias', 'ssd_a_log', 'ssd_d',
           'ssd_norm', 'w_branch', 'w_out', 'w_ffn_in', 'w_ffn_out', 'norm_mix_pre', 'norm_mix_post',
           'norm_ffn_pre', 'norm_ffn_post']
SHARD_AXIS = {'meta': 1, 'w_in': 2, 'conv_a': 2, 'ssd_conv_w': 2, 'w_branch': 2, 'w_out': 1,
              'w_ffn_in': 2, 'w_ffn_out': 1}
MATMUL_WEIGHTS = ['w_in', 'w_branch', 'w_out', 'w_ffn_in', 'w_ffn_out']
SMALL_SHARDED = ['meta', 'conv_a', 'ssd_conv_w']
SMALL_ORDER = SMALL_SHARDED + [n for n in WEIGHTS if n not in SHARD_AXIS]


def _params(**kw):
    return pltpu.CompilerParams(vmem_limit_bytes=VMEM_LIMIT, **kw)


def _tile(n, prefs):
    for p in prefs:
        if n % p == 0:
            return p
    return n


MM_VMEM_BUDGET = 40 * 1024 * 1024
MM_ROW_TILES = (2752, 1376, 688, 384, 256, 128)
MM_COL_TILES = (1024, 512, 256, 128)


def _mm_tiles(m, n, cost):
    for tm in MM_ROW_TILES:
        if m % tm:
            continue
        for tn in MM_COL_TILES:
            if n % tn == 0 and cost(tm, tn) <= MM_VMEM_BUDGET:
                return tm, tn
    return _tile(m, (128, 8)), _tile(n, (128,))


def _size(x):
    return jnp.dtype(x.dtype).itemsize


def mm_nn(a, b, out_dtype=F32, name="mm_nn"):
    m, k = a.shape
    n = b.shape[1]
    ob = jnp.dtype(out_dtype).itemsize
    tm, tn = _mm_tiles(m, n, lambda tm, tn: (2 * tm * k * _size(a) + tm * k * 2 + 2 * k * tn * _size(b)
                                              + 2 * tm * tn * ob + tm * tn * 4))

    def body(a_ref, b_ref, o_ref):
        o_ref[...] = jnp.dot(a_ref[...].astype(BF16), b_ref[...].astype(BF16),
                             preferred_element_type=F32).astype(o_ref.dtype)

    return pl.pallas_call(
        body, name=name, grid=(m // tm, n // tn),
        in_specs=[pl.BlockSpec((tm, k), lambda i, j: (i, 0)), pl.BlockSpec((k, tn), lambda i, j: (0, j))],
        out_specs=pl.BlockSpec((tm, tn), lambda i, j: (i, j)),
        out_shape=jax.ShapeDtypeStruct((m, n), out_dtype),
        compiler_params=_params(dimension_semantics=("arbitrary", "arbitrary")),
    )(a, b)


def mm_nt(g, w, acc=None, name="mm_nt"):
    m, n = g.shape
    k = w.shape[0]
    has_acc = acc is not None
    tm, tn = _mm_tiles(m, n, lambda tm, tn: ((2 + 2 * has_acc) * tm * k * 4 + tm * k * 4 + 2 * tm * tn * _size(g)
                                              + tm * tn * 2 + 2 * k * tn * _size(w)))

    def body(*refs):
        if has_acc:
            g_ref, w_ref, acc_ref, o_ref = refs
        else:
            g_ref, w_ref, o_ref = refs
        j = pl.program_id(1)

        @pl.when(j == 0)
        def _():
            o_ref[...] = acc_ref[...] if has_acc else jnp.zeros_like(o_ref)

        o_ref[...] += lax.dot_general(g_ref[...].astype(BF16), w_ref[...].astype(BF16),
                                      (((1,), (1,)), ((), ())), preferred_element_type=F32)

    in_specs = [pl.BlockSpec((tm, tn), lambda i, j: (i, j)), pl.BlockSpec((k, tn), lambda i, j: (0, j))]
    args = [g, w]
    if has_acc:
        in_specs.append(pl.BlockSpec((tm, k), lambda i, j: (i, 0)))
        args.append(acc)
    return pl.pallas_call(
        body, name=name, grid=(m // tm, n // tn),
        in_specs=in_specs,
        out_specs=pl.BlockSpec((tm, k), lambda i, j: (i, 0)),
        out_shape=jax.ShapeDtypeStruct((m, k), F32),
        compiler_params=_params(dimension_semantics=("arbitrary", "arbitrary")),
    )(*args)


def mm_tn(x, g, name="mm_tn"):
    m, k = x.shape
    n = g.shape[1]
    tk = _tile(k, (1024, 1408, 512, 256, 128))
    tm, tn = _mm_tiles(m, n, lambda tm, tn: (3 * tk * tn * 4 + 2 * tm * tk * _size(x) + tm * tk * 2
                                              + 2 * tm * tn * _size(g) + tm * tn * 2))

    def body(x_ref, g_ref, o_ref):
        s = pl.program_id(2)

        @pl.when(s == 0)
        def _():
            o_ref[...] = jnp.zeros_like(o_ref)

        o_ref[...] += lax.dot_general(x_ref[...].astype(BF16), g_ref[...].astype(BF16),
                                      (((0,), (0,)), ((), ())), preferred_element_type=F32)

    return pl.pallas_call(
        body, name=name, grid=(k // tk, n // tn, m // tm),
        in_specs=[pl.BlockSpec((tm, tk), lambda a, b, s: (s, a)), pl.BlockSpec((tm, tn), lambda a, b, s: (s, b))],
        out_specs=pl.BlockSpec((tk, tn), lambda a, b, s: (a, b)),
        out_shape=jax.ShapeDtypeStruct((k, n), F32),
        compiler_params=_params(dimension_semantics=("arbitrary", "arbitrary", "arbitrary")),
    )(x, g)


def rms_fwd(x, w, res=None, out_dtype=F32, name="rms_fwd"):
    m, d = x.shape
    tm = _tile(m, (384, 256, 128))
    has_res = res is not None

    def body(*refs):
        if has_res:
            x_ref, w_ref, r_ref, o_ref = refs
        else:
            x_ref, w_ref, o_ref = refs
        xv = x_ref[...]
        y = xv * lax.rsqrt(jnp.mean(xv * xv, axis=-1, keepdims=True) + EPS) * w_ref[...]
        o_ref[...] = (y + r_ref[...] if has_res else y).astype(o_ref.dtype)

    row = pl.BlockSpec((tm, d), lambda i: (i, 0))
    in_specs = [row, pl.BlockSpec((1, d), lambda i: (0, 0))]
    args = [x, w]
    if has_res:
        in_specs.append(row)
        args.append(res)
    return pl.pallas_call(
        body, name=name, grid=(m // tm,), in_specs=in_specs, out_specs=row,
        out_shape=jax.ShapeDtypeStruct((m, d), out_dtype),
        compiler_params=_params(dimension_semantics=("arbitrary",)),
    )(*args)


def rms_bwd(x, w, dy, add=None, dx_dtype=F32, name="rms_bwd"):
    m, d = x.shape
    tm = _tile(m, (384, 256, 128))
    has_add = add is not None

    def body(*refs):
        if has_add:
            x_ref, w_ref, dy_ref, add_ref, dx_ref, dw_ref = refs
        else:
            x_ref, w_ref, dy_ref, dx_ref, dw_ref = refs
        i = pl.program_id(0)
        xv = x_ref[...]
        dyv = dy_ref[...]
        r = lax.rsqrt(jnp.mean(xv * xv, axis=-1, keepdims=True) + EPS)
        xh = xv * r
        dxh = dyv * w_ref[...]
        dx = r * (dxh - xh * jnp.mean(dxh * xh, axis=-1, keepdims=True))
        dx_ref[...] = (dx + add_ref[...] if has_add else dx).astype(dx_ref.dtype)

        @pl.when(i == 0)
        def _():
            dw_ref[...] = jnp.zeros_like(dw_ref)

        dw_ref[...] += jnp.sum(dyv * xh, axis=0, keepdims=True)

    row = pl.BlockSpec((tm, d), lambda i: (i, 0))
    vec = pl.BlockSpec((1, d), lambda i: (0, 0))
    in_specs = [row, vec, row]
    args = [x, w, dy]
    if has_add:
        in_specs.append(row)
        args.append(add)
    return pl.pallas_call(
        body, name=name, grid=(m // tm,), in_specs=in_specs, out_specs=[row, vec],
        out_shape=[jax.ShapeDtypeStruct((m, d), dx_dtype), jax.ShapeDtypeStruct((1, d), F32)],
        compiler_params=_params(dimension_semantics=("arbitrary",)),
    )(*args)


def loss_head(h, target):
    l, d = h.shape
    nblk = l // CHUNK

    def body(h_ref, t_ref, loss_ref, dh_ref, acc_ref):
        i = pl.program_id(0)

        @pl.when(i == 0)
        def _():
            acc_ref[...] = jnp.zeros_like(acc_ref)
            dh_ref[...] = jnp.zeros_like(dh_ref)

        @pl.when(i > 0)
        def _():
            e = h_ref[...] - t_ref[...]
            dh_ref[...] = e / d
            acc_ref[...] += jnp.sum(e * e, axis=0, keepdims=True)

        @pl.when(i == nblk - 1)
        def _():
            loss_ref[...] = jnp.zeros_like(loss_ref) + 0.5 * jnp.sum(acc_ref[...]) / d

    return pl.pallas_call(
        body, name="loss_head", grid=(nblk,),
        in_specs=[pl.BlockSpec((CHUNK, d), lambda i: (i, 0)),
                  pl.BlockSpec((CHUNK, d), lambda i: (jnp.maximum(i - 1, 0), 0))],
        out_specs=[pl.BlockSpec((1, LANES), lambda i: (0, 0)), pl.BlockSpec((CHUNK, d), lambda i: (i, 0))],
        out_shape=[jax.ShapeDtypeStruct((1, LANES), F32), jax.ShapeDtypeStruct((l, d), F32)],
        scratch_shapes=[pltpu.VMEM((1, d), F32)],
        compiler_params=_params(dimension_semantics=("arbitrary",)),
    )(h, target)


SB_BLK = 128


def _sb_tile(l):
    return _tile(l, (384, 256, 128))


def _sb_tri(strict_later):
    r = lax.broadcasted_iota(jnp.int32, (SB_BLK, 2 * SB_BLK), 0)
    c = lax.broadcasted_iota(jnp.int32, (SB_BLK, 2 * SB_BLK), 1)
    keep = (r > c) if strict_later else (r < c)
    return jnp.where(keep | (c >= SB_BLK), 1.0, 0.0).astype(BF16)


def _sb_mask(i, j, t):
    qpos = i * t + lax.broadcasted_iota(jnp.int32, (t, t), 0)
    kpos = j * t + lax.broadcasted_iota(jnp.int32, (t, t), 1)
    return (kpos < qpos) & (kpos >= PAD)


def _sb_scores(q, k, scale, mask):
    z = lax.dot_general(q, k, (((1,), (1,)), ((), ())), preferred_element_type=F32) * scale
    sp = jnp.maximum(z, 0.0) + jnp.log(1.0 + jnp.exp(-jnp.abs(z)))
    lneg = -sp if mask is None else jnp.where(mask, -sp, 0.0)
    return z - sp, lneg


def _sb_block_sums(x, tri):
    s = jnp.dot(x.astype(BF16), tri, preferred_element_type=F32)
    return s[:, :SB_BLK], s[:, SB_BLK:]


def _sb_walk_down(i, step, carry):
    carry = step(i, carry, True)
    n = jnp.maximum(i - 1, 0)
    carry = lax.fori_loop(0, n // 2, lambda t, c: step(i - 2 - 2 * t, step(i - 1 - 2 * t, c, False), False), carry)
    carry = lax.fori_loop(0, n % 2, lambda t, c: step(1, c, False), carry)
    return lax.fori_loop(0, jnp.minimum(i, 1), lambda t, c: step(0, c, True), carry)


def _sb_walk_up(i, step, carry):
    carry = lax.fori_loop(0, jnp.minimum(i, 1), lambda t, c: step(0, c, True), carry)
    n = jnp.maximum(i - 1, 0)
    carry = lax.fori_loop(0, n % 2, lambda t, c: step(1, c, False), carry)
    first = 1 + n % 2
    carry = lax.fori_loop(0, n // 2, lambda t, c: step(first + 2 * t + 1, step(first + 2 * t, c, False), False),
                          carry)
    return step(i, carry, True)


def sb_fwd(qkv):
    l = qkv.shape[0]
    t = _sb_tile(l)
    nb = t // SB_BLK
    scale = SB_HEAD_DIM ** -0.5

    def body(q_ref, k_ref, v_ref, o_ref, tot_ref):
        i = pl.program_id(1)
        q = q_ref[...]
        tri = _sb_tri(True)

        def step(j, carry, masked):
            later, acc = carry
            rows = pl.ds(pl.multiple_of(j * t, t), t)
            mask = _sb_mask(i, j, t) if masked else None
            lpos, lneg = _sb_scores(q, k_ref[rows, :], scale, mask)
            ws = [None] * nb
            for b in reversed(range(nb)):
                cols = slice(b * SB_BLK, (b + 1) * SB_BLK)
                within, total = _sb_block_sums(lneg[:, cols], tri)
                ws[b] = jnp.exp(lpos[:, cols] + within + later)
                later = later + total
            w = jnp.concatenate(ws, axis=1)
            if masked:
                w = jnp.where(mask, w, 0.0)
            acc = acc + jnp.dot(w.astype(BF16), v_ref[rows, :], preferred_element_type=F32)
            return later, acc

        carry = (jnp.zeros((t, SB_BLK), F32), jnp.zeros((t, SB_HEAD_DIM), F32))
        later, acc = _sb_walk_down(i, step, carry)
        o_ref[...] = acc.astype(o_ref.dtype)
        tot_ref[...] = later[:, :1]

    return pl.pallas_call(
        body, name="sb_fwd", grid=(SB_HEADS, l // t),
        in_specs=[pl.BlockSpec((t, SB_HEAD_DIM), lambda h, i: (i, h)),
                  pl.BlockSpec((l, SB_HEAD_DIM), lambda h, i: (0, SB_HEADS + h)),
                  pl.BlockSpec((l, SB_HEAD_DIM), lambda h, i: (0, 2 * SB_HEADS + h))],
        out_specs=[pl.BlockSpec((t, SB_HEAD_DIM), lambda h, i: (i, h)),
                   pl.BlockSpec((None, t, 1), lambda h, i: (h, i, 0))],
        out_shape=[jax.ShapeDtypeStruct((l, D_MODEL), BF16), jax.ShapeDtypeStruct((SB_HEADS, l, 1), F32)],
        compiler_params=_params(dimension_semantics=("arbitrary", "arbitrary")),
    )(qkv, qkv, qkv)


def sb_bwd(qkv, row_total, dout):
    l = qkv.shape[0]
    t = _sb_tile(l)
    nb = t // SB_BLK
    nq = l // t
    scale = SB_HEAD_DIM ** -0.5

    def body(q_ref, k_ref, v_ref, tot_ref, do_ref, dq_ref, dk_hbm, dv_hbm, dk_acc, dv_acc):
        h = pl.program_id(0)
        i = pl.program_id(1)

        @pl.when(i == 0)
        def _():
            dk_acc[...] = jnp.zeros_like(dk_acc)
            dv_acc[...] = jnp.zeros_like(dv_acc)

        q = q_ref[...]
        dob = do_ref[...].astype(BF16)
        tri_later = _sb_tri(True)
        tri_before = _sb_tri(False)

        def step(j, carry, masked):
            later, g_before, dq = carry
            rows = pl.ds(pl.multiple_of(j * t, t), t)
            k = k_ref[rows, :]
            v = v_ref[rows, :]
            mask = _sb_mask(i, j, t) if masked else None
            lpos, lneg = _sb_scores(q, k, scale, mask)
            dw = lax.dot_general(dob, v, (((1,), (1,)), ((), ())), preferred_element_type=F32)
            ws, dzs = [None] * nb, [None] * nb
            for b in range(nb):
                cols = slice(b * SB_BLK, (b + 1) * SB_BLK)
                within, total = _sb_block_sums(lneg[:, cols], tri_later)
                later = later - total
                wb = jnp.exp(lpos[:, cols] + within + later)
                if masked:
                    wb = jnp.where(mask[:, cols], wb, 0.0)
                g = dw[:, cols] * wb
                g_within, g_total = _sb_block_sums(g, tri_before)
                dz = g - (g + g_before + g_within) * jnp.exp(lpos[:, cols])
                if masked:
                    dz = jnp.where(mask[:, cols], dz, 0.0)
                g_before = g_before + g_total
                ws[b] = wb.astype(BF16)
                dzs[b] = (dz * scale).astype(BF16)
            w = jnp.concatenate(ws, axis=1)
            dzb = jnp.concatenate(dzs, axis=1)
            dq = dq + jnp.dot(dzb, k, preferred_element_type=F32)
            dk_acc[rows, :] += lax.dot_general(dzb, q, (((0,), (0,)), ((), ())), preferred_element_type=F32)
            dv_acc[rows, :] += lax.dot_general(w, dob, (((0,), (0,)), ((), ())), preferred_element_type=F32)
            return later, g_before, dq

        carry = (jnp.broadcast_to(tot_ref[...], (t, SB_BLK)), jnp.zeros((t, SB_BLK), F32),
                 jnp.zeros((t, SB_HEAD_DIM), F32))
        _, _, dq = _sb_walk_up(i, step, carry)
        dq_ref[...] = dq.astype(dq_ref.dtype)

        @pl.when(i == nq - 1)
        def _():
            cols = pl.ds(pl.multiple_of(h * SB_HEAD_DIM, SB_HEAD_DIM), SB_HEAD_DIM)
            pltpu.sync_copy(dk_acc, dk_hbm.at[:, cols])
            pltpu.sync_copy(dv_acc, dv_hbm.at[:, cols])

    blk = lambda h, i: (i, h)
    return pl.pallas_call(
        body, name="sb_bwd", grid=(SB_HEADS, nq),
        in_specs=[pl.BlockSpec((t, SB_HEAD_DIM), blk),
                  pl.BlockSpec((l, SB_HEAD_DIM), lambda h, i: (0, SB_HEADS + h)),
                  pl.BlockSpec((l, SB_HEAD_DIM), lambda h, i: (0, 2 * SB_HEADS + h)),
                  pl.BlockSpec((None, t, 1), lambda h, i: (h, i, 0)), pl.BlockSpec((t, SB_HEAD_DIM), blk)],
        out_specs=[pl.BlockSpec((t, SB_HEAD_DIM), blk), pl.BlockSpec(memory_space=pl.ANY),
                   pl.BlockSpec(memory_space=pl.ANY)],
        out_shape=[jax.ShapeDtypeStruct((l, D_MODEL), BF16), jax.ShapeDtypeStruct((l, D_MODEL), F32),
                   jax.ShapeDtypeStruct((l, D_MODEL), F32)],
        scratch_shapes=[pltpu.VMEM((l, SB_HEAD_DIM), F32), pltpu.VMEM((l, SB_HEAD_DIM), F32)],
        compiler_params=_params(dimension_semantics=("arbitrary", "arbitrary")),
    )(qkv, qkv, qkv, row_total, dout)


_HBM = pl.BlockSpec(memory_space=pl.ANY)


def _other_chips(x, y):
    return [(1 - x, y), (x, 1 - y), (1 - x, 1 - y)]


def _comm_call(body, name, ins, out_shapes, n_remote, n_local):
    return pl.pallas_call(
        body, name=name, in_specs=[_HBM] * len(ins), out_specs=[_HBM] * len(out_shapes), out_shape=out_shapes,
        scratch_shapes=[pltpu.SemaphoreType.DMA((n_remote,)), pltpu.SemaphoreType.DMA((n_remote,)),
                        pltpu.SemaphoreType.DMA((max(n_local, 1),))],
    )(*ins)


def gather_shards(shards):
    n = len(shards)

    def body(*refs):
        ins, outs = refs[:n], refs[n:2 * n]
        send_sems, recv_sems, local_sems = refs[2 * n:]
        x, y, c = lax.axis_index("x"), lax.axis_index("y"), lax.axis_index("c")
        me = 2 * x + y
        own = [pltpu.make_async_copy(ins[t], outs[t].at[me], local_sems.at[t]) for t in range(n)]
        for cp in own:
            cp.start()

        def copy(t, k, px, py, slot):
            return pltpu.make_async_remote_copy(
                src_ref=ins[t], dst_ref=outs[t].at[slot], send_sem=send_sems.at[3 * t + k],
                recv_sem=recv_sems.at[3 * t + k], device_id=(px, py, c), device_id_type=MESH)

        chips = _other_chips(x, y)
        sends = [copy(t, k, px, py, me) for t in range(n) for k, (px, py) in enumerate(chips)]
        for cp in sends:
            cp.start()
        for t in range(n):
            for k, (px, py) in enumerate(chips):
                copy(t, k, px, py, 2 * px + py).wait_recv()
        for cp in sends:
            cp.wait_send()
        for cp in own:
            cp.wait()

    out_shapes = [jax.ShapeDtypeStruct((N_CHIPS,) + s.shape, s.dtype) for s in shards]
    return _comm_call(body, "gather_shards", shards, out_shapes, 3 * n, n)


def sibling_swap_halves(gs):
    n = len(gs)

    def body(*refs):
        ins, outs = refs[:n], refs[n:2 * n]
        send_sems, recv_sems, _ = refs[2 * n:]
        x, y, c = lax.axis_index("x"), lax.axis_index("y"), lax.axis_index("c")
        copies = []
        for t in range(n):
            rh = ins[t].shape[1] // 2
            src = ins[t].at[:, pl.ds(pl.multiple_of((1 - c) * rh, 8), rh), :]
            copies.append(pltpu.make_async_remote_copy(
                src_ref=src, dst_ref=outs[t], send_sem=send_sems.at[t], recv_sem=recv_sems.at[t],
                device_id=(x, y, 1 - c), device_id_type=MESH))
        for cp in copies:
            cp.start()
        for cp in copies:
            cp.wait_recv()
        for cp in copies:
            cp.wait_send()

    out_shapes = [jax.ShapeDtypeStruct((g.shape[0], g.shape[1] // 2, g.shape[2]), g.dtype) for g in gs]
    return _comm_call(body, "sibling_swap_halves", gs, out_shapes, n, 0)


def chip_exchange(ps):
    n = len(ps)

    def body(*refs):
        ins, outs = refs[:n], refs[n:2 * n]
        send_sems, recv_sems, local_sems = refs[2 * n:]
        x, y, c = lax.axis_index("x"), lax.axis_index("y"), lax.axis_index("c")
        me = 2 * x + y
        own = [pltpu.make_async_copy(ins[t].at[me], outs[t].at[me], local_sems.at[t]) for t in range(n)]
        for cp in own:
            cp.start()

        def copy(t, k, px, py, src_slot, dst_slot):
            return pltpu.make_async_remote_copy(
                src_ref=ins[t].at[src_slot], dst_ref=outs[t].at[dst_slot], send_sem=send_sems.at[3 * t + k],
                recv_sem=recv_sems.at[3 * t + k], device_id=(px, py, c), device_id_type=MESH)

        chips = _other_chips(x, y)
        sends = [copy(t, k, px, py, 2 * px + py, me) for t in range(n) for k, (px, py) in enumerate(chips)]
        for cp in sends:
            cp.start()
        for t in range(n):
            for k, (px, py) in enumerate(chips):
                copy(t, k, px, py, me, 2 * px + py).wait_recv()
        for cp in sends:
            cp.wait_send()
        for cp in own:
            cp.wait()

    out_shapes = [jax.ShapeDtypeStruct(p.shape, p.dtype) for p in ps]
    return _comm_call(body, "chip_exchange", ps, out_shapes, 3 * n, n)


def sibling_share(ss):
    n = len(ss)

    def body(*refs):
        ins, outs = refs[:n], refs[n:2 * n]
        send_sems, recv_sems, _ = refs[2 * n:]
        x, y, c = lax.axis_index("x"), lax.axis_index("y"), lax.axis_index("c")
        copies = [pltpu.make_async_remote_copy(
            src_ref=ins[t], dst_ref=outs[t], send_sem=send_sems.at[t], recv_sem=recv_sems.at[t],
            device_id=(x, y, 1 - c), device_id_type=MESH) for t in range(n)]
        for cp in copies:
            cp.start()
        for cp in copies:
            cp.wait_recv()
        for cp in copies:
            cp.wait_send()

    out_shapes = [jax.ShapeDtypeStruct(s.shape, s.dtype) for s in ss]
    return _comm_call(body, "sibling_share", ss, out_shapes, n, 0)


EW_BLOCK_BYTES = 2 * 1024 * 1024


def _ew_rows(rows, cols, copies=1):
    padded = -(-cols // LANES) * LANES
    for tr in (1024, 512, 256, 128, 64, 32, 16, 8):
        if rows % tr == 0 and copies * tr * padded * 4 <= EW_BLOCK_BYTES:
            return tr
    return rows


def add_pairs(a, b):
    rows, cols = a.shape
    tr = _ew_rows(rows, cols)

    def body(a_ref, b_ref, o_ref):
        o_ref[...] = a_ref[...] + b_ref[...]

    blk = pl.BlockSpec((tr, cols), lambda i: (i, 0))
    return pl.pallas_call(
        body, name="add_pairs", grid=(rows // tr,), in_specs=[blk, blk], out_specs=blk,
        out_shape=jax.ShapeDtypeStruct((rows, cols), F32),
        compiler_params=_params(dimension_semantics=("arbitrary",)),
    )(a, b)


def sum_chips(slots):
    _, rows, cols = slots.shape
    tr = _ew_rows(rows, cols, N_CHIPS)

    def body(s_ref, o_ref):
        acc = s_ref[0]
        for j in range(1, N_CHIPS):
            acc = acc + s_ref[j]
        o_ref[...] = acc

    return pl.pallas_call(
        body, name="sum_chips", grid=(rows // tr,),
        in_specs=[pl.BlockSpec((N_CHIPS, tr, cols), lambda i: (0, i, 0))],
        out_specs=pl.BlockSpec((tr, cols), lambda i: (i, 0)),
        out_shape=jax.ShapeDtypeStruct((rows, cols), F32),
        compiler_params=_params(dimension_semantics=("arbitrary",)),
    )(slots)


def adamw(g, w, m, v):
    rows, cols = g.shape
    tr = _ew_rows(rows, cols)

    def body(g_ref, w_ref, m_ref, v_ref, d_out, m_out, v_out):
        gv = g_ref[...]
        m_new = ADAM_B1 * m_ref[...] + (1.0 - ADAM_B1) * gv
        v_new = ADAM_B2 * v_ref[...] + (1.0 - ADAM_B2) * jnp.square(gv)
        m_hat = m_new / (1.0 - ADAM_B1 ** ADAM_STEP)
        v_hat = v_new / (1.0 - ADAM_B2 ** ADAM_STEP)
        d_out[...] = -ADAM_LR * (m_hat / (jnp.sqrt(v_hat) + ADAM_EPS) + ADAM_WD * w_ref[...])
        m_out[...] = m_new
        v_out[...] = v_new

    blk = pl.BlockSpec((tr, cols), lambda i: (i, 0))
    return pl.pallas_call(
        body, name="adamw", grid=(rows // tr,), in_specs=[blk] * 4, out_specs=[blk] * 3,
        out_shape=[jax.ShapeDtypeStruct((rows, cols), F32)] * 3,
        compiler_params=_params(dimension_semantics=("arbitrary",)),
    )(g, w, m, v)


def reduce_gradients(quarters):
    theirs = sibling_swap_halves(quarters)
    c = lax.axis_index("c")
    chip_partials = []
    for q, t in zip(quarters, theirs):
        four, rh, cols = t.shape
        mine = lax.dynamic_slice_in_dim(q, c * rh, rh, axis=1)
        chip_partials.append(add_pairs(mine.reshape(four * rh, cols), t.reshape(four * rh, cols))
                             .reshape(four, rh, cols))
    slots = chip_exchange(chip_partials)
    mine = [sum_chips(s) for s in slots]
    theirs = sibling_share(mine)
    return [jnp.concatenate([jnp.where(c == 0, m, t), jnp.where(c == 0, t, m)], axis=0)
            for m, t in zip(mine, theirs)]


def _pack(pieces, dtype, row_multiple):
    flat = jnp.concatenate([p.astype(dtype).reshape(-1) for p in pieces])
    per = row_multiple * LANES
    padded = -(-flat.shape[0] // per) * per
    flat = jnp.pad(flat, (0, padded - flat.shape[0]))
    return flat.reshape(-1, LANES)


def _unpack(buf, shapes):
    flat = buf.reshape(-1)
    out, off = [], 0
    for s in shapes:
        n = int(np.prod(s))
        out.append(flat[off:off + n].reshape(s))
        off += n
    return out


RET_SCALE = RET_QK_DIM ** -0.5
RET_LOG_GAMMA = [math.log(1.0 - 2.0 ** (-5.0 - h)) for h in range(RET_HEADS)]
RET_HALF = RET_QK_DIM // 2


def _ret_tables(length):
    inv = ROPE_BASE ** (-jnp.arange(RET_HALF, dtype=F32) / RET_HALF)
    ang = jnp.arange(length).astype(F32)[:, None] * inv[None, :]
    log_gamma = jnp.log(1.0 - jnp.power(2.0, -5.0 - jnp.arange(RET_HEADS, dtype=F32)))
    idx = jnp.arange(CHUNK, dtype=F32)
    rel = idx[:, None] - idx[None, :]
    dmask = jnp.where(rel >= 0, jnp.exp(log_gamma[:, None, None] * jnp.maximum(rel, 0.0)), 0.0)
    k_decay = jnp.exp(log_gamma[:, None] * (CHUNK - 1 - idx)[None, :])[:, :, None]
    q_decay = jnp.exp(log_gamma[:, None] * (idx + 1.0)[None, :])[:, :, None]
    return jnp.cos(ang), jnp.sin(ang), dmask, k_decay, q_decay


def _rot(x, cs, sn):
    x1, x2 = x[:, :RET_HALF], x[:, RET_HALF:]
    return jnp.concatenate([x1 * cs - x2 * sn, x1 * sn + x2 * cs], axis=1)


def _unrot(d, cs, sn):
    d1, d2 = d[:, :RET_HALF], d[:, RET_HALF:]
    return jnp.concatenate([d1 * cs + d2 * sn, d2 * cs - d1 * sn], axis=1)


def _sigmoid(x):
    return 1.0 / (1.0 + jnp.exp(-x))


_NT = (((1,), (1,)), ((), ()))
_TN = (((0,), (0,)), ((), ()))


def _ret_specs(nc, rev):
    ch = (lambda c: nc - 1 - c) if rev else (lambda c: c)
    row = lambda w: pl.BlockSpec((CHUNK, w), lambda c: (ch(c), 0))
    const3 = lambda a, b: pl.BlockSpec((RET_HEADS, a, b), lambda c: (0, 0, 0))
    tables = [row(RET_HALF), row(RET_HALF), const3(CHUNK, CHUNK), const3(CHUNK, 1), const3(CHUNK, 1)]
    state = pl.BlockSpec((None, RET_HEADS, RET_QK_DIM, RET_V_DIM), lambda c: (ch(c), 0, 0, 0))
    return row, tables, state


def ret_fwd(p, tables):
    length = p.shape[0]
    nc = length // CHUNK
    row, table_specs, state_spec = _ret_specs(nc, False)

    def body(p_ref, cos_ref, sin_ref, dm_ref, kd_ref, qd_ref, y_ref, ypre_ref, st_ref, r_scr):
        c = pl.program_id(0)

        @pl.when(c == 0)
        def _():
            r_scr[...] = jnp.zeros_like(r_scr)

        cs, sn = cos_ref[...], sin_ref[...]
        valid = (c * CHUNK + lax.broadcasted_iota(jnp.int32, (CHUNK, 1), 0)) >= PAD
        for h in range(RET_HEADS):
            col = lambda part: slice(part * D_MODEL + h * RET_QK_DIM, part * D_MODEL + (h + 1) * RET_QK_DIM)
            qb = _rot(p_ref[:, col(0)], cs, sn).astype(BF16)
            kr = _rot(p_ref[:, col(1)], cs, sn) * RET_SCALE
            kb = kr.astype(BF16)
            vb = jnp.where(valid, p_ref[:, col(2)], 0.0).astype(BF16)
            s = lax.dot_general(qb, kb, _NT, preferred_element_type=F32) * dm_ref[h]
            r = r_scr[h]
            st_ref[h] = r
            y = (jnp.dot(s.astype(BF16), vb, preferred_element_type=F32)
                 + jnp.dot(qb, r.astype(BF16), preferred_element_type=F32) * qd_ref[h])
            kdb = (kr * kd_ref[h]).astype(BF16)
            r_scr[h] = r * math.exp(RET_LOG_GAMMA[h] * CHUNK) + lax.dot_general(kdb, vb, _TN,
                                                                                preferred_element_type=F32)
            out = slice(h * RET_V_DIM, (h + 1) * RET_V_DIM)
            ypre_ref[:, out] = y
            mu = jnp.mean(y, axis=-1, keepdims=True)
            yc = y - mu
            yn = yc * lax.rsqrt(jnp.mean(yc * yc, axis=-1, keepdims=True) + EPS)
            g = p_ref[:, col(3)]
            y_ref[:, out] = (yn * (g * _sigmoid(g))).astype(y_ref.dtype)

    return pl.pallas_call(
        body, name="ret_fwd", grid=(nc,),
        in_specs=[row(4 * D_MODEL)] + table_specs,
        out_specs=[row(D_MODEL), row(D_MODEL), state_spec],
        out_shape=[jax.ShapeDtypeStruct((length, D_MODEL), BF16), jax.ShapeDtypeStruct((length, D_MODEL), F32),
                   jax.ShapeDtypeStruct((nc, RET_HEADS, RET_QK_DIM, RET_V_DIM), F32)],
        scratch_shapes=[pltpu.VMEM((RET_HEADS, RET_QK_DIM, RET_V_DIM), F32)],
        compiler_params=_params(dimension_semantics=("arbitrary",)),
    )(p, *tables)


def ret_bwd(p, tables, ypre, states, dyo):
    length = p.shape[0]
    nc = length // CHUNK
    row, table_specs, state_spec = _ret_specs(nc, True)

    def body(p_ref, cos_ref, sin_ref, dm_ref, kd_ref, qd_ref, ypre_ref, st_ref, dyo_ref, dp_ref, dr_scr):
        c = pl.program_id(0)

        @pl.when(c == 0)
        def _():
            dr_scr[...] = jnp.zeros_like(dr_scr)

        cs, sn = cos_ref[...], sin_ref[...]
        valid = ((nc - 1 - c) * CHUNK + lax.broadcasted_iota(jnp.int32, (CHUNK, 1), 0)) >= PAD
        for h in range(RET_HEADS):
            col = lambda part: slice(part * D_MODEL + h * RET_QK_DIM, part * D_MODEL + (h + 1) * RET_QK_DIM)
            out = slice(h * RET_V_DIM, (h + 1) * RET_V_DIM)
            qb = _rot(p_ref[:, col(0)], cs, sn).astype(BF16)
            kr = _rot(p_ref[:, col(1)], cs, sn) * RET_SCALE
            kb = kr.astype(BF16)
            vb = jnp.where(valid, p_ref[:, col(2)], 0.0).astype(BF16)
            g = p_ref[:, col(3)]
            y = ypre_ref[:, out]
            dyo_h = dyo_ref[:, out]
            mu = jnp.mean(y, axis=-1, keepdims=True)
            yc = y - mu
            rs = lax.rsqrt(jnp.mean(yc * yc, axis=-1, keepdims=True) + EPS)
            xh = yc * rs
            sg = _sigmoid(g)
            dp_ref[:, col(3)] = (dyo_h * xh * (sg * (1.0 + g * (1.0 - sg)))).astype(dp_ref.dtype)
            dyn = dyo_h * (g * sg)
            dy = rs * (dyn - jnp.mean(dyn, axis=-1, keepdims=True)
                       - xh * jnp.mean(dyn * xh, axis=-1, keepdims=True))
            dyb = dy.astype(BF16)
            dm = dm_ref[h]
            sm = (lax.dot_general(qb, kb, _NT, preferred_element_type=F32) * dm).astype(BF16)
            dsb = (lax.dot_general(dyb, vb, _NT, preferred_element_type=F32) * dm).astype(BF16)
            rb = st_ref[h].astype(BF16)
            dyqb = (dy * qd_ref[h]).astype(BF16)
            dr = dr_scr[h]
            drb = dr.astype(BF16)
            kd = kd_ref[h]
            dq = (jnp.dot(dsb, kb, preferred_element_type=F32)
                  + lax.dot_general(dyqb, rb, _NT, preferred_element_type=F32))
            dk = (lax.dot_general(dsb, qb, _TN, preferred_element_type=F32)
                  + lax.dot_general(vb, drb, _NT, preferred_element_type=F32) * kd)
            dv = (lax.dot_general(sm, dyb, _TN, preferred_element_type=F32)
                  + jnp.dot((kr * kd).astype(BF16), drb, preferred_element_type=F32))
            dr_scr[h] = dr * math.exp(RET_LOG_GAMMA[h] * CHUNK) + lax.dot_general(qb, dyqb, _TN,
                                                                                 preferred_element_type=F32)
            dp_ref[:, col(0)] = _unrot(dq, cs, sn).astype(dp_ref.dtype)
            dp_ref[:, col(1)] = (_unrot(dk, cs, sn) * RET_SCALE).astype(dp_ref.dtype)
            dp_ref[:, col(2)] = jnp.where(valid, dv, 0.0).astype(dp_ref.dtype)

    return pl.pallas_call(
        body, name="ret_bwd", grid=(nc,),
        in_specs=[row(4 * D_MODEL)] + table_specs + [row(D_MODEL), state_spec, row(D_MODEL)],
        out_specs=row(4 * D_MODEL),
        out_shape=jax.ShapeDtypeStruct((length, 4 * D_MODEL), BF16),
        scratch_shapes=[pltpu.VMEM((RET_HEADS, RET_QK_DIM, RET_V_DIM), F32)],
        compiler_params=_params(dimension_semantics=("arbitrary",)),
    )(p, *tables, ypre, states, dyo)


def _shift_down(cur, prev, m):
    if m == 0:
        return cur
    rows = lax.broadcasted_iota(jnp.int32, cur.shape, 0)
    return jnp.where(rows < m, pltpu.roll(prev, m, 0), pltpu.roll(cur, m, 0))


def _shift_up(cur, nxt, m):
    if m == 0:
        return cur
    n = cur.shape[0]
    rows = lax.broadcasted_iota(jnp.int32, cur.shape, 0)
    return jnp.where(rows >= n - m, pltpu.roll(nxt, n - m, 0), pltpu.roll(cur, n - m, 0))


def conv_mixer_fwd(p, conv_w):
    length = p.shape[0]
    nc = length // CHUNK
    kt = conv_w.shape[0]

    def body(cur_ref, prev_ref, w_ref, y_ref):
        c = pl.program_id(0)
        rows = lax.broadcasted_iota(jnp.int32, (CHUNK, 1), 0)

        def u_of(ref, blk):
            ok = (blk * CHUNK + rows >= PAD) & (blk >= 0)
            return jnp.where(ok, ref[:, D_MODEL:2 * D_MODEL] * ref[:, 2 * D_MODEL:], 0.0)

        u_cur = u_of(cur_ref, c)
        u_prev = u_of(prev_ref, c - 1)
        acc = jnp.zeros((CHUNK, D_MODEL), F32)
        for i in range(kt):
            acc = acc + _shift_down(u_cur, u_prev, kt - 1 - i) * w_ref[i:i + 1, :]
        y_ref[...] = (cur_ref[:, :D_MODEL] * acc).astype(y_ref.dtype)

    return pl.pallas_call(
        body, name="conv_mixer_fwd", grid=(nc,),
        in_specs=[pl.BlockSpec((CHUNK, 3 * D_MODEL), lambda c: (c, 0)),
                  pl.BlockSpec((CHUNK, 3 * D_MODEL), lambda c: (jnp.maximum(c - 1, 0), 0)),
                  pl.BlockSpec((kt, D_MODEL), lambda c: (0, 0))],
        out_specs=pl.BlockSpec((CHUNK, D_MODEL), lambda c: (c, 0)),
        out_shape=jax.ShapeDtypeStruct((length, D_MODEL), BF16),
        compiler_params=_params(dimension_semantics=("arbitrary",)),
    )(p, p, conv_w)


def conv_mixer_bwd(p, conv_w, dy):
    length = p.shape[0]
    nc = length // CHUNK
    kt = conv_w.shape[0]

    def body(cur_ref, prev_ref, w_ref, dy_ref, dyn_ref, pn_ref, dp_ref, dw_ref):
        c = pl.program_id(0)
        rows = lax.broadcasted_iota(jnp.int32, (CHUNK, 1), 0)

        def u_of(ref, blk):
            ok = (blk * CHUNK + rows >= PAD) & (blk >= 0)
            return jnp.where(ok, ref[:, D_MODEL:2 * D_MODEL] * ref[:, 2 * D_MODEL:], 0.0)

        u_cur = u_of(cur_ref, c)
        u_prev = u_of(prev_ref, c - 1)
        b_gate = cur_ref[:, :D_MODEL]
        dyv = dy_ref[...]
        dconv = dyv * b_gate
        dconv_next = jnp.where(c + 1 < nc, dyn_ref[...] * pn_ref[:, :D_MODEL], 0.0)

        @pl.when(c == 0)
        def _():
            dw_ref[...] = jnp.zeros_like(dw_ref)

        acc = jnp.zeros((CHUNK, D_MODEL), F32)
        du = jnp.zeros((CHUNK, D_MODEL), F32)
        for i in range(kt):
            shifted = _shift_down(u_cur, u_prev, kt - 1 - i)
            acc = acc + shifted * w_ref[i:i + 1, :]
            dw_ref[i:i + 1, :] += jnp.sum(dconv * shifted, axis=0, keepdims=True)
            du = du + _shift_up(dconv, dconv_next, kt - 1 - i) * w_ref[i:i + 1, :]
        du = jnp.where(c * CHUNK + rows >= PAD, du, 0.0)
        dp_ref[:, :D_MODEL] = (dyv * acc).astype(dp_ref.dtype)
        dp_ref[:, D_MODEL:2 * D_MODEL] = (du * cur_ref[:, 2 * D_MODEL:]).astype(dp_ref.dtype)
        dp_ref[:, 2 * D_MODEL:] = (du * cur_ref[:, D_MODEL:2 * D_MODEL]).astype(dp_ref.dtype)

    nxt = lambda c: (jnp.minimum(c + 1, nc - 1), 0)
    return pl.pallas_call(
        body, name="conv_mixer_bwd", grid=(nc,),
        in_specs=[pl.BlockSpec((CHUNK, 3 * D_MODEL), lambda c: (c, 0)),
                  pl.BlockSpec((CHUNK, 3 * D_MODEL), lambda c: (jnp.maximum(c - 1, 0), 0)),
                  pl.BlockSpec((kt, D_MODEL), lambda c: (0, 0)),
                  pl.BlockSpec((CHUNK, D_MODEL), lambda c: (c, 0)),
                  pl.BlockSpec((CHUNK, D_MODEL), nxt),
                  pl.BlockSpec((CHUNK, 3 * D_MODEL), nxt)],
        out_specs=[pl.BlockSpec((CHUNK, 3 * D_MODEL), lambda c: (c, 0)),
                   pl.BlockSpec((kt, D_MODEL), lambda c: (0, 0))],
        out_shape=[jax.ShapeDtypeStruct((length, 3 * D_MODEL), BF16), jax.ShapeDtypeStruct((kt, D_MODEL), F32)],
        compiler_params=_params(dimension_semantics=("arbitrary",)),
    )(p, p, conv_w, dy, dy, p)


def merge_fwd(gate_logits, ups):
    length = gate_logits.shape[0]
    tm = _tile(length, (384, 256, 128))

    def body(g_ref, u0, u1, u2, u3, o_ref):
        acc = jnp.zeros((tm, D_MODEL), F32)
        for n, u in enumerate((u0, u1, u2, u3)):
            acc = acc + _sigmoid(g_ref[:, n * D_MODEL:(n + 1) * D_MODEL]) * u[...]
        o_ref[...] = acc.astype(o_ref.dtype)

    row = pl.BlockSpec((tm, D_MODEL), lambda i: (i, 0))
    return pl.pallas_call(
        body, name="merge_fwd", grid=(length // tm,),
        in_specs=[pl.BlockSpec((tm, N_BRANCH * D_MODEL), lambda i: (i, 0))] + [row] * N_BRANCH,
        out_specs=row, out_shape=jax.ShapeDtypeStruct((length, D_MODEL), BF16),
        compiler_params=_params(dimension_semantics=("arbitrary",)),
    )(gate_logits, *ups)


def merge_bwd(gate_logits, ups, dmerged):
    length = gate_logits.shape[0]
    tm = _tile(length, (384, 256, 128))

    def body(g_ref, u0, u1, u2, u3, dm_ref, dg_ref, d0, d1, d2, d3):
        dm = dm_ref[...]
        for n, (u, du) in enumerate(((u0, d0), (u1, d1), (u2, d2), (u3, d3))):
            cols = slice(n * D_MODEL, (n + 1) * D_MODEL)
            s = _sigmoid(g_ref[:, cols])
            du[...] = (dm * s).astype(du.dtype)
            dg_ref[:, cols] = (dm * u[...] * (s * (1.0 - s))).astype(dg_ref.dtype)

    row = pl.BlockSpec((tm, D_MODEL), lambda i: (i, 0))
    wide = pl.BlockSpec((tm, N_BRANCH * D_MODEL), lambda i: (i, 0))
    outs = pl.pallas_call(
        body, name="merge_bwd", grid=(length // tm,),
        in_specs=[wide] + [row] * (N_BRANCH + 1),
        out_specs=[wide] + [row] * N_BRANCH,
        out_shape=[jax.ShapeDtypeStruct((length, N_BRANCH * D_MODEL), BF16)]
        + [jax.ShapeDtypeStruct((length, D_MODEL), BF16)] * N_BRANCH,
        compiler_params=_params(dimension_semantics=("arbitrary",)),
    )(gate_logits, *ups, dmerged)
    return outs[0], list(outs[1:])


def swiglu_fwd(f):
    length = f.shape[0]
    tm = _tile(length, (384, 256, 128))

    def body(f_ref, o_ref):
        a = f_ref[:, :D_FF]
        o_ref[...] = (a * _sigmoid(a) * f_ref[:, D_FF:]).astype(o_ref.dtype)

    return pl.pallas_call(
        body, name="swiglu_fwd", grid=(length // tm,),
        in_specs=[pl.BlockSpec((tm, 2 * D_FF), lambda i: (i, 0))],
        out_specs=pl.BlockSpec((tm, D_FF), lambda i: (i, 0)),
        out_shape=jax.ShapeDtypeStruct((length, D_FF), BF16),
        compiler_params=_params(dimension_semantics=("arbitrary",)),
    )(f)


def swiglu_bwd(f, dact):
    length = f.shape[0]
    tm = _tile(length, (384, 256, 128))

    def body(f_ref, d_ref, df_ref):
        a = f_ref[:, :D_FF]
        up = f_ref[:, D_FF:]
        d = d_ref[...]
        s = _sigmoid(a)
        df_ref[:, :D_FF] = (d * up * (s * (1.0 + a * (1.0 - s)))).astype(df_ref.dtype)
        df_ref[:, D_FF:] = (d * (a * s)).astype(df_ref.dtype)

    return pl.pallas_call(
        body, name="swiglu_bwd", grid=(length // tm,),
        in_specs=[pl.BlockSpec((tm, 2 * D_FF), lambda i: (i, 0)), pl.BlockSpec((tm, D_FF), lambda i: (i, 0))],
        out_specs=pl.BlockSpec((tm, 2 * D_FF), lambda i: (i, 0)),
        out_shape=jax.ShapeDtypeStruct((length, 2 * D_FF), BF16),
        compiler_params=_params(dimension_semantics=("arbitrary",)),
    )(f, dact)


SSD_PAIRS = SSD_HEADS // 2
SSD_XBC = SSD_CONV_DIM
SSD_GW = SSD_INNER // SSD_GROUPS


def _split3(x):
    h1 = x.astype(BF16)
    r1 = x - h1.astype(F32)
    h2 = r1.astype(BF16)
    h3 = (r1 - h2.astype(F32)).astype(BF16)
    return h1, h2, h3


def _tri_apply(tri, x, dims):
    out = None
    for part in _split3(x):
        t = lax.dot_general(tri, part, dims, preferred_element_type=F32)
        out = t if out is None else out + t
    return out


def _softplus(x):
    return jnp.maximum(x, 0.0) + jnp.log(1.0 + jnp.exp(-jnp.abs(x)))


def _lane_pair(x, pair):
    lanes = lax.broadcasted_iota(jnp.int32, (x.shape[0], LANES), 1)
    return jnp.where(lanes < SSD_HEAD_DIM, x[:, 2 * pair:2 * pair + 1], x[:, 2 * pair + 1:2 * pair + 2])


def _half_sums(t):
    lanes = lax.broadcasted_iota(jnp.int32, t.shape, 1)
    lo = jnp.sum(jnp.where(lanes < SSD_HEAD_DIM, t, 0.0), axis=1, keepdims=True)
    return lo, jnp.sum(t, axis=1, keepdims=True) - lo


def _put_cols(cols):
    rows = cols[0].shape[0]
    lanes = lax.broadcasted_iota(jnp.int32, (rows, LANES), 1)
    out = jnp.zeros((rows, LANES), F32)
    for h, col in enumerate(cols):
        out = out + jnp.where(lanes == h, col, 0.0)
    return out


def ssd_pre_fwd(p, dt_raw, conv_w, conv_b, dt_bias, a_log):
    length = p.shape[0]
    nc = length // CHUNK
    kt = conv_w.shape[0]

    def body(cur_ref, prev_ref, raw_ref, w_ref, b_ref, bias_ref, alog_ref, act_ref, dt_ref, a_ref):
        c = pl.program_id(0)
        rows = lax.broadcasted_iota(jnp.int32, (CHUNK, 1), 0)
        vm = c * CHUNK + rows >= PAD
        u_cur = jnp.where(vm, cur_ref[:, SSD_INNER:], 0.0)
        u_prev = jnp.where(((c - 1) * CHUNK + rows >= PAD) & (c >= 1), prev_ref[:, SSD_INNER:], 0.0)
        pre = jnp.zeros((CHUNK, SSD_XBC), F32) + b_ref[...]
        for i in range(kt):
            pre = pre + _shift_down(u_cur, u_prev, kt - 1 - i) * w_ref[i:i + 1, :]
        act = pre * _sigmoid(pre)
        act_ref[:, :SSD_INNER] = jnp.where(vm, act[:, :SSD_INNER], 0.0)
        act_ref[:, SSD_INNER:] = act[:, SSD_INNER:]
        dt = _softplus(raw_ref[...] + bias_ref[...])
        dt_ref[...] = dt
        a_ref[...] = -jnp.exp(alog_ref[...]) * dt

    row = lambda w: pl.BlockSpec((CHUNK, w), lambda c: (c, 0))
    vec = lambda w: pl.BlockSpec((1, w), lambda c: (0, 0))
    return pl.pallas_call(
        body, name="ssd_pre_fwd", grid=(nc,),
        in_specs=[row(3 * D_MODEL), pl.BlockSpec((CHUNK, 3 * D_MODEL), lambda c: (jnp.maximum(c - 1, 0), 0)),
                  row(LANES), pl.BlockSpec((kt, SSD_XBC), lambda c: (0, 0)), vec(SSD_XBC), vec(LANES), vec(LANES)],
        out_specs=[row(SSD_XBC), row(LANES), row(LANES)],
        out_shape=[jax.ShapeDtypeStruct((length, SSD_XBC), F32), jax.ShapeDtypeStruct((length, LANES), F32),
                   jax.ShapeDtypeStruct((length, LANES), F32)],
        compiler_params=_params(dimension_semantics=("arbitrary",)),
    )(p, p, dt_raw, conv_w, conv_b, dt_bias, a_log)


def ssd_pre_bwd(p, dt_raw, conv_w, conv_b, dt_bias, a_log, dact, ddt, da, dz):
    length = p.shape[0]
    nc = length // CHUNK
    kt = conv_w.shape[0]

    def body(cur_ref, prev_ref, raw_ref, w_ref, b_ref, bias_ref, alog_ref, dact_ref, ddt_ref, da_ref, dz_ref,
             dp_ref, draw_ref, dw_ref, db_ref, dbias_ref, dalog_ref, dpre_next):
        step = pl.program_id(0)
        c = nc - 1 - step
        rows = lax.broadcasted_iota(jnp.int32, (CHUNK, 1), 0)
        vm = c * CHUNK + rows >= PAD

        @pl.when(step == 0)
        def _():
            dpre_next[...] = jnp.zeros_like(dpre_next)
            dw_ref[...] = jnp.zeros_like(dw_ref)
            db_ref[...] = jnp.zeros_like(db_ref)
            dbias_ref[...] = jnp.zeros_like(dbias_ref)
            dalog_ref[...] = jnp.zeros_like(dalog_ref)

        u_cur = jnp.where(vm, cur_ref[:, SSD_INNER:], 0.0)
        u_prev = jnp.where(((c - 1) * CHUNK + rows >= PAD) & (c >= 1), prev_ref[:, SSD_INNER:], 0.0)
        shifted = [_shift_down(u_cur, u_prev, kt - 1 - i) for i in range(kt)]
        pre = jnp.zeros((CHUNK, SSD_XBC), F32) + b_ref[...]
        for i in range(kt):
            pre = pre + shifted[i] * w_ref[i:i + 1, :]
        sg = _sigmoid(pre)
        lanes = lax.broadcasted_iota(jnp.int32, (CHUNK, SSD_XBC), 1)
        dact_v = jnp.where(vm | (lanes >= SSD_INNER), dact_ref[...], 0.0)
        dpre = dact_v * (sg * (1.0 + pre * (1.0 - sg)))
        db_ref[...] += jnp.sum(dpre, axis=0, keepdims=True)
        nxt = dpre_next[...]
        du = jnp.zeros((CHUNK, SSD_XBC), F32)
        for i in range(kt):
            dw_ref[i:i + 1, :] += jnp.sum(dpre * shifted[i], axis=0, keepdims=True)
            du = du + _shift_up(dpre, nxt, kt - 1 - i) * w_ref[i:i + 1, :]
        dpre_next[...] = dpre
        dp_ref[:, :SSD_INNER] = dz_ref[...].astype(dp_ref.dtype)
        dp_ref[:, SSD_INNER:] = jnp.where(vm, du, 0.0).astype(dp_ref.dtype)
        x = raw_ref[...] + bias_ref[...]
        neg_exp = -jnp.exp(alog_ref[...])
        dav = da_ref[...]
        draw = (ddt_ref[...] + dav * neg_exp) * _sigmoid(x)
        draw_ref[...] = draw.astype(draw_ref.dtype)
        dbias_ref[...] += jnp.sum(draw, axis=0, keepdims=True)
        dalog_ref[...] += jnp.sum(dav * (neg_exp * _softplus(x)), axis=0, keepdims=True)

    rev = lambda c: (nc - 1 - c, 0)
    row = lambda w: pl.BlockSpec((CHUNK, w), rev)
    vec = lambda w: pl.BlockSpec((1, w), lambda c: (0, 0))
    taps = pl.BlockSpec((kt, SSD_XBC), lambda c: (0, 0))
    return pl.pallas_call(
        body, name="ssd_pre_bwd", grid=(nc,),
        in_specs=[row(3 * D_MODEL),
                  pl.BlockSpec((CHUNK, 3 * D_MODEL), lambda c: (jnp.maximum(nc - 2 - c, 0), 0)),
                  row(LANES), taps, vec(SSD_XBC), vec(LANES), vec(LANES),
                  row(SSD_XBC), row(LANES), row(LANES), row(SSD_INNER)],
        out_specs=[row(3 * D_MODEL), row(LANES), taps, vec(SSD_XBC), vec(LANES), vec(LANES)],
        out_shape=[jax.ShapeDtypeStruct((length, 3 * D_MODEL), BF16), jax.ShapeDtypeStruct((length, LANES), BF16),
                   jax.ShapeDtypeStruct((kt, SSD_XBC), F32), jax.ShapeDtypeStruct((1, SSD_XBC), F32),
                   jax.ShapeDtypeStruct((1, LANES), F32), jax.ShapeDtypeStruct((1, LANES), F32)],
        scratch_shapes=[pltpu.VMEM((CHUNK, SSD_XBC), F32)],
        compiler_params=_params(dimension_semantics=("arbitrary",)),
    )(p, p, dt_raw, conv_w, conv_b, dt_bias, a_log, dact, ddt, da, dz)


def _tri_apply_lhs_t(x, tri):
    out = None
    for part in _split3(x):
        t = lax.dot_general(part, tri, (((0,), (1,)), ((), ())), preferred_element_type=F32)
        out = t if out is None else out + t
    return out


def ssd_core_fwd(act, dt, a, d_skip):
    length = act.shape[0]
    nc = length // CHUNK

    def body(act_ref, dt_ref, a_ref, dskip_ref, y_ref, st_ref, h_scr):
        c = pl.program_id(0)

        @pl.when(c == 0)
        def _():
            h_scr[...] = jnp.zeros_like(h_scr)

        r = lax.broadcasted_iota(jnp.int32, (CHUNK, CHUNK), 0)
        s = lax.broadcasted_iota(jnp.int32, (CHUNK, CHUNK), 1)
        causal = r >= s
        incl = jnp.where(causal, 1.0, 0.0).astype(BF16)
        a_v = a_ref[...]
        acs = _tri_apply(incl, a_v, (((1,), (0,)), ((), ())))
        acs_t = _tri_apply_lhs_t(a_v, incl)
        dt_v = dt_ref[...]
        lanes = lax.broadcasted_iota(jnp.int32, (CHUNK, LANES), 1)
        low = lanes < SSD_HEAD_DIM
        for g in range(SSD_GROUPS):
            bg = act_ref[:, SSD_INNER + g * SSD_STATE:SSD_INNER + (g + 1) * SSD_STATE].astype(BF16)
            cg = act_ref[:, SSD_INNER + (SSD_GROUPS + g) * SSD_STATE:
                         SSD_INNER + (SSD_GROUPS + g + 1) * SSD_STATE].astype(BF16)
            cb = lax.dot_general(cg, bg, _NT, preferred_element_type=F32)
            for pair in (2 * g, 2 * g + 1):
                cols = slice(pair * LANES, (pair + 1) * LANES)
                xs = act_ref[:, cols]
                x = xs * _lane_pair(dt_v, pair)
                ydiag = jnp.zeros((CHUNK, LANES), F32)
                for k, keep in ((0, low), (1, ~low)):
                    h = 2 * pair + k
                    seg = jnp.where(causal, jnp.exp(acs[:, h:h + 1] - acs_t[h:h + 1, :]), 0.0)
                    ydiag = ydiag + jnp.dot((cb * seg).astype(BF16), jnp.where(keep, x, 0.0).astype(BF16),
                                            preferred_element_type=F32)
                acs_p = _lane_pair(acs, pair)
                last = acs_p[CHUNK - 1:CHUNK, :]
                xds = (x * jnp.exp(last - acs_p)).astype(BF16)
                hprev = h_scr[pair]
                st_ref[pair] = hprev
                yoff = lax.dot_general(cg, hprev.astype(BF16), _NT, preferred_element_type=F32) * jnp.exp(acs_p)
                prow = lax.broadcasted_iota(jnp.int32, (LANES, 1), 0)
                cd = jnp.where(prow < SSD_HEAD_DIM, jnp.exp(acs_t[2 * pair:2 * pair + 1, CHUNK - 1:CHUNK]),
                               jnp.exp(acs_t[2 * pair + 1:2 * pair + 2, CHUNK - 1:CHUNK]))
                h_scr[pair] = hprev * cd + lax.dot_general(xds, bg, _TN, preferred_element_type=F32)
                y_ref[:, cols] = ydiag + yoff + xs * dskip_ref[:, cols]

    row = lambda w: pl.BlockSpec((CHUNK, w), lambda c: (c, 0))
    return pl.pallas_call(
        body, name="ssd_core_fwd", grid=(nc,),
        in_specs=[row(SSD_XBC), row(LANES), row(LANES), pl.BlockSpec((1, SSD_INNER), lambda c: (0, 0))],
        out_specs=[row(SSD_INNER), pl.BlockSpec((None, SSD_PAIRS, LANES, SSD_STATE), lambda c: (c, 0, 0, 0))],
        out_shape=[jax.ShapeDtypeStruct((length, SSD_INNER), F32),
                   jax.ShapeDtypeStruct((nc, SSD_PAIRS, LANES, SSD_STATE), F32)],
        scratch_shapes=[pltpu.VMEM((SSD_PAIRS, LANES, SSD_STATE), F32)],
        compiler_params=_params(dimension_semantics=("arbitrary",)),
    )(act, dt, a, d_skip)


def ssd_core_bwd(act, dt, a, d_skip, states, dy):
    length = act.shape[0]
    nc = length // CHUNK

    def body(act_ref, dt_ref, a_ref, dskip_ref, st_ref, dy_ref, dact_ref, ddt_ref, da_ref, dds_ref, dh_scr):
        step = pl.program_id(0)

        @pl.when(step == 0)
        def _():
            dh_scr[...] = jnp.zeros_like(dh_scr)
            dds_ref[...] = jnp.zeros_like(dds_ref)

        r = lax.broadcasted_iota(jnp.int32, (CHUNK, CHUNK), 0)
        s = lax.broadcasted_iota(jnp.int32, (CHUNK, CHUNK), 1)
        causal = r >= s
        incl = jnp.where(causal, 1.0, 0.0).astype(BF16)
        a_v = a_ref[...]
        acs = _tri_apply(incl, a_v, (((1,), (0,)), ((), ())))
        acs_t = _tri_apply_lhs_t(a_v, incl)
        dt_v = dt_ref[...]
        lanes = lax.broadcasted_iota(jnp.int32, (CHUNK, LANES), 1)
        low = lanes < SSD_HEAD_DIM
        prow = lax.broadcasted_iota(jnp.int32, (LANES, 1), 0)
        is_last = lax.broadcasted_iota(jnp.int32, (CHUNK, 1), 0) == CHUNK - 1
        dacs_cols = [None] * SSD_HEADS
        dacs_rows = [None] * SSD_HEADS
        ddt_cols = [None] * SSD_HEADS
        for g in range(SSD_GROUPS):
            b_cols = slice(SSD_INNER + g * SSD_STATE, SSD_INNER + (g + 1) * SSD_STATE)
            c_cols = slice(SSD_INNER + (SSD_GROUPS + g) * SSD_STATE, SSD_INNER + (SSD_GROUPS + g + 1) * SSD_STATE)
            bg = act_ref[:, b_cols].astype(BF16)
            cg = act_ref[:, c_cols].astype(BF16)
            cb = lax.dot_general(cg, bg, _NT, preferred_element_type=F32)
            dcb = jnp.zeros((CHUNK, CHUNK), F32)
            dbg = jnp.zeros((CHUNK, SSD_STATE), F32)
            dcg = jnp.zeros((CHUNK, SSD_STATE), F32)
            for pair in (2 * g, 2 * g + 1):
                cols = slice(pair * LANES, (pair + 1) * LANES)
                xs = act_ref[:, cols]
                dtp = _lane_pair(dt_v, pair)
                x = xs * dtp
                xb = x.astype(BF16)
                dyv = dy_ref[:, cols]
                dyb = dyv.astype(BF16)
                dds_ref[:, cols] += jnp.sum(dyv * xs, axis=0, keepdims=True)
                acs_p = _lane_pair(acs, pair)
                last = acs_p[CHUNK - 1:CHUNK, :]
                ds = jnp.exp(last - acs_p)
                ea = jnp.exp(acs_p)
                hprev = st_ref[pair]
                hb = hprev.astype(BF16)
                dh = dh_scr[pair]
                dhb = dh.astype(BF16)
                dx = jnp.zeros((CHUNK, LANES), F32)
                for k, keep in ((0, low), (1, ~low)):
                    h = 2 * pair + k
                    seg = jnp.where(causal, jnp.exp(acs[:, h:h + 1] - acs_t[h:h + 1, :]), 0.0)
                    lmat = cb * seg
                    dl = lax.dot_general(jnp.where(keep, dyv, 0.0).astype(BF16), xb, _NT,
                                         preferred_element_type=F32)
                    dcb = dcb + dl * seg
                    t = dl * lmat
                    dacs_cols[h] = jnp.sum(t, axis=1, keepdims=True)
                    dacs_rows[h] = jnp.sum(t, axis=0, keepdims=True)
                    dx = dx + jnp.where(keep, lax.dot_general(lmat.astype(BF16), dyb, _TN,
                                                              preferred_element_type=F32), 0.0)
                yoff = lax.dot_general(cg, hb, _NT, preferred_element_type=F32) * ea
                dm = (dyv * ea).astype(BF16)
                dcg = dcg + jnp.dot(dm, hb, preferred_element_type=F32)
                dxds = lax.dot_general(bg, dhb, _NT, preferred_element_type=F32)
                xds = x * ds
                dbg = dbg + jnp.dot(xds.astype(BF16), dhb, preferred_element_type=F32)
                dx = dx + dxds * ds
                t_ds = dxds * xds
                e_a = jnp.exp(acs_t[2 * pair:2 * pair + 1, CHUNK - 1:CHUNK])
                e_b = jnp.exp(acs_t[2 * pair + 1:2 * pair + 2, CHUNK - 1:CHUNK])
                cd = jnp.where(prow < SSD_HEAD_DIM, e_a, e_b)
                hd = dh * hprev
                dcd_a = jnp.sum(jnp.where(prow < SSD_HEAD_DIM, hd, 0.0), keepdims=True)
                dcd_b = jnp.sum(hd, keepdims=True) - dcd_a
                dh_scr[pair] = dh * cd + lax.dot_general(dm, cg, _TN, preferred_element_type=F32)
                col_lo, col_hi = _half_sums(dyv * yoff - t_ds)
                tot_lo, tot_hi = _half_sums(jnp.sum(t_ds, axis=0, keepdims=True))
                dacs_cols[2 * pair] += col_lo + jnp.where(is_last, tot_lo + dcd_a.reshape(1, 1) * e_a, 0.0)
                dacs_cols[2 * pair + 1] += col_hi + jnp.where(is_last, tot_hi + dcd_b.reshape(1, 1) * e_b, 0.0)
                dact_ref[:, cols] = dyv * dskip_ref[:, cols] + dx * dtp
                ddt_cols[2 * pair], ddt_cols[2 * pair + 1] = _half_sums(dx * xs)
            dcbb = dcb.astype(BF16)
            dact_ref[:, b_cols] = dbg + lax.dot_general(dcbb, cg, _TN, preferred_element_type=F32)
            dact_ref[:, c_cols] = dcg + jnp.dot(dcbb, bg, preferred_element_type=F32)
        ddt_ref[...] = _put_cols(ddt_cols)
        sub = lax.broadcasted_iota(jnp.int32, (LANES, CHUNK), 0)
        rows_mat = jnp.zeros((LANES, CHUNK), F32)
        for h in range(SSD_HEADS):
            rows_mat = rows_mat + jnp.where(sub == h, dacs_rows[h], 0.0)
        dacs = _put_cols(dacs_cols) - rows_mat.T
        da_ref[...] = _tri_apply(incl, dacs, (((0,), (0,)), ((), ())))

    rev = lambda c: (nc - 1 - c, 0)
    row = lambda w: pl.BlockSpec((CHUNK, w), rev)
    lane_vec = pl.BlockSpec((1, SSD_INNER), lambda c: (0, 0))
    return pl.pallas_call(
        body, name="ssd_core_bwd", grid=(nc,),
        in_specs=[row(SSD_XBC), row(LANES), row(LANES), lane_vec,
                  pl.BlockSpec((None, SSD_PAIRS, LANES, SSD_STATE), lambda c: (nc - 1 - c, 0, 0, 0)),
                  row(SSD_INNER)],
        out_specs=[row(SSD_XBC), row(LANES), row(LANES), lane_vec],
        out_shape=[jax.ShapeDtypeStruct((length, SSD_XBC), F32), jax.ShapeDtypeStruct((length, LANES), F32),
                   jax.ShapeDtypeStruct((length, LANES), F32), jax.ShapeDtypeStruct((1, SSD_INNER), F32)],
        scratch_shapes=[pltpu.VMEM((SSD_PAIRS, LANES, SSD_STATE), F32)],
        compiler_params=_params(dimension_semantics=("arbitrary",)),
    )(act, dt, a, d_skip, states, dy)


def ssd_post_fwd(y, p, norm_w):
    length = y.shape[0]
    tm = _tile(length, (384, 256, 128))

    def body(y_ref, p_ref, w_ref, o_ref):
        for g in range(SSD_GROUPS):
            cols = slice(g * SSD_GW, (g + 1) * SSD_GW)
            z = p_ref[:, cols]
            v = y_ref[:, cols] * (z * _sigmoid(z))
            o_ref[:, cols] = (v * lax.rsqrt(jnp.mean(v * v, axis=-1, keepdims=True) + EPS)
                              * w_ref[:, cols]).astype(o_ref.dtype)

    return pl.pallas_call(
        body, name="ssd_post_fwd", grid=(length // tm,),
        in_specs=[pl.BlockSpec((tm, SSD_INNER), lambda i: (i, 0)), pl.BlockSpec((tm, SSD_INNER), lambda i: (i, 0)),
                  pl.BlockSpec((1, SSD_INNER), lambda i: (0, 0))],
        out_specs=pl.BlockSpec((tm, SSD_INNER), lambda i: (i, 0)),
        out_shape=jax.ShapeDtypeStruct((length, SSD_INNER), BF16),
        compiler_params=_params(dimension_semantics=("arbitrary",)),
    )(y, p, norm_w)


def ssd_post_bwd(y, p, norm_w, dout):
    length = y.shape[0]
    tm = _tile(length, (384, 256, 128))

    def body(y_ref, p_ref, w_ref, do_ref, dy_ref, dz_ref, dw_ref):
        @pl.when(pl.program_id(0) == 0)
        def _():
            dw_ref[...] = jnp.zeros_like(dw_ref)

        for g in range(SSD_GROUPS):
            cols = slice(g * SSD_GW, (g + 1) * SSD_GW)
            z = p_ref[:, cols]
            yv = y_ref[:, cols]
            sg = _sigmoid(z)
            v = yv * (z * sg)
            rs = lax.rsqrt(jnp.mean(v * v, axis=-1, keepdims=True) + EPS)
            vh = v * rs
            do = do_ref[:, cols]
            dw_ref[:, cols] += jnp.sum(do * vh, axis=0, keepdims=True)
            dvh = do * w_ref[:, cols]
            dv = rs * (dvh - vh * jnp.mean(dvh * vh, axis=-1, keepdims=True))
            dy_ref[:, cols] = dv * (z * sg)
            dz_ref[:, cols] = dv * yv * (sg * (1.0 + z * (1.0 - sg)))

    blk = pl.BlockSpec((tm, SSD_INNER), lambda i: (i, 0))
    vec = pl.BlockSpec((1, SSD_INNER), lambda i: (0, 0))
    return pl.pallas_call(
        body, name="ssd_post_bwd", grid=(length // tm,),
        in_specs=[blk, blk, vec, blk], out_specs=[blk, blk, vec],
        out_shape=[jax.ShapeDtypeStruct((length, SSD_INNER), F32), jax.ShapeDtypeStruct((length, SSD_INNER), F32),
                   jax.ShapeDtypeStruct((1, SSD_INNER), F32)],
        compiler_params=_params(dimension_semantics=("arbitrary",)),
    )(y, p, norm_w, dout)


def _ssd_rows(lw):
    pad = lambda v: jnp.pad(v, (0, LANES - SSD_HEADS))[None]
    return dict(conv_w=lw['ssd_conv_w'], conv_b=lw['ssd_conv_b'][None], dt_bias=pad(lw['ssd_dt_bias']),
                a_log=pad(lw['ssd_a_log']), d_skip=jnp.repeat(lw['ssd_d'], SSD_HEAD_DIM)[None],
                norm_w=lw['ssd_norm'][None])


def ssd_fwd(p, dt_raw, rows):
    act, dt, a = ssd_pre_fwd(p, dt_raw, rows['conv_w'], rows['conv_b'], rows['dt_bias'], rows['a_log'])
    y, states = ssd_core_fwd(act, dt, a, rows['d_skip'])
    return ssd_post_fwd(y, p, rows['norm_w']), (act, dt, a, y, states)


def ssd_bwd(p, dt_raw, rows, saved, dout):
    act, dt, a, y, states = saved
    dy, dz, dnorm = ssd_post_bwd(y, p, rows['norm_w'], dout)
    dact, ddt, da, dskip_lanes = ssd_core_bwd(act, dt, a, rows['d_skip'], states, dy)
    dp, draw, dconv_w, dconv_b, dbias, dalog = ssd_pre_bwd(
        p, dt_raw, rows['conv_w'], rows['conv_b'], rows['dt_bias'], rows['a_log'], dact, ddt, da, dz)
    grads = dict(ssd_conv_w=dconv_w, ssd_conv_b=dconv_b[0], ssd_dt_bias=dbias[0, :SSD_HEADS],
                 ssd_a_log=dalog[0, :SSD_HEADS], ssd_norm=dnorm[0],
                 ssd_d=jnp.sum(dskip_lanes.reshape(SSD_HEADS, SSD_HEAD_DIM), axis=1))
    return dp, draw, grads


IN_A = (0, 3 * D_MODEL)
IN_S = (IN_A[1], IN_A[1] + SSD_INNER + SSD_CONV_DIM)
IN_DT = (IN_S[1], IN_S[1] + SSD_HEADS)
IN_R = (IN_DT[1], IN_DT[1] + 4 * D_MODEL)
IN_SB = (IN_R[1], IN_R[1] + 3 * D_MODEL)
IN_G = (IN_SB[1], IN_SB[1] + N_BRANCH * D_MODEL)
IN_WIDTH = IN_G[1]


def _layer_weights(full, small, l):
    w_in = full['w_in'][l]
    cut = lambda r: w_in[:, r[0]:r[1]]
    w_dt = jnp.pad(cut(IN_DT), ((0, 0), (0, DT_PAD - SSD_HEADS)))
    return dict(
        w_a=cut(IN_A), w_s=cut(IN_S), w_dt=w_dt, w_r=cut(IN_R), w_sb=cut(IN_SB), w_g=cut(IN_G),
        w_branch=[full['w_branch'][l, n] for n in range(N_BRANCH)],
        w_out=full['w_out'][l], w_ffn_in=full['w_ffn_in'][l], w_ffn_out=full['w_ffn_out'][l],
        conv_a=full['conv_a'][l], ssd_conv_w=full['ssd_conv_w'][l],
        ssd_conv_b=small['ssd_conv_b'][l], ssd_dt_bias=small['ssd_dt_bias'][l], ssd_a_log=small['ssd_a_log'][l],
        ssd_d=small['ssd_d'][l], ssd_norm=small['ssd_norm'][l],
        n_mix_pre=small['norm_mix_pre'][l][None], n_mix_post=small['norm_mix_post'][l][None],
        n_ffn_pre=small['norm_ffn_pre'][l][None], n_ffn_post=small['norm_ffn_post'][l][None],
    )


def _layer_fwd(h_res, lw, ret_tables):
    s = {'h_res': h_res, 'ret_tables': ret_tables}
    hn = rms_fwd(h_res, lw['n_mix_pre'], out_dtype=BF16, name="rms_mix_pre")
    s['hn'] = hn
    p_a = mm_nn(hn, lw['w_a'], name="proj_conv")
    p_s = mm_nn(hn, lw['w_s'], name="proj_ssd")
    p_dt = mm_nn(hn, lw['w_dt'], name="proj_dt")
    p_r = mm_nn(hn, lw['w_r'], name="proj_ret")
    p_sb = mm_nn(hn, lw['w_sb'], out_dtype=BF16, name="proj_sb")
    p_g = mm_nn(hn, lw['w_g'], name="proj_gate")
    y_a = conv_mixer_fwd(p_a, lw['conv_a'])
    s['p_a'] = p_a
    s['ssd_rows'] = _ssd_rows(lw)
    y_b, s['ssd_saved'] = ssd_fwd(p_s, p_dt, s['ssd_rows'])
    s['p_s'], s['p_dt'] = p_s, p_dt
    y_c, s['ret_ypre'], s['ret_states'] = ret_fwd(p_r, ret_tables)
    s['p_r'] = p_r
    y_d, s['sb_total'] = sb_fwd(p_sb)
    s['p_sb'] = p_sb
    ys = [y_a, y_b, y_c, y_d]
    s['ys'] = ys
    ups = [mm_nn(ys[n], lw['w_branch'][n], name="branch_up") for n in range(N_BRANCH)]
    merged = merge_fwd(p_g, ups)
    s['p_g'], s['ups'] = p_g, ups
    s['merged'] = merged
    mix = mm_nn(merged, lw['w_out'], name="mix_out")
    s['mix'] = mix
    h2 = rms_fwd(mix, lw['n_mix_post'], res=h_res, name="rms_mix_post")
    s['h2'] = h2
    hf = rms_fwd(h2, lw['n_ffn_pre'], out_dtype=BF16, name="rms_ffn_pre")
    s['hf'] = hf
    f = mm_nn(hf, lw['w_ffn_in'], name="ffn_in")
    act = swiglu_fwd(f)
    s['f'], s['act'] = f, act
    fo = mm_nn(act, lw['w_ffn_out'], name="ffn_out")
    s['fo'] = fo
    return rms_fwd(fo, lw['n_ffn_post'], res=h2, name="rms_ffn_post"), s


def _layer_bwd(dh3, lw, s):
    g = {}
    d_fo, g['norm_ffn_post'] = rms_bwd(s['fo'], lw['n_ffn_post'], dh3, dx_dtype=BF16, name="rms_ffn_post_bwd")
    d_act = mm_nt(d_fo, lw['w_ffn_out'], name="ffn_out_dx")
    g['w_ffn_out'] = mm_tn(s['act'], d_fo, name="ffn_out_dw")
    df = swiglu_bwd(s['f'], d_act)
    d_hf = mm_nt(df, lw['w_ffn_in'], name="ffn_in_dx")
    g['w_ffn_in'] = mm_tn(s['hf'], df, name="ffn_in_dw")
    dh2, g['norm_ffn_pre'] = rms_bwd(s['h2'], lw['n_ffn_pre'], d_hf, add=dh3, name="rms_ffn_pre_bwd")
    d_mix, g['norm_mix_post'] = rms_bwd(s['mix'], lw['n_mix_post'], dh2, dx_dtype=BF16, name="rms_mix_post_bwd")
    d_merged = mm_nt(d_mix, lw['w_out'], name="mix_out_dx")
    g['w_out'] = mm_tn(s['merged'], d_mix, name="mix_out_dw")
    dp_g, dups = merge_bwd(s['p_g'], s['ups'], d_merged)
    dys = [mm_nt(dups[n], lw['w_branch'][n], name="branch_dx") for n in range(N_BRANCH)]
    g['w_branch'] = jnp.stack([mm_tn(s['ys'][n], dups[n], name="branch_dw") for n in range(N_BRANCH)])
    dp_a, g['conv_a'] = conv_mixer_bwd(s['p_a'], lw['conv_a'], dys[0])
    dp_s, dp_dt, ssd_grads = ssd_bwd(s['p_s'], s['p_dt'], s['ssd_rows'], s['ssd_saved'], dys[1])
    g.update(ssd_grads)
    dp_r = ret_bwd(s['p_r'], s['ret_tables'], s['ret_ypre'], s['ret_states'], dys[2])
    dq, dk, dv = sb_bwd(s['p_sb'], s['sb_total'], dys[3])
    dp_sb = jnp.concatenate([dq, dk.astype(BF16), dv.astype(BF16)], axis=1)
    hn = s['hn']
    d_hn = None
    dws = []
    for dp, w, nm in ((dp_a, lw['w_a'], "conv"), (dp_s, lw['w_s'], "ssd"), (dp_dt, lw['w_dt'], "dt"),
                      (dp_r, lw['w_r'], "ret"), (dp_sb, lw['w_sb'], "sb"), (dp_g, lw['w_g'], "gate")):
        d_hn = mm_nt(dp, w, acc=d_hn, name="proj_dx")
        dws.append(mm_tn(hn, dp, name="proj_dw"))
    dws[2] = dws[2][:, :SSD_HEADS]
    g['w_in'] = jnp.concatenate(dws, axis=1)
    dh_res, g['norm_mix_pre'] = rms_bwd(s['h_res'], lw['n_mix_pre'], d_hn, add=dh2, name="rms_mix_pre_bwd")
    for k in ('norm_ffn_post', 'norm_ffn_pre', 'norm_mix_post', 'norm_mix_pre'):
        g[k] = g[k][0]
    return dh_res, g


def _quarter(a, axis, j):
    n = a.shape[axis] // N_CHIPS
    return lax.slice_in_dim(a, j * n, (j + 1) * n, axis=axis)


def kernel(x, meta, w_in, conv_a, ssd_conv_w, ssd_conv_b, ssd_dt_bias, ssd_a_log, ssd_d, ssd_norm, w_branch, w_out, w_ffn_in, w_ffn_out, norm_mix_pre, norm_mix_post, norm_ffn_pre, norm_ffn_post, loss_target, m_meta, m_w_in, m_conv_a, m_ssd_conv_w, m_ssd_conv_b, m_ssd_dt_bias, m_ssd_a_log, m_ssd_d, m_ssd_norm, m_w_branch, m_w_out, m_w_ffn_in, m_w_ffn_out, m_norm_mix_pre, m_norm_mix_post, m_norm_ffn_pre, m_norm_ffn_post, v_meta, v_w_in, v_conv_a, v_ssd_conv_w, v_ssd_conv_b, v_ssd_dt_bias, v_ssd_a_log, v_ssd_d, v_ssd_norm, v_w_branch, v_w_out, v_w_ffn_in, v_w_ffn_out, v_norm_mix_pre, v_norm_mix_post, v_norm_ffn_pre, v_norm_ffn_post):
    w_loc = dict(meta=meta, w_in=w_in, conv_a=conv_a, ssd_conv_w=ssd_conv_w, ssd_conv_b=ssd_conv_b,
                 ssd_dt_bias=ssd_dt_bias, ssd_a_log=ssd_a_log, ssd_d=ssd_d, ssd_norm=ssd_norm, w_branch=w_branch,
                 w_out=w_out, w_ffn_in=w_ffn_in, w_ffn_out=w_ffn_out, norm_mix_pre=norm_mix_pre,
                 norm_mix_post=norm_mix_post, norm_ffn_pre=norm_ffn_pre, norm_ffn_post=norm_ffn_post)
    m_loc = dict(meta=m_meta, w_in=m_w_in, conv_a=m_conv_a, ssd_conv_w=m_ssd_conv_w, ssd_conv_b=m_ssd_conv_b,
                 ssd_dt_bias=m_ssd_dt_bias, ssd_a_log=m_ssd_a_log, ssd_d=m_ssd_d, ssd_norm=m_ssd_norm,
                 w_branch=m_w_branch, w_out=m_w_out, w_ffn_in=m_w_ffn_in, w_ffn_out=m_w_ffn_out,
                 norm_mix_pre=m_norm_mix_pre, norm_mix_post=m_norm_mix_post, norm_ffn_pre=m_norm_ffn_pre,
                 norm_ffn_post=m_norm_ffn_post)
    v_loc = dict(meta=v_meta, w_in=v_w_in, conv_a=v_conv_a, ssd_conv_w=v_ssd_conv_w, ssd_conv_b=v_ssd_conv_b,
                 ssd_dt_bias=v_ssd_dt_bias, ssd_a_log=v_ssd_a_log, ssd_d=v_ssd_d, ssd_norm=v_ssd_norm,
                 w_branch=v_w_branch, w_out=v_w_out, w_ffn_in=v_w_ffn_in, w_ffn_out=v_w_ffn_out,
                 norm_mix_pre=v_norm_mix_pre, norm_mix_post=v_norm_mix_post, norm_ffn_pre=v_norm_ffn_pre,
                 norm_ffn_post=v_norm_ffn_post)

    gathered = gather_shards([w_loc[n].astype(BF16) for n in MATMUL_WEIGHTS]
                             + [_pack([w_loc[n] for n in SMALL_SHARDED], F32, 8)])
    full = {}
    for t, n in enumerate(MATMUL_WEIGHTS):
        full[n] = jnp.concatenate([gathered[t][j] for j in range(N_CHIPS)], axis=SHARD_AXIS[n])
    parts_f = [_unpack(gathered[-1][j], [w_loc[n].shape for n in SMALL_SHARDED]) for j in range(N_CHIPS)]
    for t, n in enumerate(SMALL_SHARDED):
        full[n] = jnp.concatenate([parts_f[j][t] for j in range(N_CHIPS)], axis=SHARD_AXIS[n])

    xs = x[0]
    seq = xs.shape[0]
    length = CHUNK + seq
    h = jnp.concatenate([jnp.zeros((PAD, D_MODEL), F32), full['meta'], xs], axis=0)
    lws, saved = [], []
    ret_tables = _ret_tables(length)
    for l in range(DEPTH):
        lw = _layer_weights(full, w_loc, l)
        h, s = _layer_fwd(h, lw, ret_tables)
        lws.append(lw)
        saved.append(s)

    loss_row, dh = loss_head(h, loss_target[0])
    loss = lax.psum(loss_row[0, 0], ("x", "y", "c"))

    layer_grads = [None] * DEPTH
    for l in reversed(range(DEPTH)):
        dh, layer_grads[l] = _layer_bwd(dh, lws[l], saved[l])
    grad_x = dh[CHUNK:][None]
    grads = {n: jnp.stack([layer_grads[l][n] for l in range(DEPTH)]) for n in WEIGHTS if n != 'meta'}
    grads['meta'] = dh[PAD:CHUNK]

    def rows2d(a):
        return a.reshape(-1, a.shape[-1])

    def small_pieces(j):
        return [_quarter(grads[n], SHARD_AXIS[n], j) if n in SHARD_AXIS else grads[n] for n in SMALL_ORDER]

    quarters = [jnp.stack([rows2d(_quarter(grads[n], SHARD_AXIS[n], j)) for j in range(N_CHIPS)])
                for n in MATMUL_WEIGHTS]
    quarters.append(jnp.stack([_pack(small_pieces(j), F32, 16) for j in range(N_CHIPS)]))
    reduced = reduce_gradients(quarters)
    results = {}
    for t, n in enumerate(MATMUL_WEIGHTS):
        shape = w_loc[n].shape
        new = adamw(reduced[t], rows2d(w_loc[n]), rows2d(m_loc[n]), rows2d(v_loc[n]))
        results[n] = [a.reshape(shape) for a in (reduced[t], *new)]
    small_new = adamw(reduced[-1], *[_pack([d[n] for n in SMALL_ORDER], F32, 16) for d in (w_loc, m_loc, v_loc)])
    small_shapes = [w_loc[n].shape for n in SMALL_ORDER]
    for kind, buf in enumerate((reduced[-1], *small_new)):
        for n, piece in zip(SMALL_ORDER, _unpack(buf, small_shapes)):
            results.setdefault(n, [None] * 4)[kind] = piece
    outs = [results[n][kind] for kind in range(4) for n in WEIGHTS]
    return (loss, grad_x, *outs)
```

```python
import functools
import math

import numpy as np
import jax
import jax.numpy as jnp
from jax import lax
from jax.experimental import pallas as pl
from jax.experimental.pallas import tpu as pltpu

F32 = jnp.float32
BF16 = jnp.bfloat16

D_MODEL = 1024
DEPTH = 2
N_META = 16
CHUNK = 128
PAD = CHUNK - N_META
EPS = 1e-6

CONV_A_K = 3
SSD_HEAD_DIM = 64
SSD_HEADS = 16
SSD_INNER = 1024
SSD_GROUPS = 4
SSD_STATE = 128
SSD_CONV_K = 4
SSD_CONV_DIM = SSD_INNER + 2 * SSD_GROUPS * SSD_STATE
RET_HEADS = 4
RET_QK_DIM = 256
RET_V_DIM = 256
RET_WIDTH = 1024
ROPE_BASE = 10000.0
SB_HEADS = 8
SB_HEAD_DIM = 128
N_BRANCH = 4
D_FF = 2816
DT_PAD = 128

ADAM_LR = 0.001
ADAM_B1 = 0.9
ADAM_B2 = 0.999
ADAM_EPS = 1e-08
ADAM_WD = 0.01
ADAM_STEP = 10

N_CHIPS = 4
N_DEV = 8
LANES = 128
VMEM_LIMIT = 56 * 1024 * 1024
MESH = pl.DeviceIdType.MESH

WEIGHTS = ['meta', 'w_in', 'conv_a', 'ssd_conv_w', 'ssd_conv_b', 'ssd_dt_bias', 'ssd_a_log', 'ssd_d',
           'ssd_norm', 'w_branch', 'w_out', 'w_ffn_in', 'w_ffn_out', 'norm_mix_pre', 'norm_mix_post',
           'norm_ffn_pre', 'norm_ffn_post']
SHARD_AXIS = {'meta': 1, 'w_in': 2, 'conv_a': 2, 'ssd_conv_w': 2, 'w_branch': 2, 'w_out': 1,
              'w_ffn_in': 2, 'w_ffn_out': 1}
MATMUL_WEIGHTS = ['w_in', 'w_branch', 'w_out', 'w_ffn_in', 'w_ffn_out']
SMALL_SHARDED = ['meta', 'conv_a', 'ssd_conv_w']
SMALL_ORDER = SMALL_SHARDED + [n for n in WEIGHTS if n not in SHARD_AXIS]


def _params(**kw):
    return pltpu.CompilerParams(vmem_limit_bytes=VMEM_LIMIT, **kw)


def _tile(n, prefs):
    for p in prefs:
        if n % p == 0:
            return p
    return n


MM_VMEM_BUDGET = 40 * 1024 * 1024
MM_ROW_TILES = (2752, 1376, 688, 384, 256, 128)
MM_COL_TILES = (1024, 512, 256, 128)


def _mm_tiles(m, n, cost):
    for tm in MM_ROW_TILES:
        if m % tm:
            continue
        for tn in MM_COL_TILES:
            if n % tn == 0 and cost(tm, tn) <= MM_VMEM_BUDGET:
                return tm, tn
    return _tile(m, (128, 8)), _tile(n, (128,))


def _size(x):
    return jnp.dtype(x.dtype).itemsize


def mm_nn(a, b, out_dtype=F32, name="mm_nn"):
    m, k = a.shape
    n = b.shape[1]
    ob = jnp.dtype(out_dtype).itemsize
    tm, tn = _mm_tiles(m, n, lambda tm, tn: (2 * tm * k * _size(a) + tm * k * 2 + 2 * k * tn * _size(b)
                                              + 2 * tm * tn * ob + tm * tn * 4))

    def body(a_ref, b_ref, o_ref):
        o_ref[...] = jnp.dot(a_ref[...].astype(BF16), b_ref[...].astype(BF16),
                             preferred_element_type=F32).astype(o_ref.dtype)

    return pl.pallas_call(
        body, name=name, grid=(m // tm, n // tn),
        in_specs=[pl.BlockSpec((tm, k), lambda i, j: (i, 0)), pl.BlockSpec((k, tn), lambda i, j: (0, j))],
        out_specs=pl.BlockSpec((tm, tn), lambda i, j: (i, j)),
        out_shape=jax.ShapeDtypeStruct((m, n), out_dtype),
        compiler_params=_params(dimension_semantics=("arbitrary", "arbitrary")),
    )(a, b)


def mm_nt(g, w, acc=None, name="mm_nt"):
    m, n = g.shape
    k = w.shape[0]
    has_acc = acc is not None
    tm, tn = _mm_tiles(m, n, lambda tm, tn: ((2 + 2 * has_acc) * tm * k * 4 + tm * k * 4 + 2 * tm * tn * _size(g)
                                              + tm * tn * 2 + 2 * k * tn * _size(w)))

    def body(*refs):
        if has_acc:
            g_ref, w_ref, acc_ref, o_ref = refs
        else:
            g_ref, w_ref, o_ref = refs
        j = pl.program_id(1)

        @pl.when(j == 0)
        def _():
            o_ref[...] = acc_ref[...] if has_acc else jnp.zeros_like(o_ref)

        o_ref[...] += lax.dot_general(g_ref[...].astype(BF16), w_ref[...].astype(BF16),
                                      (((1,), (1,)), ((), ())), preferred_element_type=F32)

    in_specs = [pl.BlockSpec((tm, tn), lambda i, j: (i, j)), pl.BlockSpec((k, tn), lambda i, j: (0, j))]
    args = [g, w]
    if has_acc:
        in_specs.append(pl.BlockSpec((tm, k), lambda i, j: (i, 0)))
        args.append(acc)
    return pl.pallas_call(
        body, name=name, grid=(m // tm, n // tn),
        in_specs=in_specs,
        out_specs=pl.BlockSpec((tm, k), lambda i, j: (i, 0)),
        out_shape=jax.ShapeDtypeStruct((m, k), F32),
        compiler_params=_params(dimension_semantics=("arbitrary", "arbitrary")),
    )(*args)


def mm_tn(x, g, name="mm_tn"):
    m, k = x.shape
    n = g.shape[1]
    tk = _tile(k, (1024, 1408, 512, 256, 128))
    tm, tn = _mm_tiles(m, n, lambda tm, tn: (3 * tk * tn * 4 + 2 * tm * tk * _size(x) + tm * tk * 2
                                              + 2 * tm * tn * _size(g) + tm * tn * 2))

    def body(x_ref, g_ref, o_ref):
        s = pl.program_id(2)

        @pl.when(s == 0)
        def _():
            o_ref[...] = jnp.zeros_like(o_ref)

        o_ref[...] += lax.dot_general(x_ref[...].astype(BF16), g_ref[...].astype(BF16),
                                      (((0,), (0,)), ((), ())), preferred_element_type=F32)

    return pl.pallas_call(
        body, name=name, grid=(k // tk, n // tn, m // tm),
        in_specs=[pl.BlockSpec((tm, tk), lambda a, b, s: (s, a)), pl.BlockSpec((tm, tn), lambda a, b, s: (s, b))],
        out_specs=pl.BlockSpec((tk, tn), lambda a, b, s: (a, b)),
        out_shape=jax.ShapeDtypeStruct((k, n), F32),
        compiler_params=_params(dimension_semantics=("arbitrary", "arbitrary", "arbitrary")),
    )(x, g)


def rms_fwd(x, w, res=None, out_dtype=F32, name="rms_fwd"):
    m, d = x.shape
    tm = _tile(m, (384, 256, 128))
    has_res = res is not None

    def body(*refs):
        if has_res:
            x_ref, w_ref, r_ref, o_ref = refs
        else:
            x_ref, w_ref, o_ref = refs
        xv = x_ref[...]
        y = xv * lax.rsqrt(jnp.mean(xv * xv, axis=-1, keepdims=True) + EPS) * w_ref[...]
        o_ref[...] = (y + r_ref[...] if has_res else y).astype(o_ref.dtype)

    row = pl.BlockSpec((tm, d), lambda i: (i, 0))
    in_specs = [row, pl.BlockSpec((1, d), lambda i: (0, 0))]
    args = [x, w]
    if has_res:
        in_specs.append(row)
        args.append(res)
    return pl.pallas_call(
        body, name=name, grid=(m // tm,), in_specs=in_specs, out_specs=row,
        out_shape=jax.ShapeDtypeStruct((m, d), out_dtype),
        compiler_params=_params(dimension_semantics=("arbitrary",)),
    )(*args)


def rms_bwd(x, w, dy, add=None, dx_dtype=F32, name="rms_bwd"):
    m, d = x.shape
    tm = _tile(m, (384, 256, 128))
    has_add = add is not None

    def body(*refs):
        if has_add:
            x_ref, w_ref, dy_ref, add_ref, dx_ref, dw_ref = refs
        else:
            x_ref, w_ref, dy_ref, dx_ref, dw_ref = refs
        i = pl.program_id(0)
        xv = x_ref[...]
        dyv = dy_ref[...]
        r = lax.rsqrt(jnp.mean(xv * xv, axis=-1, keepdims=True) + EPS)
        xh = xv * r
        dxh = dyv * w_ref[...]
        dx = r * (dxh - xh * jnp.mean(dxh * xh, axis=-1, keepdims=True))
        dx_ref[...] = (dx + add_ref[...] if has_add else dx).astype(dx_ref.dtype)

        @pl.when(i == 0)
        def _():
            dw_ref[...] = jnp.zeros_like(dw_ref)

        dw_ref[...] += jnp.sum(dyv * xh, axis=0, keepdims=True)

    row = pl.BlockSpec((tm, d), lambda i: (i, 0))
    vec = pl.BlockSpec((1, d), lambda i: (0, 0))
    in_specs = [row, vec, row]
    args = [x, w, dy]
    if has_add:
        in_specs.append(row)
        args.append(add)
    return pl.pallas_call(
        body, name=name, grid=(m // tm,), in_specs=in_specs, out_specs=[row, vec],
        out_shape=[jax.ShapeDtypeStruct((m, d), dx_dtype), jax.ShapeDtypeStruct((1, d), F32)],
        compiler_params=_params(dimension_semantics=("arbitrary",)),
    )(*args)


def loss_head(h, target):
    l, d = h.shape
    nblk = l // CHUNK

    def body(h_ref, t_ref, loss_ref, dh_ref, acc_ref):
        i = pl.program_id(0)

        @pl.when(i == 0)
        def _():
            acc_ref[...] = jnp.zeros_like(acc_ref)
            dh_ref[...] = jnp.zeros_like(dh_ref)

        @pl.when(i > 0)
        def _():
            e = h_ref[...] - t_ref[...]
            dh_ref[...] = e / d
            acc_ref[...] += jnp.sum(e * e, axis=0, keepdims=True)

        @pl.when(i == nblk - 1)
        def _():
            loss_ref[...] = jnp.zeros_like(loss_ref) + 0.5 * jnp.sum(acc_ref[...]) / d

    return pl.pallas_call(
        body, name="loss_head", grid=(nblk,),
        in_specs=[pl.BlockSpec((CHUNK, d), lambda i: (i, 0)),
                  pl.BlockSpec((CHUNK, d), lambda i: (jnp.maximum(i - 1, 0), 0))],
        out_specs=[pl.BlockSpec((1, LANES), lambda i: (0, 0)), pl.BlockSpec((CHUNK, d), lambda i: (i, 0))],
        out_shape=[jax.ShapeDtypeStruct((1, LANES), F32), jax.ShapeDtypeStruct((l, d), F32)],
        scratch_shapes=[pltpu.VMEM((1, d), F32)],
        compiler_params=_params(dimension_semantics=("arbitrary",)),
    )(h, target)


SB_BLK = 128


def _sb_tile(l):
    return _tile(l, (384, 256, 128))


def _sb_tri(strict_later):
    r = lax.broadcasted_iota(jnp.int32, (SB_BLK, 2 * SB_BLK), 0)
    c = lax.broadcasted_iota(jnp.int32, (SB_BLK, 2 * SB_BLK), 1)
    keep = (r > c) if strict_later else (r < c)
    return jnp.where(keep | (c >= SB_BLK), 1.0, 0.0).astype(BF16)


def _sb_mask(i, j, t):
    qpos = i * t + lax.broadcasted_iota(jnp.int32, (t, t), 0)
    kpos = j * t + lax.broadcasted_iota(jnp.int32, (t, t), 1)
    return (kpos < qpos) & (kpos >= PAD)


def _sb_scores(q, k, scale, mask):
    z = lax.dot_general(q, k, (((1,), (1,)), ((), ())), preferred_element_type=F32) * scale
    sp = jnp.maximum(z, 0.0) + jnp.log(1.0 + jnp.exp(-jnp.abs(z)))
    lneg = -sp if mask is None else jnp.where(mask, -sp, 0.0)
    return z - sp, lneg


def _sb_block_sums(x, tri):
    s = jnp.dot(x.astype(BF16), tri, preferred_element_type=F32)
    return s[:, :SB_BLK], s[:, SB_BLK:]


SB_DEAD = -110.0


def _sb_walk_down(i, step, carry):
    carry = step(i, carry, True)

    def alive(c):
        return (jnp.max(c[0]) > SB_DEAD).astype(jnp.int32)

    def body(state):
        j, _, c = state
        c = step(j, c, False)
        return j - 1, alive(c), c

    j, go, carry = lax.while_loop(lambda s: (s[0] >= 1) & (s[1] > 0), body, (i - 1, alive(carry), carry))
    reach0 = ((j == 0) & (go > 0) & (i > 0)).astype(jnp.int32)
    carry = lax.fori_loop(0, reach0, lambda t, c: step(0, c, True), carry)
    return carry, jnp.where(reach0 > 0, 0, j + 1)


def _sb_walk_up(i, first, step, carry):
    start0 = ((first == 0) & (i > 0)).astype(jnp.int32)
    carry = lax.fori_loop(0, start0, lambda t, c: step(0, c, True), carry)
    carry = lax.fori_loop(jnp.maximum(first, 1), i, lambda j, c: step(j, c, False), carry)
    return step(i, carry, True)


def sb_fwd(qkv):
    l = qkv.shape[0]
    t = _sb_tile(l)
    nb = t // SB_BLK
    scale = SB_HEAD_DIM ** -0.5

    def body(q_ref, k_ref, v_ref, o_ref, walk_ref):
        i = pl.program_id(1)
        q = q_ref[...]
        tri = _sb_tri(True)

        def step(j, carry, masked):
            later, acc = carry
            rows = pl.ds(pl.multiple_of(j * t, t), t)
            mask = _sb_mask(i, j, t) if masked else None
            lpos, lneg = _sb_scores(q, k_ref[rows, :], scale, mask)
            ws = [None] * nb
            for b in reversed(range(nb)):
                cols = slice(b * SB_BLK, (b + 1) * SB_BLK)
                within, total = _sb_block_sums(lneg[:, cols], tri)
                ws[b] = jnp.exp(lpos[:, cols] + within + later)
                later = later + total
            w = jnp.concatenate(ws, axis=1)
            if masked:
                w = jnp.where(mask, w, 0.0)
            acc = acc + jnp.dot(w.astype(BF16), v_ref[rows, :], preferred_element_type=F32)
            return later, acc

        carry = (jnp.zeros((t, SB_BLK), F32), jnp.zeros((t, SB_HEAD_DIM), F32))
        (later, acc), first = _sb_walk_down(i, step, carry)
        o_ref[...] = acc.astype(o_ref.dtype)
        walk_ref[0] = later[:, :1]
        walk_ref[1] = jnp.zeros((t, 1), F32) + first.astype(F32)

    return pl.pallas_call(
        body, name="sb_fwd", grid=(SB_HEADS, l // t),
        in_specs=[pl.BlockSpec((t, SB_HEAD_DIM), lambda h, i: (i, h)),
                  pl.BlockSpec((l, SB_HEAD_DIM), lambda h, i: (0, SB_HEADS + h)),
                  pl.BlockSpec((l, SB_HEAD_DIM), lambda h, i: (0, 2 * SB_HEADS + h))],
        out_specs=[pl.BlockSpec((t, SB_HEAD_DIM), lambda h, i: (i, h)),
                   pl.BlockSpec((2, None, t, 1), lambda h, i: (0, h, i, 0))],
        out_shape=[jax.ShapeDtypeStruct((l, D_MODEL), BF16), jax.ShapeDtypeStruct((2, SB_HEADS, l, 1), F32)],
        compiler_params=_params(dimension_semantics=("arbitrary", "arbitrary")),
    )(qkv, qkv, qkv)


def sb_bwd(qkv, walk, dout):
    l = qkv.shape[0]
    t = _sb_tile(l)
    nb = t // SB_BLK
    nq = l // t
    scale = SB_HEAD_DIM ** -0.5

    def body(q_ref, k_ref, v_ref, walk_ref, do_ref, dq_ref, dk_hbm, dv_hbm, dk_acc, dv_acc):
        h = pl.program_id(0)
        i = pl.program_id(1)

        @pl.when(i == 0)
        def _():
            dk_acc[...] = jnp.zeros_like(dk_acc)
            dv_acc[...] = jnp.zeros_like(dv_acc)

        q = q_ref[...]
        dob = do_ref[...].astype(BF16)
        tri_later = _sb_tri(True)
        tri_before = _sb_tri(False)

        def step(j, carry, masked):
            later, g_before, dq = carry
            rows = pl.ds(pl.multiple_of(j * t, t), t)
            k = k_ref[rows, :]
            v = v_ref[rows, :]
            mask = _sb_mask(i, j, t) if masked else None
            lpos, lneg = _sb_scores(q, k, scale, mask)
            dw = lax.dot_general(dob, v, (((1,), (1,)), ((), ())), preferred_element_type=F32)
            ws, dzs = [None] * nb, [None] * nb
            for b in range(nb):
                cols = slice(b * SB_BLK, (b + 1) * SB_BLK)
                within, total = _sb_block_sums(lneg[:, cols], tri_later)
                later = later - total
                wb = jnp.exp(lpos[:, cols] + within + later)
                if masked:
                    wb = jnp.where(mask[:, cols], wb, 0.0)
                g = dw[:, cols] * wb
                g_within, g_total = _sb_block_sums(g, tri_before)
                dz = g - (g + g_before + g_within) * jnp.exp(lpos[:, cols])
                if masked:
                    dz = jnp.where(mask[:, cols], dz, 0.0)
                g_before = g_before + g_total
                ws[b] = wb.astype(BF16)
                dzs[b] = (dz * scale).astype(BF16)
            w = jnp.concatenate(ws, axis=1)
            dzb = jnp.concatenate(dzs, axis=1)
            dq = dq + jnp.dot(dzb, k, preferred_element_type=F32)
            dk_acc[rows, :] += lax.dot_general(dzb, q, (((0,), (0,)), ((), ())), preferred_element_type=F32)
            dv_acc[rows, :] += lax.dot_general(w, dob, (((0,), (0,)), ((), ())), preferred_element_type=F32)
            return later, g_before, dq

        carry = (jnp.broadcast_to(walk_ref[0], (t, SB_BLK)), jnp.zeros((t, SB_BLK), F32),
                 jnp.zeros((t, SB_HEAD_DIM), F32))
        first = jnp.max(walk_ref[1]).astype(jnp.int32)
        _, _, dq = _sb_walk_up(i, first, step, carry)
        dq_ref[...] = dq.astype(dq_ref.dtype)

        @pl.when(i == nq - 1)
        def _():
            cols = pl.ds(pl.multiple_of(h * SB_HEAD_DIM, SB_HEAD_DIM), SB_HEAD_DIM)
            pltpu.sync_copy(dk_acc, dk_hbm.at[:, cols])
            pltpu.sync_copy(dv_acc, dv_hbm.at[:, cols])

    blk = lambda h, i: (i, h)
    return pl.pallas_call(
        body, name="sb_bwd", grid=(SB_HEADS, nq),
        in_specs=[pl.BlockSpec((t, SB_HEAD_DIM), blk),
                  pl.BlockSpec((l, SB_HEAD_DIM), lambda h, i: (0, SB_HEADS + h)),
                  pl.BlockSpec((l, SB_HEAD_DIM), lambda h, i: (0, 2 * SB_HEADS + h)),
                  pl.BlockSpec((2, None, t, 1), lambda h, i: (0, h, i, 0)), pl.BlockSpec((t, SB_HEAD_DIM), blk)],
        out_specs=[pl.BlockSpec((t, SB_HEAD_DIM), blk), pl.BlockSpec(memory_space=pl.ANY),
                   pl.BlockSpec(memory_space=pl.ANY)],
        out_shape=[jax.ShapeDtypeStruct((l, D_MODEL), BF16), jax.ShapeDtypeStruct((l, D_MODEL), F32),
                   jax.ShapeDtypeStruct((l, D_MODEL), F32)],
        scratch_shapes=[pltpu.VMEM((l, SB_HEAD_DIM), F32), pltpu.VMEM((l, SB_HEAD_DIM), F32)],
        compiler_params=_params(dimension_semantics=("arbitrary", "arbitrary")),
    )(qkv, qkv, qkv, walk, dout)


_HBM = pl.BlockSpec(memory_space=pl.ANY)


def _other_chips(x, y):
    return [(1 - x, y), (x, 1 - y), (1 - x, 1 - y)]


def _comm_call(body, name, ins, out_shapes, n_remote, n_local):
    return pl.pallas_call(
        body, name=name, in_specs=[_HBM] * len(ins), out_specs=[_HBM] * len(out_shapes), out_shape=out_shapes,
        scratch_shapes=[pltpu.SemaphoreType.DMA((n_remote,)), pltpu.SemaphoreType.DMA((n_remote,)),
                        pltpu.SemaphoreType.DMA((max(n_local, 1),))],
    )(*ins)


def gather_shards(shards):
    n = len(shards)

    def body(*refs):
        ins, outs = refs[:n], refs[n:2 * n]
        send_sems, recv_sems, local_sems = refs[2 * n:]
        x, y, c = lax.axis_index("x"), lax.axis_index("y"), lax.axis_index("c")
        me = 2 * x + y
        own = [pltpu.make_async_copy(ins[t], outs[t].at[me], local_sems.at[t]) for t in range(n)]
        for cp in own:
            cp.start()

        def copy(t, k, px, py, slot):
            return pltpu.make_async_remote_copy(
                src_ref=ins[t], dst_ref=outs[t].at[slot], send_sem=send_sems.at[3 * t + k],
                recv_sem=recv_sems.at[3 * t + k], device_id=(px, py, c), device_id_type=MESH)

        chips = _other_chips(x, y)
        sends = [copy(t, k, px, py, me) for t in range(n) for k, (px, py) in enumerate(chips)]
        for cp in sends:
            cp.start()
        for t in range(n):
            for k, (px, py) in enumerate(chips):
                copy(t, k, px, py, 2 * px + py).wait_recv()
        for cp in sends:
            cp.wait_send()
        for cp in own:
            cp.wait()

    out_shapes = [jax.ShapeDtypeStruct((N_CHIPS,) + s.shape, s.dtype) for s in shards]
    return _comm_call(body, "gather_shards", shards, out_shapes, 3 * n, n)


def sibling_swap_halves(gs):
    n = len(gs)

    def body(*refs):
        ins, outs = refs[:n], refs[n:2 * n]
        send_sems, recv_sems, _ = refs[2 * n:]
        x, y, c = lax.axis_index("x"), lax.axis_index("y"), lax.axis_index("c")
        copies = []
        for t in range(n):
            rh = ins[t].shape[1] // 2
            src = ins[t].at[:, pl.ds(pl.multiple_of((1 - c) * rh, 8), rh), :]
            copies.append(pltpu.make_async_remote_copy(
                src_ref=src, dst_ref=outs[t], send_sem=send_sems.at[t], recv_sem=recv_sems.at[t],
                device_id=(x, y, 1 - c), device_id_type=MESH))
        for cp in copies:
            cp.start()
        for cp in copies:
            cp.wait_recv()
        for cp in copies:
            cp.wait_send()

    out_shapes = [jax.ShapeDtypeStruct((g.shape[0], g.shape[1] // 2, g.shape[2]), g.dtype) for g in gs]
    return _comm_call(body, "sibling_swap_halves", gs, out_shapes, n, 0)


def chip_exchange(ps):
    n = len(ps)

    def body(*refs):
        ins, outs = refs[:n], refs[n:2 * n]
        send_sems, recv_sems, local_sems = refs[2 * n:]
        x, y, c = lax.axis_index("x"), lax.axis_index("y"), lax.axis_index("c")
        me = 2 * x + y
        own = [pltpu.make_async_copy(ins[t].at[me], outs[t].at[me], local_sems.at[t]) for t in range(n)]
        for cp in own:
            cp.start()

        def copy(t, k, px, py, src_slot, dst_slot):
            return pltpu.make_async_remote_copy(
                src_ref=ins[t].at[src_slot], dst_ref=outs[t].at[dst_slot], send_sem=send_sems.at[3 * t + k],
                recv_sem=recv_sems.at[3 * t + k], device_id=(px, py, c), device_id_type=MESH)

        chips = _other_chips(x, y)
        sends = [copy(t, k, px, py, 2 * px + py, me) for t in range(n) for k, (px, py) in enumerate(chips)]
        for cp in sends:
            cp.start()
        for t in range(n):
            for k, (px, py) in enumerate(chips):
                copy(t, k, px, py, me, 2 * px + py).wait_recv()
        for cp in sends:
            cp.wait_send()
        for cp in own:
            cp.wait()

    out_shapes = [jax.ShapeDtypeStruct(p.shape, p.dtype) for p in ps]
    return _comm_call(body, "chip_exchange", ps, out_shapes, 3 * n, n)


def sibling_share(ss):
    n = len(ss)

    def body(*refs):
        ins, outs = refs[:n], refs[n:2 * n]
        send_sems, recv_sems, _ = refs[2 * n:]
        x, y, c = lax.axis_index("x"), lax.axis_index("y"), lax.axis_index("c")
        copies = [pltpu.make_async_remote_copy(
            src_ref=ins[t], dst_ref=outs[t], send_sem=send_sems.at[t], recv_sem=recv_sems.at[t],
            device_id=(x, y, 1 - c), device_id_type=MESH) for t in range(n)]
        for cp in copies:
            cp.start()
        for cp in copies:
            cp.wait_recv()
        for cp in copies:
            cp.wait_send()

    out_shapes = [jax.ShapeDtypeStruct(s.shape, s.dtype) for s in ss]
    return _comm_call(body, "sibling_share", ss, out_shapes, n, 0)


EW_BLOCK_BYTES = 2 * 1024 * 1024


def _ew_rows(rows, cols, copies=1):
    padded = -(-cols // LANES) * LANES
    for tr in (1024, 512, 256, 128, 64, 32, 16, 8):
        if rows % tr == 0 and copies * tr * padded * 4 <= EW_BLOCK_BYTES:
            return tr
    return rows


def add_pairs(a, b):
    rows, cols = a.shape
    tr = _ew_rows(rows, cols)

    def body(a_ref, b_ref, o_ref):
        o_ref[...] = a_ref[...] + b_ref[...]

    blk = pl.BlockSpec((tr, cols), lambda i: (i, 0))
    return pl.pallas_call(
        body, name="add_pairs", grid=(rows // tr,), in_specs=[blk, blk], out_specs=blk,
        out_shape=jax.ShapeDtypeStruct((rows, cols), F32),
        compiler_params=_params(dimension_semantics=("arbitrary",)),
    )(a, b)


def sum_chips(slots):
    _, rows, cols = slots.shape
    tr = _ew_rows(rows, cols, N_CHIPS)

    def body(s_ref, o_ref):
        acc = s_ref[0]
        for j in range(1, N_CHIPS):
            acc = acc + s_ref[j]
        o_ref[...] = acc

    return pl.pallas_call(
        body, name="sum_chips", grid=(rows // tr,),
        in_specs=[pl.BlockSpec((N_CHIPS, tr, cols), lambda i: (0, i, 0))],
        out_specs=pl.BlockSpec((tr, cols), lambda i: (i, 0)),
        out_shape=jax.ShapeDtypeStruct((rows, cols), F32),
        compiler_params=_params(dimension_semantics=("arbitrary",)),
    )(slots)


def adamw(g, w, m, v):
    rows, cols = g.shape
    tr = _ew_rows(rows, cols)

    def body(g_ref, w_ref, m_ref, v_ref, d_out, m_out, v_out):
        gv = g_ref[...]
        m_new = ADAM_B1 * m_ref[...] + (1.0 - ADAM_B1) * gv
        v_new = ADAM_B2 * v_ref[...] + (1.0 - ADAM_B2) * jnp.square(gv)
        m_hat = m_new / (1.0 - ADAM_B1 ** ADAM_STEP)
        v_hat = v_new / (1.0 - ADAM_B2 ** ADAM_STEP)
        d_out[...] = -ADAM_LR * (m_hat / (jnp.sqrt(v_hat) + ADAM_EPS) + ADAM_WD * w_ref[...])
        m_out[...] = m_new
        v_out[...] = v_new

    blk = pl.BlockSpec((tr, cols), lambda i: (i, 0))
    return pl.pallas_call(
        body, name="adamw", grid=(rows // tr,), in_specs=[blk] * 4, out_specs=[blk] * 3,
        out_shape=[jax.ShapeDtypeStruct((rows, cols), F32)] * 3,
        compiler_params=_params(dimension_semantics=("arbitrary",)),
    )(g, w, m, v)


def reduce_gradients(quarters):
    theirs = sibling_swap_halves(quarters)
    c = lax.axis_index("c")
    chip_partials = []
    for q, t in zip(quarters, theirs):
        four, rh, cols = t.shape
        mine = lax.dynamic_slice_in_dim(q, c * rh, rh, axis=1)
        chip_partials.append(add_pairs(mine.reshape(four * rh, cols), t.reshape(four * rh, cols))
                             .reshape(four, rh, cols))
    slots = chip_exchange(chip_partials)
    mine = [sum_chips(s) for s in slots]
    theirs = sibling_share(mine)
    return [jnp.concatenate([jnp.where(c == 0, m, t), jnp.where(c == 0, t, m)], axis=0)
            for m, t in zip(mine, theirs)]


def _pack(pieces, dtype, row_multiple):
    flat = jnp.concatenate([p.astype(dtype).reshape(-1) for p in pieces])
    per = row_multiple * LANES
    padded = -(-flat.shape[0] // per) * per
    flat = jnp.pad(flat, (0, padded - flat.shape[0]))
    return flat.reshape(-1, LANES)


def _unpack(buf, shapes):
    flat = buf.reshape(-1)
    out, off = [], 0
    for s in shapes:
        n = int(np.prod(s))
        out.append(flat[off:off + n].reshape(s))
        off += n
    return out


RET_SCALE = RET_QK_DIM ** -0.5
RET_LOG_GAMMA = [math.log(1.0 - 2.0 ** (-5.0 - h)) for h in range(RET_HEADS)]
RET_HALF = RET_QK_DIM // 2


def _ret_tables(length):
    inv = ROPE_BASE ** (-jnp.arange(RET_HALF, dtype=F32) / RET_HALF)
    ang = jnp.arange(length).astype(F32)[:, None] * inv[None, :]
    log_gamma = jnp.log(1.0 - jnp.power(2.0, -5.0 - jnp.arange(RET_HEADS, dtype=F32)))
    idx = jnp.arange(CHUNK, dtype=F32)
    rel = idx[:, None] - idx[None, :]
    dmask = jnp.where(rel >= 0, jnp.exp(log_gamma[:, None, None] * jnp.maximum(rel, 0.0)), 0.0)
    k_decay = jnp.exp(log_gamma[:, None] * (CHUNK - 1 - idx)[None, :])[:, :, None]
    q_decay = jnp.exp(log_gamma[:, None] * (idx + 1.0)[None, :])[:, :, None]
    return jnp.cos(ang), jnp.sin(ang), dmask, k_decay, q_decay


def _rot(x, cs, sn):
    x1, x2 = x[:, :RET_HALF], x[:, RET_HALF:]
    return jnp.concatenate([x1 * cs - x2 * sn, x1 * sn + x2 * cs], axis=1)


def _unrot(d, cs, sn):
    d1, d2 = d[:, :RET_HALF], d[:, RET_HALF:]
    return jnp.concatenate([d1 * cs + d2 * sn, d2 * cs - d1 * sn], axis=1)


def _sigmoid(x):
    return 1.0 / (1.0 + jnp.exp(-x))


_NT = (((1,), (1,)), ((), ()))
_TN = (((0,), (0,)), ((), ()))


def _ret_specs(nc, rev):
    ch = (lambda c: nc - 1 - c) if rev else (lambda c: c)
    row = lambda w: pl.BlockSpec((CHUNK, w), lambda c: (ch(c), 0))
    const3 = lambda a, b: pl.BlockSpec((RET_HEADS, a, b), lambda c: (0, 0, 0))
    tables = [row(RET_HALF), row(RET_HALF), const3(CHUNK, CHUNK), const3(CHUNK, 1), const3(CHUNK, 1)]
    state = pl.BlockSpec((None, RET_HEADS, RET_QK_DIM, RET_V_DIM), lambda c: (ch(c), 0, 0, 0))
    return row, tables, state


def ret_fwd(p, tables):
    length = p.shape[0]
    nc = length // CHUNK
    row, table_specs, state_spec = _ret_specs(nc, False)

    def body(p_ref, cos_ref, sin_ref, dm_ref, kd_ref, qd_ref, y_ref, ypre_ref, st_ref, r_scr):
        c = pl.program_id(0)

        @pl.when(c == 0)
        def _():
            r_scr[...] = jnp.zeros_like(r_scr)

        cs, sn = cos_ref[...], sin_ref[...]
        valid = (c * CHUNK + lax.broadcasted_iota(jnp.int32, (CHUNK, 1), 0)) >= PAD
        for h in range(RET_HEADS):
            col = lambda part: slice(part * D_MODEL + h * RET_QK_DIM, part * D_MODEL + (h + 1) * RET_QK_DIM)
            qb = _rot(p_ref[:, col(0)], cs, sn).astype(BF16)
            kr = _rot(p_ref[:, col(1)], cs, sn) * RET_SCALE
            kb = kr.astype(BF16)
            vb = jnp.where(valid, p_ref[:, col(2)], 0.0).astype(BF16)
            s = lax.dot_general(qb, kb, _NT, preferred_element_type=F32) * dm_ref[h]
            r = r_scr[h]
            st_ref[h] = r
            y = (jnp.dot(s.astype(BF16), vb, preferred_element_type=F32)
                 + jnp.dot(qb, r.astype(BF16), preferred_element_type=F32) * qd_ref[h])
            kdb = (kr * kd_ref[h]).astype(BF16)
            r_scr[h] = r * math.exp(RET_LOG_GAMMA[h] * CHUNK) + lax.dot_general(kdb, vb, _TN,
                                                                                preferred_element_type=F32)
            out = slice(h * RET_V_DIM, (h + 1) * RET_V_DIM)
            ypre_ref[:, out] = y
            mu = jnp.mean(y, axis=-1, keepdims=True)
            yc = y - mu
            yn = yc * lax.rsqrt(jnp.mean(yc * yc, axis=-1, keepdims=True) + EPS)
            g = p_ref[:, col(3)]
            y_ref[:, out] = (yn * (g * _sigmoid(g))).astype(y_ref.dtype)

    return pl.pallas_call(
        body, name="ret_fwd", grid=(nc,),
        in_specs=[row(4 * D_MODEL)] + table_specs,
        out_specs=[row(D_MODEL), row(D_MODEL), state_spec],
        out_shape=[jax.ShapeDtypeStruct((length, D_MODEL), BF16), jax.ShapeDtypeStruct((length, D_MODEL), F32),
                   jax.ShapeDtypeStruct((nc, RET_HEADS, RET_QK_DIM, RET_V_DIM), F32)],
        scratch_shapes=[pltpu.VMEM((RET_HEADS, RET_QK_DIM, RET_V_DIM), F32)],
        compiler_params=_params(dimension_semantics=("arbitrary",)),
    )(p, *tables)


def ret_bwd(p, tables, ypre, states, dyo):
    length = p.shape[0]
    nc = length // CHUNK
    row, table_specs, state_spec = _ret_specs(nc, True)

    def body(p_ref, cos_ref, sin_ref, dm_ref, kd_ref, qd_ref, ypre_ref, st_ref, dyo_ref, dp_ref, dr_scr):
        c = pl.program_id(0)

        @pl.when(c == 0)
        def _():
            dr_scr[...] = jnp.zeros_like(dr_scr)

        cs, sn = cos_ref[...], sin_ref[...]
        valid = ((nc - 1 - c) * CHUNK + lax.broadcasted_iota(jnp.int32, (CHUNK, 1), 0)) >= PAD
        for h in range(RET_HEADS):
            col = lambda part: slice(part * D_MODEL + h * RET_QK_DIM, part * D_MODEL + (h + 1) * RET_QK_DIM)
            out = slice(h * RET_V_DIM, (h + 1) * RET_V_DIM)
            qb = _rot(p_ref[:, col(0)], cs, sn).astype(BF16)
            kr = _rot(p_ref[:, col(1)], cs, sn) * RET_SCALE
            kb = kr.astype(BF16)
            vb = jnp.where(valid, p_ref[:, col(2)], 0.0).astype(BF16)
            g = p_ref[:, col(3)]
            y = ypre_ref[:, out]
            dyo_h = dyo_ref[:, out]
            mu = jnp.mean(y, axis=-1, keepdims=True)
            yc = y - mu
            rs = lax.rsqrt(jnp.mean(yc * yc, axis=-1, keepdims=True) + EPS)
            xh = yc * rs
            sg = _sigmoid(g)
            dp_ref[:, col(3)] = (dyo_h * xh * (sg * (1.0 + g * (1.0 - sg)))).astype(dp_ref.dtype)
            dyn = dyo_h * (g * sg)
            dy = rs * (dyn - jnp.mean(dyn, axis=-1, keepdims=True)
                       - xh * jnp.mean(dyn * xh, axis=-1, keepdims=True))
            dyb = dy.astype(BF16)
            dm = dm_ref[h]
            sm = (lax.dot_general(qb, kb, _NT, preferred_element_type=F32) * dm).astype(BF16)
            dsb = (lax.dot_general(dyb, vb, _NT, preferred_element_type=F32) * dm).astype(BF16)
            rb = st_ref[h].astype(BF16)
            dyqb = (dy * qd_ref[h]).astype(BF16)
            dr = dr_scr[h]
            drb = dr.astype(BF16)
            kd = kd_ref[h]
            dq = (jnp.dot(dsb, kb, preferred_element_type=F32)
                  + lax.dot_general(dyqb, rb, _NT, preferred_element_type=F32))
            dk = (lax.dot_general(dsb, qb, _TN, preferred_element_type=F32)
                  + lax.dot_general(vb, drb, _NT, preferred_element_type=F32) * kd)
            dv = (lax.dot_general(sm, dyb, _TN, preferred_element_type=F32)
                  + jnp.dot((kr * kd).astype(BF16), drb, preferred_element_type=F32))
            dr_scr[h] = dr * math.exp(RET_LOG_GAMMA[h] * CHUNK) + lax.dot_general(qb, dyqb, _TN,
                                                                                 preferred_element_type=F32)
            dp_ref[:, col(0)] = _unrot(dq, cs, sn).astype(dp_ref.dtype)
            dp_ref[:, col(1)] = (_unrot(dk, cs, sn) * RET_SCALE).astype(dp_ref.dtype)
            dp_ref[:, col(2)] = jnp.where(valid, dv, 0.0).astype(dp_ref.dtype)

    return pl.pallas_call(
        body, name="ret_bwd", grid=(nc,),
        in_specs=[row(4 * D_MODEL)] + table_specs + [row(D_MODEL), state_spec, row(D_MODEL)],
        out_specs=row(4 * D_MODEL),
        out_shape=jax.ShapeDtypeStruct((length, 4 * D_MODEL), BF16),
        scratch_shapes=[pltpu.VMEM((RET_HEADS, RET_QK_DIM, RET_V_DIM), F32)],
        compiler_params=_params(dimension_semantics=("arbitrary",)),
    )(p, *tables, ypre, states, dyo)


def _shift_down(cur, prev, m):
    if m == 0:
        return cur
    rows = lax.broadcasted_iota(jnp.int32, cur.shape, 0)
    return jnp.where(rows < m, pltpu.roll(prev, m, 0), pltpu.roll(cur, m, 0))


def _shift_up(cur, nxt, m):
    if m == 0:
        return cur
    n = cur.shape[0]
    rows = lax.broadcasted_iota(jnp.int32, cur.shape, 0)
    return jnp.where(rows >= n - m, pltpu.roll(nxt, n - m, 0), pltpu.roll(cur, n - m, 0))


def conv_mixer_fwd(p, conv_w):
    length = p.shape[0]
    nc = length // CHUNK
    kt = conv_w.shape[0]

    def body(cur_ref, prev_ref, w_ref, y_ref):
        c = pl.program_id(0)
        rows = lax.broadcasted_iota(jnp.int32, (CHUNK, 1), 0)

        def u_of(ref, blk):
            ok = (blk * CHUNK + rows >= PAD) & (blk >= 0)
            return jnp.where(ok, ref[:, D_MODEL:2 * D_MODEL] * ref[:, 2 * D_MODEL:], 0.0)

        u_cur = u_of(cur_ref, c)
        u_prev = u_of(prev_ref, c - 1)
        acc = jnp.zeros((CHUNK, D_MODEL), F32)
        for i in range(kt):
            acc = acc + _shift_down(u_cur, u_prev, kt - 1 - i) * w_ref[i:i + 1, :]
        y_ref[...] = (cur_ref[:, :D_MODEL] * acc).astype(y_ref.dtype)

    return pl.pallas_call(
        body, name="conv_mixer_fwd", grid=(nc,),
        in_specs=[pl.BlockSpec((CHUNK, 3 * D_MODEL), lambda c: (c, 0)),
                  pl.BlockSpec((CHUNK, 3 * D_MODEL), lambda c: (jnp.maximum(c - 1, 0), 0)),
                  pl.BlockSpec((kt, D_MODEL), lambda c: (0, 0))],
        out_specs=pl.BlockSpec((CHUNK, D_MODEL), lambda c: (c, 0)),
        out_shape=jax.ShapeDtypeStruct((length, D_MODEL), BF16),
        compiler_params=_params(dimension_semantics=("arbitrary",)),
    )(p, p, conv_w)


def conv_mixer_bwd(p, conv_w, dy):
    length = p.shape[0]
    nc = length // CHUNK
    kt = conv_w.shape[0]

    def body(cur_ref, prev_ref, w_ref, dy_ref, dyn_ref, pn_ref, dp_ref, dw_ref):
        c = pl.program_id(0)
        rows = lax.broadcasted_iota(jnp.int32, (CHUNK, 1), 0)

        def u_of(ref, blk):
            ok = (blk * CHUNK + rows >= PAD) & (blk >= 0)
            return jnp.where(ok, ref[:, D_MODEL:2 * D_MODEL] * ref[:, 2 * D_MODEL:], 0.0)

        u_cur = u_of(cur_ref, c)
        u_prev = u_of(prev_ref, c - 1)
        b_gate = cur_ref[:, :D_MODEL]
        dyv = dy_ref[...]
        dconv = dyv * b_gate
        dconv_next = jnp.where(c + 1 < nc, dyn_ref[...] * pn_ref[:, :D_MODEL], 0.0)

        @pl.when(c == 0)
        def _():
            dw_ref[...] = jnp.zeros_like(dw_ref)

        acc = jnp.zeros((CHUNK, D_MODEL), F32)
        du = jnp.zeros((CHUNK, D_MODEL), F32)
        for i in range(kt):
            shifted = _shift_down(u_cur, u_prev, kt - 1 - i)
            acc = acc + shifted * w_ref[i:i + 1, :]
            dw_ref[i:i + 1, :] += jnp.sum(dconv * shifted, axis=0, keepdims=True)
            du = du + _shift_up(dconv, dconv_next, kt - 1 - i) * w_ref[i:i + 1, :]
        du = jnp.where(c * CHUNK + rows >= PAD, du, 0.0)
        dp_ref[:, :D_MODEL] = (dyv * acc).astype(dp_ref.dtype)
        dp_ref[:, D_MODEL:2 * D_MODEL] = (du * cur_ref[:, 2 * D_MODEL:]).astype(dp_ref.dtype)
        dp_ref[:, 2 * D_MODEL:] = (du * cur_ref[:, D_MODEL:2 * D_MODEL]).astype(dp_ref.dtype)

    nxt = lambda c: (jnp.minimum(c + 1, nc - 1), 0)
    return pl.pallas_call(
        body, name="conv_mixer_bwd", grid=(nc,),
        in_specs=[pl.BlockSpec((CHUNK, 3 * D_MODEL), lambda c: (c, 0)),
                  pl.BlockSpec((CHUNK, 3 * D_MODEL), lambda c: (jnp.maximum(c - 1, 0), 0)),
                  pl.BlockSpec((kt, D_MODEL), lambda c: (0, 0)),
                  pl.BlockSpec((CHUNK, D_MODEL), lambda c: (c, 0)),
                  pl.BlockSpec((CHUNK, D_MODEL), nxt),
                  pl.BlockSpec((CHUNK, 3 * D_MODEL), nxt)],
        out_specs=[pl.BlockSpec((CHUNK, 3 * D_MODEL), lambda c: (c, 0)),
                   pl.BlockSpec((kt, D_MODEL), lambda c: (0, 0))],
        out_shape=[jax.ShapeDtypeStruct((length, 3 * D_MODEL), BF16), jax.ShapeDtypeStruct((kt, D_MODEL), F32)],
        compiler_params=_params(dimension_semantics=("arbitrary",)),
    )(p, p, conv_w, dy, dy, p)


def merge_fwd(gate_logits, ups):
    length = gate_logits.shape[0]
    tm = _tile(length, (384, 256, 128))

    def body(g_ref, u0, u1, u2, u3, o_ref):
        acc = jnp.zeros((tm, D_MODEL), F32)
        for n, u in enumerate((u0, u1, u2, u3)):
            acc = acc + _sigmoid(g_ref[:, n * D_MODEL:(n + 1) * D_MODEL]) * u[...]
        o_ref[...] = acc.astype(o_ref.dtype)

    row = pl.BlockSpec((tm, D_MODEL), lambda i: (i, 0))
    return pl.pallas_call(
        body, name="merge_fwd", grid=(length // tm,),
        in_specs=[pl.BlockSpec((tm, N_BRANCH * D_MODEL), lambda i: (i, 0))] + [row] * N_BRANCH,
        out_specs=row, out_shape=jax.ShapeDtypeStruct((length, D_MODEL), BF16),
        compiler_params=_params(dimension_semantics=("arbitrary",)),
    )(gate_logits, *ups)


def merge_bwd(gate_logits, ups, dmerged):
    length = gate_logits.shape[0]
    tm = _tile(length, (384, 256, 128))

    def body(g_ref, u0, u1, u2, u3, dm_ref, dg_ref, d0, d1, d2, d3):
        dm = dm_ref[...]
        for n, (u, du) in enumerate(((u0, d0), (u1, d1), (u2, d2), (u3, d3))):
            cols = slice(n * D_MODEL, (n + 1) * D_MODEL)
            s = _sigmoid(g_ref[:, cols])
            du[...] = (dm * s).astype(du.dtype)
            dg_ref[:, cols] = (dm * u[...] * (s * (1.0 - s))).astype(dg_ref.dtype)

    row = pl.BlockSpec((tm, D_MODEL), lambda i: (i, 0))
    wide = pl.BlockSpec((tm, N_BRANCH * D_MODEL), lambda i: (i, 0))
    outs = pl.pallas_call(
        body, name="merge_bwd", grid=(length // tm,),
        in_specs=[wide] + [row] * (N_BRANCH + 1),
        out_specs=[wide] + [row] * N_BRANCH,
        out_shape=[jax.ShapeDtypeStruct((length, N_BRANCH * D_MODEL), BF16)]
        + [jax.ShapeDtypeStruct((length, D_MODEL), BF16)] * N_BRANCH,
        compiler_params=_params(dimension_semantics=("arbitrary",)),
    )(gate_logits, *ups, dmerged)
    return outs[0], list(outs[1:])


def swiglu_fwd(f):
    length = f.shape[0]
    tm = _tile(length, (384, 256, 128))

    def body(f_ref, o_ref):
        a = f_ref[:, :D_FF]
        o_ref[...] = (a * _sigmoid(a) * f_ref[:, D_FF:]).astype(o_ref.dtype)

    return pl.pallas_call(
        body, name="swiglu_fwd", grid=(length // tm,),
        in_specs=[pl.BlockSpec((tm, 2 * D_FF), lambda i: (i, 0))],
        out_specs=pl.BlockSpec((tm, D_FF), lambda i: (i, 0)),
        out_shape=jax.ShapeDtypeStruct((length, D_FF), BF16),
        compiler_params=_params(dimension_semantics=("arbitrary",)),
    )(f)


def swiglu_bwd(f, dact):
    length = f.shape[0]
    tm = _tile(length, (384, 256, 128))

    def body(f_ref, d_ref, df_ref):
        a = f_ref[:, :D_FF]
        up = f_ref[:, D_FF:]
        d = d_ref[...]
        s = _sigmoid(a)
        df_ref[:, :D_FF] = (d * up * (s * (1.0 + a * (1.0 - s)))).astype(df_ref.dtype)
        df_ref[:, D_FF:] = (d * (a * s)).astype(df_ref.dtype)

    return pl.pallas_call(
        body, name="swiglu_bwd", grid=(length // tm,),
        in_specs=[pl.BlockSpec((tm, 2 * D_FF), lambda i: (i, 0)), pl.BlockSpec((tm, D_FF), lambda i: (i, 0))],
        out_specs=pl.BlockSpec((tm, 2 * D_FF), lambda i: (i, 0)),
        out_shape=jax.ShapeDtypeStruct((length, 2 * D_FF), BF16),
        compiler_params=_params(dimension_semantics=("arbitrary",)),
    )(f, dact)


SSD_PAIRS = SSD_HEADS // 2
SSD_XBC = SSD_CONV_DIM
SSD_GW = SSD_INNER // SSD_GROUPS


def _split3(x):
    h1 = x.astype(BF16)
    r1 = x - h1.astype(F32)
    h2 = r1.astype(BF16)
    h3 = (r1 - h2.astype(F32)).astype(BF16)
    return h1, h2, h3


def _tri_apply(tri, x, dims):
    out = None
    for part in _split3(x):
        t = lax.dot_general(tri, part, dims, preferred_element_type=F32)
        out = t if out is None else out + t
    return out


def _softplus(x):
    return jnp.maximum(x, 0.0) + jnp.log(1.0 + jnp.exp(-jnp.abs(x)))


def _lane_pair(x, pair):
    lanes = lax.broadcasted_iota(jnp.int32, (x.shape[0], LANES), 1)
    return jnp.where(lanes < SSD_HEAD_DIM, x[:, 2 * pair:2 * pair + 1], x[:, 2 * pair + 1:2 * pair + 2])


def _half_sums(t):
    lanes = lax.broadcasted_iota(jnp.int32, t.shape, 1)
    lo = jnp.sum(jnp.where(lanes < SSD_HEAD_DIM, t, 0.0), axis=1, keepdims=True)
    return lo, jnp.sum(t, axis=1, keepdims=True) - lo


def _put_cols(cols):
    rows = cols[0].shape[0]
    lanes = lax.broadcasted_iota(jnp.int32, (rows, LANES), 1)
    out = jnp.zeros((rows, LANES), F32)
    for h, col in enumerate(cols):
        out = out + jnp.where(lanes == h, col, 0.0)
    return out


def ssd_pre_fwd(p, dt_raw, conv_w, conv_b, dt_bias, a_log):
    length = p.shape[0]
    nc = length // CHUNK
    kt = conv_w.shape[0]

    def body(cur_ref, prev_ref, raw_ref, w_ref, b_ref, bias_ref, alog_ref, act_ref, dt_ref, a_ref):
        c = pl.program_id(0)
        rows = lax.broadcasted_iota(jnp.int32, (CHUNK, 1), 0)
        vm = c * CHUNK + rows >= PAD
        u_cur = jnp.where(vm, cur_ref[:, SSD_INNER:], 0.0)
        u_prev = jnp.where(((c - 1) * CHUNK + rows >= PAD) & (c >= 1), prev_ref[:, SSD_INNER:], 0.0)
        pre = jnp.zeros((CHUNK, SSD_XBC), F32) + b_ref[...]
        for i in range(kt):
            pre = pre + _shift_down(u_cur, u_prev, kt - 1 - i) * w_ref[i:i + 1, :]
        act = pre * _sigmoid(pre)
        act_ref[:, :SSD_INNER] = jnp.where(vm, act[:, :SSD_INNER], 0.0)
        act_ref[:, SSD_INNER:] = act[:, SSD_INNER:]
        dt = _softplus(raw_ref[...] + bias_ref[...])
        dt_ref[...] = dt
        a_ref[...] = -jnp.exp(alog_ref[...]) * dt

    row = lambda w: pl.BlockSpec((CHUNK, w), lambda c: (c, 0))
    vec = lambda w: pl.BlockSpec((1, w), lambda c: (0, 0))
    return pl.pallas_call(
        body, name="ssd_pre_fwd", grid=(nc,),
        in_specs=[row(3 * D_MODEL), pl.BlockSpec((CHUNK, 3 * D_MODEL), lambda c: (jnp.maximum(c - 1, 0), 0)),
                  row(LANES), pl.BlockSpec((kt, SSD_XBC), lambda c: (0, 0)), vec(SSD_XBC), vec(LANES), vec(LANES)],
        out_specs=[row(SSD_XBC), row(LANES), row(LANES)],
        out_shape=[jax.ShapeDtypeStruct((length, SSD_XBC), F32), jax.ShapeDtypeStruct((length, LANES), F32),
                   jax.ShapeDtypeStruct((length, LANES), F32)],
        compiler_params=_params(dimension_semantics=("arbitrary",)),
    )(p, p, dt_raw, conv_w, conv_b, dt_bias, a_log)


def ssd_pre_bwd(p, dt_raw, conv_w, conv_b, dt_bias, a_log, dact, ddt, da, dz):
    length = p.shape[0]
    nc = length // CHUNK
    kt = conv_w.shape[0]

    def body(cur_ref, prev_ref, raw_ref, w_ref, b_ref, bias_ref, alog_ref, dact_ref, ddt_ref, da_ref, dz_ref,
             dp_ref, draw_ref, dw_ref, db_ref, dbias_ref, dalog_ref, dpre_next):
        step = pl.program_id(0)
        c = nc - 1 - step
        rows = lax.broadcasted_iota(jnp.int32, (CHUNK, 1), 0)
        vm = c * CHUNK + rows >= PAD

        @pl.when(step == 0)
        def _():
            dpre_next[...] = jnp.zeros_like(dpre_next)
            dw_ref[...] = jnp.zeros_like(dw_ref)
            db_ref[...] = jnp.zeros_like(db_ref)
            dbias_ref[...] = jnp.zeros_like(dbias_ref)
            dalog_ref[...] = jnp.zeros_like(dalog_ref)

        u_cur = jnp.where(vm, cur_ref[:, SSD_INNER:], 0.0)
        u_prev = jnp.where(((c - 1) * CHUNK + rows >= PAD) & (c >= 1), prev_ref[:, SSD_INNER:], 0.0)
        shifted = [_shift_down(u_cur, u_prev, kt - 1 - i) for i in range(kt)]
        pre = jnp.zeros((CHUNK, SSD_XBC), F32) + b_ref[...]
        for i in range(kt):
            pre = pre + shifted[i] * w_ref[i:i + 1, :]
        sg = _sigmoid(pre)
        lanes = lax.broadcasted_iota(jnp.int32, (CHUNK, SSD_XBC), 1)
        dact_v = jnp.where(vm | (lanes >= SSD_INNER), dact_ref[...], 0.0)
        dpre = dact_v * (sg * (1.0 + pre * (1.0 - sg)))
        db_ref[...] += jnp.sum(dpre, axis=0, keepdims=True)
        nxt = dpre_next[...]
        du = jnp.zeros((CHUNK, SSD_XBC), F32)
        for i in range(kt):
            dw_ref[i:i + 1, :] += jnp.sum(dpre * shifted[i], axis=0, keepdims=True)
            du = du + _shift_up(dpre, nxt, kt - 1 - i) * w_ref[i:i + 1, :]
        dpre_next[...] = dpre
        dp_ref[:, :SSD_INNER] = dz_ref[...].astype(dp_ref.dtype)
        dp_ref[:, SSD_INNER:] = jnp.where(vm, du, 0.0).astype(dp_ref.dtype)
        x = raw_ref[...] + bias_ref[...]
        neg_exp = -jnp.exp(alog_ref[...])
        dav = da_ref[...]
        draw = (ddt_ref[...] + dav * neg_exp) * _sigmoid(x)
        draw_ref[...] = draw.astype(draw_ref.dtype)
        dbias_ref[...] += jnp.sum(draw, axis=0, keepdims=True)
        dalog_ref[...] += jnp.sum(dav * (neg_exp * _softplus(x)), axis=0, keepdims=True)

    rev = lambda c: (nc - 1 - c, 0)
    row = lambda w: pl.BlockSpec((CHUNK, w), rev)
    vec = lambda w: pl.BlockSpec((1, w), lambda c: (0, 0))
    taps = pl.BlockSpec((kt, SSD_XBC), lambda c: (0, 0))
    return pl.pallas_call(
        body, name="ssd_pre_bwd", grid=(nc,),
        in_specs=[row(3 * D_MODEL),
                  pl.BlockSpec((CHUNK, 3 * D_MODEL), lambda c: (jnp.maximum(nc - 2 - c, 0), 0)),
                  row(LANES), taps, vec(SSD_XBC), vec(LANES), vec(LANES),
                  row(SSD_XBC), row(LANES), row(LANES), row(SSD_INNER)],
        out_specs=[row(3 * D_MODEL), row(LANES), taps, vec(SSD_XBC), vec(LANES), vec(LANES)],
        out_shape=[jax.ShapeDtypeStruct((length, 3 * D_MODEL), BF16), jax.ShapeDtypeStruct((length, LANES), BF16),
                   jax.ShapeDtypeStruct((kt, SSD_XBC), F32), jax.ShapeDtypeStruct((1, SSD_XBC), F32),
                   jax.ShapeDtypeStruct((1, LANES), F32), jax.ShapeDtypeStruct((1, LANES), F32)],
        scratch_shapes=[pltpu.VMEM((CHUNK, SSD_XBC), F32)],
        compiler_params=_params(dimension_semantics=("arbitrary",)),
    )(p, p, dt_raw, conv_w, conv_b, dt_bias, a_log, dact, ddt, da, dz)


def _tri_apply_lhs_t(x, tri):
    out = None
    for part in _split3(x):
        t = lax.dot_general(part, tri, (((0,), (1,)), ((), ())), preferred_element_type=F32)
        out = t if out is None else out + t
    return out


def ssd_core_fwd(act, dt, a, d_skip):
    length = act.shape[0]
    nc = length // CHUNK

    def body(act_ref, dt_ref, a_ref, dskip_ref, y_ref, st_ref, h_scr):
        c = pl.program_id(0)

        @pl.when(c == 0)
        def _():
            h_scr[...] = jnp.zeros_like(h_scr)

        r = lax.broadcasted_iota(jnp.int32, (CHUNK, CHUNK), 0)
        s = lax.broadcasted_iota(jnp.int32, (CHUNK, CHUNK), 1)
        causal = r >= s
        incl = jnp.where(causal, 1.0, 0.0).astype(BF16)
        a_v = a_ref[...]
        acs = _tri_apply(incl, a_v, (((1,), (0,)), ((), ())))
        acs_t = _tri_apply_lhs_t(a_v, incl)
        dt_v = dt_ref[...]
        lanes = lax.broadcasted_iota(jnp.int32, (CHUNK, LANES), 1)
        low = lanes < SSD_HEAD_DIM
        for g in range(SSD_GROUPS):
            bg = act_ref[:, SSD_INNER + g * SSD_STATE:SSD_INNER + (g + 1) * SSD_STATE].astype(BF16)
            cg = act_ref[:, SSD_INNER + (SSD_GROUPS + g) * SSD_STATE:
                         SSD_INNER + (SSD_GROUPS + g + 1) * SSD_STATE].astype(BF16)
            cb = lax.dot_general(cg, bg, _NT, preferred_element_type=F32)
            for pair in (2 * g, 2 * g + 1):
                cols = slice(pair * LANES, (pair + 1) * LANES)
                xs = act_ref[:, cols]
                x = xs * _lane_pair(dt_v, pair)
                ydiag = jnp.zeros((CHUNK, LANES), F32)
                for k, keep in ((0, low), (1, ~low)):
                    h = 2 * pair + k
                    seg = jnp.where(causal, jnp.exp(acs[:, h:h + 1] - acs_t[h:h + 1, :]), 0.0)
                    ydiag = ydiag + jnp.dot((cb * seg).astype(BF16), jnp.where(keep, x, 0.0).astype(BF16),
                                            preferred_element_type=F32)
                acs_p = _lane_pair(acs, pair)
                last = acs_p[CHUNK - 1:CHUNK, :]
                xds = (x * jnp.exp(last - acs_p)).astype(BF16)
                hprev = h_scr[pair]
                st_ref[pair] = hprev
                yoff = lax.dot_general(cg, hprev.astype(BF16), _NT, preferred_element_type=F32) * jnp.exp(acs_p)
                prow = lax.broadcasted_iota(jnp.int32, (LANES, 1), 0)
                cd = jnp.where(prow < SSD_HEAD_DIM, jnp.exp(acs_t[2 * pair:2 * pair + 1, CHUNK - 1:CHUNK]),
                               jnp.exp(acs_t[2 * pair + 1:2 * pair + 2, CHUNK - 1:CHUNK]))
                h_scr[pair] = hprev * cd + lax.dot_general(xds, bg, _TN, preferred_element_type=F32)
                y_ref[:, cols] = ydiag + yoff + xs * dskip_ref[:, cols]

    row = lambda w: pl.BlockSpec((CHUNK, w), lambda c: (c, 0))
    return pl.pallas_call(
        body, name="ssd_core_fwd", grid=(nc,),
        in_specs=[row(SSD_XBC), row(LANES), row(LANES), pl.BlockSpec((1, SSD_INNER), lambda c: (0, 0))],
        out_specs=[row(SSD_INNER), pl.BlockSpec((None, SSD_PAIRS, LANES, SSD_STATE), lambda c: (c, 0, 0, 0))],
        out_shape=[jax.ShapeDtypeStruct((length, SSD_INNER), F32),
                   jax.ShapeDtypeStruct((nc, SSD_PAIRS, LANES, SSD_STATE), F32)],
        scratch_shapes=[pltpu.VMEM((SSD_PAIRS, LANES, SSD_STATE), F32)],
        compiler_params=_params(dimension_semantics=("arbitrary",)),
    )(act, dt, a, d_skip)


def ssd_core_bwd(act, dt, a, d_skip, states, dy):
    length = act.shape[0]
    nc = length // CHUNK

    def body(act_ref, dt_ref, a_ref, dskip_ref, st_ref, dy_ref, dact_ref, ddt_ref, da_ref, dds_ref, dh_scr):
        step = pl.program_id(0)

        @pl.when(step == 0)
        def _():
            dh_scr[...] = jnp.zeros_like(dh_scr)
            dds_ref[...] = jnp.zeros_like(dds_ref)

        r = lax.broadcasted_iota(jnp.int32, (CHUNK, CHUNK), 0)
        s = lax.broadcasted_iota(jnp.int32, (CHUNK, CHUNK), 1)
        causal = r >= s
        incl = jnp.where(causal, 1.0, 0.0).astype(BF16)
        a_v = a_ref[...]
        acs = _tri_apply(incl, a_v, (((1,), (0,)), ((), ())))
        acs_t = _tri_apply_lhs_t(a_v, incl)
        dt_v = dt_ref[...]
        lanes = lax.broadcasted_iota(jnp.int32, (CHUNK, LANES), 1)
        low = lanes < SSD_HEAD_DIM
        prow = lax.broadcasted_iota(jnp.int32, (LANES, 1), 0)
        is_last = lax.broadcasted_iota(jnp.int32, (CHUNK, 1), 0) == CHUNK - 1
        dacs_cols = [None] * SSD_HEADS
        dacs_rows = [None] * SSD_HEADS
        ddt_cols = [None] * SSD_HEADS
        for g in range(SSD_GROUPS):
            b_cols = slice(SSD_INNER + g * SSD_STATE, SSD_INNER + (g + 1) * SSD_STATE)
            c_cols = slice(SSD_INNER + (SSD_GROUPS + g) * SSD_STATE, SSD_INNER + (SSD_GROUPS + g + 1) * SSD_STATE)
            bg = act_ref[:, b_cols].astype(BF16)
            cg = act_ref[:, c_cols].astype(BF16)
            cb = lax.dot_general(cg, bg, _NT, preferred_element_type=F32)
            dcb = jnp.zeros((CHUNK, CHUNK), F32)
            dbg = jnp.zeros((CHUNK, SSD_STATE), F32)
            dcg = jnp.zeros((CHUNK, SSD_STATE), F32)
            for pair in (2 * g, 2 * g + 1):
                cols = slice(pair * LANES, (pair + 1) * LANES)
                xs = act_ref[:, cols]
                dtp = _lane_pair(dt_v, pair)
                x = xs * dtp
                xb = x.astype(BF16)
                dyv = dy_ref[:, cols]
                dyb = dyv.astype(BF16)
                dds_ref[:, cols] += jnp.sum(dyv * xs, axis=0, keepdims=True)
                acs_p = _lane_pair(acs, pair)
                last = acs_p[CHUNK - 1:CHUNK, :]
                ds = jnp.exp(last - acs_p)
                ea = jnp.exp(acs_p)
                hprev = st_ref[pair]
                hb = hprev.astype(BF16)
                dh = dh_scr[pair]
                dhb = dh.astype(BF16)
                dx = jnp.zeros((CHUNK, LANES), F32)
                for k, keep in ((0, low), (1, ~low)):
                    h = 2 * pair + k
                    seg = jnp.where(causal, jnp.exp(acs[:, h:h + 1] - acs_t[h:h + 1, :]), 0.0)
                    lmat = cb * seg
                    dl = lax.dot_general(jnp.where(keep, dyv, 0.0).astype(BF16), xb, _NT,
                                         preferred_element_type=F32)
                    dcb = dcb + dl * seg
                    t = dl * lmat
                    dacs_cols[h] = jnp.sum(t, axis=1, keepdims=True)
                    dacs_rows[h] = jnp.sum(t, axis=0, keepdims=True)
                    dx = dx + jnp.where(keep, lax.dot_general(lmat.astype(BF16), dyb, _TN,
                                                              preferred_element_type=F32), 0.0)
                yoff = lax.dot_general(cg, hb, _NT, preferred_element_type=F32) * ea
                dm = (dyv * ea).astype(BF16)
                dcg = dcg + jnp.dot(dm, hb, preferred_element_type=F32)
                dxds = lax.dot_general(bg, dhb, _NT, preferred_element_type=F32)
                xds = x * ds
                dbg = dbg + jnp.dot(xds.astype(BF16), dhb, preferred_element_type=F32)
                dx = dx + dxds * ds
                t_ds = dxds * xds
                e_a = jnp.exp(acs_t[2 * pair:2 * pair + 1, CHUNK - 1:CHUNK])
                e_b = jnp.exp(acs_t[2 * pair + 1:2 * pair + 2, CHUNK - 1:CHUNK])
                cd = jnp.where(prow < SSD_HEAD_DIM, e_a, e_b)
                hd = dh * hprev
                dcd_a = jnp.sum(jnp.where(prow < SSD_HEAD_DIM, hd, 0.0), keepdims=True)
                dcd_b = jnp.sum(hd, keepdims=True) - dcd_a
                dh_scr[pair] = dh * cd + lax.dot_general(dm, cg, _TN, preferred_element_type=F32)
                col_lo, col_hi = _half_sums(dyv * yoff - t_ds)
                tot_lo, tot_hi = _half_sums(jnp.sum(t_ds, axis=0, keepdims=True))
                dacs_cols[2 * pair] += col_lo + jnp.where(is_last, tot_lo + dcd_a.reshape(1, 1) * e_a, 0.0)
                dacs_cols[2 * pair + 1] += col_hi + jnp.where(is_last, tot_hi + dcd_b.reshape(1, 1) * e_b, 0.0)
                dact_ref[:, cols] = dyv * dskip_ref[:, cols] + dx * dtp
                ddt_cols[2 * pair], ddt_cols[2 * pair + 1] = _half_sums(dx * xs)
            dcbb = dcb.astype(BF16)
            dact_ref[:, b_cols] = dbg + lax.dot_general(dcbb, cg, _TN, preferred_element_type=F32)
            dact_ref[:, c_cols] = dcg + jnp.dot(dcbb, bg, preferred_element_type=F32)
        ddt_ref[...] = _put_cols(ddt_cols)
        sub = lax.broadcasted_iota(jnp.int32, (LANES, CHUNK), 0)
        rows_mat = jnp.zeros((LANES, CHUNK), F32)
        for h in range(SSD_HEADS):
            rows_mat = rows_mat + jnp.where(sub == h, dacs_rows[h], 0.0)
        dacs = _put_cols(dacs_cols) - rows_mat.T
        da_ref[...] = _tri_apply(incl, dacs, (((0,), (0,)), ((), ())))

    rev = lambda c: (nc - 1 - c, 0)
    row = lambda w: pl.BlockSpec((CHUNK, w), rev)
    lane_vec = pl.BlockSpec((1, SSD_INNER), lambda c: (0, 0))
    return pl.pallas_call(
        body, name="ssd_core_bwd", grid=(nc,),
        in_specs=[row(SSD_XBC), row(LANES), row(LANES), lane_vec,
                  pl.BlockSpec((None, SSD_PAIRS, LANES, SSD_STATE), lambda c: (nc - 1 - c, 0, 0, 0)),
                  row(SSD_INNER)],
        out_specs=[row(SSD_XBC), row(LANES), row(LANES), lane_vec],
        out_shape=[jax.ShapeDtypeStruct((length, SSD_XBC), F32), jax.ShapeDtypeStruct((length, LANES), F32),
                   jax.ShapeDtypeStruct((length, LANES), F32), jax.ShapeDtypeStruct((1, SSD_INNER), F32)],
        scratch_shapes=[pltpu.VMEM((SSD_PAIRS, LANES, SSD_STATE), F32)],
        compiler_params=_params(dimension_semantics=("arbitrary",)),
    )(act, dt, a, d_skip, states, dy)


def ssd_post_fwd(y, p, norm_w):
    length = y.shape[0]
    tm = _tile(length, (384, 256, 128))

    def body(y_ref, p_ref, w_ref, o_ref):
        for g in range(SSD_GROUPS):
            cols = slice(g * SSD_GW, (g + 1) * SSD_GW)
            z = p_ref[:, cols]
            v = y_ref[:, cols] * (z * _sigmoid(z))
            o_ref[:, cols] = (v * lax.rsqrt(jnp.mean(v * v, axis=-1, keepdims=True) + EPS)
                              * w_ref[:, cols]).astype(o_ref.dtype)

    return pl.pallas_call(
        body, name="ssd_post_fwd", grid=(length // tm,),
        in_specs=[pl.BlockSpec((tm, SSD_INNER), lambda i: (i, 0)), pl.BlockSpec((tm, SSD_INNER), lambda i: (i, 0)),
                  pl.BlockSpec((1, SSD_INNER), lambda i: (0, 0))],
        out_specs=pl.BlockSpec((tm, SSD_INNER), lambda i: (i, 0)),
        out_shape=jax.ShapeDtypeStruct((length, SSD_INNER), BF16),
        compiler_params=_params(dimension_semantics=("arbitrary",)),
    )(y, p, norm_w)


def ssd_post_bwd(y, p, norm_w, dout):
    length = y.shape[0]
    tm = _tile(length, (384, 256, 128))

    def body(y_ref, p_ref, w_ref, do_ref, dy_ref, dz_ref, dw_ref):
        @pl.when(pl.program_id(0) == 0)
        def _():
            dw_ref[...] = jnp.zeros_like(dw_ref)

        for g in range(SSD_GROUPS):
            cols = slice(g * SSD_GW, (g + 1) * SSD_GW)
            z = p_ref[:, cols]
            yv = y_ref[:, cols]
            sg = _sigmoid(z)
            v = yv * (z * sg)
            rs = lax.rsqrt(jnp.mean(v * v, axis=-1, keepdims=True) + EPS)
            vh = v * rs
            do = do_ref[:, cols]
            dw_ref[:, cols] += jnp.sum(do * vh, axis=0, keepdims=True)
            dvh = do * w_ref[:, cols]
            dv = rs * (dvh - vh * jnp.mean(dvh * vh, axis=-1, keepdims=True))
            dy_ref[:, cols] = dv * (z * sg)
            dz_ref[:, cols] = dv * yv * (sg * (1.0 + z * (1.0 - sg)))

    blk = pl.BlockSpec((tm, SSD_INNER), lambda i: (i, 0))
    vec = pl.BlockSpec((1, SSD_INNER), lambda i: (0, 0))
    return pl.pallas_call(
        body, name="ssd_post_bwd", grid=(length // tm,),
        in_specs=[blk, blk, vec, blk], out_specs=[blk, blk, vec],
        out_shape=[jax.ShapeDtypeStruct((length, SSD_INNER), F32), jax.ShapeDtypeStruct((length, SSD_INNER), F32),
                   jax.ShapeDtypeStruct((1, SSD_INNER), F32)],
        compiler_params=_params(dimension_semantics=("arbitrary",)),
    )(y, p, norm_w, dout)


def _ssd_rows(lw):
    pad = lambda v: jnp.pad(v, (0, LANES - SSD_HEADS))[None]
    return dict(conv_w=lw['ssd_conv_w'], conv_b=lw['ssd_conv_b'][None], dt_bias=pad(lw['ssd_dt_bias']),
                a_log=pad(lw['ssd_a_log']), d_skip=jnp.repeat(lw['ssd_d'], SSD_HEAD_DIM)[None],
                norm_w=lw['ssd_norm'][None])


def ssd_fwd(p, dt_raw, rows):
    act, dt, a = ssd_pre_fwd(p, dt_raw, rows['conv_w'], rows['conv_b'], rows['dt_bias'], rows['a_log'])
    y, states = ssd_core_fwd(act, dt, a, rows['d_skip'])
    return ssd_post_fwd(y, p, rows['norm_w']), (act, dt, a, y, states)


def ssd_bwd(p, dt_raw, rows, saved, dout):
    act, dt, a, y, states = saved
    dy, dz, dnorm = ssd_post_bwd(y, p, rows['norm_w'], dout)
    dact, ddt, da, dskip_lanes = ssd_core_bwd(act, dt, a, rows['d_skip'], states, dy)
    dp, draw, dconv_w, dconv_b, dbias, dalog = ssd_pre_bwd(
        p, dt_raw, rows['conv_w'], rows['conv_b'], rows['dt_bias'], rows['a_log'], dact, ddt, da, dz)
    grads = dict(ssd_conv_w=dconv_w, ssd_conv_b=dconv_b[0], ssd_dt_bias=dbias[0, :SSD_HEADS],
                 ssd_a_log=dalog[0, :SSD_HEADS], ssd_norm=dnorm[0],
                 ssd_d=jnp.sum(dskip_lanes.reshape(SSD_HEADS, SSD_HEAD_DIM), axis=1))
    return dp, draw, grads


IN_A = (0, 3 * D_MODEL)
IN_S = (IN_A[1], IN_A[1] + SSD_INNER + SSD_CONV_DIM)
IN_DT = (IN_S[1], IN_S[1] + SSD_HEADS)
IN_R = (IN_DT[1], IN_DT[1] + 4 * D_MODEL)
IN_SB = (IN_R[1], IN_R[1] + 3 * D_MODEL)
IN_G = (IN_SB[1], IN_SB[1] + N_BRANCH * D_MODEL)
IN_WIDTH = IN_G[1]


def _layer_weights(full, small, l):
    w_in = full['w_in'][l]
    cut = lambda r: w_in[:, r[0]:r[1]]
    w_dt = jnp.pad(cut(IN_DT), ((0, 0), (0, DT_PAD - SSD_HEADS)))
    return dict(
        w_a=cut(IN_A), w_s=cut(IN_S), w_dt=w_dt, w_r=cut(IN_R), w_sb=cut(IN_SB), w_g=cut(IN_G),
        w_branch=[full['w_branch'][l, n] for n in range(N_BRANCH)],
        w_out=full['w_out'][l], w_ffn_in=full['w_ffn_in'][l], w_ffn_out=full['w_ffn_out'][l],
        conv_a=full['conv_a'][l], ssd_conv_w=full['ssd_conv_w'][l],
        ssd_conv_b=small['ssd_conv_b'][l], ssd_dt_bias=small['ssd_dt_bias'][l], ssd_a_log=small['ssd_a_log'][l],
        ssd_d=small['ssd_d'][l], ssd_norm=small['ssd_norm'][l],
        n_mix_pre=small['norm_mix_pre'][l][None], n_mix_post=small['norm_mix_post'][l][None],
        n_ffn_pre=small['norm_ffn_pre'][l][None], n_ffn_post=small['norm_ffn_post'][l][None],
    )


def _layer_fwd(h_res, lw, ret_tables):
    s = {'h_res': h_res, 'ret_tables': ret_tables}
    hn = rms_fwd(h_res, lw['n_mix_pre'], out_dtype=BF16, name="rms_mix_pre")
    s['hn'] = hn
    p_a = mm_nn(hn, lw['w_a'], name="proj_conv")
    p_s = mm_nn(hn, lw['w_s'], name="proj_ssd")
    p_dt = mm_nn(hn, lw['w_dt'], name="proj_dt")
    p_r = mm_nn(hn, lw['w_r'], name="proj_ret")
    p_sb = mm_nn(hn, lw['w_sb'], out_dtype=BF16, name="proj_sb")
    p_g = mm_nn(hn, lw['w_g'], name="proj_gate")
    y_a = conv_mixer_fwd(p_a, lw['conv_a'])
    s['p_a'] = p_a
    s['ssd_rows'] = _ssd_rows(lw)
    y_b, s['ssd_saved'] = ssd_fwd(p_s, p_dt, s['ssd_rows'])
    s['p_s'], s['p_dt'] = p_s, p_dt
    y_c, s['ret_ypre'], s['ret_states'] = ret_fwd(p_r, ret_tables)
    s['p_r'] = p_r
    y_d, s['sb_total'] = sb_fwd(p_sb)
    s['p_sb'] = p_sb
    ys = [y_a, y_b, y_c, y_d]
    s['ys'] = ys
    ups = [mm_nn(ys[n], lw['w_branch'][n], name="branch_up") for n in range(N_BRANCH)]
    merged = merge_fwd(p_g, ups)
    s['p_g'], s['ups'] = p_g, ups
    s['merged'] = merged
    mix = mm_nn(merged, lw['w_out'], name="mix_out")
    s['mix'] = mix
    h2 = rms_fwd(mix, lw['n_mix_post'], res=h_res, name="rms_mix_post")
    s['h2'] = h2
    hf = rms_fwd(h2, lw['n_ffn_pre'], out_dtype=BF16, name="rms_ffn_pre")
    s['hf'] = hf
    f = mm_nn(hf, lw['w_ffn_in'], name="ffn_in")
    act = swiglu_fwd(f)
    s['f'], s['act'] = f, act
    fo = mm_nn(act, lw['w_ffn_out'], name="ffn_out")
    s['fo'] = fo
    return rms_fwd(fo, lw['n_ffn_post'], res=h2, name="rms_ffn_post"), s


def _layer_bwd(dh3, lw, s):
    g = {}
    d_fo, g['norm_ffn_post'] = rms_bwd(s['fo'], lw['n_ffn_post'], dh3, dx_dtype=BF16, name="rms_ffn_post_bwd")
    d_act = mm_nt(d_fo, lw['w_ffn_out'], name="ffn_out_dx")
    g['w_ffn_out'] = mm_tn(s['act'], d_fo, name="ffn_out_dw")
    df = swiglu_bwd(s['f'], d_act)
    d_hf = mm_nt(df, lw['w_ffn_in'], name="ffn_in_dx")
    g['w_ffn_in'] = mm_tn(s['hf'], df, name="ffn_in_dw")
    dh2, g['norm_ffn_pre'] = rms_bwd(s['h2'], lw['n_ffn_pre'], d_hf, add=dh3, name="rms_ffn_pre_bwd")
    d_mix, g['norm_mix_post'] = rms_bwd(s['mix'], lw['n_mix_post'], dh2, dx_dtype=BF16, name="rms_mix_post_bwd")
    d_merged = mm_nt(d_mix, lw['w_out'], name="mix_out_dx")
    g['w_out'] = mm_tn(s['merged'], d_mix, name="mix_out_dw")
    dp_g, dups = merge_bwd(s['p_g'], s['ups'], d_merged)
    dys = [mm_nt(dups[n], lw['w_branch'][n], name="branch_dx") for n in range(N_BRANCH)]
    g['w_branch'] = jnp.stack([mm_tn(s['ys'][n], dups[n], name="branch_dw") for n in range(N_BRANCH)])
    dp_a, g['conv_a'] = conv_mixer_bwd(s['p_a'], lw['conv_a'], dys[0])
    dp_s, dp_dt, ssd_grads = ssd_bwd(s['p_s'], s['p_dt'], s['ssd_rows'], s['ssd_saved'], dys[1])
    g.update(ssd_grads)
    dp_r = ret_bwd(s['p_r'], s['ret_tables'], s['ret_ypre'], s['ret_states'], dys[2])
    dq, dk, dv = sb_bwd(s['p_sb'], s['sb_total'], dys[3])
    dp_sb = jnp.concatenate([dq, dk.astype(BF16), dv.astype(BF16)], axis=1)
    hn = s['hn']
    d_hn = None
    dws = []
    for dp, w, nm in ((dp_a, lw['w_a'], "conv"), (dp_s, lw['w_s'], "ssd"), (dp_dt, lw['w_dt'], "dt"),
                      (dp_r, lw['w_r'], "ret"), (dp_sb, lw['w_sb'], "sb"), (dp_g, lw['w_g'], "gate")):
        d_hn = mm_nt(dp, w, acc=d_hn, name="proj_dx")
        dws.append(mm_tn(hn, dp, name="proj_dw"))
    dws[2] = dws[2][:, :SSD_HEADS]
    g['w_in'] = jnp.concatenate(dws, axis=1)
    dh_res, g['norm_mix_pre'] = rms_bwd(s['h_res'], lw['n_mix_pre'], d_hn, add=dh2, name="rms_mix_pre_bwd")
    for k in ('norm_ffn_post', 'norm_ffn_pre', 'norm_mix_post', 'norm_mix_pre'):
        g[k] = g[k][0]
    return dh_res, g


def _quarter(a, axis, j):
    n = a.shape[axis] // N_CHIPS
    return lax.slice_in_dim(a, j * n, (j + 1) * n, axis=axis)


def kernel(x, meta, w_in, conv_a, ssd_conv_w, ssd_conv_b, ssd_dt_bias, ssd_a_log, ssd_d, ssd_norm, w_branch, w_out, w_ffn_in, w_ffn_out, norm_mix_pre, norm_mix_post, norm_ffn_pre, norm_ffn_post, loss_target, m_meta, m_w_in, m_conv_a, m_ssd_conv_w, m_ssd_conv_b, m_ssd_dt_bias, m_ssd_a_log, m_ssd_d, m_ssd_norm, m_w_branch, m_w_out, m_w_ffn_in, m_w_ffn_out, m_norm_mix_pre, m_norm_mix_post, m_norm_ffn_pre, m_norm_ffn_post, v_meta, v_w_in, v_conv_a, v_ssd_conv_w, v_ssd_conv_b, v_ssd_dt_bias, v_ssd_a_log, v_ssd_d, v_ssd_norm, v_w_branch, v_w_out, v_w_ffn_in, v_w_ffn_out, v_norm_mix_pre, v_norm_mix_post, v_norm_ffn_pre, v_norm_ffn_post):
    w_loc = dict(meta=meta, w_in=w_in, conv_a=conv_a, ssd_conv_w=ssd_conv_w, ssd_conv_b=ssd_conv_b,
                 ssd_dt_bias=ssd_dt_bias, ssd_a_log=ssd_a_log, ssd_d=ssd_d, ssd_norm=ssd_norm, w_branch=w_branch,
                 w_out=w_out, w_ffn_in=w_ffn_in, w_ffn_out=w_ffn_out, norm_mix_pre=norm_mix_pre,
                 norm_mix_post=norm_mix_post, norm_ffn_pre=norm_ffn_pre, norm_ffn_post=norm_ffn_post)
    m_loc = dict(meta=m_meta, w_in=m_w_in, conv_a=m_conv_a, ssd_conv_w=m_ssd_conv_w, ssd_conv_b=m_ssd_conv_b,
                 ssd_dt_bias=m_ssd_dt_bias, ssd_a_log=m_ssd_a_log, ssd_d=m_ssd_d, ssd_norm=m_ssd_norm,
                 w_branch=m_w_branch, w_out=m_w_out, w_ffn_in=m_w_ffn_in, w_ffn_out=m_w_ffn_out,
                 norm_mix_pre=m_norm_mix_pre, norm_mix_post=m_norm_mix_post, norm_ffn_pre=m_norm_ffn_pre,
                 norm_ffn_post=m_norm_ffn_post)
    v_loc = dict(meta=v_meta, w_in=v_w_in, conv_a=v_conv_a, ssd_conv_w=v_ssd_conv_w, ssd_conv_b=v_ssd_conv_b,
                 ssd_dt_bias=v_ssd_dt_bias, ssd_a_log=v_ssd_a_log, ssd_d=v_ssd_d, ssd_norm=v_ssd_norm,
                 w_branch=v_w_branch, w_out=v_w_out, w_ffn_in=v_w_ffn_in, w_ffn_out=v_w_ffn_out,
                 norm_mix_pre=v_norm_mix_pre, norm_mix_post=v_norm_mix_post, norm_ffn_pre=v_norm_ffn_pre,
                 norm_ffn_post=v_norm_ffn_post)

    gathered = gather_shards([w_loc[n].astype(BF16) for n in MATMUL_WEIGHTS]
                             + [_pack([w_loc[n] for n in SMALL_SHARDED], F32, 8)])
    full = {}
    for t, n in enumerate(MATMUL_WEIGHTS):
        full[n] = jnp.concatenate([gathered[t][j] for j in range(N_CHIPS)], axis=SHARD_AXIS[n])
    parts_f = [_unpack(gathered[-1][j], [w_loc[n].shape for n in SMALL_SHARDED]) for j in range(N_CHIPS)]
    for t, n in enumerate(SMALL_SHARDED):
        full[n] = jnp.concatenate([parts_f[j][t] for j in range(N_CHIPS)], axis=SHARD_AXIS[n])

    xs = x[0]
    seq = xs.shape[0]
    length = CHUNK + seq
    h = jnp.concatenate([jnp.zeros((PAD, D_MODEL), F32), full['meta'], xs], axis=0)
    lws, saved = [], []
    ret_tables = _ret_tables(length)
    for l in range(DEPTH):
        lw = _layer_weights(full, w_loc, l)
        h, s = _layer_fwd(h, lw, ret_tables)
        lws.append(lw)
        saved.append(s)

    loss_row, dh = loss_head(h, loss_target[0])
    loss = lax.psum(loss_row[0, 0], ("x", "y", "c"))

    layer_grads = [None] * DEPTH
    for l in reversed(range(DEPTH)):
        dh, layer_grads[l] = _layer_bwd(dh, lws[l], saved[l])
    grad_x = dh[CHUNK:][None]
    grads = {n: jnp.stack([layer_grads[l][n] for l in range(DEPTH)]) for n in WEIGHTS if n != 'meta'}
    grads['meta'] = dh[PAD:CHUNK]

    def rows2d(a):
        return a.reshape(-1, a.shape[-1])

    def small_pieces(j):
        return [_quarter(grads[n], SHARD_AXIS[n], j) if n in SHARD_AXIS else grads[n] for n in SMALL_ORDER]

    quarters = [jnp.stack([rows2d(_quarter(grads[n], SHARD_AXIS[n], j)) for j in range(N_CHIPS)])
                for n in MATMUL_WEIGHTS]
    quarters.append(jnp.stack([_pack(small_pieces(j), F32, 16) for j in range(N_CHIPS)]))
    reduced = reduce_gradients(quarters)
    results = {}
    for t, n in enumerate(MATMUL_WEIGHTS):
        shape = w_loc[n].shape
        new = adamw(reduced[t], rows2d(w_loc[n]), rows2d(m_loc[n]), rows2d(v_loc[n]))
        results[n] = [a.reshape(shape) for a in (reduced[t], *new)]
    small_new = adamw(reduced[-1], *[_pack([d[n] for n in SMALL_ORDER], F32, 16) for d in (w_loc, m_loc, v_loc)])
    small_shapes = [w_loc[n].shape for n in SMALL_ORDER]
    for kind, buf in enumerate((reduced[-1], *small_new)):
        for n, piece in zip(SMALL_ORDER, _unpack(buf, small_shapes)):
            results.setdefault(n, [None] * 4)[kind] = piece
    outs = [results[n][kind] for kind in range(4) for n in WEIGHTS]
    return (loss, grad_x, *outs)
```

```python
import functools
import math

import numpy as np
import jax
import jax.numpy as jnp
from jax import lax
from jax.experimental import pallas as pl
from jax.experimental.pallas import tpu as pltpu

F32 = jnp.float32
BF16 = jnp.bfloat16

D_MODEL = 1024
DEPTH = 2
N_META = 16
CHUNK = 128
PAD = CHUNK - N_META
EPS = 1e-6

CONV_A_K = 3
SSD_HEAD_DIM = 64
SSD_HEADS = 16
SSD_INNER = 1024
SSD_GROUPS = 4
SSD_STATE = 128
SSD_CONV_K = 4
SSD_CONV_DIM = SSD_INNER + 2 * SSD_GROUPS * SSD_STATE
RET_HEADS = 4
RET_QK_DIM = 256
RET_V_DIM = 256
RET_WIDTH = 1024
ROPE_BASE = 10000.0
SB_HEADS = 8
SB_HEAD_DIM = 128
N_BRANCH = 4
D_FF = 2816
DT_PAD = 128

ADAM_LR = 0.001
ADAM_B1 = 0.9
ADAM_B2 = 0.999
ADAM_EPS = 1e-08
ADAM_WD = 0.01
ADAM_STEP = 10

N_CHIPS = 4
N_DEV = 8
LANES = 128
VMEM_LIMIT = 56 * 1024 * 1024
MESH = pl.DeviceIdType.MESH

WEIGHTS = ['meta', 'w_in', 'conv_a', 'ssd_conv_w', 'ssd_conv_b', 'ssd_dt_bias', 'ssd_a_log', 'ssd_d',
           'ssd_norm', 'w_branch', 'w_out', 'w_ffn_in', 'w_ffn_out', 'norm_mix_pre', 'norm_mix_post',
           'norm_ffn_pre', 'norm_ffn_post']
SHARD_AXIS = {'meta': 1, 'w_in': 2, 'conv_a': 2, 'ssd_conv_w': 2, 'w_branch': 2, 'w_out': 1,
              'w_ffn_in': 2, 'w_ffn_out': 1}
MATMUL_WEIGHTS = ['w_in', 'w_branch', 'w_out', 'w_ffn_in', 'w_ffn_out']
SMALL_SHARDED = ['meta', 'conv_a', 'ssd_conv_w']
SMALL_ORDER = SMALL_SHARDED + [n for n in WEIGHTS if n not in SHARD_AXIS]


def _params(**kw):
    return pltpu.CompilerParams(vmem_limit_bytes=VMEM_LIMIT, **kw)


def _tile(n, prefs):
    for p in prefs:
        if n % p == 0:
            return p
    return n


MM_VMEM_BUDGET = 40 * 1024 * 1024
MM_ROW_TILES = (2752, 1376, 688, 384, 256, 128)
MM_COL_TILES = (1024, 512, 256, 128)


def _mm_tiles(m, n, cost):
    for tm in MM_ROW_TILES:
        if m % tm:
            continue
        for tn in MM_COL_TILES:
            if n % tn == 0 and cost(tm, tn) <= MM_VMEM_BUDGET:
                return tm, tn
    return _tile(m, (128, 8)), _tile(n, (128,))


def _size(x):
    return jnp.dtype(x.dtype).itemsize


def mm_nn(a, b, out_dtype=F32, name="mm_nn"):
    m, k = a.shape
    n = b.shape[1]
    ob = jnp.dtype(out_dtype).itemsize
    tm, tn = _mm_tiles(m, n, lambda tm, tn: (2 * tm * k * _size(a) + tm * k * 2 + 2 * k * tn * _size(b)
                                              + 2 * tm * tn * ob + tm * tn * 4))

    def body(a_ref, b_ref, o_ref):
        o_ref[...] = jnp.dot(a_ref[...].astype(BF16), b_ref[...].astype(BF16),
                             preferred_element_type=F32).astype(o_ref.dtype)

    return pl.pallas_call(
        body, name=name, grid=(m // tm, n // tn),
        in_specs=[pl.BlockSpec((tm, k), lambda i, j: (i, 0)), pl.BlockSpec((k, tn), lambda i, j: (0, j))],
        out_specs=pl.BlockSpec((tm, tn), lambda i, j: (i, j)),
        out_shape=jax.ShapeDtypeStruct((m, n), out_dtype),
        compiler_params=_params(dimension_semantics=("arbitrary", "arbitrary")),
    )(a, b)


def mm_nt(g, w, acc=None, name="mm_nt"):
    m, n = g.shape
    k = w.shape[0]
    has_acc = acc is not None
    tm, tn = _mm_tiles(m, n, lambda tm, tn: ((2 + 2 * has_acc) * tm * k * 4 + tm * k * 4 + 2 * tm * tn * _size(g)
                                              + tm * tn * 2 + 2 * k * tn * _size(w)))

    def body(*refs):
        if has_acc:
            g_ref, w_ref, acc_ref, o_ref = refs
        else:
            g_ref, w_ref, o_ref = refs
        j = pl.program_id(1)

        @pl.when(j == 0)
        def _():
            o_ref[...] = acc_ref[...] if has_acc else jnp.zeros_like(o_ref)

        o_ref[...] += lax.dot_general(g_ref[...].astype(BF16), w_ref[...].astype(BF16),
                                      (((1,), (1,)), ((), ())), preferred_element_type=F32)

    in_specs = [pl.BlockSpec((tm, tn), lambda i, j: (i, j)), pl.BlockSpec((k, tn), lambda i, j: (0, j))]
    args = [g, w]
    if has_acc:
        in_specs.append(pl.BlockSpec((tm, k), lambda i, j: (i, 0)))
        args.append(acc)
    return pl.pallas_call(
        body, name=name, grid=(m // tm, n // tn),
        in_specs=in_specs,
        out_specs=pl.BlockSpec((tm, k), lambda i, j: (i, 0)),
        out_shape=jax.ShapeDtypeStruct((m, k), F32),
        compiler_params=_params(dimension_semantics=("arbitrary", "arbitrary")),
    )(*args)


def mm_tn(x, g, name="mm_tn"):
    m, k = x.shape
    n = g.shape[1]
    tk = _tile(k, (1024, 1408, 512, 256, 128))
    tm, tn = _mm_tiles(m, n, lambda tm, tn: (3 * tk * tn * 4 + 2 * tm * tk * _size(x) + tm * tk * 2
                                              + 2 * tm * tn * _size(g) + tm * tn * 2))

    def body(x_ref, g_ref, o_ref):
        s = pl.program_id(2)

        @pl.when(s == 0)
        def _():
            o_ref[...] = jnp.zeros_like(o_ref)

        o_ref[...] += lax.dot_general(x_ref[...].astype(BF16), g_ref[...].astype(BF16),
                                      (((0,), (0,)), ((), ())), preferred_element_type=F32)

    return pl.pallas_call(
        body, name=name, grid=(k // tk, n // tn, m // tm),
        in_specs=[pl.BlockSpec((tm, tk), lambda a, b, s: (s, a)), pl.BlockSpec((tm, tn), lambda a, b, s: (s, b))],
        out_specs=pl.BlockSpec((tk, tn), lambda a, b, s: (a, b)),
        out_shape=jax.ShapeDtypeStruct((k, n), F32),
        compiler_params=_params(dimension_semantics=("arbitrary", "arbitrary", "arbitrary")),
    )(x, g)


def rms_fwd(x, w, res=None, out_dtype=F32, name="rms_fwd"):
    m, d = x.shape
    tm = _tile(m, (384, 256, 128))
    has_res = res is not None

    def body(*refs):
        if has_res:
            x_ref, w_ref, r_ref, o_ref = refs
        else:
            x_ref, w_ref, o_ref = refs
        xv = x_ref[...]
        y = xv * lax.rsqrt(jnp.mean(xv * xv, axis=-1, keepdims=True) + EPS) * w_ref[...]
        o_ref[...] = (y + r_ref[...] if has_res else y).astype(o_ref.dtype)

    row = pl.BlockSpec((tm, d), lambda i: (i, 0))
    in_specs = [row, pl.BlockSpec((1, d), lambda i: (0, 0))]
    args = [x, w]
    if has_res:
        in_specs.append(row)
        args.append(res)
    return pl.pallas_call(
        body, name=name, grid=(m // tm,), in_specs=in_specs, out_specs=row,
        out_shape=jax.ShapeDtypeStruct((m, d), out_dtype),
        compiler_params=_params(dimension_semantics=("arbitrary",)),
    )(*args)


def rms_bwd(x, w, dy, add=None, dx_dtype=F32, name="rms_bwd"):
    m, d = x.shape
    tm = _tile(m, (384, 256, 128))
    has_add = add is not None

    def body(*refs):
        if has_add:
            x_ref, w_ref, dy_ref, add_ref, dx_ref, dw_ref = refs
        else:
            x_ref, w_ref, dy_ref, dx_ref, dw_ref = refs
        i = pl.program_id(0)
        xv = x_ref[...]
        dyv = dy_ref[...]
        r = lax.rsqrt(jnp.mean(xv * xv, axis=-1, keepdims=True) + EPS)
        xh = xv * r
        dxh = dyv * w_ref[...]
        dx = r * (dxh - xh * jnp.mean(dxh * xh, axis=-1, keepdims=True))
        dx_ref[...] = (dx + add_ref[...] if has_add else dx).astype(dx_ref.dtype)

        @pl.when(i == 0)
        def _():
            dw_ref[...] = jnp.zeros_like(dw_ref)

        dw_ref[...] += jnp.sum(dyv * xh, axis=0, keepdims=True)

    row = pl.BlockSpec((tm, d), lambda i: (i, 0))
    vec = pl.BlockSpec((1, d), lambda i: (0, 0))
    in_specs = [row, vec, row]
    args = [x, w, dy]
    if has_add:
        in_specs.append(row)
        args.append(add)
    return pl.pallas_call(
        body, name=name, grid=(m // tm,), in_specs=in_specs, out_specs=[row, vec],
        out_shape=[jax.ShapeDtypeStruct((m, d), dx_dtype), jax.ShapeDtypeStruct((1, d), F32)],
        compiler_params=_params(dimension_semantics=("arbitrary",)),
    )(*args)


def loss_head(h, target):
    l, d = h.shape
    nblk = l // CHUNK

    def body(h_ref, t_ref, loss_ref, dh_ref, acc_ref):
        i = pl.program_id(0)

        @pl.when(i == 0)
        def _():
            acc_ref[...] = jnp.zeros_like(acc_ref)
            dh_ref[...] = jnp.zeros_like(dh_ref)

        @pl.when(i > 0)
        def _():
            e = h_ref[...] - t_ref[...]
            dh_ref[...] = e / d
            acc_ref[...] += jnp.sum(e * e, axis=0, keepdims=True)

        @pl.when(i == nblk - 1)
        def _():
            loss_ref[...] = jnp.zeros_like(loss_ref) + 0.5 * jnp.sum(acc_ref[...]) / d

    return pl.pallas_call(
        body, name="loss_head", grid=(nblk,),
        in_specs=[pl.BlockSpec((CHUNK, d), lambda i: (i, 0)),
                  pl.BlockSpec((CHUNK, d), lambda i: (jnp.maximum(i - 1, 0), 0))],
        out_specs=[pl.BlockSpec((1, LANES), lambda i: (0, 0)), pl.BlockSpec((CHUNK, d), lambda i: (i, 0))],
        out_shape=[jax.ShapeDtypeStruct((1, LANES), F32), jax.ShapeDtypeStruct((l, d), F32)],
        scratch_shapes=[pltpu.VMEM((1, d), F32)],
        compiler_params=_params(dimension_semantics=("arbitrary",)),
    )(h, target)


SB_BLK = 128


def _sb_tile(l):
    return _tile(l, (384, 256, 128))


def _sb_tri(strict_later):
    r = lax.broadcasted_iota(jnp.int32, (SB_BLK, 2 * SB_BLK), 0)
    c = lax.broadcasted_iota(jnp.int32, (SB_BLK, 2 * SB_BLK), 1)
    keep = (r > c) if strict_later else (r < c)
    return jnp.where(keep | (c >= SB_BLK), 1.0, 0.0).astype(BF16)


def _sb_mask(i, j, t):
    qpos = i * t + lax.broadcasted_iota(jnp.int32, (t, t), 0)
    kpos = j * t + lax.broadcasted_iota(jnp.int32, (t, t), 1)
    return (kpos < qpos) & (kpos >= PAD)


def _sb_scores(q, k, scale, mask):
    z = lax.dot_general(q, k, (((1,), (1,)), ((), ())), preferred_element_type=F32) * scale
    sp = jnp.maximum(z, 0.0) + jnp.log(1.0 + jnp.exp(-jnp.abs(z)))
    lneg = -sp if mask is None else jnp.where(mask, -sp, 0.0)
    return z - sp, lneg


def _sb_block_sums(x, tri):
    s = jnp.dot(x.astype(BF16), tri, preferred_element_type=F32)
    return s[:, :SB_BLK], s[:, SB_BLK:]


SB_DEAD = -110.0


def _sb_walk_down(i, step, carry):
    carry = step(i, carry, True)

    def alive(c):
        return (jnp.max(c[0]) > SB_DEAD).astype(jnp.int32)

    def body(state):
        j, _, c = state
        c = step(j, c, False)
        return j - 1, alive(c), c

    j, go, carry = lax.while_loop(lambda s: (s[0] >= 1) & (s[1] > 0), body, (i - 1, alive(carry), carry))
    reach0 = ((j == 0) & (go > 0) & (i > 0)).astype(jnp.int32)
    carry = lax.fori_loop(0, reach0, lambda t, c: step(0, c, True), carry)
    return carry, jnp.where(reach0 > 0, 0, j + 1)


def _sb_walk_up(i, first, step, carry):
    start0 = ((first == 0) & (i > 0)).astype(jnp.int32)
    carry = lax.fori_loop(0, start0, lambda t, c: step(0, c, True), carry)
    carry = lax.fori_loop(jnp.maximum(first, 1), i, lambda j, c: step(j, c, False), carry)
    return step(i, carry, True)


def sb_fwd(qkv):
    l = qkv.shape[0]
    t = _sb_tile(l)
    nb = t // SB_BLK
    scale = SB_HEAD_DIM ** -0.5

    def body(q_ref, k_ref, v_ref, o_ref, walk_ref):
        i = pl.program_id(1)
        q = q_ref[...]
        tri = _sb_tri(True)

        def step(j, carry, masked):
            later, acc = carry
            rows = pl.ds(pl.multiple_of(j * t, t), t)
            mask = _sb_mask(i, j, t) if masked else None
            lpos, lneg = _sb_scores(q, k_ref[rows, :], scale, mask)
            ws = [None] * nb
            for b in reversed(range(nb)):
                cols = slice(b * SB_BLK, (b + 1) * SB_BLK)
                within, total = _sb_block_sums(lneg[:, cols], tri)
                ws[b] = jnp.exp(lpos[:, cols] + within + later)
                later = later + total
            w = jnp.concatenate(ws, axis=1)
            if masked:
                w = jnp.where(mask, w, 0.0)
            acc = acc + jnp.dot(w.astype(BF16), v_ref[rows, :], preferred_element_type=F32)
            return later, acc

        carry = (jnp.zeros((t, SB_BLK), F32), jnp.zeros((t, SB_HEAD_DIM), F32))
        (later, acc), first = _sb_walk_down(i, step, carry)
        o_ref[...] = acc.astype(o_ref.dtype)
        walk_ref[0] = later[:, :1]
        walk_ref[1] = jnp.zeros((t, 1), F32) + first.astype(F32)

    return pl.pallas_call(
        body, name="sb_fwd", grid=(SB_HEADS, l // t),
        in_specs=[pl.BlockSpec((t, SB_HEAD_DIM), lambda h, i: (i, h)),
                  pl.BlockSpec((l, SB_HEAD_DIM), lambda h, i: (0, SB_HEADS + h)),
                  pl.BlockSpec((l, SB_HEAD_DIM), lambda h, i: (0, 2 * SB_HEADS + h))],
        out_specs=[pl.BlockSpec((t, SB_HEAD_DIM), lambda h, i: (i, h)),
                   pl.BlockSpec((2, None, t, 1), lambda h, i: (0, h, i, 0))],
        out_shape=[jax.ShapeDtypeStruct((l, D_MODEL), BF16), jax.ShapeDtypeStruct((2, SB_HEADS, l, 1), F32)],
        compiler_params=_params(dimension_semantics=("arbitrary", "arbitrary")),
    )(qkv, qkv, qkv)


def sb_bwd(qkv, walk, dout):
    l = qkv.shape[0]
    t = _sb_tile(l)
    nb = t // SB_BLK
    nq = l // t
    scale = SB_HEAD_DIM ** -0.5

    def body(q_ref, k_ref, v_ref, walk_ref, do_ref, dq_ref, dk_hbm, dv_hbm, dk_acc, dv_acc):
        h = pl.program_id(0)
        i = pl.program_id(1)

        @pl.when(i == 0)
        def _():
            dk_acc[...] = jnp.zeros_like(dk_acc)
            dv_acc[...] = jnp.zeros_like(dv_acc)

        q = q_ref[...]
        dob = do_ref[...].astype(BF16)
        tri_later = _sb_tri(True)
        tri_before = _sb_tri(False)

        def step(j, carry, masked):
            later, g_before, dq = carry
            rows = pl.ds(pl.multiple_of(j * t, t), t)
            k = k_ref[rows, :]
            v = v_ref[rows, :]
            mask = _sb_mask(i, j, t) if masked else None
            lpos, lneg = _sb_scores(q, k, scale, mask)
            dw = lax.dot_general(dob, v, (((1,), (1,)), ((), ())), preferred_element_type=F32)
            ws, dzs = [None] * nb, [None] * nb
            for b in range(nb):
                cols = slice(b * SB_BLK, (b + 1) * SB_BLK)
                within, total = _sb_block_sums(lneg[:, cols], tri_later)
                later = later - total
                wb = jnp.exp(lpos[:, cols] + within + later)
                if masked:
                    wb = jnp.where(mask[:, cols], wb, 0.0)
                g = dw[:, cols] * wb
                g_within, g_total = _sb_block_sums(g, tri_before)
                dz = g - (g + g_before + g_within) * jnp.exp(lpos[:, cols])
                if masked:
                    dz = jnp.where(mask[:, cols], dz, 0.0)
                g_before = g_before + g_total
                ws[b] = wb.astype(BF16)
                dzs[b] = (dz * scale).astype(BF16)
            w = jnp.concatenate(ws, axis=1)
            dzb = jnp.concatenate(dzs, axis=1)
            dq = dq + jnp.dot(dzb, k, preferred_element_type=F32)
            dk_acc[rows, :] += lax.dot_general(dzb, q, (((0,), (0,)), ((), ())), preferred_element_type=F32)
            dv_acc[rows, :] += lax.dot_general(w, dob, (((0,), (0,)), ((), ())), preferred_element_type=F32)
            return later, g_before, dq

        carry = (jnp.broadcast_to(walk_ref[0], (t, SB_BLK)), jnp.zeros((t, SB_BLK), F32),
                 jnp.zeros((t, SB_HEAD_DIM), F32))
        first = jnp.max(walk_ref[1]).astype(jnp.int32)
        _, _, dq = _sb_walk_up(i, first, step, carry)
        dq_ref[...] = dq.astype(dq_ref.dtype)

        @pl.when(i == nq - 1)
        def _():
            cols = pl.ds(pl.multiple_of(h * SB_HEAD_DIM, SB_HEAD_DIM), SB_HEAD_DIM)
            pltpu.sync_copy(dk_acc, dk_hbm.at[:, cols])
            pltpu.sync_copy(dv_acc, dv_hbm.at[:, cols])

    blk = lambda h, i: (i, h)
    return pl.pallas_call(
        body, name="sb_bwd", grid=(SB_HEADS, nq),
        in_specs=[pl.BlockSpec((t, SB_HEAD_DIM), blk),
                  pl.BlockSpec((l, SB_HEAD_DIM), lambda h, i: (0, SB_HEADS + h)),
                  pl.BlockSpec((l, SB_HEAD_DIM), lambda h, i: (0, 2 * SB_HEADS + h)),
                  pl.BlockSpec((2, None, t, 1), lambda h, i: (0, h, i, 0)), pl.BlockSpec((t, SB_HEAD_DIM), blk)],
        out_specs=[pl.BlockSpec((t, SB_HEAD_DIM), blk), pl.BlockSpec(memory_space=pl.ANY),
                   pl.BlockSpec(memory_space=pl.ANY)],
        out_shape=[jax.ShapeDtypeStruct((l, D_MODEL), BF16), jax.ShapeDtypeStruct((l, D_MODEL), F32),
                   jax.ShapeDtypeStruct((l, D_MODEL), F32)],
        scratch_shapes=[pltpu.VMEM((l, SB_HEAD_DIM), F32), pltpu.VMEM((l, SB_HEAD_DIM), F32)],
        compiler_params=_params(dimension_semantics=("arbitrary", "arbitrary")),
    )(qkv, qkv, qkv, walk, dout)


_HBM = pl.BlockSpec(memory_space=pl.ANY)


def _other_chips(x, y):
    return [(1 - x, y), (x, 1 - y), (1 - x, 1 - y)]


def _comm_call(body, name, ins, out_shapes, n_remote, n_local):
    return pl.pallas_call(
        body, name=name, in_specs=[_HBM] * len(ins), out_specs=[_HBM] * len(out_shapes), out_shape=out_shapes,
        scratch_shapes=[pltpu.SemaphoreType.DMA((n_remote,)), pltpu.SemaphoreType.DMA((n_remote,)),
                        pltpu.SemaphoreType.DMA((max(n_local, 1),))],
    )(*ins)


def gather_shards(shards):
    n = len(shards)

    def body(*refs):
        ins, outs = refs[:n], refs[n:2 * n]
        send_sems, recv_sems, local_sems = refs[2 * n:]
        x, y, c = lax.axis_index("x"), lax.axis_index("y"), lax.axis_index("c")
        me = 2 * x + y

        def half(t):
            rh = ins[t].shape[0] // 2
            return ins[t].at[pl.ds(pl.multiple_of(c * rh, 8), rh), :]

        own = [pltpu.make_async_copy(half(t), outs[t].at[me], local_sems.at[t]) for t in range(n)]
        for cp in own:
            cp.start()

        def copy(t, k, px, py, slot):
            return pltpu.make_async_remote_copy(
                src_ref=half(t), dst_ref=outs[t].at[slot], send_sem=send_sems.at[3 * t + k],
                recv_sem=recv_sems.at[3 * t + k], device_id=(px, py, c), device_id_type=MESH)

        chips = _other_chips(x, y)
        sends = [copy(t, k, px, py, me) for t in range(n) for k, (px, py) in enumerate(chips)]
        for cp in sends:
            cp.start()
        for t in range(n):
            for k, (px, py) in enumerate(chips):
                copy(t, k, px, py, 2 * px + py).wait_recv()
        for cp in sends:
            cp.wait_send()
        for cp in own:
            cp.wait()

    out_shapes = [jax.ShapeDtypeStruct((N_CHIPS, s.shape[0] // 2, s.shape[1]), s.dtype) for s in shards]
    return _comm_call(body, "gather_shards", shards, out_shapes, 3 * n, n)


def sibling_swap_halves(gs):
    n = len(gs)

    def body(*refs):
        ins, outs = refs[:n], refs[n:2 * n]
        send_sems, recv_sems, _ = refs[2 * n:]
        x, y, c = lax.axis_index("x"), lax.axis_index("y"), lax.axis_index("c")
        copies = []
        for t in range(n):
            rh = ins[t].shape[1] // 2
            src = ins[t].at[:, pl.ds(pl.multiple_of((1 - c) * rh, 8), rh), :]
            copies.append(pltpu.make_async_remote_copy(
                src_ref=src, dst_ref=outs[t], send_sem=send_sems.at[t], recv_sem=recv_sems.at[t],
                device_id=(x, y, 1 - c), device_id_type=MESH))
        for cp in copies:
            cp.start()
        for cp in copies:
            cp.wait_recv()
        for cp in copies:
            cp.wait_send()

    out_shapes = [jax.ShapeDtypeStruct((g.shape[0], g.shape[1] // 2, g.shape[2]), g.dtype) for g in gs]
    return _comm_call(body, "sibling_swap_halves", gs, out_shapes, n, 0)


def chip_exchange(ps):
    n = len(ps)

    def body(*refs):
        ins, outs = refs[:n], refs[n:2 * n]
        send_sems, recv_sems, local_sems = refs[2 * n:]
        x, y, c = lax.axis_index("x"), lax.axis_index("y"), lax.axis_index("c")
        me = 2 * x + y
        own = [pltpu.make_async_copy(ins[t].at[me], outs[t].at[me], local_sems.at[t]) for t in range(n)]
        for cp in own:
            cp.start()

        def copy(t, k, px, py, src_slot, dst_slot):
            return pltpu.make_async_remote_copy(
                src_ref=ins[t].at[src_slot], dst_ref=outs[t].at[dst_slot], send_sem=send_sems.at[3 * t + k],
                recv_sem=recv_sems.at[3 * t + k], device_id=(px, py, c), device_id_type=MESH)

        chips = _other_chips(x, y)
        sends = [copy(t, k, px, py, 2 * px + py, me) for t in range(n) for k, (px, py) in enumerate(chips)]
        for cp in sends:
            cp.start()
        for t in range(n):
            for k, (px, py) in enumerate(chips):
                copy(t, k, px, py, me, 2 * px + py).wait_recv()
        for cp in sends:
            cp.wait_send()
        for cp in own:
            cp.wait()

    out_shapes = [jax.ShapeDtypeStruct(p.shape, p.dtype) for p in ps]
    return _comm_call(body, "chip_exchange", ps, out_shapes, 3 * n, n)


def sibling_share(ss, name):
    n = len(ss)

    def body(*refs):
        ins, outs = refs[:n], refs[n:2 * n]
        send_sems, recv_sems, _ = refs[2 * n:]
        x, y, c = lax.axis_index("x"), lax.axis_index("y"), lax.axis_index("c")
        copies = [pltpu.make_async_remote_copy(
            src_ref=ins[t], dst_ref=outs[t], send_sem=send_sems.at[t], recv_sem=recv_sems.at[t],
            device_id=(x, y, 1 - c), device_id_type=MESH) for t in range(n)]
        for cp in copies:
            cp.start()
        for cp in copies:
            cp.wait_recv()
        for cp in copies:
            cp.wait_send()

    out_shapes = [jax.ShapeDtypeStruct(s.shape, s.dtype) for s in ss]
    return _comm_call(body, name, ss, out_shapes, n, 0)


EW_BLOCK_BYTES = 2 * 1024 * 1024


def _ew_rows(rows, cols, copies=1):
    padded = -(-cols // LANES) * LANES
    for tr in (1024, 512, 256, 128, 64, 32, 16, 8):
        if rows % tr == 0 and copies * tr * padded * 4 <= EW_BLOCK_BYTES:
            return tr
    return rows


def add_pairs(a, b, out_dtype=F32):
    rows, cols = a.shape
    tr = _ew_rows(rows, cols)

    def body(a_ref, b_ref, o_ref):
        o_ref[...] = (a_ref[...] + b_ref[...]).astype(o_ref.dtype)

    blk = pl.BlockSpec((tr, cols), lambda i: (i, 0))
    return pl.pallas_call(
        body, name="add_pairs", grid=(rows // tr,), in_specs=[blk, blk], out_specs=blk,
        out_shape=jax.ShapeDtypeStruct((rows, cols), out_dtype),
        compiler_params=_params(dimension_semantics=("arbitrary",)),
    )(a, b)


def sum_chips(slots):
    _, rows, cols = slots.shape
    tr = _ew_rows(rows, cols, N_CHIPS)

    def body(s_ref, o_ref):
        acc = s_ref[0].astype(F32)
        for j in range(1, N_CHIPS):
            acc = acc + s_ref[j].astype(F32)
        o_ref[...] = acc

    return pl.pallas_call(
        body, name="sum_chips", grid=(rows // tr,),
        in_specs=[pl.BlockSpec((N_CHIPS, tr, cols), lambda i: (0, i, 0))],
        out_specs=pl.BlockSpec((tr, cols), lambda i: (i, 0)),
        out_shape=jax.ShapeDtypeStruct((rows, cols), F32),
        compiler_params=_params(dimension_semantics=("arbitrary",)),
    )(slots)


def adamw(g, w, m, v):
    rows, cols = g.shape
    tr = _ew_rows(rows, cols)

    def body(g_ref, w_ref, m_ref, v_ref, d_out, m_out, v_out):
        gv = g_ref[...]
        m_new = ADAM_B1 * m_ref[...] + (1.0 - ADAM_B1) * gv
        v_new = ADAM_B2 * v_ref[...] + (1.0 - ADAM_B2) * jnp.square(gv)
        m_hat = m_new / (1.0 - ADAM_B1 ** ADAM_STEP)
        v_hat = v_new / (1.0 - ADAM_B2 ** ADAM_STEP)
        d_out[...] = -ADAM_LR * (m_hat / (jnp.sqrt(v_hat) + ADAM_EPS) + ADAM_WD * w_ref[...])
        m_out[...] = m_new
        v_out[...] = v_new

    blk = pl.BlockSpec((tr, cols), lambda i: (i, 0))
    return pl.pallas_call(
        body, name="adamw", grid=(rows // tr,), in_specs=[blk] * 4, out_specs=[blk] * 3,
        out_shape=[jax.ShapeDtypeStruct((rows, cols), F32)] * 3,
        compiler_params=_params(dimension_semantics=("arbitrary",)),
    )(g, w, m, v)


def reduce_gradients(quarters):
    theirs = sibling_swap_halves(quarters)
    c = lax.axis_index("c")
    chip_partials = []
    for n, (q, t) in enumerate(zip(quarters, theirs)):
        four, rh, cols = t.shape
        mine = lax.dynamic_slice_in_dim(q, c * rh, rh, axis=1)
        wire = F32 if n == len(quarters) - 1 else BF16
        chip_partials.append(add_pairs(mine.reshape(four * rh, cols), t.reshape(four * rh, cols), wire)
                             .reshape(four, rh, cols))
    slots = chip_exchange(chip_partials)
    mine = [sum_chips(s) for s in slots]
    return _join_halves(mine, sibling_share(mine, "share_gradient_halves"), axis=0)


def _join_halves(mine, theirs, axis):
    c = lax.axis_index("c")
    return [jnp.concatenate([jnp.where(c == 0, m, t), jnp.where(c == 0, t, m)], axis=axis)
            for m, t in zip(mine, theirs)]


def _pack(pieces, dtype, row_multiple):
    flat = jnp.concatenate([p.astype(dtype).reshape(-1) for p in pieces])
    per = row_multiple * LANES
    padded = -(-flat.shape[0] // per) * per
    flat = jnp.pad(flat, (0, padded - flat.shape[0]))
    return flat.reshape(-1, LANES)


def _unpack(buf, shapes):
    flat = buf.reshape(-1)
    out, off = [], 0
    for s in shapes:
        n = int(np.prod(s))
        out.append(flat[off:off + n].reshape(s))
        off += n
    return out


RET_SCALE = RET_QK_DIM ** -0.5
RET_LOG_GAMMA = [math.log(1.0 - 2.0 ** (-5.0 - h)) for h in range(RET_HEADS)]
RET_HALF = RET_QK_DIM // 2


def _ret_tables(length):
    inv = ROPE_BASE ** (-jnp.arange(RET_HALF, dtype=F32) / RET_HALF)
    ang = jnp.arange(length).astype(F32)[:, None] * inv[None, :]
    log_gamma = jnp.log(1.0 - jnp.power(2.0, -5.0 - jnp.arange(RET_HEADS, dtype=F32)))
    idx = jnp.arange(CHUNK, dtype=F32)
    rel = idx[:, None] - idx[None, :]
    dmask = jnp.where(rel >= 0, jnp.exp(log_gamma[:, None, None] * jnp.maximum(rel, 0.0)), 0.0)
    k_decay = jnp.exp(log_gamma[:, None] * (CHUNK - 1 - idx)[None, :])[:, :, None]
    q_decay = jnp.exp(log_gamma[:, None] * (idx + 1.0)[None, :])[:, :, None]
    return jnp.cos(ang), jnp.sin(ang), dmask, k_decay, q_decay


def _rot(x, cs, sn):
    x1, x2 = x[:, :RET_HALF], x[:, RET_HALF:]
    return jnp.concatenate([x1 * cs - x2 * sn, x1 * sn + x2 * cs], axis=1)


def _unrot(d, cs, sn):
    d1, d2 = d[:, :RET_HALF], d[:, RET_HALF:]
    return jnp.concatenate([d1 * cs + d2 * sn, d2 * cs - d1 * sn], axis=1)


def _sigmoid(x):
    return 1.0 / (1.0 + jnp.exp(-x))


_NT = (((1,), (1,)), ((), ()))
_TN = (((0,), (0,)), ((), ()))


def _ret_specs(nc, rev):
    ch = (lambda c: nc - 1 - c) if rev else (lambda c: c)
    row = lambda w: pl.BlockSpec((CHUNK, w), lambda c: (ch(c), 0))
    const3 = lambda a, b: pl.BlockSpec((RET_HEADS, a, b), lambda c: (0, 0, 0))
    tables = [row(RET_HALF), row(RET_HALF), const3(CHUNK, CHUNK), const3(CHUNK, 1), const3(CHUNK, 1)]
    state = pl.BlockSpec((None, RET_HEADS, RET_QK_DIM, RET_V_DIM), lambda c: (ch(c), 0, 0, 0))
    return row, tables, state


def ret_fwd(p, tables):
    length = p.shape[0]
    nc = length // CHUNK
    row, table_specs, state_spec = _ret_specs(nc, False)

    def body(p_ref, cos_ref, sin_ref, dm_ref, kd_ref, qd_ref, y_ref, ypre_ref, st_ref, r_scr):
        c = pl.program_id(0)

        @pl.when(c == 0)
        def _():
            r_scr[...] = jnp.zeros_like(r_scr)

        cs, sn = cos_ref[...], sin_ref[...]
        valid = (c * CHUNK + lax.broadcasted_iota(jnp.int32, (CHUNK, 1), 0)) >= PAD
        for h in range(RET_HEADS):
            col = lambda part: slice(part * D_MODEL + h * RET_QK_DIM, part * D_MODEL + (h + 1) * RET_QK_DIM)
            qb = _rot(p_ref[:, col(0)], cs, sn).astype(BF16)
            kr = _rot(p_ref[:, col(1)], cs, sn) * RET_SCALE
            kb = kr.astype(BF16)
            vb = jnp.where(valid, p_ref[:, col(2)], 0.0).astype(BF16)
            s = lax.dot_general(qb, kb, _NT, preferred_element_type=F32) * dm_ref[h]
            r = r_scr[h]
            st_ref[h] = r
            y = (jnp.dot(s.astype(BF16), vb, preferred_element_type=F32)
                 + jnp.dot(qb, r.astype(BF16), preferred_element_type=F32) * qd_ref[h])
            kdb = (kr * kd_ref[h]).astype(BF16)
            r_scr[h] = r * math.exp(RET_LOG_GAMMA[h] * CHUNK) + lax.dot_general(kdb, vb, _TN,
                                                                                preferred_element_type=F32)
            out = slice(h * RET_V_DIM, (h + 1) * RET_V_DIM)
            ypre_ref[:, out] = y
            mu = jnp.mean(y, axis=-1, keepdims=True)
            yc = y - mu
            yn = yc * lax.rsqrt(jnp.mean(yc * yc, axis=-1, keepdims=True) + EPS)
            g = p_ref[:, col(3)]
            y_ref[:, out] = (yn * (g * _sigmoid(g))).astype(y_ref.dtype)

    return pl.pallas_call(
        body, name="ret_fwd", grid=(nc,),
        in_specs=[row(4 * D_MODEL)] + table_specs,
        out_specs=[row(D_MODEL), row(D_MODEL), state_spec],
        out_shape=[jax.ShapeDtypeStruct((length, D_MODEL), BF16), jax.ShapeDtypeStruct((length, D_MODEL), F32),
                   jax.ShapeDtypeStruct((nc, RET_HEADS, RET_QK_DIM, RET_V_DIM), F32)],
        scratch_shapes=[pltpu.VMEM((RET_HEADS, RET_QK_DIM, RET_V_DIM), F32)],
        compiler_params=_params(dimension_semantics=("arbitrary",)),
    )(p, *tables)


def ret_bwd(p, tables, ypre, states, dyo):
    length = p.shape[0]
    nc = length // CHUNK
    row, table_specs, state_spec = _ret_specs(nc, True)

    def body(p_ref, cos_ref, sin_ref, dm_ref, kd_ref, qd_ref, ypre_ref, st_ref, dyo_ref, dp_ref, dr_scr):
        c = pl.program_id(0)

        @pl.when(c == 0)
        def _():
            dr_scr[...] = jnp.zeros_like(dr_scr)

        cs, sn = cos_ref[...], sin_ref[...]
        valid = ((nc - 1 - c) * CHUNK + lax.broadcasted_iota(jnp.int32, (CHUNK, 1), 0)) >= PAD
        for h in range(RET_HEADS):
            col = lambda part: slice(part * D_MODEL + h * RET_QK_DIM, part * D_MODEL + (h + 1) * RET_QK_DIM)
            out = slice(h * RET_V_DIM, (h + 1) * RET_V_DIM)
            qb = _rot(p_ref[:, col(0)], cs, sn).astype(BF16)
            kr = _rot(p_ref[:, col(1)], cs, sn) * RET_SCALE
            kb = kr.astype(BF16)
            vb = jnp.where(valid, p_ref[:, col(2)], 0.0).astype(BF16)
            g = p_ref[:, col(3)]
            y = ypre_ref[:, out]
            dyo_h = dyo_ref[:, out]
            mu = jnp.mean(y, axis=-1, keepdims=True)
            yc = y - mu
            rs = lax.rsqrt(jnp.mean(yc * yc, axis=-1, keepdims=True) + EPS)
            xh = yc * rs
            sg = _sigmoid(g)
            dp_ref[:, col(3)] = (dyo_h * xh * (sg * (1.0 + g * (1.0 - sg)))).astype(dp_ref.dtype)
            dyn = dyo_h * (g * sg)
            dy = rs * (dyn - jnp.mean(dyn, axis=-1, keepdims=True)
                       - xh * jnp.mean(dyn * xh, axis=-1, keepdims=True))
            dyb = dy.astype(BF16)
            dm = dm_ref[h]
            sm = (lax.dot_general(qb, kb, _NT, preferred_element_type=F32) * dm).astype(BF16)
            dsb = (lax.dot_general(dyb, vb, _NT, preferred_element_type=F32) * dm).astype(BF16)
            rb = st_ref[h].astype(BF16)
            dyqb = (dy * qd_ref[h]).astype(BF16)
            dr = dr_scr[h]
            drb = dr.astype(BF16)
            kd = kd_ref[h]
            dq = (jnp.dot(dsb, kb, preferred_element_type=F32)
                  + lax.dot_general(dyqb, rb, _NT, preferred_element_type=F32))
            dk = (lax.dot_general(dsb, qb, _TN, preferred_element_type=F32)
                  + lax.dot_general(vb, drb, _NT, preferred_element_type=F32) * kd)
            dv = (lax.dot_general(sm, dyb, _TN, preferred_element_type=F32)
                  + jnp.dot((kr * kd).astype(BF16), drb, preferred_element_type=F32))
            dr_scr[h] = dr * math.exp(RET_LOG_GAMMA[h] * CHUNK) + lax.dot_general(qb, dyqb, _TN,
                                                                                 preferred_element_type=F32)
            dp_ref[:, col(0)] = _unrot(dq, cs, sn).astype(dp_ref.dtype)
            dp_ref[:, col(1)] = (_unrot(dk, cs, sn) * RET_SCALE).astype(dp_ref.dtype)
            dp_ref[:, col(2)] = jnp.where(valid, dv, 0.0).astype(dp_ref.dtype)

    return pl.pallas_call(
        body, name="ret_bwd", grid=(nc,),
        in_specs=[row(4 * D_MODEL)] + table_specs + [row(D_MODEL), state_spec, row(D_MODEL)],
        out_specs=row(4 * D_MODEL),
        out_shape=jax.ShapeDtypeStruct((length, 4 * D_MODEL), BF16),
        scratch_shapes=[pltpu.VMEM((RET_HEADS, RET_QK_DIM, RET_V_DIM), F32)],
        compiler_params=_params(dimension_semantics=("arbitrary",)),
    )(p, *tables, ypre, states, dyo)


def _shift_down(cur, prev, m):
    if m == 0:
        return cur
    rows = lax.broadcasted_iota(jnp.int32, cur.shape, 0)
    return jnp.where(rows < m, pltpu.roll(prev, m, 0), pltpu.roll(cur, m, 0))


def _shift_up(cur, nxt, m):
    if m == 0:
        return cur
    n = cur.shape[0]
    rows = lax.broadcasted_iota(jnp.int32, cur.shape, 0)
    return jnp.where(rows >= n - m, pltpu.roll(nxt, n - m, 0), pltpu.roll(cur, n - m, 0))


def conv_mixer_fwd(p, conv_w):
    length = p.shape[0]
    nc = length // CHUNK
    kt = conv_w.shape[0]

    def body(cur_ref, prev_ref, w_ref, y_ref):
        c = pl.program_id(0)
        rows = lax.broadcasted_iota(jnp.int32, (CHUNK, 1), 0)

        def u_of(ref, blk):
            ok = (blk * CHUNK + rows >= PAD) & (blk >= 0)
            return jnp.where(ok, ref[:, D_MODEL:2 * D_MODEL] * ref[:, 2 * D_MODEL:], 0.0)

        u_cur = u_of(cur_ref, c)
        u_prev = u_of(prev_ref, c - 1)
        acc = jnp.zeros((CHUNK, D_MODEL), F32)
        for i in range(kt):
            acc = acc + _shift_down(u_cur, u_prev, kt - 1 - i) * w_ref[i:i + 1, :]
        y_ref[...] = (cur_ref[:, :D_MODEL] * acc).astype(y_ref.dtype)

    return pl.pallas_call(
        body, name="conv_mixer_fwd", grid=(nc,),
        in_specs=[pl.BlockSpec((CHUNK, 3 * D_MODEL), lambda c: (c, 0)),
                  pl.BlockSpec((CHUNK, 3 * D_MODEL), lambda c: (jnp.maximum(c - 1, 0), 0)),
                  pl.BlockSpec((kt, D_MODEL), lambda c: (0, 0))],
        out_specs=pl.BlockSpec((CHUNK, D_MODEL), lambda c: (c, 0)),
        out_shape=jax.ShapeDtypeStruct((length, D_MODEL), BF16),
        compiler_params=_params(dimension_semantics=("arbitrary",)),
    )(p, p, conv_w)


def conv_mixer_bwd(p, conv_w, dy):
    length = p.shape[0]
    nc = length // CHUNK
    kt = conv_w.shape[0]

    def body(cur_ref, prev_ref, w_ref, dy_ref, dyn_ref, pn_ref, dp_ref, dw_ref):
        c = pl.program_id(0)
        rows = lax.broadcasted_iota(jnp.int32, (CHUNK, 1), 0)

        def u_of(ref, blk):
            ok = (blk * CHUNK + rows >= PAD) & (blk >= 0)
            return jnp.where(ok, ref[:, D_MODEL:2 * D_MODEL] * ref[:, 2 * D_MODEL:], 0.0)

        u_cur = u_of(cur_ref, c)
        u_prev = u_of(prev_ref, c - 1)
        b_gate = cur_ref[:, :D_MODEL]
        dyv = dy_ref[...]
        dconv = dyv * b_gate
        dconv_next = jnp.where(c + 1 < nc, dyn_ref[...] * pn_ref[:, :D_MODEL], 0.0)

        @pl.when(c == 0)
        def _():
            dw_ref[...] = jnp.zeros_like(dw_ref)

        acc = jnp.zeros((CHUNK, D_MODEL), F32)
        du = jnp.zeros((CHUNK, D_MODEL), F32)
        for i in range(kt):
            shifted = _shift_down(u_cur, u_prev, kt - 1 - i)
            acc = acc + shifted * w_ref[i:i + 1, :]
            dw_ref[i:i + 1, :] += jnp.sum(dconv * shifted, axis=0, keepdims=True)
            du = du + _shift_up(dconv, dconv_next, kt - 1 - i) * w_ref[i:i + 1, :]
        du = jnp.where(c * CHUNK + rows >= PAD, du, 0.0)
        dp_ref[:, :D_MODEL] = (dyv * acc).astype(dp_ref.dtype)
        dp_ref[:, D_MODEL:2 * D_MODEL] = (du * cur_ref[:, 2 * D_MODEL:]).astype(dp_ref.dtype)
        dp_ref[:, 2 * D_MODEL:] = (du * cur_ref[:, D_MODEL:2 * D_MODEL]).astype(dp_ref.dtype)

    nxt = lambda c: (jnp.minimum(c + 1, nc - 1), 0)
    return pl.pallas_call(
        body, name="conv_mixer_bwd", grid=(nc,),
        in_specs=[pl.BlockSpec((CHUNK, 3 * D_MODEL), lambda c: (c, 0)),
                  pl.BlockSpec((CHUNK, 3 * D_MODEL), lambda c: (jnp.maximum(c - 1, 0), 0)),
                  pl.BlockSpec((kt, D_MODEL), lambda c: (0, 0)),
                  pl.BlockSpec((CHUNK, D_MODEL), lambda c: (c, 0)),
                  pl.BlockSpec((CHUNK, D_MODEL), nxt),
                  pl.BlockSpec((CHUNK, 3 * D_MODEL), nxt)],
        out_specs=[pl.BlockSpec((CHUNK, 3 * D_MODEL), lambda c: (c, 0)),
                   pl.BlockSpec((kt, D_MODEL), lambda c: (0, 0))],
        out_shape=[jax.ShapeDtypeStruct((length, 3 * D_MODEL), BF16), jax.ShapeDtypeStruct((kt, D_MODEL), F32)],
        compiler_params=_params(dimension_semantics=("arbitrary",)),
    )(p, p, conv_w, dy, dy, p)


def merge_fwd(gate_logits, ups):
    length = gate_logits.shape[0]
    tm = _tile(length, (384, 256, 128))

    def body(g_ref, u0, u1, u2, u3, o_ref):
        acc = jnp.zeros((tm, D_MODEL), F32)
        for n, u in enumerate((u0, u1, u2, u3)):
            acc = acc + _sigmoid(g_ref[:, n * D_MODEL:(n + 1) * D_MODEL]) * u[...]
        o_ref[...] = acc.astype(o_ref.dtype)

    row = pl.BlockSpec((tm, D_MODEL), lambda i: (i, 0))
    return pl.pallas_call(
        body, name="merge_fwd", grid=(length // tm,),
        in_specs=[pl.BlockSpec((tm, N_BRANCH * D_MODEL), lambda i: (i, 0))] + [row] * N_BRANCH,
        out_specs=row, out_shape=jax.ShapeDtypeStruct((length, D_MODEL), BF16),
        compiler_params=_params(dimension_semantics=("arbitrary",)),
    )(gate_logits, *ups)


def merge_bwd(gate_logits, ups, dmerged):
    length = gate_logits.shape[0]
    tm = _tile(length, (384, 256, 128))

    def body(g_ref, u0, u1, u2, u3, dm_ref, dg_ref, d0, d1, d2, d3):
        dm = dm_ref[...]
        for n, (u, du) in enumerate(((u0, d0), (u1, d1), (u2, d2), (u3, d3))):
            cols = slice(n * D_MODEL, (n + 1) * D_MODEL)
            s = _sigmoid(g_ref[:, cols])
            du[...] = (dm * s).astype(du.dtype)
            dg_ref[:, cols] = (dm * u[...] * (s * (1.0 - s))).astype(dg_ref.dtype)

    row = pl.BlockSpec((tm, D_MODEL), lambda i: (i, 0))
    wide = pl.BlockSpec((tm, N_BRANCH * D_MODEL), lambda i: (i, 0))
    outs = pl.pallas_call(
        body, name="merge_bwd", grid=(length // tm,),
        in_specs=[wide] + [row] * (N_BRANCH + 1),
        out_specs=[wide] + [row] * N_BRANCH,
        out_shape=[jax.ShapeDtypeStruct((length, N_BRANCH * D_MODEL), BF16)]
        + [jax.ShapeDtypeStruct((length, D_MODEL), BF16)] * N_BRANCH,
        compiler_params=_params(dimension_semantics=("arbitrary",)),
    )(gate_logits, *ups, dmerged)
    return outs[0], list(outs[1:])


def swiglu_fwd(f):
    length = f.shape[0]
    tm = _tile(length, (384, 256, 128))

    def body(f_ref, o_ref):
        a = f_ref[:, :D_FF]
        o_ref[...] = (a * _sigmoid(a) * f_ref[:, D_FF:]).astype(o_ref.dtype)

    return pl.pallas_call(
        body, name="swiglu_fwd", grid=(length // tm,),
        in_specs=[pl.BlockSpec((tm, 2 * D_FF), lambda i: (i, 0))],
        out_specs=pl.BlockSpec((tm, D_FF), lambda i: (i, 0)),
        out_shape=jax.ShapeDtypeStruct((length, D_FF), BF16),
        compiler_params=_params(dimension_semantics=("arbitrary",)),
    )(f)


def swiglu_bwd(f, dact):
    length = f.shape[0]
    tm = _tile(length, (384, 256, 128))

    def body(f_ref, d_ref, df_ref):
        a = f_ref[:, :D_FF]
        up = f_ref[:, D_FF:]
        d = d_ref[...]
        s = _sigmoid(a)
        df_ref[:, :D_FF] = (d * up * (s * (1.0 + a * (1.0 - s)))).astype(df_ref.dtype)
        df_ref[:, D_FF:] = (d * (a * s)).astype(df_ref.dtype)

    return pl.pallas_call(
        body, name="swiglu_bwd", grid=(length // tm,),
        in_specs=[pl.BlockSpec((tm, 2 * D_FF), lambda i: (i, 0)), pl.BlockSpec((tm, D_FF), lambda i: (i, 0))],
        out_specs=pl.BlockSpec((tm, 2 * D_FF), lambda i: (i, 0)),
        out_shape=jax.ShapeDtypeStruct((length, 2 * D_FF), BF16),
        compiler_params=_params(dimension_semantics=("arbitrary",)),
    )(f, dact)


SSD_PAIRS = SSD_HEADS // 2
SSD_XBC = SSD_CONV_DIM
SSD_GW = SSD_INNER // SSD_GROUPS


def _split3(x):
    h1 = x.astype(BF16)
    r1 = x - h1.astype(F32)
    h2 = r1.astype(BF16)
    h3 = (r1 - h2.astype(F32)).astype(BF16)
    return h1, h2, h3


def _tri_apply(tri, x, dims):
    out = None
    for part in _split3(x):
        t = lax.dot_general(tri, part, dims, preferred_element_type=F32)
        out = t if out is None else out + t
    return out


def _softplus(x):
    return jnp.maximum(x, 0.0) + jnp.log(1.0 + jnp.exp(-jnp.abs(x)))


def _lane_pair(x, pair):
    lanes = lax.broadcasted_iota(jnp.int32, (x.shape[0], LANES), 1)
    return jnp.where(lanes < SSD_HEAD_DIM, x[:, 2 * pair:2 * pair + 1], x[:, 2 * pair + 1:2 * pair + 2])


def _half_sums(t):
    lanes = lax.broadcasted_iota(jnp.int32, t.shape, 1)
    lo = jnp.sum(jnp.where(lanes < SSD_HEAD_DIM, t, 0.0), axis=1, keepdims=True)
    return lo, jnp.sum(t, axis=1, keepdims=True) - lo


def _put_cols(cols):
    rows = cols[0].shape[0]
    lanes = lax.broadcasted_iota(jnp.int32, (rows, LANES), 1)
    out = jnp.zeros((rows, LANES), F32)
    for h, col in enumerate(cols):
        out = out + jnp.where(lanes == h, col, 0.0)
    return out


def ssd_pre_fwd(p, dt_raw, conv_w, conv_b, dt_bias, a_log):
    length = p.shape[0]
    nc = length // CHUNK
    kt = conv_w.shape[0]

    def body(cur_ref, prev_ref, raw_ref, w_ref, b_ref, bias_ref, alog_ref, act_ref, dt_ref, a_ref):
        c = pl.program_id(0)
        rows = lax.broadcasted_iota(jnp.int32, (CHUNK, 1), 0)
        vm = c * CHUNK + rows >= PAD
        u_cur = jnp.where(vm, cur_ref[:, SSD_INNER:], 0.0)
        u_prev = jnp.where(((c - 1) * CHUNK + rows >= PAD) & (c >= 1), prev_ref[:, SSD_INNER:], 0.0)
        pre = jnp.zeros((CHUNK, SSD_XBC), F32) + b_ref[...]
        for i in range(kt):
            pre = pre + _shift_down(u_cur, u_prev, kt - 1 - i) * w_ref[i:i + 1, :]
        act = pre * _sigmoid(pre)
        act_ref[:, :SSD_INNER] = jnp.where(vm, act[:, :SSD_INNER], 0.0)
        act_ref[:, SSD_INNER:] = act[:, SSD_INNER:]
        dt = _softplus(raw_ref[...] + bias_ref[...])
        dt_ref[...] = dt
        a_ref[...] = -jnp.exp(alog_ref[...]) * dt

    row = lambda w: pl.BlockSpec((CHUNK, w), lambda c: (c, 0))
    vec = lambda w: pl.BlockSpec((1, w), lambda c: (0, 0))
    return pl.pallas_call(
        body, name="ssd_pre_fwd", grid=(nc,),
        in_specs=[row(3 * D_MODEL), pl.BlockSpec((CHUNK, 3 * D_MODEL), lambda c: (jnp.maximum(c - 1, 0), 0)),
                  row(LANES), pl.BlockSpec((kt, SSD_XBC), lambda c: (0, 0)), vec(SSD_XBC), vec(LANES), vec(LANES)],
        out_specs=[row(SSD_XBC), row(LANES), row(LANES)],
        out_shape=[jax.ShapeDtypeStruct((length, SSD_XBC), F32), jax.ShapeDtypeStruct((length, LANES), F32),
                   jax.ShapeDtypeStruct((length, LANES), F32)],
        compiler_params=_params(dimension_semantics=("arbitrary",)),
    )(p, p, dt_raw, conv_w, conv_b, dt_bias, a_log)


def ssd_pre_bwd(p, dt_raw, conv_w, conv_b, dt_bias, a_log, dact, ddt, da, dz):
    length = p.shape[0]
    nc = length // CHUNK
    kt = conv_w.shape[0]

    def body(cur_ref, prev_ref, raw_ref, w_ref, b_ref, bias_ref, alog_ref, dact_ref, ddt_ref, da_ref, dz_ref,
             dp_ref, draw_ref, dw_ref, db_ref, dbias_ref, dalog_ref, dpre_next):
        step = pl.program_id(0)
        c = nc - 1 - step
        rows = lax.broadcasted_iota(jnp.int32, (CHUNK, 1), 0)
        vm = c * CHUNK + rows >= PAD

        @pl.when(step == 0)
        def _():
            dpre_next[...] = jnp.zeros_like(dpre_next)
            dw_ref[...] = jnp.zeros_like(dw_ref)
            db_ref[...] = jnp.zeros_like(db_ref)
            dbias_ref[...] = jnp.zeros_like(dbias_ref)
            dalog_ref[...] = jnp.zeros_like(dalog_ref)

        u_cur = jnp.where(vm, cur_ref[:, SSD_INNER:], 0.0)
        u_prev = jnp.where(((c - 1) * CHUNK + rows >= PAD) & (c >= 1), prev_ref[:, SSD_INNER:], 0.0)
        shifted = [_shift_down(u_cur, u_prev, kt - 1 - i) for i in range(kt)]
        pre = jnp.zeros((CHUNK, SSD_XBC), F32) + b_ref[...]
        for i in range(kt):
            pre = pre + shifted[i] * w_ref[i:i + 1, :]
        sg = _sigmoid(pre)
        lanes = lax.broadcasted_iota(jnp.int32, (CHUNK, SSD_XBC), 1)
        dact_v = jnp.where(vm | (lanes >= SSD_INNER), dact_ref[...], 0.0)
        dpre = dact_v * (sg * (1.0 + pre * (1.0 - sg)))
        db_ref[...] += jnp.sum(dpre, axis=0, keepdims=True)
        nxt = dpre_next[...]
        du = jnp.zeros((CHUNK, SSD_XBC), F32)
        for i in range(kt):
            dw_ref[i:i + 1, :] += jnp.sum(dpre * shifted[i], axis=0, keepdims=True)
            du = du + _shift_up(dpre, nxt, kt - 1 - i) * w_ref[i:i + 1, :]
        dpre_next[...] = dpre
        dp_ref[:, :SSD_INNER] = dz_ref[...].astype(dp_ref.dtype)
        dp_ref[:, SSD_INNER:] = jnp.where(vm, du, 0.0).astype(dp_ref.dtype)
        x = raw_ref[...] + bias_ref[...]
        neg_exp = -jnp.exp(alog_ref[...])
        dav = da_ref[...]
        draw = (ddt_ref[...] + dav * neg_exp) * _sigmoid(x)
        draw_ref[...] = draw.astype(draw_ref.dtype)
        dbias_ref[...] += jnp.sum(draw, axis=0, keepdims=True)
        dalog_ref[...] += jnp.sum(dav * (neg_exp * _softplus(x)), axis=0, keepdims=True)

    rev = lambda c: (nc - 1 - c, 0)
    row = lambda w: pl.BlockSpec((CHUNK, w), rev)
    vec = lambda w: pl.BlockSpec((1, w), lambda c: (0, 0))
    taps = pl.BlockSpec((kt, SSD_XBC), lambda c: (0, 0))
    return pl.pallas_call(
        body, name="ssd_pre_bwd", grid=(nc,),
        in_specs=[row(3 * D_MODEL),
                  pl.BlockSpec((CHUNK, 3 * D_MODEL), lambda c: (jnp.maximum(nc - 2 - c, 0), 0)),
                  row(LANES), taps, vec(SSD_XBC), vec(LANES), vec(LANES),
                  row(SSD_XBC), row(LANES), row(LANES), row(SSD_INNER)],
        out_specs=[row(3 * D_MODEL), row(LANES), taps, vec(SSD_XBC), vec(LANES), vec(LANES)],
        out_shape=[jax.ShapeDtypeStruct((length, 3 * D_MODEL), BF16), jax.ShapeDtypeStruct((length, LANES), BF16),
                   jax.ShapeDtypeStruct((kt, SSD_XBC), F32), jax.ShapeDtypeStruct((1, SSD_XBC), F32),
                   jax.ShapeDtypeStruct((1, LANES), F32), jax.ShapeDtypeStruct((1, LANES), F32)],
        scratch_shapes=[pltpu.VMEM((CHUNK, SSD_XBC), F32)],
        compiler_params=_params(dimension_semantics=("arbitrary",)),
    )(p, p, dt_raw, conv_w, conv_b, dt_bias, a_log, dact, ddt, da, dz)


def _tri_apply_lhs_t(x, tri):
    out = None
    for part in _split3(x):
        t = lax.dot_general(part, tri, (((0,), (1,)), ((), ())), preferred_element_type=F32)
        out = t if out is None else out + t
    return out


def ssd_core_fwd(act, dt, a, d_skip):
    length = act.shape[0]
    nc = length // CHUNK

    def body(act_ref, dt_ref, a_ref, dskip_ref, y_ref, st_ref, h_scr):
        c = pl.program_id(0)

        @pl.when(c == 0)
        def _():
            h_scr[...] = jnp.zeros_like(h_scr)

        r = lax.broadcasted_iota(jnp.int32, (CHUNK, CHUNK), 0)
        s = lax.broadcasted_iota(jnp.int32, (CHUNK, CHUNK), 1)
        causal = r >= s
        incl = jnp.where(causal, 1.0, 0.0).astype(BF16)
        a_v = a_ref[...]
        acs = _tri_apply(incl, a_v, (((1,), (0,)), ((), ())))
        acs_t = _tri_apply_lhs_t(a_v, incl)
        dt_v = dt_ref[...]
        lanes = lax.broadcasted_iota(jnp.int32, (CHUNK, LANES), 1)
        low = lanes < SSD_HEAD_DIM
        for g in range(SSD_GROUPS):
            bg = act_ref[:, SSD_INNER + g * SSD_STATE:SSD_INNER + (g + 1) * SSD_STATE].astype(BF16)
            cg = act_ref[:, SSD_INNER + (SSD_GROUPS + g) * SSD_STATE:
                         SSD_INNER + (SSD_GROUPS + g + 1) * SSD_STATE].astype(BF16)
            cb = lax.dot_general(cg, bg, _NT, preferred_element_type=F32)
            for pair in (2 * g, 2 * g + 1):
                cols = slice(pair * LANES, (pair + 1) * LANES)
                xs = act_ref[:, cols]
                x = xs * _lane_pair(dt_v, pair)
                ydiag = jnp.zeros((CHUNK, LANES), F32)
                for k, keep in ((0, low), (1, ~low)):
                    h = 2 * pair + k
                    seg = jnp.where(causal, jnp.exp(acs[:, h:h + 1] - acs_t[h:h + 1, :]), 0.0)
                    ydiag = ydiag + jnp.dot((cb * seg).astype(BF16), jnp.where(keep, x, 0.0).astype(BF16),
                                            preferred_element_type=F32)
                acs_p = _lane_pair(acs, pair)
                last = acs_p[CHUNK - 1:CHUNK, :]
                xds = (x * jnp.exp(last - acs_p)).astype(BF16)
                hprev = h_scr[pair]
                st_ref[pair] = hprev
                yoff = lax.dot_general(cg, hprev.astype(BF16), _NT, preferred_element_type=F32) * jnp.exp(acs_p)
                prow = lax.broadcasted_iota(jnp.int32, (LANES, 1), 0)
                cd = jnp.where(prow < SSD_HEAD_DIM, jnp.exp(acs_t[2 * pair:2 * pair + 1, CHUNK - 1:CHUNK]),
                               jnp.exp(acs_t[2 * pair + 1:2 * pair + 2, CHUNK - 1:CHUNK]))
                h_scr[pair] = hprev * cd + lax.dot_general(xds, bg, _TN, preferred_element_type=F32)
                y_ref[:, cols] = ydiag + yoff + xs * dskip_ref[:, cols]

    row = lambda w: pl.BlockSpec((CHUNK, w), lambda c: (c, 0))
    return pl.pallas_call(
        body, name="ssd_core_fwd", grid=(nc,),
        in_specs=[row(SSD_XBC), row(LANES), row(LANES), pl.BlockSpec((1, SSD_INNER), lambda c: (0, 0))],
        out_specs=[row(SSD_INNER), pl.BlockSpec((None, SSD_PAIRS, LANES, SSD_STATE), lambda c: (c, 0, 0, 0))],
        out_shape=[jax.ShapeDtypeStruct((length, SSD_INNER), F32),
                   jax.ShapeDtypeStruct((nc, SSD_PAIRS, LANES, SSD_STATE), F32)],
        scratch_shapes=[pltpu.VMEM((SSD_PAIRS, LANES, SSD_STATE), F32)],
        compiler_params=_params(dimension_semantics=("arbitrary",)),
    )(act, dt, a, d_skip)


def ssd_core_bwd(act, dt, a, d_skip, states, dy):
    length = act.shape[0]
    nc = length // CHUNK

    def body(act_ref, dt_ref, a_ref, dskip_ref, st_ref, dy_ref, dact_ref, ddt_ref, da_ref, dds_ref, dh_scr):
        step = pl.program_id(0)

        @pl.when(step == 0)
        def _():
            dh_scr[...] = jnp.zeros_like(dh_scr)
            dds_ref[...] = jnp.zeros_like(dds_ref)

        r = lax.broadcasted_iota(jnp.int32, (CHUNK, CHUNK), 0)
        s = lax.broadcasted_iota(jnp.int32, (CHUNK, CHUNK), 1)
        causal = r >= s
        incl = jnp.where(causal, 1.0, 0.0).astype(BF16)
        a_v = a_ref[...]
        acs = _tri_apply(incl, a_v, (((1,), (0,)), ((), ())))
        acs_t = _tri_apply_lhs_t(a_v, incl)
        dt_v = dt_ref[...]
        lanes = lax.broadcasted_iota(jnp.int32, (CHUNK, LANES), 1)
        low = lanes < SSD_HEAD_DIM
        prow = lax.broadcasted_iota(jnp.int32, (LANES, 1), 0)
        is_last = lax.broadcasted_iota(jnp.int32, (CHUNK, 1), 0) == CHUNK - 1
        dacs_cols = [None] * SSD_HEADS
        dacs_rows = [None] * SSD_HEADS
        ddt_cols = [None] * SSD_HEADS
        for g in range(SSD_GROUPS):
            b_cols = slice(SSD_INNER + g * SSD_STATE, SSD_INNER + (g + 1) * SSD_STATE)
            c_cols = slice(SSD_INNER + (SSD_GROUPS + g) * SSD_STATE, SSD_INNER + (SSD_GROUPS + g + 1) * SSD_STATE)
            bg = act_ref[:, b_cols].astype(BF16)
            cg = act_ref[:, c_cols].astype(BF16)
            cb = lax.dot_general(cg, bg, _NT, preferred_element_type=F32)
            dcb = jnp.zeros((CHUNK, CHUNK), F32)
            dbg = jnp.zeros((CHUNK, SSD_STATE), F32)
            dcg = jnp.zeros((CHUNK, SSD_STATE), F32)
            for pair in (2 * g, 2 * g + 1):
                cols = slice(pair * LANES, (pair + 1) * LANES)
                xs = act_ref[:, cols]
                dtp = _lane_pair(dt_v, pair)
                x = xs * dtp
                xb = x.astype(BF16)
                dyv = dy_ref[:, cols]
                dyb = dyv.astype(BF16)
                dds_ref[:, cols] += jnp.sum(dyv * xs, axis=0, keepdims=True)
                acs_p = _lane_pair(acs, pair)
                last = acs_p[CHUNK - 1:CHUNK, :]
                ds = jnp.exp(last - acs_p)
                ea = jnp.exp(acs_p)
                hprev = st_ref[pair]
                hb = hprev.astype(BF16)
                dh = dh_scr[pair]
                dhb = dh.astype(BF16)
                dx = jnp.zeros((CHUNK, LANES), F32)
                for k, keep in ((0, low), (1, ~low)):
                    h = 2 * pair + k
                    seg = jnp.where(causal, jnp.exp(acs[:, h:h + 1] - acs_t[h:h + 1, :]), 0.0)
                    lmat = cb * seg
                    dl = lax.dot_general(jnp.where(keep, dyv, 0.0).astype(BF16), xb, _NT,
                                         preferred_element_type=F32)
                    dcb = dcb + dl * seg
                    t = dl * lmat
                    dacs_cols[h] = jnp.sum(t, axis=1, keepdims=True)
                    dacs_rows[h] = jnp.sum(t, axis=0, keepdims=True)
                    dx = dx + jnp.where(keep, lax.dot_general(lmat.astype(BF16), dyb, _TN,
                                                              preferred_element_type=F32), 0.0)
                yoff = lax.dot_general(cg, hb, _NT, preferred_element_type=F32) * ea
                dm = (dyv * ea).astype(BF16)
                dcg = dcg + jnp.dot(dm, hb, preferred_element_type=F32)
                dxds = lax.dot_general(bg, dhb, _NT, preferred_element_type=F32)
                xds = x * ds
                dbg = dbg + jnp.dot(xds.astype(BF16), dhb, preferred_element_type=F32)
                dx = dx + dxds * ds
                t_ds = dxds * xds
                e_a = jnp.exp(acs_t[2 * pair:2 * pair + 1, CHUNK - 1:CHUNK])
                e_b = jnp.exp(acs_t[2 * pair + 1:2 * pair + 2, CHUNK - 1:CHUNK])
                cd = jnp.where(prow < SSD_HEAD_DIM, e_a, e_b)
                hd = dh * hprev
                dcd_a = jnp.sum(jnp.where(prow < SSD_HEAD_DIM, hd, 0.0), keepdims=True)
                dcd_b = jnp.sum(hd, keepdims=True) - dcd_a
                dh_scr[pair] = dh * cd + lax.dot_general(dm, cg, _TN, preferred_element_type=F32)
                col_lo, col_hi = _half_sums(dyv * yoff - t_ds)
                tot_lo, tot_hi = _half_sums(jnp.sum(t_ds, axis=0, keepdims=True))
                dacs_cols[2 * pair] += col_lo + jnp.where(is_last, tot_lo + dcd_a.reshape(1, 1) * e_a, 0.0)
                dacs_cols[2 * pair + 1] += col_hi + jnp.where(is_last, tot_hi + dcd_b.reshape(1, 1) * e_b, 0.0)
                dact_ref[:, cols] = dyv * dskip_ref[:, cols] + dx * dtp
                ddt_cols[2 * pair], ddt_cols[2 * pair + 1] = _half_sums(dx * xs)
            dcbb = dcb.astype(BF16)
            dact_ref[:, b_cols] = dbg + lax.dot_general(dcbb, cg, _TN, preferred_element_type=F32)
            dact_ref[:, c_cols] = dcg + jnp.dot(dcbb, bg, preferred_element_type=F32)
        ddt_ref[...] = _put_cols(ddt_cols)
        sub = lax.broadcasted_iota(jnp.int32, (LANES, CHUNK), 0)
        rows_mat = jnp.zeros((LANES, CHUNK), F32)
        for h in range(SSD_HEADS):
            rows_mat = rows_mat + jnp.where(sub == h, dacs_rows[h], 0.0)
        dacs = _put_cols(dacs_cols) - rows_mat.T
        da_ref[...] = _tri_apply(incl, dacs, (((0,), (0,)), ((), ())))

    rev = lambda c: (nc - 1 - c, 0)
    row = lambda w: pl.BlockSpec((CHUNK, w), rev)
    lane_vec = pl.BlockSpec((1, SSD_INNER), lambda c: (0, 0))
    return pl.pallas_call(
        body, name="ssd_core_bwd", grid=(nc,),
        in_specs=[row(SSD_XBC), row(LANES), row(LANES), lane_vec,
                  pl.BlockSpec((None, SSD_PAIRS, LANES, SSD_STATE), lambda c: (nc - 1 - c, 0, 0, 0)),
                  row(SSD_INNER)],
        out_specs=[row(SSD_XBC), row(LANES), row(LANES), lane_vec],
        out_shape=[jax.ShapeDtypeStruct((length, SSD_XBC), F32), jax.ShapeDtypeStruct((length, LANES), F32),
                   jax.ShapeDtypeStruct((length, LANES), F32), jax.ShapeDtypeStruct((1, SSD_INNER), F32)],
        scratch_shapes=[pltpu.VMEM((SSD_PAIRS, LANES, SSD_STATE), F32)],
        compiler_params=_params(dimension_semantics=("arbitrary",)),
    )(act, dt, a, d_skip, states, dy)


def ssd_post_fwd(y, p, norm_w):
    length = y.shape[0]
    tm = _tile(length, (384, 256, 128))

    def body(y_ref, p_ref, w_ref, o_ref):
        for g in range(SSD_GROUPS):
            cols = slice(g * SSD_GW, (g + 1) * SSD_GW)
            z = p_ref[:, cols]
            v = y_ref[:, cols] * (z * _sigmoid(z))
            o_ref[:, cols] = (v * lax.rsqrt(jnp.mean(v * v, axis=-1, keepdims=True) + EPS)
                              * w_ref[:, cols]).astype(o_ref.dtype)

    return pl.pallas_call(
        body, name="ssd_post_fwd", grid=(length // tm,),
        in_specs=[pl.BlockSpec((tm, SSD_INNER), lambda i: (i, 0)), pl.BlockSpec((tm, SSD_INNER), lambda i: (i, 0)),
                  pl.BlockSpec((1, SSD_INNER), lambda i: (0, 0))],
        out_specs=pl.BlockSpec((tm, SSD_INNER), lambda i: (i, 0)),
        out_shape=jax.ShapeDtypeStruct((length, SSD_INNER), BF16),
        compiler_params=_params(dimension_semantics=("arbitrary",)),
    )(y, p, norm_w)


def ssd_post_bwd(y, p, norm_w, dout):
    length = y.shape[0]
    tm = _tile(length, (384, 256, 128))

    def body(y_ref, p_ref, w_ref, do_ref, dy_ref, dz_ref, dw_ref):
        @pl.when(pl.program_id(0) == 0)
        def _():
            dw_ref[...] = jnp.zeros_like(dw_ref)

        for g in range(SSD_GROUPS):
            cols = slice(g * SSD_GW, (g + 1) * SSD_GW)
            z = p_ref[:, cols]
            yv = y_ref[:, cols]
            sg = _sigmoid(z)
            v = yv * (z * sg)
            rs = lax.rsqrt(jnp.mean(v * v, axis=-1, keepdims=True) + EPS)
            vh = v * rs
            do = do_ref[:, cols]
            dw_ref[:, cols] += jnp.sum(do * vh, axis=0, keepdims=True)
            dvh = do * w_ref[:, cols]
            dv = rs * (dvh - vh * jnp.mean(dvh * vh, axis=-1, keepdims=True))
            dy_ref[:, cols] = dv * (z * sg)
            dz_ref[:, cols] = dv * yv * (sg * (1.0 + z * (1.0 - sg)))

    blk = pl.BlockSpec((tm, SSD_INNER), lambda i: (i, 0))
    vec = pl.BlockSpec((1, SSD_INNER), lambda i: (0, 0))
    return pl.pallas_call(
        body, name="ssd_post_bwd", grid=(length // tm,),
        in_specs=[blk, blk, vec, blk], out_specs=[blk, blk, vec],
        out_shape=[jax.ShapeDtypeStruct((length, SSD_INNER), F32), jax.ShapeDtypeStruct((length, SSD_INNER), F32),
                   jax.ShapeDtypeStruct((1, SSD_INNER), F32)],
        compiler_params=_params(dimension_semantics=("arbitrary",)),
    )(y, p, norm_w, dout)


def _ssd_rows(lw):
    pad = lambda v: jnp.pad(v, (0, LANES - SSD_HEADS))[None]
    return dict(conv_w=lw['ssd_conv_w'], conv_b=lw['ssd_conv_b'][None], dt_bias=pad(lw['ssd_dt_bias']),
                a_log=pad(lw['ssd_a_log']), d_skip=jnp.repeat(lw['ssd_d'], SSD_HEAD_DIM)[None],
                norm_w=lw['ssd_norm'][None])


def ssd_fwd(p, dt_raw, rows):
    act, dt, a = ssd_pre_fwd(p, dt_raw, rows['conv_w'], rows['conv_b'], rows['dt_bias'], rows['a_log'])
    y, states = ssd_core_fwd(act, dt, a, rows['d_skip'])
    return ssd_post_fwd(y, p, rows['norm_w']), (act, dt, a, y, states)


def ssd_bwd(p, dt_raw, rows, saved, dout):
    act, dt, a, y, states = saved
    dy, dz, dnorm = ssd_post_bwd(y, p, rows['norm_w'], dout)
    dact, ddt, da, dskip_lanes = ssd_core_bwd(act, dt, a, rows['d_skip'], states, dy)
    dp, draw, dconv_w, dconv_b, dbias, dalog = ssd_pre_bwd(
        p, dt_raw, rows['conv_w'], rows['conv_b'], rows['dt_bias'], rows['a_log'], dact, ddt, da, dz)
    grads = dict(ssd_conv_w=dconv_w, ssd_conv_b=dconv_b[0], ssd_dt_bias=dbias[0, :SSD_HEADS],
                 ssd_a_log=dalog[0, :SSD_HEADS], ssd_norm=dnorm[0],
                 ssd_d=jnp.sum(dskip_lanes.reshape(SSD_HEADS, SSD_HEAD_DIM), axis=1))
    return dp, draw, grads


IN_A = (0, 3 * D_MODEL)
IN_S = (IN_A[1], IN_A[1] + SSD_INNER + SSD_CONV_DIM)
IN_DT = (IN_S[1], IN_S[1] + SSD_HEADS)
IN_R = (IN_DT[1], IN_DT[1] + 4 * D_MODEL)
IN_SB = (IN_R[1], IN_R[1] + 3 * D_MODEL)
IN_G = (IN_SB[1], IN_SB[1] + N_BRANCH * D_MODEL)
IN_WIDTH = IN_G[1]


def _layer_weights(full, small, l):
    w_in = full['w_in'][l]
    cut = lambda r: w_in[:, r[0]:r[1]]
    w_dt = jnp.pad(cut(IN_DT), ((0, 0), (0, DT_PAD - SSD_HEADS)))
    return dict(
        w_a=cut(IN_A), w_s=cut(IN_S), w_dt=w_dt, w_r=cut(IN_R), w_sb=cut(IN_SB), w_g=cut(IN_G),
        w_branch=[full['w_branch'][l, n] for n in range(N_BRANCH)],
        w_out=full['w_out'][l], w_ffn_in=full['w_ffn_in'][l], w_ffn_out=full['w_ffn_out'][l],
        conv_a=full['conv_a'][l], ssd_conv_w=full['ssd_conv_w'][l],
        ssd_conv_b=small['ssd_conv_b'][l], ssd_dt_bias=small['ssd_dt_bias'][l], ssd_a_log=small['ssd_a_log'][l],
        ssd_d=small['ssd_d'][l], ssd_norm=small['ssd_norm'][l],
        n_mix_pre=small['norm_mix_pre'][l][None], n_mix_post=small['norm_mix_post'][l][None],
        n_ffn_pre=small['norm_ffn_pre'][l][None], n_ffn_post=small['norm_ffn_post'][l][None],
    )


def _layer_fwd(h_res, lw, ret_tables):
    s = {'h_res': h_res, 'ret_tables': ret_tables}
    hn = rms_fwd(h_res, lw['n_mix_pre'], out_dtype=BF16, name="rms_mix_pre")
    s['hn'] = hn
    p_a = mm_nn(hn, lw['w_a'], name="proj_conv")
    p_s = mm_nn(hn, lw['w_s'], name="proj_ssd")
    p_dt = mm_nn(hn, lw['w_dt'], name="proj_dt")
    p_r = mm_nn(hn, lw['w_r'], name="proj_ret")
    p_sb = mm_nn(hn, lw['w_sb'], out_dtype=BF16, name="proj_sb")
    p_g = mm_nn(hn, lw['w_g'], name="proj_gate")
    y_a = conv_mixer_fwd(p_a, lw['conv_a'])
    s['p_a'] = p_a
    s['ssd_rows'] = _ssd_rows(lw)
    y_b, s['ssd_saved'] = ssd_fwd(p_s, p_dt, s['ssd_rows'])
    s['p_s'], s['p_dt'] = p_s, p_dt
    y_c, s['ret_ypre'], s['ret_states'] = ret_fwd(p_r, ret_tables)
    s['p_r'] = p_r
    y_d, s['sb_total'] = sb_fwd(p_sb)
    s['p_sb'] = p_sb
    ys = [y_a, y_b, y_c, y_d]
    s['ys'] = ys
    ups = [mm_nn(ys[n], lw['w_branch'][n], name="branch_up") for n in range(N_BRANCH)]
    merged = merge_fwd(p_g, ups)
    s['p_g'], s['ups'] = p_g, ups
    s['merged'] = merged
    mix = mm_nn(merged, lw['w_out'], name="mix_out")
    s['mix'] = mix
    h2 = rms_fwd(mix, lw['n_mix_post'], res=h_res, name="rms_mix_post")
    s['h2'] = h2
    hf = rms_fwd(h2, lw['n_ffn_pre'], out_dtype=BF16, name="rms_ffn_pre")
    s['hf'] = hf
    f = mm_nn(hf, lw['w_ffn_in'], name="ffn_in")
    act = swiglu_fwd(f)
    s['f'], s['act'] = f, act
    fo = mm_nn(act, lw['w_ffn_out'], name="ffn_out")
    s['fo'] = fo
    return rms_fwd(fo, lw['n_ffn_post'], res=h2, name="rms_ffn_post"), s


def _layer_bwd(dh3, lw, s):
    g = {}
    d_fo, g['norm_ffn_post'] = rms_bwd(s['fo'], lw['n_ffn_post'], dh3, dx_dtype=BF16, name="rms_ffn_post_bwd")
    d_act = mm_nt(d_fo, lw['w_ffn_out'], name="ffn_out_dx")
    g['w_ffn_out'] = mm_tn(s['act'], d_fo, name="ffn_out_dw")
    df = swiglu_bwd(s['f'], d_act)
    d_hf = mm_nt(df, lw['w_ffn_in'], name="ffn_in_dx")
    g['w_ffn_in'] = mm_tn(s['hf'], df, name="ffn_in_dw")
    dh2, g['norm_ffn_pre'] = rms_bwd(s['h2'], lw['n_ffn_pre'], d_hf, add=dh3, name="rms_ffn_pre_bwd")
    d_mix, g['norm_mix_post'] = rms_bwd(s['mix'], lw['n_mix_post'], dh2, dx_dtype=BF16, name="rms_mix_post_bwd")
    d_merged = mm_nt(d_mix, lw['w_out'], name="mix_out_dx")
    g['w_out'] = mm_tn(s['merged'], d_mix, name="mix_out_dw")
    dp_g, dups = merge_bwd(s['p_g'], s['ups'], d_merged)
    dys = [mm_nt(dups[n], lw['w_branch'][n], name="branch_dx") for n in range(N_BRANCH)]
    g['w_branch'] = jnp.stack([mm_tn(s['ys'][n], dups[n], name="branch_dw") for n in range(N_BRANCH)])
    dp_a, g['conv_a'] = conv_mixer_bwd(s['p_a'], lw['conv_a'], dys[0])
    dp_s, dp_dt, ssd_grads = ssd_bwd(s['p_s'], s['p_dt'], s['ssd_rows'], s['ssd_saved'], dys[1])
    g.update(ssd_grads)
    dp_r = ret_bwd(s['p_r'], s['ret_tables'], s['ret_ypre'], s['ret_states'], dys[2])
    dq, dk, dv = sb_bwd(s['p_sb'], s['sb_total'], dys[3])
    dp_sb = jnp.concatenate([dq, dk.astype(BF16), dv.astype(BF16)], axis=1)
    hn = s['hn']
    d_hn = None
    dws = []
    for dp, w, nm in ((dp_a, lw['w_a'], "conv"), (dp_s, lw['w_s'], "ssd"), (dp_dt, lw['w_dt'], "dt"),
                      (dp_r, lw['w_r'], "ret"), (dp_sb, lw['w_sb'], "sb"), (dp_g, lw['w_g'], "gate")):
        d_hn = mm_nt(dp, w, acc=d_hn, name="proj_dx")
        dws.append(mm_tn(hn, dp, name="proj_dw"))
    dws[2] = dws[2][:, :SSD_HEADS]
    g['w_in'] = jnp.concatenate(dws, axis=1)
    dh_res, g['norm_mix_pre'] = rms_bwd(s['h_res'], lw['n_mix_pre'], d_hn, add=dh2, name="rms_mix_pre_bwd")
    for k in ('norm_ffn_post', 'norm_ffn_pre', 'norm_mix_post', 'norm_mix_pre'):
        g[k] = g[k][0]
    return dh_res, g


def _quarter(a, axis, j):
    n = a.shape[axis] // N_CHIPS
    return lax.slice_in_dim(a, j * n, (j + 1) * n, axis=axis)


def kernel(x, meta, w_in, conv_a, ssd_conv_w, ssd_conv_b, ssd_dt_bias, ssd_a_log, ssd_d, ssd_norm, w_branch, w_out, w_ffn_in, w_ffn_out, norm_mix_pre, norm_mix_post, norm_ffn_pre, norm_ffn_post, loss_target, m_meta, m_w_in, m_conv_a, m_ssd_conv_w, m_ssd_conv_b, m_ssd_dt_bias, m_ssd_a_log, m_ssd_d, m_ssd_norm, m_w_branch, m_w_out, m_w_ffn_in, m_w_ffn_out, m_norm_mix_pre, m_norm_mix_post, m_norm_ffn_pre, m_norm_ffn_post, v_meta, v_w_in, v_conv_a, v_ssd_conv_w, v_ssd_conv_b, v_ssd_dt_bias, v_ssd_a_log, v_ssd_d, v_ssd_norm, v_w_branch, v_w_out, v_w_ffn_in, v_w_ffn_out, v_norm_mix_pre, v_norm_mix_post, v_norm_ffn_pre, v_norm_ffn_post):
    w_loc = dict(meta=meta, w_in=w_in, conv_a=conv_a, ssd_conv_w=ssd_conv_w, ssd_conv_b=ssd_conv_b,
                 ssd_dt_bias=ssd_dt_bias, ssd_a_log=ssd_a_log, ssd_d=ssd_d, ssd_norm=ssd_norm, w_branch=w_branch,
                 w_out=w_out, w_ffn_in=w_ffn_in, w_ffn_out=w_ffn_out, norm_mix_pre=norm_mix_pre,
                 norm_mix_post=norm_mix_post, norm_ffn_pre=norm_ffn_pre, norm_ffn_post=norm_ffn_post)
    m_loc = dict(meta=m_meta, w_in=m_w_in, conv_a=m_conv_a, ssd_conv_w=m_ssd_conv_w, ssd_conv_b=m_ssd_conv_b,
                 ssd_dt_bias=m_ssd_dt_bias, ssd_a_log=m_ssd_a_log, ssd_d=m_ssd_d, ssd_norm=m_ssd_norm,
                 w_branch=m_w_branch, w_out=m_w_out, w_ffn_in=m_w_ffn_in, w_ffn_out=m_w_ffn_out,
                 norm_mix_pre=m_norm_mix_pre, norm_mix_post=m_norm_mix_post, norm_ffn_pre=m_norm_ffn_pre,
                 norm_ffn_post=m_norm_ffn_post)
    v_loc = dict(meta=v_meta, w_in=v_w_in, conv_a=v_conv_a, ssd_conv_w=v_ssd_conv_w, ssd_conv_b=v_ssd_conv_b,
                 ssd_dt_bias=v_ssd_dt_bias, ssd_a_log=v_ssd_a_log, ssd_d=v_ssd_d, ssd_norm=v_ssd_norm,
                 w_branch=v_w_branch, w_out=v_w_out, w_ffn_in=v_w_ffn_in, w_ffn_out=v_w_ffn_out,
                 norm_mix_pre=v_norm_mix_pre, norm_mix_post=v_norm_mix_post, norm_ffn_pre=v_norm_ffn_pre,
                 norm_ffn_post=v_norm_ffn_post)

    halves = gather_shards([w_loc[n].astype(BF16).reshape(-1, w_loc[n].shape[-1]) for n in MATMUL_WEIGHTS]
                           + [_pack([w_loc[n] for n in SMALL_SHARDED], F32, 16)])
    gathered = _join_halves(halves, sibling_share(halves, "share_weight_halves"), axis=1)
    full = {}
    for t, n in enumerate(MATMUL_WEIGHTS):
        full[n] = jnp.concatenate([gathered[t][j].reshape(w_loc[n].shape) for j in range(N_CHIPS)],
                                  axis=SHARD_AXIS[n])
    parts_f = [_unpack(gathered[-1][j], [w_loc[n].shape for n in SMALL_SHARDED]) for j in range(N_CHIPS)]
    for t, n in enumerate(SMALL_SHARDED):
        full[n] = jnp.concatenate([parts_f[j][t] for j in range(N_CHIPS)], axis=SHARD_AXIS[n])

    xs = x[0]
    seq = xs.shape[0]
    length = CHUNK + seq
    h = jnp.concatenate([jnp.zeros((PAD, D_MODEL), F32), full['meta'], xs], axis=0)
    lws, saved = [], []
    ret_tables = _ret_tables(length)
    for l in range(DEPTH):
        lw = _layer_weights(full, w_loc, l)
        h, s = _layer_fwd(h, lw, ret_tables)
        lws.append(lw)
        saved.append(s)

    loss_row, dh = loss_head(h, loss_target[0])
    loss = lax.psum(loss_row[0, 0], ("x", "y", "c"))

    layer_grads = [None] * DEPTH
    for l in reversed(range(DEPTH)):
        dh, layer_grads[l] = _layer_bwd(dh, lws[l], saved[l])
    grad_x = dh[CHUNK:][None]
    grads = {n: jnp.stack([layer_grads[l][n] for l in range(DEPTH)]) for n in WEIGHTS if n != 'meta'}
    grads['meta'] = dh[PAD:CHUNK]

    def rows2d(a):
        return a.reshape(-1, a.shape[-1])

    def small_pieces(j):
        return [_quarter(grads[n], SHARD_AXIS[n], j) if n in SHARD_AXIS else grads[n] for n in SMALL_ORDER]

    quarters = [jnp.stack([rows2d(_quarter(grads[n], SHARD_AXIS[n], j)) for j in range(N_CHIPS)])
                for n in MATMUL_WEIGHTS]
    quarters.append(jnp.stack([_pack(small_pieces(j), F32, 16) for j in range(N_CHIPS)]))
    reduced = reduce_gradients(quarters)
    results = {}
    for t, n in enumerate(MATMUL_WEIGHTS):
        shape = w_loc[n].shape
        new = adamw(reduced[t], rows2d(w_loc[n]), rows2d(m_loc[n]), rows2d(v_loc[n]))
        results[n] = [a.reshape(shape) for a in (reduced[t], *new)]
    small_new = adamw(reduced[-1], *[_pack([d[n] for n in SMALL_ORDER], F32, 16) for d in (w_loc, m_loc, v_loc)])
    small_shapes = [w_loc[n].shape for n in SMALL_ORDER]
    for kind, buf in enumerate((reduced[-1], *small_new)):
        for n, piece in zip(SMALL_ORDER, _unpack(buf, small_shapes)):
            results.setdefault(n, [None] * 4)[kind] = piece
    outs = [results[n][kind] for kind in range(4) for n in WEIGHTS]
    return (loss, grad_x, *outs)
```

```python
import functools
import math

import numpy as np
import jax
import jax.numpy as jnp
from jax import lax
from jax.experimental import pallas as pl
from jax.experimental.pallas import tpu as pltpu

F32 = jnp.float32
BF16 = jnp.bfloat16

D_MODEL = 1024
DEPTH = 2
N_META = 16
CHUNK = 128
PAD = CHUNK - N_META
EPS = 1e-6

CONV_A_K = 3
SSD_HEAD_DIM = 64
SSD_HEADS = 16
SSD_INNER = 1024
SSD_GROUPS = 4
SSD_STATE = 128
SSD_CONV_K = 4
SSD_CONV_DIM = SSD_INNER + 2 * SSD_GROUPS * SSD_STATE
RET_HEADS = 4
RET_QK_DIM = 256
RET_V_DIM = 256
RET_WIDTH = 1024
ROPE_BASE = 10000.0
SB_HEADS = 8
SB_HEAD_DIM = 128
N_BRANCH = 4
D_FF = 2816
DT_PAD = 128

ADAM_LR = 0.001
ADAM_B1 = 0.9
ADAM_B2 = 0.999
ADAM_EPS = 1e-08
ADAM_WD = 0.01
ADAM_STEP = 10

N_CHIPS = 4
N_DEV = 8
LANES = 128
VMEM_LIMIT = 56 * 1024 * 1024
MESH = pl.DeviceIdType.MESH

WEIGHTS = ['meta', 'w_in', 'conv_a', 'ssd_conv_w', 'ssd_conv_b', 'ssd_dt_bias', 'ssd_a_log', 'ssd_d',
           'ssd_norm', 'w_branch', 'w_out', 'w_ffn_in', 'w_ffn_out', 'norm_mix_pre', 'norm_mix_post',
           'norm_ffn_pre', 'norm_ffn_post']
SHARD_AXIS = {'meta': 1, 'w_in': 2, 'conv_a': 2, 'ssd_conv_w': 2, 'w_branch': 2, 'w_out': 1,
              'w_ffn_in': 2, 'w_ffn_out': 1}
MATMUL_WEIGHTS = ['w_in', 'w_branch', 'w_out', 'w_ffn_in', 'w_ffn_out']
SMALL_SHARDED = ['meta', 'conv_a', 'ssd_conv_w']
SMALL_ORDER = SMALL_SHARDED + [n for n in WEIGHTS if n not in SHARD_AXIS]


def _params(**kw):
    return pltpu.CompilerParams(vmem_limit_bytes=VMEM_LIMIT, **kw)


def _tile(n, prefs):
    for p in prefs:
        if n % p == 0:
            return p
    return n


MM_VMEM_BUDGET = 40 * 1024 * 1024
MM_ROW_TILES = (2752, 1376, 688, 384, 256, 128)
MM_COL_TILES = (1024, 512, 256, 128)


def _mm_tiles(m, n, cost):
    for tm in MM_ROW_TILES:
        if m % tm:
            continue
        for tn in MM_COL_TILES:
            if n % tn == 0 and cost(tm, tn) <= MM_VMEM_BUDGET:
                return tm, tn
    return _tile(m, (128, 8)), _tile(n, (128,))


def _size(x):
    return jnp.dtype(x.dtype).itemsize


def mm_nn(a, b, out_dtype=F32, name="mm_nn"):
    m, k = a.shape
    n = b.shape[1]
    ob = jnp.dtype(out_dtype).itemsize
    tm, tn = _mm_tiles(m, n, lambda tm, tn: (2 * tm * k * _size(a) + tm * k * 2 + 2 * k * tn * _size(b)
                                              + 2 * tm * tn * ob + tm * tn * 4))

    def body(a_ref, b_ref, o_ref):
        o_ref[...] = jnp.dot(a_ref[...].astype(BF16), b_ref[...].astype(BF16),
                             preferred_element_type=F32).astype(o_ref.dtype)

    return pl.pallas_call(
        body, name=name, grid=(m // tm, n // tn),
        in_specs=[pl.BlockSpec((tm, k), lambda i, j: (i, 0)), pl.BlockSpec((k, tn), lambda i, j: (0, j))],
        out_specs=pl.BlockSpec((tm, tn), lambda i, j: (i, j)),
        out_shape=jax.ShapeDtypeStruct((m, n), out_dtype),
        compiler_params=_params(dimension_semantics=("arbitrary", "arbitrary")),
    )(a, b)


def mm_nt(g, w, acc=None, name="mm_nt"):
    m, n = g.shape
    k = w.shape[0]
    has_acc = acc is not None
    tm, tn = _mm_tiles(m, n, lambda tm, tn: ((2 + 2 * has_acc) * tm * k * 4 + tm * k * 4 + 2 * tm * tn * _size(g)
                                              + tm * tn * 2 + 2 * k * tn * _size(w)))

    def body(*refs):
        if has_acc:
            g_ref, w_ref, acc_ref, o_ref = refs
        else:
            g_ref, w_ref, o_ref = refs
        j = pl.program_id(1)

        @pl.when(j == 0)
        def _():
            o_ref[...] = acc_ref[...] if has_acc else jnp.zeros_like(o_ref)

        o_ref[...] += lax.dot_general(g_ref[...].astype(BF16), w_ref[...].astype(BF16),
                                      (((1,), (1,)), ((), ())), preferred_element_type=F32)

    in_specs = [pl.BlockSpec((tm, tn), lambda i, j: (i, j)), pl.BlockSpec((k, tn), lambda i, j: (0, j))]
    args = [g, w]
    if has_acc:
        in_specs.append(pl.BlockSpec((tm, k), lambda i, j: (i, 0)))
        args.append(acc)
    return pl.pallas_call(
        body, name=name, grid=(m // tm, n // tn),
        in_specs=in_specs,
        out_specs=pl.BlockSpec((tm, k), lambda i, j: (i, 0)),
        out_shape=jax.ShapeDtypeStruct((m, k), F32),
        compiler_params=_params(dimension_semantics=("arbitrary", "arbitrary")),
    )(*args)


def mm_tn(x, g, name="mm_tn"):
    m, k = x.shape
    n = g.shape[1]
    tk = _tile(k, (1024, 1408, 512, 256, 128))
    tm, tn = _mm_tiles(m, n, lambda tm, tn: (3 * tk * tn * 4 + 2 * tm * tk * _size(x) + tm * tk * 2
                                              + 2 * tm * tn * _size(g) + tm * tn * 2))

    def body(x_ref, g_ref, o_ref):
        s = pl.program_id(2)

        @pl.when(s == 0)
        def _():
            o_ref[...] = jnp.zeros_like(o_ref)

        o_ref[...] += lax.dot_general(x_ref[...].astype(BF16), g_ref[...].astype(BF16),
                                      (((0,), (0,)), ((), ())), preferred_element_type=F32)

    return pl.pallas_call(
        body, name=name, grid=(k // tk, n // tn, m // tm),
        in_specs=[pl.BlockSpec((tm, tk), lambda a, b, s: (s, a)), pl.BlockSpec((tm, tn), lambda a, b, s: (s, b))],
        out_specs=pl.BlockSpec((tk, tn), lambda a, b, s: (a, b)),
        out_shape=jax.ShapeDtypeStruct((k, n), F32),
        compiler_params=_params(dimension_semantics=("arbitrary", "arbitrary", "arbitrary")),
    )(x, g)


def rms_fwd(x, w, res=None, out_dtype=F32, name="rms_fwd"):
    m, d = x.shape
    tm = _tile(m, (384, 256, 128))
    has_res = res is not None

    def body(*refs):
        if has_res:
            x_ref, w_ref, r_ref, o_ref = refs
        else:
            x_ref, w_ref, o_ref = refs
        xv = x_ref[...]
        y = xv * lax.rsqrt(jnp.mean(xv * xv, axis=-1, keepdims=True) + EPS) * w_ref[...]
        o_ref[...] = (y + r_ref[...] if has_res else y).astype(o_ref.dtype)

    row = pl.BlockSpec((tm, d), lambda i: (i, 0))
    in_specs = [row, pl.BlockSpec((1, d), lambda i: (0, 0))]
    args = [x, w]
    if has_res:
        in_specs.append(row)
        args.append(res)
    return pl.pallas_call(
        body, name=name, grid=(m // tm,), in_specs=in_specs, out_specs=row,
        out_shape=jax.ShapeDtypeStruct((m, d), out_dtype),
        compiler_params=_params(dimension_semantics=("arbitrary",)),
    )(*args)


def rms_bwd(x, w, dy, add=None, dx_dtype=F32, name="rms_bwd"):
    m, d = x.shape
    tm = _tile(m, (384, 256, 128))
    has_add = add is not None

    def body(*refs):
        if has_add:
            x_ref, w_ref, dy_ref, add_ref, dx_ref, dw_ref = refs
        else:
            x_ref, w_ref, dy_ref, dx_ref, dw_ref = refs
        i = pl.program_id(0)
        xv = x_ref[...]
        dyv = dy_ref[...]
        r = lax.rsqrt(jnp.mean(xv * xv, axis=-1, keepdims=True) + EPS)
        xh = xv * r
        dxh = dyv * w_ref[...]
        dx = r * (dxh - xh * jnp.mean(dxh * xh, axis=-1, keepdims=True))
        dx_ref[...] = (dx + add_ref[...] if has_add else dx).astype(dx_ref.dtype)

        @pl.when(i == 0)
        def _():
            dw_ref[...] = jnp.zeros_like(dw_ref)

        dw_ref[...] += jnp.sum(dyv * xh, axis=0, keepdims=True)

    row = pl.BlockSpec((tm, d), lambda i: (i, 0))
    vec = pl.BlockSpec((1, d), lambda i: (0, 0))
    in_specs = [row, vec, row]
    args = [x, w, dy]
    if has_add:
        in_specs.append(row)
        args.append(add)
    return pl.pallas_call(
        body, name=name, grid=(m // tm,), in_specs=in_specs, out_specs=[row, vec],
        out_shape=[jax.ShapeDtypeStruct((m, d), dx_dtype), jax.ShapeDtypeStruct((1, d), F32)],
        compiler_params=_params(dimension_semantics=("arbitrary",)),
    )(*args)


def loss_head(h, target):
    l, d = h.shape
    nblk = l // CHUNK

    def body(h_ref, t_ref, loss_ref, dh_ref, acc_ref):
        i = pl.program_id(0)

        @pl.when(i == 0)
        def _():
            acc_ref[...] = jnp.zeros_like(acc_ref)
            dh_ref[...] = jnp.zeros_like(dh_ref)

        @pl.when(i > 0)
        def _():
            e = h_ref[...] - t_ref[...]
            dh_ref[...] = e / d
            acc_ref[...] += jnp.sum(e * e, axis=0, keepdims=True)

        @pl.when(i == nblk - 1)
        def _():
            loss_ref[...] = jnp.zeros_like(loss_ref) + 0.5 * jnp.sum(acc_ref[...]) / d

    return pl.pallas_call(
        body, name="loss_head", grid=(nblk,),
        in_specs=[pl.BlockSpec((CHUNK, d), lambda i: (i, 0)),
                  pl.BlockSpec((CHUNK, d), lambda i: (jnp.maximum(i - 1, 0), 0))],
        out_specs=[pl.BlockSpec((1, LANES), lambda i: (0, 0)), pl.BlockSpec((CHUNK, d), lambda i: (i, 0))],
        out_shape=[jax.ShapeDtypeStruct((1, LANES), F32), jax.ShapeDtypeStruct((l, d), F32)],
        scratch_shapes=[pltpu.VMEM((1, d), F32)],
        compiler_params=_params(dimension_semantics=("arbitrary",)),
    )(h, target)


SB_BLK = 128


def _sb_tile(l):
    return _tile(l, (384, 256, 128))


def _sb_tri(strict_later):
    r = lax.broadcasted_iota(jnp.int32, (SB_BLK, 2 * SB_BLK), 0)
    c = lax.broadcasted_iota(jnp.int32, (SB_BLK, 2 * SB_BLK), 1)
    keep = (r > c) if strict_later else (r < c)
    return jnp.where(keep | (c >= SB_BLK), 1.0, 0.0).astype(BF16)


def _sb_mask(i, j, t):
    qpos = i * t + lax.broadcasted_iota(jnp.int32, (t, t), 0)
    kpos = j * t + lax.broadcasted_iota(jnp.int32, (t, t), 1)
    return (kpos < qpos) & (kpos >= PAD)


def _sb_scores(q, k, scale, mask):
    z = lax.dot_general(q, k, (((1,), (1,)), ((), ())), preferred_element_type=F32) * scale
    sp = jnp.maximum(z, 0.0) + jnp.log(1.0 + jnp.exp(-jnp.abs(z)))
    lneg = -sp if mask is None else jnp.where(mask, -sp, 0.0)
    return z - sp, lneg


def _sb_block_sums(x, tri):
    s = jnp.dot(x.astype(BF16), tri, preferred_element_type=F32)
    return s[:, :SB_BLK], s[:, SB_BLK:]


SB_DEAD = -110.0


def _sb_walk_down(i, step, carry):
    carry = step(i, carry, True)

    def alive(c):
        return (jnp.max(c[0]) > SB_DEAD).astype(jnp.int32)

    def body(state):
        j, _, c = state
        c = step(j, c, False)
        return j - 1, alive(c), c

    j, go, carry = lax.while_loop(lambda s: (s[0] >= 1) & (s[1] > 0), body, (i - 1, alive(carry), carry))
    reach0 = ((j == 0) & (go > 0) & (i > 0)).astype(jnp.int32)
    carry = lax.fori_loop(0, reach0, lambda t, c: step(0, c, True), carry)
    return carry, jnp.where(reach0 > 0, 0, j + 1)


def _sb_walk_up(i, first, step, carry):
    start0 = ((first == 0) & (i > 0)).astype(jnp.int32)
    carry = lax.fori_loop(0, start0, lambda t, c: step(0, c, True), carry)
    carry = lax.fori_loop(jnp.maximum(first, 1), i, lambda j, c: step(j, c, False), carry)
    return step(i, carry, True)


def sb_fwd(qkv):
    l = qkv.shape[0]
    t = _sb_tile(l)
    nb = t // SB_BLK
    scale = SB_HEAD_DIM ** -0.5

    def body(q_ref, k_ref, v_ref, o_ref, walk_ref):
        i = pl.program_id(1)
        q = q_ref[...]
        tri = _sb_tri(True)

        def step(j, carry, masked):
            later, acc = carry
            rows = pl.ds(pl.multiple_of(j * t, t), t)
            mask = _sb_mask(i, j, t) if masked else None
            lpos, lneg = _sb_scores(q, k_ref[rows, :], scale, mask)
            ws = [None] * nb
            for b in reversed(range(nb)):
                cols = slice(b * SB_BLK, (b + 1) * SB_BLK)
                within, total = _sb_block_sums(lneg[:, cols], tri)
                ws[b] = jnp.exp(lpos[:, cols] + within + later)
                later = later + total
            w = jnp.concatenate(ws, axis=1)
            if masked:
                w = jnp.where(mask, w, 0.0)
            acc = acc + jnp.dot(w.astype(BF16), v_ref[rows, :], preferred_element_type=F32)
            return later, acc

        carry = (jnp.zeros((t, SB_BLK), F32), jnp.zeros((t, SB_HEAD_DIM), F32))
        (later, acc), first = _sb_walk_down(i, step, carry)
        o_ref[...] = acc.astype(o_ref.dtype)
        walk_ref[0] = later[:, :1]
        walk_ref[1] = jnp.zeros((t, 1), F32) + first.astype(F32)

    return pl.pallas_call(
        body, name="sb_fwd", grid=(SB_HEADS, l // t),
        in_specs=[pl.BlockSpec((t, SB_HEAD_DIM), lambda h, i: (i, h)),
                  pl.BlockSpec((l, SB_HEAD_DIM), lambda h, i: (0, SB_HEADS + h)),
                  pl.BlockSpec((l, SB_HEAD_DIM), lambda h, i: (0, 2 * SB_HEADS + h))],
        out_specs=[pl.BlockSpec((t, SB_HEAD_DIM), lambda h, i: (i, h)),
                   pl.BlockSpec((2, None, t, 1), lambda h, i: (0, h, i, 0))],
        out_shape=[jax.ShapeDtypeStruct((l, D_MODEL), BF16), jax.ShapeDtypeStruct((2, SB_HEADS, l, 1), F32)],
        compiler_params=_params(dimension_semantics=("arbitrary", "arbitrary")),
    )(qkv, qkv, qkv)


def sb_bwd(qkv, walk, dout):
    l = qkv.shape[0]
    t = _sb_tile(l)
    nb = t // SB_BLK
    nq = l // t
    scale = SB_HEAD_DIM ** -0.5

    def body(q_ref, k_ref, v_ref, walk_ref, do_ref, dq_ref, dk_hbm, dv_hbm, dk_acc, dv_acc):
        h = pl.program_id(0)
        i = pl.program_id(1)

        @pl.when(i == 0)
        def _():
            dk_acc[...] = jnp.zeros_like(dk_acc)
            dv_acc[...] = jnp.zeros_like(dv_acc)

        q = q_ref[...]
        dob = do_ref[...].astype(BF16)
        tri_later = _sb_tri(True)
        tri_before = _sb_tri(False)

        def step(j, carry, masked):
            later, g_before, dq = carry
            rows = pl.ds(pl.multiple_of(j * t, t), t)
            k = k_ref[rows, :]
            v = v_ref[rows, :]
            mask = _sb_mask(i, j, t) if masked else None
            lpos, lneg = _sb_scores(q, k, scale, mask)
            dw = lax.dot_general(dob, v, (((1,), (1,)), ((), ())), preferred_element_type=F32)
            ws, dzs = [None] * nb, [None] * nb
            for b in range(nb):
                cols = slice(b * SB_BLK, (b + 1) * SB_BLK)
                within, total = _sb_block_sums(lneg[:, cols], tri_later)
                later = later - total
                wb = jnp.exp(lpos[:, cols] + within + later)
                if masked:
                    wb = jnp.where(mask[:, cols], wb, 0.0)
                g = dw[:, cols] * wb
                g_within, g_total = _sb_block_sums(g, tri_before)
                dz = g - (g + g_before + g_within) * jnp.exp(lpos[:, cols])
                if masked:
                    dz = jnp.where(mask[:, cols], dz, 0.0)
                g_before = g_before + g_total
                ws[b] = wb.astype(BF16)
                dzs[b] = (dz * scale).astype(BF16)
            w = jnp.concatenate(ws, axis=1)
            dzb = jnp.concatenate(dzs, axis=1)
            dq = dq + jnp.dot(dzb, k, preferred_element_type=F32)
            dk_acc[rows, :] += lax.dot_general(dzb, q, (((0,), (0,)), ((), ())), preferred_element_type=F32)
            dv_acc[rows, :] += lax.dot_general(w, dob, (((0,), (0,)), ((), ())), preferred_element_type=F32)
            return later, g_before, dq

        carry = (jnp.broadcast_to(walk_ref[0], (t, SB_BLK)), jnp.zeros((t, SB_BLK), F32),
                 jnp.zeros((t, SB_HEAD_DIM), F32))
        first = jnp.max(walk_ref[1]).astype(jnp.int32)
        _, _, dq = _sb_walk_up(i, first, step, carry)
        dq_ref[...] = dq.astype(dq_ref.dtype)

        @pl.when(i == nq - 1)
        def _():
            cols = pl.ds(pl.multiple_of(h * SB_HEAD_DIM, SB_HEAD_DIM), SB_HEAD_DIM)
            pltpu.sync_copy(dk_acc, dk_hbm.at[:, cols])
            pltpu.sync_copy(dv_acc, dv_hbm.at[:, cols])

    blk = lambda h, i: (i, h)
    return pl.pallas_call(
        body, name="sb_bwd", grid=(SB_HEADS, nq),
        in_specs=[pl.BlockSpec((t, SB_HEAD_DIM), blk),
                  pl.BlockSpec((l, SB_HEAD_DIM), lambda h, i: (0, SB_HEADS + h)),
                  pl.BlockSpec((l, SB_HEAD_DIM), lambda h, i: (0, 2 * SB_HEADS + h)),
                  pl.BlockSpec((2, None, t, 1), lambda h, i: (0, h, i, 0)), pl.BlockSpec((t, SB_HEAD_DIM), blk)],
        out_specs=[pl.BlockSpec((t, SB_HEAD_DIM), blk), pl.BlockSpec(memory_space=pl.ANY),
                   pl.BlockSpec(memory_space=pl.ANY)],
        out_shape=[jax.ShapeDtypeStruct((l, D_MODEL), BF16), jax.ShapeDtypeStruct((l, D_MODEL), F32),
                   jax.ShapeDtypeStruct((l, D_MODEL), F32)],
        scratch_shapes=[pltpu.VMEM((l, SB_HEAD_DIM), F32), pltpu.VMEM((l, SB_HEAD_DIM), F32)],
        compiler_params=_params(dimension_semantics=("arbitrary", "arbitrary")),
    )(qkv, qkv, qkv, walk, dout)


_HBM = pl.BlockSpec(memory_space=pl.ANY)


def _other_chips(x, y):
    return [(1 - x, y), (x, 1 - y), (1 - x, 1 - y)]


def _comm_call(body, name, ins, out_shapes, n_remote, n_local):
    return pl.pallas_call(
        body, name=name, in_specs=[_HBM] * len(ins), out_specs=[_HBM] * len(out_shapes), out_shape=out_shapes,
        scratch_shapes=[pltpu.SemaphoreType.DMA((n_remote,)), pltpu.SemaphoreType.DMA((n_remote,)),
                        pltpu.SemaphoreType.DMA((max(n_local, 1),))],
    )(*ins)


def gather_shards(shards):
    n = len(shards)

    def body(*refs):
        ins, outs = refs[:n], refs[n:2 * n]
        send_sems, recv_sems, local_sems = refs[2 * n:]
        x, y, c = lax.axis_index("x"), lax.axis_index("y"), lax.axis_index("c")
        me = 2 * x + y

        def half(t):
            rh = ins[t].shape[0] // 2
            return ins[t].at[pl.ds(pl.multiple_of(c * rh, 8), rh), :]

        own = [pltpu.make_async_copy(half(t), outs[t].at[me], local_sems.at[t]) for t in range(n)]
        for cp in own:
            cp.start()

        def copy(t, k, px, py, slot):
            return pltpu.make_async_remote_copy(
                src_ref=half(t), dst_ref=outs[t].at[slot], send_sem=send_sems.at[3 * t + k],
                recv_sem=recv_sems.at[3 * t + k], device_id=(px, py, c), device_id_type=MESH)

        chips = _other_chips(x, y)
        sends = [copy(t, k, px, py, me) for t in range(n) for k, (px, py) in enumerate(chips)]
        for cp in sends:
            cp.start()
        for t in range(n):
            for k, (px, py) in enumerate(chips):
                copy(t, k, px, py, 2 * px + py).wait_recv()
        for cp in sends:
            cp.wait_send()
        for cp in own:
            cp.wait()

    out_shapes = [jax.ShapeDtypeStruct((N_CHIPS, s.shape[0] // 2, s.shape[1]), s.dtype) for s in shards]
    return _comm_call(body, "gather_shards", shards, out_shapes, 3 * n, n)


def sibling_swap_halves(gs):
    n = len(gs)

    def body(*refs):
        ins, outs = refs[:n], refs[n:2 * n]
        send_sems, recv_sems, _ = refs[2 * n:]
        x, y, c = lax.axis_index("x"), lax.axis_index("y"), lax.axis_index("c")
        copies = []
        for t in range(n):
            rh = ins[t].shape[1] // 2
            src = ins[t].at[:, pl.ds(pl.multiple_of((1 - c) * rh, 8), rh), :]
            copies.append(pltpu.make_async_remote_copy(
                src_ref=src, dst_ref=outs[t], send_sem=send_sems.at[t], recv_sem=recv_sems.at[t],
                device_id=(x, y, 1 - c), device_id_type=MESH))
        for cp in copies:
            cp.start()
        for cp in copies:
            cp.wait_recv()
        for cp in copies:
            cp.wait_send()

    out_shapes = [jax.ShapeDtypeStruct((g.shape[0], g.shape[1] // 2, g.shape[2]), g.dtype) for g in gs]
    return _comm_call(body, "sibling_swap_halves", gs, out_shapes, n, 0)


def chip_exchange(ps):
    n = len(ps)

    def body(*refs):
        ins, outs = refs[:n], refs[n:2 * n]
        send_sems, recv_sems, local_sems = refs[2 * n:]
        x, y, c = lax.axis_index("x"), lax.axis_index("y"), lax.axis_index("c")
        me = 2 * x + y
        own = [pltpu.make_async_copy(ins[t].at[me], outs[t].at[me], local_sems.at[t]) for t in range(n)]
        for cp in own:
            cp.start()

        def copy(t, k, px, py, src_slot, dst_slot):
            return pltpu.make_async_remote_copy(
                src_ref=ins[t].at[src_slot], dst_ref=outs[t].at[dst_slot], send_sem=send_sems.at[3 * t + k],
                recv_sem=recv_sems.at[3 * t + k], device_id=(px, py, c), device_id_type=MESH)

        chips = _other_chips(x, y)
        sends = [copy(t, k, px, py, 2 * px + py, me) for t in range(n) for k, (px, py) in enumerate(chips)]
        for cp in sends:
            cp.start()
        for t in range(n):
            for k, (px, py) in enumerate(chips):
                copy(t, k, px, py, me, 2 * px + py).wait_recv()
        for cp in sends:
            cp.wait_send()
        for cp in own:
            cp.wait()

    out_shapes = [jax.ShapeDtypeStruct(p.shape, p.dtype) for p in ps]
    return _comm_call(body, "chip_exchange", ps, out_shapes, 3 * n, n)


def sibling_share(ss, name):
    n = len(ss)

    def body(*refs):
        ins, outs = refs[:n], refs[n:2 * n]
        send_sems, recv_sems, _ = refs[2 * n:]
        x, y, c = lax.axis_index("x"), lax.axis_index("y"), lax.axis_index("c")
        copies = [pltpu.make_async_remote_copy(
            src_ref=ins[t], dst_ref=outs[t], send_sem=send_sems.at[t], recv_sem=recv_sems.at[t],
            device_id=(x, y, 1 - c), device_id_type=MESH) for t in range(n)]
        for cp in copies:
            cp.start()
        for cp in copies:
            cp.wait_recv()
        for cp in copies:
            cp.wait_send()

    out_shapes = [jax.ShapeDtypeStruct(s.shape, s.dtype) for s in ss]
    return _comm_call(body, name, ss, out_shapes, n, 0)


EW_BLOCK_BYTES = 2 * 1024 * 1024


def _ew_rows(rows, cols, copies=1):
    padded = -(-cols // LANES) * LANES
    for tr in (1024, 512, 256, 128, 64, 32, 16, 8):
        if rows % tr == 0 and copies * tr * padded * 4 <= EW_BLOCK_BYTES:
            return tr
    return rows


def add_pairs(a, b, out_dtype=F32):
    rows, cols = a.shape
    tr = _ew_rows(rows, cols)

    def body(a_ref, b_ref, o_ref):
        o_ref[...] = (a_ref[...] + b_ref[...]).astype(o_ref.dtype)

    blk = pl.BlockSpec((tr, cols), lambda i: (i, 0))
    return pl.pallas_call(
        body, name="add_pairs", grid=(rows // tr,), in_specs=[blk, blk], out_specs=blk,
        out_shape=jax.ShapeDtypeStruct((rows, cols), out_dtype),
        compiler_params=_params(dimension_semantics=("arbitrary",)),
    )(a, b)


def sum_chips(slots):
    _, rows, cols = slots.shape
    tr = _ew_rows(rows, cols, N_CHIPS)

    def body(s_ref, o_ref):
        acc = s_ref[0].astype(F32)
        for j in range(1, N_CHIPS):
            acc = acc + s_ref[j].astype(F32)
        o_ref[...] = acc

    return pl.pallas_call(
        body, name="sum_chips", grid=(rows // tr,),
        in_specs=[pl.BlockSpec((N_CHIPS, tr, cols), lambda i: (0, i, 0))],
        out_specs=pl.BlockSpec((tr, cols), lambda i: (i, 0)),
        out_shape=jax.ShapeDtypeStruct((rows, cols), F32),
        compiler_params=_params(dimension_semantics=("arbitrary",)),
    )(slots)


def adamw(g, w, m, v):
    rows, cols = g.shape
    tr = _ew_rows(rows, cols)

    def body(g_ref, w_ref, m_ref, v_ref, d_out, m_out, v_out):
        gv = g_ref[...]
        m_new = ADAM_B1 * m_ref[...] + (1.0 - ADAM_B1) * gv
        v_new = ADAM_B2 * v_ref[...] + (1.0 - ADAM_B2) * jnp.square(gv)
        m_hat = m_new / (1.0 - ADAM_B1 ** ADAM_STEP)
        v_hat = v_new / (1.0 - ADAM_B2 ** ADAM_STEP)
        d_out[...] = -ADAM_LR * (m_hat / (jnp.sqrt(v_hat) + ADAM_EPS) + ADAM_WD * w_ref[...])
        m_out[...] = m_new
        v_out[...] = v_new

    blk = pl.BlockSpec((tr, cols), lambda i: (i, 0))
    return pl.pallas_call(
        body, name="adamw", grid=(rows // tr,), in_specs=[blk] * 4, out_specs=[blk] * 3,
        out_shape=[jax.ShapeDtypeStruct((rows, cols), F32)] * 3,
        compiler_params=_params(dimension_semantics=("arbitrary",)),
    )(g, w, m, v)


def reduce_gradients(quarters):
    theirs = sibling_swap_halves(quarters)
    c = lax.axis_index("c")
    chip_partials = []
    for n, (q, t) in enumerate(zip(quarters, theirs)):
        four, rh, cols = t.shape
        mine = lax.dynamic_slice_in_dim(q, c * rh, rh, axis=1)
        wire = F32 if n == len(quarters) - 1 else BF16
        chip_partials.append(add_pairs(mine.reshape(four * rh, cols), t.reshape(four * rh, cols), wire)
                             .reshape(four, rh, cols))
    slots = chip_exchange(chip_partials)
    mine = [sum_chips(s) for s in slots]
    return _join_halves(mine, sibling_share(mine, "share_gradient_halves"), axis=0)


def _join_halves(mine, theirs, axis):
    c = lax.axis_index("c")
    return [jnp.concatenate([jnp.where(c == 0, m, t), jnp.where(c == 0, t, m)], axis=axis)
            for m, t in zip(mine, theirs)]


def _pack(pieces, dtype, row_multiple):
    flat = jnp.concatenate([p.astype(dtype).reshape(-1) for p in pieces])
    per = row_multiple * LANES
    padded = -(-flat.shape[0] // per) * per
    flat = jnp.pad(flat, (0, padded - flat.shape[0]))
    return flat.reshape(-1, LANES)


def _unpack(buf, shapes):
    flat = buf.reshape(-1)
    out, off = [], 0
    for s in shapes:
        n = int(np.prod(s))
        out.append(flat[off:off + n].reshape(s))
        off += n
    return out


RET_SCALE = RET_QK_DIM ** -0.5
RET_LOG_GAMMA = [math.log(1.0 - 2.0 ** (-5.0 - h)) for h in range(RET_HEADS)]
RET_HALF = RET_QK_DIM // 2


def _ret_tables(length):
    inv = ROPE_BASE ** (-jnp.arange(RET_HALF, dtype=F32) / RET_HALF)
    ang = jnp.arange(length).astype(F32)[:, None] * inv[None, :]
    log_gamma = jnp.log(1.0 - jnp.power(2.0, -5.0 - jnp.arange(RET_HEADS, dtype=F32)))
    idx = jnp.arange(CHUNK, dtype=F32)
    rel = idx[:, None] - idx[None, :]
    dmask = jnp.where(rel >= 0, jnp.exp(log_gamma[:, None, None] * jnp.maximum(rel, 0.0)), 0.0)
    k_decay = jnp.exp(log_gamma[:, None] * (CHUNK - 1 - idx)[None, :])[:, :, None]
    q_decay = jnp.exp(log_gamma[:, None] * (idx + 1.0)[None, :])[:, :, None]
    return jnp.cos(ang), jnp.sin(ang), dmask, k_decay, q_decay


def _rot(x, cs, sn):
    x1, x2 = x[:, :RET_HALF], x[:, RET_HALF:]
    return jnp.concatenate([x1 * cs - x2 * sn, x1 * sn + x2 * cs], axis=1)


def _unrot(d, cs, sn):
    d1, d2 = d[:, :RET_HALF], d[:, RET_HALF:]
    return jnp.concatenate([d1 * cs + d2 * sn, d2 * cs - d1 * sn], axis=1)


def _sigmoid(x):
    return 1.0 / (1.0 + jnp.exp(-x))


_NT = (((1,), (1,)), ((), ()))
_TN = (((0,), (0,)), ((), ()))


def _ret_specs(nc, rev):
    ch = (lambda c: nc - 1 - c) if rev else (lambda c: c)
    row = lambda w: pl.BlockSpec((CHUNK, w), lambda c: (ch(c), 0))
    const3 = lambda a, b: pl.BlockSpec((RET_HEADS, a, b), lambda c: (0, 0, 0))
    tables = [row(RET_HALF), row(RET_HALF), const3(CHUNK, CHUNK), const3(CHUNK, 1), const3(CHUNK, 1)]
    state = pl.BlockSpec((None, RET_HEADS, RET_QK_DIM, RET_V_DIM), lambda c: (ch(c), 0, 0, 0))
    return row, tables, state


def ret_fwd(p, tables):
    length = p.shape[0]
    nc = length // CHUNK
    row, table_specs, state_spec = _ret_specs(nc, False)

    def body(p_ref, cos_ref, sin_ref, dm_ref, kd_ref, qd_ref, y_ref, ypre_ref, st_ref, r_scr):
        c = pl.program_id(0)

        @pl.when(c == 0)
        def _():
            r_scr[...] = jnp.zeros_like(r_scr)

        cs, sn = cos_ref[...], sin_ref[...]
        valid = (c * CHUNK + lax.broadcasted_iota(jnp.int32, (CHUNK, 1), 0)) >= PAD
        for h in range(RET_HEADS):
            col = lambda part: slice(part * D_MODEL + h * RET_QK_DIM, part * D_MODEL + (h + 1) * RET_QK_DIM)
            qb = _rot(p_ref[:, col(0)], cs, sn).astype(BF16)
            kr = _rot(p_ref[:, col(1)], cs, sn) * RET_SCALE
            kb = kr.astype(BF16)
            vb = jnp.where(valid, p_ref[:, col(2)], 0.0).astype(BF16)
            s = lax.dot_general(qb, kb, _NT, preferred_element_type=F32) * dm_ref[h]
            r = r_scr[h]
            st_ref[h] = r
            y = (jnp.dot(s.astype(BF16), vb, preferred_element_type=F32)
                 + jnp.dot(qb, r.astype(BF16), preferred_element_type=F32) * qd_ref[h])
            kdb = (kr * kd_ref[h]).astype(BF16)
            r_scr[h] = r * math.exp(RET_LOG_GAMMA[h] * CHUNK) + lax.dot_general(kdb, vb, _TN,
                                                                                preferred_element_type=F32)
            out = slice(h * RET_V_DIM, (h + 1) * RET_V_DIM)
            ypre_ref[:, out] = y
            mu = jnp.mean(y, axis=-1, keepdims=True)
            yc = y - mu
            yn = yc * lax.rsqrt(jnp.mean(yc * yc, axis=-1, keepdims=True) + EPS)
            g = p_ref[:, col(3)]
            y_ref[:, out] = (yn * (g * _sigmoid(g))).astype(y_ref.dtype)

    return pl.pallas_call(
        body, name="ret_fwd", grid=(nc,),
        in_specs=[row(4 * D_MODEL)] + table_specs,
        out_specs=[row(D_MODEL), row(D_MODEL), state_spec],
        out_shape=[jax.ShapeDtypeStruct((length, D_MODEL), BF16), jax.ShapeDtypeStruct((length, D_MODEL), F32),
                   jax.ShapeDtypeStruct((nc, RET_HEADS, RET_QK_DIM, RET_V_DIM), F32)],
        scratch_shapes=[pltpu.VMEM((RET_HEADS, RET_QK_DIM, RET_V_DIM), F32)],
        compiler_params=_params(dimension_semantics=("arbitrary",)),
    )(p, *tables)


def ret_bwd(p, tables, ypre, states, dyo):
    length = p.shape[0]
    nc = length // CHUNK
    row, table_specs, state_spec = _ret_specs(nc, True)

    def body(p_ref, cos_ref, sin_ref, dm_ref, kd_ref, qd_ref, ypre_ref, st_ref, dyo_ref, dp_ref, dr_scr):
        c = pl.program_id(0)

        @pl.when(c == 0)
        def _():
            dr_scr[...] = jnp.zeros_like(dr_scr)

        cs, sn = cos_ref[...], sin_ref[...]
        valid = ((nc - 1 - c) * CHUNK + lax.broadcasted_iota(jnp.int32, (CHUNK, 1), 0)) >= PAD
        for h in range(RET_HEADS):
            col = lambda part: slice(part * D_MODEL + h * RET_QK_DIM, part * D_MODEL + (h + 1) * RET_QK_DIM)
            out = slice(h * RET_V_DIM, (h + 1) * RET_V_DIM)
            qb = _rot(p_ref[:, col(0)], cs, sn).astype(BF16)
            kr = _rot(p_ref[:, col(1)], cs, sn) * RET_SCALE
            kb = kr.astype(BF16)
            vb = jnp.where(valid, p_ref[:, col(2)], 0.0).astype(BF16)
            g = p_ref[:, col(3)]
            y = ypre_ref[:, out]
            dyo_h = dyo_ref[:, out]
            mu = jnp.mean(y, axis=-1, keepdims=True)
            yc = y - mu
            rs = lax.rsqrt(jnp.mean(yc * yc, axis=-1, keepdims=True) + EPS)
            xh = yc * rs
            sg = _sigmoid(g)
            dp_ref[:, col(3)] = (dyo_h * xh * (sg * (1.0 + g * (1.0 - sg)))).astype(dp_ref.dtype)
            dyn = dyo_h * (g * sg)
            dy = rs * (dyn - jnp.mean(dyn, axis=-1, keepdims=True)
                       - xh * jnp.mean(dyn * xh, axis=-1, keepdims=True))
            dyb = dy.astype(BF16)
            dm = dm_ref[h]
            sm = (lax.dot_general(qb, kb, _NT, preferred_element_type=F32) * dm).astype(BF16)
            dsb = (lax.dot_general(dyb, vb, _NT, preferred_element_type=F32) * dm).astype(BF16)
            rb = st_ref[h].astype(BF16)
            dyqb = (dy * qd_ref[h]).astype(BF16)
            dr = dr_scr[h]
            drb = dr.astype(BF16)
            kd = kd_ref[h]
            dq = (jnp.dot(dsb, kb, preferred_element_type=F32)
                  + lax.dot_general(dyqb, rb, _NT, preferred_element_type=F32))
            dk = (lax.dot_general(dsb, qb, _TN, preferred_element_type=F32)
                  + lax.dot_general(vb, drb, _NT, preferred_element_type=F32) * kd)
            dv = (lax.dot_general(sm, dyb, _TN, preferred_element_type=F32)
                  + jnp.dot((kr * kd).astype(BF16), drb, preferred_element_type=F32))
            dr_scr[h] = dr * math.exp(RET_LOG_GAMMA[h] * CHUNK) + lax.dot_general(qb, dyqb, _TN,
                                                                                 preferred_element_type=F32)
            dp_ref[:, col(0)] = _unrot(dq, cs, sn).astype(dp_ref.dtype)
            dp_ref[:, col(1)] = (_unrot(dk, cs, sn) * RET_SCALE).astype(dp_ref.dtype)
            dp_ref[:, col(2)] = jnp.where(valid, dv, 0.0).astype(dp_ref.dtype)

    return pl.pallas_call(
        body, name="ret_bwd", grid=(nc,),
        in_specs=[row(4 * D_MODEL)] + table_specs + [row(D_MODEL), state_spec, row(D_MODEL)],
        out_specs=row(4 * D_MODEL),
        out_shape=jax.ShapeDtypeStruct((length, 4 * D_MODEL), BF16),
        scratch_shapes=[pltpu.VMEM((RET_HEADS, RET_QK_DIM, RET_V_DIM), F32)],
        compiler_params=_params(dimension_semantics=("arbitrary",)),
    )(p, *tables, ypre, states, dyo)


HALO = 8
HALO_PER_CHUNK = CHUNK // HALO


def _shift_down(cur, halo, m):
    if m == 0:
        return cur
    n = cur.shape[0]
    rows = lax.broadcasted_iota(jnp.int32, cur.shape, 0)
    edge = jnp.tile(pltpu.roll(halo, m, 0), (n // HALO, 1))
    return jnp.where(rows < m, edge, pltpu.roll(cur, m, 0))


def _shift_up(cur, halo, m):
    if m == 0:
        return cur
    n = cur.shape[0]
    rows = lax.broadcasted_iota(jnp.int32, cur.shape, 0)
    edge = jnp.tile(pltpu.roll(halo, HALO - m, 0), (n // HALO, 1))
    return jnp.where(rows >= n - m, edge, pltpu.roll(cur, n - m, 0))


def _gated_input(ref, row0):
    rows = row0 + lax.broadcasted_iota(jnp.int32, (ref.shape[0], 1), 0)
    return jnp.where((rows >= PAD) & (row0 >= 0), ref[:, D_MODEL:2 * D_MODEL] * ref[:, 2 * D_MODEL:], 0.0)


def conv_mixer_fwd(p, conv_w):
    length = p.shape[0]
    nc = length // CHUNK
    kt = conv_w.shape[0]

    def body(cur_ref, prev_ref, w_ref, y_ref):
        c = pl.program_id(0)
        u_cur = _gated_input(cur_ref, c * CHUNK)
        u_prev = _gated_input(prev_ref, c * CHUNK - HALO)
        acc = jnp.zeros((CHUNK, D_MODEL), F32)
        for i in range(kt):
            acc = acc + _shift_down(u_cur, u_prev, kt - 1 - i) * w_ref[i:i + 1, :]
        y_ref[...] = (cur_ref[:, :D_MODEL] * acc).astype(y_ref.dtype)

    return pl.pallas_call(
        body, name="conv_mixer_fwd", grid=(nc,),
        in_specs=[pl.BlockSpec((CHUNK, 3 * D_MODEL), lambda c: (c, 0)),
                  pl.BlockSpec((HALO, 3 * D_MODEL), lambda c: (jnp.maximum(c * HALO_PER_CHUNK - 1, 0), 0)),
                  pl.BlockSpec((kt, D_MODEL), lambda c: (0, 0))],
        out_specs=pl.BlockSpec((CHUNK, D_MODEL), lambda c: (c, 0)),
        out_shape=jax.ShapeDtypeStruct((length, D_MODEL), BF16),
        compiler_params=_params(dimension_semantics=("arbitrary",)),
    )(p, p, conv_w)


def conv_mixer_bwd(p, conv_w, dy):
    length = p.shape[0]
    nc = length // CHUNK
    kt = conv_w.shape[0]

    def body(cur_ref, prev_ref, w_ref, dy_ref, dyn_ref, pn_ref, dp_ref, dw_ref):
        c = pl.program_id(0)
        rows = lax.broadcasted_iota(jnp.int32, (CHUNK, 1), 0)
        u_cur = _gated_input(cur_ref, c * CHUNK)
        u_prev = _gated_input(prev_ref, c * CHUNK - HALO)
        b_gate = cur_ref[:, :D_MODEL]
        dyv = dy_ref[...]
        dconv = dyv * b_gate
        dconv_next = jnp.where(c + 1 < nc, dyn_ref[...] * pn_ref[:, :D_MODEL], 0.0)

        @pl.when(c == 0)
        def _():
            dw_ref[...] = jnp.zeros_like(dw_ref)

        acc = jnp.zeros((CHUNK, D_MODEL), F32)
        du = jnp.zeros((CHUNK, D_MODEL), F32)
        for i in range(kt):
            shifted = _shift_down(u_cur, u_prev, kt - 1 - i)
            acc = acc + shifted * w_ref[i:i + 1, :]
            dw_ref[i:i + 1, :] += jnp.sum(dconv * shifted, axis=0, keepdims=True)
            du = du + _shift_up(dconv, dconv_next, kt - 1 - i) * w_ref[i:i + 1, :]
        du = jnp.where(c * CHUNK + rows >= PAD, du, 0.0)
        dp_ref[:, :D_MODEL] = (dyv * acc).astype(dp_ref.dtype)
        dp_ref[:, D_MODEL:2 * D_MODEL] = (du * cur_ref[:, 2 * D_MODEL:]).astype(dp_ref.dtype)
        dp_ref[:, 2 * D_MODEL:] = (du * cur_ref[:, D_MODEL:2 * D_MODEL]).astype(dp_ref.dtype)

    nxt = lambda c: (jnp.minimum((c + 1) * HALO_PER_CHUNK, length // HALO - 1), 0)
    return pl.pallas_call(
        body, name="conv_mixer_bwd", grid=(nc,),
        in_specs=[pl.BlockSpec((CHUNK, 3 * D_MODEL), lambda c: (c, 0)),
                  pl.BlockSpec((HALO, 3 * D_MODEL), lambda c: (jnp.maximum(c * HALO_PER_CHUNK - 1, 0), 0)),
                  pl.BlockSpec((kt, D_MODEL), lambda c: (0, 0)),
                  pl.BlockSpec((CHUNK, D_MODEL), lambda c: (c, 0)),
                  pl.BlockSpec((HALO, D_MODEL), nxt),
                  pl.BlockSpec((HALO, 3 * D_MODEL), nxt)],
        out_specs=[pl.BlockSpec((CHUNK, 3 * D_MODEL), lambda c: (c, 0)),
                   pl.BlockSpec((kt, D_MODEL), lambda c: (0, 0))],
        out_shape=[jax.ShapeDtypeStruct((length, 3 * D_MODEL), BF16), jax.ShapeDtypeStruct((kt, D_MODEL), F32)],
        compiler_params=_params(dimension_semantics=("arbitrary",)),
    )(p, p, conv_w, dy, dy, p)


def merge_fwd(gate_logits, ups):
    length = gate_logits.shape[0]
    tm = _tile(length, (384, 256, 128))

    def body(g_ref, u0, u1, u2, u3, o_ref):
        acc = jnp.zeros((tm, D_MODEL), F32)
        for n, u in enumerate((u0, u1, u2, u3)):
            acc = acc + _sigmoid(g_ref[:, n * D_MODEL:(n + 1) * D_MODEL]) * u[...]
        o_ref[...] = acc.astype(o_ref.dtype)

    row = pl.BlockSpec((tm, D_MODEL), lambda i: (i, 0))
    return pl.pallas_call(
        body, name="merge_fwd", grid=(length // tm,),
        in_specs=[pl.BlockSpec((tm, N_BRANCH * D_MODEL), lambda i: (i, 0))] + [row] * N_BRANCH,
        out_specs=row, out_shape=jax.ShapeDtypeStruct((length, D_MODEL), BF16),
        compiler_params=_params(dimension_semantics=("arbitrary",)),
    )(gate_logits, *ups)


def merge_bwd(gate_logits, ups, dmerged):
    length = gate_logits.shape[0]
    tm = _tile(length, (384, 256, 128))

    def body(g_ref, u0, u1, u2, u3, dm_ref, dg_ref, d0, d1, d2, d3):
        dm = dm_ref[...]
        for n, (u, du) in enumerate(((u0, d0), (u1, d1), (u2, d2), (u3, d3))):
            cols = slice(n * D_MODEL, (n + 1) * D_MODEL)
            s = _sigmoid(g_ref[:, cols])
            du[...] = (dm * s).astype(du.dtype)
            dg_ref[:, cols] = (dm * u[...] * (s * (1.0 - s))).astype(dg_ref.dtype)

    row = pl.BlockSpec((tm, D_MODEL), lambda i: (i, 0))
    wide = pl.BlockSpec((tm, N_BRANCH * D_MODEL), lambda i: (i, 0))
    outs = pl.pallas_call(
        body, name="merge_bwd", grid=(length // tm,),
        in_specs=[wide] + [row] * (N_BRANCH + 1),
        out_specs=[wide] + [row] * N_BRANCH,
        out_shape=[jax.ShapeDtypeStruct((length, N_BRANCH * D_MODEL), BF16)]
        + [jax.ShapeDtypeStruct((length, D_MODEL), BF16)] * N_BRANCH,
        compiler_params=_params(dimension_semantics=("arbitrary",)),
    )(gate_logits, *ups, dmerged)
    return outs[0], list(outs[1:])


def swiglu_fwd(f):
    length = f.shape[0]
    tm = _tile(length, (384, 256, 128))

    def body(f_ref, o_ref):
        a = f_ref[:, :D_FF]
        o_ref[...] = (a * _sigmoid(a) * f_ref[:, D_FF:]).astype(o_ref.dtype)

    return pl.pallas_call(
        body, name="swiglu_fwd", grid=(length // tm,),
        in_specs=[pl.BlockSpec((tm, 2 * D_FF), lambda i: (i, 0))],
        out_specs=pl.BlockSpec((tm, D_FF), lambda i: (i, 0)),
        out_shape=jax.ShapeDtypeStruct((length, D_FF), BF16),
        compiler_params=_params(dimension_semantics=("arbitrary",)),
    )(f)


def swiglu_bwd(f, dact):
    length = f.shape[0]
    tm = _tile(length, (384, 256, 128))

    def body(f_ref, d_ref, df_ref):
        a = f_ref[:, :D_FF]
        up = f_ref[:, D_FF:]
        d = d_ref[...]
        s = _sigmoid(a)
        df_ref[:, :D_FF] = (d * up * (s * (1.0 + a * (1.0 - s)))).astype(df_ref.dtype)
        df_ref[:, D_FF:] = (d * (a * s)).astype(df_ref.dtype)

    return pl.pallas_call(
        body, name="swiglu_bwd", grid=(length // tm,),
        in_specs=[pl.BlockSpec((tm, 2 * D_FF), lambda i: (i, 0)), pl.BlockSpec((tm, D_FF), lambda i: (i, 0))],
        out_specs=pl.BlockSpec((tm, 2 * D_FF), lambda i: (i, 0)),
        out_shape=jax.ShapeDtypeStruct((length, 2 * D_FF), BF16),
        compiler_params=_params(dimension_semantics=("arbitrary",)),
    )(f, dact)


SSD_PAIRS = SSD_HEADS // 2
SSD_XBC = SSD_CONV_DIM
SSD_GW = SSD_INNER // SSD_GROUPS


def _split3(x):
    h1 = x.astype(BF16)
    r1 = x - h1.astype(F32)
    h2 = r1.astype(BF16)
    h3 = (r1 - h2.astype(F32)).astype(BF16)
    return h1, h2, h3


def _tri_apply(tri, x, dims):
    out = None
    for part in _split3(x):
        t = lax.dot_general(tri, part, dims, preferred_element_type=F32)
        out = t if out is None else out + t
    return out


def _softplus(x):
    return jnp.maximum(x, 0.0) + jnp.log(1.0 + jnp.exp(-jnp.abs(x)))


def _lane_pair(x, pair):
    lanes = lax.broadcasted_iota(jnp.int32, (x.shape[0], LANES), 1)
    return jnp.where(lanes < SSD_HEAD_DIM, x[:, 2 * pair:2 * pair + 1], x[:, 2 * pair + 1:2 * pair + 2])


def _half_sums(t):
    lanes = lax.broadcasted_iota(jnp.int32, t.shape, 1)
    lo = jnp.sum(jnp.where(lanes < SSD_HEAD_DIM, t, 0.0), axis=1, keepdims=True)
    return lo, jnp.sum(t, axis=1, keepdims=True) - lo


def _put_cols(cols):
    rows = cols[0].shape[0]
    lanes = lax.broadcasted_iota(jnp.int32, (rows, LANES), 1)
    out = jnp.zeros((rows, LANES), F32)
    for h, col in enumerate(cols):
        out = out + jnp.where(lanes == h, col, 0.0)
    return out


def ssd_pre_fwd(p, dt_raw, conv_w, conv_b, dt_bias, a_log):
    length = p.shape[0]
    nc = length // CHUNK
    kt = conv_w.shape[0]

    def body(cur_ref, prev_ref, raw_ref, w_ref, b_ref, bias_ref, alog_ref, act_ref, dt_ref, a_ref):
        c = pl.program_id(0)
        rows = lax.broadcasted_iota(jnp.int32, (CHUNK, 1), 0)
        vm = c * CHUNK + rows >= PAD
        u_cur = jnp.where(vm, cur_ref[:, SSD_INNER:], 0.0)
        u_prev = jnp.where((c * CHUNK - HALO + rows[:HALO] >= PAD) & (c >= 1), prev_ref[:, SSD_INNER:], 0.0)
        pre = jnp.zeros((CHUNK, SSD_XBC), F32) + b_ref[...]
        for i in range(kt):
            pre = pre + _shift_down(u_cur, u_prev, kt - 1 - i) * w_ref[i:i + 1, :]
        act = pre * _sigmoid(pre)
        act_ref[:, :SSD_INNER] = jnp.where(vm, act[:, :SSD_INNER], 0.0)
        act_ref[:, SSD_INNER:] = act[:, SSD_INNER:]
        dt = _softplus(raw_ref[...] + bias_ref[...])
        dt_ref[...] = dt
        a_ref[...] = -jnp.exp(alog_ref[...]) * dt

    row = lambda w: pl.BlockSpec((CHUNK, w), lambda c: (c, 0))
    vec = lambda w: pl.BlockSpec((1, w), lambda c: (0, 0))
    return pl.pallas_call(
        body, name="ssd_pre_fwd", grid=(nc,),
        in_specs=[row(3 * D_MODEL),
                  pl.BlockSpec((HALO, 3 * D_MODEL), lambda c: (jnp.maximum(c * HALO_PER_CHUNK - 1, 0), 0)),
                  row(LANES), pl.BlockSpec((kt, SSD_XBC), lambda c: (0, 0)), vec(SSD_XBC), vec(LANES), vec(LANES)],
        out_specs=[row(SSD_XBC), row(LANES), row(LANES)],
        out_shape=[jax.ShapeDtypeStruct((length, SSD_XBC), F32), jax.ShapeDtypeStruct((length, LANES), F32),
                   jax.ShapeDtypeStruct((length, LANES), F32)],
        compiler_params=_params(dimension_semantics=("arbitrary",)),
    )(p, p, dt_raw, conv_w, conv_b, dt_bias, a_log)


def ssd_pre_bwd(p, dt_raw, conv_w, conv_b, dt_bias, a_log, dact, ddt, da, dz):
    length = p.shape[0]
    nc = length // CHUNK
    kt = conv_w.shape[0]

    def body(cur_ref, prev_ref, raw_ref, w_ref, b_ref, bias_ref, alog_ref, dact_ref, ddt_ref, da_ref, dz_ref,
             dp_ref, draw_ref, dw_ref, db_ref, dbias_ref, dalog_ref, dpre_next):
        step = pl.program_id(0)
        c = nc - 1 - step
        rows = lax.broadcasted_iota(jnp.int32, (CHUNK, 1), 0)
        vm = c * CHUNK + rows >= PAD

        @pl.when(step == 0)
        def _():
            dpre_next[...] = jnp.zeros_like(dpre_next)
            dw_ref[...] = jnp.zeros_like(dw_ref)
            db_ref[...] = jnp.zeros_like(db_ref)
            dbias_ref[...] = jnp.zeros_like(dbias_ref)
            dalog_ref[...] = jnp.zeros_like(dalog_ref)

        u_cur = jnp.where(vm, cur_ref[:, SSD_INNER:], 0.0)
        u_prev = jnp.where((c * CHUNK - HALO + rows[:HALO] >= PAD) & (c >= 1), prev_ref[:, SSD_INNER:], 0.0)
        shifted = [_shift_down(u_cur, u_prev, kt - 1 - i) for i in range(kt)]
        pre = jnp.zeros((CHUNK, SSD_XBC), F32) + b_ref[...]
        for i in range(kt):
            pre = pre + shifted[i] * w_ref[i:i + 1, :]
        sg = _sigmoid(pre)
        lanes = lax.broadcasted_iota(jnp.int32, (CHUNK, SSD_XBC), 1)
        dact_v = jnp.where(vm | (lanes >= SSD_INNER), dact_ref[...], 0.0)
        dpre = dact_v * (sg * (1.0 + pre * (1.0 - sg)))
        db_ref[...] += jnp.sum(dpre, axis=0, keepdims=True)
        nxt = dpre_next[...]
        du = jnp.zeros((CHUNK, SSD_XBC), F32)
        for i in range(kt):
            dw_ref[i:i + 1, :] += jnp.sum(dpre * shifted[i], axis=0, keepdims=True)
            du = du + _shift_up(dpre, nxt, kt - 1 - i) * w_ref[i:i + 1, :]
        dpre_next[...] = dpre[:HALO]
        dp_ref[:, :SSD_INNER] = dz_ref[...].astype(dp_ref.dtype)
        dp_ref[:, SSD_INNER:] = jnp.where(vm, du, 0.0).astype(dp_ref.dtype)
        x = raw_ref[...] + bias_ref[...]
        neg_exp = -jnp.exp(alog_ref[...])
        dav = da_ref[...]
        draw = (ddt_ref[...] + dav * neg_exp) * _sigmoid(x)
        draw_ref[...] = draw.astype(draw_ref.dtype)
        dbias_ref[...] += jnp.sum(draw, axis=0, keepdims=True)
        dalog_ref[...] += jnp.sum(dav * (neg_exp * _softplus(x)), axis=0, keepdims=True)

    rev = lambda c: (nc - 1 - c, 0)
    row = lambda w: pl.BlockSpec((CHUNK, w), rev)
    vec = lambda w: pl.BlockSpec((1, w), lambda c: (0, 0))
    taps = pl.BlockSpec((kt, SSD_XBC), lambda c: (0, 0))
    return pl.pallas_call(
        body, name="ssd_pre_bwd", grid=(nc,),
        in_specs=[row(3 * D_MODEL),
                  pl.BlockSpec((HALO, 3 * D_MODEL),
                               lambda c: (jnp.maximum((nc - 1 - c) * HALO_PER_CHUNK - 1, 0), 0)),
                  row(LANES), taps, vec(SSD_XBC), vec(LANES), vec(LANES),
                  row(SSD_XBC), row(LANES), row(LANES), row(SSD_INNER)],
        out_specs=[row(3 * D_MODEL), row(LANES), taps, vec(SSD_XBC), vec(LANES), vec(LANES)],
        out_shape=[jax.ShapeDtypeStruct((length, 3 * D_MODEL), BF16), jax.ShapeDtypeStruct((length, LANES), BF16),
                   jax.ShapeDtypeStruct((kt, SSD_XBC), F32), jax.ShapeDtypeStruct((1, SSD_XBC), F32),
                   jax.ShapeDtypeStruct((1, LANES), F32), jax.ShapeDtypeStruct((1, LANES), F32)],
        scratch_shapes=[pltpu.VMEM((HALO, SSD_XBC), F32)],
        compiler_params=_params(dimension_semantics=("arbitrary",)),
    )(p, p, dt_raw, conv_w, conv_b, dt_bias, a_log, dact, ddt, da, dz)


def _tri_apply_lhs_t(x, tri):
    out = None
    for part in _split3(x):
        t = lax.dot_general(part, tri, (((0,), (1,)), ((), ())), preferred_element_type=F32)
        out = t if out is None else out + t
    return out


def ssd_core_fwd(act, dt, a, d_skip):
    length = act.shape[0]
    nc = length // CHUNK

    def body(act_ref, dt_ref, a_ref, dskip_ref, y_ref, st_ref, h_scr):
        c = pl.program_id(0)

        @pl.when(c == 0)
        def _():
            h_scr[...] = jnp.zeros_like(h_scr)

        r = lax.broadcasted_iota(jnp.int32, (CHUNK, CHUNK), 0)
        s = lax.broadcasted_iota(jnp.int32, (CHUNK, CHUNK), 1)
        causal = r >= s
        incl = jnp.where(causal, 1.0, 0.0).astype(BF16)
        a_v = a_ref[...]
        acs = _tri_apply(incl, a_v, (((1,), (0,)), ((), ())))
        acs_t = _tri_apply_lhs_t(a_v, incl)
        dt_v = dt_ref[...]
        lanes = lax.broadcasted_iota(jnp.int32, (CHUNK, LANES), 1)
        low = lanes < SSD_HEAD_DIM
        for g in range(SSD_GROUPS):
            bg = act_ref[:, SSD_INNER + g * SSD_STATE:SSD_INNER + (g + 1) * SSD_STATE].astype(BF16)
            cg = act_ref[:, SSD_INNER + (SSD_GROUPS + g) * SSD_STATE:
                         SSD_INNER + (SSD_GROUPS + g + 1) * SSD_STATE].astype(BF16)
            cb = lax.dot_general(cg, bg, _NT, preferred_element_type=F32)
            for pair in (2 * g, 2 * g + 1):
                cols = slice(pair * LANES, (pair + 1) * LANES)
                xs = act_ref[:, cols]
                x = xs * _lane_pair(dt_v, pair)
                ydiag = jnp.zeros((CHUNK, LANES), F32)
                for k, keep in ((0, low), (1, ~low)):
                    h = 2 * pair + k
                    seg = jnp.where(causal, jnp.exp(acs[:, h:h + 1] - acs_t[h:h + 1, :]), 0.0)
                    ydiag = ydiag + jnp.dot((cb * seg).astype(BF16), jnp.where(keep, x, 0.0).astype(BF16),
                                            preferred_element_type=F32)
                acs_p = _lane_pair(acs, pair)
                last = acs_p[CHUNK - 1:CHUNK, :]
                xds = (x * jnp.exp(last - acs_p)).astype(BF16)
                hprev = h_scr[pair]
                st_ref[pair] = hprev
                yoff = lax.dot_general(cg, hprev.astype(BF16), _NT, preferred_element_type=F32) * jnp.exp(acs_p)
                prow = lax.broadcasted_iota(jnp.int32, (LANES, 1), 0)
                cd = jnp.where(prow < SSD_HEAD_DIM, jnp.exp(acs_t[2 * pair:2 * pair + 1, CHUNK - 1:CHUNK]),
                               jnp.exp(acs_t[2 * pair + 1:2 * pair + 2, CHUNK - 1:CHUNK]))
                h_scr[pair] = hprev * cd + lax.dot_general(xds, bg, _TN, preferred_element_type=F32)
                y_ref[:, cols] = ydiag + yoff + xs * dskip_ref[:, cols]

    row = lambda w: pl.BlockSpec((CHUNK, w), lambda c: (c, 0))
    return pl.pallas_call(
        body, name="ssd_core_fwd", grid=(nc,),
        in_specs=[row(SSD_XBC), row(LANES), row(LANES), pl.BlockSpec((1, SSD_INNER), lambda c: (0, 0))],
        out_specs=[row(SSD_INNER), pl.BlockSpec((None, SSD_PAIRS, LANES, SSD_STATE), lambda c: (c, 0, 0, 0))],
        out_shape=[jax.ShapeDtypeStruct((length, SSD_INNER), F32),
                   jax.ShapeDtypeStruct((nc, SSD_PAIRS, LANES, SSD_STATE), F32)],
        scratch_shapes=[pltpu.VMEM((SSD_PAIRS, LANES, SSD_STATE), F32)],
        compiler_params=_params(dimension_semantics=("arbitrary",)),
    )(act, dt, a, d_skip)


def ssd_core_bwd(act, dt, a, d_skip, states, dy):
    length = act.shape[0]
    nc = length // CHUNK

    def body(act_ref, dt_ref, a_ref, dskip_ref, st_ref, dy_ref, dact_ref, ddt_ref, da_ref, dds_ref, dh_scr):
        step = pl.program_id(0)

        @pl.when(step == 0)
        def _():
            dh_scr[...] = jnp.zeros_like(dh_scr)
            dds_ref[...] = jnp.zeros_like(dds_ref)

        r = lax.broadcasted_iota(jnp.int32, (CHUNK, CHUNK), 0)
        s = lax.broadcasted_iota(jnp.int32, (CHUNK, CHUNK), 1)
        causal = r >= s
        incl = jnp.where(causal, 1.0, 0.0).astype(BF16)
        a_v = a_ref[...]
        acs = _tri_apply(incl, a_v, (((1,), (0,)), ((), ())))
        acs_t = _tri_apply_lhs_t(a_v, incl)
        dt_v = dt_ref[...]
        lanes = lax.broadcasted_iota(jnp.int32, (CHUNK, LANES), 1)
        low = lanes < SSD_HEAD_DIM
        prow = lax.broadcasted_iota(jnp.int32, (LANES, 1), 0)
        is_last = lax.broadcasted_iota(jnp.int32, (CHUNK, 1), 0) == CHUNK - 1
        dacs_cols = [None] * SSD_HEADS
        dacs_rows = [None] * SSD_HEADS
        ddt_cols = [None] * SSD_HEADS
        for g in range(SSD_GROUPS):
            b_cols = slice(SSD_INNER + g * SSD_STATE, SSD_INNER + (g + 1) * SSD_STATE)
            c_cols = slice(SSD_INNER + (SSD_GROUPS + g) * SSD_STATE, SSD_INNER + (SSD_GROUPS + g + 1) * SSD_STATE)
            bg = act_ref[:, b_cols].astype(BF16)
            cg = act_ref[:, c_cols].astype(BF16)
            cb = lax.dot_general(cg, bg, _NT, preferred_element_type=F32)
            dcb = jnp.zeros((CHUNK, CHUNK), F32)
            dbg = jnp.zeros((CHUNK, SSD_STATE), F32)
            dcg = jnp.zeros((CHUNK, SSD_STATE), F32)
            for pair in (2 * g, 2 * g + 1):
                cols = slice(pair * LANES, (pair + 1) * LANES)
                xs = act_ref[:, cols]
                dtp = _lane_pair(dt_v, pair)
                x = xs * dtp
                xb = x.astype(BF16)
                dyv = dy_ref[:, cols]
                dyb = dyv.astype(BF16)
                dds_ref[:, cols] += jnp.sum(dyv * xs, axis=0, keepdims=True)
                acs_p = _lane_pair(acs, pair)
                last = acs_p[CHUNK - 1:CHUNK, :]
                ds = jnp.exp(last - acs_p)
                ea = jnp.exp(acs_p)
                hprev = st_ref[pair]
                hb = hprev.astype(BF16)
                dh = dh_scr[pair]
                dhb = dh.astype(BF16)
                dx = jnp.zeros((CHUNK, LANES), F32)
                for k, keep in ((0, low), (1, ~low)):
                    h = 2 * pair + k
                    seg = jnp.where(causal, jnp.exp(acs[:, h:h + 1] - acs_t[h:h + 1, :]), 0.0)
                    lmat = cb * seg
                    dl = lax.dot_general(jnp.where(keep, dyv, 0.0).astype(BF16), xb, _NT,
                                         preferred_element_type=F32)
                    dcb = dcb + dl * seg
                    t = dl * lmat
                    dacs_cols[h] = jnp.sum(t, axis=1, keepdims=True)
                    dacs_rows[h] = jnp.sum(t, axis=0, keepdims=True)
                    dx = dx + jnp.where(keep, lax.dot_general(lmat.astype(BF16), dyb, _TN,
                                                              preferred_element_type=F32), 0.0)
                yoff = lax.dot_general(cg, hb, _NT, preferred_element_type=F32) * ea
                dm = (dyv * ea).astype(BF16)
                dcg = dcg + jnp.dot(dm, hb, preferred_element_type=F32)
                dxds = lax.dot_general(bg, dhb, _NT, preferred_element_type=F32)
                xds = x * ds
                dbg = dbg + jnp.dot(xds.astype(BF16), dhb, preferred_element_type=F32)
                dx = dx + dxds * ds
                t_ds = dxds * xds
                e_a = jnp.exp(acs_t[2 * pair:2 * pair + 1, CHUNK - 1:CHUNK])
                e_b = jnp.exp(acs_t[2 * pair + 1:2 * pair + 2, CHUNK - 1:CHUNK])
                cd = jnp.where(prow < SSD_HEAD_DIM, e_a, e_b)
                hd = dh * hprev
                dcd_a = jnp.sum(jnp.where(prow < SSD_HEAD_DIM, hd, 0.0), keepdims=True)
                dcd_b = jnp.sum(hd, keepdims=True) - dcd_a
                dh_scr[pair] = dh * cd + lax.dot_general(dm, cg, _TN, preferred_element_type=F32)
                col_lo, col_hi = _half_sums(dyv * yoff - t_ds)
                tot_lo, tot_hi = _half_sums(jnp.sum(t_ds, axis=0, keepdims=True))
                dacs_cols[2 * pair] += col_lo + jnp.where(is_last, tot_lo + dcd_a.reshape(1, 1) * e_a, 0.0)
                dacs_cols[2 * pair + 1] += col_hi + jnp.where(is_last, tot_hi + dcd_b.reshape(1, 1) * e_b, 0.0)
                dact_ref[:, cols] = dyv * dskip_ref[:, cols] + dx * dtp
                ddt_cols[2 * pair], ddt_cols[2 * pair + 1] = _half_sums(dx * xs)
            dcbb = dcb.astype(BF16)
            dact_ref[:, b_cols] = dbg + lax.dot_general(dcbb, cg, _TN, preferred_element_type=F32)
            dact_ref[:, c_cols] = dcg + jnp.dot(dcbb, bg, preferred_element_type=F32)
        ddt_ref[...] = _put_cols(ddt_cols)
        sub = lax.broadcasted_iota(jnp.int32, (LANES, CHUNK), 0)
        rows_mat = jnp.zeros((LANES, CHUNK), F32)
        for h in range(SSD_HEADS):
            rows_mat = rows_mat + jnp.where(sub == h, dacs_rows[h], 0.0)
        dacs = _put_cols(dacs_cols) - rows_mat.T
        da_ref[...] = _tri_apply(incl, dacs, (((0,), (0,)), ((), ())))

    rev = lambda c: (nc - 1 - c, 0)
    row = lambda w: pl.BlockSpec((CHUNK, w), rev)
    lane_vec = pl.BlockSpec((1, SSD_INNER), lambda c: (0, 0))
    return pl.pallas_call(
        body, name="ssd_core_bwd", grid=(nc,),
        in_specs=[row(SSD_XBC), row(LANES), row(LANES), lane_vec,
                  pl.BlockSpec((None, SSD_PAIRS, LANES, SSD_STATE), lambda c: (nc - 1 - c, 0, 0, 0)),
                  row(SSD_INNER)],
        out_specs=[row(SSD_XBC), row(LANES), row(LANES), lane_vec],
        out_shape=[jax.ShapeDtypeStruct((length, SSD_XBC), F32), jax.ShapeDtypeStruct((length, LANES), F32),
                   jax.ShapeDtypeStruct((length, LANES), F32), jax.ShapeDtypeStruct((1, SSD_INNER), F32)],
        scratch_shapes=[pltpu.VMEM((SSD_PAIRS, LANES, SSD_STATE), F32)],
        compiler_params=_params(dimension_semantics=("arbitrary",)),
    )(act, dt, a, d_skip, states, dy)


def ssd_post_fwd(y, p, norm_w):
    length = y.shape[0]
    tm = _tile(length, (384, 256, 128))

    def body(y_ref, p_ref, w_ref, o_ref):
        for g in range(SSD_GROUPS):
            cols = slice(g * SSD_GW, (g + 1) * SSD_GW)
            z = p_ref[:, cols]
            v = y_ref[:, cols] * (z * _sigmoid(z))
            o_ref[:, cols] = (v * lax.rsqrt(jnp.mean(v * v, axis=-1, keepdims=True) + EPS)
                              * w_ref[:, cols]).astype(o_ref.dtype)

    return pl.pallas_call(
        body, name="ssd_post_fwd", grid=(length // tm,),
        in_specs=[pl.BlockSpec((tm, SSD_INNER), lambda i: (i, 0)), pl.BlockSpec((tm, SSD_INNER), lambda i: (i, 0)),
                  pl.BlockSpec((1, SSD_INNER), lambda i: (0, 0))],
        out_specs=pl.BlockSpec((tm, SSD_INNER), lambda i: (i, 0)),
        out_shape=jax.ShapeDtypeStruct((length, SSD_INNER), BF16),
        compiler_params=_params(dimension_semantics=("arbitrary",)),
    )(y, p, norm_w)


def ssd_post_bwd(y, p, norm_w, dout):
    length = y.shape[0]
    tm = _tile(length, (384, 256, 128))

    def body(y_ref, p_ref, w_ref, do_ref, dy_ref, dz_ref, dw_ref):
        @pl.when(pl.program_id(0) == 0)
        def _():
            dw_ref[...] = jnp.zeros_like(dw_ref)

        for g in range(SSD_GROUPS):
            cols = slice(g * SSD_GW, (g + 1) * SSD_GW)
            z = p_ref[:, cols]
            yv = y_ref[:, cols]
            sg = _sigmoid(z)
            v = yv * (z * sg)
            rs = lax.rsqrt(jnp.mean(v * v, axis=-1, keepdims=True) + EPS)
            vh = v * rs
            do = do_ref[:, cols]
            dw_ref[:, cols] += jnp.sum(do * vh, axis=0, keepdims=True)
            dvh = do * w_ref[:, cols]
            dv = rs * (dvh - vh * jnp.mean(dvh * vh, axis=-1, keepdims=True))
            dy_ref[:, cols] = dv * (z * sg)
            dz_ref[:, cols] = dv * yv * (sg * (1.0 + z * (1.0 - sg)))

    blk = pl.BlockSpec((tm, SSD_INNER), lambda i: (i, 0))
    vec = pl.BlockSpec((1, SSD_INNER), lambda i: (0, 0))
    return pl.pallas_call(
        body, name="ssd_post_bwd", grid=(length // tm,),
        in_specs=[blk, blk, vec, blk], out_specs=[blk, blk, vec],
        out_shape=[jax.ShapeDtypeStruct((length, SSD_INNER), F32), jax.ShapeDtypeStruct((length, SSD_INNER), F32),
                   jax.ShapeDtypeStruct((1, SSD_INNER), F32)],
        compiler_params=_params(dimension_semantics=("arbitrary",)),
    )(y, p, norm_w, dout)


def _ssd_rows(lw):
    pad = lambda v: jnp.pad(v, (0, LANES - SSD_HEADS))[None]
    return dict(conv_w=lw['ssd_conv_w'], conv_b=lw['ssd_conv_b'][None], dt_bias=pad(lw['ssd_dt_bias']),
                a_log=pad(lw['ssd_a_log']), d_skip=jnp.repeat(lw['ssd_d'], SSD_HEAD_DIM)[None],
                norm_w=lw['ssd_norm'][None])


def ssd_fwd(p, dt_raw, rows):
    act, dt, a = ssd_pre_fwd(p, dt_raw, rows['conv_w'], rows['conv_b'], rows['dt_bias'], rows['a_log'])
    y, states = ssd_core_fwd(act, dt, a, rows['d_skip'])
    return ssd_post_fwd(y, p, rows['norm_w']), (act, dt, a, y, states)


def ssd_bwd(p, dt_raw, rows, saved, dout):
    act, dt, a, y, states = saved
    dy, dz, dnorm = ssd_post_bwd(y, p, rows['norm_w'], dout)
    dact, ddt, da, dskip_lanes = ssd_core_bwd(act, dt, a, rows['d_skip'], states, dy)
    dp, draw, dconv_w, dconv_b, dbias, dalog = ssd_pre_bwd(
        p, dt_raw, rows['conv_w'], rows['conv_b'], rows['dt_bias'], rows['a_log'], dact, ddt, da, dz)
    grads = dict(ssd_conv_w=dconv_w, ssd_conv_b=dconv_b[0], ssd_dt_bias=dbias[0, :SSD_HEADS],
                 ssd_a_log=dalog[0, :SSD_HEADS], ssd_norm=dnorm[0],
                 ssd_d=jnp.sum(dskip_lanes.reshape(SSD_HEADS, SSD_HEAD_DIM), axis=1))
    return dp, draw, grads


IN_A = (0, 3 * D_MODEL)
IN_S = (IN_A[1], IN_A[1] + SSD_INNER + SSD_CONV_DIM)
IN_DT = (IN_S[1], IN_S[1] + SSD_HEADS)
IN_R = (IN_DT[1], IN_DT[1] + 4 * D_MODEL)
IN_SB = (IN_R[1], IN_R[1] + 3 * D_MODEL)
IN_G = (IN_SB[1], IN_SB[1] + N_BRANCH * D_MODEL)
IN_WIDTH = IN_G[1]


def _layer_weights(full, small, l):
    w_in = full['w_in'][l]
    cut = lambda r: w_in[:, r[0]:r[1]]
    w_dt = jnp.pad(cut(IN_DT), ((0, 0), (0, DT_PAD - SSD_HEADS)))
    return dict(
        w_a=cut(IN_A), w_s=cut(IN_S), w_dt=w_dt, w_r=cut(IN_R), w_sb=cut(IN_SB), w_g=cut(IN_G),
        w_branch=[full['w_branch'][l, n] for n in range(N_BRANCH)],
        w_out=full['w_out'][l], w_ffn_in=full['w_ffn_in'][l], w_ffn_out=full['w_ffn_out'][l],
        conv_a=full['conv_a'][l], ssd_conv_w=full['ssd_conv_w'][l],
        ssd_conv_b=small['ssd_conv_b'][l], ssd_dt_bias=small['ssd_dt_bias'][l], ssd_a_log=small['ssd_a_log'][l],
        ssd_d=small['ssd_d'][l], ssd_norm=small['ssd_norm'][l],
        n_mix_pre=small['norm_mix_pre'][l][None], n_mix_post=small['norm_mix_post'][l][None],
        n_ffn_pre=small['norm_ffn_pre'][l][None], n_ffn_post=small['norm_ffn_post'][l][None],
    )


def _layer_fwd(h_res, lw, ret_tables):
    s = {'h_res': h_res, 'ret_tables': ret_tables}
    hn = rms_fwd(h_res, lw['n_mix_pre'], out_dtype=BF16, name="rms_mix_pre")
    s['hn'] = hn
    p_a = mm_nn(hn, lw['w_a'], name="proj_conv")
    p_s = mm_nn(hn, lw['w_s'], name="proj_ssd")
    p_dt = mm_nn(hn, lw['w_dt'], name="proj_dt")
    p_r = mm_nn(hn, lw['w_r'], name="proj_ret")
    p_sb = mm_nn(hn, lw['w_sb'], out_dtype=BF16, name="proj_sb")
    p_g = mm_nn(hn, lw['w_g'], name="proj_gate")
    y_a = conv_mixer_fwd(p_a, lw['conv_a'])
    s['p_a'] = p_a
    s['ssd_rows'] = _ssd_rows(lw)
    y_b, s['ssd_saved'] = ssd_fwd(p_s, p_dt, s['ssd_rows'])
    s['p_s'], s['p_dt'] = p_s, p_dt
    y_c, s['ret_ypre'], s['ret_states'] = ret_fwd(p_r, ret_tables)
    s['p_r'] = p_r
    y_d, s['sb_total'] = sb_fwd(p_sb)
    s['p_sb'] = p_sb
    ys = [y_a, y_b, y_c, y_d]
    s['ys'] = ys
    ups = [mm_nn(ys[n], lw['w_branch'][n], name="branch_up") for n in range(N_BRANCH)]
    merged = merge_fwd(p_g, ups)
    s['p_g'], s['ups'] = p_g, ups
    s['merged'] = merged
    mix = mm_nn(merged, lw['w_out'], name="mix_out")
    s['mix'] = mix
    h2 = rms_fwd(mix, lw['n_mix_post'], res=h_res, name="rms_mix_post")
    s['h2'] = h2
    hf = rms_fwd(h2, lw['n_ffn_pre'], out_dtype=BF16, name="rms_ffn_pre")
    s['hf'] = hf
    f = mm_nn(hf, lw['w_ffn_in'], name="ffn_in")
    act = swiglu_fwd(f)
    s['f'], s['act'] = f, act
    fo = mm_nn(act, lw['w_ffn_out'], name="ffn_out")
    s['fo'] = fo
    return rms_fwd(fo, lw['n_ffn_post'], res=h2, name="rms_ffn_post"), s


def _layer_bwd(dh3, lw, s):
    g = {}
    d_fo, g['norm_ffn_post'] = rms_bwd(s['fo'], lw['n_ffn_post'], dh3, dx_dtype=BF16, name="rms_ffn_post_bwd")
    d_act = mm_nt(d_fo, lw['w_ffn_out'], name="ffn_out_dx")
    g['w_ffn_out'] = mm_tn(s['act'], d_fo, name="ffn_out_dw")
    df = swiglu_bwd(s['f'], d_act)
    d_hf = mm_nt(df, lw['w_ffn_in'], name="ffn_in_dx")
    g['w_ffn_in'] = mm_tn(s['hf'], df, name="ffn_in_dw")
    dh2, g['norm_ffn_pre'] = rms_bwd(s['h2'], lw['n_ffn_pre'], d_hf, add=dh3, name="rms_ffn_pre_bwd")
    d_mix, g['norm_mix_post'] = rms_bwd(s['mix'], lw['n_mix_post'], dh2, dx_dtype=BF16, name="rms_mix_post_bwd")
    d_merged = mm_nt(d_mix, lw['w_out'], name="mix_out_dx")
    g['w_out'] = mm_tn(s['merged'], d_mix, name="mix_out_dw")
    dp_g, dups = merge_bwd(s['p_g'], s['ups'], d_merged)
    dys = [mm_nt(dups[n], lw['w_branch'][n], name="branch_dx") for n in range(N_BRANCH)]
    g['w_branch'] = jnp.stack([mm_tn(s['ys'][n], dups[n], name="branch_dw") for n in range(N_BRANCH)])
    dp_a, g['conv_a'] = conv_mixer_bwd(s['p_a'], lw['conv_a'], dys[0])
    dp_s, dp_dt, ssd_grads = ssd_bwd(s['p_s'], s['p_dt'], s['ssd_rows'], s['ssd_saved'], dys[1])
    g.update(ssd_grads)
    dp_r = ret_bwd(s['p_r'], s['ret_tables'], s['ret_ypre'], s['ret_states'], dys[2])
    dq, dk, dv = sb_bwd(s['p_sb'], s['sb_total'], dys[3])
    dp_sb = jnp.concatenate([dq, dk.astype(BF16), dv.astype(BF16)], axis=1)
    hn = s['hn']
    d_hn = None
    dws = []
    for dp, w, nm in ((dp_a, lw['w_a'], "conv"), (dp_s, lw['w_s'], "ssd"), (dp_dt, lw['w_dt'], "dt"),
                      (dp_r, lw['w_r'], "ret"), (dp_sb, lw['w_sb'], "sb"), (dp_g, lw['w_g'], "gate")):
        d_hn = mm_nt(dp, w, acc=d_hn, name="proj_dx")
        dws.append(mm_tn(hn, dp, name="proj_dw"))
    dws[2] = dws[2][:, :SSD_HEADS]
    g['w_in'] = jnp.concatenate(dws, axis=1)
    dh_res, g['norm_mix_pre'] = rms_bwd(s['h_res'], lw['n_mix_pre'], d_hn, add=dh2, name="rms_mix_pre_bwd")
    for k in ('norm_ffn_post', 'norm_ffn_pre', 'norm_mix_post', 'norm_mix_pre'):
        g[k] = g[k][0]
    return dh_res, g


def _quarter(a, axis, j):
    n = a.shape[axis] // N_CHIPS
    return lax.slice_in_dim(a, j * n, (j + 1) * n, axis=axis)


def kernel(x, meta, w_in, conv_a, ssd_conv_w, ssd_conv_b, ssd_dt_bias, ssd_a_log, ssd_d, ssd_norm, w_branch, w_out, w_ffn_in, w_ffn_out, norm_mix_pre, norm_mix_post, norm_ffn_pre, norm_ffn_post, loss_target, m_meta, m_w_in, m_conv_a, m_ssd_conv_w, m_ssd_conv_b, m_ssd_dt_bias, m_ssd_a_log, m_ssd_d, m_ssd_norm, m_w_branch, m_w_out, m_w_ffn_in, m_w_ffn_out, m_norm_mix_pre, m_norm_mix_post, m_norm_ffn_pre, m_norm_ffn_post, v_meta, v_w_in, v_conv_a, v_ssd_conv_w, v_ssd_conv_b, v_ssd_dt_bias, v_ssd_a_log, v_ssd_d, v_ssd_norm, v_w_branch, v_w_out, v_w_ffn_in, v_w_ffn_out, v_norm_mix_pre, v_norm_mix_post, v_norm_ffn_pre, v_norm_ffn_post):
    w_loc = dict(meta=meta, w_in=w_in, conv_a=conv_a, ssd_conv_w=ssd_conv_w, ssd_conv_b=ssd_conv_b,
                 ssd_dt_bias=ssd_dt_bias, ssd_a_log=ssd_a_log, ssd_d=ssd_d, ssd_norm=ssd_norm, w_branch=w_branch,
                 w_out=w_out, w_ffn_in=w_ffn_in, w_ffn_out=w_ffn_out, norm_mix_pre=norm_mix_pre,
                 norm_mix_post=norm_mix_post, norm_ffn_pre=norm_ffn_pre, norm_ffn_post=norm_ffn_post)
    m_loc = dict(meta=m_meta, w_in=m_w_in, conv_a=m_conv_a, ssd_conv_w=m_ssd_conv_w, ssd_conv_b=m_ssd_conv_b,
                 ssd_dt_bias=m_ssd_dt_bias, ssd_a_log=m_ssd_a_log, ssd_d=m_ssd_d, ssd_norm=m_ssd_norm,
                 w_branch=m_w_branch, w_out=m_w_out, w_ffn_in=m_w_ffn_in, w_ffn_out=m_w_ffn_out,
                 norm_mix_pre=m_norm_mix_pre, norm_mix_post=m_norm_mix_post, norm_ffn_pre=m_norm_ffn_pre,
                 norm_ffn_post=m_norm_ffn_post)
    v_loc = dict(meta=v_meta, w_in=v_w_in, conv_a=v_conv_a, ssd_conv_w=v_ssd_conv_w, ssd_conv_b=v_ssd_conv_b,
                 ssd_dt_bias=v_ssd_dt_bias, ssd_a_log=v_ssd_a_log, ssd_d=v_ssd_d, ssd_norm=v_ssd_norm,
                 w_branch=v_w_branch, w_out=v_w_out, w_ffn_in=v_w_ffn_in, w_ffn_out=v_w_ffn_out,
                 norm_mix_pre=v_norm_mix_pre, norm_mix_post=v_norm_mix_post, norm_ffn_pre=v_norm_ffn_pre,
                 norm_ffn_post=v_norm_ffn_post)

    halves = gather_shards([w_loc[n].astype(BF16).reshape(-1, w_loc[n].shape[-1]) for n in MATMUL_WEIGHTS]
                           + [_pack([w_loc[n] for n in SMALL_SHARDED], F32, 16)])
    gathered = _join_halves(halves, sibling_share(halves, "share_weight_halves"), axis=1)
    full = {}
    for t, n in enumerate(MATMUL_WEIGHTS):
        full[n] = jnp.concatenate([gathered[t][j].reshape(w_loc[n].shape) for j in range(N_CHIPS)],
                                  axis=SHARD_AXIS[n])
    parts_f = [_unpack(gathered[-1][j], [w_loc[n].shape for n in SMALL_SHARDED]) for j in range(N_CHIPS)]
    for t, n in enumerate(SMALL_SHARDED):
        full[n] = jnp.concatenate([parts_f[j][t] for j in range(N_CHIPS)], axis=SHARD_AXIS[n])

    xs = x[0]
    seq = xs.shape[0]
    length = CHUNK + seq
    h = jnp.concatenate([jnp.zeros((PAD, D_MODEL), F32), full['meta'], xs], axis=0)
    lws, saved = [], []
    ret_tables = _ret_tables(length)
    for l in range(DEPTH):
        lw = _layer_weights(full, w_loc, l)
        h, s = _layer_fwd(h, lw, ret_tables)
        lws.append(lw)
        saved.append(s)

    loss_row, dh = loss_head(h, loss_target[0])
    loss = lax.psum(loss_row[0, 0], ("x", "y", "c"))

    layer_grads = [None] * DEPTH
    for l in reversed(range(DEPTH)):
        dh, layer_grads[l] = _layer_bwd(dh, lws[l], saved[l])
    grad_x = dh[CHUNK:][None]
    grads = {n: jnp.stack([layer_grads[l][n] for l in range(DEPTH)]) for n in WEIGHTS if n != 'meta'}
    grads['meta'] = dh[PAD:CHUNK]

    def rows2d(a):
        return a.reshape(-1, a.shape[-1])

    def small_pieces(j):
        return [_quarter(grads[n], SHARD_AXIS[n], j) if n in SHARD_AXIS else grads[n] for n in SMALL_ORDER]

    quarters = [jnp.stack([rows2d(_quarter(grads[n], SHARD_AXIS[n], j)) for j in range(N_CHIPS)])
                for n in MATMUL_WEIGHTS]
    quarters.append(jnp.stack([_pack(small_pieces(j), F32, 16) for j in range(N_CHIPS)]))
    reduced = reduce_gradients(quarters)
    results = {}
    for t, n in enumerate(MATMUL_WEIGHTS):
        shape = w_loc[n].shape
        new = adamw(reduced[t], rows2d(w_loc[n]), rows2d(m_loc[n]), rows2d(v_loc[n]))
        results[n] = [a.reshape(shape) for a in (reduced[t], *new)]
    small_new = adamw(reduced[-1], *[_pack([d[n] for n in SMALL_ORDER], F32, 16) for d in (w_loc, m_loc, v_loc)])
    small_shapes = [w_loc[n].shape for n in SMALL_ORDER]
    for kind, buf in enumerate((reduced[-1], *small_new)):
        for n, piece in zip(SMALL_ORDER, _unpack(buf, small_shapes)):
            results.setdefault(n, [None] * 4)[kind] = piece
    outs = [results[n][kind] for kind in range(4) for n in WEIGHTS]
    return (loss, grad_x, *outs)
```

```python
import functools
import math

import numpy as np
import jax
import jax.numpy as jnp
from jax import lax
from jax.experimental import pallas as pl
from jax.experimental.pallas import tpu as pltpu

F32 = jnp.float32
BF16 = jnp.bfloat16

D_MODEL = 1024
DEPTH = 2
N_META = 16
CHUNK = 128
PAD = CHUNK - N_META
EPS = 1e-6

CONV_A_K = 3
SSD_HEAD_DIM = 64
SSD_HEADS = 16
SSD_INNER = 1024
SSD_GROUPS = 4
SSD_STATE = 128
SSD_CONV_K = 4
SSD_CONV_DIM = SSD_INNER + 2 * SSD_GROUPS * SSD_STATE
RET_HEADS = 4
RET_QK_DIM = 256
RET_V_DIM = 256
RET_WIDTH = 1024
ROPE_BASE = 10000.0
SB_HEADS = 8
SB_HEAD_DIM = 128
N_BRANCH = 4
D_FF = 2816
DT_PAD = 128

ADAM_LR = 0.001
ADAM_B1 = 0.9
ADAM_B2 = 0.999
ADAM_EPS = 1e-08
ADAM_WD = 0.01
ADAM_STEP = 10

N_CHIPS = 4
N_DEV = 8
LANES = 128
VMEM_LIMIT = 56 * 1024 * 1024
MESH = pl.DeviceIdType.MESH

WEIGHTS = ['meta', 'w_in', 'conv_a', 'ssd_conv_w', 'ssd_conv_b', 'ssd_dt_bias', 'ssd_a_log', 'ssd_d',
           'ssd_norm', 'w_branch', 'w_out', 'w_ffn_in', 'w_ffn_out', 'norm_mix_pre', 'norm_mix_post',
           'norm_ffn_pre', 'norm_ffn_post']
SHARD_AXIS = {'meta': 1, 'w_in': 2, 'conv_a': 2, 'ssd_conv_w': 2, 'w_branch': 2, 'w_out': 1,
              'w_ffn_in': 2, 'w_ffn_out': 1}
MATMUL_WEIGHTS = ['w_in', 'w_branch', 'w_out', 'w_ffn_in', 'w_ffn_out']
SMALL_SHARDED = ['meta', 'conv_a', 'ssd_conv_w']
SMALL_ORDER = SMALL_SHARDED + [n for n in WEIGHTS if n not in SHARD_AXIS]


def _params(**kw):
    return pltpu.CompilerParams(vmem_limit_bytes=VMEM_LIMIT, **kw)


def _tile(n, prefs):
    for p in prefs:
        if n % p == 0:
            return p
    return n


MM_VMEM_BUDGET = 40 * 1024 * 1024
MM_ROW_TILES = (2752, 1376, 688, 384, 256, 128)
MM_COL_TILES = (1024, 512, 256, 128)


def _mm_tiles(m, n, cost):
    for tm in MM_ROW_TILES:
        if m % tm:
            continue
        for tn in MM_COL_TILES:
            if n % tn == 0 and cost(tm, tn) <= MM_VMEM_BUDGET:
                return tm, tn
    return _tile(m, (128, 8)), _tile(n, (128,))


def _size(x):
    return jnp.dtype(x.dtype).itemsize


def mm_nn(a, b, out_dtype=F32, name="mm_nn"):
    m, k = a.shape
    n = b.shape[1]
    ob = jnp.dtype(out_dtype).itemsize
    tm, tn = _mm_tiles(m, n, lambda tm, tn: (2 * tm * k * _size(a) + tm * k * 2 + 2 * k * tn * _size(b)
                                              + 2 * tm * tn * ob + tm * tn * 4))

    def body(a_ref, b_ref, o_ref):
        o_ref[...] = jnp.dot(a_ref[...].astype(BF16), b_ref[...].astype(BF16),
                             preferred_element_type=F32).astype(o_ref.dtype)

    return pl.pallas_call(
        body, name=name, grid=(m // tm, n // tn),
        in_specs=[pl.BlockSpec((tm, k), lambda i, j: (i, 0)), pl.BlockSpec((k, tn), lambda i, j: (0, j))],
        out_specs=pl.BlockSpec((tm, tn), lambda i, j: (i, j)),
        out_shape=jax.ShapeDtypeStruct((m, n), out_dtype),
        compiler_params=_params(dimension_semantics=("arbitrary", "arbitrary")),
    )(a, b)


def mm_nt(g, w, acc=None, name="mm_nt"):
    m, n = g.shape
    k = w.shape[0]
    has_acc = acc is not None
    tm, tn = _mm_tiles(m, n, lambda tm, tn: ((2 + 2 * has_acc) * tm * k * 4 + tm * k * 4 + 2 * tm * tn * _size(g)
                                              + tm * tn * 2 + 2 * k * tn * _size(w)))

    def body(*refs):
        if has_acc:
            g_ref, w_ref, acc_ref, o_ref = refs
        else:
            g_ref, w_ref, o_ref = refs
        j = pl.program_id(1)

        @pl.when(j == 0)
        def _():
            o_ref[...] = acc_ref[...] if has_acc else jnp.zeros_like(o_ref)

        o_ref[...] += lax.dot_general(g_ref[...].astype(BF16), w_ref[...].astype(BF16),
                                      (((1,), (1,)), ((), ())), preferred_element_type=F32)

    in_specs = [pl.BlockSpec((tm, tn), lambda i, j: (i, j)), pl.BlockSpec((k, tn), lambda i, j: (0, j))]
    args = [g, w]
    if has_acc:
        in_specs.append(pl.BlockSpec((tm, k), lambda i, j: (i, 0)))
        args.append(acc)
    return pl.pallas_call(
        body, name=name, grid=(m // tm, n // tn),
        in_specs=in_specs,
        out_specs=pl.BlockSpec((tm, k), lambda i, j: (i, 0)),
        out_shape=jax.ShapeDtypeStruct((m, k), F32),
        compiler_params=_params(dimension_semantics=("arbitrary", "arbitrary")),
    )(*args)


def mm_tn(x, g, name="mm_tn"):
    m, k = x.shape
    n = g.shape[1]
    tk = _tile(k, (1024, 1408, 512, 256, 128))
    tm, tn = _mm_tiles(m, n, lambda tm, tn: (3 * tk * tn * 4 + 2 * tm * tk * _size(x) + tm * tk * 2
                                              + 2 * tm * tn * _size(g) + tm * tn * 2))

    def body(x_ref, g_ref, o_ref):
        s = pl.program_id(2)

        @pl.when(s == 0)
        def _():
            o_ref[...] = jnp.zeros_like(o_ref)

        o_ref[...] += lax.dot_general(x_ref[...].astype(BF16), g_ref[...].astype(BF16),
                                      (((0,), (0,)), ((), ())), preferred_element_type=F32)

    return pl.pallas_call(
        body, name=name, grid=(k // tk, n // tn, m // tm),
        in_specs=[pl.BlockSpec((tm, tk), lambda a, b, s: (s, a)), pl.BlockSpec((tm, tn), lambda a, b, s: (s, b))],
        out_specs=pl.BlockSpec((tk, tn), lambda a, b, s: (a, b)),
        out_shape=jax.ShapeDtypeStruct((k, n), F32),
        compiler_params=_params(dimension_semantics=("arbitrary", "arbitrary", "arbitrary")),
    )(x, g)


def rms_fwd(x, w, res=None, out_dtype=F32, name="rms_fwd"):
    m, d = x.shape
    tm = _tile(m, (688, 384, 256, 128))
    has_res = res is not None

    def body(*refs):
        if has_res:
            x_ref, w_ref, r_ref, o_ref = refs
        else:
            x_ref, w_ref, o_ref = refs
        xv = x_ref[...]
        y = xv * lax.rsqrt(jnp.mean(xv * xv, axis=-1, keepdims=True) + EPS) * w_ref[...]
        o_ref[...] = (y + r_ref[...] if has_res else y).astype(o_ref.dtype)

    row = pl.BlockSpec((tm, d), lambda i: (i, 0))
    in_specs = [row, pl.BlockSpec((1, d), lambda i: (0, 0))]
    args = [x, w]
    if has_res:
        in_specs.append(row)
        args.append(res)
    return pl.pallas_call(
        body, name=name, grid=(m // tm,), in_specs=in_specs, out_specs=row,
        out_shape=jax.ShapeDtypeStruct((m, d), out_dtype),
        compiler_params=_params(dimension_semantics=("arbitrary",)),
    )(*args)


def rms_bwd(x, w, dy, add=None, dx_dtype=F32, name="rms_bwd"):
    m, d = x.shape
    tm = _tile(m, (688, 384, 256, 128))
    has_add = add is not None

    def body(*refs):
        if has_add:
            x_ref, w_ref, dy_ref, add_ref, dx_ref, dw_ref = refs
        else:
            x_ref, w_ref, dy_ref, dx_ref, dw_ref = refs
        i = pl.program_id(0)
        xv = x_ref[...]
        dyv = dy_ref[...]
        r = lax.rsqrt(jnp.mean(xv * xv, axis=-1, keepdims=True) + EPS)
        xh = xv * r
        dxh = dyv * w_ref[...]
        dx = r * (dxh - xh * jnp.mean(dxh * xh, axis=-1, keepdims=True))
        dx_ref[...] = (dx + add_ref[...] if has_add else dx).astype(dx_ref.dtype)

        @pl.when(i == 0)
        def _():
            dw_ref[...] = jnp.zeros_like(dw_ref)

        dw_ref[...] += jnp.sum(dyv * xh, axis=0, keepdims=True)

    row = pl.BlockSpec((tm, d), lambda i: (i, 0))
    vec = pl.BlockSpec((1, d), lambda i: (0, 0))
    in_specs = [row, vec, row]
    args = [x, w, dy]
    if has_add:
        in_specs.append(row)
        args.append(add)
    return pl.pallas_call(
        body, name=name, grid=(m // tm,), in_specs=in_specs, out_specs=[row, vec],
        out_shape=[jax.ShapeDtypeStruct((m, d), dx_dtype), jax.ShapeDtypeStruct((1, d), F32)],
        compiler_params=_params(dimension_semantics=("arbitrary",)),
    )(*args)


def loss_head(h, target):
    l, d = h.shape
    nblk = l // CHUNK

    def body(h_ref, t_ref, loss_ref, dh_ref, acc_ref):
        i = pl.program_id(0)

        @pl.when(i == 0)
        def _():
            acc_ref[...] = jnp.zeros_like(acc_ref)
            dh_ref[...] = jnp.zeros_like(dh_ref)

        @pl.when(i > 0)
        def _():
            e = h_ref[...] - t_ref[...]
            dh_ref[...] = e / d
            acc_ref[...] += jnp.sum(e * e, axis=0, keepdims=True)

        @pl.when(i == nblk - 1)
        def _():
            loss_ref[...] = jnp.zeros_like(loss_ref) + 0.5 * jnp.sum(acc_ref[...]) / d

    return pl.pallas_call(
        body, name="loss_head", grid=(nblk,),
        in_specs=[pl.BlockSpec((CHUNK, d), lambda i: (i, 0)),
                  pl.BlockSpec((CHUNK, d), lambda i: (jnp.maximum(i - 1, 0), 0))],
        out_specs=[pl.BlockSpec((1, LANES), lambda i: (0, 0)), pl.BlockSpec((CHUNK, d), lambda i: (i, 0))],
        out_shape=[jax.ShapeDtypeStruct((1, LANES), F32), jax.ShapeDtypeStruct((l, d), F32)],
        scratch_shapes=[pltpu.VMEM((1, d), F32)],
        compiler_params=_params(dimension_semantics=("arbitrary",)),
    )(h, target)


SB_BLK = 128


def _sb_tile(l):
    return _tile(l, (384, 256, 128))


def _sb_tri(strict_later):
    r = lax.broadcasted_iota(jnp.int32, (SB_BLK, 2 * SB_BLK), 0)
    c = lax.broadcasted_iota(jnp.int32, (SB_BLK, 2 * SB_BLK), 1)
    keep = (r > c) if strict_later else (r < c)
    return jnp.where(keep | (c >= SB_BLK), 1.0, 0.0).astype(BF16)


def _sb_mask(i, j, t):
    qpos = i * t + lax.broadcasted_iota(jnp.int32, (t, t), 0)
    kpos = j * t + lax.broadcasted_iota(jnp.int32, (t, t), 1)
    return (kpos < qpos) & (kpos >= PAD)


def _sb_scores(q, k, scale, mask):
    z = lax.dot_general(q, k, (((1,), (1,)), ((), ())), preferred_element_type=F32) * scale
    sp = jnp.maximum(z, 0.0) + jnp.log(1.0 + jnp.exp(-jnp.abs(z)))
    lneg = -sp if mask is None else jnp.where(mask, -sp, 0.0)
    return z - sp, lneg


def _sb_block_sums(x, tri):
    s = jnp.dot(x.astype(BF16), tri, preferred_element_type=F32)
    return s[:, :SB_BLK], s[:, SB_BLK:]


SB_DEAD = -110.0


def _sb_walk_down(i, step, carry):
    carry = step(i, carry, True)

    def alive(c):
        return (jnp.max(c[0]) > SB_DEAD).astype(jnp.int32)

    def body(state):
        j, _, c = state
        c = step(j, c, False)
        return j - 1, alive(c), c

    j, go, carry = lax.while_loop(lambda s: (s[0] >= 1) & (s[1] > 0), body, (i - 1, alive(carry), carry))
    reach0 = ((j == 0) & (go > 0) & (i > 0)).astype(jnp.int32)
    carry = lax.fori_loop(0, reach0, lambda t, c: step(0, c, True), carry)
    return carry, jnp.where(reach0 > 0, 0, j + 1)


def _sb_walk_up(i, first, step, carry):
    start0 = ((first == 0) & (i > 0)).astype(jnp.int32)
    carry = lax.fori_loop(0, start0, lambda t, c: step(0, c, True), carry)
    carry = lax.fori_loop(jnp.maximum(first, 1), i, lambda j, c: step(j, c, False), carry)
    return step(i, carry, True)


def sb_fwd(qkv):
    l = qkv.shape[0]
    t = _sb_tile(l)
    nb = t // SB_BLK
    scale = SB_HEAD_DIM ** -0.5

    def body(q_ref, k_ref, v_ref, o_ref, walk_ref):
        i = pl.program_id(1)
        q = q_ref[...]
        tri = _sb_tri(True)

        def step(j, carry, masked):
            later, acc = carry
            rows = pl.ds(pl.multiple_of(j * t, t), t)
            mask = _sb_mask(i, j, t) if masked else None
            lpos, lneg = _sb_scores(q, k_ref[rows, :], scale, mask)
            ws = [None] * nb
            for b in reversed(range(nb)):
                cols = slice(b * SB_BLK, (b + 1) * SB_BLK)
                within, total = _sb_block_sums(lneg[:, cols], tri)
                ws[b] = jnp.exp(lpos[:, cols] + within + later)
                later = later + total
            w = jnp.concatenate(ws, axis=1)
            if masked:
                w = jnp.where(mask, w, 0.0)
            acc = acc + jnp.dot(w.astype(BF16), v_ref[rows, :], preferred_element_type=F32)
            return later, acc

        carry = (jnp.zeros((t, SB_BLK), F32), jnp.zeros((t, SB_HEAD_DIM), F32))
        (later, acc), first = _sb_walk_down(i, step, carry)
        o_ref[...] = acc.astype(o_ref.dtype)
        walk_ref[0] = later[:, :1]
        walk_ref[1] = jnp.zeros((t, 1), F32) + first.astype(F32)

    return pl.pallas_call(
        body, name="sb_fwd", grid=(SB_HEADS, l // t),
        in_specs=[pl.BlockSpec((t, SB_HEAD_DIM), lambda h, i: (i, h)),
                  pl.BlockSpec((l, SB_HEAD_DIM), lambda h, i: (0, SB_HEADS + h)),
                  pl.BlockSpec((l, SB_HEAD_DIM), lambda h, i: (0, 2 * SB_HEADS + h))],
        out_specs=[pl.BlockSpec((t, SB_HEAD_DIM), lambda h, i: (i, h)),
                   pl.BlockSpec((2, None, t, 1), lambda h, i: (0, h, i, 0))],
        out_shape=[jax.ShapeDtypeStruct((l, D_MODEL), BF16), jax.ShapeDtypeStruct((2, SB_HEADS, l, 1), F32)],
        compiler_params=_params(dimension_semantics=("arbitrary", "arbitrary")),
    )(qkv, qkv, qkv)


def sb_bwd(qkv, walk, dout):
    l = qkv.shape[0]
    t = _sb_tile(l)
    nb = t // SB_BLK
    nq = l // t
    scale = SB_HEAD_DIM ** -0.5

    def body(q_ref, k_ref, v_ref, walk_ref, do_ref, dq_ref, dk_hbm, dv_hbm, dk_acc, dv_acc):
        h = pl.program_id(0)
        i = pl.program_id(1)

        @pl.when(i == 0)
        def _():
            dk_acc[...] = jnp.zeros_like(dk_acc)
            dv_acc[...] = jnp.zeros_like(dv_acc)

        q = q_ref[...]
        dob = do_ref[...].astype(BF16)
        tri_later = _sb_tri(True)
        tri_before = _sb_tri(False)

        def step(j, carry, masked):
            later, g_before, dq = carry
            rows = pl.ds(pl.multiple_of(j * t, t), t)
            k = k_ref[rows, :]
            v = v_ref[rows, :]
            mask = _sb_mask(i, j, t) if masked else None
            lpos, lneg = _sb_scores(q, k, scale, mask)
            dw = lax.dot_general(dob, v, (((1,), (1,)), ((), ())), preferred_element_type=F32)
            ws, dzs = [None] * nb, [None] * nb
            for b in range(nb):
                cols = slice(b * SB_BLK, (b + 1) * SB_BLK)
                within, total = _sb_block_sums(lneg[:, cols], tri_later)
                later = later - total
                wb = jnp.exp(lpos[:, cols] + within + later)
                if masked:
                    wb = jnp.where(mask[:, cols], wb, 0.0)
                g = dw[:, cols] * wb
                g_within, g_total = _sb_block_sums(g, tri_before)
                dz = g - (g + g_before + g_within) * jnp.exp(lpos[:, cols])
                if masked:
                    dz = jnp.where(mask[:, cols], dz, 0.0)
                g_before = g_before + g_total
                ws[b] = wb.astype(BF16)
                dzs[b] = (dz * scale).astype(BF16)
            w = jnp.concatenate(ws, axis=1)
            dzb = jnp.concatenate(dzs, axis=1)
            dq = dq + jnp.dot(dzb, k, preferred_element_type=F32)
            dk_acc[rows, :] += lax.dot_general(dzb, q, (((0,), (0,)), ((), ())), preferred_element_type=F32)
            dv_acc[rows, :] += lax.dot_general(w, dob, (((0,), (0,)), ((), ())), preferred_element_type=F32)
            return later, g_before, dq

        carry = (jnp.broadcast_to(walk_ref[0], (t, SB_BLK)), jnp.zeros((t, SB_BLK), F32),
                 jnp.zeros((t, SB_HEAD_DIM), F32))
        first = jnp.max(walk_ref[1]).astype(jnp.int32)
        _, _, dq = _sb_walk_up(i, first, step, carry)
        dq_ref[...] = dq.astype(dq_ref.dtype)

        @pl.when(i == nq - 1)
        def _():
            cols = pl.ds(pl.multiple_of(h * SB_HEAD_DIM, SB_HEAD_DIM), SB_HEAD_DIM)
            pltpu.sync_copy(dk_acc, dk_hbm.at[:, cols])
            pltpu.sync_copy(dv_acc, dv_hbm.at[:, cols])

    blk = lambda h, i: (i, h)
    return pl.pallas_call(
        body, name="sb_bwd", grid=(SB_HEADS, nq),
        in_specs=[pl.BlockSpec((t, SB_HEAD_DIM), blk),
                  pl.BlockSpec((l, SB_HEAD_DIM), lambda h, i: (0, SB_HEADS + h)),
                  pl.BlockSpec((l, SB_HEAD_DIM), lambda h, i: (0, 2 * SB_HEADS + h)),
                  pl.BlockSpec((2, None, t, 1), lambda h, i: (0, h, i, 0)), pl.BlockSpec((t, SB_HEAD_DIM), blk)],
        out_specs=[pl.BlockSpec((t, SB_HEAD_DIM), blk), pl.BlockSpec(memory_space=pl.ANY),
                   pl.BlockSpec(memory_space=pl.ANY)],
        out_shape=[jax.ShapeDtypeStruct((l, D_MODEL), BF16), jax.ShapeDtypeStruct((l, D_MODEL), F32),
                   jax.ShapeDtypeStruct((l, D_MODEL), F32)],
        scratch_shapes=[pltpu.VMEM((l, SB_HEAD_DIM), F32), pltpu.VMEM((l, SB_HEAD_DIM), F32)],
        compiler_params=_params(dimension_semantics=("arbitrary", "arbitrary")),
    )(qkv, qkv, qkv, walk, dout)


_HBM = pl.BlockSpec(memory_space=pl.ANY)


def _other_chips(x, y):
    return [(1 - x, y), (x, 1 - y), (1 - x, 1 - y)]


def _comm_call(body, name, ins, out_shapes, n_remote, n_local):
    return pl.pallas_call(
        body, name=name, in_specs=[_HBM] * len(ins), out_specs=[_HBM] * len(out_shapes), out_shape=out_shapes,
        scratch_shapes=[pltpu.SemaphoreType.DMA((n_remote,)), pltpu.SemaphoreType.DMA((n_remote,)),
                        pltpu.SemaphoreType.DMA((max(n_local, 1),))],
    )(*ins)


def gather_shards(shards):
    n = len(shards)

    def body(*refs):
        ins, outs = refs[:n], refs[n:2 * n]
        send_sems, recv_sems, local_sems = refs[2 * n:]
        x, y, c = lax.axis_index("x"), lax.axis_index("y"), lax.axis_index("c")
        me = 2 * x + y

        def half(t):
            rh = ins[t].shape[0] // 2
            return ins[t].at[pl.ds(pl.multiple_of(c * rh, 8), rh), :]

        own = [pltpu.make_async_copy(half(t), outs[t].at[me], local_sems.at[t]) for t in range(n)]
        for cp in own:
            cp.start()

        def copy(t, k, px, py, slot):
            return pltpu.make_async_remote_copy(
                src_ref=half(t), dst_ref=outs[t].at[slot], send_sem=send_sems.at[3 * t + k],
                recv_sem=recv_sems.at[3 * t + k], device_id=(px, py, c), device_id_type=MESH)

        chips = _other_chips(x, y)
        sends = [copy(t, k, px, py, me) for t in range(n) for k, (px, py) in enumerate(chips)]
        for cp in sends:
            cp.start()
        for t in range(n):
            for k, (px, py) in enumerate(chips):
                copy(t, k, px, py, 2 * px + py).wait_recv()
        for cp in sends:
            cp.wait_send()
        for cp in own:
            cp.wait()

    out_shapes = [jax.ShapeDtypeStruct((N_CHIPS, s.shape[0] // 2, s.shape[1]), s.dtype) for s in shards]
    return _comm_call(body, "gather_shards", shards, out_shapes, 3 * n, n)


def sibling_swap_halves(gs):
    n = len(gs)

    def body(*refs):
        ins, outs = refs[:n], refs[n:2 * n]
        send_sems, recv_sems, _ = refs[2 * n:]
        x, y, c = lax.axis_index("x"), lax.axis_index("y"), lax.axis_index("c")
        copies = []
        for t in range(n):
            rh = ins[t].shape[1] // 2
            src = ins[t].at[:, pl.ds(pl.multiple_of((1 - c) * rh, 8), rh), :]
            copies.append(pltpu.make_async_remote_copy(
                src_ref=src, dst_ref=outs[t], send_sem=send_sems.at[t], recv_sem=recv_sems.at[t],
                device_id=(x, y, 1 - c), device_id_type=MESH))
        for cp in copies:
            cp.start()
        for cp in copies:
            cp.wait_recv()
        for cp in copies:
            cp.wait_send()

    out_shapes = [jax.ShapeDtypeStruct((g.shape[0], g.shape[1] // 2, g.shape[2]), g.dtype) for g in gs]
    return _comm_call(body, "sibling_swap_halves", gs, out_shapes, n, 0)


def chip_exchange(ps):
    n = len(ps)

    def body(*refs):
        ins, outs = refs[:n], refs[n:2 * n]
        send_sems, recv_sems, local_sems = refs[2 * n:]
        x, y, c = lax.axis_index("x"), lax.axis_index("y"), lax.axis_index("c")
        me = 2 * x + y
        own = [pltpu.make_async_copy(ins[t].at[me], outs[t].at[me], local_sems.at[t]) for t in range(n)]
        for cp in own:
            cp.start()

        def copy(t, k, px, py, src_slot, dst_slot):
            return pltpu.make_async_remote_copy(
                src_ref=ins[t].at[src_slot], dst_ref=outs[t].at[dst_slot], send_sem=send_sems.at[3 * t + k],
                recv_sem=recv_sems.at[3 * t + k], device_id=(px, py, c), device_id_type=MESH)

        chips = _other_chips(x, y)
        sends = [copy(t, k, px, py, 2 * px + py, me) for t in range(n) for k, (px, py) in enumerate(chips)]
        for cp in sends:
            cp.start()
        for t in range(n):
            for k, (px, py) in enumerate(chips):
                copy(t, k, px, py, me, 2 * px + py).wait_recv()
        for cp in sends:
            cp.wait_send()
        for cp in own:
            cp.wait()

    out_shapes = [jax.ShapeDtypeStruct(p.shape, p.dtype) for p in ps]
    return _comm_call(body, "chip_exchange", ps, out_shapes, 3 * n, n)


def sibling_share(ss, name):
    n = len(ss)

    def body(*refs):
        ins, outs = refs[:n], refs[n:2 * n]
        send_sems, recv_sems, _ = refs[2 * n:]
        x, y, c = lax.axis_index("x"), lax.axis_index("y"), lax.axis_index("c")
        copies = [pltpu.make_async_remote_copy(
            src_ref=ins[t], dst_ref=outs[t], send_sem=send_sems.at[t], recv_sem=recv_sems.at[t],
            device_id=(x, y, 1 - c), device_id_type=MESH) for t in range(n)]
        for cp in copies:
            cp.start()
        for cp in copies:
            cp.wait_recv()
        for cp in copies:
            cp.wait_send()

    out_shapes = [jax.ShapeDtypeStruct(s.shape, s.dtype) for s in ss]
    return _comm_call(body, name, ss, out_shapes, n, 0)


EW_BLOCK_BYTES = 2 * 1024 * 1024


def _ew_rows(rows, cols, copies=1):
    padded = -(-cols // LANES) * LANES
    for tr in (1024, 512, 256, 128, 64, 32, 16, 8):
        if rows % tr == 0 and copies * tr * padded * 4 <= EW_BLOCK_BYTES:
            return tr
    return rows


def add_pairs(a, b, out_dtype=F32):
    rows, cols = a.shape
    tr = _ew_rows(rows, cols)

    def body(a_ref, b_ref, o_ref):
        o_ref[...] = (a_ref[...] + b_ref[...]).astype(o_ref.dtype)

    blk = pl.BlockSpec((tr, cols), lambda i: (i, 0))
    return pl.pallas_call(
        body, name="add_pairs", grid=(rows // tr,), in_specs=[blk, blk], out_specs=blk,
        out_shape=jax.ShapeDtypeStruct((rows, cols), out_dtype),
        compiler_params=_params(dimension_semantics=("arbitrary",)),
    )(a, b)


def sum_chips(slots):
    _, rows, cols = slots.shape
    tr = _ew_rows(rows, cols, N_CHIPS)

    def body(s_ref, o_ref):
        acc = s_ref[0].astype(F32)
        for j in range(1, N_CHIPS):
            acc = acc + s_ref[j].astype(F32)
        o_ref[...] = acc

    return pl.pallas_call(
        body, name="sum_chips", grid=(rows // tr,),
        in_specs=[pl.BlockSpec((N_CHIPS, tr, cols), lambda i: (0, i, 0))],
        out_specs=pl.BlockSpec((tr, cols), lambda i: (i, 0)),
        out_shape=jax.ShapeDtypeStruct((rows, cols), F32),
        compiler_params=_params(dimension_semantics=("arbitrary",)),
    )(slots)


def adamw(g, w, m, v):
    rows, cols = g.shape
    tr = _ew_rows(rows, cols)

    def body(g_ref, w_ref, m_ref, v_ref, d_out, m_out, v_out):
        gv = g_ref[...]
        m_new = ADAM_B1 * m_ref[...] + (1.0 - ADAM_B1) * gv
        v_new = ADAM_B2 * v_ref[...] + (1.0 - ADAM_B2) * jnp.square(gv)
        m_hat = m_new / (1.0 - ADAM_B1 ** ADAM_STEP)
        v_hat = v_new / (1.0 - ADAM_B2 ** ADAM_STEP)
        d_out[...] = -ADAM_LR * (m_hat / (jnp.sqrt(v_hat) + ADAM_EPS) + ADAM_WD * w_ref[...])
        m_out[...] = m_new
        v_out[...] = v_new

    blk = pl.BlockSpec((tr, cols), lambda i: (i, 0))
    return pl.pallas_call(
        body, name="adamw", grid=(rows // tr,), in_specs=[blk] * 4, out_specs=[blk] * 3,
        out_shape=[jax.ShapeDtypeStruct((rows, cols), F32)] * 3,
        compiler_params=_params(dimension_semantics=("arbitrary",)),
    )(g, w, m, v)


def reduce_gradients(quarters):
    theirs = sibling_swap_halves(quarters)
    c = lax.axis_index("c")
    chip_partials = []
    for n, (q, t) in enumerate(zip(quarters, theirs)):
        four, rh, cols = t.shape
        mine = lax.dynamic_slice_in_dim(q, c * rh, rh, axis=1)
        wire = F32 if n == len(quarters) - 1 else BF16
        chip_partials.append(add_pairs(mine.reshape(four * rh, cols), t.reshape(four * rh, cols), wire)
                             .reshape(four, rh, cols))
    slots = chip_exchange(chip_partials)
    mine = [sum_chips(s) for s in slots]
    return _join_halves(mine, sibling_share(mine, "share_gradient_halves"), axis=0)


def _join_halves(mine, theirs, axis):
    c = lax.axis_index("c")
    return [jnp.concatenate([jnp.where(c == 0, m, t), jnp.where(c == 0, t, m)], axis=axis)
            for m, t in zip(mine, theirs)]


def _pack(pieces, dtype, row_multiple):
    flat = jnp.concatenate([p.astype(dtype).reshape(-1) for p in pieces])
    per = row_multiple * LANES
    padded = -(-flat.shape[0] // per) * per
    flat = jnp.pad(flat, (0, padded - flat.shape[0]))
    return flat.reshape(-1, LANES)


def _unpack(buf, shapes):
    flat = buf.reshape(-1)
    out, off = [], 0
    for s in shapes:
        n = int(np.prod(s))
        out.append(flat[off:off + n].reshape(s))
        off += n
    return out


RET_SCALE = RET_QK_DIM ** -0.5
RET_LOG_GAMMA = [math.log(1.0 - 2.0 ** (-5.0 - h)) for h in range(RET_HEADS)]
RET_HALF = RET_QK_DIM // 2


def _ret_tables(length):
    inv = ROPE_BASE ** (-jnp.arange(RET_HALF, dtype=F32) / RET_HALF)
    ang = jnp.arange(length).astype(F32)[:, None] * inv[None, :]
    log_gamma = jnp.log(1.0 - jnp.power(2.0, -5.0 - jnp.arange(RET_HEADS, dtype=F32)))
    idx = jnp.arange(CHUNK, dtype=F32)
    rel = idx[:, None] - idx[None, :]
    dmask = jnp.where(rel >= 0, jnp.exp(log_gamma[:, None, None] * jnp.maximum(rel, 0.0)), 0.0)
    k_decay = jnp.exp(log_gamma[:, None] * (CHUNK - 1 - idx)[None, :])[:, :, None]
    q_decay = jnp.exp(log_gamma[:, None] * (idx + 1.0)[None, :])[:, :, None]
    return jnp.cos(ang), jnp.sin(ang), dmask, k_decay, q_decay


def _rot(x, cs, sn):
    x1, x2 = x[:, :RET_HALF], x[:, RET_HALF:]
    return jnp.concatenate([x1 * cs - x2 * sn, x1 * sn + x2 * cs], axis=1)


def _unrot(d, cs, sn):
    d1, d2 = d[:, :RET_HALF], d[:, RET_HALF:]
    return jnp.concatenate([d1 * cs + d2 * sn, d2 * cs - d1 * sn], axis=1)


def _sigmoid(x):
    return 1.0 / (1.0 + jnp.exp(-x))


_NT = (((1,), (1,)), ((), ()))
_TN = (((0,), (0,)), ((), ()))


def _ret_specs(nc, rev):
    ch = (lambda c: nc - 1 - c) if rev else (lambda c: c)
    row = lambda w: pl.BlockSpec((CHUNK, w), lambda c: (ch(c), 0))
    const3 = lambda a, b: pl.BlockSpec((RET_HEADS, a, b), lambda c: (0, 0, 0))
    tables = [row(RET_HALF), row(RET_HALF), const3(CHUNK, CHUNK), const3(CHUNK, 1), const3(CHUNK, 1)]
    state = pl.BlockSpec((None, RET_HEADS, RET_QK_DIM, RET_V_DIM), lambda c: (ch(c), 0, 0, 0))
    return row, tables, state


def ret_fwd(p, tables):
    length = p.shape[0]
    nc = length // CHUNK
    row, table_specs, state_spec = _ret_specs(nc, False)

    def body(p_ref, cos_ref, sin_ref, dm_ref, kd_ref, qd_ref, y_ref, ypre_ref, st_ref, r_scr):
        c = pl.program_id(0)

        @pl.when(c == 0)
        def _():
            r_scr[...] = jnp.zeros_like(r_scr)

        cs, sn = cos_ref[...], sin_ref[...]
        valid = (c * CHUNK + lax.broadcasted_iota(jnp.int32, (CHUNK, 1), 0)) >= PAD
        for h in range(RET_HEADS):
            col = lambda part: slice(part * D_MODEL + h * RET_QK_DIM, part * D_MODEL + (h + 1) * RET_QK_DIM)
            qb = _rot(p_ref[:, col(0)], cs, sn).astype(BF16)
            kr = _rot(p_ref[:, col(1)], cs, sn) * RET_SCALE
            kb = kr.astype(BF16)
            vb = jnp.where(valid, p_ref[:, col(2)], 0.0).astype(BF16)
            s = lax.dot_general(qb, kb, _NT, preferred_element_type=F32) * dm_ref[h]
            r = r_scr[h]
            st_ref[h] = r
            y = (jnp.dot(s.astype(BF16), vb, preferred_element_type=F32)
                 + jnp.dot(qb, r.astype(BF16), preferred_element_type=F32) * qd_ref[h])
            kdb = (kr * kd_ref[h]).astype(BF16)
            r_scr[h] = r * math.exp(RET_LOG_GAMMA[h] * CHUNK) + lax.dot_general(kdb, vb, _TN,
                                                                                preferred_element_type=F32)
            out = slice(h * RET_V_DIM, (h + 1) * RET_V_DIM)
            ypre_ref[:, out] = y
            mu = jnp.mean(y, axis=-1, keepdims=True)
            yc = y - mu
            yn = yc * lax.rsqrt(jnp.mean(yc * yc, axis=-1, keepdims=True) + EPS)
            g = p_ref[:, col(3)]
            y_ref[:, out] = (yn * (g * _sigmoid(g))).astype(y_ref.dtype)

    return pl.pallas_call(
        body, name="ret_fwd", grid=(nc,),
        in_specs=[row(4 * D_MODEL)] + table_specs,
        out_specs=[row(D_MODEL), row(D_MODEL), state_spec],
        out_shape=[jax.ShapeDtypeStruct((length, D_MODEL), BF16), jax.ShapeDtypeStruct((length, D_MODEL), F32),
                   jax.ShapeDtypeStruct((nc, RET_HEADS, RET_QK_DIM, RET_V_DIM), F32)],
        scratch_shapes=[pltpu.VMEM((RET_HEADS, RET_QK_DIM, RET_V_DIM), F32)],
        compiler_params=_params(dimension_semantics=("arbitrary",)),
    )(p, *tables)


def ret_bwd(p, tables, ypre, states, dyo):
    length = p.shape[0]
    nc = length // CHUNK
    row, table_specs, state_spec = _ret_specs(nc, True)

    def body(p_ref, cos_ref, sin_ref, dm_ref, kd_ref, qd_ref, ypre_ref, st_ref, dyo_ref, dp_ref, dr_scr):
        c = pl.program_id(0)

        @pl.when(c == 0)
        def _():
            dr_scr[...] = jnp.zeros_like(dr_scr)

        cs, sn = cos_ref[...], sin_ref[...]
        valid = ((nc - 1 - c) * CHUNK + lax.broadcasted_iota(jnp.int32, (CHUNK, 1), 0)) >= PAD
        for h in range(RET_HEADS):
            col = lambda part: slice(part * D_MODEL + h * RET_QK_DIM, part * D_MODEL + (h + 1) * RET_QK_DIM)
            out = slice(h * RET_V_DIM, (h + 1) * RET_V_DIM)
            qb = _rot(p_ref[:, col(0)], cs, sn).astype(BF16)
            kr = _rot(p_ref[:, col(1)], cs, sn) * RET_SCALE
            kb = kr.astype(BF16)
            vb = jnp.where(valid, p_ref[:, col(2)], 0.0).astype(BF16)
            g = p_ref[:, col(3)]
            y = ypre_ref[:, out]
            dyo_h = dyo_ref[:, out]
            mu = jnp.mean(y, axis=-1, keepdims=True)
            yc = y - mu
            rs = lax.rsqrt(jnp.mean(yc * yc, axis=-1, keepdims=True) + EPS)
            xh = yc * rs
            sg = _sigmoid(g)
            dp_ref[:, col(3)] = (dyo_h * xh * (sg * (1.0 + g * (1.0 - sg)))).astype(dp_ref.dtype)
            dyn = dyo_h * (g * sg)
            dy = rs * (dyn - jnp.mean(dyn, axis=-1, keepdims=True)
                       - xh * jnp.mean(dyn * xh, axis=-1, keepdims=True))
            dyb = dy.astype(BF16)
            dm = dm_ref[h]
            sm = (lax.dot_general(qb, kb, _NT, preferred_element_type=F32) * dm).astype(BF16)
            dsb = (lax.dot_general(dyb, vb, _NT, preferred_element_type=F32) * dm).astype(BF16)
            rb = st_ref[h].astype(BF16)
            dyqb = (dy * qd_ref[h]).astype(BF16)
            dr = dr_scr[h]
            drb = dr.astype(BF16)
            kd = kd_ref[h]
            dq = (jnp.dot(dsb, kb, preferred_element_type=F32)
                  + lax.dot_general(dyqb, rb, _NT, preferred_element_type=F32))
            dk = (lax.dot_general(dsb, qb, _TN, preferred_element_type=F32)
                  + lax.dot_general(vb, drb, _NT, preferred_element_type=F32) * kd)
            dv = (lax.dot_general(sm, dyb, _TN, preferred_element_type=F32)
                  + jnp.dot((kr * kd).astype(BF16), drb, preferred_element_type=F32))
            dr_scr[h] = dr * math.exp(RET_LOG_GAMMA[h] * CHUNK) + lax.dot_general(qb, dyqb, _TN,
                                                                                 preferred_element_type=F32)
            dp_ref[:, col(0)] = _unrot(dq, cs, sn).astype(dp_ref.dtype)
            dp_ref[:, col(1)] = (_unrot(dk, cs, sn) * RET_SCALE).astype(dp_ref.dtype)
            dp_ref[:, col(2)] = jnp.where(valid, dv, 0.0).astype(dp_ref.dtype)

    return pl.pallas_call(
        body, name="ret_bwd", grid=(nc,),
        in_specs=[row(4 * D_MODEL)] + table_specs + [row(D_MODEL), state_spec, row(D_MODEL)],
        out_specs=row(4 * D_MODEL),
        out_shape=jax.ShapeDtypeStruct((length, 4 * D_MODEL), BF16),
        scratch_shapes=[pltpu.VMEM((RET_HEADS, RET_QK_DIM, RET_V_DIM), F32)],
        compiler_params=_params(dimension_semantics=("arbitrary",)),
    )(p, *tables, ypre, states, dyo)


HALO = 8
HALO_PER_CHUNK = CHUNK // HALO


def _shift_down(cur, halo, m):
    if m == 0:
        return cur
    n = cur.shape[0]
    rows = lax.broadcasted_iota(jnp.int32, cur.shape, 0)
    edge = jnp.tile(pltpu.roll(halo, m, 0), (n // HALO, 1))
    return jnp.where(rows < m, edge, pltpu.roll(cur, m, 0))


def _shift_up(cur, halo, m):
    if m == 0:
        return cur
    n = cur.shape[0]
    rows = lax.broadcasted_iota(jnp.int32, cur.shape, 0)
    edge = jnp.tile(pltpu.roll(halo, HALO - m, 0), (n // HALO, 1))
    return jnp.where(rows >= n - m, edge, pltpu.roll(cur, n - m, 0))


def _gated_input(ref, row0):
    rows = row0 + lax.broadcasted_iota(jnp.int32, (ref.shape[0], 1), 0)
    return jnp.where((rows >= PAD) & (row0 >= 0), ref[:, D_MODEL:2 * D_MODEL] * ref[:, 2 * D_MODEL:], 0.0)


def conv_mixer_fwd(p, conv_w):
    length = p.shape[0]
    nc = length // CHUNK
    kt = conv_w.shape[0]

    def body(cur_ref, prev_ref, w_ref, y_ref):
        c = pl.program_id(0)
        u_cur = _gated_input(cur_ref, c * CHUNK)
        u_prev = _gated_input(prev_ref, c * CHUNK - HALO)
        acc = jnp.zeros((CHUNK, D_MODEL), F32)
        for i in range(kt):
            acc = acc + _shift_down(u_cur, u_prev, kt - 1 - i) * w_ref[i:i + 1, :]
        y_ref[...] = (cur_ref[:, :D_MODEL] * acc).astype(y_ref.dtype)

    return pl.pallas_call(
        body, name="conv_mixer_fwd", grid=(nc,),
        in_specs=[pl.BlockSpec((CHUNK, 3 * D_MODEL), lambda c: (c, 0)),
                  pl.BlockSpec((HALO, 3 * D_MODEL), lambda c: (jnp.maximum(c * HALO_PER_CHUNK - 1, 0), 0)),
                  pl.BlockSpec((kt, D_MODEL), lambda c: (0, 0))],
        out_specs=pl.BlockSpec((CHUNK, D_MODEL), lambda c: (c, 0)),
        out_shape=jax.ShapeDtypeStruct((length, D_MODEL), BF16),
        compiler_params=_params(dimension_semantics=("arbitrary",)),
    )(p, p, conv_w)


def conv_mixer_bwd(p, conv_w, dy):
    length = p.shape[0]
    nc = length // CHUNK
    kt = conv_w.shape[0]

    def body(cur_ref, prev_ref, w_ref, dy_ref, dyn_ref, pn_ref, dp_ref, dw_ref):
        c = pl.program_id(0)
        rows = lax.broadcasted_iota(jnp.int32, (CHUNK, 1), 0)
        u_cur = _gated_input(cur_ref, c * CHUNK)
        u_prev = _gated_input(prev_ref, c * CHUNK - HALO)
        b_gate = cur_ref[:, :D_MODEL]
        dyv = dy_ref[...]
        dconv = dyv * b_gate
        dconv_next = jnp.where(c + 1 < nc, dyn_ref[...] * pn_ref[:, :D_MODEL], 0.0)

        @pl.when(c == 0)
        def _():
            dw_ref[...] = jnp.zeros_like(dw_ref)

        acc = jnp.zeros((CHUNK, D_MODEL), F32)
        du = jnp.zeros((CHUNK, D_MODEL), F32)
        for i in range(kt):
            shifted = _shift_down(u_cur, u_prev, kt - 1 - i)
            acc = acc + shifted * w_ref[i:i + 1, :]
            dw_ref[i:i + 1, :] += jnp.sum(dconv * shifted, axis=0, keepdims=True)
            du = du + _shift_up(dconv, dconv_next, kt - 1 - i) * w_ref[i:i + 1, :]
        du = jnp.where(c * CHUNK + rows >= PAD, du, 0.0)
        dp_ref[:, :D_MODEL] = (dyv * acc).astype(dp_ref.dtype)
        dp_ref[:, D_MODEL:2 * D_MODEL] = (du * cur_ref[:, 2 * D_MODEL:]).astype(dp_ref.dtype)
        dp_ref[:, 2 * D_MODEL:] = (du * cur_ref[:, D_MODEL:2 * D_MODEL]).astype(dp_ref.dtype)

    nxt = lambda c: (jnp.minimum((c + 1) * HALO_PER_CHUNK, length // HALO - 1), 0)
    return pl.pallas_call(
        body, name="conv_mixer_bwd", grid=(nc,),
        in_specs=[pl.BlockSpec((CHUNK, 3 * D_MODEL), lambda c: (c, 0)),
                  pl.BlockSpec((HALO, 3 * D_MODEL), lambda c: (jnp.maximum(c * HALO_PER_CHUNK - 1, 0), 0)),
                  pl.BlockSpec((kt, D_MODEL), lambda c: (0, 0)),
                  pl.BlockSpec((CHUNK, D_MODEL), lambda c: (c, 0)),
                  pl.BlockSpec((HALO, D_MODEL), nxt),
                  pl.BlockSpec((HALO, 3 * D_MODEL), nxt)],
        out_specs=[pl.BlockSpec((CHUNK, 3 * D_MODEL), lambda c: (c, 0)),
                   pl.BlockSpec((kt, D_MODEL), lambda c: (0, 0))],
        out_shape=[jax.ShapeDtypeStruct((length, 3 * D_MODEL), BF16), jax.ShapeDtypeStruct((kt, D_MODEL), F32)],
        compiler_params=_params(dimension_semantics=("arbitrary",)),
    )(p, p, conv_w, dy, dy, p)


def merge_fwd(gate_logits, ups):
    length = gate_logits.shape[0]
    tm = _tile(length, (384, 256, 128))

    def body(g_ref, u0, u1, u2, u3, o_ref):
        acc = jnp.zeros((tm, D_MODEL), F32)
        for n, u in enumerate((u0, u1, u2, u3)):
            acc = acc + _sigmoid(g_ref[:, n * D_MODEL:(n + 1) * D_MODEL]) * u[...]
        o_ref[...] = acc.astype(o_ref.dtype)

    row = pl.BlockSpec((tm, D_MODEL), lambda i: (i, 0))
    return pl.pallas_call(
        body, name="merge_fwd", grid=(length // tm,),
        in_specs=[pl.BlockSpec((tm, N_BRANCH * D_MODEL), lambda i: (i, 0))] + [row] * N_BRANCH,
        out_specs=row, out_shape=jax.ShapeDtypeStruct((length, D_MODEL), BF16),
        compiler_params=_params(dimension_semantics=("arbitrary",)),
    )(gate_logits, *ups)


def merge_bwd(gate_logits, ups, dmerged):
    length = gate_logits.shape[0]
    tm = _tile(length, (384, 256, 128))

    def body(g_ref, u0, u1, u2, u3, dm_ref, dg_ref, d0, d1, d2, d3):
        dm = dm_ref[...]
        for n, (u, du) in enumerate(((u0, d0), (u1, d1), (u2, d2), (u3, d3))):
            cols = slice(n * D_MODEL, (n + 1) * D_MODEL)
            s = _sigmoid(g_ref[:, cols])
            du[...] = (dm * s).astype(du.dtype)
            dg_ref[:, cols] = (dm * u[...] * (s * (1.0 - s))).astype(dg_ref.dtype)

    row = pl.BlockSpec((tm, D_MODEL), lambda i: (i, 0))
    wide = pl.BlockSpec((tm, N_BRANCH * D_MODEL), lambda i: (i, 0))
    outs = pl.pallas_call(
        body, name="merge_bwd", grid=(length // tm,),
        in_specs=[wide] + [row] * (N_BRANCH + 1),
        out_specs=[wide] + [row] * N_BRANCH,
        out_shape=[jax.ShapeDtypeStruct((length, N_BRANCH * D_MODEL), BF16)]
        + [jax.ShapeDtypeStruct((length, D_MODEL), BF16)] * N_BRANCH,
        compiler_params=_params(dimension_semantics=("arbitrary",)),
    )(gate_logits, *ups, dmerged)
    return outs[0], list(outs[1:])


def swiglu_fwd(f):
    length = f.shape[0]
    tm = _tile(length, (384, 256, 128))

    def body(f_ref, o_ref):
        a = f_ref[:, :D_FF]
        o_ref[...] = (a * _sigmoid(a) * f_ref[:, D_FF:]).astype(o_ref.dtype)

    return pl.pallas_call(
        body, name="swiglu_fwd", grid=(length // tm,),
        in_specs=[pl.BlockSpec((tm, 2 * D_FF), lambda i: (i, 0))],
        out_specs=pl.BlockSpec((tm, D_FF), lambda i: (i, 0)),
        out_shape=jax.ShapeDtypeStruct((length, D_FF), BF16),
        compiler_params=_params(dimension_semantics=("arbitrary",)),
    )(f)


def swiglu_bwd(f, dact):
    length = f.shape[0]
    tm = _tile(length, (384, 256, 128))

    def body(f_ref, d_ref, df_ref):
        a = f_ref[:, :D_FF]
        up = f_ref[:, D_FF:]
        d = d_ref[...]
        s = _sigmoid(a)
        df_ref[:, :D_FF] = (d * up * (s * (1.0 + a * (1.0 - s)))).astype(df_ref.dtype)
        df_ref[:, D_FF:] = (d * (a * s)).astype(df_ref.dtype)

    return pl.pallas_call(
        body, name="swiglu_bwd", grid=(length // tm,),
        in_specs=[pl.BlockSpec((tm, 2 * D_FF), lambda i: (i, 0)), pl.BlockSpec((tm, D_FF), lambda i: (i, 0))],
        out_specs=pl.BlockSpec((tm, 2 * D_FF), lambda i: (i, 0)),
        out_shape=jax.ShapeDtypeStruct((length, 2 * D_FF), BF16),
        compiler_params=_params(dimension_semantics=("arbitrary",)),
    )(f, dact)


SSD_PAIRS = SSD_HEADS // 2
SSD_XBC = SSD_CONV_DIM
SSD_GW = SSD_INNER // SSD_GROUPS


def _split3(x):
    h1 = x.astype(BF16)
    r1 = x - h1.astype(F32)
    h2 = r1.astype(BF16)
    h3 = (r1 - h2.astype(F32)).astype(BF16)
    return h1, h2, h3


def _tri_apply(tri, x, dims):
    out = None
    for part in _split3(x):
        t = lax.dot_general(tri, part, dims, preferred_element_type=F32)
        out = t if out is None else out + t
    return out


def _softplus(x):
    return jnp.maximum(x, 0.0) + jnp.log(1.0 + jnp.exp(-jnp.abs(x)))


def _lane_pair(x, pair):
    lanes = lax.broadcasted_iota(jnp.int32, (x.shape[0], LANES), 1)
    return jnp.where(lanes < SSD_HEAD_DIM, x[:, 2 * pair:2 * pair + 1], x[:, 2 * pair + 1:2 * pair + 2])


def _half_sums(t):
    lanes = lax.broadcasted_iota(jnp.int32, t.shape, 1)
    lo = jnp.sum(jnp.where(lanes < SSD_HEAD_DIM, t, 0.0), axis=1, keepdims=True)
    return lo, jnp.sum(t, axis=1, keepdims=True) - lo


def _put_cols(cols):
    rows = cols[0].shape[0]
    lanes = lax.broadcasted_iota(jnp.int32, (rows, LANES), 1)
    out = jnp.zeros((rows, LANES), F32)
    for h, col in enumerate(cols):
        out = out + jnp.where(lanes == h, col, 0.0)
    return out


def ssd_pre_fwd(p, dt_raw, conv_w, conv_b, dt_bias, a_log):
    length = p.shape[0]
    nc = length // CHUNK
    kt = conv_w.shape[0]

    def body(cur_ref, prev_ref, raw_ref, w_ref, b_ref, bias_ref, alog_ref, act_ref, dt_ref, a_ref):
        c = pl.program_id(0)
        rows = lax.broadcasted_iota(jnp.int32, (CHUNK, 1), 0)
        vm = c * CHUNK + rows >= PAD
        u_cur = jnp.where(vm, cur_ref[:, SSD_INNER:], 0.0)
        u_prev = jnp.where((c * CHUNK - HALO + rows[:HALO] >= PAD) & (c >= 1), prev_ref[:, SSD_INNER:], 0.0)
        pre = jnp.zeros((CHUNK, SSD_XBC), F32) + b_ref[...]
        for i in range(kt):
            pre = pre + _shift_down(u_cur, u_prev, kt - 1 - i) * w_ref[i:i + 1, :]
        act = pre * _sigmoid(pre)
        act_ref[:, :SSD_INNER] = jnp.where(vm, act[:, :SSD_INNER], 0.0)
        act_ref[:, SSD_INNER:] = act[:, SSD_INNER:]
        dt = _softplus(raw_ref[...] + bias_ref[...])
        dt_ref[...] = dt
        a_ref[...] = -jnp.exp(alog_ref[...]) * dt

    row = lambda w: pl.BlockSpec((CHUNK, w), lambda c: (c, 0))
    vec = lambda w: pl.BlockSpec((1, w), lambda c: (0, 0))
    return pl.pallas_call(
        body, name="ssd_pre_fwd", grid=(nc,),
        in_specs=[row(3 * D_MODEL),
                  pl.BlockSpec((HALO, 3 * D_MODEL), lambda c: (jnp.maximum(c * HALO_PER_CHUNK - 1, 0), 0)),
                  row(LANES), pl.BlockSpec((kt, SSD_XBC), lambda c: (0, 0)), vec(SSD_XBC), vec(LANES), vec(LANES)],
        out_specs=[row(SSD_XBC), row(LANES), row(LANES)],
        out_shape=[jax.ShapeDtypeStruct((length, SSD_XBC), F32), jax.ShapeDtypeStruct((length, LANES), F32),
                   jax.ShapeDtypeStruct((length, LANES), F32)],
        compiler_params=_params(dimension_semantics=("arbitrary",)),
    )(p, p, dt_raw, conv_w, conv_b, dt_bias, a_log)


def ssd_pre_bwd(p, dt_raw, conv_w, conv_b, dt_bias, a_log, dact, ddt, da, dz):
    length = p.shape[0]
    nc = length // CHUNK
    kt = conv_w.shape[0]

    def body(cur_ref, prev_ref, raw_ref, w_ref, b_ref, bias_ref, alog_ref, dact_ref, ddt_ref, da_ref, dz_ref,
             dp_ref, draw_ref, dw_ref, db_ref, dbias_ref, dalog_ref, dpre_next):
        step = pl.program_id(0)
        c = nc - 1 - step
        rows = lax.broadcasted_iota(jnp.int32, (CHUNK, 1), 0)
        vm = c * CHUNK + rows >= PAD

        @pl.when(step == 0)
        def _():
            dpre_next[...] = jnp.zeros_like(dpre_next)
            dw_ref[...] = jnp.zeros_like(dw_ref)
            db_ref[...] = jnp.zeros_like(db_ref)
            dbias_ref[...] = jnp.zeros_like(dbias_ref)
            dalog_ref[...] = jnp.zeros_like(dalog_ref)

        u_cur = jnp.where(vm, cur_ref[:, SSD_INNER:], 0.0)
        u_prev = jnp.where((c * CHUNK - HALO + rows[:HALO] >= PAD) & (c >= 1), prev_ref[:, SSD_INNER:], 0.0)
        shifted = [_shift_down(u_cur, u_prev, kt - 1 - i) for i in range(kt)]
        pre = jnp.zeros((CHUNK, SSD_XBC), F32) + b_ref[...]
        for i in range(kt):
            pre = pre + shifted[i] * w_ref[i:i + 1, :]
        sg = _sigmoid(pre)
        lanes = lax.broadcasted_iota(jnp.int32, (CHUNK, SSD_XBC), 1)
        dact_v = jnp.where(vm | (lanes >= SSD_INNER), dact_ref[...], 0.0)
        dpre = dact_v * (sg * (1.0 + pre * (1.0 - sg)))
        db_ref[...] += jnp.sum(dpre, axis=0, keepdims=True)
        nxt = dpre_next[...]
        du = jnp.zeros((CHUNK, SSD_XBC), F32)
        for i in range(kt):
            dw_ref[i:i + 1, :] += jnp.sum(dpre * shifted[i], axis=0, keepdims=True)
            du = du + _shift_up(dpre, nxt, kt - 1 - i) * w_ref[i:i + 1, :]
        dpre_next[...] = dpre[:HALO]
        dp_ref[:, :SSD_INNER] = dz_ref[...].astype(dp_ref.dtype)
        dp_ref[:, SSD_INNER:] = jnp.where(vm, du, 0.0).astype(dp_ref.dtype)
        x = raw_ref[...] + bias_ref[...]
        neg_exp = -jnp.exp(alog_ref[...])
        dav = da_ref[...]
        draw = (ddt_ref[...] + dav * neg_exp) * _sigmoid(x)
        draw_ref[...] = draw.astype(draw_ref.dtype)
        dbias_ref[...] += jnp.sum(draw, axis=0, keepdims=True)
        dalog_ref[...] += jnp.sum(dav * (neg_exp * _softplus(x)), axis=0, keepdims=True)

    rev = lambda c: (nc - 1 - c, 0)
    row = lambda w: pl.BlockSpec((CHUNK, w), rev)
    vec = lambda w: pl.BlockSpec((1, w), lambda c: (0, 0))
    taps = pl.BlockSpec((kt, SSD_XBC), lambda c: (0, 0))
    return pl.pallas_call(
        body, name="ssd_pre_bwd", grid=(nc,),
        in_specs=[row(3 * D_MODEL),
                  pl.BlockSpec((HALO, 3 * D_MODEL),
                               lambda c: (jnp.maximum((nc - 1 - c) * HALO_PER_CHUNK - 1, 0), 0)),
                  row(LANES), taps, vec(SSD_XBC), vec(LANES), vec(LANES),
                  row(SSD_XBC), row(LANES), row(LANES), row(SSD_INNER)],
        out_specs=[row(3 * D_MODEL), row(LANES), taps, vec(SSD_XBC), vec(LANES), vec(LANES)],
        out_shape=[jax.ShapeDtypeStruct((length, 3 * D_MODEL), BF16), jax.ShapeDtypeStruct((length, LANES), BF16),
                   jax.ShapeDtypeStruct((kt, SSD_XBC), F32), jax.ShapeDtypeStruct((1, SSD_XBC), F32),
                   jax.ShapeDtypeStruct((1, LANES), F32), jax.ShapeDtypeStruct((1, LANES), F32)],
        scratch_shapes=[pltpu.VMEM((HALO, SSD_XBC), F32)],
        compiler_params=_params(dimension_semantics=("arbitrary",)),
    )(p, p, dt_raw, conv_w, conv_b, dt_bias, a_log, dact, ddt, da, dz)


def _tri_apply_lhs_t(x, tri):
    out = None
    for part in _split3(x):
        t = lax.dot_general(part, tri, (((0,), (1,)), ((), ())), preferred_element_type=F32)
        out = t if out is None else out + t
    return out


def ssd_core_fwd(act, dt, a, d_skip):
    length = act.shape[0]
    nc = length // CHUNK

    def body(act_ref, dt_ref, a_ref, dskip_ref, y_ref, st_ref, h_scr):
        c = pl.program_id(0)

        @pl.when(c == 0)
        def _():
            h_scr[...] = jnp.zeros_like(h_scr)

        r = lax.broadcasted_iota(jnp.int32, (CHUNK, CHUNK), 0)
        s = lax.broadcasted_iota(jnp.int32, (CHUNK, CHUNK), 1)
        causal = r >= s
        incl = jnp.where(causal, 1.0, 0.0).astype(BF16)
        a_v = a_ref[...]
        acs = _tri_apply(incl, a_v, (((1,), (0,)), ((), ())))
        acs_t = _tri_apply_lhs_t(a_v, incl)
        dt_v = dt_ref[...]
        lanes = lax.broadcasted_iota(jnp.int32, (CHUNK, LANES), 1)
        low = lanes < SSD_HEAD_DIM
        for g in range(SSD_GROUPS):
            bg = act_ref[:, SSD_INNER + g * SSD_STATE:SSD_INNER + (g + 1) * SSD_STATE].astype(BF16)
            cg = act_ref[:, SSD_INNER + (SSD_GROUPS + g) * SSD_STATE:
                         SSD_INNER + (SSD_GROUPS + g + 1) * SSD_STATE].astype(BF16)
            cb = lax.dot_general(cg, bg, _NT, preferred_element_type=F32)
            for pair in (2 * g, 2 * g + 1):
                cols = slice(pair * LANES, (pair + 1) * LANES)
                xs = act_ref[:, cols]
                x = xs * _lane_pair(dt_v, pair)
                ydiag = jnp.zeros((CHUNK, LANES), F32)
                for k, keep in ((0, low), (1, ~low)):
                    h = 2 * pair + k
                    seg = jnp.where(causal, jnp.exp(acs[:, h:h + 1] - acs_t[h:h + 1, :]), 0.0)
                    ydiag = ydiag + jnp.dot((cb * seg).astype(BF16), jnp.where(keep, x, 0.0).astype(BF16),
                                            preferred_element_type=F32)
                acs_p = _lane_pair(acs, pair)
                last = acs_p[CHUNK - 1:CHUNK, :]
                xds = (x * jnp.exp(last - acs_p)).astype(BF16)
                hprev = h_scr[pair]
                st_ref[pair] = hprev
                yoff = lax.dot_general(cg, hprev.astype(BF16), _NT, preferred_element_type=F32) * jnp.exp(acs_p)
                prow = lax.broadcasted_iota(jnp.int32, (LANES, 1), 0)
                cd = jnp.where(prow < SSD_HEAD_DIM, jnp.exp(acs_t[2 * pair:2 * pair + 1, CHUNK - 1:CHUNK]),
                               jnp.exp(acs_t[2 * pair + 1:2 * pair + 2, CHUNK - 1:CHUNK]))
                h_scr[pair] = hprev * cd + lax.dot_general(xds, bg, _TN, preferred_element_type=F32)
                y_ref[:, cols] = ydiag + yoff + xs * dskip_ref[:, cols]

    row = lambda w: pl.BlockSpec((CHUNK, w), lambda c: (c, 0))
    return pl.pallas_call(
        body, name="ssd_core_fwd", grid=(nc,),
        in_specs=[row(SSD_XBC), row(LANES), row(LANES), pl.BlockSpec((1, SSD_INNER), lambda c: (0, 0))],
        out_specs=[row(SSD_INNER), pl.BlockSpec((None, SSD_PAIRS, LANES, SSD_STATE), lambda c: (c, 0, 0, 0))],
        out_shape=[jax.ShapeDtypeStruct((length, SSD_INNER), F32),
                   jax.ShapeDtypeStruct((nc, SSD_PAIRS, LANES, SSD_STATE), F32)],
        scratch_shapes=[pltpu.VMEM((SSD_PAIRS, LANES, SSD_STATE), F32)],
        compiler_params=_params(dimension_semantics=("arbitrary",)),
    )(act, dt, a, d_skip)


def ssd_core_bwd(act, dt, a, d_skip, states, dy):
    length = act.shape[0]
    nc = length // CHUNK

    def body(act_ref, dt_ref, a_ref, dskip_ref, st_ref, dy_ref, dact_ref, ddt_ref, da_ref, dds_ref, dh_scr):
        step = pl.program_id(0)

        @pl.when(step == 0)
        def _():
            dh_scr[...] = jnp.zeros_like(dh_scr)
            dds_ref[...] = jnp.zeros_like(dds_ref)

        r = lax.broadcasted_iota(jnp.int32, (CHUNK, CHUNK), 0)
        s = lax.broadcasted_iota(jnp.int32, (CHUNK, CHUNK), 1)
        causal = r >= s
        incl = jnp.where(causal, 1.0, 0.0).astype(BF16)
        a_v = a_ref[...]
        acs = _tri_apply(incl, a_v, (((1,), (0,)), ((), ())))
        acs_t = _tri_apply_lhs_t(a_v, incl)
        dt_v = dt_ref[...]
        lanes = lax.broadcasted_iota(jnp.int32, (CHUNK, LANES), 1)
        low = lanes < SSD_HEAD_DIM
        prow = lax.broadcasted_iota(jnp.int32, (LANES, 1), 0)
        is_last = lax.broadcasted_iota(jnp.int32, (CHUNK, 1), 0) == CHUNK - 1
        dacs_cols = [None] * SSD_HEADS
        dacs_rows = [None] * SSD_HEADS
        ddt_cols = [None] * SSD_HEADS
        for g in range(SSD_GROUPS):
            b_cols = slice(SSD_INNER + g * SSD_STATE, SSD_INNER + (g + 1) * SSD_STATE)
            c_cols = slice(SSD_INNER + (SSD_GROUPS + g) * SSD_STATE, SSD_INNER + (SSD_GROUPS + g + 1) * SSD_STATE)
            bg = act_ref[:, b_cols].astype(BF16)
            cg = act_ref[:, c_cols].astype(BF16)
            cb = lax.dot_general(cg, bg, _NT, preferred_element_type=F32)
            dcb = jnp.zeros((CHUNK, CHUNK), F32)
            dbg = jnp.zeros((CHUNK, SSD_STATE), F32)
            dcg = jnp.zeros((CHUNK, SSD_STATE), F32)
            for pair in (2 * g, 2 * g + 1):
                cols = slice(pair * LANES, (pair + 1) * LANES)
                xs = act_ref[:, cols]
                dtp = _lane_pair(dt_v, pair)
                x = xs * dtp
                xb = x.astype(BF16)
                dyv = dy_ref[:, cols]
                dyb = dyv.astype(BF16)
                dds_ref[:, cols] += jnp.sum(dyv * xs, axis=0, keepdims=True)
                acs_p = _lane_pair(acs, pair)
                last = acs_p[CHUNK - 1:CHUNK, :]
                ds = jnp.exp(last - acs_p)
                ea = jnp.exp(acs_p)
                hprev = st_ref[pair]
                hb = hprev.astype(BF16)
                dh = dh_scr[pair]
                dhb = dh.astype(BF16)
                dx = jnp.zeros((CHUNK, LANES), F32)
                for k, keep in ((0, low), (1, ~low)):
                    h = 2 * pair + k
                    seg = jnp.where(causal, jnp.exp(acs[:, h:h + 1] - acs_t[h:h + 1, :]), 0.0)
                    lmat = cb * seg
                    dl = lax.dot_general(jnp.where(keep, dyv, 0.0).astype(BF16), xb, _NT,
                                         preferred_element_type=F32)
                    dcb = dcb + dl * seg
                    t = dl * lmat
                    dacs_cols[h] = jnp.sum(t, axis=1, keepdims=True)
                    dacs_rows[h] = jnp.sum(t, axis=0, keepdims=True)
                    dx = dx + jnp.where(keep, lax.dot_general(lmat.astype(BF16), dyb, _TN,
                                                              preferred_element_type=F32), 0.0)
                yoff = lax.dot_general(cg, hb, _NT, preferred_element_type=F32) * ea
                dm = (dyv * ea).astype(BF16)
                dcg = dcg + jnp.dot(dm, hb, preferred_element_type=F32)
                dxds = lax.dot_general(bg, dhb, _NT, preferred_element_type=F32)
                xds = x * ds
                dbg = dbg + jnp.dot(xds.astype(BF16), dhb, preferred_element_type=F32)
                dx = dx + dxds * ds
                t_ds = dxds * xds
                e_a = jnp.exp(acs_t[2 * pair:2 * pair + 1, CHUNK - 1:CHUNK])
                e_b = jnp.exp(acs_t[2 * pair + 1:2 * pair + 2, CHUNK - 1:CHUNK])
                cd = jnp.where(prow < SSD_HEAD_DIM, e_a, e_b)
                hd = dh * hprev
                dcd_a = jnp.sum(jnp.where(prow < SSD_HEAD_DIM, hd, 0.0), keepdims=True)
                dcd_b = jnp.sum(hd, keepdims=True) - dcd_a
                dh_scr[pair] = dh * cd + lax.dot_general(dm, cg, _TN, preferred_element_type=F32)
                col_lo, col_hi = _half_sums(dyv * yoff - t_ds)
                tot_lo, tot_hi = _half_sums(jnp.sum(t_ds, axis=0, keepdims=True))
                dacs_cols[2 * pair] += col_lo + jnp.where(is_last, tot_lo + dcd_a.reshape(1, 1) * e_a, 0.0)
                dacs_cols[2 * pair + 1] += col_hi + jnp.where(is_last, tot_hi + dcd_b.reshape(1, 1) * e_b, 0.0)
                dact_ref[:, cols] = dyv * dskip_ref[:, cols] + dx * dtp
                ddt_cols[2 * pair], ddt_cols[2 * pair + 1] = _half_sums(dx * xs)
            dcbb = dcb.astype(BF16)
            dact_ref[:, b_cols] = dbg + lax.dot_general(dcbb, cg, _TN, preferred_element_type=F32)
            dact_ref[:, c_cols] = dcg + jnp.dot(dcbb, bg, preferred_element_type=F32)
        ddt_ref[...] = _put_cols(ddt_cols)
        sub = lax.broadcasted_iota(jnp.int32, (LANES, CHUNK), 0)
        rows_mat = jnp.zeros((LANES, CHUNK), F32)
        for h in range(SSD_HEADS):
            rows_mat = rows_mat + jnp.where(sub == h, dacs_rows[h], 0.0)
        dacs = _put_cols(dacs_cols) - rows_mat.T
        da_ref[...] = _tri_apply(incl, dacs, (((0,), (0,)), ((), ())))

    rev = lambda c: (nc - 1 - c, 0)
    row = lambda w: pl.BlockSpec((CHUNK, w), rev)
    lane_vec = pl.BlockSpec((1, SSD_INNER), lambda c: (0, 0))
    return pl.pallas_call(
        body, name="ssd_core_bwd", grid=(nc,),
        in_specs=[row(SSD_XBC), row(LANES), row(LANES), lane_vec,
                  pl.BlockSpec((None, SSD_PAIRS, LANES, SSD_STATE), lambda c: (nc - 1 - c, 0, 0, 0)),
                  row(SSD_INNER)],
        out_specs=[row(SSD_XBC), row(LANES), row(LANES), lane_vec],
        out_shape=[jax.ShapeDtypeStruct((length, SSD_XBC), F32), jax.ShapeDtypeStruct((length, LANES), F32),
                   jax.ShapeDtypeStruct((length, LANES), F32), jax.ShapeDtypeStruct((1, SSD_INNER), F32)],
        scratch_shapes=[pltpu.VMEM((SSD_PAIRS, LANES, SSD_STATE), F32)],
        compiler_params=_params(dimension_semantics=("arbitrary",)),
    )(act, dt, a, d_skip, states, dy)


def ssd_post_fwd(y, p, norm_w):
    length = y.shape[0]
    tm = _tile(length, (384, 256, 128))

    def body(y_ref, p_ref, w_ref, o_ref):
        for g in range(SSD_GROUPS):
            cols = slice(g * SSD_GW, (g + 1) * SSD_GW)
            z = p_ref[:, cols]
            v = y_ref[:, cols] * (z * _sigmoid(z))
            o_ref[:, cols] = (v * lax.rsqrt(jnp.mean(v * v, axis=-1, keepdims=True) + EPS)
                              * w_ref[:, cols]).astype(o_ref.dtype)

    return pl.pallas_call(
        body, name="ssd_post_fwd", grid=(length // tm,),
        in_specs=[pl.BlockSpec((tm, SSD_INNER), lambda i: (i, 0)), pl.BlockSpec((tm, SSD_INNER), lambda i: (i, 0)),
                  pl.BlockSpec((1, SSD_INNER), lambda i: (0, 0))],
        out_specs=pl.BlockSpec((tm, SSD_INNER), lambda i: (i, 0)),
        out_shape=jax.ShapeDtypeStruct((length, SSD_INNER), BF16),
        compiler_params=_params(dimension_semantics=("arbitrary",)),
    )(y, p, norm_w)


def ssd_post_bwd(y, p, norm_w, dout):
    length = y.shape[0]
    tm = _tile(length, (384, 256, 128))

    def body(y_ref, p_ref, w_ref, do_ref, dy_ref, dz_ref, dw_ref):
        @pl.when(pl.program_id(0) == 0)
        def _():
            dw_ref[...] = jnp.zeros_like(dw_ref)

        for g in range(SSD_GROUPS):
            cols = slice(g * SSD_GW, (g + 1) * SSD_GW)
            z = p_ref[:, cols]
            yv = y_ref[:, cols]
            sg = _sigmoid(z)
            v = yv * (z * sg)
            rs = lax.rsqrt(jnp.mean(v * v, axis=-1, keepdims=True) + EPS)
            vh = v * rs
            do = do_ref[:, cols]
            dw_ref[:, cols] += jnp.sum(do * vh, axis=0, keepdims=True)
            dvh = do * w_ref[:, cols]
            dv = rs * (dvh - vh * jnp.mean(dvh * vh, axis=-1, keepdims=True))
            dy_ref[:, cols] = dv * (z * sg)
            dz_ref[:, cols] = dv * yv * (sg * (1.0 + z * (1.0 - sg)))

    blk = pl.BlockSpec((tm, SSD_INNER), lambda i: (i, 0))
    vec = pl.BlockSpec((1, SSD_INNER), lambda i: (0, 0))
    return pl.pallas_call(
        body, name="ssd_post_bwd", grid=(length // tm,),
        in_specs=[blk, blk, vec, blk], out_specs=[blk, blk, vec],
        out_shape=[jax.ShapeDtypeStruct((length, SSD_INNER), F32), jax.ShapeDtypeStruct((length, SSD_INNER), F32),
                   jax.ShapeDtypeStruct((1, SSD_INNER), F32)],
        compiler_params=_params(dimension_semantics=("arbitrary",)),
    )(y, p, norm_w, dout)


def _ssd_rows(lw):
    pad = lambda v: jnp.pad(v, (0, LANES - SSD_HEADS))[None]
    return dict(conv_w=lw['ssd_conv_w'], conv_b=lw['ssd_conv_b'][None], dt_bias=pad(lw['ssd_dt_bias']),
                a_log=pad(lw['ssd_a_log']), d_skip=jnp.repeat(lw['ssd_d'], SSD_HEAD_DIM)[None],
                norm_w=lw['ssd_norm'][None])


def ssd_fwd(p, dt_raw, rows):
    act, dt, a = ssd_pre_fwd(p, dt_raw, rows['conv_w'], rows['conv_b'], rows['dt_bias'], rows['a_log'])
    y, states = ssd_core_fwd(act, dt, a, rows['d_skip'])
    return ssd_post_fwd(y, p, rows['norm_w']), (act, dt, a, y, states)


def ssd_bwd(p, dt_raw, rows, saved, dout):
    act, dt, a, y, states = saved
    dy, dz, dnorm = ssd_post_bwd(y, p, rows['norm_w'], dout)
    dact, ddt, da, dskip_lanes = ssd_core_bwd(act, dt, a, rows['d_skip'], states, dy)
    dp, draw, dconv_w, dconv_b, dbias, dalog = ssd_pre_bwd(
        p, dt_raw, rows['conv_w'], rows['conv_b'], rows['dt_bias'], rows['a_log'], dact, ddt, da, dz)
    grads = dict(ssd_conv_w=dconv_w, ssd_conv_b=dconv_b[0], ssd_dt_bias=dbias[0, :SSD_HEADS],
                 ssd_a_log=dalog[0, :SSD_HEADS], ssd_norm=dnorm[0],
                 ssd_d=jnp.sum(dskip_lanes.reshape(SSD_HEADS, SSD_HEAD_DIM), axis=1))
    return dp, draw, grads


IN_A = (0, 3 * D_MODEL)
IN_S = (IN_A[1], IN_A[1] + SSD_INNER + SSD_CONV_DIM)
IN_DT = (IN_S[1], IN_S[1] + SSD_HEADS)
IN_R = (IN_DT[1], IN_DT[1] + 4 * D_MODEL)
IN_SB = (IN_R[1], IN_R[1] + 3 * D_MODEL)
IN_G = (IN_SB[1], IN_SB[1] + N_BRANCH * D_MODEL)
IN_WIDTH = IN_G[1]


def _layer_weights(full, small, l):
    w_in = full['w_in'][l]
    cut = lambda r: w_in[:, r[0]:r[1]]
    w_dt = jnp.pad(cut(IN_DT), ((0, 0), (0, DT_PAD - SSD_HEADS)))
    return dict(
        w_a=cut(IN_A), w_s=cut(IN_S), w_dt=w_dt, w_r=cut(IN_R), w_sb=cut(IN_SB), w_g=cut(IN_G),
        w_branch=[full['w_branch'][l, n] for n in range(N_BRANCH)],
        w_out=full['w_out'][l], w_ffn_in=full['w_ffn_in'][l], w_ffn_out=full['w_ffn_out'][l],
        conv_a=full['conv_a'][l], ssd_conv_w=full['ssd_conv_w'][l],
        ssd_conv_b=small['ssd_conv_b'][l], ssd_dt_bias=small['ssd_dt_bias'][l], ssd_a_log=small['ssd_a_log'][l],
        ssd_d=small['ssd_d'][l], ssd_norm=small['ssd_norm'][l],
        n_mix_pre=small['norm_mix_pre'][l][None], n_mix_post=small['norm_mix_post'][l][None],
        n_ffn_pre=small['norm_ffn_pre'][l][None], n_ffn_post=small['norm_ffn_post'][l][None],
    )


def _layer_fwd(h_res, lw, ret_tables):
    s = {'h_res': h_res, 'ret_tables': ret_tables}
    hn = rms_fwd(h_res, lw['n_mix_pre'], out_dtype=BF16, name="rms_mix_pre")
    s['hn'] = hn
    p_a = mm_nn(hn, lw['w_a'], name="proj_conv")
    p_s = mm_nn(hn, lw['w_s'], name="proj_ssd")
    p_dt = mm_nn(hn, lw['w_dt'], name="proj_dt")
    p_r = mm_nn(hn, lw['w_r'], name="proj_ret")
    p_sb = mm_nn(hn, lw['w_sb'], out_dtype=BF16, name="proj_sb")
    p_g = mm_nn(hn, lw['w_g'], name="proj_gate")
    y_a = conv_mixer_fwd(p_a, lw['conv_a'])
    s['p_a'] = p_a
    s['ssd_rows'] = _ssd_rows(lw)
    y_b, s['ssd_saved'] = ssd_fwd(p_s, p_dt, s['ssd_rows'])
    s['p_s'], s['p_dt'] = p_s, p_dt
    y_c, s['ret_ypre'], s['ret_states'] = ret_fwd(p_r, ret_tables)
    s['p_r'] = p_r
    y_d, s['sb_total'] = sb_fwd(p_sb)
    s['p_sb'] = p_sb
    ys = [y_a, y_b, y_c, y_d]
    s['ys'] = ys
    ups = [mm_nn(ys[n], lw['w_branch'][n], name="branch_up") for n in range(N_BRANCH)]
    merged = merge_fwd(p_g, ups)
    s['p_g'], s['ups'] = p_g, ups
    s['merged'] = merged
    mix = mm_nn(merged, lw['w_out'], name="mix_out")
    s['mix'] = mix
    h2 = rms_fwd(mix, lw['n_mix_post'], res=h_res, name="rms_mix_post")
    s['h2'] = h2
    hf = rms_fwd(h2, lw['n_ffn_pre'], out_dtype=BF16, name="rms_ffn_pre")
    s['hf'] = hf
    f = mm_nn(hf, lw['w_ffn_in'], name="ffn_in")
    act = swiglu_fwd(f)
    s['f'], s['act'] = f, act
    fo = mm_nn(act, lw['w_ffn_out'], name="ffn_out")
    s['fo'] = fo
    return rms_fwd(fo, lw['n_ffn_post'], res=h2, name="rms_ffn_post"), s


def _layer_bwd(dh3, lw, s):
    g = {}
    d_fo, g['norm_ffn_post'] = rms_bwd(s['fo'], lw['n_ffn_post'], dh3, dx_dtype=BF16, name="rms_ffn_post_bwd")
    d_act = mm_nt(d_fo, lw['w_ffn_out'], name="ffn_out_dx")
    g['w_ffn_out'] = mm_tn(s['act'], d_fo, name="ffn_out_dw")
    df = swiglu_bwd(s['f'], d_act)
    d_hf = mm_nt(df, lw['w_ffn_in'], name="ffn_in_dx")
    g['w_ffn_in'] = mm_tn(s['hf'], df, name="ffn_in_dw")
    dh2, g['norm_ffn_pre'] = rms_bwd(s['h2'], lw['n_ffn_pre'], d_hf, add=dh3, name="rms_ffn_pre_bwd")
    d_mix, g['norm_mix_post'] = rms_bwd(s['mix'], lw['n_mix_post'], dh2, dx_dtype=BF16, name="rms_mix_post_bwd")
    d_merged = mm_nt(d_mix, lw['w_out'], name="mix_out_dx")
    g['w_out'] = mm_tn(s['merged'], d_mix, name="mix_out_dw")
    dp_g, dups = merge_bwd(s['p_g'], s['ups'], d_merged)
    dys = [mm_nt(dups[n], lw['w_branch'][n], name="branch_dx") for n in range(N_BRANCH)]
    g['w_branch'] = jnp.stack([mm_tn(s['ys'][n], dups[n], name="branch_dw") for n in range(N_BRANCH)])
    dp_a, g['conv_a'] = conv_mixer_bwd(s['p_a'], lw['conv_a'], dys[0])
    dp_s, dp_dt, ssd_grads = ssd_bwd(s['p_s'], s['p_dt'], s['ssd_rows'], s['ssd_saved'], dys[1])
    g.update(ssd_grads)
    dp_r = ret_bwd(s['p_r'], s['ret_tables'], s['ret_ypre'], s['ret_states'], dys[2])
    dq, dk, dv = sb_bwd(s['p_sb'], s['sb_total'], dys[3])
    dp_sb = jnp.concatenate([dq, dk.astype(BF16), dv.astype(BF16)], axis=1)
    hn = s['hn']
    d_hn = None
    dws = []
    for dp, w, nm in ((dp_a, lw['w_a'], "conv"), (dp_s, lw['w_s'], "ssd"), (dp_dt, lw['w_dt'], "dt"),
                      (dp_r, lw['w_r'], "ret"), (dp_sb, lw['w_sb'], "sb"), (dp_g, lw['w_g'], "gate")):
        d_hn = mm_nt(dp, w, acc=d_hn, name="proj_dx")
        dws.append(mm_tn(hn, dp, name="proj_dw"))
    dws[2] = dws[2][:, :SSD_HEADS]
    g['w_in'] = jnp.concatenate(dws, axis=1)
    dh_res, g['norm_mix_pre'] = rms_bwd(s['h_res'], lw['n_mix_pre'], d_hn, add=dh2, name="rms_mix_pre_bwd")
    for k in ('norm_ffn_post', 'norm_ffn_pre', 'norm_mix_post', 'norm_mix_pre'):
        g[k] = g[k][0]
    return dh_res, g


def _quarter(a, axis, j):
    n = a.shape[axis] // N_CHIPS
    return lax.slice_in_dim(a, j * n, (j + 1) * n, axis=axis)


def kernel(x, meta, w_in, conv_a, ssd_conv_w, ssd_conv_b, ssd_dt_bias, ssd_a_log, ssd_d, ssd_norm, w_branch, w_out, w_ffn_in, w_ffn_out, norm_mix_pre, norm_mix_post, norm_ffn_pre, norm_ffn_post, loss_target, m_meta, m_w_in, m_conv_a, m_ssd_conv_w, m_ssd_conv_b, m_ssd_dt_bias, m_ssd_a_log, m_ssd_d, m_ssd_norm, m_w_branch, m_w_out, m_w_ffn_in, m_w_ffn_out, m_norm_mix_pre, m_norm_mix_post, m_norm_ffn_pre, m_norm_ffn_post, v_meta, v_w_in, v_conv_a, v_ssd_conv_w, v_ssd_conv_b, v_ssd_dt_bias, v_ssd_a_log, v_ssd_d, v_ssd_norm, v_w_branch, v_w_out, v_w_ffn_in, v_w_ffn_out, v_norm_mix_pre, v_norm_mix_post, v_norm_ffn_pre, v_norm_ffn_post):
    w_loc = dict(meta=meta, w_in=w_in, conv_a=conv_a, ssd_conv_w=ssd_conv_w, ssd_conv_b=ssd_conv_b,
                 ssd_dt_bias=ssd_dt_bias, ssd_a_log=ssd_a_log, ssd_d=ssd_d, ssd_norm=ssd_norm, w_branch=w_branch,
                 w_out=w_out, w_ffn_in=w_ffn_in, w_ffn_out=w_ffn_out, norm_mix_pre=norm_mix_pre,
                 norm_mix_post=norm_mix_post, norm_ffn_pre=norm_ffn_pre, norm_ffn_post=norm_ffn_post)
    m_loc = dict(meta=m_meta, w_in=m_w_in, conv_a=m_conv_a, ssd_conv_w=m_ssd_conv_w, ssd_conv_b=m_ssd_conv_b,
                 ssd_dt_bias=m_ssd_dt_bias, ssd_a_log=m_ssd_a_log, ssd_d=m_ssd_d, ssd_norm=m_ssd_norm,
                 w_branch=m_w_branch, w_out=m_w_out, w_ffn_in=m_w_ffn_in, w_ffn_out=m_w_ffn_out,
                 norm_mix_pre=m_norm_mix_pre, norm_mix_post=m_norm_mix_post, norm_ffn_pre=m_norm_ffn_pre,
                 norm_ffn_post=m_norm_ffn_post)
    v_loc = dict(meta=v_meta, w_in=v_w_in, conv_a=v_conv_a, ssd_conv_w=v_ssd_conv_w, ssd_conv_b=v_ssd_conv_b,
                 ssd_dt_bias=v_ssd_dt_bias, ssd_a_log=v_ssd_a_log, ssd_d=v_ssd_d, ssd_norm=v_ssd_norm,
                 w_branch=v_w_branch, w_out=v_w_out, w_ffn_in=v_w_ffn_in, w_ffn_out=v_w_ffn_out,
                 norm_mix_pre=v_norm_mix_pre, norm_mix_post=v_norm_mix_post, norm_ffn_pre=v_norm_ffn_pre,
                 norm_ffn_post=v_norm_ffn_post)

    halves = gather_shards([w_loc[n].astype(BF16).reshape(-1, w_loc[n].shape[-1]) for n in MATMUL_WEIGHTS]
                           + [_pack([w_loc[n] for n in SMALL_SHARDED], F32, 16)])
    gathered = _join_halves(halves, sibling_share(halves, "share_weight_halves"), axis=1)
    full = {}
    for t, n in enumerate(MATMUL_WEIGHTS):
        full[n] = jnp.concatenate([gathered[t][j].reshape(w_loc[n].shape) for j in range(N_CHIPS)],
                                  axis=SHARD_AXIS[n])
    parts_f = [_unpack(gathered[-1][j], [w_loc[n].shape for n in SMALL_SHARDED]) for j in range(N_CHIPS)]
    for t, n in enumerate(SMALL_SHARDED):
        full[n] = jnp.concatenate([parts_f[j][t] for j in range(N_CHIPS)], axis=SHARD_AXIS[n])

    xs = x[0]
    seq = xs.shape[0]
    length = CHUNK + seq
    h = jnp.concatenate([jnp.zeros((PAD, D_MODEL), F32), full['meta'], xs], axis=0)
    lws, saved = [], []
    ret_tables = _ret_tables(length)
    for l in range(DEPTH):
        lw = _layer_weights(full, w_loc, l)
        h, s = _layer_fwd(h, lw, ret_tables)
        lws.append(lw)
        saved.append(s)

    loss_row, dh = loss_head(h, loss_target[0])
    loss = lax.psum(loss_row[0, 0], ("x", "y", "c"))

    layer_grads = [None] * DEPTH
    for l in reversed(range(DEPTH)):
        dh, layer_grads[l] = _layer_bwd(dh, lws[l], saved[l])
    grad_x = dh[CHUNK:][None]
    grads = {n: jnp.stack([layer_grads[l][n] for l in range(DEPTH)]) for n in WEIGHTS if n != 'meta'}
    grads['meta'] = dh[PAD:CHUNK]

    def rows2d(a):
        return a.reshape(-1, a.shape[-1])

    def small_pieces(j):
        return [_quarter(grads[n], SHARD_AXIS[n], j) if n in SHARD_AXIS else grads[n] for n in SMALL_ORDER]

    quarters = [jnp.stack([rows2d(_quarter(grads[n], SHARD_AXIS[n], j)) for j in range(N_CHIPS)])
                for n in MATMUL_WEIGHTS]
    quarters.append(jnp.stack([_pack(small_pieces(j), F32, 16) for j in range(N_CHIPS)]))
    reduced = reduce_gradients(quarters)
    results = {}
    for t, n in enumerate(MATMUL_WEIGHTS):
        shape = w_loc[n].shape
        new = adamw(reduced[t], rows2d(w_loc[n]), rows2d(m_loc[n]), rows2d(v_loc[n]))
        results[n] = [a.reshape(shape) for a in (reduced[t], *new)]
    small_new = adamw(reduced[-1], *[_pack([d[n] for n in SMALL_ORDER], F32, 16) for d in (w_loc, m_loc, v_loc)])
    small_shapes = [w_loc[n].shape for n in SMALL_ORDER]
    for kind, buf in enumerate((reduced[-1], *small_new)):
        for n, piece in zip(SMALL_ORDER, _unpack(buf, small_shapes)):
            results.setdefault(n, [None] * 4)[kind] = piece
    outs = [results[n][kind] for kind in range(4) for n in WEIGHTS]
    return (loss, grad_x, *outs)
```

```python
import functools
import math

import numpy as np
import jax
import jax.numpy as jnp
from jax import lax
from jax.experimental import pallas as pl
from jax.experimental.pallas import tpu as pltpu

F32 = jnp.float32
BF16 = jnp.bfloat16

D_MODEL = 1024
DEPTH = 2
N_META = 16
CHUNK = 128
PAD = CHUNK - N_META
EPS = 1e-6

CONV_A_K = 3
SSD_HEAD_DIM = 64
SSD_HEADS = 16
SSD_INNER = 1024
SSD_GROUPS = 4
SSD_STATE = 128
SSD_CONV_K = 4
SSD_CONV_DIM = SSD_INNER + 2 * SSD_GROUPS * SSD_STATE
RET_HEADS = 4
RET_QK_DIM = 256
RET_V_DIM = 256
RET_WIDTH = 1024
ROPE_BASE = 10000.0
SB_HEADS = 8
SB_HEAD_DIM = 128
N_BRANCH = 4
D_FF = 2816
DT_PAD = 128

ADAM_LR = 0.001
ADAM_B1 = 0.9
ADAM_B2 = 0.999
ADAM_EPS = 1e-08
ADAM_WD = 0.01
ADAM_STEP = 10

N_CHIPS = 4
N_DEV = 8
LANES = 128
VMEM_LIMIT = 56 * 1024 * 1024
MESH = pl.DeviceIdType.MESH

WEIGHTS = ['meta', 'w_in', 'conv_a', 'ssd_conv_w', 'ssd_conv_b', 'ssd_dt_bias', 'ssd_a_log', 'ssd_d',
           'ssd_norm', 'w_branch', 'w_out', 'w_ffn_in', 'w_ffn_out', 'norm_mix_pre', 'norm_mix_post',
           'norm_ffn_pre', 'norm_ffn_post']
SHARD_AXIS = {'meta': 1, 'w_in': 2, 'conv_a': 2, 'ssd_conv_w': 2, 'w_branch': 2, 'w_out': 1,
              'w_ffn_in': 2, 'w_ffn_out': 1}
MATMUL_WEIGHTS = ['w_in', 'w_branch', 'w_out', 'w_ffn_in', 'w_ffn_out']
SMALL_SHARDED = ['meta', 'conv_a', 'ssd_conv_w']
SMALL_ORDER = SMALL_SHARDED + [n for n in WEIGHTS if n not in SHARD_AXIS]


def _params(**kw):
    return pltpu.CompilerParams(vmem_limit_bytes=VMEM_LIMIT, **kw)


def _tile(n, prefs):
    for p in prefs:
        if n % p == 0:
            return p
    return n


MM_VMEM_BUDGET = 40 * 1024 * 1024
MM_ROW_TILES = (2752, 1376, 688, 384, 256, 128)
MM_COL_TILES = (1024, 512, 256, 128)


def _mm_tiles(m, n, cost):
    for tm in MM_ROW_TILES:
        if m % tm:
            continue
        for tn in MM_COL_TILES:
            if n % tn == 0 and cost(tm, tn) <= MM_VMEM_BUDGET:
                return tm, tn
    return _tile(m, (128, 8)), _tile(n, (128,))


def _size(x):
    return jnp.dtype(x.dtype).itemsize


def mm_nn(a, b, out_dtype=F32, name="mm_nn"):
    m, k = a.shape
    n = b.shape[1]
    ob = jnp.dtype(out_dtype).itemsize
    tm, tn = _mm_tiles(m, n, lambda tm, tn: (2 * tm * k * _size(a) + tm * k * 2 + 2 * k * tn * _size(b)
                                              + 2 * tm * tn * ob + tm * tn * 4))

    def body(a_ref, b_ref, o_ref):
        o_ref[...] = jnp.dot(a_ref[...].astype(BF16), b_ref[...].astype(BF16),
                             preferred_element_type=F32).astype(o_ref.dtype)

    return pl.pallas_call(
        body, name=name, grid=(m // tm, n // tn),
        in_specs=[pl.BlockSpec((tm, k), lambda i, j: (i, 0)), pl.BlockSpec((k, tn), lambda i, j: (0, j))],
        out_specs=pl.BlockSpec((tm, tn), lambda i, j: (i, j)),
        out_shape=jax.ShapeDtypeStruct((m, n), out_dtype),
        compiler_params=_params(dimension_semantics=("arbitrary", "arbitrary")),
    )(a, b)


def mm_nt(g, w, acc=None, name="mm_nt"):
    m, n = g.shape
    k = w.shape[0]
    has_acc = acc is not None
    tm, tn = _mm_tiles(m, n, lambda tm, tn: ((2 + 2 * has_acc) * tm * k * 4 + tm * k * 4 + 2 * tm * tn * _size(g)
                                              + tm * tn * 2 + 2 * k * tn * _size(w)))

    def body(*refs):
        if has_acc:
            g_ref, w_ref, acc_ref, o_ref = refs
        else:
            g_ref, w_ref, o_ref = refs
        j = pl.program_id(1)

        @pl.when(j == 0)
        def _():
            o_ref[...] = acc_ref[...] if has_acc else jnp.zeros_like(o_ref)

        o_ref[...] += lax.dot_general(g_ref[...].astype(BF16), w_ref[...].astype(BF16),
                                      (((1,), (1,)), ((), ())), preferred_element_type=F32)

    in_specs = [pl.BlockSpec((tm, tn), lambda i, j: (i, j)), pl.BlockSpec((k, tn), lambda i, j: (0, j))]
    args = [g, w]
    if has_acc:
        in_specs.append(pl.BlockSpec((tm, k), lambda i, j: (i, 0)))
        args.append(acc)
    return pl.pallas_call(
        body, name=name, grid=(m // tm, n // tn),
        in_specs=in_specs,
        out_specs=pl.BlockSpec((tm, k), lambda i, j: (i, 0)),
        out_shape=jax.ShapeDtypeStruct((m, k), F32),
        compiler_params=_params(dimension_semantics=("arbitrary", "arbitrary")),
    )(*args)


def mm_tn(x, g, name="mm_tn"):
    m, k = x.shape
    n = g.shape[1]
    tk = _tile(k, (1024, 1408, 512, 256, 128))
    tm, tn = _mm_tiles(m, n, lambda tm, tn: (3 * tk * tn * 4 + 2 * tm * tk * _size(x) + tm * tk * 2
                                              + 2 * tm * tn * _size(g) + tm * tn * 2))

    def body(x_ref, g_ref, o_ref):
        s = pl.program_id(2)

        @pl.when(s == 0)
        def _():
            o_ref[...] = jnp.zeros_like(o_ref)

        o_ref[...] += lax.dot_general(x_ref[...].astype(BF16), g_ref[...].astype(BF16),
                                      (((0,), (0,)), ((), ())), preferred_element_type=F32)

    return pl.pallas_call(
        body, name=name, grid=(k // tk, n // tn, m // tm),
        in_specs=[pl.BlockSpec((tm, tk), lambda a, b, s: (s, a)), pl.BlockSpec((tm, tn), lambda a, b, s: (s, b))],
        out_specs=pl.BlockSpec((tk, tn), lambda a, b, s: (a, b)),
        out_shape=jax.ShapeDtypeStruct((k, n), F32),
        compiler_params=_params(dimension_semantics=("arbitrary", "arbitrary", "arbitrary")),
    )(x, g)


def rms_fwd(x, w, res=None, out_dtype=F32, name="rms_fwd"):
    m, d = x.shape
    tm = _tile(m, (688, 384, 256, 128))
    has_res = res is not None

    def body(*refs):
        if has_res:
            x_ref, w_ref, r_ref, o_ref = refs
        else:
            x_ref, w_ref, o_ref = refs
        xv = x_ref[...]
        y = xv * lax.rsqrt(jnp.mean(xv * xv, axis=-1, keepdims=True) + EPS) * w_ref[...]
        o_ref[...] = (y + r_ref[...] if has_res else y).astype(o_ref.dtype)

    row = pl.BlockSpec((tm, d), lambda i: (i, 0))
    in_specs = [row, pl.BlockSpec((1, d), lambda i: (0, 0))]
    args = [x, w]
    if has_res:
        in_specs.append(row)
        args.append(res)
    return pl.pallas_call(
        body, name=name, grid=(m // tm,), in_specs=in_specs, out_specs=row,
        out_shape=jax.ShapeDtypeStruct((m, d), out_dtype),
        compiler_params=_params(dimension_semantics=("arbitrary",)),
    )(*args)


def rms_bwd(x, w, dy, add=None, dx_dtype=F32, name="rms_bwd"):
    m, d = x.shape
    tm = _tile(m, (688, 384, 256, 128))
    has_add = add is not None

    def body(*refs):
        if has_add:
            x_ref, w_ref, dy_ref, add_ref, dx_ref, dw_ref = refs
        else:
            x_ref, w_ref, dy_ref, dx_ref, dw_ref = refs
        i = pl.program_id(0)
        xv = x_ref[...]
        dyv = dy_ref[...]
        r = lax.rsqrt(jnp.mean(xv * xv, axis=-1, keepdims=True) + EPS)
        xh = xv * r
        dxh = dyv * w_ref[...]
        dx = r * (dxh - xh * jnp.mean(dxh * xh, axis=-1, keepdims=True))
        dx_ref[...] = (dx + add_ref[...] if has_add else dx).astype(dx_ref.dtype)

        @pl.when(i == 0)
        def _():
            dw_ref[...] = jnp.zeros_like(dw_ref)

        dw_ref[...] += jnp.sum(dyv * xh, axis=0, keepdims=True)

    row = pl.BlockSpec((tm, d), lambda i: (i, 0))
    vec = pl.BlockSpec((1, d), lambda i: (0, 0))
    in_specs = [row, vec, row]
    args = [x, w, dy]
    if has_add:
        in_specs.append(row)
        args.append(add)
    return pl.pallas_call(
        body, name=name, grid=(m // tm,), in_specs=in_specs, out_specs=[row, vec],
        out_shape=[jax.ShapeDtypeStruct((m, d), dx_dtype), jax.ShapeDtypeStruct((1, d), F32)],
        compiler_params=_params(dimension_semantics=("arbitrary",)),
    )(*args)


def loss_head(h, target):
    l, d = h.shape
    nblk = l // CHUNK

    def body(h_ref, t_ref, loss_ref, dh_ref, acc_ref):
        i = pl.program_id(0)

        @pl.when(i == 0)
        def _():
            acc_ref[...] = jnp.zeros_like(acc_ref)
            dh_ref[...] = jnp.zeros_like(dh_ref)

        @pl.when(i > 0)
        def _():
            e = h_ref[...] - t_ref[...]
            dh_ref[...] = e / d
            acc_ref[...] += jnp.sum(e * e, axis=0, keepdims=True)

        @pl.when(i == nblk - 1)
        def _():
            loss_ref[...] = jnp.zeros_like(loss_ref) + 0.5 * jnp.sum(acc_ref[...]) / d

    return pl.pallas_call(
        body, name="loss_head", grid=(nblk,),
        in_specs=[pl.BlockSpec((CHUNK, d), lambda i: (i, 0)),
                  pl.BlockSpec((CHUNK, d), lambda i: (jnp.maximum(i - 1, 0), 0))],
        out_specs=[pl.BlockSpec((1, LANES), lambda i: (0, 0)), pl.BlockSpec((CHUNK, d), lambda i: (i, 0))],
        out_shape=[jax.ShapeDtypeStruct((1, LANES), F32), jax.ShapeDtypeStruct((l, d), F32)],
        scratch_shapes=[pltpu.VMEM((1, d), F32)],
        compiler_params=_params(dimension_semantics=("arbitrary",)),
    )(h, target)


SB_BLK = 128


def _sb_tile(l):
    return _tile(l, (384, 256, 128))


def _sb_tri(strict_later):
    r = lax.broadcasted_iota(jnp.int32, (SB_BLK, 2 * SB_BLK), 0)
    c = lax.broadcasted_iota(jnp.int32, (SB_BLK, 2 * SB_BLK), 1)
    keep = (r > c) if strict_later else (r < c)
    return jnp.where(keep | (c >= SB_BLK), 1.0, 0.0).astype(BF16)


def _sb_mask(i, j, t):
    qpos = i * t + lax.broadcasted_iota(jnp.int32, (t, t), 0)
    kpos = j * t + lax.broadcasted_iota(jnp.int32, (t, t), 1)
    return (kpos < qpos) & (kpos >= PAD)


def _sb_scores(q, k, scale, mask):
    z = lax.dot_general(q, k, (((1,), (1,)), ((), ())), preferred_element_type=F32) * scale
    sp = jnp.maximum(z, 0.0) + jnp.log(1.0 + jnp.exp(-jnp.abs(z)))
    lneg = -sp if mask is None else jnp.where(mask, -sp, 0.0)
    return z - sp, lneg


def _sb_block_sums(x, tri):
    s = jnp.dot(x.astype(BF16), tri, preferred_element_type=F32)
    return s[:, :SB_BLK], s[:, SB_BLK:]


SB_DEAD = -110.0


def _sb_walk_down(i, step, carry):
    carry = step(i, carry, True)

    def alive(c):
        return (jnp.max(c[0]) > SB_DEAD).astype(jnp.int32)

    def body(state):
        j, _, c = state
        c = step(j, c, False)
        return j - 1, alive(c), c

    j, go, carry = lax.while_loop(lambda s: (s[0] >= 1) & (s[1] > 0), body, (i - 1, alive(carry), carry))
    reach0 = ((j == 0) & (go > 0) & (i > 0)).astype(jnp.int32)
    carry = lax.fori_loop(0, reach0, lambda t, c: step(0, c, True), carry)
    return carry, jnp.where(reach0 > 0, 0, j + 1)


def _sb_walk_up(i, first, step, carry):
    start0 = ((first == 0) & (i > 0)).astype(jnp.int32)
    carry = lax.fori_loop(0, start0, lambda t, c: step(0, c, True), carry)
    carry = lax.fori_loop(jnp.maximum(first, 1), i, lambda j, c: step(j, c, False), carry)
    return step(i, carry, True)


def sb_fwd(qkv):
    l = qkv.shape[0]
    t = _sb_tile(l)
    nb = t // SB_BLK
    scale = SB_HEAD_DIM ** -0.5

    def body(q_ref, k_ref, v_ref, o_ref, walk_ref):
        i = pl.program_id(1)
        q = q_ref[...]
        tri = _sb_tri(True)

        def step(j, carry, masked):
            later, acc = carry
            rows = pl.ds(pl.multiple_of(j * t, t), t)
            mask = _sb_mask(i, j, t) if masked else None
            lpos, lneg = _sb_scores(q, k_ref[rows, :], scale, mask)
            ws = [None] * nb
            for b in reversed(range(nb)):
                cols = slice(b * SB_BLK, (b + 1) * SB_BLK)
                within, total = _sb_block_sums(lneg[:, cols], tri)
                ws[b] = jnp.exp(lpos[:, cols] + within + later)
                later = later + total
            w = jnp.concatenate(ws, axis=1)
            if masked:
                w = jnp.where(mask, w, 0.0)
            acc = acc + jnp.dot(w.astype(BF16), v_ref[rows, :], preferred_element_type=F32)
            return later, acc

        carry = (jnp.zeros((t, SB_BLK), F32), jnp.zeros((t, SB_HEAD_DIM), F32))
        (later, acc), first = _sb_walk_down(i, step, carry)
        o_ref[...] = acc.astype(o_ref.dtype)
        walk_ref[0] = later[:, :1]
        walk_ref[1] = jnp.zeros((t, 1), F32) + first.astype(F32)

    return pl.pallas_call(
        body, name="sb_fwd", grid=(SB_HEADS, l // t),
        in_specs=[pl.BlockSpec((t, SB_HEAD_DIM), lambda h, i: (i, h)),
                  pl.BlockSpec((l, SB_HEAD_DIM), lambda h, i: (0, SB_HEADS + h)),
                  pl.BlockSpec((l, SB_HEAD_DIM), lambda h, i: (0, 2 * SB_HEADS + h))],
        out_specs=[pl.BlockSpec((t, SB_HEAD_DIM), lambda h, i: (i, h)),
                   pl.BlockSpec((2, None, t, 1), lambda h, i: (0, h, i, 0))],
        out_shape=[jax.ShapeDtypeStruct((l, D_MODEL), BF16), jax.ShapeDtypeStruct((2, SB_HEADS, l, 1), F32)],
        compiler_params=_params(dimension_semantics=("arbitrary", "arbitrary")),
    )(qkv, qkv, qkv)


def sb_bwd(qkv, walk, dout):
    l = qkv.shape[0]
    t = _sb_tile(l)
    nb = t // SB_BLK
    nq = l // t
    scale = SB_HEAD_DIM ** -0.5

    def body(q_ref, k_ref, v_ref, walk_ref, do_ref, dq_ref, dk_hbm, dv_hbm, dk_acc, dv_acc):
        h = pl.program_id(0)
        i = pl.program_id(1)

        @pl.when(i == 0)
        def _():
            dk_acc[...] = jnp.zeros_like(dk_acc)
            dv_acc[...] = jnp.zeros_like(dv_acc)

        q = q_ref[...]
        dob = do_ref[...].astype(BF16)
        tri_later = _sb_tri(True)
        tri_before = _sb_tri(False)

        def step(j, carry, masked):
            later, g_before, dq = carry
            rows = pl.ds(pl.multiple_of(j * t, t), t)
            k = k_ref[rows, :]
            v = v_ref[rows, :]
            mask = _sb_mask(i, j, t) if masked else None
            lpos, lneg = _sb_scores(q, k, scale, mask)
            dw = lax.dot_general(dob, v, (((1,), (1,)), ((), ())), preferred_element_type=F32)
            ws, dzs = [None] * nb, [None] * nb
            for b in range(nb):
                cols = slice(b * SB_BLK, (b + 1) * SB_BLK)
                within, total = _sb_block_sums(lneg[:, cols], tri_later)
                later = later - total
                wb = jnp.exp(lpos[:, cols] + within + later)
                if masked:
                    wb = jnp.where(mask[:, cols], wb, 0.0)
                g = dw[:, cols] * wb
                g_within, g_total = _sb_block_sums(g, tri_before)
                dz = g - (g + g_before + g_within) * jnp.exp(lpos[:, cols])
                if masked:
                    dz = jnp.where(mask[:, cols], dz, 0.0)
                g_before = g_before + g_total
                ws[b] = wb.astype(BF16)
                dzs[b] = (dz * scale).astype(BF16)
            w = jnp.concatenate(ws, axis=1)
            dzb = jnp.concatenate(dzs, axis=1)
            dq = dq + jnp.dot(dzb, k, preferred_element_type=F32)
            dk_acc[rows, :] += lax.dot_general(dzb, q, (((0,), (0,)), ((), ())), preferred_element_type=F32)
            dv_acc[rows, :] += lax.dot_general(w, dob, (((0,), (0,)), ((), ())), preferred_element_type=F32)
            return later, g_before, dq

        carry = (jnp.broadcast_to(walk_ref[0], (t, SB_BLK)), jnp.zeros((t, SB_BLK), F32),
                 jnp.zeros((t, SB_HEAD_DIM), F32))
        first = jnp.max(walk_ref[1]).astype(jnp.int32)
        _, _, dq = _sb_walk_up(i, first, step, carry)
        dq_ref[...] = dq.astype(dq_ref.dtype)

        @pl.when(i == nq - 1)
        def _():
            cols = pl.ds(pl.multiple_of(h * SB_HEAD_DIM, SB_HEAD_DIM), SB_HEAD_DIM)
            pltpu.sync_copy(dk_acc, dk_hbm.at[:, cols])
            pltpu.sync_copy(dv_acc, dv_hbm.at[:, cols])

    blk = lambda h, i: (i, h)
    return pl.pallas_call(
        body, name="sb_bwd", grid=(SB_HEADS, nq),
        in_specs=[pl.BlockSpec((t, SB_HEAD_DIM), blk),
                  pl.BlockSpec((l, SB_HEAD_DIM), lambda h, i: (0, SB_HEADS + h)),
                  pl.BlockSpec((l, SB_HEAD_DIM), lambda h, i: (0, 2 * SB_HEADS + h)),
                  pl.BlockSpec((2, None, t, 1), lambda h, i: (0, h, i, 0)), pl.BlockSpec((t, SB_HEAD_DIM), blk)],
        out_specs=[pl.BlockSpec((t, SB_HEAD_DIM), blk), pl.BlockSpec(memory_space=pl.ANY),
                   pl.BlockSpec(memory_space=pl.ANY)],
        out_shape=[jax.ShapeDtypeStruct((l, D_MODEL), BF16), jax.ShapeDtypeStruct((l, D_MODEL), F32),
                   jax.ShapeDtypeStruct((l, D_MODEL), F32)],
        scratch_shapes=[pltpu.VMEM((l, SB_HEAD_DIM), F32), pltpu.VMEM((l, SB_HEAD_DIM), F32)],
        compiler_params=_params(dimension_semantics=("arbitrary", "arbitrary")),
    )(qkv, qkv, qkv, walk, dout)


_HBM = pl.BlockSpec(memory_space=pl.ANY)


def _other_chips(x, y):
    return [(1 - x, y), (x, 1 - y), (1 - x, 1 - y)]


def _comm_call(body, name, ins, out_shapes, n_remote, n_local):
    return pl.pallas_call(
        body, name=name, in_specs=[_HBM] * len(ins), out_specs=[_HBM] * len(out_shapes), out_shape=out_shapes,
        scratch_shapes=[pltpu.SemaphoreType.DMA((n_remote,)), pltpu.SemaphoreType.DMA((n_remote,)),
                        pltpu.SemaphoreType.DMA((max(n_local, 1),))],
    )(*ins)


def gather_shards(shards):
    n = len(shards)

    def body(*refs):
        ins, outs = refs[:n], refs[n:2 * n]
        send_sems, recv_sems, local_sems = refs[2 * n:]
        x, y, c = lax.axis_index("x"), lax.axis_index("y"), lax.axis_index("c")
        me = 2 * x + y

        def half(t):
            rh = ins[t].shape[0] // 2
            return ins[t].at[pl.ds(pl.multiple_of(c * rh, 8), rh), :]

        own = [pltpu.make_async_copy(half(t), outs[t].at[me], local_sems.at[t]) for t in range(n)]
        for cp in own:
            cp.start()

        def copy(t, k, px, py, slot):
            return pltpu.make_async_remote_copy(
                src_ref=half(t), dst_ref=outs[t].at[slot], send_sem=send_sems.at[3 * t + k],
                recv_sem=recv_sems.at[3 * t + k], device_id=(px, py, c), device_id_type=MESH)

        chips = _other_chips(x, y)
        sends = [copy(t, k, px, py, me) for t in range(n) for k, (px, py) in enumerate(chips)]
        for cp in sends:
            cp.start()
        for t in range(n):
            for k, (px, py) in enumerate(chips):
                copy(t, k, px, py, 2 * px + py).wait_recv()
        for cp in sends:
            cp.wait_send()
        for cp in own:
            cp.wait()

    out_shapes = [jax.ShapeDtypeStruct((N_CHIPS, s.shape[0] // 2, s.shape[1]), s.dtype) for s in shards]
    return _comm_call(body, "gather_shards", shards, out_shapes, 3 * n, n)


def sibling_swap_halves(gs):
    n = len(gs)

    def body(*refs):
        ins, outs = refs[:n], refs[n:2 * n]
        send_sems, recv_sems, _ = refs[2 * n:]
        x, y, c = lax.axis_index("x"), lax.axis_index("y"), lax.axis_index("c")
        copies = []
        for t in range(n):
            rh = ins[t].shape[1] // 2
            src = ins[t].at[:, pl.ds(pl.multiple_of((1 - c) * rh, 8), rh), :]
            copies.append(pltpu.make_async_remote_copy(
                src_ref=src, dst_ref=outs[t], send_sem=send_sems.at[t], recv_sem=recv_sems.at[t],
                device_id=(x, y, 1 - c), device_id_type=MESH))
        for cp in copies:
            cp.start()
        for cp in copies:
            cp.wait_recv()
        for cp in copies:
            cp.wait_send()

    out_shapes = [jax.ShapeDtypeStruct((g.shape[0], g.shape[1] // 2, g.shape[2]), g.dtype) for g in gs]
    return _comm_call(body, "sibling_swap_halves", gs, out_shapes, n, 0)


def chip_exchange(ps):
    n = len(ps)

    def body(*refs):
        ins, outs = refs[:n], refs[n:2 * n]
        send_sems, recv_sems, local_sems = refs[2 * n:]
        x, y, c = lax.axis_index("x"), lax.axis_index("y"), lax.axis_index("c")
        me = 2 * x + y
        own = [pltpu.make_async_copy(ins[t].at[me], outs[t].at[me], local_sems.at[t]) for t in range(n)]
        for cp in own:
            cp.start()

        def copy(t, k, px, py, src_slot, dst_slot):
            return pltpu.make_async_remote_copy(
                src_ref=ins[t].at[src_slot], dst_ref=outs[t].at[dst_slot], send_sem=send_sems.at[3 * t + k],
                recv_sem=recv_sems.at[3 * t + k], device_id=(px, py, c), device_id_type=MESH)

        chips = _other_chips(x, y)
        sends = [copy(t, k, px, py, 2 * px + py, me) for t in range(n) for k, (px, py) in enumerate(chips)]
        for cp in sends:
            cp.start()
        for t in range(n):
            for k, (px, py) in enumerate(chips):
                copy(t, k, px, py, me, 2 * px + py).wait_recv()
        for cp in sends:
            cp.wait_send()
        for cp in own:
            cp.wait()

    out_shapes = [jax.ShapeDtypeStruct(p.shape, p.dtype) for p in ps]
    return _comm_call(body, "chip_exchange", ps, out_shapes, 3 * n, n)


def sibling_share(ss, name):
    n = len(ss)

    def body(*refs):
        ins, outs = refs[:n], refs[n:2 * n]
        send_sems, recv_sems, _ = refs[2 * n:]
        x, y, c = lax.axis_index("x"), lax.axis_index("y"), lax.axis_index("c")
        copies = [pltpu.make_async_remote_copy(
            src_ref=ins[t], dst_ref=outs[t], send_sem=send_sems.at[t], recv_sem=recv_sems.at[t],
            device_id=(x, y, 1 - c), device_id_type=MESH) for t in range(n)]
        for cp in copies:
            cp.start()
        for cp in copies:
            cp.wait_recv()
        for cp in copies:
            cp.wait_send()

    out_shapes = [jax.ShapeDtypeStruct(s.shape, s.dtype) for s in ss]
    return _comm_call(body, name, ss, out_shapes, n, 0)


EW_BLOCK_BYTES = 2 * 1024 * 1024


def _ew_rows(rows, cols, copies=1):
    padded = -(-cols // LANES) * LANES
    for tr in (1024, 512, 256, 128, 64, 32, 16, 8):
        if rows % tr == 0 and copies * tr * padded * 4 <= EW_BLOCK_BYTES:
            return tr
    return rows


def add_pairs(a, b, out_dtype=F32):
    rows, cols = a.shape
    tr = _ew_rows(rows, cols)

    def body(a_ref, b_ref, o_ref):
        o_ref[...] = (a_ref[...] + b_ref[...]).astype(o_ref.dtype)

    blk = pl.BlockSpec((tr, cols), lambda i: (i, 0))
    return pl.pallas_call(
        body, name="add_pairs", grid=(rows // tr,), in_specs=[blk, blk], out_specs=blk,
        out_shape=jax.ShapeDtypeStruct((rows, cols), out_dtype),
        compiler_params=_params(dimension_semantics=("arbitrary",)),
    )(a, b)


def sum_chips(slots):
    _, rows, cols = slots.shape
    tr = _ew_rows(rows, cols, N_CHIPS)

    def body(s_ref, o_ref):
        acc = s_ref[0].astype(F32)
        for j in range(1, N_CHIPS):
            acc = acc + s_ref[j].astype(F32)
        o_ref[...] = acc

    return pl.pallas_call(
        body, name="sum_chips", grid=(rows // tr,),
        in_specs=[pl.BlockSpec((N_CHIPS, tr, cols), lambda i: (0, i, 0))],
        out_specs=pl.BlockSpec((tr, cols), lambda i: (i, 0)),
        out_shape=jax.ShapeDtypeStruct((rows, cols), F32),
        compiler_params=_params(dimension_semantics=("arbitrary",)),
    )(slots)


def adamw(g, w, m, v):
    rows, cols = g.shape
    tr = _ew_rows(rows, cols)

    def body(g_ref, w_ref, m_ref, v_ref, d_out, m_out, v_out):
        gv = g_ref[...]
        m_new = ADAM_B1 * m_ref[...] + (1.0 - ADAM_B1) * gv
        v_new = ADAM_B2 * v_ref[...] + (1.0 - ADAM_B2) * jnp.square(gv)
        m_hat = m_new / (1.0 - ADAM_B1 ** ADAM_STEP)
        v_hat = v_new / (1.0 - ADAM_B2 ** ADAM_STEP)
        d_out[...] = -ADAM_LR * (m_hat / (jnp.sqrt(v_hat) + ADAM_EPS) + ADAM_WD * w_ref[...])
        m_out[...] = m_new
        v_out[...] = v_new

    blk = pl.BlockSpec((tr, cols), lambda i: (i, 0))
    return pl.pallas_call(
        body, name="adamw", grid=(rows // tr,), in_specs=[blk] * 4, out_specs=[blk] * 3,
        out_shape=[jax.ShapeDtypeStruct((rows, cols), F32)] * 3,
        compiler_params=_params(dimension_semantics=("arbitrary",)),
    )(g, w, m, v)


def reduce_gradients(quarters):
    theirs = sibling_swap_halves(quarters)
    c = lax.axis_index("c")
    chip_partials = []
    for n, (q, t) in enumerate(zip(quarters, theirs)):
        four, rh, cols = t.shape
        mine = lax.dynamic_slice_in_dim(q, c * rh, rh, axis=1)
        wire = F32 if n == len(quarters) - 1 else BF16
        chip_partials.append(add_pairs(mine.reshape(four * rh, cols), t.reshape(four * rh, cols), wire)
                             .reshape(four, rh, cols))
    slots = chip_exchange(chip_partials)
    mine = [sum_chips(s) for s in slots]
    return _join_halves(mine, sibling_share(mine, "share_gradient_halves"), axis=0)


def _join_halves(mine, theirs, axis):
    c = lax.axis_index("c")
    return [jnp.concatenate([jnp.where(c == 0, m, t), jnp.where(c == 0, t, m)], axis=axis)
            for m, t in zip(mine, theirs)]


def _pack(pieces, dtype, row_multiple):
    flat = jnp.concatenate([p.astype(dtype).reshape(-1) for p in pieces])
    per = row_multiple * LANES
    padded = -(-flat.shape[0] // per) * per
    flat = jnp.pad(flat, (0, padded - flat.shape[0]))
    return flat.reshape(-1, LANES)


def _unpack(buf, shapes):
    flat = buf.reshape(-1)
    out, off = [], 0
    for s in shapes:
        n = int(np.prod(s))
        out.append(flat[off:off + n].reshape(s))
        off += n
    return out


RET_SCALE = RET_QK_DIM ** -0.5
RET_LOG_GAMMA = [math.log(1.0 - 2.0 ** (-5.0 - h)) for h in range(RET_HEADS)]
RET_HALF = RET_QK_DIM // 2


def _ret_tables(length):
    inv = ROPE_BASE ** (-jnp.arange(RET_HALF, dtype=F32) / RET_HALF)
    ang = jnp.arange(length).astype(F32)[:, None] * inv[None, :]
    log_gamma = jnp.log(1.0 - jnp.power(2.0, -5.0 - jnp.arange(RET_HEADS, dtype=F32)))
    idx = jnp.arange(CHUNK, dtype=F32)
    rel = idx[:, None] - idx[None, :]
    dmask = jnp.where(rel >= 0, jnp.exp(log_gamma[:, None, None] * jnp.maximum(rel, 0.0)), 0.0)
    k_decay = jnp.exp(log_gamma[:, None] * (CHUNK - 1 - idx)[None, :])[:, :, None]
    q_decay = jnp.exp(log_gamma[:, None] * (idx + 1.0)[None, :])[:, :, None]
    return jnp.cos(ang), jnp.sin(ang), dmask, k_decay, q_decay


def _rot(x, cs, sn):
    x1, x2 = x[:, :RET_HALF], x[:, RET_HALF:]
    return jnp.concatenate([x1 * cs - x2 * sn, x1 * sn + x2 * cs], axis=1)


def _unrot(d, cs, sn):
    d1, d2 = d[:, :RET_HALF], d[:, RET_HALF:]
    return jnp.concatenate([d1 * cs + d2 * sn, d2 * cs - d1 * sn], axis=1)


def _sigmoid(x):
    return 1.0 / (1.0 + jnp.exp(-x))


_NT = (((1,), (1,)), ((), ()))
_TN = (((0,), (0,)), ((), ()))


def _ret_specs(nc, rev):
    ch = (lambda c: nc - 1 - c) if rev else (lambda c: c)
    row = lambda w: pl.BlockSpec((CHUNK, w), lambda c: (ch(c), 0))
    const3 = lambda a, b: pl.BlockSpec((RET_HEADS, a, b), lambda c: (0, 0, 0))
    tables = [row(RET_HALF), row(RET_HALF), const3(CHUNK, CHUNK), const3(CHUNK, 1), const3(CHUNK, 1)]
    state = pl.BlockSpec((None, RET_HEADS, RET_QK_DIM, RET_V_DIM), lambda c: (ch(c), 0, 0, 0))
    return row, tables, state


def ret_fwd(p, tables):
    length = p.shape[0]
    nc = length // CHUNK
    row, table_specs, state_spec = _ret_specs(nc, False)

    def body(p_ref, cos_ref, sin_ref, dm_ref, kd_ref, qd_ref, y_ref, ypre_ref, st_ref, r_scr):
        c = pl.program_id(0)

        @pl.when(c == 0)
        def _():
            r_scr[...] = jnp.zeros_like(r_scr)

        cs, sn = cos_ref[...], sin_ref[...]
        valid = (c * CHUNK + lax.broadcasted_iota(jnp.int32, (CHUNK, 1), 0)) >= PAD
        for h in range(RET_HEADS):
            col = lambda part: slice(part * D_MODEL + h * RET_QK_DIM, part * D_MODEL + (h + 1) * RET_QK_DIM)
            qb = _rot(p_ref[:, col(0)].astype(F32), cs, sn).astype(BF16)
            kr = _rot(p_ref[:, col(1)].astype(F32), cs, sn) * RET_SCALE
            kb = kr.astype(BF16)
            vb = jnp.where(valid, p_ref[:, col(2)].astype(F32), 0.0).astype(BF16)
            s = lax.dot_general(qb, kb, _NT, preferred_element_type=F32) * dm_ref[h]
            r = r_scr[h]
            st_ref[h] = r
            y = (jnp.dot(s.astype(BF16), vb, preferred_element_type=F32)
                 + jnp.dot(qb, r.astype(BF16), preferred_element_type=F32) * qd_ref[h])
            kdb = (kr * kd_ref[h]).astype(BF16)
            r_scr[h] = r * math.exp(RET_LOG_GAMMA[h] * CHUNK) + lax.dot_general(kdb, vb, _TN,
                                                                                preferred_element_type=F32)
            out = slice(h * RET_V_DIM, (h + 1) * RET_V_DIM)
            ypre_ref[:, out] = y
            mu = jnp.mean(y, axis=-1, keepdims=True)
            yc = y - mu
            yn = yc * lax.rsqrt(jnp.mean(yc * yc, axis=-1, keepdims=True) + EPS)
            g = p_ref[:, col(3)].astype(F32)
            y_ref[:, out] = (yn * (g * _sigmoid(g))).astype(y_ref.dtype)

    return pl.pallas_call(
        body, name="ret_fwd", grid=(nc,),
        in_specs=[row(4 * D_MODEL)] + table_specs,
        out_specs=[row(D_MODEL), row(D_MODEL), state_spec],
        out_shape=[jax.ShapeDtypeStruct((length, D_MODEL), BF16), jax.ShapeDtypeStruct((length, D_MODEL), F32),
                   jax.ShapeDtypeStruct((nc, RET_HEADS, RET_QK_DIM, RET_V_DIM), F32)],
        scratch_shapes=[pltpu.VMEM((RET_HEADS, RET_QK_DIM, RET_V_DIM), F32)],
        compiler_params=_params(dimension_semantics=("arbitrary",)),
    )(p, *tables)


def ret_bwd(p, tables, ypre, states, dyo):
    length = p.shape[0]
    nc = length // CHUNK
    row, table_specs, state_spec = _ret_specs(nc, True)

    def body(p_ref, cos_ref, sin_ref, dm_ref, kd_ref, qd_ref, ypre_ref, st_ref, dyo_ref, dp_ref, dr_scr):
        c = pl.program_id(0)

        @pl.when(c == 0)
        def _():
            dr_scr[...] = jnp.zeros_like(dr_scr)

        cs, sn = cos_ref[...], sin_ref[...]
        valid = ((nc - 1 - c) * CHUNK + lax.broadcasted_iota(jnp.int32, (CHUNK, 1), 0)) >= PAD
        for h in range(RET_HEADS):
            col = lambda part: slice(part * D_MODEL + h * RET_QK_DIM, part * D_MODEL + (h + 1) * RET_QK_DIM)
            out = slice(h * RET_V_DIM, (h + 1) * RET_V_DIM)
            qb = _rot(p_ref[:, col(0)].astype(F32), cs, sn).astype(BF16)
            kr = _rot(p_ref[:, col(1)].astype(F32), cs, sn) * RET_SCALE
            kb = kr.astype(BF16)
            vb = jnp.where(valid, p_ref[:, col(2)].astype(F32), 0.0).astype(BF16)
            g = p_ref[:, col(3)].astype(F32)
            y = ypre_ref[:, out]
            dyo_h = dyo_ref[:, out]
            mu = jnp.mean(y, axis=-1, keepdims=True)
            yc = y - mu
            rs = lax.rsqrt(jnp.mean(yc * yc, axis=-1, keepdims=True) + EPS)
            xh = yc * rs
            sg = _sigmoid(g)
            dp_ref[:, col(3)] = (dyo_h * xh * (sg * (1.0 + g * (1.0 - sg)))).astype(dp_ref.dtype)
            dyn = dyo_h * (g * sg)
            dy = rs * (dyn - jnp.mean(dyn, axis=-1, keepdims=True)
                       - xh * jnp.mean(dyn * xh, axis=-1, keepdims=True))
            dyb = dy.astype(BF16)
            dm = dm_ref[h]
            sm = (lax.dot_general(qb, kb, _NT, preferred_element_type=F32) * dm).astype(BF16)
            dsb = (lax.dot_general(dyb, vb, _NT, preferred_element_type=F32) * dm).astype(BF16)
            rb = st_ref[h].astype(BF16)
            dyqb = (dy * qd_ref[h]).astype(BF16)
            dr = dr_scr[h]
            drb = dr.astype(BF16)
            kd = kd_ref[h]
            dq = (jnp.dot(dsb, kb, preferred_element_type=F32)
                  + lax.dot_general(dyqb, rb, _NT, preferred_element_type=F32))
            dk = (lax.dot_general(dsb, qb, _TN, preferred_element_type=F32)
                  + lax.dot_general(vb, drb, _NT, preferred_element_type=F32) * kd)
            dv = (lax.dot_general(sm, dyb, _TN, preferred_element_type=F32)
                  + jnp.dot((kr * kd).astype(BF16), drb, preferred_element_type=F32))
            dr_scr[h] = dr * math.exp(RET_LOG_GAMMA[h] * CHUNK) + lax.dot_general(qb, dyqb, _TN,
                                                                                 preferred_element_type=F32)
            dp_ref[:, col(0)] = _unrot(dq, cs, sn).astype(dp_ref.dtype)
            dp_ref[:, col(1)] = (_unrot(dk, cs, sn) * RET_SCALE).astype(dp_ref.dtype)
            dp_ref[:, col(2)] = jnp.where(valid, dv, 0.0).astype(dp_ref.dtype)

    return pl.pallas_call(
        body, name="ret_bwd", grid=(nc,),
        in_specs=[row(4 * D_MODEL)] + table_specs + [row(D_MODEL), state_spec, row(D_MODEL)],
        out_specs=row(4 * D_MODEL),
        out_shape=jax.ShapeDtypeStruct((length, 4 * D_MODEL), BF16),
        scratch_shapes=[pltpu.VMEM((RET_HEADS, RET_QK_DIM, RET_V_DIM), F32)],
        compiler_params=_params(dimension_semantics=("arbitrary",)),
    )(p, *tables, ypre, states, dyo)


HALO = 8
HALO_PER_CHUNK = CHUNK // HALO


def _shift_down(cur, halo, m):
    if m == 0:
        return cur
    n = cur.shape[0]
    rows = lax.broadcasted_iota(jnp.int32, cur.shape, 0)
    edge = jnp.tile(pltpu.roll(halo, m, 0), (n // HALO, 1))
    return jnp.where(rows < m, edge, pltpu.roll(cur, m, 0))


def _shift_up(cur, halo, m):
    if m == 0:
        return cur
    n = cur.shape[0]
    rows = lax.broadcasted_iota(jnp.int32, cur.shape, 0)
    edge = jnp.tile(pltpu.roll(halo, HALO - m, 0), (n // HALO, 1))
    return jnp.where(rows >= n - m, edge, pltpu.roll(cur, n - m, 0))


def _gated_input(ref, row0):
    rows = row0 + lax.broadcasted_iota(jnp.int32, (ref.shape[0], 1), 0)
    return jnp.where((rows >= PAD) & (row0 >= 0), ref[:, D_MODEL:2 * D_MODEL] * ref[:, 2 * D_MODEL:], 0.0)


def conv_mixer_fwd(p, conv_w):
    length = p.shape[0]
    nc = length // CHUNK
    kt = conv_w.shape[0]

    def body(cur_ref, prev_ref, w_ref, y_ref):
        c = pl.program_id(0)
        u_cur = _gated_input(cur_ref, c * CHUNK)
        u_prev = _gated_input(prev_ref, c * CHUNK - HALO)
        acc = jnp.zeros((CHUNK, D_MODEL), F32)
        for i in range(kt):
            acc = acc + _shift_down(u_cur, u_prev, kt - 1 - i) * w_ref[i:i + 1, :]
        y_ref[...] = (cur_ref[:, :D_MODEL] * acc).astype(y_ref.dtype)

    return pl.pallas_call(
        body, name="conv_mixer_fwd", grid=(nc,),
        in_specs=[pl.BlockSpec((CHUNK, 3 * D_MODEL), lambda c: (c, 0)),
                  pl.BlockSpec((HALO, 3 * D_MODEL), lambda c: (jnp.maximum(c * HALO_PER_CHUNK - 1, 0), 0)),
                  pl.BlockSpec((kt, D_MODEL), lambda c: (0, 0))],
        out_specs=pl.BlockSpec((CHUNK, D_MODEL), lambda c: (c, 0)),
        out_shape=jax.ShapeDtypeStruct((length, D_MODEL), BF16),
        compiler_params=_params(dimension_semantics=("arbitrary",)),
    )(p, p, conv_w)


def conv_mixer_bwd(p, conv_w, dy):
    length = p.shape[0]
    nc = length // CHUNK
    kt = conv_w.shape[0]

    def body(cur_ref, prev_ref, w_ref, dy_ref, dyn_ref, pn_ref, dp_ref, dw_ref):
        c = pl.program_id(0)
        rows = lax.broadcasted_iota(jnp.int32, (CHUNK, 1), 0)
        u_cur = _gated_input(cur_ref, c * CHUNK)
        u_prev = _gated_input(prev_ref, c * CHUNK - HALO)
        b_gate = cur_ref[:, :D_MODEL]
        dyv = dy_ref[...]
        dconv = dyv * b_gate
        dconv_next = jnp.where(c + 1 < nc, dyn_ref[...] * pn_ref[:, :D_MODEL], 0.0)

        @pl.when(c == 0)
        def _():
            dw_ref[...] = jnp.zeros_like(dw_ref)

        acc = jnp.zeros((CHUNK, D_MODEL), F32)
        du = jnp.zeros((CHUNK, D_MODEL), F32)
        for i in range(kt):
            shifted = _shift_down(u_cur, u_prev, kt - 1 - i)
            acc = acc + shifted * w_ref[i:i + 1, :]
            dw_ref[i:i + 1, :] += jnp.sum(dconv * shifted, axis=0, keepdims=True)
            du = du + _shift_up(dconv, dconv_next, kt - 1 - i) * w_ref[i:i + 1, :]
        du = jnp.where(c * CHUNK + rows >= PAD, du, 0.0)
        dp_ref[:, :D_MODEL] = (dyv * acc).astype(dp_ref.dtype)
        dp_ref[:, D_MODEL:2 * D_MODEL] = (du * cur_ref[:, 2 * D_MODEL:]).astype(dp_ref.dtype)
        dp_ref[:, 2 * D_MODEL:] = (du * cur_ref[:, D_MODEL:2 * D_MODEL]).astype(dp_ref.dtype)

    nxt = lambda c: (jnp.minimum((c + 1) * HALO_PER_CHUNK, length // HALO - 1), 0)
    return pl.pallas_call(
        body, name="conv_mixer_bwd", grid=(nc,),
        in_specs=[pl.BlockSpec((CHUNK, 3 * D_MODEL), lambda c: (c, 0)),
                  pl.BlockSpec((HALO, 3 * D_MODEL), lambda c: (jnp.maximum(c * HALO_PER_CHUNK - 1, 0), 0)),
                  pl.BlockSpec((kt, D_MODEL), lambda c: (0, 0)),
                  pl.BlockSpec((CHUNK, D_MODEL), lambda c: (c, 0)),
                  pl.BlockSpec((HALO, D_MODEL), nxt),
                  pl.BlockSpec((HALO, 3 * D_MODEL), nxt)],
        out_specs=[pl.BlockSpec((CHUNK, 3 * D_MODEL), lambda c: (c, 0)),
                   pl.BlockSpec((kt, D_MODEL), lambda c: (0, 0))],
        out_shape=[jax.ShapeDtypeStruct((length, 3 * D_MODEL), BF16), jax.ShapeDtypeStruct((kt, D_MODEL), F32)],
        compiler_params=_params(dimension_semantics=("arbitrary",)),
    )(p, p, conv_w, dy, dy, p)


def merge_fwd(gate_logits, ups):
    length = gate_logits.shape[0]
    tm = _tile(length, (384, 256, 128))

    def body(g_ref, u0, u1, u2, u3, o_ref):
        acc = jnp.zeros((tm, D_MODEL), F32)
        for n, u in enumerate((u0, u1, u2, u3)):
            acc = acc + _sigmoid(g_ref[:, n * D_MODEL:(n + 1) * D_MODEL].astype(F32)) * u[...]
        o_ref[...] = acc.astype(o_ref.dtype)

    row = pl.BlockSpec((tm, D_MODEL), lambda i: (i, 0))
    return pl.pallas_call(
        body, name="merge_fwd", grid=(length // tm,),
        in_specs=[pl.BlockSpec((tm, N_BRANCH * D_MODEL), lambda i: (i, 0))] + [row] * N_BRANCH,
        out_specs=row, out_shape=jax.ShapeDtypeStruct((length, D_MODEL), BF16),
        compiler_params=_params(dimension_semantics=("arbitrary",)),
    )(gate_logits, *ups)


def merge_bwd(gate_logits, ups, dmerged):
    length = gate_logits.shape[0]
    tm = _tile(length, (384, 256, 128))

    def body(g_ref, u0, u1, u2, u3, dm_ref, dg_ref, d0, d1, d2, d3):
        dm = dm_ref[...]
        for n, (u, du) in enumerate(((u0, d0), (u1, d1), (u2, d2), (u3, d3))):
            cols = slice(n * D_MODEL, (n + 1) * D_MODEL)
            s = _sigmoid(g_ref[:, cols].astype(F32))
            du[...] = (dm * s).astype(du.dtype)
            dg_ref[:, cols] = (dm * u[...] * (s * (1.0 - s))).astype(dg_ref.dtype)

    row = pl.BlockSpec((tm, D_MODEL), lambda i: (i, 0))
    wide = pl.BlockSpec((tm, N_BRANCH * D_MODEL), lambda i: (i, 0))
    outs = pl.pallas_call(
        body, name="merge_bwd", grid=(length // tm,),
        in_specs=[wide] + [row] * (N_BRANCH + 1),
        out_specs=[wide] + [row] * N_BRANCH,
        out_shape=[jax.ShapeDtypeStruct((length, N_BRANCH * D_MODEL), BF16)]
        + [jax.ShapeDtypeStruct((length, D_MODEL), BF16)] * N_BRANCH,
        compiler_params=_params(dimension_semantics=("arbitrary",)),
    )(gate_logits, *ups, dmerged)
    return outs[0], list(outs[1:])


def swiglu_fwd(f):
    length = f.shape[0]
    tm = _tile(length, (384, 256, 128))

    def body(f_ref, o_ref):
        a = f_ref[:, :D_FF].astype(F32)
        o_ref[...] = (a * _sigmoid(a) * f_ref[:, D_FF:].astype(F32)).astype(o_ref.dtype)

    return pl.pallas_call(
        body, name="swiglu_fwd", grid=(length // tm,),
        in_specs=[pl.BlockSpec((tm, 2 * D_FF), lambda i: (i, 0))],
        out_specs=pl.BlockSpec((tm, D_FF), lambda i: (i, 0)),
        out_shape=jax.ShapeDtypeStruct((length, D_FF), BF16),
        compiler_params=_params(dimension_semantics=("arbitrary",)),
    )(f)


def swiglu_bwd(f, dact):
    length = f.shape[0]
    tm = _tile(length, (384, 256, 128))

    def body(f_ref, d_ref, df_ref):
        a = f_ref[:, :D_FF].astype(F32)
        up = f_ref[:, D_FF:].astype(F32)
        d = d_ref[...]
        s = _sigmoid(a)
        df_ref[:, :D_FF] = (d * up * (s * (1.0 + a * (1.0 - s)))).astype(df_ref.dtype)
        df_ref[:, D_FF:] = (d * (a * s)).astype(df_ref.dtype)

    return pl.pallas_call(
        body, name="swiglu_bwd", grid=(length // tm,),
        in_specs=[pl.BlockSpec((tm, 2 * D_FF), lambda i: (i, 0)), pl.BlockSpec((tm, D_FF), lambda i: (i, 0))],
        out_specs=pl.BlockSpec((tm, 2 * D_FF), lambda i: (i, 0)),
        out_shape=jax.ShapeDtypeStruct((length, 2 * D_FF), BF16),
        compiler_params=_params(dimension_semantics=("arbitrary",)),
    )(f, dact)


SSD_PAIRS = SSD_HEADS // 2
SSD_XBC = SSD_CONV_DIM
SSD_GW = SSD_INNER // SSD_GROUPS


def _split3(x):
    h1 = x.astype(BF16)
    r1 = x - h1.astype(F32)
    h2 = r1.astype(BF16)
    h3 = (r1 - h2.astype(F32)).astype(BF16)
    return h1, h2, h3


def _tri_apply(tri, x, dims):
    out = None
    for part in _split3(x):
        t = lax.dot_general(tri, part, dims, preferred_element_type=F32)
        out = t if out is None else out + t
    return out


def _softplus(x):
    return jnp.maximum(x, 0.0) + jnp.log(1.0 + jnp.exp(-jnp.abs(x)))


def _lane_pair(x, pair):
    lanes = lax.broadcasted_iota(jnp.int32, (x.shape[0], LANES), 1)
    return jnp.where(lanes < SSD_HEAD_DIM, x[:, 2 * pair:2 * pair + 1], x[:, 2 * pair + 1:2 * pair + 2])


def _half_sums(t):
    lanes = lax.broadcasted_iota(jnp.int32, t.shape, 1)
    lo = jnp.sum(jnp.where(lanes < SSD_HEAD_DIM, t, 0.0), axis=1, keepdims=True)
    return lo, jnp.sum(t, axis=1, keepdims=True) - lo


def _put_cols(cols):
    rows = cols[0].shape[0]
    lanes = lax.broadcasted_iota(jnp.int32, (rows, LANES), 1)
    out = jnp.zeros((rows, LANES), F32)
    for h, col in enumerate(cols):
        out = out + jnp.where(lanes == h, col, 0.0)
    return out


def ssd_pre_fwd(p, dt_raw, conv_w, conv_b, dt_bias, a_log):
    length = p.shape[0]
    nc = length // CHUNK
    kt = conv_w.shape[0]

    def body(cur_ref, prev_ref, raw_ref, w_ref, b_ref, bias_ref, alog_ref, act_ref, dt_ref, a_ref):
        c = pl.program_id(0)
        rows = lax.broadcasted_iota(jnp.int32, (CHUNK, 1), 0)
        vm = c * CHUNK + rows >= PAD
        u_cur = jnp.where(vm, cur_ref[:, SSD_INNER:], 0.0)
        u_prev = jnp.where((c * CHUNK - HALO + rows[:HALO] >= PAD) & (c >= 1), prev_ref[:, SSD_INNER:], 0.0)
        pre = jnp.zeros((CHUNK, SSD_XBC), F32) + b_ref[...]
        for i in range(kt):
            pre = pre + _shift_down(u_cur, u_prev, kt - 1 - i) * w_ref[i:i + 1, :]
        act = pre * _sigmoid(pre)
        act_ref[:, :SSD_INNER] = jnp.where(vm, act[:, :SSD_INNER], 0.0)
        act_ref[:, SSD_INNER:] = act[:, SSD_INNER:]
        dt = _softplus(raw_ref[...] + bias_ref[...])
        dt_ref[...] = dt
        a_ref[...] = -jnp.exp(alog_ref[...]) * dt

    row = lambda w: pl.BlockSpec((CHUNK, w), lambda c: (c, 0))
    vec = lambda w: pl.BlockSpec((1, w), lambda c: (0, 0))
    return pl.pallas_call(
        body, name="ssd_pre_fwd", grid=(nc,),
        in_specs=[row(3 * D_MODEL),
                  pl.BlockSpec((HALO, 3 * D_MODEL), lambda c: (jnp.maximum(c * HALO_PER_CHUNK - 1, 0), 0)),
                  row(LANES), pl.BlockSpec((kt, SSD_XBC), lambda c: (0, 0)), vec(SSD_XBC), vec(LANES), vec(LANES)],
        out_specs=[row(SSD_XBC), row(LANES), row(LANES)],
        out_shape=[jax.ShapeDtypeStruct((length, SSD_XBC), F32), jax.ShapeDtypeStruct((length, LANES), F32),
                   jax.ShapeDtypeStruct((length, LANES), F32)],
        compiler_params=_params(dimension_semantics=("arbitrary",)),
    )(p, p, dt_raw, conv_w, conv_b, dt_bias, a_log)


def ssd_pre_bwd(p, dt_raw, conv_w, conv_b, dt_bias, a_log, dact, ddt, da, dz):
    length = p.shape[0]
    nc = length // CHUNK
    kt = conv_w.shape[0]

    def body(cur_ref, prev_ref, raw_ref, w_ref, b_ref, bias_ref, alog_ref, dact_ref, ddt_ref, da_ref, dz_ref,
             dp_ref, draw_ref, dw_ref, db_ref, dbias_ref, dalog_ref, dpre_next):
        step = pl.program_id(0)
        c = nc - 1 - step
        rows = lax.broadcasted_iota(jnp.int32, (CHUNK, 1), 0)
        vm = c * CHUNK + rows >= PAD

        @pl.when(step == 0)
        def _():
            dpre_next[...] = jnp.zeros_like(dpre_next)
            dw_ref[...] = jnp.zeros_like(dw_ref)
            db_ref[...] = jnp.zeros_like(db_ref)
            dbias_ref[...] = jnp.zeros_like(dbias_ref)
            dalog_ref[...] = jnp.zeros_like(dalog_ref)

        u_cur = jnp.where(vm, cur_ref[:, SSD_INNER:], 0.0)
        u_prev = jnp.where((c * CHUNK - HALO + rows[:HALO] >= PAD) & (c >= 1), prev_ref[:, SSD_INNER:], 0.0)
        shifted = [_shift_down(u_cur, u_prev, kt - 1 - i) for i in range(kt)]
        pre = jnp.zeros((CHUNK, SSD_XBC), F32) + b_ref[...]
        for i in range(kt):
            pre = pre + shifted[i] * w_ref[i:i + 1, :]
        sg = _sigmoid(pre)
        lanes = lax.broadcasted_iota(jnp.int32, (CHUNK, SSD_XBC), 1)
        dact_v = jnp.where(vm | (lanes >= SSD_INNER), dact_ref[...], 0.0)
        dpre = dact_v * (sg * (1.0 + pre * (1.0 - sg)))
        db_ref[...] += jnp.sum(dpre, axis=0, keepdims=True)
        nxt = dpre_next[...]
        du = jnp.zeros((CHUNK, SSD_XBC), F32)
        for i in range(kt):
            dw_ref[i:i + 1, :] += jnp.sum(dpre * shifted[i], axis=0, keepdims=True)
            du = du + _shift_up(dpre, nxt, kt - 1 - i) * w_ref[i:i + 1, :]
        dpre_next[...] = dpre[:HALO]
        dp_ref[:, :SSD_INNER] = dz_ref[...].astype(dp_ref.dtype)
        dp_ref[:, SSD_INNER:] = jnp.where(vm, du, 0.0).astype(dp_ref.dtype)
        x = raw_ref[...] + bias_ref[...]
        neg_exp = -jnp.exp(alog_ref[...])
        dav = da_ref[...]
        draw = (ddt_ref[...] + dav * neg_exp) * _sigmoid(x)
        draw_ref[...] = draw.astype(draw_ref.dtype)
        dbias_ref[...] += jnp.sum(draw, axis=0, keepdims=True)
        dalog_ref[...] += jnp.sum(dav * (neg_exp * _softplus(x)), axis=0, keepdims=True)

    rev = lambda c: (nc - 1 - c, 0)
    row = lambda w: pl.BlockSpec((CHUNK, w), rev)
    vec = lambda w: pl.BlockSpec((1, w), lambda c: (0, 0))
    taps = pl.BlockSpec((kt, SSD_XBC), lambda c: (0, 0))
    return pl.pallas_call(
        body, name="ssd_pre_bwd", grid=(nc,),
        in_specs=[row(3 * D_MODEL),
                  pl.BlockSpec((HALO, 3 * D_MODEL),
                               lambda c: (jnp.maximum((nc - 1 - c) * HALO_PER_CHUNK - 1, 0), 0)),
                  row(LANES), taps, vec(SSD_XBC), vec(LANES), vec(LANES),
                  row(SSD_XBC), row(LANES), row(LANES), row(SSD_INNER)],
        out_specs=[row(3 * D_MODEL), row(LANES), taps, vec(SSD_XBC), vec(LANES), vec(LANES)],
        out_shape=[jax.ShapeDtypeStruct((length, 3 * D_MODEL), BF16), jax.ShapeDtypeStruct((length, LANES), BF16),
                   jax.ShapeDtypeStruct((kt, SSD_XBC), F32), jax.ShapeDtypeStruct((1, SSD_XBC), F32),
                   jax.ShapeDtypeStruct((1, LANES), F32), jax.ShapeDtypeStruct((1, LANES), F32)],
        scratch_shapes=[pltpu.VMEM((HALO, SSD_XBC), F32)],
        compiler_params=_params(dimension_semantics=("arbitrary",)),
    )(p, p, dt_raw, conv_w, conv_b, dt_bias, a_log, dact, ddt, da, dz)


def _tri_apply_lhs_t(x, tri):
    out = None
    for part in _split3(x):
        t = lax.dot_general(part, tri, (((0,), (1,)), ((), ())), preferred_element_type=F32)
        out = t if out is None else out + t
    return out


def ssd_core_fwd(act, dt, a, d_skip):
    length = act.shape[0]
    nc = length // CHUNK

    def body(act_ref, dt_ref, a_ref, dskip_ref, y_ref, st_ref, h_scr):
        c = pl.program_id(0)

        @pl.when(c == 0)
        def _():
            h_scr[...] = jnp.zeros_like(h_scr)

        r = lax.broadcasted_iota(jnp.int32, (CHUNK, CHUNK), 0)
        s = lax.broadcasted_iota(jnp.int32, (CHUNK, CHUNK), 1)
        causal = r >= s
        incl = jnp.where(causal, 1.0, 0.0).astype(BF16)
        a_v = a_ref[...]
        acs = _tri_apply(incl, a_v, (((1,), (0,)), ((), ())))
        acs_t = _tri_apply_lhs_t(a_v, incl)
        dt_v = dt_ref[...]
        lanes = lax.broadcasted_iota(jnp.int32, (CHUNK, LANES), 1)
        low = lanes < SSD_HEAD_DIM
        for g in range(SSD_GROUPS):
            bg = act_ref[:, SSD_INNER + g * SSD_STATE:SSD_INNER + (g + 1) * SSD_STATE].astype(BF16)
            cg = act_ref[:, SSD_INNER + (SSD_GROUPS + g) * SSD_STATE:
                         SSD_INNER + (SSD_GROUPS + g + 1) * SSD_STATE].astype(BF16)
            cb = lax.dot_general(cg, bg, _NT, preferred_element_type=F32)
            for pair in (2 * g, 2 * g + 1):
                cols = slice(pair * LANES, (pair + 1) * LANES)
                xs = act_ref[:, cols]
                x = xs * _lane_pair(dt_v, pair)
                ydiag = jnp.zeros((CHUNK, LANES), F32)
                for k, keep in ((0, low), (1, ~low)):
                    h = 2 * pair + k
                    seg = jnp.where(causal, jnp.exp(acs[:, h:h + 1] - acs_t[h:h + 1, :]), 0.0)
                    ydiag = ydiag + jnp.dot((cb * seg).astype(BF16), jnp.where(keep, x, 0.0).astype(BF16),
                                            preferred_element_type=F32)
                acs_p = _lane_pair(acs, pair)
                last = acs_p[CHUNK - 1:CHUNK, :]
                xds = (x * jnp.exp(last - acs_p)).astype(BF16)
                hprev = h_scr[pair]
                st_ref[pair] = hprev
                yoff = lax.dot_general(cg, hprev.astype(BF16), _NT, preferred_element_type=F32) * jnp.exp(acs_p)
                prow = lax.broadcasted_iota(jnp.int32, (LANES, 1), 0)
                cd = jnp.where(prow < SSD_HEAD_DIM, jnp.exp(acs_t[2 * pair:2 * pair + 1, CHUNK - 1:CHUNK]),
                               jnp.exp(acs_t[2 * pair + 1:2 * pair + 2, CHUNK - 1:CHUNK]))
                h_scr[pair] = hprev * cd + lax.dot_general(xds, bg, _TN, preferred_element_type=F32)
                y_ref[:, cols] = ydiag + yoff + xs * dskip_ref[:, cols]

    row = lambda w: pl.BlockSpec((CHUNK, w), lambda c: (c, 0))
    return pl.pallas_call(
        body, name="ssd_core_fwd", grid=(nc,),
        in_specs=[row(SSD_XBC), row(LANES), row(LANES), pl.BlockSpec((1, SSD_INNER), lambda c: (0, 0))],
        out_specs=[row(SSD_INNER), pl.BlockSpec((None, SSD_PAIRS, LANES, SSD_STATE), lambda c: (c, 0, 0, 0))],
        out_shape=[jax.ShapeDtypeStruct((length, SSD_INNER), F32),
                   jax.ShapeDtypeStruct((nc, SSD_PAIRS, LANES, SSD_STATE), F32)],
        scratch_shapes=[pltpu.VMEM((SSD_PAIRS, LANES, SSD_STATE), F32)],
        compiler_params=_params(dimension_semantics=("arbitrary",)),
    )(act, dt, a, d_skip)


def ssd_core_bwd(act, dt, a, d_skip, states, dy):
    length = act.shape[0]
    nc = length // CHUNK

    def body(act_ref, dt_ref, a_ref, dskip_ref, st_ref, dy_ref, dact_ref, ddt_ref, da_ref, dds_ref, dh_scr):
        step = pl.program_id(0)

        @pl.when(step == 0)
        def _():
            dh_scr[...] = jnp.zeros_like(dh_scr)
            dds_ref[...] = jnp.zeros_like(dds_ref)

        r = lax.broadcasted_iota(jnp.int32, (CHUNK, CHUNK), 0)
        s = lax.broadcasted_iota(jnp.int32, (CHUNK, CHUNK), 1)
        causal = r >= s
        incl = jnp.where(causal, 1.0, 0.0).astype(BF16)
        a_v = a_ref[...]
        acs = _tri_apply(incl, a_v, (((1,), (0,)), ((), ())))
        acs_t = _tri_apply_lhs_t(a_v, incl)
        dt_v = dt_ref[...]
        lanes = lax.broadcasted_iota(jnp.int32, (CHUNK, LANES), 1)
        low = lanes < SSD_HEAD_DIM
        prow = lax.broadcasted_iota(jnp.int32, (LANES, 1), 0)
        is_last = lax.broadcasted_iota(jnp.int32, (CHUNK, 1), 0) == CHUNK - 1
        dacs_cols = [None] * SSD_HEADS
        dacs_rows = [None] * SSD_HEADS
        ddt_cols = [None] * SSD_HEADS
        for g in range(SSD_GROUPS):
            b_cols = slice(SSD_INNER + g * SSD_STATE, SSD_INNER + (g + 1) * SSD_STATE)
            c_cols = slice(SSD_INNER + (SSD_GROUPS + g) * SSD_STATE, SSD_INNER + (SSD_GROUPS + g + 1) * SSD_STATE)
            bg = act_ref[:, b_cols].astype(BF16)
            cg = act_ref[:, c_cols].astype(BF16)
            cb = lax.dot_general(cg, bg, _NT, preferred_element_type=F32)
            dcb = jnp.zeros((CHUNK, CHUNK), F32)
            dbg = jnp.zeros((CHUNK, SSD_STATE), F32)
            dcg = jnp.zeros((CHUNK, SSD_STATE), F32)
            for pair in (2 * g, 2 * g + 1):
                cols = slice(pair * LANES, (pair + 1) * LANES)
                xs = act_ref[:, cols]
                dtp = _lane_pair(dt_v, pair)
                x = xs * dtp
                xb = x.astype(BF16)
                dyv = dy_ref[:, cols]
                dyb = dyv.astype(BF16)
                dds_ref[:, cols] += jnp.sum(dyv * xs, axis=0, keepdims=True)
                acs_p = _lane_pair(acs, pair)
                last = acs_p[CHUNK - 1:CHUNK, :]
                ds = jnp.exp(last - acs_p)
                ea = jnp.exp(acs_p)
                hprev = st_ref[pair]
                hb = hprev.astype(BF16)
                dh = dh_scr[pair]
                dhb = dh.astype(BF16)
                dx = jnp.zeros((CHUNK, LANES), F32)
                for k, keep in ((0, low), (1, ~low)):
                    h = 2 * pair + k
                    seg = jnp.where(causal, jnp.exp(acs[:, h:h + 1] - acs_t[h:h + 1, :]), 0.0)
                    lmat = cb * seg
                    dl = lax.dot_general(jnp.where(keep, dyv, 0.0).astype(BF16), xb, _NT,
                                         preferred_element_type=F32)
                    dcb = dcb + dl * seg
                    t = dl * lmat
                    dacs_cols[h] = jnp.sum(t, axis=1, keepdims=True)
                    dacs_rows[h] = jnp.sum(t, axis=0, keepdims=True)
                    dx = dx + jnp.where(keep, lax.dot_general(lmat.astype(BF16), dyb, _TN,
                                                              preferred_element_type=F32), 0.0)
                yoff = lax.dot_general(cg, hb, _NT, preferred_element_type=F32) * ea
                dm = (dyv * ea).astype(BF16)
                dcg = dcg + jnp.dot(dm, hb, preferred_element_type=F32)
                dxds = lax.dot_general(bg, dhb, _NT, preferred_element_type=F32)
                xds = x * ds
                dbg = dbg + jnp.dot(xds.astype(BF16), dhb, preferred_element_type=F32)
                dx = dx + dxds * ds
                t_ds = dxds * xds
                e_a = jnp.exp(acs_t[2 * pair:2 * pair + 1, CHUNK - 1:CHUNK])
                e_b = jnp.exp(acs_t[2 * pair + 1:2 * pair + 2, CHUNK - 1:CHUNK])
                cd = jnp.where(prow < SSD_HEAD_DIM, e_a, e_b)
                hd = dh * hprev
                dcd_a = jnp.sum(jnp.where(prow < SSD_HEAD_DIM, hd, 0.0), keepdims=True)
                dcd_b = jnp.sum(hd, keepdims=True) - dcd_a
                dh_scr[pair] = dh * cd + lax.dot_general(dm, cg, _TN, preferred_element_type=F32)
                col_lo, col_hi = _half_sums(dyv * yoff - t_ds)
                tot_lo, tot_hi = _half_sums(jnp.sum(t_ds, axis=0, keepdims=True))
                dacs_cols[2 * pair] += col_lo + jnp.where(is_last, tot_lo + dcd_a.reshape(1, 1) * e_a, 0.0)
                dacs_cols[2 * pair + 1] += col_hi + jnp.where(is_last, tot_hi + dcd_b.reshape(1, 1) * e_b, 0.0)
                dact_ref[:, cols] = dyv * dskip_ref[:, cols] + dx * dtp
                ddt_cols[2 * pair], ddt_cols[2 * pair + 1] = _half_sums(dx * xs)
            dcbb = dcb.astype(BF16)
            dact_ref[:, b_cols] = dbg + lax.dot_general(dcbb, cg, _TN, preferred_element_type=F32)
            dact_ref[:, c_cols] = dcg + jnp.dot(dcbb, bg, preferred_element_type=F32)
        ddt_ref[...] = _put_cols(ddt_cols)
        sub = lax.broadcasted_iota(jnp.int32, (LANES, CHUNK), 0)
        rows_mat = jnp.zeros((LANES, CHUNK), F32)
        for h in range(SSD_HEADS):
            rows_mat = rows_mat + jnp.where(sub == h, dacs_rows[h], 0.0)
        dacs = _put_cols(dacs_cols) - rows_mat.T
        da_ref[...] = _tri_apply(incl, dacs, (((0,), (0,)), ((), ())))

    rev = lambda c: (nc - 1 - c, 0)
    row = lambda w: pl.BlockSpec((CHUNK, w), rev)
    lane_vec = pl.BlockSpec((1, SSD_INNER), lambda c: (0, 0))
    return pl.pallas_call(
        body, name="ssd_core_bwd", grid=(nc,),
        in_specs=[row(SSD_XBC), row(LANES), row(LANES), lane_vec,
                  pl.BlockSpec((None, SSD_PAIRS, LANES, SSD_STATE), lambda c: (nc - 1 - c, 0, 0, 0)),
                  row(SSD_INNER)],
        out_specs=[row(SSD_XBC), row(LANES), row(LANES), lane_vec],
        out_shape=[jax.ShapeDtypeStruct((length, SSD_XBC), F32), jax.ShapeDtypeStruct((length, LANES), F32),
                   jax.ShapeDtypeStruct((length, LANES), F32), jax.ShapeDtypeStruct((1, SSD_INNER), F32)],
        scratch_shapes=[pltpu.VMEM((SSD_PAIRS, LANES, SSD_STATE), F32)],
        compiler_params=_params(dimension_semantics=("arbitrary",)),
    )(act, dt, a, d_skip, states, dy)


def ssd_post_fwd(y, p, norm_w):
    length = y.shape[0]
    tm = _tile(length, (384, 256, 128))

    def body(y_ref, p_ref, w_ref, o_ref):
        for g in range(SSD_GROUPS):
            cols = slice(g * SSD_GW, (g + 1) * SSD_GW)
            z = p_ref[:, cols]
            v = y_ref[:, cols] * (z * _sigmoid(z))
            o_ref[:, cols] = (v * lax.rsqrt(jnp.mean(v * v, axis=-1, keepdims=True) + EPS)
                              * w_ref[:, cols]).astype(o_ref.dtype)

    return pl.pallas_call(
        body, name="ssd_post_fwd", grid=(length // tm,),
        in_specs=[pl.BlockSpec((tm, SSD_INNER), lambda i: (i, 0)), pl.BlockSpec((tm, SSD_INNER), lambda i: (i, 0)),
                  pl.BlockSpec((1, SSD_INNER), lambda i: (0, 0))],
        out_specs=pl.BlockSpec((tm, SSD_INNER), lambda i: (i, 0)),
        out_shape=jax.ShapeDtypeStruct((length, SSD_INNER), BF16),
        compiler_params=_params(dimension_semantics=("arbitrary",)),
    )(y, p, norm_w)


def ssd_post_bwd(y, p, norm_w, dout):
    length = y.shape[0]
    tm = _tile(length, (384, 256, 128))

    def body(y_ref, p_ref, w_ref, do_ref, dy_ref, dz_ref, dw_ref):
        @pl.when(pl.program_id(0) == 0)
        def _():
            dw_ref[...] = jnp.zeros_like(dw_ref)

        for g in range(SSD_GROUPS):
            cols = slice(g * SSD_GW, (g + 1) * SSD_GW)
            z = p_ref[:, cols]
            yv = y_ref[:, cols]
            sg = _sigmoid(z)
            v = yv * (z * sg)
            rs = lax.rsqrt(jnp.mean(v * v, axis=-1, keepdims=True) + EPS)
            vh = v * rs
            do = do_ref[:, cols]
            dw_ref[:, cols] += jnp.sum(do * vh, axis=0, keepdims=True)
            dvh = do * w_ref[:, cols]
            dv = rs * (dvh - vh * jnp.mean(dvh * vh, axis=-1, keepdims=True))
            dy_ref[:, cols] = dv * (z * sg)
            dz_ref[:, cols] = dv * yv * (sg * (1.0 + z * (1.0 - sg)))

    blk = pl.BlockSpec((tm, SSD_INNER), lambda i: (i, 0))
    vec = pl.BlockSpec((1, SSD_INNER), lambda i: (0, 0))
    return pl.pallas_call(
        body, name="ssd_post_bwd", grid=(length // tm,),
        in_specs=[blk, blk, vec, blk], out_specs=[blk, blk, vec],
        out_shape=[jax.ShapeDtypeStruct((length, SSD_INNER), F32), jax.ShapeDtypeStruct((length, SSD_INNER), F32),
                   jax.ShapeDtypeStruct((1, SSD_INNER), F32)],
        compiler_params=_params(dimension_semantics=("arbitrary",)),
    )(y, p, norm_w, dout)


def _ssd_rows(lw):
    pad = lambda v: jnp.pad(v, (0, LANES - SSD_HEADS))[None]
    return dict(conv_w=lw['ssd_conv_w'], conv_b=lw['ssd_conv_b'][None], dt_bias=pad(lw['ssd_dt_bias']),
                a_log=pad(lw['ssd_a_log']), d_skip=jnp.repeat(lw['ssd_d'], SSD_HEAD_DIM)[None],
                norm_w=lw['ssd_norm'][None])


def ssd_fwd(p, dt_raw, rows):
    act, dt, a = ssd_pre_fwd(p, dt_raw, rows['conv_w'], rows['conv_b'], rows['dt_bias'], rows['a_log'])
    y, states = ssd_core_fwd(act, dt, a, rows['d_skip'])
    return ssd_post_fwd(y, p, rows['norm_w']), (act, dt, a, y, states)


def ssd_bwd(p, dt_raw, rows, saved, dout):
    act, dt, a, y, states = saved
    dy, dz, dnorm = ssd_post_bwd(y, p, rows['norm_w'], dout)
    dact, ddt, da, dskip_lanes = ssd_core_bwd(act, dt, a, rows['d_skip'], states, dy)
    dp, draw, dconv_w, dconv_b, dbias, dalog = ssd_pre_bwd(
        p, dt_raw, rows['conv_w'], rows['conv_b'], rows['dt_bias'], rows['a_log'], dact, ddt, da, dz)
    grads = dict(ssd_conv_w=dconv_w, ssd_conv_b=dconv_b[0], ssd_dt_bias=dbias[0, :SSD_HEADS],
                 ssd_a_log=dalog[0, :SSD_HEADS], ssd_norm=dnorm[0],
                 ssd_d=jnp.sum(dskip_lanes.reshape(SSD_HEADS, SSD_HEAD_DIM), axis=1))
    return dp, draw, grads


IN_A = (0, 3 * D_MODEL)
IN_S = (IN_A[1], IN_A[1] + SSD_INNER + SSD_CONV_DIM)
IN_DT = (IN_S[1], IN_S[1] + SSD_HEADS)
IN_R = (IN_DT[1], IN_DT[1] + 4 * D_MODEL)
IN_SB = (IN_R[1], IN_R[1] + 3 * D_MODEL)
IN_G = (IN_SB[1], IN_SB[1] + N_BRANCH * D_MODEL)
IN_WIDTH = IN_G[1]


def _layer_weights(full, small, l):
    w_in = full['w_in'][l]
    cut = lambda r: w_in[:, r[0]:r[1]]
    w_dt = jnp.pad(cut(IN_DT), ((0, 0), (0, DT_PAD - SSD_HEADS)))
    return dict(
        w_a=cut(IN_A), w_s=cut(IN_S), w_dt=w_dt, w_r=cut(IN_R), w_sb=cut(IN_SB), w_g=cut(IN_G),
        w_branch=[full['w_branch'][l, n] for n in range(N_BRANCH)],
        w_out=full['w_out'][l], w_ffn_in=full['w_ffn_in'][l], w_ffn_out=full['w_ffn_out'][l],
        conv_a=full['conv_a'][l], ssd_conv_w=full['ssd_conv_w'][l],
        ssd_conv_b=small['ssd_conv_b'][l], ssd_dt_bias=small['ssd_dt_bias'][l], ssd_a_log=small['ssd_a_log'][l],
        ssd_d=small['ssd_d'][l], ssd_norm=small['ssd_norm'][l],
        n_mix_pre=small['norm_mix_pre'][l][None], n_mix_post=small['norm_mix_post'][l][None],
        n_ffn_pre=small['norm_ffn_pre'][l][None], n_ffn_post=small['norm_ffn_post'][l][None],
    )


def _layer_fwd(h_res, lw, ret_tables):
    s = {'h_res': h_res, 'ret_tables': ret_tables}
    hn = rms_fwd(h_res, lw['n_mix_pre'], out_dtype=BF16, name="rms_mix_pre")
    s['hn'] = hn
    p_a = mm_nn(hn, lw['w_a'], name="proj_conv")
    p_s = mm_nn(hn, lw['w_s'], name="proj_ssd")
    p_dt = mm_nn(hn, lw['w_dt'], name="proj_dt")
    p_r = mm_nn(hn, lw['w_r'], out_dtype=BF16, name="proj_ret")
    p_sb = mm_nn(hn, lw['w_sb'], out_dtype=BF16, name="proj_sb")
    p_g = mm_nn(hn, lw['w_g'], out_dtype=BF16, name="proj_gate")
    y_a = conv_mixer_fwd(p_a, lw['conv_a'])
    s['p_a'] = p_a
    s['ssd_rows'] = _ssd_rows(lw)
    y_b, s['ssd_saved'] = ssd_fwd(p_s, p_dt, s['ssd_rows'])
    s['p_s'], s['p_dt'] = p_s, p_dt
    y_c, s['ret_ypre'], s['ret_states'] = ret_fwd(p_r, ret_tables)
    s['p_r'] = p_r
    y_d, s['sb_total'] = sb_fwd(p_sb)
    s['p_sb'] = p_sb
    ys = [y_a, y_b, y_c, y_d]
    s['ys'] = ys
    ups = [mm_nn(ys[n], lw['w_branch'][n], name="branch_up") for n in range(N_BRANCH)]
    merged = merge_fwd(p_g, ups)
    s['p_g'], s['ups'] = p_g, ups
    s['merged'] = merged
    mix = mm_nn(merged, lw['w_out'], name="mix_out")
    s['mix'] = mix
    h2 = rms_fwd(mix, lw['n_mix_post'], res=h_res, name="rms_mix_post")
    s['h2'] = h2
    hf = rms_fwd(h2, lw['n_ffn_pre'], out_dtype=BF16, name="rms_ffn_pre")
    s['hf'] = hf
    f = mm_nn(hf, lw['w_ffn_in'], out_dtype=BF16, name="ffn_in")
    act = swiglu_fwd(f)
    s['f'], s['act'] = f, act
    fo = mm_nn(act, lw['w_ffn_out'], name="ffn_out")
    s['fo'] = fo
    return rms_fwd(fo, lw['n_ffn_post'], res=h2, name="rms_ffn_post"), s


def _layer_bwd(dh3, lw, s):
    g = {}
    d_fo, g['norm_ffn_post'] = rms_bwd(s['fo'], lw['n_ffn_post'], dh3, dx_dtype=BF16, name="rms_ffn_post_bwd")
    d_act = mm_nt(d_fo, lw['w_ffn_out'], name="ffn_out_dx")
    g['w_ffn_out'] = mm_tn(s['act'], d_fo, name="ffn_out_dw")
    df = swiglu_bwd(s['f'], d_act)
    d_hf = mm_nt(df, lw['w_ffn_in'], name="ffn_in_dx")
    g['w_ffn_in'] = mm_tn(s['hf'], df, name="ffn_in_dw")
    dh2, g['norm_ffn_pre'] = rms_bwd(s['h2'], lw['n_ffn_pre'], d_hf, add=dh3, name="rms_ffn_pre_bwd")
    d_mix, g['norm_mix_post'] = rms_bwd(s['mix'], lw['n_mix_post'], dh2, dx_dtype=BF16, name="rms_mix_post_bwd")
    d_merged = mm_nt(d_mix, lw['w_out'], name="mix_out_dx")
    g['w_out'] = mm_tn(s['merged'], d_mix, name="mix_out_dw")
    dp_g, dups = merge_bwd(s['p_g'], s['ups'], d_merged)
    dys = [mm_nt(dups[n], lw['w_branch'][n], name="branch_dx") for n in range(N_BRANCH)]
    g['w_branch'] = jnp.stack([mm_tn(s['ys'][n], dups[n], name="branch_dw") for n in range(N_BRANCH)])
    dp_a, g['conv_a'] = conv_mixer_bwd(s['p_a'], lw['conv_a'], dys[0])
    dp_s, dp_dt, ssd_grads = ssd_bwd(s['p_s'], s['p_dt'], s['ssd_rows'], s['ssd_saved'], dys[1])
    g.update(ssd_grads)
    dp_r = ret_bwd(s['p_r'], s['ret_tables'], s['ret_ypre'], s['ret_states'], dys[2])
    dq, dk, dv = sb_bwd(s['p_sb'], s['sb_total'], dys[3])
    dp_sb = jnp.concatenate([dq, dk.astype(BF16), dv.astype(BF16)], axis=1)
    hn = s['hn']
    d_hn = None
    dws = []
    for dp, w, nm in ((dp_a, lw['w_a'], "conv"), (dp_s, lw['w_s'], "ssd"), (dp_dt, lw['w_dt'], "dt"),
                      (dp_r, lw['w_r'], "ret"), (dp_sb, lw['w_sb'], "sb"), (dp_g, lw['w_g'], "gate")):
        d_hn = mm_nt(dp, w, acc=d_hn, name="proj_dx")
        dws.append(mm_tn(hn, dp, name="proj_dw"))
    dws[2] = dws[2][:, :SSD_HEADS]
    g['w_in'] = jnp.concatenate(dws, axis=1)
    dh_res, g['norm_mix_pre'] = rms_bwd(s['h_res'], lw['n_mix_pre'], d_hn, add=dh2, name="rms_mix_pre_bwd")
    for k in ('norm_ffn_post', 'norm_ffn_pre', 'norm_mix_post', 'norm_mix_pre'):
        g[k] = g[k][0]
    return dh_res, g


def _quarter(a, axis, j):
    n = a.shape[axis] // N_CHIPS
    return lax.slice_in_dim(a, j * n, (j + 1) * n, axis=axis)


def kernel(x, meta, w_in, conv_a, ssd_conv_w, ssd_conv_b, ssd_dt_bias, ssd_a_log, ssd_d, ssd_norm, w_branch, w_out, w_ffn_in, w_ffn_out, norm_mix_pre, norm_mix_post, norm_ffn_pre, norm_ffn_post, loss_target, m_meta, m_w_in, m_conv_a, m_ssd_conv_w, m_ssd_conv_b, m_ssd_dt_bias, m_ssd_a_log, m_ssd_d, m_ssd_norm, m_w_branch, m_w_out, m_w_ffn_in, m_w_ffn_out, m_norm_mix_pre, m_norm_mix_post, m_norm_ffn_pre, m_norm_ffn_post, v_meta, v_w_in, v_conv_a, v_ssd_conv_w, v_ssd_conv_b, v_ssd_dt_bias, v_ssd_a_log, v_ssd_d, v_ssd_norm, v_w_branch, v_w_out, v_w_ffn_in, v_w_ffn_out, v_norm_mix_pre, v_norm_mix_post, v_norm_ffn_pre, v_norm_ffn_post):
    w_loc = dict(meta=meta, w_in=w_in, conv_a=conv_a, ssd_conv_w=ssd_conv_w, ssd_conv_b=ssd_conv_b,
                 ssd_dt_bias=ssd_dt_bias, ssd_a_log=ssd_a_log, ssd_d=ssd_d, ssd_norm=ssd_norm, w_branch=w_branch,
                 w_out=w_out, w_ffn_in=w_ffn_in, w_ffn_out=w_ffn_out, norm_mix_pre=norm_mix_pre,
                 norm_mix_post=norm_mix_post, norm_ffn_pre=norm_ffn_pre, norm_ffn_post=norm_ffn_post)
    m_loc = dict(meta=m_meta, w_in=m_w_in, conv_a=m_conv_a, ssd_conv_w=m_ssd_conv_w, ssd_conv_b=m_ssd_conv_b,
                 ssd_dt_bias=m_ssd_dt_bias, ssd_a_log=m_ssd_a_log, ssd_d=m_ssd_d, ssd_norm=m_ssd_norm,
                 w_branch=m_w_branch, w_out=m_w_out, w_ffn_in=m_w_ffn_in, w_ffn_out=m_w_ffn_out,
                 norm_mix_pre=m_norm_mix_pre, norm_mix_post=m_norm_mix_post, norm_ffn_pre=m_norm_ffn_pre,
                 norm_ffn_post=m_norm_ffn_post)
    v_loc = dict(meta=v_meta, w_in=v_w_in, conv_a=v_conv_a, ssd_conv_w=v_ssd_conv_w, ssd_conv_b=v_ssd_conv_b,
                 ssd_dt_bias=v_ssd_dt_bias, ssd_a_log=v_ssd_a_log, ssd_d=v_ssd_d, ssd_norm=v_ssd_norm,
                 w_branch=v_w_branch, w_out=v_w_out, w_ffn_in=v_w_ffn_in, w_ffn_out=v_w_ffn_out,
                 norm_mix_pre=v_norm_mix_pre, norm_mix_post=v_norm_mix_post, norm_ffn_pre=v_norm_ffn_pre,
                 norm_ffn_post=v_norm_ffn_post)

    halves = gather_shards([w_loc[n].astype(BF16).reshape(-1, w_loc[n].shape[-1]) for n in MATMUL_WEIGHTS]
                           + [_pack([w_loc[n] for n in SMALL_SHARDED], F32, 16)])
    gathered = _join_halves(halves, sibling_share(halves, "share_weight_halves"), axis=1)
    full = {}
    for t, n in enumerate(MATMUL_WEIGHTS):
        full[n] = jnp.concatenate([gathered[t][j].reshape(w_loc[n].shape) for j in range(N_CHIPS)],
                                  axis=SHARD_AXIS[n])
    parts_f = [_unpack(gathered[-1][j], [w_loc[n].shape for n in SMALL_SHARDED]) for j in range(N_CHIPS)]
    for t, n in enumerate(SMALL_SHARDED):
        full[n] = jnp.concatenate([parts_f[j][t] for j in range(N_CHIPS)], axis=SHARD_AXIS[n])

    xs = x[0]
    seq = xs.shape[0]
    length = CHUNK + seq
    h = jnp.concatenate([jnp.zeros((PAD, D_MODEL), F32), full['meta'], xs], axis=0)
    lws, saved = [], []
    ret_tables = _ret_tables(length)
    for l in range(DEPTH):
        lw = _layer_weights(full, w_loc, l)
        h, s = _layer_fwd(h, lw, ret_tables)
        lws.append(lw)
        saved.append(s)

    loss_row, dh = loss_head(h, loss_target[0])
    loss = lax.psum(loss_row[0, 0], ("x", "y", "c"))

    layer_grads = [None] * DEPTH
    for l in reversed(range(DEPTH)):
        dh, layer_grads[l] = _layer_bwd(dh, lws[l], saved[l])
    grad_x = dh[CHUNK:][None]
    grads = {n: jnp.stack([layer_grads[l][n] for l in range(DEPTH)]) for n in WEIGHTS if n != 'meta'}
    grads['meta'] = dh[PAD:CHUNK]

    def rows2d(a):
        return a.reshape(-1, a.shape[-1])

    def small_pieces(j):
        return [_quarter(grads[n], SHARD_AXIS[n], j) if n in SHARD_AXIS else grads[n] for n in SMALL_ORDER]

    quarters = [jnp.stack([rows2d(_quarter(grads[n], SHARD_AXIS[n], j)) for j in range(N_CHIPS)])
                for n in MATMUL_WEIGHTS]
    quarters.append(jnp.stack([_pack(small_pieces(j), F32, 16) for j in range(N_CHIPS)]))
    reduced = reduce_gradients(quarters)
    results = {}
    for t, n in enumerate(MATMUL_WEIGHTS):
        shape = w_loc[n].shape
        new = adamw(reduced[t], rows2d(w_loc[n]), rows2d(m_loc[n]), rows2d(v_loc[n]))
        results[n] = [a.reshape(shape) for a in (reduced[t], *new)]
    small_new = adamw(reduced[-1], *[_pack([d[n] for n in SMALL_ORDER], F32, 16) for d in (w_loc, m_loc, v_loc)])
    small_shapes = [w_loc[n].shape for n in SMALL_ORDER]
    for kind, buf in enumerate((reduced[-1], *small_new)):
        for n, piece in zip(SMALL_ORDER, _unpack(buf, small_shapes)):
            results.setdefault(n, [None] * 4)[kind] = piece
    outs = [results[n][kind] for kind in range(4) for n in WEIGHTS]
    return (loss, grad_x, *outs)
```
